```python
import math
import jax, jax.numpy as jnp
from jax import lax
import numpy as np

D_MODEL = 1024
BATCH = 4
SEQ = 4096
DEPTH = 1

CHUNK = 64
Q_BLOCK = 128
ROPE_THETA = 10000.0
NORM_EPS = 1e-6
SUBLN_EPS = 1e-5
DA_HEAD_DIM = 64
DA_HEADS = D_MODEL // 256
DA_WIDTH = DA_HEADS * 2 * DA_HEAD_DIM
RWKV_HEAD = 64
RWKV_HEADS = D_MODEL // 128
RWKV_WIDTH = RWKV_HEADS * RWKV_HEAD
DECAY_LORA = 64
AAA_LORA = 64
GATE_LORA = 128
GN_EPS = 64e-5
RWKV_COLS = 3 * RWKV_WIDTH + DECAY_LORA + AAA_LORA + GATE_LORA
IN_COLS = 3 * DA_WIDTH + RWKV_COLS
N_EXPERTS = 256
TOP_K = 8
N_GROUPS = 8
TOPK_GROUPS = 4
EXPERT_HIDDEN = D_MODEL // 4
ROUTED_SCALE = 2.5
MOE_BLOCK = 128

kernel_name = 'hybrid_diffattn_rwkv7_moe_block'


def rms_norm(x, g, eps):
    xf = x.astype(jnp.float32)
    y = xf * lax.rsqrt(jnp.mean(xf * xf, axis=-1, keepdims=True) + eps)
    return (y * g.astype(jnp.float32)).astype(x.dtype)


def rope(x, positions):
    d = x.shape[-1]
    inv_freq = 1.0 / (ROPE_THETA ** (jnp.arange(0, d, 2, dtype=jnp.float32) / d))
    ang = positions.astype(jnp.float32)[:, :, None] * inv_freq
    cos = jnp.cos(ang)[:, :, None, :]
    sin = jnp.sin(ang)[:, :, None, :]
    xf = x.astype(jnp.float32)
    x1, x2 = xf[..., : d // 2], xf[..., d // 2:]
    return jnp.concatenate([x1 * cos - x2 * sin, x2 * cos + x1 * sin], axis=-1).astype(x.dtype)


def token_shift(p, mu):
    prev = jnp.pad(p, ((0, 0), (1, 0), (0, 0)))[:, :-1]
    return p + (prev - p) * mu


def diff_attention(pq, pk, pv, positions, q_norm_g, k_norm_g, lambda_q1, lambda_k1,
                   lambda_q2, lambda_k2, subln_g, lambda_init):
    B, S, _ = pq.shape
    H, d = DA_HEADS, DA_HEAD_DIM
    q = rope(rms_norm(pq.reshape(B, S, 2 * H, d), q_norm_g, NORM_EPS), positions)
    k = rope(rms_norm(pk.reshape(B, S, 2 * H, d), k_norm_g, NORM_EPS), positions)
    q = q.reshape(B, S, H, 2, d).transpose(0, 2, 3, 1, 4)
    k = k.reshape(B, S, H, 2, d).transpose(0, 2, 3, 1, 4)
    v = pv.reshape(B, S, H, 2 * d).transpose(0, 2, 1, 3)
    f32 = jnp.float32
    lam = (jnp.exp(jnp.sum(lambda_q1.astype(f32) * lambda_k1.astype(f32)))
           - jnp.exp(jnp.sum(lambda_q2.astype(f32) * lambda_k2.astype(f32))) + lambda_init)
    scale = d ** -0.5
    outs = []
    for i in range(S // Q_BLOCK):
        q0 = i * Q_BLOCK
        kv_len = q0 + Q_BLOCK
        qb = q[:, :, :, q0:kv_len]
        kb = k[:, :, :, :kv_len]
        vb = v[:, :, :kv_len]
        s = jnp.einsum('bhmqd,bhmkd->bhmqk', qb, kb).astype(f32) * scale
        q_chunk = (q0 + jnp.arange(Q_BLOCK)) // CHUNK
        k_chunk = jnp.arange(kv_len) // CHUNK
        mask = k_chunk[None, :] <= q_chunk[:, None]
        p = jax.nn.softmax(jnp.where(mask, s, -jnp.inf), axis=-1)
        a = p[:, :, 0] - lam * p[:, :, 1]
        outs.append(jnp.einsum('bhqk,bhkv->bhqv', a.astype(vb.dtype), vb))
    o = jnp.concatenate(outs, axis=2)
    o = rms_norm(o, subln_g, SUBLN_EPS) * (1.0 - lambda_init)
    return o.transpose(0, 2, 1, 3).reshape(B, S, H * 2 * d)


def rwkv7_time_mix(p, mu, w_decay0, w_decay2, a0, a2, g2, k_k, k_a, r_k, ln_x_w, ln_x_b):
    B, S, _ = p.shape
    H, N = RWKV_HEADS, RWKV_HEAD
    f32 = jnp.float32
    xs = token_shift(p, mu)
    c1 = RWKV_WIDTH
    xr, xk, xv, xw, xa, xg = jnp.split(
        xs, [c1, 2 * c1, 3 * c1, 3 * c1 + DECAY_LORA, 3 * c1 + DECAY_LORA + AAA_LORA], axis=-1)
    w = -jax.nn.softplus(-(w_decay0 + jnp.tanh(xw) @ w_decay2)) - 0.5
    a = jax.nn.sigmoid(a0 + xa @ a2)
    g = jax.nn.sigmoid(xg) @ g2
    heads = lambda t: t.astype(f32).reshape(B, S, H, N)
    r, k, v, a, w = heads(xr), heads(xk), heads(xv), heads(a), heads(w)
    kk = k * k_k.astype(f32).reshape(H, N)
    kk = kk / jnp.maximum(jnp.linalg.norm(kk, axis=-1, keepdims=True), 1e-12)
    k = k * (1.0 + (a - 1.0) * k_a.astype(f32).reshape(H, N))
    decay = jnp.exp(-jnp.exp(w))

    def step(state, inp):
        r_t, d_t, k_t, v_t, kk_t, a_t = inp
        sa = jnp.einsum('bhvk,bhk->bhv', state, -kk_t)
        state = (state * d_t[:, :, None, :] + sa[..., None] * (kk_t * a_t)[:, :, None, :]
                 + v_t[..., None] * k_t[:, :, None, :])
        return state, jnp.einsum('bhvk,bhk->bhv', state, r_t)

    seq_first = lambda t: jnp.swapaxes(t, 0, 1)
    state0 = jnp.zeros((B, H, N, N), f32)
    _, y = lax.scan(step, state0, (seq_first(r), seq_first(decay), seq_first(k),
                                   seq_first(v), seq_first(kk), seq_first(a)))
    y = seq_first(y)
    mean = jnp.mean(y, axis=-1, keepdims=True)
    var = jnp.mean(jnp.square(y - mean), axis=-1, keepdims=True)
    y = ((y - mean) * lax.rsqrt(var + GN_EPS)).reshape(B, S, RWKV_WIDTH)
    y = y * ln_x_w.astype(f32) + ln_x_b.astype(f32)
    bonus = jnp.sum(r * k * r_k.astype(f32), axis=-1, keepdims=True) * v
    y = (y + bonus.reshape(B, S, RWKV_WIDTH)) * g.astype(f32)
    return y.astype(p.dtype)


def moe_ffn(h, w_router, router_bias, w_expert_up_gate, w_expert_down, w_shared_up_gate, w_shared_down):
    B, S, D = h.shape
    n_tok = B * S
    F = w_expert_down.shape[1]
    f32 = jnp.float32
    hf = h.reshape(n_tok, D)
    scores = jax.nn.sigmoid((hf @ w_router).astype(f32))
    biased = scores + router_bias.astype(f32)
    grp = biased.reshape(n_tok, N_GROUPS, N_EXPERTS // N_GROUPS)
    grp_score = jnp.sum(lax.top_k(grp, 2)[0], axis=-1)
    _, gidx = lax.top_k(grp_score, TOPK_GROUPS)
    gmask = jnp.sum(jax.nn.one_hot(gidx, N_GROUPS, dtype=f32), axis=-2) > 0
    emask = jnp.repeat(gmask, N_EXPERTS // N_GROUPS, axis=-1)
    _, eidx = lax.top_k(jnp.where(emask, biased, -jnp.inf), TOP_K)
    wts = jnp.take_along_axis(scores, eidx, axis=-1)
    wts = wts / jnp.sum(wts, axis=-1, keepdims=True) * ROUTED_SCALE

    nk = n_tok * TOP_K
    flat_e = eidx.reshape(nk)
    flat_tok = (jnp.arange(nk) // TOP_K).astype(jnp.int32)
    flat_w = wts.reshape(nk)
    order = jnp.argsort(flat_e)
    sorted_e = flat_e[order]
    counts = jnp.bincount(flat_e, length=N_EXPERTS)
    offsets = jnp.cumsum(counts) - counts
    pcounts = (counts + MOE_BLOCK - 1) // MOE_BLOCK * MOE_BLOCK
    pends = jnp.cumsum(pcounts)
    pstarts = pends - pcounts
    dest = pstarts[sorted_e] + (jnp.arange(nk) - offsets[sorted_e])
    n_rows = (nk + N_EXPERTS * (MOE_BLOCK - 1) + MOE_BLOCK - 1) // MOE_BLOCK * MOE_BLOCK
    n_blocks = n_rows // MOE_BLOCK
    row_tok = jnp.zeros((n_rows,), jnp.int32).at[dest].set(flat_tok[order])
    row_w = jnp.zeros((n_rows,), flat_w.dtype).at[dest].set(flat_w[order])
    block_start = jnp.arange(n_blocks) * MOE_BLOCK
    block_e = jnp.minimum(jnp.searchsorted(pends, block_start, side='right'), N_EXPERTS - 1)

    def body(acc, blk):
        tok, w, e = blk
        xb = hf[tok]
        gu = xb @ w_expert_up_gate[e]
        hid = jax.nn.silu(gu[:, :F]) * gu[:, F:]
        yb = hid @ w_expert_down[e]
        return acc.at[tok].add((yb * w[:, None]).astype(acc.dtype)), None

    routed, _ = lax.scan(body, jnp.zeros_like(hf),
                         (row_tok.reshape(n_blocks, MOE_BLOCK), row_w.reshape(n_blocks, MOE_BLOCK), block_e))
    gu = hf @ w_shared_up_gate
    shared = (jax.nn.silu(gu[:, :F]) * gu[:, F:]) @ w_shared_down
    return (shared + routed).reshape(B, S, D)


def hybrid_layer(x, c, positions, layer_idx, w_ada, b_ada, norm1_g, w_in, w_gate, b_gate,
                 q_norm_g, k_norm_g, lambda_q1, lambda_k1, lambda_q2, lambda_k2, subln_g,
                 rwkv_mu, w_decay0, w_decay2, a0, a2, g2, k_k, k_a, r_k, ln_x_w, ln_x_b,
                 w_branch_a, w_branch_b, w_out, norm2_g, w_router, router_bias,
                 w_expert_up_gate, w_expert_down, w_shared_up_gate, w_shared_down):
    mod = jax.nn.silu(c) @ w_ada + b_ada
    sh1, sc1, gt1, sh2, sc2, gt2 = [m[:, None, :] for m in jnp.split(mod, 6, axis=-1)]
    h = rms_norm(x, norm1_g, NORM_EPS) * (1.0 + sc1) + sh1
    proj = h @ w_in
    pq, pk, pv, prw = jnp.split(proj, [DA_WIDTH, 2 * DA_WIDTH, 3 * DA_WIDTH], axis=-1)
    lambda_init = 0.8 - 0.6 * math.exp(-0.3 * layer_idx)
    ya = diff_attention(pq, pk, pv, positions, q_norm_g, k_norm_g, lambda_q1, lambda_k1,
                        lambda_q2, lambda_k2, subln_g, lambda_init) @ w_branch_a
    yb = rwkv7_time_mix(prw, rwkv_mu, w_decay0, w_decay2, a0, a2, g2, k_k, k_a, r_k,
                        ln_x_w, ln_x_b) @ w_branch_b
    ga, gb = jnp.split(jax.nn.sigmoid(h @ w_gate + b_gate), 2, axis=-1)
    x = x + gt1 * ((ga * ya + gb * yb) @ w_out)
    h2 = rms_norm(x, norm2_g, NORM_EPS) * (1.0 + sc2) + sh2
    x = x + gt2 * moe_ffn(h2, w_router, router_bias, w_expert_up_gate, w_expert_down,
                          w_shared_up_gate, w_shared_down)
    return x


def setup_inputs(seed: int = 0) -> dict:
    key = jax.random.key(seed)
    ks = iter(jax.random.split(key, 48))
    f32 = jnp.float32
    L, D, F, E = DEPTH, D_MODEL, EXPERT_HIDDEN, N_EXPERTS
    nrm = lambda shape, s: jax.random.normal(next(ks), shape, f32) * s
    gain = lambda shape: 1.0 + jax.random.normal(next(ks), shape, f32) * 0.02
    x = jax.random.normal(next(ks), (BATCH, SEQ, D), f32)
    c = jax.random.normal(next(ks), (BATCH, D), f32)
    offset = jax.random.randint(next(ks), (BATCH, 1), 0, 1024) * CHUNK
    positions = (offset + jnp.arange(SEQ)[None, :]).astype(jnp.int32)
    return {
        'x': x, 'c': c, 'positions': positions,
        'w_ada': nrm((L, D, 6 * D), 0.5 * D ** -0.5),
        'b_ada': nrm((L, 6 * D), 0.02),
        'norm1_g': gain((L, D)),
        'w_in': nrm((L, D, IN_COLS), D ** -0.5),
        'w_gate': nrm((L, D, 2 * D), D ** -0.5),
        'b_gate': nrm((L, 2 * D), 0.02),
        'q_norm_g': gain((L, DA_HEAD_DIM)),
        'k_norm_g': gain((L, DA_HEAD_DIM)),
        'lambda_q1': nrm((L, DA_HEAD_DIM), 0.1),
        'lambda_k1': nrm((L, DA_HEAD_DIM), 0.1),
        'lambda_q2': nrm((L, DA_HEAD_DIM), 0.1),
        'lambda_k2': nrm((L, DA_HEAD_DIM), 0.1),
        'subln_g': gain((L, 2 * DA_HEAD_DIM)),
        'rwkv_mu': jax.random.uniform(next(ks), (L, RWKV_COLS), f32),
        'w_decay0': jax.random.uniform(next(ks), (L, RWKV_WIDTH), f32, -4.0, -0.5),
        'w_decay2': nrm((L, DECAY_LORA, RWKV_WIDTH), 0.5 * DECAY_LORA ** -0.5),
        'a0': nrm((L, RWKV_WIDTH), 0.1),
        'a2': nrm((L, AAA_LORA, RWKV_WIDTH), 0.5 * AAA_LORA ** -0.5),
        'g2': nrm((L, GATE_LORA, RWKV_WIDTH), GATE_LORA ** -0.5),
        'k_k': 0.85 + nrm((L, RWKV_WIDTH), 0.02),
        'k_a': gain((L, RWKV_WIDTH)),
        'r_k': nrm((L, RWKV_HEADS, RWKV_HEAD), 0.1),
        'ln_x_w': gain((L, RWKV_WIDTH)),
        'ln_x_b': nrm((L, RWKV_WIDTH), 0.02),
        'w_branch_a': nrm((L, DA_WIDTH, D), DA_WIDTH ** -0.5),
        'w_branch_b': nrm((L, RWKV_WIDTH, D), RWKV_WIDTH ** -0.5),
        'w_out': nrm((L, D, D), D ** -0.5),
        'norm2_g': gain((L, D)),
        'w_router': nrm((L, D, E), D ** -0.5),
        'router_bias': nrm((L, E), 0.01),
        'w_expert_up_gate': nrm((L, E, D, 2 * F), D ** -0.5),
        'w_expert_down': nrm((L, E, F, D), F ** -0.5),
        'w_shared_up_gate': nrm((L, D, 2 * F), D ** -0.5),
        'w_shared_down': nrm((L, F, D), F ** -0.5),
    }


def reference(x, c, positions, w_ada, b_ada, norm1_g, w_in, w_gate, b_gate,
              q_norm_g, k_norm_g, lambda_q1, lambda_k1, lambda_q2, lambda_k2, subln_g,
              rwkv_mu, w_decay0, w_decay2, a0, a2, g2, k_k, k_a, r_k, ln_x_w, ln_x_b,
              w_branch_a, w_branch_b, w_out, norm2_g, w_router, router_bias,
              w_expert_up_gate, w_expert_down, w_shared_up_gate, w_shared_down):
    for l in range(DEPTH):
        x = hybrid_layer(x, c, positions, l, w_ada[l], b_ada[l], norm1_g[l], w_in[l], w_gate[l], b_gate[l],
                         q_norm_g[l], k_norm_g[l], lambda_q1[l], lambda_k1[l], lambda_q2[l], lambda_k2[l],
                         subln_g[l], rwkv_mu[l], w_decay0[l], w_decay2[l], a0[l], a2[l], g2[l], k_k[l],
                         k_a[l], r_k[l], ln_x_w[l], ln_x_b[l], w_branch_a[l], w_branch_b[l], w_out[l],
                         norm2_g[l], w_router[l], router_bias[l], w_expert_up_gate[l], w_expert_down[l],
                         w_shared_up_gate[l], w_shared_down[l])
    return x
```

```python
import functools
import math

import jax
import jax.numpy as jnp
from jax import lax
from jax.experimental import pallas as pl
from jax.experimental.pallas import tpu as pltpu

F32 = jnp.float32
BF16 = jnp.bfloat16
I32 = jnp.int32
HI = lax.Precision.HIGHEST

CHUNK = 64
ROPE_THETA = 10000.0
NORM_EPS = 1e-6
SUBLN_EPS = 1e-5
DA_HEAD_DIM = 64
RWKV_HEAD = 64
GN_EPS = 64e-5
TOP_K = 8
N_GROUPS = 8
TOPK_GROUPS = 4
ROUTED_SCALE = 2.5
MOE_BLOCK = 128
RWKV_CHUNK = 64
LANES = 128
NEG = -1e30
VMEM_LIMIT = 56 * 1024 * 1024


def _cp(*sem):
    return pltpu.CompilerParams(dimension_semantics=sem, vmem_limit_bytes=VMEM_LIMIT)


def _bdot(a, b):
    return jnp.dot(a.astype(BF16), b.astype(BF16), preferred_element_type=F32)


def _fdot(a, b):
    return jnp.dot(a, b, precision=HI, preferred_element_type=F32)


def _nt(a, b, precision=None):
    return lax.dot_general(a, b, (((1,), (1,)), ((), ())), precision=precision,
                           preferred_element_type=F32)


def _tn(a, b, precision=None):
    return lax.dot_general(a, b, (((0,), (0,)), ((), ())), precision=precision,
                           preferred_element_type=F32)


def _sigmoid(x):
    return 1.0 / (1.0 + jnp.exp(-x))


def _silu(x):
    return x * _sigmoid(x)


def _ada_kernel(c_ref, w_ref, b_ref, o_ref):
    o_ref[...] = _fdot(_silu(c_ref[...]), w_ref[...]) + b_ref[...]


def _adaln(c, w_ada, b_ada):
    B, D = c.shape
    rows = -(-B // 8) * 8
    cpad = jnp.zeros((rows, D), F32).at[:B].set(c)
    n_out = w_ada.shape[1]
    out = pl.pallas_call(
        _ada_kernel,
        grid=(n_out // D,),
        in_specs=[pl.BlockSpec((rows, D), lambda j: (0, 0)),
                  pl.BlockSpec((D, D), lambda j: (0, j)),
                  pl.BlockSpec((1, D), lambda j: (0, j))],
        out_specs=pl.BlockSpec((rows, D), lambda j: (0, j)),
        out_shape=jax.ShapeDtypeStruct((rows, n_out), F32),
        compiler_params=_cp("arbitrary"),
        name="adaln",
    )(cpad, w_ada, b_ada.reshape(1, n_out))
    return out[:B].reshape(B, n_out // D, D)


def _inproj_kernel(x_ref, mod_ref, g_ref, w_ref, bg_ref, q_ref, k_ref, v_ref, rw_ref, gate_ref,
                   *, da_width, rw_cols):
    x = x_ref[...]
    y = x * lax.rsqrt(jnp.mean(x * x, axis=-1, keepdims=True) + NORM_EPS) * g_ref[...]
    h = (y * (1.0 + mod_ref[1:2, :]) + mod_ref[0:1, :]).astype(BF16)
    c = 0
    for ref in (q_ref, k_ref, v_ref):
        ref[...] = jnp.dot(h, w_ref[:, c:c + da_width], preferred_element_type=F32).astype(ref.dtype)
        c += da_width
    step = 512
    for o in range(0, rw_cols, step):
        wd = min(step, rw_cols - o)
        rw_ref[:, o:o + wd] = jnp.dot(h, w_ref[:, c + o:c + o + wd], preferred_element_type=F32)
    c += rw_cols
    n_gate = gate_ref.shape[1]
    for o in range(0, n_gate, step):
        z = jnp.dot(h, w_ref[:, c + o:c + o + step], preferred_element_type=F32) + bg_ref[:, o:o + step]
        gate_ref[:, o:o + step] = _sigmoid(z)


def _inproj(x2, mod, norm1_g, w_cat, b_gate, S, da_width, rw_cols, tm=256):
    N, D = x2.shape
    n_gate = b_gate.shape[0]
    tpb = S // tm
    kern = functools.partial(_inproj_kernel, da_width=da_width, rw_cols=rw_cols)
    row = lambda w: pl.BlockSpec((tm, w), lambda i: (i, 0))
    return pl.pallas_call(
        kern,
        grid=(N // tm,),
        in_specs=[row(D),
                  pl.BlockSpec((None, 6, D), lambda i: (i // tpb, 0, 0)),
                  pl.BlockSpec((1, D), lambda i: (0, 0)),
                  pl.BlockSpec(w_cat.shape, lambda i: (0, 0)),
                  pl.BlockSpec((1, n_gate), lambda i: (0, 0))],
        out_specs=[row(da_width), row(da_width), row(da_width), row(rw_cols), row(n_gate)],
        out_shape=[jax.ShapeDtypeStruct((N, da_width), F32),
                   jax.ShapeDtypeStruct((N, da_width), F32),
                   jax.ShapeDtypeStruct((N, da_width), BF16),
                   jax.ShapeDtypeStruct((N, rw_cols), F32),
                   jax.ShapeDtypeStruct((N, n_gate), F32)],
        compiler_params=_cp("parallel"),
        name="inproj",
    )(x2, mod, norm1_g.reshape(1, D), w_cat, b_gate.reshape(1, n_gate))


def _qkprep_kernel(q_ref, k_ref, pos_ref, invf_ref, qg_ref, kg_ref, qo_ref, ko_ref, *, scale):
    tm = q_ref.shape[0]
    lane = lax.broadcasted_iota(I32, (tm, LANES), 1)
    first = lane < DA_HEAD_DIM
    lo_half = (lane & (DA_HEAD_DIM - 1)) < DA_HEAD_DIM // 2
    ang = pos_ref[...].astype(F32) * invf_ref[...]
    cos = jnp.cos(ang)
    sin = jnp.sin(ang)
    sin = jnp.where(lo_half, -sin, sin)

    def one(src, dst, g_ref, mult):
        for blk in range(src.shape[1] // LANES):
            x = src[:, blk * LANES:(blk + 1) * LANES]
            xx = x * x
            s_all = jnp.sum(xx, axis=-1, keepdims=True)
            s_first = jnp.sum(jnp.where(first, xx, 0.0), axis=-1, keepdims=True)
            ms = jnp.where(first, s_first, s_all - s_first) * (1.0 / DA_HEAD_DIM)
            xn = x * lax.rsqrt(ms + NORM_EPS) * g_ref[...]
            rot = jnp.where(lo_half, pltpu.roll(xn, LANES - DA_HEAD_DIM // 2, axis=1),
                            pltpu.roll(xn, DA_HEAD_DIM // 2, axis=1))
            dst[:, blk * LANES:(blk + 1) * LANES] = ((xn * cos + rot * sin) * mult).astype(dst.dtype)

    one(q_ref, qo_ref, qg_ref, scale)
    one(k_ref, ko_ref, kg_ref, 1.0)


def _qkprep(q, k, pos2, q_norm_g, k_norm_g, tm=512):
    N, W = q.shape
    d = DA_HEAD_DIM
    inv_freq = 1.0 / (ROPE_THETA ** (jnp.arange(0, d, 2, dtype=F32) / d))
    invf = jnp.tile(inv_freq, LANES // (d // 2)).reshape(1, LANES)
    row = pl.BlockSpec((tm, W), lambda i: (i, 0))
    vec = pl.BlockSpec((1, LANES), lambda i: (0, 0))
    return pl.pallas_call(
        functools.partial(_qkprep_kernel, scale=d ** -0.5),
        grid=(N // tm,),
        in_specs=[row, row, pl.BlockSpec((tm, 1), lambda i: (i, 0)), vec, vec, vec],
        out_specs=[row, row],
        out_shape=[jax.ShapeDtypeStruct((N, W), BF16)] * 2,
        compiler_params=_cp("parallel"),
        name="qkprep",
    )(q, k, pos2, invf, jnp.tile(q_norm_g, 2).reshape(1, LANES), jnp.tile(k_norm_g, 2).reshape(1, LANES))


def _attn_kernel(qi_ref, kj_ref, q_ref, k_ref, v_ref, lam_ref, sg_ref, o_ref,
                 qz_ref, m_ref, l_ref, acc_ref, *, tq, lambda_init):
    p = pl.program_id(2)
    i = qi_ref[p]
    j = kj_ref[p]

    @pl.when(j == 0)
    def _():
        lane = lax.broadcasted_iota(I32, (tq, LANES), 1)
        q = q_ref[...]
        zero = jnp.zeros_like(q)
        qz_ref[0:tq, :] = jnp.where(lane < DA_HEAD_DIM, q, zero)
        qz_ref[tq:, :] = jnp.where(lane >= DA_HEAD_DIM, q, zero)
        m_ref[...] = jnp.full_like(m_ref, NEG)
        l_ref[...] = jnp.zeros_like(l_ref)
        acc_ref[...] = jnp.zeros_like(acc_ref)

    def step(masked):
        s = _nt(qz_ref[...], k_ref[...])
        if masked:
            row = lax.broadcasted_iota(I32, s.shape, 0)
            col = lax.broadcasted_iota(I32, s.shape, 1)
            s = jnp.where((col // CHUNK) <= ((row & (tq - 1)) // CHUNK), s, NEG)
        m_old = m_ref[...]
        m_new = jnp.maximum(m_old, jnp.max(s, axis=-1, keepdims=True))
        alpha = jnp.exp(m_old - m_new)
        pr = jnp.exp(s - m_new)
        l_ref[...] = alpha * l_ref[...] + jnp.sum(pr, axis=-1, keepdims=True)
        acc_ref[...] = alpha * acc_ref[...] + jnp.dot(pr.astype(BF16), v_ref[...],
                                                      preferred_element_type=F32)
        m_ref[...] = m_new

    @pl.when(j < i)
    def _():
        step(False)

    @pl.when(j == i)
    def _():
        step(True)
        lv = lam_ref[...]
        lam = (jnp.exp(jnp.sum(lv[0:1] * lv[1:2], keepdims=True))
               - jnp.exp(jnp.sum(lv[2:3] * lv[3:4], keepdims=True)) + lambda_init)
        o1 = acc_ref[0:tq, :] / l_ref[0:tq, :]
        o2 = acc_ref[tq:, :] / l_ref[tq:, :]
        o = o1 - lam * o2
        o = o * lax.rsqrt(jnp.mean(o * o, axis=-1, keepdims=True) + SUBLN_EPS) * sg_ref[...]
        o_ref[...] = (o * (1.0 - lambda_init)).astype(o_ref.dtype)


def _diff_attention(qn, kn, v, lam_vecs, subln_g, B, S, lambda_init, tq=256):
    W = qn.shape[1]
    H = W // LANES
    nq = S // tq
    pairs = [(i, j) for i in range(nq) for j in range(i + 1)]
    qi = jnp.array([p[0] for p in pairs], I32)
    kj = jnp.array([p[1] for p in pairs], I32)
    q3 = qn.reshape(B, S, W)
    k3 = kn.reshape(B, S, W)
    v3 = v.reshape(B, S, W)
    grid_spec = pltpu.PrefetchScalarGridSpec(
        num_scalar_prefetch=2,
        grid=(B, H, len(pairs)),
        in_specs=[pl.BlockSpec((None, tq, LANES), lambda b, h, p, qi, kj: (b, qi[p], h)),
                  pl.BlockSpec((None, tq, LANES), lambda b, h, p, qi, kj: (b, kj[p], h)),
                  pl.BlockSpec((None, tq, LANES), lambda b, h, p, qi, kj: (b, kj[p], h)),
                  pl.BlockSpec((4, DA_HEAD_DIM), lambda b, h, p, qi, kj: (0, 0)),
                  pl.BlockSpec((1, LANES), lambda b, h, p, qi, kj: (0, 0))],
        out_specs=pl.BlockSpec((None, tq, LANES), lambda b, h, p, qi, kj: (b, qi[p], h)),
        scratch_shapes=[pltpu.VMEM((2 * tq, LANES), BF16),
                        pltpu.VMEM((2 * tq, 1), F32),
                        pltpu.VMEM((2 * tq, 1), F32),
                        pltpu.VMEM((2 * tq, LANES), F32)],
    )
    out = pl.pallas_call(
        functools.partial(_attn_kernel, tq=tq, lambda_init=lambda_init),
        grid_spec=grid_spec,
        out_shape=jax.ShapeDtypeStruct((B, S, W), BF16),
        compiler_params=_cp("parallel", "parallel", "arbitrary"),
        name="diff_attn",
    )(qi, kj, q3, k3, v3, lam_vecs, subln_g.reshape(1, LANES))
    return out.reshape(B * S, W)


def _rwkv_prep_kernel(p_ref, prev_ref, mu_ref, w0_ref, w2_ref, a0_ref, a2_ref, g2_ref,
                      r_ref, k_ref, v_ref, a_ref, ld_ref, g_ref, *, width):
    i = pl.program_id(1)
    p = p_ref[...]
    last_prev = jnp.where(i > 0, prev_ref[7:8, :], 0.0)
    rowi = lax.broadcasted_iota(I32, p.shape, 0)
    prev = jnp.where(rowi == 0, last_prev, pltpu.roll(p, 1, axis=0))
    xs = p + (prev - p) * mu_ref[...]
    r_ref[...] = xs[:, 0:width]
    k_ref[...] = xs[:, width:2 * width]
    v_ref[...] = xs[:, 2 * width:3 * width]
    xwa = xs[:, 3 * width:3 * width + LANES]
    xg = xs[:, 3 * width + LANES:]
    z = w0_ref[...] + _fdot(jnp.tanh(xwa), w2_ref[...])
    w = -(jnp.maximum(-z, 0.0) + jnp.log(1.0 + jnp.exp(-jnp.abs(z)))) - 0.5
    ld_ref[...] = -jnp.exp(w)
    a_ref[...] = _sigmoid(a0_ref[...] + _fdot(xwa, a2_ref[...]))
    g_ref[...] = _fdot(_sigmoid(xg), g2_ref[...])


def _rwkv_prep(prw3, mu, w_decay0, w_decay2, a0, a2, g2, width, tm=256):
    B, S, C = prw3.shape
    dl, al = w_decay2.shape[0], a2.shape[0]
    assert dl + al == LANES and g2.shape[0] == LANES
    w2p = jnp.zeros((LANES, width), F32).at[:dl].set(w_decay2)
    a2p = jnp.zeros((LANES, width), F32).at[dl:].set(a2)
    vec = lambda n: pl.BlockSpec((1, n), lambda b, i: (0, 0))
    mat = pl.BlockSpec((LANES, width), lambda b, i: (0, 0))
    out = pl.BlockSpec((None, tm, width), lambda b, i: (b, i, 0))
    return pl.pallas_call(
        functools.partial(_rwkv_prep_kernel, width=width),
        grid=(B, S // tm),
        in_specs=[pl.BlockSpec((None, tm, C), lambda b, i: (b, i, 0)),
                  pl.BlockSpec((None, 8, C), lambda b, i: (b, jnp.maximum(i * (tm // 8) - 1, 0), 0)),
                  vec(C), vec(width), mat, vec(width), mat, mat],
        out_specs=[out] * 6,
        out_shape=[jax.ShapeDtypeStruct((B, S, width), F32)] * 6,
        compiler_params=_cp("parallel", "parallel"),
        name="rwkv_prep",
    )(prw3, prw3, mu.reshape(1, C), w_decay0.reshape(1, width), w2p, a0.reshape(1, width), a2p, g2)


def _rwkv_scan_kernel(r_ref, k_ref, v_ref, a_ref, ld_ref, g_ref, kk_ref, ka_ref, rk_ref, lnw_ref, lnb_ref,
                      o_ref, s_ref, *, n_heads, dot):
    L = r_ref.shape[0]
    hd = RWKV_HEAD

    @pl.when(pl.program_id(1) == 0)
    def _():
        s_ref[...] = jnp.zeros_like(s_ref)

    row = lax.broadcasted_iota(I32, (L, L), 0)
    col = lax.broadcasted_iota(I32, (L, L), 1)
    strict = col < row
    incl = col <= row
    eye = (col == row).astype(F32)

    ld = ld_ref[...]
    c = _fdot(incl.astype(F32), ld)
    ec = jnp.exp(c)
    eci = jnp.exp(-c)
    ecm = jnp.exp(c - ld)
    gam = ec[L - 1:L, :]
    r = r_ref[...]
    k = k_ref[...]
    v = v_ref[...]
    a = a_ref[...]
    kkr = k * kk_ref[...]
    kmod = k * (1.0 + (a - 1.0) * ka_ref[...])
    bonus_rk = r * kmod * rk_ref[...]
    outs = []
    for h in range(n_heads):
        sl = slice(h * hd, (h + 1) * hd)
        kkh = kkr[:, sl]
        kkh = kkh / jnp.maximum(jnp.sqrt(jnp.sum(kkh * kkh, axis=-1, keepdims=True)), 1e-12)
        vh = v[:, sl]
        g_l = gam[:, sl]
        at = -kkh * ecm[:, sl]
        bt = kkh * a[:, sl] * eci[:, sl]
        kt = kmod[:, sl] * eci[:, sl]
        rt = r[:, sl] * ec[:, sl]
        gmat = _nt(jnp.concatenate([at, rt], axis=0), jnp.concatenate([bt, kt], axis=0), precision=HI)
        a_ab = jnp.where(strict, gmat[:L, :L], 0.0)
        a_ak = jnp.where(strict, gmat[:L, L:], 0.0)
        a_rb = jnp.where(incl, gmat[L:, :L], 0.0)
        a_rk = jnp.where(incl, gmat[L:, L:], 0.0)
        t_inv = eye + a_ab
        pw = a_ab
        for _ in range(int(math.log2(L)) - 1):
            pw = dot(pw, pw)
            t_inv = t_inv + dot(pw, t_inv)
        z = dot(t_inv, jnp.concatenate([at, dot(a_ak, vh)], axis=1))
        qy = dot(a_rb, z)
        q_hat = rt + qy[:, :hd]
        y0 = qy[:, hd:] + dot(a_rk, vh)
        s_old = s_ref[h]
        y = _nt(q_hat, s_old, precision=HI) + y0
        w = _nt(z[:, :hd], s_old, precision=HI) + z[:, hd:]
        s_ref[h] = s_old * g_l + _tn(w, bt * g_l, precision=HI) + _tn(vh, kt * g_l, precision=HI)
        mean = jnp.mean(y, axis=-1, keepdims=True)
        yc = y - mean
        var = jnp.mean(yc * yc, axis=-1, keepdims=True)
        yn = yc * lax.rsqrt(var + GN_EPS) * lnw_ref[:, sl] + lnb_ref[:, sl]
        bonus = jnp.sum(bonus_rk[:, sl], axis=-1, keepdims=True) * vh
        outs.append(yn + bonus)
    o_ref[...] = (jnp.concatenate(outs, axis=1) * g_ref[...]).astype(o_ref.dtype)


def _rwkv_scan(r, k, v, a, ld, g, k_k, k_a, r_k, ln_w, ln_b, L=RWKV_CHUNK):
    B, S, W = r.shape
    H = W // RWKV_HEAD
    seq = pl.BlockSpec((None, L, W), lambda b, c: (b, c, 0))
    vec = pl.BlockSpec((1, W), lambda b, c: (0, 0))
    return pl.pallas_call(
        functools.partial(_rwkv_scan_kernel, n_heads=H, dot=_fdot),
        grid=(B, S // L),
        in_specs=[seq] * 6 + [vec] * 5,
        out_specs=seq,
        out_shape=jax.ShapeDtypeStruct((B, S, W), BF16),
        scratch_shapes=[pltpu.VMEM((H, RWKV_HEAD, RWKV_HEAD), F32)],
        compiler_params=_cp("parallel", "arbitrary"),
        name="rwkv_scan",
    )(r, k, v, a, ld, g, k_k.reshape(1, W), k_a.reshape(1, W), r_k.reshape(1, W),
      ln_w.reshape(1, W), ln_b.reshape(1, W))


def _post_kernel(attn_ref, rw_ref, gate_ref, x_ref, mod_ref, wa_ref, wb_ref, wo_ref, g2_ref, wr_ref,
                 x1_ref, h2_ref, lg_ref):
    D = x_ref.shape[1]
    ya = jnp.dot(attn_ref[...], wa_ref[...], preferred_element_type=F32)
    yb = jnp.dot(rw_ref[...], wb_ref[...], preferred_element_type=F32)
    m = gate_ref[:, 0:D] * ya + gate_ref[:, D:] * yb
    x1 = x_ref[...] + mod_ref[2:3, :] * jnp.dot(m.astype(BF16), wo_ref[...], preferred_element_type=F32)
    x1_ref[...] = x1
    y = x1 * lax.rsqrt(jnp.mean(x1 * x1, axis=-1, keepdims=True) + NORM_EPS) * g2_ref[...]
    h2 = y * (1.0 + mod_ref[4:5, :]) + mod_ref[3:4, :]
    h2_ref[...] = h2
    lg_ref[...] = _nt(wr_ref[...], h2, precision=HI)


def _post(attn, rw, gate, x2, mod, wa, wb, wo, norm2_g, w_router_t, S, tm=256):
    N, D = x2.shape
    E = w_router_t.shape[0]
    tpb = S // tm
    row = lambda w: pl.BlockSpec((tm, w), lambda i: (i, 0))
    full = lambda a: pl.BlockSpec(a.shape, lambda i: (0, 0))
    return pl.pallas_call(
        _post_kernel,
        grid=(N // tm,),
        in_specs=[row(attn.shape[1]), row(rw.shape[1]), row(gate.shape[1]), row(D),
                  pl.BlockSpec((None, 6, D), lambda i: (i // tpb, 0, 0)),
                  full(wa), full(wb), full(wo), pl.BlockSpec((1, D), lambda i: (0, 0)), full(w_router_t)],
        out_specs=[row(D), row(D), pl.BlockSpec((E, tm), lambda i: (0, i))],
        out_shape=[jax.ShapeDtypeStruct((N, D), F32), jax.ShapeDtypeStruct((N, D), F32),
                   jax.ShapeDtypeStruct((E, N), F32)],
        compiler_params=_cp("parallel"),
        name="post_mixer",
    )(attn, rw, gate, x2, mod, wa, wb, wo, norm2_g.reshape(1, D), w_router_t)


def _first_argmax(x, idx, sentinel):
    m = jnp.max(x, axis=0, keepdims=True)
    return m, jnp.min(jnp.where(x == m, idx, sentinel), axis=0, keepdims=True)


def _route_kernel(lg_ref, bias_ref, e_ref, w_ref):
    E, T = lg_ref.shape
    gsz = E // N_GROUPS
    scores = _sigmoid(lg_ref[...])
    biased = scores + bias_ref[...]
    ig = lax.broadcasted_iota(I32, (gsz, T), 0)
    gs = []
    for g in range(N_GROUPS):
        blk = biased[g * gsz:(g + 1) * gsz, :]
        m1, i1 = _first_argmax(blk, ig, gsz)
        m2 = jnp.max(jnp.where(ig == i1, -jnp.inf, blk), axis=0, keepdims=True)
        gs.append(m1 + m2)
    gsc = jnp.concatenate(gs, axis=0)
    i8 = lax.broadcasted_iota(I32, (N_GROUPS, T), 0)
    chosen = jnp.zeros((N_GROUPS, T), F32)
    for _ in range(TOPK_GROUPS):
        _, gi = _first_argmax(gsc, i8, N_GROUPS)
        hit = i8 == gi
        chosen = jnp.where(hit, 1.0, chosen)
        gsc = jnp.where(hit, -jnp.inf, gsc)
    masked = jnp.concatenate(
        [jnp.where(chosen[g:g + 1, :] > 0.0, biased[g * gsz:(g + 1) * gsz, :], -jnp.inf)
         for g in range(N_GROUPS)], axis=0)
    ie = lax.broadcasted_iota(I32, (E, T), 0)
    idxs, wts = [], []
    for _ in range(TOP_K):
        _, ei = _first_argmax(masked, ie, E)
        hit = ie == ei
        idxs.append(ei)
        wts.append(jnp.sum(jnp.where(hit, scores, 0.0), axis=0, keepdims=True))
        masked = jnp.where(hit, -jnp.inf, masked)
    wt = jnp.concatenate(wts, axis=0)
    e_ref[...] = jnp.concatenate(idxs, axis=0)
    w_ref[...] = wt / jnp.sum(wt, axis=0, keepdims=True) * ROUTED_SCALE


def _route(logits_t, router_bias, T=512):
    E, N = logits_t.shape
    blk = pl.BlockSpec((TOP_K, T), lambda i: (0, i))
    return pl.pallas_call(
        _route_kernel,
        grid=(N // T,),
        in_specs=[pl.BlockSpec((E, T), lambda i: (0, i)), pl.BlockSpec((E, 1), lambda i: (0, 0))],
        out_specs=[blk, blk],
        out_shape=[jax.ShapeDtypeStruct((TOP_K, N), I32), jax.ShapeDtypeStruct((TOP_K, N), F32)],
        compiler_params=_cp("parallel"),
        name="route",
    )(logits_t, router_bias.reshape(E, 1))


def _rank_kernel(e_ref, rank_ref, cnt_ref, carry_ref, *, n_experts):
    T = e_ref.shape[1]

    @pl.when(pl.program_id(0) == 0)
    def _():
        carry_ref[...] = jnp.zeros_like(carry_ref)

    ie = lax.broadcasted_iota(I32, (n_experts, T), 0)
    e = e_ref[...]
    hits = [ie == e[kk:kk + 1, :] for kk in range(TOP_K)]
    onehot = jnp.zeros((n_experts, T), F32)
    for hsel in hits:
        onehot = onehot + hsel.astype(F32)
    tr = lax.broadcasted_iota(I32, (T, T), 0)
    tc = lax.broadcasted_iota(I32, (T, T), 1)
    before = (tr < tc).astype(BF16)
    base = _bdot(onehot, before) + carry_ref[:, 0:1]
    rank_ref[...] = jnp.concatenate(
        [jnp.sum(jnp.where(hsel, base, 0.0), axis=0, keepdims=True) for hsel in hits], axis=0).astype(I32)
    carry_ref[...] = carry_ref[...] + jnp.sum(onehot, axis=1, keepdims=True)
    cnt_ref[...] = carry_ref[...]


def _ranks(eidx_t, n_experts, T=512):
    N = eidx_t.shape[1]
    blk = pl.BlockSpec((TOP_K, T), lambda i: (0, i))
    cnt = pl.BlockSpec((n_experts, LANES), lambda i: (0, 0))
    return pl.pallas_call(
        functools.partial(_rank_kernel, n_experts=n_experts),
        grid=(N // T,),
        in_specs=[blk],
        out_specs=[blk, cnt],
        out_shape=[jax.ShapeDtypeStruct((TOP_K, N), I32), jax.ShapeDtypeStruct((n_experts, LANES), F32)],
        scratch_shapes=[pltpu.VMEM((n_experts, LANES), F32)],
        compiler_params=_cp("arbitrary"),
        name="ranks",
    )(eidx_t)


def _dest_kernel(e_ref, rank_ref, start_ref, d_ref):
    E = start_ref.shape[0]
    T = e_ref.shape[1]
    ie = lax.broadcasted_iota(I32, (E, T), 0)
    e = e_ref[...]
    start = start_ref[:, 0:1]
    rows = [jnp.sum(jnp.where(ie == e[kk:kk + 1, :], start, 0.0), axis=0, keepdims=True) for kk in range(TOP_K)]
    d_ref[...] = jnp.concatenate(rows, axis=0).astype(I32) + rank_ref[...]


def _dests(eidx_t, rank_t, pstart, T=512):
    N = eidx_t.shape[1]
    E = pstart.shape[0]
    blk = pl.BlockSpec((TOP_K, T), lambda i: (0, i))
    return pl.pallas_call(
        _dest_kernel,
        grid=(N // T,),
        in_specs=[blk, blk, pl.BlockSpec((E, LANES), lambda i: (0, 0))],
        out_specs=blk,
        out_shape=jax.ShapeDtypeStruct((TOP_K, N), I32),
        compiler_params=_cp("parallel"),
        name="dests",
    )(eidx_t, rank_t, jnp.broadcast_to(pstart.astype(F32)[:, None], (E, LANES)))


def _dispatch_kernel(d_ref, h_ref, xg_in_ref, xg_ref, sem):
    del xg_in_ref
    T = h_ref.shape[0]

    def body(t, carry):
        for kk in range(TOP_K):
            pltpu.make_async_copy(h_ref.at[pl.ds(t, 1)], xg_ref.at[pl.ds(d_ref[kk, t], 1)], sem).start()
        return carry

    lax.fori_loop(0, T, body, 0)
    for kk in range(TOP_K):
        pltpu.make_async_copy(h_ref, xg_ref.at[pl.ds(0, T)], sem).wait()


def _dispatch(dest_t, h2, n_rows, T=128):
    N, D = h2.shape
    zeros = jnp.zeros((n_rows, D), h2.dtype)
    return pl.pallas_call(
        _dispatch_kernel,
        grid=(N // T,),
        in_specs=[pl.BlockSpec((TOP_K, T), lambda i: (0, i), memory_space=pltpu.SMEM),
                  pl.BlockSpec((T, D), lambda i: (i, 0)),
                  pl.BlockSpec(memory_space=pl.ANY)],
        out_specs=pl.BlockSpec(memory_space=pl.ANY),
        out_shape=jax.ShapeDtypeStruct((n_rows, D), h2.dtype),
        scratch_shapes=[pltpu.SemaphoreType.DMA],
        input_output_aliases={2: 0},
        compiler_params=_cp("arbitrary"),
        name="dispatch",
    )(dest_t, h2, zeros)


def _expert_kernel(be_ref, x_ref, wug_ref, wd_ref, y_ref, wug_bf, wd_bf):
    i = pl.program_id(0)
    F = wd_ref.shape[0]
    changed = jnp.logical_or(i == 0, be_ref[i] != be_ref[jnp.maximum(i - 1, 0)])

    @pl.when(changed)
    def _():
        wug_bf[...] = wug_ref[...].astype(BF16)
        wd_bf[...] = wd_ref[...].astype(BF16)

    gu = jnp.dot(x_ref[...].astype(BF16), wug_bf[...], preferred_element_type=F32)
    hid = _silu(gu[:, :F]) * gu[:, F:]
    y_ref[...] = jnp.dot(hid.astype(BF16), wd_bf[...], preferred_element_type=F32)


def _experts(block_e, xg, w_ug, w_d):
    n_rows, D = xg.shape
    E, _, F2 = w_ug.shape
    F = w_d.shape[1]
    n_blocks = n_rows // MOE_BLOCK
    grid_spec = pltpu.PrefetchScalarGridSpec(
        num_scalar_prefetch=1,
        grid=(n_blocks,),
        in_specs=[pl.BlockSpec((MOE_BLOCK, D), lambda i, be: (i, 0)),
                  pl.BlockSpec((None, D, F2), lambda i, be: (be[i], 0, 0)),
                  pl.BlockSpec((None, F, D), lambda i, be: (be[i], 0, 0))],
        out_specs=pl.BlockSpec((MOE_BLOCK, D), lambda i, be: (i, 0)),
        scratch_shapes=[pltpu.VMEM((D, F2), BF16), pltpu.VMEM((F, D), BF16)],
    )
    return pl.pallas_call(
        _expert_kernel,
        grid_spec=grid_spec,
        out_shape=jax.ShapeDtypeStruct((n_rows, D), F32),
        compiler_params=_cp("arbitrary"),
        name="experts",
    )(block_e, xg, w_ug, w_d)


def _combine_kernel(d_ref, y_ref, w_ref, h_ref, x1_ref, mod_ref, sug_ref, sd_ref, o_ref, buf, sem):
    T = h_ref.shape[0]
    F = sd_ref.shape[0]

    def body(t, carry):
        for kk in range(TOP_K):
            pltpu.make_async_copy(y_ref.at[pl.ds(d_ref[kk, t], 1)], buf.at[kk, pl.ds(t, 1)], sem).start()
        return carry

    lax.fori_loop(0, T, body, 0)
    gu = _bdot(h_ref[...], sug_ref[...])
    shared = _bdot(_silu(gu[:, :F]) * gu[:, F:], sd_ref[...])
    tr = lax.broadcasted_iota(I32, (T, T), 0)
    tc = lax.broadcasted_iota(I32, (T, T), 1)
    wcol = _nt((tr == tc).astype(F32), w_ref[...], precision=HI)
    for kk in range(TOP_K):
        pltpu.make_async_copy(y_ref.at[pl.ds(0, T)], buf.at[kk], sem).wait()
    routed = buf[0] * wcol[:, 0:1]
    for kk in range(1, TOP_K):
        routed = routed + buf[kk] * wcol[:, kk:kk + 1]
    o_ref[...] = x1_ref[...] + mod_ref[5:6, :] * (shared + routed)


def _combine(dest_t, y, w_t, h2, x1, mod, sug, sd, S, T=128):
    N, D = h2.shape
    tpb = S // T
    row = pl.BlockSpec((T, D), lambda i: (i, 0))
    full = lambda a: pl.BlockSpec(a.shape, lambda i: (0, 0))
    return pl.pallas_call(
        _combine_kernel,
        grid=(N // T,),
        in_specs=[pl.BlockSpec((TOP_K, T), lambda i: (0, i), memory_space=pltpu.SMEM),
                  pl.BlockSpec(memory_space=pl.ANY),
                  pl.BlockSpec((TOP_K, T), lambda i: (0, i)),
                  row, row,
                  pl.BlockSpec((None, 6, D), lambda i: (i // tpb, 0, 0)),
                  full(sug), full(sd)],
        out_specs=row,
        out_shape=jax.ShapeDtypeStruct((N, D), F32),
        scratch_shapes=[pltpu.VMEM((TOP_K, T, D), F32), pltpu.SemaphoreType.DMA],
        compiler_params=_cp("arbitrary"),
        name="combine",
    )(dest_t, y, w_t, h2, x1, mod, sug, sd)


def _layer(x, c, positions, layer_idx, w_ada, b_ada, norm1_g, w_in, w_gate, b_gate,
           q_norm_g, k_norm_g, lambda_q1, lambda_k1, lambda_q2, lambda_k2, subln_g,
           rwkv_mu, w_decay0, w_decay2, a0, a2, g2, k_k, k_a, r_k, ln_x_w, ln_x_b,
           w_branch_a, w_branch_b, w_out, norm2_g, w_router, router_bias,
           w_expert_up_gate, w_expert_down, w_shared_up_gate, w_shared_down):
    B, S, D = x.shape
    N = B * S
    E = w_router.shape[1]
    da_width = w_branch_a.shape[0]
    rw_width = w_branch_b.shape[0]
    rw_cols = rwkv_mu.shape[0]
    lambda_init = 0.8 - 0.6 * math.exp(-0.3 * layer_idx)

    mod = _adaln(c, w_ada, b_ada)
    x2 = x.reshape(N, D)
    w_cat = jnp.concatenate([w_in, w_gate], axis=1).astype(BF16)
    q, k, v, prw, gate = _inproj(x2, mod, norm1_g, w_cat, b_gate, S, da_width, rw_cols)

    qn, kn = _qkprep(q, k, positions.reshape(N, 1), q_norm_g, k_norm_g)
    lam_vecs = jnp.stack([lambda_q1, lambda_k1, lambda_q2, lambda_k2])
    attn = _diff_attention(qn, kn, v, lam_vecs, subln_g, B, S, lambda_init)

    r_, k_, v_, a_, ld_, g_ = _rwkv_prep(prw.reshape(B, S, rw_cols), rwkv_mu, w_decay0, w_decay2,
                                         a0, a2, g2, rw_width)
    rw = _rwkv_scan(r_, k_, v_, a_, ld_, g_, k_k, k_a, r_k.reshape(-1), ln_x_w, ln_x_b).reshape(N, rw_width)

    x1, h2, logits_t = _post(attn, rw, gate, x2, mod, w_branch_a.astype(BF16), w_branch_b.astype(BF16),
                             w_out.astype(BF16), norm2_g, w_router.T, S)

    eidx_t, w_t = _route(logits_t, router_bias)
    rank_t, counts = _ranks(eidx_t, E)
    cnt = counts[:, 0].astype(I32)
    pcnt = (cnt + MOE_BLOCK - 1) // MOE_BLOCK * MOE_BLOCK
    pends = jnp.cumsum(pcnt)
    dest_t = _dests(eidx_t, rank_t, pends - pcnt)
    n_blocks = (N * TOP_K + E * (MOE_BLOCK - 1) + MOE_BLOCK - 1) // MOE_BLOCK
    block_e = jnp.minimum(jnp.searchsorted(pends, jnp.arange(n_blocks, dtype=I32) * MOE_BLOCK, side='right'),
                          E - 1).astype(I32)
    xg = _dispatch(dest_t, h2, n_blocks * MOE_BLOCK)
    y = _experts(block_e, xg, w_expert_up_gate, w_expert_down)
    out = _combine(dest_t, y, w_t, h2, x1, mod, w_shared_up_gate.astype(BF16), w_shared_down.astype(BF16), S)
    return out.reshape(B, S, D)


def kernel(x, c, positions, w_ada, b_ada, norm1_g, w_in, w_gate, b_gate, q_norm_g, k_norm_g, lambda_q1, lambda_k1, lambda_q2, lambda_k2, subln_g, rwkv_mu, w_decay0, w_decay2, a0, a2, g2, k_k, k_a, r_k, ln_x_w, ln_x_b, w_branch_a, w_branch_b, w_out, norm2_g, w_router, router_bias, w_expert_up_gate, w_expert_down, w_shared_up_gate, w_shared_down):
    for l in range(w_ada.shape[0]):
        x = _layer(x, c, positions, l, w_ada[l], b_ada[l], norm1_g[l], w_in[l], w_gate[l], b_gate[l],
                   q_norm_g[l], k_norm_g[l], lambda_q1[l], lambda_k1[l], lambda_q2[l], lambda_k2[l],
                   subln_g[l], rwkv_mu[l], w_decay0[l], w_decay2[l], a0[l], a2[l], g2[l], k_k[l],
                   k_a[l], r_k[l], ln_x_w[l], ln_x_b[l], w_branch_a[l], w_branch_b[l], w_out[l],
                   norm2_g[l], w_router[l], router_bias[l], w_expert_up_gate[l], w_expert_down[l],
                   w_shared_up_gate[l], w_shared_down[l])
    return x
```

```python
import functools
import math

import jax
import jax.numpy as jnp
from jax import lax
from jax.experimental import pallas as pl
from jax.experimental.pallas import tpu as pltpu

F32 = jnp.float32
BF16 = jnp.bfloat16
I32 = jnp.int32
HI = lax.Precision.HIGHEST

CHUNK = 64
ROPE_THETA = 10000.0
NORM_EPS = 1e-6
SUBLN_EPS = 1e-5
DA_HEAD_DIM = 64
RWKV_HEAD = 64
GN_EPS = 64e-5
TOP_K = 8
N_GROUPS = 8
TOPK_GROUPS = 4
ROUTED_SCALE = 2.5
MOE_BLOCK = 128
RWKV_CHUNK = 64
LANES = 128
NEG = -1e30
MAX_PLAIN_SCORE = 40.0
VMEM_LIMIT = 56 * 1024 * 1024


def _cp(*sem):
    return pltpu.CompilerParams(dimension_semantics=sem, vmem_limit_bytes=VMEM_LIMIT)


def _bdot(a, b):
    return jnp.dot(a.astype(BF16), b.astype(BF16), preferred_element_type=F32)


def _fdot(a, b):
    return jnp.dot(a, b, precision=HI, preferred_element_type=F32)


def _nt(a, b, precision=None):
    return lax.dot_general(a, b, (((1,), (1,)), ((), ())), precision=precision,
                           preferred_element_type=F32)


def _tn(a, b, precision=None):
    return lax.dot_general(a, b, (((0,), (0,)), ((), ())), precision=precision,
                           preferred_element_type=F32)


def _sigmoid(x):
    return 1.0 / (1.0 + jnp.exp(-x))


def _silu(x):
    return x * _sigmoid(x)


def _ada_kernel(c_ref, w_ref, b_ref, o_ref):
    o_ref[...] = _fdot(_silu(c_ref[...]), w_ref[...]) + b_ref[...]


def _adaln(c, w_ada, b_ada):
    B, D = c.shape
    rows = -(-B // 8) * 8
    cpad = jnp.zeros((rows, D), F32).at[:B].set(c)
    n_out = w_ada.shape[1]
    out = pl.pallas_call(
        _ada_kernel,
        grid=(n_out // D,),
        in_specs=[pl.BlockSpec((rows, D), lambda j: (0, 0)),
                  pl.BlockSpec((D, D), lambda j: (0, j)),
                  pl.BlockSpec((1, D), lambda j: (0, j))],
        out_specs=pl.BlockSpec((rows, D), lambda j: (0, j)),
        out_shape=jax.ShapeDtypeStruct((rows, n_out), F32),
        compiler_params=_cp("arbitrary"),
        name="adaln",
    )(cpad, w_ada, b_ada.reshape(1, n_out))
    return out[:B].reshape(B, n_out // D, D)


def _inproj_kernel(x_ref, mod_ref, g_ref, w_ref, bg_ref, q_ref, k_ref, v_ref, rw_ref, gate_ref,
                   *, da_width, rw_cols):
    x = x_ref[...]
    y = x * lax.rsqrt(jnp.mean(x * x, axis=-1, keepdims=True) + NORM_EPS) * g_ref[...]
    h = (y * (1.0 + mod_ref[1:2, :]) + mod_ref[0:1, :]).astype(BF16)
    c = 0
    for ref in (q_ref, k_ref, v_ref):
        ref[...] = jnp.dot(h, w_ref[:, c:c + da_width], preferred_element_type=F32).astype(ref.dtype)
        c += da_width
    step = 512
    for o in range(0, rw_cols, step):
        wd = min(step, rw_cols - o)
        rw_ref[:, o:o + wd] = jnp.dot(h, w_ref[:, c + o:c + o + wd], preferred_element_type=F32)
    c += rw_cols
    n_gate = gate_ref.shape[1]
    for o in range(0, n_gate, step):
        z = jnp.dot(h, w_ref[:, c + o:c + o + step], preferred_element_type=F32) + bg_ref[:, o:o + step]
        gate_ref[:, o:o + step] = _sigmoid(z)


def _inproj(x2, mod, norm1_g, w_cat, b_gate, S, da_width, rw_cols, tm=256):
    N, D = x2.shape
    n_gate = b_gate.shape[0]
    tpb = S // tm
    kern = functools.partial(_inproj_kernel, da_width=da_width, rw_cols=rw_cols)
    row = lambda w: pl.BlockSpec((tm, w), lambda i: (i, 0))
    return pl.pallas_call(
        kern,
        grid=(N // tm,),
        in_specs=[row(D),
                  pl.BlockSpec((None, 6, D), lambda i: (i // tpb, 0, 0)),
                  pl.BlockSpec((1, D), lambda i: (0, 0)),
                  pl.BlockSpec(w_cat.shape, lambda i: (0, 0)),
                  pl.BlockSpec((1, n_gate), lambda i: (0, 0))],
        out_specs=[row(da_width), row(da_width), row(da_width), row(rw_cols), row(n_gate)],
        out_shape=[jax.ShapeDtypeStruct((N, da_width), F32),
                   jax.ShapeDtypeStruct((N, da_width), F32),
                   jax.ShapeDtypeStruct((N, da_width), BF16),
                   jax.ShapeDtypeStruct((N, rw_cols), F32),
                   jax.ShapeDtypeStruct((N, n_gate), F32)],
        compiler_params=_cp("parallel"),
        name="inproj",
    )(x2, mod, norm1_g.reshape(1, D), w_cat, b_gate.reshape(1, n_gate))


def _qkprep_kernel(q_ref, k_ref, pos_ref, invf_ref, qg_ref, kg_ref, qo_ref, ko_ref, *, scale):
    tm = q_ref.shape[0]
    lane = lax.broadcasted_iota(I32, (tm, LANES), 1)
    first = lane < DA_HEAD_DIM
    lo_half = (lane & (DA_HEAD_DIM - 1)) < DA_HEAD_DIM // 2
    ang = pos_ref[...].astype(F32) * invf_ref[...]
    cos = jnp.cos(ang)
    sin = jnp.sin(ang)
    sin = jnp.where(lo_half, -sin, sin)

    def one(src, dst, g_ref, mult):
        for blk in range(src.shape[1] // LANES):
            x = src[:, blk * LANES:(blk + 1) * LANES]
            xx = x * x
            s_all = jnp.sum(xx, axis=-1, keepdims=True)
            s_first = jnp.sum(jnp.where(first, xx, 0.0), axis=-1, keepdims=True)
            ms = jnp.where(first, s_first, s_all - s_first) * (1.0 / DA_HEAD_DIM)
            xn = x * lax.rsqrt(ms + NORM_EPS) * g_ref[...]
            rot = jnp.where(lo_half, pltpu.roll(xn, LANES - DA_HEAD_DIM // 2, axis=1),
                            pltpu.roll(xn, DA_HEAD_DIM // 2, axis=1))
            dst[:, blk * LANES:(blk + 1) * LANES] = ((xn * cos + rot * sin) * mult).astype(dst.dtype)

    one(q_ref, qo_ref, qg_ref, scale)
    one(k_ref, ko_ref, kg_ref, 1.0)


def _qkprep(q, k, pos2, q_norm_g, k_norm_g, tm=512):
    N, W = q.shape
    d = DA_HEAD_DIM
    inv_freq = 1.0 / (ROPE_THETA ** (jnp.arange(0, d, 2, dtype=F32) / d))
    invf = jnp.tile(inv_freq, LANES // (d // 2)).reshape(1, LANES)
    row = pl.BlockSpec((tm, W), lambda i: (i, 0))
    vec = pl.BlockSpec((1, LANES), lambda i: (0, 0))
    return pl.pallas_call(
        functools.partial(_qkprep_kernel, scale=d ** -0.5 * math.log2(math.e)),
        grid=(N // tm,),
        in_specs=[row, row, pl.BlockSpec((tm, 1), lambda i: (i, 0)), vec, vec, vec],
        out_specs=[row, row],
        out_shape=[jax.ShapeDtypeStruct((N, W), BF16)] * 2,
        compiler_params=_cp("parallel"),
        name="qkprep",
    )(q, k, pos2, invf, jnp.tile(q_norm_g, 2).reshape(1, LANES), jnp.tile(k_norm_g, 2).reshape(1, LANES))


def _attn_kernel(flag_ref, q_ref, k_ref, v_ref, lam_ref, sg_ref, o_ref, qz_ref, m_ref, l_ref, lp_ref, acc_ref,
                 *, tq, lambda_init):
    i = pl.program_id(2)
    lane = lax.broadcasted_iota(I32, (tq, LANES), 1)
    q = q_ref[...]
    zero = jnp.zeros_like(q)
    qz_ref[0:tq, :] = jnp.where(lane < DA_HEAD_DIM, q, zero)
    qz_ref[tq:, :] = jnp.where(lane >= DA_HEAD_DIM, q, zero)
    acc_ref[...] = jnp.zeros_like(acc_ref)
    bounded = flag_ref[0] == 1

    def scores(j, masked):
        off = pl.multiple_of(j * tq, tq)
        s = _nt(qz_ref[...], k_ref[pl.ds(off, tq), :])
        if masked:
            row = lax.broadcasted_iota(I32, s.shape, 0)
            col = lax.broadcasted_iota(I32, s.shape, 1)
            s = jnp.where((col // CHUNK) <= ((row & (tq - 1)) // CHUNK), s, NEG)
        return s, off

    def plain_step(j, masked):
        s, off = scores(j, masked)
        pr = jnp.exp2(s)
        part = pr[:, 0:LANES]
        for cblk in range(1, tq // LANES):
            part = part + pr[:, cblk * LANES:(cblk + 1) * LANES]
        lp_ref[...] += part
        acc_ref[...] += jnp.dot(pr.astype(BF16), v_ref[pl.ds(off, tq), :], preferred_element_type=F32)

    def online_step(j, masked):
        s, off = scores(j, masked)
        m_old = m_ref[...]
        m_new = jnp.maximum(m_old, jnp.max(s, axis=-1, keepdims=True))
        alpha = jnp.exp2(m_old - m_new)
        pr = jnp.exp2(s - m_new)
        l_ref[...] = alpha * l_ref[...] + jnp.sum(pr, axis=-1, keepdims=True)
        acc_ref[...] = alpha * acc_ref[...] + jnp.dot(pr.astype(BF16), v_ref[pl.ds(off, tq), :],
                                                      preferred_element_type=F32)
        m_ref[...] = m_new

    def run(step):
        def body(j, carry):
            step(j, False)
            return carry
        lax.fori_loop(0, i, body, 0)
        step(i, True)

    @pl.when(bounded)
    def _():
        lp_ref[...] = jnp.zeros_like(lp_ref)
        run(plain_step)
        l_ref[...] = jnp.sum(lp_ref[...], axis=-1, keepdims=True)

    @pl.when(jnp.logical_not(bounded))
    def _():
        m_ref[...] = jnp.full_like(m_ref, NEG)
        l_ref[...] = jnp.zeros_like(l_ref)
        run(online_step)

    lv = lam_ref[...]
    lam = (jnp.exp(jnp.sum(lv[0:1] * lv[1:2], keepdims=True))
           - jnp.exp(jnp.sum(lv[2:3] * lv[3:4], keepdims=True)) + lambda_init)
    o1 = acc_ref[0:tq, :] / l_ref[0:tq, :]
    o2 = acc_ref[tq:, :] / l_ref[tq:, :]
    o = o1 - lam * o2
    o = o * lax.rsqrt(jnp.mean(o * o, axis=-1, keepdims=True) + SUBLN_EPS) * sg_ref[...]
    o_ref[...] = (o * (1.0 - lambda_init)).astype(o_ref.dtype)


def _diff_attention(qn, kn, v, score_bound, lam_vecs, subln_g, B, S, lambda_init, tq=512):
    W = qn.shape[1]
    H = W // LANES
    q3 = qn.reshape(B, S, W)
    k3 = kn.reshape(B, S, W)
    v3 = v.reshape(B, S, W)
    flag = (score_bound <= MAX_PLAIN_SCORE).astype(I32).reshape(1)
    qblk = pl.BlockSpec((None, tq, LANES), lambda b, h, i, f: (b, i, h))
    kvblk = pl.BlockSpec((None, S, LANES), lambda b, h, i, f: (b, 0, h))
    grid_spec = pltpu.PrefetchScalarGridSpec(
        num_scalar_prefetch=1,
        grid=(B, H, S // tq),
        in_specs=[qblk, kvblk, kvblk,
                  pl.BlockSpec((4, DA_HEAD_DIM), lambda b, h, i, f: (0, 0)),
                  pl.BlockSpec((1, LANES), lambda b, h, i, f: (0, 0))],
        out_specs=qblk,
        scratch_shapes=[pltpu.VMEM((2 * tq, LANES), BF16),
                        pltpu.VMEM((2 * tq, 1), F32),
                        pltpu.VMEM((2 * tq, 1), F32),
                        pltpu.VMEM((2 * tq, LANES), F32),
                        pltpu.VMEM((2 * tq, LANES), F32)],
    )
    out = pl.pallas_call(
        functools.partial(_attn_kernel, tq=tq, lambda_init=lambda_init),
        grid_spec=grid_spec,
        out_shape=jax.ShapeDtypeStruct((B, S, W), BF16),
        compiler_params=_cp("parallel", "parallel", "arbitrary"),
        name="diff_attn",
    )(flag, q3, k3, v3, lam_vecs, subln_g.reshape(1, LANES))
    return out.reshape(B * S, W)


def _rwkv_prep_kernel(p_ref, prev_ref, mu_ref, w0_ref, w2_ref, a0_ref, a2_ref, g2_ref,
                      r_ref, k_ref, v_ref, a_ref, ld_ref, g_ref, *, width):
    i = pl.program_id(1)
    p = p_ref[...]
    last_prev = jnp.where(i > 0, prev_ref[7:8, :], 0.0)
    rowi = lax.broadcasted_iota(I32, p.shape, 0)
    prev = jnp.where(rowi == 0, last_prev, pltpu.roll(p, 1, axis=0))
    xs = p + (prev - p) * mu_ref[...]
    r_ref[...] = xs[:, 0:width]
    k_ref[...] = xs[:, width:2 * width]
    v_ref[...] = xs[:, 2 * width:3 * width]
    xwa = xs[:, 3 * width:3 * width + LANES]
    xg = xs[:, 3 * width + LANES:]
    z = w0_ref[...] + _fdot(jnp.tanh(xwa), w2_ref[...])
    w = -(jnp.maximum(-z, 0.0) + jnp.log(1.0 + jnp.exp(-jnp.abs(z)))) - 0.5
    ld_ref[...] = -jnp.exp(w)
    a_ref[...] = _sigmoid(a0_ref[...] + _fdot(xwa, a2_ref[...]))
    g_ref[...] = _fdot(_sigmoid(xg), g2_ref[...])


def _rwkv_prep(prw3, mu, w_decay0, w_decay2, a0, a2, g2, width, tm=256):
    B, S, C = prw3.shape
    dl, al = w_decay2.shape[0], a2.shape[0]
    assert dl + al == LANES and g2.shape[0] == LANES
    w2p = jnp.zeros((LANES, width), F32).at[:dl].set(w_decay2)
    a2p = jnp.zeros((LANES, width), F32).at[dl:].set(a2)
    vec = lambda n: pl.BlockSpec((1, n), lambda b, i: (0, 0))
    mat = pl.BlockSpec((LANES, width), lambda b, i: (0, 0))
    out = pl.BlockSpec((None, tm, width), lambda b, i: (b, i, 0))
    return pl.pallas_call(
        functools.partial(_rwkv_prep_kernel, width=width),
        grid=(B, S // tm),
        in_specs=[pl.BlockSpec((None, tm, C), lambda b, i: (b, i, 0)),
                  pl.BlockSpec((None, 8, C), lambda b, i: (b, jnp.maximum(i * (tm // 8) - 1, 0), 0)),
                  vec(C), vec(width), mat, vec(width), mat, mat],
        out_specs=[out] * 6,
        out_shape=[jax.ShapeDtypeStruct((B, S, width), F32)] * 6,
        compiler_params=_cp("parallel", "parallel"),
        name="rwkv_prep",
    )(prw3, prw3, mu.reshape(1, C), w_decay0.reshape(1, width), w2p, a0.reshape(1, width), a2p, g2)


def _stackmask(m):
    lane = lax.broadcasted_iota(I32, m.shape, 1)
    z = jnp.zeros_like(m)
    return jnp.concatenate([jnp.where(lane < RWKV_HEAD, m, z), jnp.where(lane >= RWKV_HEAD, m, z)], axis=0)


def _pair_sum(x, first):
    s1 = jnp.sum(jnp.where(first, x, 0.0), axis=-1, keepdims=True)
    s2 = jnp.sum(jnp.where(first, 0.0, x), axis=-1, keepdims=True)
    return jnp.where(first, s1, s2)


def _rwkv_scan_kernel(r_ref, k_ref, v_ref, a_ref, ld_ref, g_ref, kk_ref, ka_ref, rk_ref, lnw_ref, lnb_ref,
                      o_ref, s_ref, *, L):
    tm, W = r_ref.shape
    n_chunks = tm // L
    n_pairs = W // LANES
    hd = RWKV_HEAD
    bf = lambda t: t.astype(BF16)

    @pl.when(pl.program_id(1) == 0)
    def _():
        s_ref[...] = jnp.zeros_like(s_ref)

    row = lax.broadcasted_iota(I32, (tm, tm), 0)
    col = lax.broadcasted_iota(I32, (tm, tm), 1)
    tri = jnp.where(jnp.logical_and(col <= row, (col // L) == (row // L)), 1.0, 0.0).astype(BF16)
    ld = ld_ref[...]
    ld_hi = bf(ld)
    rem = ld - ld_hi.astype(F32)
    ld_mid = bf(rem)
    ld_lo = bf(rem - ld_mid.astype(F32))
    c = (jnp.dot(tri, ld_hi, preferred_element_type=F32) + jnp.dot(tri, ld_mid, preferred_element_type=F32)
         + jnp.dot(tri, ld_lo, preferred_element_type=F32))
    ec = jnp.exp(c)
    eci = jnp.exp(-c)
    ecm = jnp.exp(c - ld)
    r = r_ref[...]
    k = k_ref[...]
    v = v_ref[...]
    a = a_ref[...]
    kkr = k * kk_ref[...]
    kmod = k * (1.0 + (a - 1.0) * ka_ref[...])
    brk = r * kmod * rk_ref[...]

    lane = lax.broadcasted_iota(I32, (L, LANES), 1)
    rowl = lax.broadcasted_iota(I32, (L, LANES), 0)
    first = lane < hd
    lane_h = lane & (hd - 1)
    strict = lane_h < rowl
    incl = lane_h <= rowl
    eye = jnp.where(lane_h == rowl, 1.0, 0.0)

    chains = [(ch, p) for ch in range(n_chunks) for p in range(n_pairs)]
    rsl = lambda ch: slice(ch * L, (ch + 1) * L)
    csl = lambda p: slice(p * LANES, (p + 1) * LANES)
    fdot = lambda x, y: jnp.dot(x, y, preferred_element_type=F32)
    at, bt, kt, rt, vh, g_l = {}, {}, {}, {}, {}, {}
    for c_ in chains:
        ch, p = c_
        rs, cs = rsl(ch), csl(p)
        kkh = kkr[rs, cs]
        kkh = kkh / jnp.maximum(jnp.sqrt(_pair_sum(kkh * kkh, first)), 1e-12)
        vh[c_] = v[rs, cs]
        g_l[c_] = ec[ch * L + L - 1:ch * L + L, cs]
        at[c_] = -kkh * ecm[rs, cs]
        bt[c_] = kkh * a[rs, cs] * eci[rs, cs]
        kt[c_] = kmod[rs, cs] * eci[rs, cs]
        rt[c_] = r[rs, cs] * ec[rs, cs]
    gm = {c_: _nt(bf(jnp.concatenate([at[c_], rt[c_]], axis=0)),
                  jnp.concatenate([_stackmask(bf(bt[c_])), _stackmask(bf(kt[c_]))], axis=0)) for c_ in chains}
    a_ab = {c_: jnp.where(strict, gm[c_][:L, :LANES], 0.0) for c_ in chains}
    vsm = {c_: _stackmask(bf(vh[c_])) for c_ in chains}
    cmat = {c_: fdot(bf(jnp.where(strict, gm[c_][:L, LANES:], 0.0)), vsm[c_]) for c_ in chains}
    t_inv = {c_: eye + a_ab[c_] for c_ in chains}
    pw = {c_: bf(a_ab[c_]) for c_ in chains}
    for _ in range(int(math.log2(L)) - 1):
        pw = {c_: bf(fdot(pw[c_], _stackmask(pw[c_]))) for c_ in chains}
        t_inv = {c_: t_inv[c_] + fdot(pw[c_], _stackmask(bf(t_inv[c_]))) for c_ in chains}
    zz = {c_: fdot(bf(t_inv[c_]), jnp.concatenate([_stackmask(bf(at[c_])), _stackmask(bf(cmat[c_]))], axis=1))
          for c_ in chains}
    qy = {c_: fdot(bf(jnp.where(incl, gm[c_][L:, :LANES], 0.0)),
                   jnp.concatenate([_stackmask(bf(zz[c_][:, :LANES])), _stackmask(bf(zz[c_][:, LANES:]))], axis=1))
          for c_ in chains}
    y0 = {c_: qy[c_][:, LANES:] + fdot(bf(jnp.where(incl, gm[c_][L:, LANES:], 0.0)), vsm[c_]) for c_ in chains}
    qa = {c_: bf(jnp.concatenate([rt[c_] + qy[c_][:, :LANES], zz[c_][:, :LANES]], axis=0)) for c_ in chains}
    bkg = {c_: bf(jnp.concatenate([bt[c_] * g_l[c_], kt[c_] * g_l[c_]], axis=0)) for c_ in chains}

    lane_s = lax.broadcasted_iota(I32, (hd, LANES), 1)
    sp = [s_ref[p] for p in range(n_pairs)]
    for ch in range(n_chunks):
        rs = rsl(ch)
        yw = [_nt(qa[ch, p], _stackmask(bf(sp[p]))) for p in range(n_pairs)]
        upd = [_tn(bf(jnp.concatenate([yw[p][L:] + zz[ch, p][:, LANES:], vh[ch, p]], axis=0)), bkg[ch, p])
               for p in range(n_pairs)]
        for p in range(n_pairs):
            cs = csl(p)
            sp[p] = sp[p] * g_l[ch, p] + jnp.where(lane_s < hd, upd[p][:hd], upd[p][hd:])
            y = yw[p][:L] + y0[ch, p]
            mean = _pair_sum(y, first) * (1.0 / hd)
            yc = y - mean
            var = _pair_sum(yc * yc, first) * (1.0 / hd)
            yn = yc * lax.rsqrt(var + GN_EPS) * lnw_ref[:, cs] + lnb_ref[:, cs]
            bonus = _pair_sum(brk[rs, cs], first) * vh[ch, p]
            o_ref[rs, cs] = ((yn + bonus) * g_ref[rs, cs]).astype(o_ref.dtype)
    for p in range(n_pairs):
        s_ref[p] = sp[p]


def _rwkv_scan(r, k, v, a, ld, g, k_k, k_a, r_k, ln_w, ln_b, L=RWKV_CHUNK, tm=256):
    B, S, W = r.shape
    seq = pl.BlockSpec((None, tm, W), lambda b, c: (b, c, 0))
    vec = pl.BlockSpec((1, W), lambda b, c: (0, 0))
    return pl.pallas_call(
        functools.partial(_rwkv_scan_kernel, L=L),
        grid=(B, S // tm),
        in_specs=[seq] * 6 + [vec] * 5,
        out_specs=seq,
        out_shape=jax.ShapeDtypeStruct((B, S, W), BF16),
        scratch_shapes=[pltpu.VMEM((W // LANES, RWKV_HEAD, LANES), F32)],
        compiler_params=_cp("parallel", "arbitrary"),
        name="rwkv_scan",
    )(r, k, v, a, ld, g, k_k.reshape(1, W), k_a.reshape(1, W), r_k.reshape(1, W),
      ln_w.reshape(1, W), ln_b.reshape(1, W))


def _post_kernel(attn_ref, rw_ref, gate_ref, x_ref, mod_ref, wa_ref, wb_ref, wo_ref, g2_ref, wr_ref,
                 x1_ref, h2_ref, lg_ref):
    D = x_ref.shape[1]
    ya = jnp.dot(attn_ref[...], wa_ref[...], preferred_element_type=F32)
    yb = jnp.dot(rw_ref[...], wb_ref[...], preferred_element_type=F32)
    m = gate_ref[:, 0:D] * ya + gate_ref[:, D:] * yb
    x1 = x_ref[...] + mod_ref[2:3, :] * jnp.dot(m.astype(BF16), wo_ref[...], preferred_element_type=F32)
    x1_ref[...] = x1
    y = x1 * lax.rsqrt(jnp.mean(x1 * x1, axis=-1, keepdims=True) + NORM_EPS) * g2_ref[...]
    h2 = y * (1.0 + mod_ref[4:5, :]) + mod_ref[3:4, :]
    h2_ref[...] = h2
    lg_ref[...] = _nt(wr_ref[...], h2, precision=HI)


def _post(attn, rw, gate, x2, mod, wa, wb, wo, norm2_g, w_router_t, S, tm=256):
    N, D = x2.shape
    E = w_router_t.shape[0]
    tpb = S // tm
    row = lambda w: pl.BlockSpec((tm, w), lambda i: (i, 0))
    full = lambda a: pl.BlockSpec(a.shape, lambda i: (0, 0))
    return pl.pallas_call(
        _post_kernel,
        grid=(N // tm,),
        in_specs=[row(attn.shape[1]), row(rw.shape[1]), row(gate.shape[1]), row(D),
                  pl.BlockSpec((None, 6, D), lambda i: (i // tpb, 0, 0)),
                  full(wa), full(wb), full(wo), pl.BlockSpec((1, D), lambda i: (0, 0)), full(w_router_t)],
        out_specs=[row(D), row(D), pl.BlockSpec((E, tm), lambda i: (0, i))],
        out_shape=[jax.ShapeDtypeStruct((N, D), F32), jax.ShapeDtypeStruct((N, D), F32),
                   jax.ShapeDtypeStruct((E, N), F32)],
        compiler_params=_cp("parallel"),
        name="post_mixer",
    )(attn, rw, gate, x2, mod, wa, wb, wo, norm2_g.reshape(1, D), w_router_t)


def _first_argmax(x, idx, sentinel):
    m = jnp.max(x, axis=0, keepdims=True)
    return m, jnp.min(jnp.where(x == m, idx, sentinel), axis=0, keepdims=True)


def _route_kernel(lg_ref, bias_ref, e_ref, w_ref):
    E, T = lg_ref.shape
    gsz = E // N_GROUPS
    scores = _sigmoid(lg_ref[...])
    biased = scores + bias_ref[...]
    ig = lax.broadcasted_iota(I32, (gsz, T), 0)
    gs = []
    for g in range(N_GROUPS):
        blk = biased[g * gsz:(g + 1) * gsz, :]
        m1, i1 = _first_argmax(blk, ig, gsz)
        m2 = jnp.max(jnp.where(ig == i1, -jnp.inf, blk), axis=0, keepdims=True)
        gs.append(m1 + m2)
    gsc = jnp.concatenate(gs, axis=0)
    i8 = lax.broadcasted_iota(I32, (N_GROUPS, T), 0)
    chosen = jnp.zeros((N_GROUPS, T), F32)
    for _ in range(TOPK_GROUPS):
        _, gi = _first_argmax(gsc, i8, N_GROUPS)
        hit = i8 == gi
        chosen = jnp.where(hit, 1.0, chosen)
        gsc = jnp.where(hit, -jnp.inf, gsc)
    masked = jnp.concatenate(
        [jnp.where(chosen[g:g + 1, :] > 0.0, biased[g * gsz:(g + 1) * gsz, :], -jnp.inf)
         for g in range(N_GROUPS)], axis=0)
    ie = lax.broadcasted_iota(I32, (E, T), 0)
    idxs, wts = [], []
    for _ in range(TOP_K):
        _, ei = _first_argmax(masked, ie, E)
        hit = ie == ei
        idxs.append(ei)
        wts.append(jnp.sum(jnp.where(hit, scores, 0.0), axis=0, keepdims=True))
        masked = jnp.where(hit, -jnp.inf, masked)
    wt = jnp.concatenate(wts, axis=0)
    e_ref[...] = jnp.concatenate(idxs, axis=0)
    w_ref[...] = wt / jnp.sum(wt, axis=0, keepdims=True) * ROUTED_SCALE


def _route(logits_t, router_bias, T=512):
    E, N = logits_t.shape
    blk = pl.BlockSpec((TOP_K, T), lambda i: (0, i))
    return pl.pallas_call(
        _route_kernel,
        grid=(N // T,),
        in_specs=[pl.BlockSpec((E, T), lambda i: (0, i)), pl.BlockSpec((E, 1), lambda i: (0, 0))],
        out_specs=[blk, blk],
        out_shape=[jax.ShapeDtypeStruct((TOP_K, N), I32), jax.ShapeDtypeStruct((TOP_K, N), F32)],
        compiler_params=_cp("parallel"),
        name="route",
    )(logits_t, router_bias.reshape(E, 1))


def _rank_kernel(e_ref, rank_ref, cnt_ref, carry_ref, *, n_experts):
    T = e_ref.shape[1]

    @pl.when(pl.program_id(0) == 0)
    def _():
        carry_ref[...] = jnp.zeros_like(carry_ref)

    ie = lax.broadcasted_iota(I32, (n_experts, T), 0)
    e = e_ref[...]
    hits = [ie == e[kk:kk + 1, :] for kk in range(TOP_K)]
    onehot = jnp.zeros((n_experts, T), F32)
    for hsel in hits:
        onehot = onehot + hsel.astype(F32)
    tr = lax.broadcasted_iota(I32, (T, T), 0)
    tc = lax.broadcasted_iota(I32, (T, T), 1)
    before = (tr < tc).astype(BF16)
    base = _bdot(onehot, before) + carry_ref[:, 0:1]
    rank_ref[...] = jnp.concatenate(
        [jnp.sum(jnp.where(hsel, base, 0.0), axis=0, keepdims=True) for hsel in hits], axis=0).astype(I32)
    carry_ref[...] = carry_ref[...] + jnp.sum(onehot, axis=1, keepdims=True)
    cnt_ref[...] = carry_ref[...]


def _ranks(eidx_t, n_experts, T=512):
    N = eidx_t.shape[1]
    blk = pl.BlockSpec((TOP_K, T), lambda i: (0, i))
    cnt = pl.BlockSpec((n_experts, LANES), lambda i: (0, 0))
    return pl.pallas_call(
        functools.partial(_rank_kernel, n_experts=n_experts),
        grid=(N // T,),
        in_specs=[blk],
        out_specs=[blk, cnt],
        out_shape=[jax.ShapeDtypeStruct((TOP_K, N), I32), jax.ShapeDtypeStruct((n_experts, LANES), F32)],
        scratch_shapes=[pltpu.VMEM((n_experts, LANES), F32)],
        compiler_params=_cp("arbitrary"),
        name="ranks",
    )(eidx_t)


def _dest_kernel(e_ref, rank_ref, start_ref, d_ref):
    E = start_ref.shape[0]
    T = e_ref.shape[1]
    ie = lax.broadcasted_iota(I32, (E, T), 0)
    e = e_ref[...]
    start = start_ref[:, 0:1]
    rows = [jnp.sum(jnp.where(ie == e[kk:kk + 1, :], start, 0.0), axis=0, keepdims=True) for kk in range(TOP_K)]
    d_ref[...] = jnp.concatenate(rows, axis=0).astype(I32) + rank_ref[...]


def _dests(eidx_t, rank_t, pstart, T=512):
    N = eidx_t.shape[1]
    E = pstart.shape[0]
    blk = pl.BlockSpec((TOP_K, T), lambda i: (0, i))
    return pl.pallas_call(
        _dest_kernel,
        grid=(N // T,),
        in_specs=[blk, blk, pl.BlockSpec((E, LANES), lambda i: (0, 0))],
        out_specs=blk,
        out_shape=jax.ShapeDtypeStruct((TOP_K, N), I32),
        compiler_params=_cp("parallel"),
        name="dests",
    )(eidx_t, rank_t, jnp.broadcast_to(pstart.astype(F32)[:, None], (E, LANES)))


def _dispatch_kernel(d_ref, h_ref, xg_in_ref, xg_ref, sem):
    del xg_in_ref
    T = h_ref.shape[0]

    def body(t, carry):
        for kk in range(TOP_K):
            pltpu.make_async_copy(h_ref.at[pl.ds(t, 1)], xg_ref.at[pl.ds(d_ref[kk, t], 1)], sem).start()
        return carry

    lax.fori_loop(0, T, body, 0)
    for kk in range(TOP_K):
        pltpu.make_async_copy(h_ref, xg_ref.at[pl.ds(0, T)], sem).wait()


def _dispatch(dest_t, h2, n_rows, T=128):
    N, D = h2.shape
    zeros = jnp.zeros((n_rows, D), h2.dtype)
    return pl.pallas_call(
        _dispatch_kernel,
        grid=(N // T,),
        in_specs=[pl.BlockSpec((TOP_K, T), lambda i: (0, i), memory_space=pltpu.SMEM),
                  pl.BlockSpec((T, D), lambda i: (i, 0)),
                  pl.BlockSpec(memory_space=pl.ANY)],
        out_specs=pl.BlockSpec(memory_space=pl.ANY),
        out_shape=jax.ShapeDtypeStruct((n_rows, D), h2.dtype),
        scratch_shapes=[pltpu.SemaphoreType.DMA],
        input_output_aliases={2: 0},
        compiler_params=_cp("arbitrary"),
        name="dispatch",
    )(dest_t, h2, zeros)


def _expert_kernel(be_ref, x_ref, wug_ref, wd_ref, y_ref, wug_bf, wd_bf):
    i = pl.program_id(0)
    F = wd_ref.shape[0]
    changed = jnp.logical_or(i == 0, be_ref[i] != be_ref[jnp.maximum(i - 1, 0)])

    @pl.when(changed)
    def _():
        wug_bf[...] = wug_ref[...].astype(BF16)
        wd_bf[...] = wd_ref[...].astype(BF16)

    gu = jnp.dot(x_ref[...].astype(BF16), wug_bf[...], preferred_element_type=F32)
    hid = _silu(gu[:, :F]) * gu[:, F:]
    y_ref[...] = jnp.dot(hid.astype(BF16), wd_bf[...], preferred_element_type=F32)


def _experts(block_e, xg, w_ug, w_d):
    n_rows, D = xg.shape
    E, _, F2 = w_ug.shape
    F = w_d.shape[1]
    n_blocks = n_rows // MOE_BLOCK
    grid_spec = pltpu.PrefetchScalarGridSpec(
        num_scalar_prefetch=1,
        grid=(n_blocks,),
        in_specs=[pl.BlockSpec((MOE_BLOCK, D), lambda i, be: (i, 0)),
                  pl.BlockSpec((None, D, F2), lambda i, be: (be[i], 0, 0)),
                  pl.BlockSpec((None, F, D), lambda i, be: (be[i], 0, 0))],
        out_specs=pl.BlockSpec((MOE_BLOCK, D), lambda i, be: (i, 0)),
        scratch_shapes=[pltpu.VMEM((D, F2), BF16), pltpu.VMEM((F, D), BF16)],
    )
    return pl.pallas_call(
        _expert_kernel,
        grid_spec=grid_spec,
        out_shape=jax.ShapeDtypeStruct((n_rows, D), F32),
        compiler_params=_cp("arbitrary"),
        name="experts",
    )(block_e, xg, w_ug, w_d)


def _combine_kernel(d_ref, y_ref, w_ref, h_ref, x1_ref, mod_ref, sug_ref, sd_ref, o_ref, buf, sem):
    T = h_ref.shape[0]
    F = sd_ref.shape[0]

    def body(t, carry):
        for kk in range(TOP_K):
            pltpu.make_async_copy(y_ref.at[pl.ds(d_ref[kk, t], 1)], buf.at[kk, pl.ds(t, 1)], sem).start()
        return carry

    lax.fori_loop(0, T, body, 0)
    gu = _bdot(h_ref[...], sug_ref[...])
    shared = _bdot(_silu(gu[:, :F]) * gu[:, F:], sd_ref[...])
    tr = lax.broadcasted_iota(I32, (T, T), 0)
    tc = lax.broadcasted_iota(I32, (T, T), 1)
    wcol = _nt((tr == tc).astype(F32), w_ref[...], precision=HI)
    for kk in range(TOP_K):
        pltpu.make_async_copy(y_ref.at[pl.ds(0, T)], buf.at[kk], sem).wait()
    routed = buf[0] * wcol[:, 0:1]
    for kk in range(1, TOP_K):
        routed = routed + buf[kk] * wcol[:, kk:kk + 1]
    o_ref[...] = x1_ref[...] + mod_ref[5:6, :] * (shared + routed)


def _combine(dest_t, y, w_t, h2, x1, mod, sug, sd, S, T=128):
    N, D = h2.shape
    tpb = S // T
    row = pl.BlockSpec((T, D), lambda i: (i, 0))
    full = lambda a: pl.BlockSpec(a.shape, lambda i: (0, 0))
    return pl.pallas_call(
        _combine_kernel,
        grid=(N // T,),
        in_specs=[pl.BlockSpec((TOP_K, T), lambda i: (0, i), memory_space=pltpu.SMEM),
                  pl.BlockSpec(memory_space=pl.ANY),
                  pl.BlockSpec((TOP_K, T), lambda i: (0, i)),
                  row, row,
                  pl.BlockSpec((None, 6, D), lambda i: (i // tpb, 0, 0)),
                  full(sug), full(sd)],
        out_specs=row,
        out_shape=jax.ShapeDtypeStruct((N, D), F32),
        scratch_shapes=[pltpu.VMEM((TOP_K, T, D), F32), pltpu.SemaphoreType.DMA],
        compiler_params=_cp("arbitrary"),
        name="combine",
    )(dest_t, y, w_t, h2, x1, mod, sug, sd)


def _layer(x, c, positions, layer_idx, w_ada, b_ada, norm1_g, w_in, w_gate, b_gate,
           q_norm_g, k_norm_g, lambda_q1, lambda_k1, lambda_q2, lambda_k2, subln_g,
           rwkv_mu, w_decay0, w_decay2, a0, a2, g2, k_k, k_a, r_k, ln_x_w, ln_x_b,
           w_branch_a, w_branch_b, w_out, norm2_g, w_router, router_bias,
           w_expert_up_gate, w_expert_down, w_shared_up_gate, w_shared_down):
    B, S, D = x.shape
    N = B * S
    E = w_router.shape[1]
    da_width = w_branch_a.shape[0]
    rw_width = w_branch_b.shape[0]
    rw_cols = rwkv_mu.shape[0]
    lambda_init = 0.8 - 0.6 * math.exp(-0.3 * layer_idx)

    mod = _adaln(c, w_ada, b_ada)
    x2 = x.reshape(N, D)
    w_cat = jnp.concatenate([w_in, w_gate], axis=1).astype(BF16)
    q, k, v, prw, gate = _inproj(x2, mod, norm1_g, w_cat, b_gate, S, da_width, rw_cols)

    qn, kn = _qkprep(q, k, positions.reshape(N, 1), q_norm_g, k_norm_g)
    lam_vecs = jnp.stack([lambda_q1, lambda_k1, lambda_q2, lambda_k2])
    score_bound = 1.01 * DA_HEAD_DIM ** 0.5 * jnp.max(jnp.abs(q_norm_g)) * jnp.max(jnp.abs(k_norm_g))
    attn = _diff_attention(qn, kn, v, score_bound, lam_vecs, subln_g, B, S, lambda_init)

    r_, k_, v_, a_, ld_, g_ = _rwkv_prep(prw.reshape(B, S, rw_cols), rwkv_mu, w_decay0, w_decay2,
                                         a0, a2, g2, rw_width)
    rw = _rwkv_scan(r_, k_, v_, a_, ld_, g_, k_k, k_a, r_k.reshape(-1), ln_x_w, ln_x_b).reshape(N, rw_width)

    x1, h2, logits_t = _post(attn, rw, gate, x2, mod, w_branch_a.astype(BF16), w_branch_b.astype(BF16),
                             w_out.astype(BF16), norm2_g, w_router.T, S)

    eidx_t, w_t = _route(logits_t, router_bias)
    rank_t, counts = _ranks(eidx_t, E)
    cnt = counts[:, 0].astype(I32)
    pcnt = (cnt + MOE_BLOCK - 1) // MOE_BLOCK * MOE_BLOCK
    pends = jnp.cumsum(pcnt)
    dest_t = _dests(eidx_t, rank_t, pends - pcnt)
    n_blocks = (N * TOP_K + E * (MOE_BLOCK - 1) + MOE_BLOCK - 1) // MOE_BLOCK
    block_e = jnp.minimum(jnp.searchsorted(pends, jnp.arange(n_blocks, dtype=I32) * MOE_BLOCK, side='right'),
                          E - 1).astype(I32)
    xg = _dispatch(dest_t, h2, n_blocks * MOE_BLOCK)
    y = _experts(block_e, xg, w_expert_up_gate, w_expert_down)
    out = _combine(dest_t, y, w_t, h2, x1, mod, w_shared_up_gate.astype(BF16), w_shared_down.astype(BF16), S)
    return out.reshape(B, S, D)


def kernel(x, c, positions, w_ada, b_ada, norm1_g, w_in, w_gate, b_gate, q_norm_g, k_norm_g, lambda_q1, lambda_k1, lambda_q2, lambda_k2, subln_g, rwkv_mu, w_decay0, w_decay2, a0, a2, g2, k_k, k_a, r_k, ln_x_w, ln_x_b, w_branch_a, w_branch_b, w_out, norm2_g, w_router, router_bias, w_expert_up_gate, w_expert_down, w_shared_up_gate, w_shared_down):
    for l in range(w_ada.shape[0]):
        x = _layer(x, c, positions, l, w_ada[l], b_ada[l], norm1_g[l], w_in[l], w_gate[l], b_gate[l],
                   q_norm_g[l], k_norm_g[l], lambda_q1[l], lambda_k1[l], lambda_q2[l], lambda_k2[l],
                   subln_g[l], rwkv_mu[l], w_decay0[l], w_decay2[l], a0[l], a2[l], g2[l], k_k[l],
                   k_a[l], r_k[l], ln_x_w[l], ln_x_b[l], w_branch_a[l], w_branch_b[l], w_out[l],
                   norm2_g[l], w_router[l], router_bias[l], w_expert_up_gate[l], w_expert_down[l],
                   w_shared_up_gate[l], w_shared_down[l])
    return x
```

```python
import functools
import math

import jax
import jax.numpy as jnp
from jax import lax
from jax.experimental import pallas as pl
from jax.experimental.pallas import tpu as pltpu

F32 = jnp.float32
BF16 = jnp.bfloat16
I32 = jnp.int32
U32 = jnp.uint32
HI = lax.Precision.HIGHEST

CHUNK = 64
ROPE_THETA = 10000.0
NORM_EPS = 1e-6
SUBLN_EPS = 1e-5
DA_HEAD_DIM = 64
RWKV_HEAD = 64
GN_EPS = 64e-5
TOP_K = 8
N_GROUPS = 8
TOPK_GROUPS = 4
ROUTED_SCALE = 2.5
EXPERT_TILE = 256
RWKV_CHUNK = 64
LANES = 128
NEG = -1e30
MAX_PLAIN_SCORE = 40.0
VMEM_LIMIT = 56 * 1024 * 1024


def _cp(*sem):
    return pltpu.CompilerParams(dimension_semantics=sem, vmem_limit_bytes=VMEM_LIMIT)


def _bdot(a, b):
    return jnp.dot(a.astype(BF16), b.astype(BF16), preferred_element_type=F32)


def _fdot(a, b):
    return jnp.dot(a, b, precision=HI, preferred_element_type=F32)


def _nt(a, b, precision=None):
    return lax.dot_general(a, b, (((1,), (1,)), ((), ())), precision=precision,
                           preferred_element_type=F32)


def _tn(a, b, precision=None):
    return lax.dot_general(a, b, (((0,), (0,)), ((), ())), precision=precision,
                           preferred_element_type=F32)


def _pack_halves(x):
    c = x.shape[1] // 2
    lo = lax.bitcast_convert_type(x[:, :c], U32)
    hi = lax.bitcast_convert_type(x[:, c:], U32)
    return (hi & jnp.uint32(0xFFFF0000)) | (lo >> 16)


def _unpack_halves(w):
    lo = lax.bitcast_convert_type(w << 16, F32)
    hi = lax.bitcast_convert_type(w & jnp.uint32(0xFFFF0000), F32)
    return lo, hi


def _sigmoid(x):
    return 1.0 / (1.0 + jnp.exp(-x))


def _silu(x):
    return x * _sigmoid(x)


def _ada_kernel(c_ref, w_ref, b_ref, o_ref):
    o_ref[...] = _fdot(_silu(c_ref[...]), w_ref[...]) + b_ref[...]


def _adaln(c, w_ada, b_ada):
    B, D = c.shape
    rows = -(-B // 8) * 8
    cpad = jnp.zeros((rows, D), F32).at[:B].set(c)
    n_out = w_ada.shape[1]
    out = pl.pallas_call(
        _ada_kernel,
        grid=(n_out // D,),
        in_specs=[pl.BlockSpec((rows, D), lambda j: (0, 0)),
                  pl.BlockSpec((D, D), lambda j: (0, j)),
                  pl.BlockSpec((1, D), lambda j: (0, j))],
        out_specs=pl.BlockSpec((rows, D), lambda j: (0, j)),
        out_shape=jax.ShapeDtypeStruct((rows, n_out), F32),
        compiler_params=_cp("arbitrary"),
        name="adaln",
    )(cpad, w_ada, b_ada.reshape(1, n_out))
    return out[:B].reshape(B, n_out // D, D)


def _inproj_kernel(x_ref, mod_ref, g_ref, w_ref, bg_ref, q_ref, k_ref, v_ref, rw_ref, gate_ref,
                   *, da_width, rw_cols):
    x = x_ref[...]
    y = x * lax.rsqrt(jnp.mean(x * x, axis=-1, keepdims=True) + NORM_EPS) * g_ref[...]
    h = (y * (1.0 + mod_ref[1:2, :]) + mod_ref[0:1, :]).astype(BF16)
    c = 0
    for ref in (q_ref, k_ref, v_ref):
        ref[...] = jnp.dot(h, w_ref[:, c:c + da_width], preferred_element_type=F32).astype(ref.dtype)
        c += da_width
    step = 512
    for o in range(0, rw_cols, step):
        wd = min(step, rw_cols - o)
        rw_ref[:, o:o + wd] = jnp.dot(h, w_ref[:, c + o:c + o + wd], preferred_element_type=F32)
    c += rw_cols
    n_gate = gate_ref.shape[1]
    for o in range(0, n_gate, step):
        z = jnp.dot(h, w_ref[:, c + o:c + o + step], preferred_element_type=F32) + bg_ref[:, o:o + step]
        gate_ref[:, o:o + step] = _sigmoid(z)


def _inproj(x2, mod, norm1_g, w_cat, b_gate, S, da_width, rw_cols, tm=256):
    N, D = x2.shape
    n_gate = b_gate.shape[0]
    tpb = S // tm
    kern = functools.partial(_inproj_kernel, da_width=da_width, rw_cols=rw_cols)
    row = lambda w: pl.BlockSpec((tm, w), lambda i: (i, 0))
    return pl.pallas_call(
        kern,
        grid=(N // tm,),
        in_specs=[row(D),
                  pl.BlockSpec((None, 6, D), lambda i: (i // tpb, 0, 0)),
                  pl.BlockSpec((1, D), lambda i: (0, 0)),
                  pl.BlockSpec(w_cat.shape, lambda i: (0, 0)),
                  pl.BlockSpec((1, n_gate), lambda i: (0, 0))],
        out_specs=[row(da_width), row(da_width), row(da_width), row(rw_cols), row(n_gate)],
        out_shape=[jax.ShapeDtypeStruct((N, da_width), F32),
                   jax.ShapeDtypeStruct((N, da_width), F32),
                   jax.ShapeDtypeStruct((N, da_width), BF16),
                   jax.ShapeDtypeStruct((N, rw_cols), F32),
                   jax.ShapeDtypeStruct((N, n_gate), F32)],
        compiler_params=_cp("parallel"),
        name="inproj",
    )(x2, mod, norm1_g.reshape(1, D), w_cat, b_gate.reshape(1, n_gate))


def _qkprep_kernel(q_ref, k_ref, pos_ref, invf_ref, qg_ref, kg_ref, qo_ref, ko_ref, *, scale):
    tm = q_ref.shape[0]
    lane = lax.broadcasted_iota(I32, (tm, LANES), 1)
    first = lane < DA_HEAD_DIM
    lo_half = (lane & (DA_HEAD_DIM - 1)) < DA_HEAD_DIM // 2
    ang = pos_ref[...].astype(F32) * invf_ref[...]
    cos = jnp.cos(ang)
    sin = jnp.sin(ang)
    sin = jnp.where(lo_half, -sin, sin)

    def one(src, dst, g_ref, mult):
        for blk in range(src.shape[1] // LANES):
            x = src[:, blk * LANES:(blk + 1) * LANES]
            xx = x * x
            s_all = jnp.sum(xx, axis=-1, keepdims=True)
            s_first = jnp.sum(jnp.where(first, xx, 0.0), axis=-1, keepdims=True)
            ms = jnp.where(first, s_first, s_all - s_first) * (1.0 / DA_HEAD_DIM)
            xn = x * lax.rsqrt(ms + NORM_EPS) * g_ref[...]
            rot = jnp.where(lo_half, pltpu.roll(xn, LANES - DA_HEAD_DIM // 2, axis=1),
                            pltpu.roll(xn, DA_HEAD_DIM // 2, axis=1))
            dst[:, blk * LANES:(blk + 1) * LANES] = ((xn * cos + rot * sin) * mult).astype(dst.dtype)

    one(q_ref, qo_ref, qg_ref, scale)
    one(k_ref, ko_ref, kg_ref, 1.0)


def _qkprep(q, k, pos2, q_norm_g, k_norm_g, tm=512):
    N, W = q.shape
    d = DA_HEAD_DIM
    inv_freq = 1.0 / (ROPE_THETA ** (jnp.arange(0, d, 2, dtype=F32) / d))
    invf = jnp.tile(inv_freq, LANES // (d // 2)).reshape(1, LANES)
    row = pl.BlockSpec((tm, W), lambda i: (i, 0))
    vec = pl.BlockSpec((1, LANES), lambda i: (0, 0))
    return pl.pallas_call(
        functools.partial(_qkprep_kernel, scale=d ** -0.5 * math.log2(math.e)),
        grid=(N // tm,),
        in_specs=[row, row, pl.BlockSpec((tm, 1), lambda i: (i, 0)), vec, vec, vec],
        out_specs=[row, row],
        out_shape=[jax.ShapeDtypeStruct((N, W), BF16)] * 2,
        compiler_params=_cp("parallel"),
        name="qkprep",
    )(q, k, pos2, invf, jnp.tile(q_norm_g, 2).reshape(1, LANES), jnp.tile(k_norm_g, 2).reshape(1, LANES))


def _attn_kernel(flag_ref, q_ref, k_ref, v_ref, lam_ref, sg_ref, o_ref, qz_ref, m_ref, l_ref, lp_ref, acc_ref,
                 *, tq, lambda_init):
    i = pl.program_id(2)
    lane = lax.broadcasted_iota(I32, (tq, LANES), 1)
    q = q_ref[...]
    zero = jnp.zeros_like(q)
    qz_ref[0:tq, :] = jnp.where(lane < DA_HEAD_DIM, q, zero)
    qz_ref[tq:, :] = jnp.where(lane >= DA_HEAD_DIM, q, zero)
    acc_ref[...] = jnp.zeros_like(acc_ref)
    bounded = flag_ref[0] == 1

    def scores(j, masked):
        off = pl.multiple_of(j * tq, tq)
        s = _nt(qz_ref[...], k_ref[pl.ds(off, tq), :])
        if masked:
            row = lax.broadcasted_iota(I32, s.shape, 0)
            col = lax.broadcasted_iota(I32, s.shape, 1)
            s = jnp.where((col // CHUNK) <= ((row & (tq - 1)) // CHUNK), s, NEG)
        return s, off

    def plain_step(j, masked):
        s, off = scores(j, masked)
        pr = jnp.exp2(s)
        part = pr[:, 0:LANES]
        for cblk in range(1, tq // LANES):
            part = part + pr[:, cblk * LANES:(cblk + 1) * LANES]
        lp_ref[...] += part
        acc_ref[...] += jnp.dot(pr.astype(BF16), v_ref[pl.ds(off, tq), :], preferred_element_type=F32)

    def online_step(j, masked):
        s, off = scores(j, masked)
        m_old = m_ref[...]
        m_new = jnp.maximum(m_old, jnp.max(s, axis=-1, keepdims=True))
        alpha = jnp.exp2(m_old - m_new)
        pr = jnp.exp2(s - m_new)
        l_ref[...] = alpha * l_ref[...] + jnp.sum(pr, axis=-1, keepdims=True)
        acc_ref[...] = alpha * acc_ref[...] + jnp.dot(pr.astype(BF16), v_ref[pl.ds(off, tq), :],
                                                      preferred_element_type=F32)
        m_ref[...] = m_new

    def run(step):
        def body(j, carry):
            step(j, False)
            return carry
        lax.fori_loop(0, i, body, 0)
        step(i, True)

    @pl.when(bounded)
    def _():
        lp_ref[...] = jnp.zeros_like(lp_ref)
        run(plain_step)
        l_ref[...] = jnp.sum(lp_ref[...], axis=-1, keepdims=True)

    @pl.when(jnp.logical_not(bounded))
    def _():
        m_ref[...] = jnp.full_like(m_ref, NEG)
        l_ref[...] = jnp.zeros_like(l_ref)
        run(online_step)

    lv = lam_ref[...]
    lam = (jnp.exp(jnp.sum(lv[0:1] * lv[1:2], keepdims=True))
           - jnp.exp(jnp.sum(lv[2:3] * lv[3:4], keepdims=True)) + lambda_init)
    o1 = acc_ref[0:tq, :] / l_ref[0:tq, :]
    o2 = acc_ref[tq:, :] / l_ref[tq:, :]
    o = o1 - lam * o2
    o = o * lax.rsqrt(jnp.mean(o * o, axis=-1, keepdims=True) + SUBLN_EPS) * sg_ref[...]
    o_ref[...] = (o * (1.0 - lambda_init)).astype(o_ref.dtype)


def _diff_attention(qn, kn, v, score_bound, lam_vecs, subln_g, B, S, lambda_init, tq=512):
    W = qn.shape[1]
    H = W // LANES
    q3 = qn.reshape(B, S, W)
    k3 = kn.reshape(B, S, W)
    v3 = v.reshape(B, S, W)
    flag = (score_bound <= MAX_PLAIN_SCORE).astype(I32).reshape(1)
    qblk = pl.BlockSpec((None, tq, LANES), lambda b, h, i, f: (b, i, h))
    kvblk = pl.BlockSpec((None, S, LANES), lambda b, h, i, f: (b, 0, h))
    grid_spec = pltpu.PrefetchScalarGridSpec(
        num_scalar_prefetch=1,
        grid=(B, H, S // tq),
        in_specs=[qblk, kvblk, kvblk,
                  pl.BlockSpec((4, DA_HEAD_DIM), lambda b, h, i, f: (0, 0)),
                  pl.BlockSpec((1, LANES), lambda b, h, i, f: (0, 0))],
        out_specs=qblk,
        scratch_shapes=[pltpu.VMEM((2 * tq, LANES), BF16),
                        pltpu.VMEM((2 * tq, 1), F32),
                        pltpu.VMEM((2 * tq, 1), F32),
                        pltpu.VMEM((2 * tq, LANES), F32),
                        pltpu.VMEM((2 * tq, LANES), F32)],
    )
    out = pl.pallas_call(
        functools.partial(_attn_kernel, tq=tq, lambda_init=lambda_init),
        grid_spec=grid_spec,
        out_shape=jax.ShapeDtypeStruct((B, S, W), BF16),
        compiler_params=_cp("parallel", "parallel", "arbitrary"),
        name="diff_attn",
    )(flag, q3, k3, v3, lam_vecs, subln_g.reshape(1, LANES))
    return out.reshape(B * S, W)


def _rwkv_prep_kernel(p_ref, prev_ref, mu_ref, w0_ref, w2_ref, a0_ref, a2_ref, g2_ref,
                      r_ref, k_ref, v_ref, a_ref, ld_ref, g_ref, *, width):
    i = pl.program_id(1)
    p = p_ref[...]
    last_prev = jnp.where(i > 0, prev_ref[7:8, :], 0.0)
    rowi = lax.broadcasted_iota(I32, p.shape, 0)
    prev = jnp.where(rowi == 0, last_prev, pltpu.roll(p, 1, axis=0))
    xs = p + (prev - p) * mu_ref[...]
    r_ref[...] = xs[:, 0:width]
    k_ref[...] = xs[:, width:2 * width]
    v_ref[...] = xs[:, 2 * width:3 * width]
    xwa = xs[:, 3 * width:3 * width + LANES]
    xg = xs[:, 3 * width + LANES:]
    z = w0_ref[...] + _fdot(jnp.tanh(xwa), w2_ref[...])
    w = -(jnp.maximum(-z, 0.0) + jnp.log(1.0 + jnp.exp(-jnp.abs(z)))) - 0.5
    ld_ref[...] = -jnp.exp(w)
    a_ref[...] = _sigmoid(a0_ref[...] + _fdot(xwa, a2_ref[...]))
    g_ref[...] = _fdot(_sigmoid(xg), g2_ref[...])


def _rwkv_prep(prw3, mu, w_decay0, w_decay2, a0, a2, g2, width, tm=256):
    B, S, C = prw3.shape
    dl, al = w_decay2.shape[0], a2.shape[0]
    assert dl + al == LANES and g2.shape[0] == LANES
    w2p = jnp.zeros((LANES, width), F32).at[:dl].set(w_decay2)
    a2p = jnp.zeros((LANES, width), F32).at[dl:].set(a2)
    vec = lambda n: pl.BlockSpec((1, n), lambda b, i: (0, 0))
    mat = pl.BlockSpec((LANES, width), lambda b, i: (0, 0))
    out = pl.BlockSpec((None, tm, width), lambda b, i: (b, i, 0))
    return pl.pallas_call(
        functools.partial(_rwkv_prep_kernel, width=width),
        grid=(B, S // tm),
        in_specs=[pl.BlockSpec((None, tm, C), lambda b, i: (b, i, 0)),
                  pl.BlockSpec((None, 8, C), lambda b, i: (b, jnp.maximum(i * (tm // 8) - 1, 0), 0)),
                  vec(C), vec(width), mat, vec(width), mat, mat],
        out_specs=[out] * 6,
        out_shape=[jax.ShapeDtypeStruct((B, S, width), F32)] * 6,
        compiler_params=_cp("parallel", "parallel"),
        name="rwkv_prep",
    )(prw3, prw3, mu.reshape(1, C), w_decay0.reshape(1, width), w2p, a0.reshape(1, width), a2p, g2)


def _stackmask(m):
    lane = lax.broadcasted_iota(I32, m.shape, 1)
    z = jnp.zeros_like(m)
    return jnp.concatenate([jnp.where(lane < RWKV_HEAD, m, z), jnp.where(lane >= RWKV_HEAD, m, z)], axis=0)


def _pair_sum(x, first):
    s1 = jnp.sum(jnp.where(first, x, 0.0), axis=-1, keepdims=True)
    s2 = jnp.sum(jnp.where(first, 0.0, x), axis=-1, keepdims=True)
    return jnp.where(first, s1, s2)


def _rwkv_scan_kernel(r_ref, k_ref, v_ref, a_ref, ld_ref, g_ref, kk_ref, ka_ref, rk_ref, lnw_ref, lnb_ref,
                      o_ref, s_ref, *, L):
    tm, W = r_ref.shape
    n_chunks = tm // L
    n_pairs = W // LANES
    hd = RWKV_HEAD
    bf = lambda t: t.astype(BF16)

    @pl.when(pl.program_id(1) == 0)
    def _():
        s_ref[...] = jnp.zeros_like(s_ref)

    row = lax.broadcasted_iota(I32, (tm, tm), 0)
    col = lax.broadcasted_iota(I32, (tm, tm), 1)
    tri = jnp.where(jnp.logical_and(col <= row, (col // L) == (row // L)), 1.0, 0.0).astype(BF16)
    ld = ld_ref[...]
    ld_hi = bf(ld)
    rem = ld - ld_hi.astype(F32)
    ld_mid = bf(rem)
    ld_lo = bf(rem - ld_mid.astype(F32))
    c = (jnp.dot(tri, ld_hi, preferred_element_type=F32) + jnp.dot(tri, ld_mid, preferred_element_type=F32)
         + jnp.dot(tri, ld_lo, preferred_element_type=F32))
    ec = jnp.exp(c)
    eci = jnp.exp(-c)
    ecm = jnp.exp(c - ld)
    r = r_ref[...]
    k = k_ref[...]
    v = v_ref[...]
    a = a_ref[...]
    kkr = k * kk_ref[...]
    kmod = k * (1.0 + (a - 1.0) * ka_ref[...])
    brk = r * kmod * rk_ref[...]

    lane = lax.broadcasted_iota(I32, (L, LANES), 1)
    rowl = lax.broadcasted_iota(I32, (L, LANES), 0)
    first = lane < hd
    lane_h = lane & (hd - 1)
    strict = lane_h < rowl
    incl = lane_h <= rowl
    eye = jnp.where(lane_h == rowl, 1.0, 0.0)

    chains = [(ch, p) for ch in range(n_chunks) for p in range(n_pairs)]
    rsl = lambda ch: slice(ch * L, (ch + 1) * L)
    csl = lambda p: slice(p * LANES, (p + 1) * LANES)
    fdot = lambda x, y: jnp.dot(x, y, preferred_element_type=F32)
    at, bt, kt, rt, vh, g_l = {}, {}, {}, {}, {}, {}
    for c_ in chains:
        ch, p = c_
        rs, cs = rsl(ch), csl(p)
        kkh = kkr[rs, cs]
        kkh = kkh / jnp.maximum(jnp.sqrt(_pair_sum(kkh * kkh, first)), 1e-12)
        vh[c_] = v[rs, cs]
        g_l[c_] = ec[ch * L + L - 1:ch * L + L, cs]
        at[c_] = -kkh * ecm[rs, cs]
        bt[c_] = kkh * a[rs, cs] * eci[rs, cs]
        kt[c_] = kmod[rs, cs] * eci[rs, cs]
        rt[c_] = r[rs, cs] * ec[rs, cs]
    gm = {c_: _nt(bf(jnp.concatenate([at[c_], rt[c_]], axis=0)),
                  jnp.concatenate([_stackmask(bf(bt[c_])), _stackmask(bf(kt[c_]))], axis=0)) for c_ in chains}
    a_ab = {c_: jnp.where(strict, gm[c_][:L, :LANES], 0.0) for c_ in chains}
    vsm = {c_: _stackmask(bf(vh[c_])) for c_ in chains}
    cmat = {c_: fdot(bf(jnp.where(strict, gm[c_][:L, LANES:], 0.0)), vsm[c_]) for c_ in chains}
    t_inv = {c_: eye + a_ab[c_] for c_ in chains}
    pw = {c_: bf(a_ab[c_]) for c_ in chains}
    for _ in range(int(math.log2(L)) - 1):
        pw = {c_: bf(fdot(pw[c_], _stackmask(pw[c_]))) for c_ in chains}
        t_inv = {c_: t_inv[c_] + fdot(pw[c_], _stackmask(bf(t_inv[c_]))) for c_ in chains}
    zz = {c_: fdot(bf(t_inv[c_]), jnp.concatenate([_stackmask(bf(at[c_])), _stackmask(bf(cmat[c_]))], axis=1))
          for c_ in chains}
    qy = {c_: fdot(bf(jnp.where(incl, gm[c_][L:, :LANES], 0.0)),
                   jnp.concatenate([_stackmask(bf(zz[c_][:, :LANES])), _stackmask(bf(zz[c_][:, LANES:]))], axis=1))
          for c_ in chains}
    y0 = {c_: qy[c_][:, LANES:] + fdot(bf(jnp.where(incl, gm[c_][L:, LANES:], 0.0)), vsm[c_]) for c_ in chains}
    qa = {c_: bf(jnp.concatenate([rt[c_] + qy[c_][:, :LANES], zz[c_][:, :LANES]], axis=0)) for c_ in chains}
    bkg = {c_: bf(jnp.concatenate([bt[c_] * g_l[c_], kt[c_] * g_l[c_]], axis=0)) for c_ in chains}

    lane_s = lax.broadcasted_iota(I32, (hd, LANES), 1)
    sp = [s_ref[p] for p in range(n_pairs)]
    for ch in range(n_chunks):
        rs = rsl(ch)
        yw = [_nt(qa[ch, p], _stackmask(bf(sp[p]))) for p in range(n_pairs)]
        upd = [_tn(bf(jnp.concatenate([yw[p][L:] + zz[ch, p][:, LANES:], vh[ch, p]], axis=0)), bkg[ch, p])
               for p in range(n_pairs)]
        for p in range(n_pairs):
            cs = csl(p)
            sp[p] = sp[p] * g_l[ch, p] + jnp.where(lane_s < hd, upd[p][:hd], upd[p][hd:])
            y = yw[p][:L] + y0[ch, p]
            mean = _pair_sum(y, first) * (1.0 / hd)
            yc = y - mean
            var = _pair_sum(yc * yc, first) * (1.0 / hd)
            yn = yc * lax.rsqrt(var + GN_EPS) * lnw_ref[:, cs] + lnb_ref[:, cs]
            bonus = _pair_sum(brk[rs, cs], first) * vh[ch, p]
            o_ref[rs, cs] = ((yn + bonus) * g_ref[rs, cs]).astype(o_ref.dtype)
    for p in range(n_pairs):
        s_ref[p] = sp[p]


def _rwkv_scan(r, k, v, a, ld, g, k_k, k_a, r_k, ln_w, ln_b, L=RWKV_CHUNK, tm=256):
    B, S, W = r.shape
    seq = pl.BlockSpec((None, tm, W), lambda b, c: (b, c, 0))
    vec = pl.BlockSpec((1, W), lambda b, c: (0, 0))
    return pl.pallas_call(
        functools.partial(_rwkv_scan_kernel, L=L),
        grid=(B, S // tm),
        in_specs=[seq] * 6 + [vec] * 5,
        out_specs=seq,
        out_shape=jax.ShapeDtypeStruct((B, S, W), BF16),
        scratch_shapes=[pltpu.VMEM((W // LANES, RWKV_HEAD, LANES), F32)],
        compiler_params=_cp("parallel", "arbitrary"),
        name="rwkv_scan",
    )(r, k, v, a, ld, g, k_k.reshape(1, W), k_a.reshape(1, W), r_k.reshape(1, W),
      ln_w.reshape(1, W), ln_b.reshape(1, W))


def _post_kernel(attn_ref, rw_ref, gate_ref, x_ref, mod_ref, wa_ref, wb_ref, wo_ref, g2_ref, wr_ref,
                 sug_ref, sd_ref, base_ref, h2p_ref, lg_ref):
    D = x_ref.shape[1]
    ya = jnp.dot(attn_ref[...], wa_ref[...], preferred_element_type=F32)
    yb = jnp.dot(rw_ref[...], wb_ref[...], preferred_element_type=F32)
    m = gate_ref[:, 0:D] * ya + gate_ref[:, D:] * yb
    x1 = x_ref[...] + mod_ref[2:3, :] * jnp.dot(m.astype(BF16), wo_ref[...], preferred_element_type=F32)
    y = x1 * lax.rsqrt(jnp.mean(x1 * x1, axis=-1, keepdims=True) + NORM_EPS) * g2_ref[...]
    h2 = y * (1.0 + mod_ref[4:5, :]) + mod_ref[3:4, :]
    lg_ref[...] = _nt(wr_ref[...], h2, precision=HI)
    hb = h2.astype(BF16)
    h2p_ref[...] = _pack_halves(hb.astype(F32))
    F = sd_ref.shape[0]
    gu = jnp.dot(hb, sug_ref[...], preferred_element_type=F32)
    shared = jnp.dot((_silu(gu[:, :F]) * gu[:, F:]).astype(BF16), sd_ref[...], preferred_element_type=F32)
    base_ref[...] = x1 + mod_ref[5:6, :] * shared


def _post(attn, rw, gate, x2, mod, wa, wb, wo, norm2_g, w_router_t, sug, sd, S, tm=256):
    N, D = x2.shape
    E = w_router_t.shape[0]
    tpb = S // tm
    row = lambda w: pl.BlockSpec((tm, w), lambda i: (i, 0))
    full = lambda a: pl.BlockSpec(a.shape, lambda i: (0, 0))
    return pl.pallas_call(
        _post_kernel,
        grid=(N // tm,),
        in_specs=[row(attn.shape[1]), row(rw.shape[1]), row(gate.shape[1]), row(D),
                  pl.BlockSpec((None, 6, D), lambda i: (i // tpb, 0, 0)),
                  full(wa), full(wb), full(wo), pl.BlockSpec((1, D), lambda i: (0, 0)), full(w_router_t),
                  full(sug), full(sd)],
        out_specs=[row(D), row(D // 2), pl.BlockSpec((E, tm), lambda i: (0, i))],
        out_shape=[jax.ShapeDtypeStruct((N, D), F32), jax.ShapeDtypeStruct((N, D // 2), U32),
                   jax.ShapeDtypeStruct((E, N), F32)],
        compiler_params=_cp("parallel"),
        name="post_mixer",
    )(attn, rw, gate, x2, mod, wa, wb, wo, norm2_g.reshape(1, D), w_router_t, sug, sd)


def _first_argmax(x, idx, sentinel):
    m = jnp.max(x, axis=0, keepdims=True)
    return m, jnp.min(jnp.where(x == m, idx, sentinel), axis=0, keepdims=True)


def _route_kernel(lg_ref, bias_ref, e_ref, w_ref):
    E, T = lg_ref.shape
    gsz = E // N_GROUPS
    scores = _sigmoid(lg_ref[...])
    biased = scores + bias_ref[...]
    ig = lax.broadcasted_iota(I32, (gsz, T), 0)
    gs = []
    for g in range(N_GROUPS):
        blk = biased[g * gsz:(g + 1) * gsz, :]
        m1, i1 = _first_argmax(blk, ig, gsz)
        m2 = jnp.max(jnp.where(ig == i1, -jnp.inf, blk), axis=0, keepdims=True)
        gs.append(m1 + m2)
    gsc = jnp.concatenate(gs, axis=0)
    i8 = lax.broadcasted_iota(I32, (N_GROUPS, T), 0)
    chosen = jnp.zeros((N_GROUPS, T), F32)
    for _ in range(TOPK_GROUPS):
        _, gi = _first_argmax(gsc, i8, N_GROUPS)
        hit = i8 == gi
        chosen = jnp.where(hit, 1.0, chosen)
        gsc = jnp.where(hit, -jnp.inf, gsc)
    masked = jnp.concatenate(
        [jnp.where(chosen[g:g + 1, :] > 0.0, biased[g * gsz:(g + 1) * gsz, :], -jnp.inf)
         for g in range(N_GROUPS)], axis=0)
    ie = lax.broadcasted_iota(I32, (E, T), 0)
    idxs, wts = [], []
    for _ in range(TOP_K):
        _, ei = _first_argmax(masked, ie, E)
        hit = ie == ei
        idxs.append(ei)
        wts.append(jnp.sum(jnp.where(hit, scores, 0.0), axis=0, keepdims=True))
        masked = jnp.where(hit, -jnp.inf, masked)
    wt = jnp.concatenate(wts, axis=0)
    e_ref[...] = jnp.concatenate(idxs, axis=0)
    w_ref[...] = wt / jnp.sum(wt, axis=0, keepdims=True) * ROUTED_SCALE


def _route(logits_t, router_bias, T=512):
    E, N = logits_t.shape
    blk = pl.BlockSpec((TOP_K, T), lambda i: (0, i))
    return pl.pallas_call(
        _route_kernel,
        grid=(N // T,),
        in_specs=[pl.BlockSpec((E, T), lambda i: (0, i)), pl.BlockSpec((E, 1), lambda i: (0, 0))],
        out_specs=[blk, blk],
        out_shape=[jax.ShapeDtypeStruct((TOP_K, N), I32), jax.ShapeDtypeStruct((TOP_K, N), F32)],
        compiler_params=_cp("parallel"),
        name="route",
    )(logits_t, router_bias.reshape(E, 1))


def _rank_kernel(e_ref, rank_ref, cnt_ref, carry_ref, *, n_experts):
    T = e_ref.shape[1]

    @pl.when(pl.program_id(0) == 0)
    def _():
        carry_ref[...] = jnp.zeros_like(carry_ref)

    ie = lax.broadcasted_iota(I32, (n_experts, T), 0)
    e = e_ref[...]
    hits = [ie == e[kk:kk + 1, :] for kk in range(TOP_K)]
    onehot = jnp.zeros((n_experts, T), F32)
    for hsel in hits:
        onehot = onehot + hsel.astype(F32)
    tr = lax.broadcasted_iota(I32, (T, T), 0)
    tc = lax.broadcasted_iota(I32, (T, T), 1)
    before = (tr < tc).astype(BF16)
    base = _bdot(onehot, before) + carry_ref[:, 0:1]
    rank_ref[...] = jnp.concatenate(
        [jnp.sum(jnp.where(hsel, base, 0.0), axis=0, keepdims=True) for hsel in hits], axis=0).astype(I32)
    carry_ref[...] = carry_ref[...] + jnp.sum(onehot, axis=1, keepdims=True)
    cnt_ref[...] = carry_ref[...]


def _ranks(eidx_t, n_experts, T=512):
    N = eidx_t.shape[1]
    blk = pl.BlockSpec((TOP_K, T), lambda i: (0, i))
    cnt = pl.BlockSpec((n_experts, LANES), lambda i: (0, 0))
    return pl.pallas_call(
        functools.partial(_rank_kernel, n_experts=n_experts),
        grid=(N // T,),
        in_specs=[blk],
        out_specs=[blk, cnt],
        out_shape=[jax.ShapeDtypeStruct((TOP_K, N), I32), jax.ShapeDtypeStruct((n_experts, LANES), F32)],
        scratch_shapes=[pltpu.VMEM((n_experts, LANES), F32)],
        compiler_params=_cp("arbitrary"),
        name="ranks",
    )(eidx_t)


def _dest_kernel(e_ref, rank_ref, ustart_ref, pstart_ref, du_ref, dp_ref):
    E = ustart_ref.shape[0]
    T = e_ref.shape[1]
    ie = lax.broadcasted_iota(I32, (E, T), 0)
    e = e_ref[...]
    rank = rank_ref[...]
    for start_ref, d_ref in ((ustart_ref, du_ref), (pstart_ref, dp_ref)):
        start = start_ref[:, 0:1]
        rows = [jnp.sum(jnp.where(ie == e[kk:kk + 1, :], start, 0.0), axis=0, keepdims=True)
                for kk in range(TOP_K)]
        d_ref[...] = jnp.concatenate(rows, axis=0).astype(I32) + rank


def _dests(eidx_t, rank_t, ustart, pstart, T=512):
    N = eidx_t.shape[1]
    E = ustart.shape[0]
    blk = pl.BlockSpec((TOP_K, T), lambda i: (0, i))
    tab = pl.BlockSpec((E, LANES), lambda i: (0, 0))
    lanes = lambda a: jnp.broadcast_to(a.astype(F32)[:, None], (E, LANES))
    return pl.pallas_call(
        _dest_kernel,
        grid=(N // T,),
        in_specs=[blk, blk, tab, tab],
        out_specs=[blk, blk],
        out_shape=[jax.ShapeDtypeStruct((TOP_K, N), I32)] * 2,
        compiler_params=_cp("parallel"),
        name="dests",
    )(eidx_t, rank_t, lanes(ustart), lanes(pstart))


def _rowtable_kernel(d_ref, tab_ref, *, n_pairs):
    T = d_ref.shape[1]
    t0 = pl.program_id(0) * T

    def body(t, carry):
        for kk in range(TOP_K):
            tab_ref[d_ref[kk, t]] = t0 + t
        return carry

    lax.fori_loop(0, T, body, 0, unroll=4)

    @pl.when(pl.program_id(0) == pl.num_programs(0) - 1)
    def _():
        def tail(j, carry):
            tab_ref[n_pairs + j] = 0
            return carry
        lax.fori_loop(0, tab_ref.shape[0] - n_pairs, tail, 0)


def _rowtable(dest_u, slack, T=512):
    N = dest_u.shape[1]
    return pl.pallas_call(
        functools.partial(_rowtable_kernel, n_pairs=N * TOP_K),
        grid=(N // T,),
        in_specs=[pl.BlockSpec((TOP_K, T), lambda i: (0, i), memory_space=pltpu.SMEM)],
        out_specs=pl.BlockSpec(memory_space=pltpu.SMEM),
        out_shape=jax.ShapeDtypeStruct((N * TOP_K + slack,), I32),
        compiler_params=_cp("arbitrary"),
        name="rowtable",
    )(dest_u)


def _expert_kernel(us_ref, ps_ref, tab_ref, h_hbm, wug_ref, wd_ref, y_hbm,
                   hres, wug_bf, wd_bf, xg, ybuf, cnt_ref, hsem, ysem, *, n_rows):
    e = pl.program_id(0)
    _, G, _, C = xg.shape
    R = G * 8
    F = wd_ref.shape[0]

    def gather(slot, row0):
        for g in range(G):
            for j in range(8):
                xg[slot, g, j:j + 1, :] = hres[pl.ds(tab_ref[row0 + (g * 8 + j)], 1), :]

    def y_copy(slot, row):
        return pltpu.make_async_copy(ybuf.at[slot], y_hbm.at[pl.ds(row, R)], ysem.at[slot])

    @pl.when(e == 0)
    def _():
        cp = pltpu.make_async_copy(h_hbm, hres, hsem)
        cp.start()
        cp.wait()
        cnt_ref[0] = 0
        gather(0, 0)
        ybuf[...] = jnp.zeros_like(ybuf)
        for slot in range(2):
            y_copy(slot, n_rows + slot * R).start()

    wug_bf[...] = wug_ref[...].astype(BF16)
    wd_bf[...] = wd_ref[...].astype(BF16)
    base = us_ref[e]
    end = us_ref[e + 1]
    pbase = ps_ref[e]
    n_tiles = lax.shift_right_logical(end - base + (R - 1), int(math.log2(R)))

    def tile(i, carry):
        n_done = cnt_ref[0]
        slot = n_done & 1
        gather(1 - slot, jnp.where(i + 1 < n_tiles, base + (i + 1) * R, end))
        lo, hi = _unpack_halves(xg[slot].reshape(R, C))
        gu = (jnp.dot(lo.astype(BF16), wug_bf[0:C, :], preferred_element_type=F32)
              + jnp.dot(hi.astype(BF16), wug_bf[C:, :], preferred_element_type=F32))
        hid = (_silu(gu[:, :F]) * gu[:, F:]).astype(BF16)
        y = jnp.dot(hid, wd_bf[...], preferred_element_type=F32)
        y_copy(slot, 0).wait()
        ybuf[slot] = _pack_halves(y.astype(BF16).astype(F32))
        y_copy(slot, pl.multiple_of(pbase + i * R, R)).start()
        cnt_ref[0] = n_done + 1
        return carry

    lax.fori_loop(0, n_tiles, tile, 0)

    @pl.when(e == pl.num_programs(0) - 1)
    def _():
        for slot in range(2):
            y_copy(slot, 0).wait()
        ybuf[0] = jnp.zeros((R, C), U32)
        first = lax.shift_right_logical(pbase + n_tiles * R, int(math.log2(R)))
        n_left = n_rows // R - first

        def fill(t, carry):
            y_copy(0, pl.multiple_of((first + t) * R, R)).start()
            return carry

        def drain(t, carry):
            y_copy(0, 0).wait()
            return carry

        lax.fori_loop(0, n_left, fill, 0)
        lax.fori_loop(0, n_left, drain, 0)


def _experts(ustart, pstart, table, h2p, w_ug, w_d, n_rows, R):
    N, C = h2p.shape
    E, D, F2 = w_ug.shape
    F = w_d.shape[1]
    grid_spec = pltpu.PrefetchScalarGridSpec(
        num_scalar_prefetch=3,
        grid=(E,),
        in_specs=[pl.BlockSpec(memory_space=pl.ANY),
                  pl.BlockSpec((None, D, F2), lambda e, us, ps, tab: (e, 0, 0)),
                  pl.BlockSpec((None, F, D), lambda e, us, ps, tab: (e, 0, 0))],
        out_specs=pl.BlockSpec(memory_space=pl.ANY),
        scratch_shapes=[pltpu.VMEM((N, C), U32), pltpu.VMEM((D, F2), BF16), pltpu.VMEM((F, D), BF16),
                        pltpu.VMEM((2, R // 8, 8, C), U32), pltpu.VMEM((2, R, C), U32), pltpu.SMEM((1,), I32),
                        pltpu.SemaphoreType.DMA, pltpu.SemaphoreType.DMA((2,))],
    )
    return pl.pallas_call(
        functools.partial(_expert_kernel, n_rows=n_rows),
        grid_spec=grid_spec,
        out_shape=jax.ShapeDtypeStruct((n_rows + 2 * R, C), U32),
        compiler_params=_cp("arbitrary"),
        name="experts",
    )(ustart, pstart, table, h2p, w_ug, w_d)


def _combine_kernel(dcur_ref, dnext_ref, y_hbm, w_ref, base_ref, mod_ref, o_ref, buf, sem):
    i = pl.program_id(0)
    T = base_ref.shape[0]

    def issue(d_ref, slot):
        def body(t, carry):
            for kk in range(TOP_K):
                pltpu.make_async_copy(y_hbm.at[pl.ds(d_ref[kk, t], 1)], buf.at[slot, kk, pl.ds(t, 1)],
                                      sem.at[slot]).start()
            return carry
        lax.fori_loop(0, T, body, 0)

    slot = i & 1

    @pl.when(i == 0)
    def _():
        issue(dcur_ref, 0)

    @pl.when(i + 1 < pl.num_programs(0))
    def _():
        issue(dnext_ref, 1 - slot)

    tr = lax.broadcasted_iota(I32, (T, T), 0)
    tc = lax.broadcasted_iota(I32, (T, T), 1)
    wcol = _nt((tr == tc).astype(F32), w_ref[...], precision=HI)
    for kk in range(TOP_K):
        pltpu.make_async_copy(y_hbm.at[pl.ds(0, T)], buf.at[slot, kk], sem.at[slot]).wait()
    acc_lo = acc_hi = None
    for kk in range(TOP_K):
        lo, hi = _unpack_halves(buf[slot, kk])
        wk = wcol[:, kk:kk + 1]
        acc_lo = lo * wk if acc_lo is None else acc_lo + lo * wk
        acc_hi = hi * wk if acc_hi is None else acc_hi + hi * wk
    routed = jnp.concatenate([acc_lo, acc_hi], axis=1)
    o_ref[...] = base_ref[...] + mod_ref[5:6, :] * routed


def _combine(dest_p, y, w_t, base, mod, S, T=256):
    N, D = base.shape
    C = y.shape[1]
    tpb = S // T
    n_tiles = N // T
    row = pl.BlockSpec((T, D), lambda i: (i, 0))
    return pl.pallas_call(
        _combine_kernel,
        grid=(n_tiles,),
        in_specs=[pl.BlockSpec((TOP_K, T), lambda i: (0, i), memory_space=pltpu.SMEM),
                  pl.BlockSpec((TOP_K, T), lambda i: (0, jnp.minimum(i + 1, n_tiles - 1)),
                               memory_space=pltpu.SMEM),
                  pl.BlockSpec(memory_space=pl.ANY),
                  pl.BlockSpec((TOP_K, T), lambda i: (0, i)),
                  row,
                  pl.BlockSpec((None, 6, D), lambda i: (i // tpb, 0, 0))],
        out_specs=row,
        out_shape=jax.ShapeDtypeStruct((N, D), F32),
        scratch_shapes=[pltpu.VMEM((2, TOP_K, T, C), U32), pltpu.SemaphoreType.DMA((2,))],
        compiler_params=_cp("arbitrary"),
        name="combine",
    )(dest_p, dest_p, y, w_t, base, mod)


def _layer(x, c, positions, layer_idx, w_ada, b_ada, norm1_g, w_in, w_gate, b_gate,
           q_norm_g, k_norm_g, lambda_q1, lambda_k1, lambda_q2, lambda_k2, subln_g,
           rwkv_mu, w_decay0, w_decay2, a0, a2, g2, k_k, k_a, r_k, ln_x_w, ln_x_b,
           w_branch_a, w_branch_b, w_out, norm2_g, w_router, router_bias,
           w_expert_up_gate, w_expert_down, w_shared_up_gate, w_shared_down):
    B, S, D = x.shape
    N = B * S
    E = w_router.shape[1]
    da_width = w_branch_a.shape[0]
    rw_width = w_branch_b.shape[0]
    rw_cols = rwkv_mu.shape[0]
    lambda_init = 0.8 - 0.6 * math.exp(-0.3 * layer_idx)

    mod = _adaln(c, w_ada, b_ada)
    x2 = x.reshape(N, D)
    w_cat = jnp.concatenate([w_in, w_gate], axis=1).astype(BF16)
    q, k, v, prw, gate = _inproj(x2, mod, norm1_g, w_cat, b_gate, S, da_width, rw_cols)

    qn, kn = _qkprep(q, k, positions.reshape(N, 1), q_norm_g, k_norm_g)
    lam_vecs = jnp.stack([lambda_q1, lambda_k1, lambda_q2, lambda_k2])
    score_bound = 1.01 * DA_HEAD_DIM ** 0.5 * jnp.max(jnp.abs(q_norm_g)) * jnp.max(jnp.abs(k_norm_g))
    attn = _diff_attention(qn, kn, v, score_bound, lam_vecs, subln_g, B, S, lambda_init)

    r_, k_, v_, a_, ld_, g_ = _rwkv_prep(prw.reshape(B, S, rw_cols), rwkv_mu, w_decay0, w_decay2,
                                         a0, a2, g2, rw_width)
    rw = _rwkv_scan(r_, k_, v_, a_, ld_, g_, k_k, k_a, r_k.reshape(-1), ln_x_w, ln_x_b).reshape(N, rw_width)

    base, h2p, logits_t = _post(attn, rw, gate, x2, mod, w_branch_a.astype(BF16), w_branch_b.astype(BF16),
                                w_out.astype(BF16), norm2_g, w_router.T,
                                w_shared_up_gate.astype(BF16), w_shared_down.astype(BF16), S)

    eidx_t, w_t = _route(logits_t, router_bias)
    rank_t, counts = _ranks(eidx_t, E)
    R = EXPERT_TILE
    cnt = counts[:, 0].astype(I32)
    uends = jnp.cumsum(cnt)
    pcnt = (cnt + R - 1) // R * R
    pstart = jnp.cumsum(pcnt) - pcnt
    dest_u, dest_p = _dests(eidx_t, rank_t, uends - cnt, pstart)
    table = _rowtable(dest_u, R)
    ustart = jnp.concatenate([jnp.zeros((1,), I32), uends])
    n_rows = (N * TOP_K + E * (R - 1) + R - 1) // R * R
    y = _experts(ustart, pstart, table, h2p, w_expert_up_gate, w_expert_down, n_rows, R)
    out = _combine(dest_p, y, w_t, base, mod, S)
    return out.reshape(B, S, D)


def kernel(x, c, positions, w_ada, b_ada, norm1_g, w_in, w_gate, b_gate, q_norm_g, k_norm_g, lambda_q1, lambda_k1, lambda_q2, lambda_k2, subln_g, rwkv_mu, w_decay0, w_decay2, a0, a2, g2, k_k, k_a, r_k, ln_x_w, ln_x_b, w_branch_a, w_branch_b, w_out, norm2_g, w_router, router_bias, w_expert_up_gate, w_expert_down, w_shared_up_gate, w_shared_down):
    for l in range(w_ada.shape[0]):
        x = _layer(x, c, positions, l, w_ada[l], b_ada[l], norm1_g[l], w_in[l], w_gate[l], b_gate[l],
                   q_norm_g[l], k_norm_g[l], lambda_q1[l], lambda_k1[l], lambda_q2[l], lambda_k2[l],
                   subln_g[l], rwkv_mu[l], w_decay0[l], w_decay2[l], a0[l], a2[l], g2[l], k_k[l],
                   k_a[l], r_k[l], ln_x_w[l], ln_x_b[l], w_branch_a[l], w_branch_b[l], w_out[l],
                   norm2_g[l], w_router[l], router_bias[l], w_expert_up_gate[l], w_expert_down[l],
                   w_shared_up_gate[l], w_shared_down[l])
    return x
```

```python
import functools
import math

import jax
import jax.numpy as jnp
from jax import lax
from jax.experimental import pallas as pl
from jax.experimental.pallas import tpu as pltpu
from jax.experimental.pallas import tpu_sc as plsc

F32 = jnp.float32
BF16 = jnp.bfloat16
I32 = jnp.int32
U32 = jnp.uint32
HI = lax.Precision.HIGHEST

CHUNK = 64
ROPE_THETA = 10000.0
NORM_EPS = 1e-6
SUBLN_EPS = 1e-5
DA_HEAD_DIM = 64
RWKV_HEAD = 64
GN_EPS = 64e-5
TOP_K = 8
N_GROUPS = 8
TOPK_GROUPS = 4
ROUTED_SCALE = 2.5
EXPERT_TILE = 256
RWKV_CHUNK = 64
LANES = 128
SC_WINDOW = 128
NEG = -1e30
MAX_PLAIN_SCORE = 40.0
VMEM_LIMIT = 56 * 1024 * 1024


def _cp(*sem):
    return pltpu.CompilerParams(dimension_semantics=sem, vmem_limit_bytes=VMEM_LIMIT)


def _bdot(a, b):
    return jnp.dot(a.astype(BF16), b.astype(BF16), preferred_element_type=F32)


def _fdot(a, b):
    return jnp.dot(a, b, precision=HI, preferred_element_type=F32)


def _nt(a, b, precision=None):
    return lax.dot_general(a, b, (((1,), (1,)), ((), ())), precision=precision,
                           preferred_element_type=F32)


def _tn(a, b, precision=None):
    return lax.dot_general(a, b, (((0,), (0,)), ((), ())), precision=precision,
                           preferred_element_type=F32)


def _pack_halves(x):
    c = x.shape[1] // 2
    lo = lax.bitcast_convert_type(x[:, :c], U32)
    hi = lax.bitcast_convert_type(x[:, c:], U32)
    return (hi & jnp.uint32(0xFFFF0000)) | (lo >> 16)


def _unpack_halves(w):
    lo = lax.bitcast_convert_type(w << 16, F32)
    hi = lax.bitcast_convert_type(w & jnp.uint32(0xFFFF0000), F32)
    return lo, hi


def _sigmoid(x):
    return 1.0 / (1.0 + jnp.exp(-x))


def _silu(x):
    return x * _sigmoid(x)


def _ada_kernel(c_ref, w_ref, b_ref, o_ref):
    o_ref[...] = _fdot(_silu(c_ref[...]), w_ref[...]) + b_ref[...]


def _adaln(c, w_ada, b_ada):
    B, D = c.shape
    rows = -(-B // 8) * 8
    cpad = jnp.zeros((rows, D), F32).at[:B].set(c)
    n_out = w_ada.shape[1]
    out = pl.pallas_call(
        _ada_kernel,
        grid=(n_out // D,),
        in_specs=[pl.BlockSpec((rows, D), lambda j: (0, 0)),
                  pl.BlockSpec((D, D), lambda j: (0, j)),
                  pl.BlockSpec((1, D), lambda j: (0, j))],
        out_specs=pl.BlockSpec((rows, D), lambda j: (0, j)),
        out_shape=jax.ShapeDtypeStruct((rows, n_out), F32),
        compiler_params=_cp("arbitrary"),
        name="adaln",
    )(cpad, w_ada, b_ada.reshape(1, n_out))
    return out[:B].reshape(B, n_out // D, D)


def _inproj_kernel(x_ref, mod_ref, g_ref, w_ref, bg_ref, q_ref, k_ref, v_ref, rw_ref, gate_ref,
                   *, da_width, rw_cols):
    x = x_ref[...]
    y = x * lax.rsqrt(jnp.mean(x * x, axis=-1, keepdims=True) + NORM_EPS) * g_ref[...]
    h = (y * (1.0 + mod_ref[1:2, :]) + mod_ref[0:1, :]).astype(BF16)
    c = 0
    for ref in (q_ref, k_ref, v_ref):
        ref[...] = jnp.dot(h, w_ref[:, c:c + da_width], preferred_element_type=F32).astype(ref.dtype)
        c += da_width
    step = 512
    for o in range(0, rw_cols, step):
        wd = min(step, rw_cols - o)
        rw_ref[:, o:o + wd] = jnp.dot(h, w_ref[:, c + o:c + o + wd], preferred_element_type=F32)
    c += rw_cols
    n_gate = gate_ref.shape[1]
    for o in range(0, n_gate, step):
        z = jnp.dot(h, w_ref[:, c + o:c + o + step], preferred_element_type=F32) + bg_ref[:, o:o + step]
        gate_ref[:, o:o + step] = _sigmoid(z)


def _inproj(x2, mod, norm1_g, w_cat, b_gate, S, da_width, rw_cols, tm=256):
    N, D = x2.shape
    n_gate = b_gate.shape[0]
    tpb = S // tm
    kern = functools.partial(_inproj_kernel, da_width=da_width, rw_cols=rw_cols)
    row = lambda w: pl.BlockSpec((tm, w), lambda i: (i, 0))
    return pl.pallas_call(
        kern,
        grid=(N // tm,),
        in_specs=[row(D),
                  pl.BlockSpec((None, 6, D), lambda i: (i // tpb, 0, 0)),
                  pl.BlockSpec((1, D), lambda i: (0, 0)),
                  pl.BlockSpec(w_cat.shape, lambda i: (0, 0)),
                  pl.BlockSpec((1, n_gate), lambda i: (0, 0))],
        out_specs=[row(da_width), row(da_width), row(da_width), row(rw_cols), row(n_gate)],
        out_shape=[jax.ShapeDtypeStruct((N, da_width), F32),
                   jax.ShapeDtypeStruct((N, da_width), F32),
                   jax.ShapeDtypeStruct((N, da_width), BF16),
                   jax.ShapeDtypeStruct((N, rw_cols), F32),
                   jax.ShapeDtypeStruct((N, n_gate), F32)],
        compiler_params=_cp("parallel"),
        name="inproj",
    )(x2, mod, norm1_g.reshape(1, D), w_cat, b_gate.reshape(1, n_gate))


def _qkprep_kernel(q_ref, k_ref, pos_ref, invf_ref, qg_ref, kg_ref, qo_ref, ko_ref, *, scale):
    tm = q_ref.shape[0]
    lane = lax.broadcasted_iota(I32, (tm, LANES), 1)
    first = lane < DA_HEAD_DIM
    lo_half = (lane & (DA_HEAD_DIM - 1)) < DA_HEAD_DIM // 2
    ang = pos_ref[...].astype(F32) * invf_ref[...]
    cos = jnp.cos(ang)
    sin = jnp.sin(ang)
    sin = jnp.where(lo_half, -sin, sin)

    def one(src, dst, g_ref, mult):
        for blk in range(src.shape[1] // LANES):
            x = src[:, blk * LANES:(blk + 1) * LANES]
            xx = x * x
            s_all = jnp.sum(xx, axis=-1, keepdims=True)
            s_first = jnp.sum(jnp.where(first, xx, 0.0), axis=-1, keepdims=True)
            ms = jnp.where(first, s_first, s_all - s_first) * (1.0 / DA_HEAD_DIM)
            xn = x * lax.rsqrt(ms + NORM_EPS) * g_ref[...]
            rot = jnp.where(lo_half, pltpu.roll(xn, LANES - DA_HEAD_DIM // 2, axis=1),
                            pltpu.roll(xn, DA_HEAD_DIM // 2, axis=1))
            dst[:, blk * LANES:(blk + 1) * LANES] = ((xn * cos + rot * sin) * mult).astype(dst.dtype)

    one(q_ref, qo_ref, qg_ref, scale)
    one(k_ref, ko_ref, kg_ref, 1.0)


def _qkprep(q, k, pos2, q_norm_g, k_norm_g, tm=512):
    N, W = q.shape
    d = DA_HEAD_DIM
    inv_freq = 1.0 / (ROPE_THETA ** (jnp.arange(0, d, 2, dtype=F32) / d))
    invf = jnp.tile(inv_freq, LANES // (d // 2)).reshape(1, LANES)
    row = pl.BlockSpec((tm, W), lambda i: (i, 0))
    vec = pl.BlockSpec((1, LANES), lambda i: (0, 0))
    return pl.pallas_call(
        functools.partial(_qkprep_kernel, scale=d ** -0.5 * math.log2(math.e)),
        grid=(N // tm,),
        in_specs=[row, row, pl.BlockSpec((tm, 1), lambda i: (i, 0)), vec, vec, vec],
        out_specs=[row, row],
        out_shape=[jax.ShapeDtypeStruct((N, W), BF16)] * 2,
        compiler_params=_cp("parallel"),
        name="qkprep",
    )(q, k, pos2, invf, jnp.tile(q_norm_g, 2).reshape(1, LANES), jnp.tile(k_norm_g, 2).reshape(1, LANES))


def _attn_kernel(flag_ref, q_ref, k_ref, v_ref, lam_ref, sg_ref, o_ref, qz_ref, m_ref, l_ref, lp_ref, acc_ref,
                 *, tq, lambda_init):
    i = pl.program_id(2)
    lane = lax.broadcasted_iota(I32, (tq, LANES), 1)
    q = q_ref[...]
    zero = jnp.zeros_like(q)
    qz_ref[0:tq, :] = jnp.where(lane < DA_HEAD_DIM, q, zero)
    qz_ref[tq:, :] = jnp.where(lane >= DA_HEAD_DIM, q, zero)
    acc_ref[...] = jnp.zeros_like(acc_ref)
    bounded = flag_ref[0] == 1

    def scores(j, masked):
        off = pl.multiple_of(j * tq, tq)
        s = _nt(qz_ref[...], k_ref[pl.ds(off, tq), :])
        if masked:
            row = lax.broadcasted_iota(I32, s.shape, 0)
            col = lax.broadcasted_iota(I32, s.shape, 1)
            s = jnp.where((col // CHUNK) <= ((row & (tq - 1)) // CHUNK), s, NEG)
        return s, off

    def plain_step(j, masked):
        s, off = scores(j, masked)
        pr = jnp.exp2(s)
        part = pr[:, 0:LANES]
        for cblk in range(1, tq // LANES):
            part = part + pr[:, cblk * LANES:(cblk + 1) * LANES]
        lp_ref[...] += part
        acc_ref[...] += jnp.dot(pr.astype(BF16), v_ref[pl.ds(off, tq), :], preferred_element_type=F32)

    def online_step(j, masked):
        s, off = scores(j, masked)
        m_old = m_ref[...]
        m_new = jnp.maximum(m_old, jnp.max(s, axis=-1, keepdims=True))
        alpha = jnp.exp2(m_old - m_new)
        pr = jnp.exp2(s - m_new)
        l_ref[...] = alpha * l_ref[...] + jnp.sum(pr, axis=-1, keepdims=True)
        acc_ref[...] = alpha * acc_ref[...] + jnp.dot(pr.astype(BF16), v_ref[pl.ds(off, tq), :],
                                                      preferred_element_type=F32)
        m_ref[...] = m_new

    def run(step):
        def body(j, carry):
            step(j, False)
            return carry
        lax.fori_loop(0, i, body, 0)
        step(i, True)

    @pl.when(bounded)
    def _():
        lp_ref[...] = jnp.zeros_like(lp_ref)
        run(plain_step)
        l_ref[...] = jnp.sum(lp_ref[...], axis=-1, keepdims=True)

    @pl.when(jnp.logical_not(bounded))
    def _():
        m_ref[...] = jnp.full_like(m_ref, NEG)
        l_ref[...] = jnp.zeros_like(l_ref)
        run(online_step)

    lv = lam_ref[...]
    lam = (jnp.exp(jnp.sum(lv[0:1] * lv[1:2], keepdims=True))
           - jnp.exp(jnp.sum(lv[2:3] * lv[3:4], keepdims=True)) + lambda_init)
    o1 = acc_ref[0:tq, :] / l_ref[0:tq, :]
    o2 = acc_ref[tq:, :] / l_ref[tq:, :]
    o = o1 - lam * o2
    o = o * lax.rsqrt(jnp.mean(o * o, axis=-1, keepdims=True) + SUBLN_EPS) * sg_ref[...]
    o_ref[...] = (o * (1.0 - lambda_init)).astype(o_ref.dtype)


def _diff_attention(qn, kn, v, score_bound, lam_vecs, subln_g, B, S, lambda_init, tq=512):
    W = qn.shape[1]
    H = W // LANES
    q3 = qn.reshape(B, S, W)
    k3 = kn.reshape(B, S, W)
    v3 = v.reshape(B, S, W)
    flag = (score_bound <= MAX_PLAIN_SCORE).astype(I32).reshape(1)
    qblk = pl.BlockSpec((None, tq, LANES), lambda b, h, i, f: (b, i, h))
    kvblk = pl.BlockSpec((None, S, LANES), lambda b, h, i, f: (b, 0, h))
    grid_spec = pltpu.PrefetchScalarGridSpec(
        num_scalar_prefetch=1,
        grid=(B, H, S // tq),
        in_specs=[qblk, kvblk, kvblk,
                  pl.BlockSpec((4, DA_HEAD_DIM), lambda b, h, i, f: (0, 0)),
                  pl.BlockSpec((1, LANES), lambda b, h, i, f: (0, 0))],
        out_specs=qblk,
        scratch_shapes=[pltpu.VMEM((2 * tq, LANES), BF16),
                        pltpu.VMEM((2 * tq, 1), F32),
                        pltpu.VMEM((2 * tq, 1), F32),
                        pltpu.VMEM((2 * tq, LANES), F32),
                        pltpu.VMEM((2 * tq, LANES), F32)],
    )
    out = pl.pallas_call(
        functools.partial(_attn_kernel, tq=tq, lambda_init=lambda_init),
        grid_spec=grid_spec,
        out_shape=jax.ShapeDtypeStruct((B, S, W), BF16),
        compiler_params=_cp("parallel", "parallel", "arbitrary"),
        name="diff_attn",
    )(flag, q3, k3, v3, lam_vecs, subln_g.reshape(1, LANES))
    return out.reshape(B * S, W)


def _rwkv_prep_kernel(p_ref, prev_ref, mu_ref, w0_ref, w2_ref, a0_ref, a2_ref, g2_ref,
                      r_ref, k_ref, v_ref, a_ref, ld_ref, g_ref, *, width):
    i = pl.program_id(1)
    p = p_ref[...]
    last_prev = jnp.where(i > 0, prev_ref[7:8, :], 0.0)
    rowi = lax.broadcasted_iota(I32, p.shape, 0)
    prev = jnp.where(rowi == 0, last_prev, pltpu.roll(p, 1, axis=0))
    xs = p + (prev - p) * mu_ref[...]
    r_ref[...] = xs[:, 0:width]
    k_ref[...] = xs[:, width:2 * width]
    v_ref[...] = xs[:, 2 * width:3 * width]
    xwa = xs[:, 3 * width:3 * width + LANES]
    xg = xs[:, 3 * width + LANES:]
    z = w0_ref[...] + _fdot(jnp.tanh(xwa), w2_ref[...])
    w = -(jnp.maximum(-z, 0.0) + jnp.log(1.0 + jnp.exp(-jnp.abs(z)))) - 0.5
    ld_ref[...] = -jnp.exp(w)
    a_ref[...] = _sigmoid(a0_ref[...] + _fdot(xwa, a2_ref[...]))
    g_ref[...] = _fdot(_sigmoid(xg), g2_ref[...])


def _rwkv_prep(prw3, mu, w_decay0, w_decay2, a0, a2, g2, width, tm=256):
    B, S, C = prw3.shape
    dl, al = w_decay2.shape[0], a2.shape[0]
    assert dl + al == LANES and g2.shape[0] == LANES
    w2p = jnp.zeros((LANES, width), F32).at[:dl].set(w_decay2)
    a2p = jnp.zeros((LANES, width), F32).at[dl:].set(a2)
    vec = lambda n: pl.BlockSpec((1, n), lambda b, i: (0, 0))
    mat = pl.BlockSpec((LANES, width), lambda b, i: (0, 0))
    out = pl.BlockSpec((None, tm, width), lambda b, i: (b, i, 0))
    return pl.pallas_call(
        functools.partial(_rwkv_prep_kernel, width=width),
        grid=(B, S // tm),
        in_specs=[pl.BlockSpec((None, tm, C), lambda b, i: (b, i, 0)),
                  pl.BlockSpec((None, 8, C), lambda b, i: (b, jnp.maximum(i * (tm // 8) - 1, 0), 0)),
                  vec(C), vec(width), mat, vec(width), mat, mat],
        out_specs=[out] * 6,
        out_shape=[jax.ShapeDtypeStruct((B, S, width), F32)] * 6,
        compiler_params=_cp("parallel", "parallel"),
        name="rwkv_prep",
    )(prw3, prw3, mu.reshape(1, C), w_decay0.reshape(1, width), w2p, a0.reshape(1, width), a2p, g2)


def _stackmask(m):
    lane = lax.broadcasted_iota(I32, m.shape, 1)
    z = jnp.zeros_like(m)
    return jnp.concatenate([jnp.where(lane < RWKV_HEAD, m, z), jnp.where(lane >= RWKV_HEAD, m, z)], axis=0)


def _pair_sum(x, first):
    s1 = jnp.sum(jnp.where(first, x, 0.0), axis=-1, keepdims=True)
    s2 = jnp.sum(jnp.where(first, 0.0, x), axis=-1, keepdims=True)
    return jnp.where(first, s1, s2)


def _rwkv_scan_kernel(r_ref, k_ref, v_ref, a_ref, ld_ref, g_ref, kk_ref, ka_ref, rk_ref, lnw_ref, lnb_ref,
                      o_ref, s_ref, *, L):
    tm, W = r_ref.shape
    n_chunks = tm // L
    n_pairs = W // LANES
    hd = RWKV_HEAD
    bf = lambda t: t.astype(BF16)

    @pl.when(pl.program_id(1) == 0)
    def _():
        s_ref[...] = jnp.zeros_like(s_ref)

    row = lax.broadcasted_iota(I32, (tm, tm), 0)
    col = lax.broadcasted_iota(I32, (tm, tm), 1)
    tri = jnp.where(jnp.logical_and(col <= row, (col // L) == (row // L)), 1.0, 0.0).astype(BF16)
    ld = ld_ref[...]
    ld_hi = bf(ld)
    rem = ld - ld_hi.astype(F32)
    ld_mid = bf(rem)
    ld_lo = bf(rem - ld_mid.astype(F32))
    c = (jnp.dot(tri, ld_hi, preferred_element_type=F32) + jnp.dot(tri, ld_mid, preferred_element_type=F32)
         + jnp.dot(tri, ld_lo, preferred_element_type=F32))
    ec = jnp.exp(c)
    eci = jnp.exp(-c)
    ecm = jnp.exp(c - ld)
    r = r_ref[...]
    k = k_ref[...]
    v = v_ref[...]
    a = a_ref[...]
    kkr = k * kk_ref[...]
    kmod = k * (1.0 + (a - 1.0) * ka_ref[...])
    brk = r * kmod * rk_ref[...]

    lane = lax.broadcasted_iota(I32, (L, LANES), 1)
    rowl = lax.broadcasted_iota(I32, (L, LANES), 0)
    first = lane < hd
    lane_h = lane & (hd - 1)
    strict = lane_h < rowl
    incl = lane_h <= rowl
    eye = jnp.where(lane_h == rowl, 1.0, 0.0)

    chains = [(ch, p) for ch in range(n_chunks) for p in range(n_pairs)]
    rsl = lambda ch: slice(ch * L, (ch + 1) * L)
    csl = lambda p: slice(p * LANES, (p + 1) * LANES)
    fdot = lambda x, y: jnp.dot(x, y, preferred_element_type=F32)
    at, bt, kt, rt, vh, g_l = {}, {}, {}, {}, {}, {}
    for c_ in chains:
        ch, p = c_
        rs, cs = rsl(ch), csl(p)
        kkh = kkr[rs, cs]
        kkh = kkh / jnp.maximum(jnp.sqrt(_pair_sum(kkh * kkh, first)), 1e-12)
        vh[c_] = v[rs, cs]
        g_l[c_] = ec[ch * L + L - 1:ch * L + L, cs]
        at[c_] = -kkh * ecm[rs, cs]
        bt[c_] = kkh * a[rs, cs] * eci[rs, cs]
        kt[c_] = kmod[rs, cs] * eci[rs, cs]
        rt[c_] = r[rs, cs] * ec[rs, cs]
    gm = {c_: _nt(bf(jnp.concatenate([at[c_], rt[c_]], axis=0)),
                  jnp.concatenate([_stackmask(bf(bt[c_])), _stackmask(bf(kt[c_]))], axis=0)) for c_ in chains}
    a_ab = {c_: jnp.where(strict, gm[c_][:L, :LANES], 0.0) for c_ in chains}
    vsm = {c_: _stackmask(bf(vh[c_])) for c_ in chains}
    cmat = {c_: fdot(bf(jnp.where(strict, gm[c_][:L, LANES:], 0.0)), vsm[c_]) for c_ in chains}
    t_inv = {c_: eye + a_ab[c_] for c_ in chains}
    pw = {c_: bf(a_ab[c_]) for c_ in chains}
    for _ in range(int(math.log2(L)) - 1):
        pw = {c_: bf(fdot(pw[c_], _stackmask(pw[c_]))) for c_ in chains}
        t_inv = {c_: t_inv[c_] + fdot(pw[c_], _stackmask(bf(t_inv[c_]))) for c_ in chains}
    zz = {c_: fdot(bf(t_inv[c_]), jnp.concatenate([_stackmask(bf(at[c_])), _stackmask(bf(cmat[c_]))], axis=1))
          for c_ in chains}
    qy = {c_: fdot(bf(jnp.where(incl, gm[c_][L:, :LANES], 0.0)),
                   jnp.concatenate([_stackmask(bf(zz[c_][:, :LANES])), _stackmask(bf(zz[c_][:, LANES:]))], axis=1))
          for c_ in chains}
    y0 = {c_: qy[c_][:, LANES:] + fdot(bf(jnp.where(incl, gm[c_][L:, LANES:], 0.0)), vsm[c_]) for c_ in chains}
    qa = {c_: bf(jnp.concatenate([rt[c_] + qy[c_][:, :LANES], zz[c_][:, :LANES]], axis=0)) for c_ in chains}
    bkg = {c_: bf(jnp.concatenate([bt[c_] * g_l[c_], kt[c_] * g_l[c_]], axis=0)) for c_ in chains}

    lane_s = lax.broadcasted_iota(I32, (hd, LANES), 1)
    sp = [s_ref[p] for p in range(n_pairs)]
    for ch in range(n_chunks):
        rs = rsl(ch)
        yw = [_nt(qa[ch, p], _stackmask(bf(sp[p]))) for p in range(n_pairs)]
        upd = [_tn(bf(jnp.concatenate([yw[p][L:] + zz[ch, p][:, LANES:], vh[ch, p]], axis=0)), bkg[ch, p])
               for p in range(n_pairs)]
        for p in range(n_pairs):
            cs = csl(p)
            sp[p] = sp[p] * g_l[ch, p] + jnp.where(lane_s < hd, upd[p][:hd], upd[p][hd:])
            y = yw[p][:L] + y0[ch, p]
            mean = _pair_sum(y, first) * (1.0 / hd)
            yc = y - mean
            var = _pair_sum(yc * yc, first) * (1.0 / hd)
            yn = yc * lax.rsqrt(var + GN_EPS) * lnw_ref[:, cs] + lnb_ref[:, cs]
            bonus = _pair_sum(brk[rs, cs], first) * vh[ch, p]
            o_ref[rs, cs] = ((yn + bonus) * g_ref[rs, cs]).astype(o_ref.dtype)
    for p in range(n_pairs):
        s_ref[p] = sp[p]


def _rwkv_scan(r, k, v, a, ld, g, k_k, k_a, r_k, ln_w, ln_b, L=RWKV_CHUNK, tm=256):
    B, S, W = r.shape
    seq = pl.BlockSpec((None, tm, W), lambda b, c: (b, c, 0))
    vec = pl.BlockSpec((1, W), lambda b, c: (0, 0))
    return pl.pallas_call(
        functools.partial(_rwkv_scan_kernel, L=L),
        grid=(B, S // tm),
        in_specs=[seq] * 6 + [vec] * 5,
        out_specs=seq,
        out_shape=jax.ShapeDtypeStruct((B, S, W), BF16),
        scratch_shapes=[pltpu.VMEM((W // LANES, RWKV_HEAD, LANES), F32)],
        compiler_params=_cp("parallel", "arbitrary"),
        name="rwkv_scan",
    )(r, k, v, a, ld, g, k_k.reshape(1, W), k_a.reshape(1, W), r_k.reshape(1, W),
      ln_w.reshape(1, W), ln_b.reshape(1, W))


def _post_kernel(attn_ref, rw_ref, gate_ref, x_ref, mod_ref, wa_ref, wb_ref, wo_ref, g2_ref, wr_ref,
                 sug_ref, sd_ref, base_ref, h2p_ref, lg_ref):
    D = x_ref.shape[1]
    ya = jnp.dot(attn_ref[...], wa_ref[...], preferred_element_type=F32)
    yb = jnp.dot(rw_ref[...], wb_ref[...], preferred_element_type=F32)
    m = gate_ref[:, 0:D] * ya + gate_ref[:, D:] * yb
    x1 = x_ref[...] + mod_ref[2:3, :] * jnp.dot(m.astype(BF16), wo_ref[...], preferred_element_type=F32)
    y = x1 * lax.rsqrt(jnp.mean(x1 * x1, axis=-1, keepdims=True) + NORM_EPS) * g2_ref[...]
    h2 = y * (1.0 + mod_ref[4:5, :]) + mod_ref[3:4, :]
    lg_ref[...] = _nt(wr_ref[...], h2, precision=HI)
    hb = h2.astype(BF16)
    h2p_ref[...] = _pack_halves(hb.astype(F32))
    F = sd_ref.shape[0]
    gu = jnp.dot(hb, sug_ref[...], preferred_element_type=F32)
    shared = jnp.dot((_silu(gu[:, :F]) * gu[:, F:]).astype(BF16), sd_ref[...], preferred_element_type=F32)
    base_ref[...] = x1 + mod_ref[5:6, :] * shared


def _post(attn, rw, gate, x2, mod, wa, wb, wo, norm2_g, w_router_t, sug, sd, S, tm=256):
    N, D = x2.shape
    E = w_router_t.shape[0]
    tpb = S // tm
    row = lambda w: pl.BlockSpec((tm, w), lambda i: (i, 0))
    full = lambda a: pl.BlockSpec(a.shape, lambda i: (0, 0))
    return pl.pallas_call(
        _post_kernel,
        grid=(N // tm,),
        in_specs=[row(attn.shape[1]), row(rw.shape[1]), row(gate.shape[1]), row(D),
                  pl.BlockSpec((None, 6, D), lambda i: (i // tpb, 0, 0)),
                  full(wa), full(wb), full(wo), pl.BlockSpec((1, D), lambda i: (0, 0)), full(w_router_t),
                  full(sug), full(sd)],
        out_specs=[row(D), row(D // 2), pl.BlockSpec((E, tm), lambda i: (0, i))],
        out_shape=[jax.ShapeDtypeStruct((N, D), F32), jax.ShapeDtypeStruct((N, D // 2), U32),
                   jax.ShapeDtypeStruct((E, N), F32)],
        compiler_params=_cp("parallel"),
        name="post_mixer",
    )(attn, rw, gate, x2, mod, wa, wb, wo, norm2_g.reshape(1, D), w_router_t, sug, sd)


def _first_argmax(x, idx, sentinel):
    m = jnp.max(x, axis=0, keepdims=True)
    return m, jnp.min(jnp.where(x == m, idx, sentinel), axis=0, keepdims=True)


def _route_kernel(lg_ref, bias_ref, e_ref, w_ref):
    E, T = lg_ref.shape
    gsz = E // N_GROUPS
    scores = _sigmoid(lg_ref[...])
    biased = scores + bias_ref[...]
    ig = lax.broadcasted_iota(I32, (gsz, T), 0)
    gs = []
    for g in range(N_GROUPS):
        blk = biased[g * gsz:(g + 1) * gsz, :]
        m1, i1 = _first_argmax(blk, ig, gsz)
        m2 = jnp.max(jnp.where(ig == i1, -jnp.inf, blk), axis=0, keepdims=True)
        gs.append(m1 + m2)
    gsc = jnp.concatenate(gs, axis=0)
    i8 = lax.broadcasted_iota(I32, (N_GROUPS, T), 0)
    chosen = jnp.zeros((N_GROUPS, T), F32)
    for _ in range(TOPK_GROUPS):
        _, gi = _first_argmax(gsc, i8, N_GROUPS)
        hit = i8 == gi
        chosen = jnp.where(hit, 1.0, chosen)
        gsc = jnp.where(hit, -jnp.inf, gsc)
    masked = jnp.concatenate(
        [jnp.where(chosen[g:g + 1, :] > 0.0, biased[g * gsz:(g + 1) * gsz, :], -jnp.inf)
         for g in range(N_GROUPS)], axis=0)
    ie = lax.broadcasted_iota(I32, (E, T), 0)
    idxs, wts = [], []
    for _ in range(TOP_K):
        _, ei = _first_argmax(masked, ie, E)
        hit = ie == ei
        idxs.append(ei)
        wts.append(jnp.sum(jnp.where(hit, scores, 0.0), axis=0, keepdims=True))
        masked = jnp.where(hit, -jnp.inf, masked)
    wt = jnp.concatenate(wts, axis=0)
    e_ref[...] = jnp.concatenate(idxs, axis=0)
    w_ref[...] = wt / jnp.sum(wt, axis=0, keepdims=True) * ROUTED_SCALE


def _route(logits_t, router_bias, T=512):
    E, N = logits_t.shape
    blk = pl.BlockSpec((TOP_K, T), lambda i: (0, i))
    return pl.pallas_call(
        _route_kernel,
        grid=(N // T,),
        in_specs=[pl.BlockSpec((E, T), lambda i: (0, i)), pl.BlockSpec((E, 1), lambda i: (0, 0))],
        out_specs=[blk, blk],
        out_shape=[jax.ShapeDtypeStruct((TOP_K, N), I32), jax.ShapeDtypeStruct((TOP_K, N), F32)],
        compiler_params=_cp("parallel"),
        name="route",
    )(logits_t, router_bias.reshape(E, 1))


def _rank_kernel(e_ref, rank_ref, cnt_ref, carry_ref, *, n_experts):
    T = e_ref.shape[1]

    @pl.when(pl.program_id(0) == 0)
    def _():
        carry_ref[...] = jnp.zeros_like(carry_ref)

    ie = lax.broadcasted_iota(I32, (n_experts, T), 0)
    e = e_ref[...]
    hits = [ie == e[kk:kk + 1, :] for kk in range(TOP_K)]
    onehot = jnp.zeros((n_experts, T), F32)
    for hsel in hits:
        onehot = onehot + hsel.astype(F32)
    tr = lax.broadcasted_iota(I32, (T, T), 0)
    tc = lax.broadcasted_iota(I32, (T, T), 1)
    before = (tr < tc).astype(BF16)
    base = _bdot(onehot, before) + carry_ref[:, 0:1]
    rank_ref[...] = jnp.concatenate(
        [jnp.sum(jnp.where(hsel, base, 0.0), axis=0, keepdims=True) for hsel in hits], axis=0).astype(I32)
    carry_ref[...] = carry_ref[...] + jnp.sum(onehot, axis=1, keepdims=True)
    cnt_ref[...] = carry_ref[...]


def _ranks(eidx_t, n_experts, T=512):
    N = eidx_t.shape[1]
    blk = pl.BlockSpec((TOP_K, T), lambda i: (0, i))
    cnt = pl.BlockSpec((n_experts, LANES), lambda i: (0, 0))
    return pl.pallas_call(
        functools.partial(_rank_kernel, n_experts=n_experts),
        grid=(N // T,),
        in_specs=[blk],
        out_specs=[blk, cnt],
        out_shape=[jax.ShapeDtypeStruct((TOP_K, N), I32), jax.ShapeDtypeStruct((n_experts, LANES), F32)],
        scratch_shapes=[pltpu.VMEM((n_experts, LANES), F32)],
        compiler_params=_cp("arbitrary"),
        name="ranks",
    )(eidx_t)


def _dest_kernel(e_ref, rank_ref, ustart_ref, pstart_ref, du_ref, dp_ref):
    E = ustart_ref.shape[0]
    T = e_ref.shape[1]
    ie = lax.broadcasted_iota(I32, (E, T), 0)
    e = e_ref[...]
    rank = rank_ref[...]
    for start_ref, d_ref in ((ustart_ref, du_ref), (pstart_ref, dp_ref)):
        start = start_ref[:, 0:1]
        rows = [jnp.sum(jnp.where(ie == e[kk:kk + 1, :], start, 0.0), axis=0, keepdims=True)
                for kk in range(TOP_K)]
        d_ref[...] = jnp.concatenate(rows, axis=0).astype(I32) + rank


def _dests(eidx_t, rank_t, ustart, pstart, T=512):
    N = eidx_t.shape[1]
    E = ustart.shape[0]
    blk = pl.BlockSpec((TOP_K, T), lambda i: (0, i))
    tab = pl.BlockSpec((E, LANES), lambda i: (0, 0))
    lanes = lambda a: jnp.broadcast_to(a.astype(F32)[:, None], (E, LANES))
    return pl.pallas_call(
        _dest_kernel,
        grid=(N // T,),
        in_specs=[blk, blk, tab, tab],
        out_specs=[blk, blk],
        out_shape=[jax.ShapeDtypeStruct((TOP_K, N), I32)] * 2,
        compiler_params=_cp("parallel"),
        name="dests",
    )(eidx_t, rank_t, lanes(ustart), lanes(pstart))


def _rowtable_kernel(d_ref, tab_ref, *, n_pairs):
    T = d_ref.shape[1]
    t0 = pl.program_id(0) * T

    def body(t, carry):
        for kk in range(TOP_K):
            tab_ref[d_ref[kk, t]] = t0 + t
        return carry

    lax.fori_loop(0, T, body, 0, unroll=4)

    @pl.when(pl.program_id(0) == pl.num_programs(0) - 1)
    def _():
        def tail(j, carry):
            tab_ref[n_pairs + j] = 0
            return carry
        lax.fori_loop(0, tab_ref.shape[0] - n_pairs, tail, 0)


def _rowtable(dest_u, slack, T=512):
    N = dest_u.shape[1]
    return pl.pallas_call(
        functools.partial(_rowtable_kernel, n_pairs=N * TOP_K),
        grid=(N // T,),
        in_specs=[pl.BlockSpec((TOP_K, T), lambda i: (0, i), memory_space=pltpu.SMEM)],
        out_specs=pl.BlockSpec(memory_space=pltpu.SMEM),
        out_shape=jax.ShapeDtypeStruct((N * TOP_K + slack,), I32),
        compiler_params=_cp("arbitrary"),
        name="rowtable",
    )(dest_u)


def _expert_kernel(us_ref, ps_ref, tab_ref, h_hbm, wug_ref, wd_ref, y_hbm,
                   hres, wug_bf, wd_bf, xg, ybuf, cnt_ref, hsem, ysem, *, n_rows):
    e = pl.program_id(0)
    _, G, _, C = xg.shape
    R = G * 8
    F = wd_ref.shape[0]

    def gather(slot, row0):
        for g in range(G):
            for j in range(8):
                xg[slot, g, j:j + 1, :] = hres[pl.ds(tab_ref[row0 + (g * 8 + j)], 1), :]

    def y_copy(slot, row):
        return pltpu.make_async_copy(ybuf.at[slot], y_hbm.at[pl.ds(row, R)], ysem.at[slot])

    @pl.when(e == 0)
    def _():
        cp = pltpu.make_async_copy(h_hbm, hres, hsem)
        cp.start()
        cp.wait()
        cnt_ref[0] = 0
        gather(0, 0)
        ybuf[...] = jnp.zeros_like(ybuf)
        for slot in range(2):
            y_copy(slot, n_rows + slot * R).start()

    wug_bf[...] = wug_ref[...].astype(BF16)
    wd_bf[...] = wd_ref[...].astype(BF16)
    base = us_ref[e]
    end = us_ref[e + 1]
    pbase = ps_ref[e]
    n_tiles = lax.shift_right_logical(end - base + (R - 1), int(math.log2(R)))

    def tile(i, carry):
        n_done = cnt_ref[0]
        slot = n_done & 1
        gather(1 - slot, jnp.where(i + 1 < n_tiles, base + (i + 1) * R, end))
        lo, hi = _unpack_halves(xg[slot].reshape(R, C))
        gu = (jnp.dot(lo.astype(BF16), wug_bf[0:C, :], preferred_element_type=F32)
              + jnp.dot(hi.astype(BF16), wug_bf[C:, :], preferred_element_type=F32))
        hid = (_silu(gu[:, :F]) * gu[:, F:]).astype(BF16)
        y = jnp.dot(hid, wd_bf[...], preferred_element_type=F32)
        y_copy(slot, 0).wait()
        ybuf[slot] = _pack_halves(y.astype(BF16).astype(F32))
        y_copy(slot, pl.multiple_of(pbase + i * R, R)).start()
        cnt_ref[0] = n_done + 1
        return carry

    lax.fori_loop(0, n_tiles, tile, 0)

    @pl.when(e == pl.num_programs(0) - 1)
    def _():
        for slot in range(2):
            y_copy(slot, 0).wait()
        ybuf[0] = jnp.zeros((R, C), U32)
        first = lax.shift_right_logical(pbase + n_tiles * R, int(math.log2(R)))
        n_left = n_rows // R - first

        def fill(t, carry):
            y_copy(0, pl.multiple_of((first + t) * R, R)).start()
            return carry

        def drain(t, carry):
            y_copy(0, 0).wait()
            return carry

        lax.fori_loop(0, n_left, fill, 0)
        lax.fori_loop(0, n_left, drain, 0)


def _experts(ustart, pstart, table, h2p, w_ug, w_d, n_rows, R):
    N, C = h2p.shape
    E, D, F2 = w_ug.shape
    F = w_d.shape[1]
    grid_spec = pltpu.PrefetchScalarGridSpec(
        num_scalar_prefetch=3,
        grid=(E,),
        in_specs=[pl.BlockSpec(memory_space=pl.ANY),
                  pl.BlockSpec((None, D, F2), lambda e, us, ps, tab: (e, 0, 0)),
                  pl.BlockSpec((None, F, D), lambda e, us, ps, tab: (e, 0, 0))],
        out_specs=pl.BlockSpec(memory_space=pl.ANY),
        scratch_shapes=[pltpu.VMEM((N, C), U32), pltpu.VMEM((D, F2), BF16), pltpu.VMEM((F, D), BF16),
                        pltpu.VMEM((2, R // 8, 8, C), U32), pltpu.VMEM((2, R, C), U32), pltpu.SMEM((1,), I32),
                        pltpu.SemaphoreType.DMA, pltpu.SemaphoreType.DMA((2,))],
    )
    return pl.pallas_call(
        functools.partial(_expert_kernel, n_rows=n_rows),
        grid_spec=grid_spec,
        out_shape=jax.ShapeDtypeStruct((n_rows + 2 * R, C), U32),
        compiler_params=_cp("arbitrary"),
        name="experts",
    )(ustart, pstart, table, h2p, w_ug, w_d)


def _sc_gather_rows(src, idx, split=2):
    R, C0 = src.shape
    P0 = idx.shape[0]
    src = src.reshape(R * split, C0 // split)
    idx_flat = (idx[:, None] * split + jnp.arange(split, dtype=I32)[None, :]).reshape(1, P0 * split)
    return _sc_gather_pieces(src, idx_flat).reshape(P0, C0)


def _sc_gather_pieces(src, idx_flat):
    C = src.shape[1]
    P = idx_flat.shape[1]
    mesh = plsc.VectorSubcoreMesh(core_axis_name="c", subcore_axis_name="s")

    @functools.partial(pl.kernel, out_type=jax.ShapeDtypeStruct((P, C), src.dtype), mesh=mesh, scratch_types=[])
    def gather_kernel(x_hbm, i_hbm, o_hbm):
        def body(i_vmem, o_vmem):
            pltpu.sync_copy(x_hbm.at[i_vmem.at[0]], o_vmem)

        pltpu.emit_pipeline(
            body,
            grid=(P // SC_WINDOW,),
            in_specs=[pl.BlockSpec((1, SC_WINDOW), lambda i: (0, i))],
            out_specs=[pl.BlockSpec((SC_WINDOW, C), lambda i: (i, 0))],
            core_axis_name=("c", "s"),
            dimension_semantics=(pltpu.PARALLEL,),
        )(i_hbm, o_hbm)

    return gather_kernel(src, idx_flat)


def _combine_kernel(*refs):
    y_refs = refs[:TOP_K]
    w_ref, base_ref, mod_ref, o_ref = refs[TOP_K:]
    T = base_ref.shape[0]
    tr = lax.broadcasted_iota(I32, (T, T), 0)
    tc = lax.broadcasted_iota(I32, (T, T), 1)
    wcol = _nt((tr == tc).astype(F32), w_ref[...], precision=HI)
    acc_lo = acc_hi = None
    for kk in range(TOP_K):
        lo, hi = _unpack_halves(y_refs[kk][...])
        wk = wcol[:, kk:kk + 1]
        acc_lo = lo * wk if acc_lo is None else acc_lo + lo * wk
        acc_hi = hi * wk if acc_hi is None else acc_hi + hi * wk
    routed = jnp.concatenate([acc_lo, acc_hi], axis=1)
    o_ref[...] = base_ref[...] + mod_ref[5:6, :] * routed


def _combine(yg, w_t, base, mod, S, T=256):
    N, D = base.shape
    C = yg.shape[1]
    tpb = S // T
    n_tiles = N // T
    row = pl.BlockSpec((T, D), lambda i: (i, 0))
    slot = lambda kk: pl.BlockSpec((T, C), lambda i: (kk * n_tiles + i, 0))
    return pl.pallas_call(
        _combine_kernel,
        grid=(n_tiles,),
        in_specs=[slot(kk) for kk in range(TOP_K)] + [
            pl.BlockSpec((TOP_K, T), lambda i: (0, i)),
            row,
            pl.BlockSpec((None, 6, D), lambda i: (i // tpb, 0, 0))],
        out_specs=row,
        out_shape=jax.ShapeDtypeStruct((N, D), F32),
        compiler_params=_cp("parallel"),
        name="combine",
    )(*([yg] * TOP_K), w_t, base, mod)


def _layer(x, c, positions, layer_idx, w_ada, b_ada, norm1_g, w_in, w_gate, b_gate,
           q_norm_g, k_norm_g, lambda_q1, lambda_k1, lambda_q2, lambda_k2, subln_g,
           rwkv_mu, w_decay0, w_decay2, a0, a2, g2, k_k, k_a, r_k, ln_x_w, ln_x_b,
           w_branch_a, w_branch_b, w_out, norm2_g, w_router, router_bias,
           w_expert_up_gate, w_expert_down, w_shared_up_gate, w_shared_down):
    B, S, D = x.shape
    N = B * S
    E = w_router.shape[1]
    da_width = w_branch_a.shape[0]
    rw_width = w_branch_b.shape[0]
    rw_cols = rwkv_mu.shape[0]
    lambda_init = 0.8 - 0.6 * math.exp(-0.3 * layer_idx)

    mod = _adaln(c, w_ada, b_ada)
    x2 = x.reshape(N, D)
    w_cat = jnp.concatenate([w_in, w_gate], axis=1).astype(BF16)
    q, k, v, prw, gate = _inproj(x2, mod, norm1_g, w_cat, b_gate, S, da_width, rw_cols)

    qn, kn = _qkprep(q, k, positions.reshape(N, 1), q_norm_g, k_norm_g)
    lam_vecs = jnp.stack([lambda_q1, lambda_k1, lambda_q2, lambda_k2])
    score_bound = 1.01 * DA_HEAD_DIM ** 0.5 * jnp.max(jnp.abs(q_norm_g)) * jnp.max(jnp.abs(k_norm_g))
    attn = _diff_attention(qn, kn, v, score_bound, lam_vecs, subln_g, B, S, lambda_init)

    r_, k_, v_, a_, ld_, g_ = _rwkv_prep(prw.reshape(B, S, rw_cols), rwkv_mu, w_decay0, w_decay2,
                                         a0, a2, g2, rw_width)
    rw = _rwkv_scan(r_, k_, v_, a_, ld_, g_, k_k, k_a, r_k.reshape(-1), ln_x_w, ln_x_b).reshape(N, rw_width)

    base, h2p, logits_t = _post(attn, rw, gate, x2, mod, w_branch_a.astype(BF16), w_branch_b.astype(BF16),
                                w_out.astype(BF16), norm2_g, w_router.T,
                                w_shared_up_gate.astype(BF16), w_shared_down.astype(BF16), S)

    eidx_t, w_t = _route(logits_t, router_bias)
    rank_t, counts = _ranks(eidx_t, E)
    R = EXPERT_TILE
    cnt = counts[:, 0].astype(I32)
    uends = jnp.cumsum(cnt)
    pcnt = (cnt + R - 1) // R * R
    pstart = jnp.cumsum(pcnt) - pcnt
    dest_u, dest_p = _dests(eidx_t, rank_t, uends - cnt, pstart)
    table = _rowtable(dest_u, R)
    ustart = jnp.concatenate([jnp.zeros((1,), I32), uends])
    n_rows = (N * TOP_K + E * (R - 1) + R - 1) // R * R
    y = _experts(ustart, pstart, table, h2p, w_expert_up_gate, w_expert_down, n_rows, R)
    yg = _sc_gather_rows(y, dest_p.reshape(N * TOP_K))
    out = _combine(yg, w_t, base, mod, S)
    return out.reshape(B, S, D)


def kernel(x, c, positions, w_ada, b_ada, norm1_g, w_in, w_gate, b_gate, q_norm_g, k_norm_g, lambda_q1, lambda_k1, lambda_q2, lambda_k2, subln_g, rwkv_mu, w_decay0, w_decay2, a0, a2, g2, k_k, k_a, r_k, ln_x_w, ln_x_b, w_branch_a, w_branch_b, w_out, norm2_g, w_router, router_bias, w_expert_up_gate, w_expert_down, w_shared_up_gate, w_shared_down):
    for l in range(w_ada.shape[0]):
        x = _layer(x, c, positions, l, w_ada[l], b_ada[l], norm1_g[l], w_in[l], w_gate[l], b_gate[l],
                   q_norm_g[l], k_norm_g[l], lambda_q1[l], lambda_k1[l], lambda_q2[l], lambda_k2[l],
                   subln_g[l], rwkv_mu[l], w_decay0[l], w_decay2[l], a0[l], a2[l], g2[l], k_k[l],
                   k_a[l], r_k[l], ln_x_w[l], ln_x_b[l], w_branch_a[l], w_branch_b[l], w_out[l],
                   norm2_g[l], w_router[l], router_bias[l], w_expert_up_gate[l], w_expert_down[l],
                   w_shared_up_gate[l], w_shared_down[l])
    return x
```

```python
import functools
import math

import jax
import jax.numpy as jnp
from jax import lax
from jax.experimental import pallas as pl
from jax.experimental.pallas import tpu as pltpu
from jax.experimental.pallas import tpu_sc as plsc

F32 = jnp.float32
BF16 = jnp.bfloat16
I32 = jnp.int32
U32 = jnp.uint32
HI = lax.Precision.HIGHEST

CHUNK = 64
ROPE_THETA = 10000.0
NORM_EPS = 1e-6
SUBLN_EPS = 1e-5
DA_HEAD_DIM = 64
RWKV_HEAD = 64
GN_EPS = 64e-5
TOP_K = 8
N_GROUPS = 8
TOPK_GROUPS = 4
ROUTED_SCALE = 2.5
EXPERT_TILE = 256
RWKV_CHUNK = 64
LANES = 128
SC_WINDOW = 128
NEG = -1e30
MAX_PLAIN_SCORE = 40.0
VMEM_LIMIT = 56 * 1024 * 1024


def _cp(*sem):
    return pltpu.CompilerParams(dimension_semantics=sem, vmem_limit_bytes=VMEM_LIMIT)


def _bdot(a, b):
    return jnp.dot(a.astype(BF16), b.astype(BF16), preferred_element_type=F32)


def _fdot(a, b):
    return jnp.dot(a, b, precision=HI, preferred_element_type=F32)


def _nt(a, b, precision=None):
    return lax.dot_general(a, b, (((1,), (1,)), ((), ())), precision=precision,
                           preferred_element_type=F32)


def _tn(a, b, precision=None):
    return lax.dot_general(a, b, (((0,), (0,)), ((), ())), precision=precision,
                           preferred_element_type=F32)


def _pack_halves(x):
    c = x.shape[1] // 2
    lo = lax.bitcast_convert_type(x[:, :c], U32)
    hi = lax.bitcast_convert_type(x[:, c:], U32)
    return (hi & jnp.uint32(0xFFFF0000)) | (lo >> 16)


def _unpack_halves(w):
    lo = lax.bitcast_convert_type(w << 16, F32)
    hi = lax.bitcast_convert_type(w & jnp.uint32(0xFFFF0000), F32)
    return lo, hi


def _sigmoid(x):
    return 1.0 / (1.0 + jnp.exp(-x))


def _silu(x):
    return x * _sigmoid(x)


def _ada_kernel(c_ref, w_ref, b_ref, o_ref):
    o_ref[...] = _fdot(_silu(c_ref[...]), w_ref[...]) + b_ref[...]


def _adaln(c, w_ada, b_ada):
    B, D = c.shape
    rows = -(-B // 8) * 8
    cpad = jnp.zeros((rows, D), F32).at[:B].set(c)
    n_out = w_ada.shape[1]
    out = pl.pallas_call(
        _ada_kernel,
        grid=(n_out // D,),
        in_specs=[pl.BlockSpec((rows, D), lambda j: (0, 0)),
                  pl.BlockSpec((D, D), lambda j: (0, j)),
                  pl.BlockSpec((1, D), lambda j: (0, j))],
        out_specs=pl.BlockSpec((rows, D), lambda j: (0, j)),
        out_shape=jax.ShapeDtypeStruct((rows, n_out), F32),
        compiler_params=_cp("arbitrary"),
        name="adaln",
    )(cpad, w_ada, b_ada.reshape(1, n_out))
    return out[:B].reshape(B, n_out // D, D)


def _inproj_kernel(x_ref, mod_ref, g_ref, w_ref, bg_ref, q_ref, k_ref, v_ref, rw_ref, gate_ref,
                   *, da_width, rw_cols):
    x = x_ref[...]
    y = x * lax.rsqrt(jnp.mean(x * x, axis=-1, keepdims=True) + NORM_EPS) * g_ref[...]
    h = (y * (1.0 + mod_ref[1:2, :]) + mod_ref[0:1, :]).astype(BF16)
    c = 0
    for ref in (q_ref, k_ref, v_ref):
        ref[...] = jnp.dot(h, w_ref[:, c:c + da_width], preferred_element_type=F32).astype(ref.dtype)
        c += da_width
    step = 512
    for o in range(0, rw_cols, step):
        wd = min(step, rw_cols - o)
        rw_ref[:, o:o + wd] = jnp.dot(h, w_ref[:, c + o:c + o + wd], preferred_element_type=F32)
    c += rw_cols
    n_gate = gate_ref.shape[1]
    for o in range(0, n_gate, step):
        z = jnp.dot(h, w_ref[:, c + o:c + o + step], preferred_element_type=F32) + bg_ref[:, o:o + step]
        gate_ref[:, o:o + step] = _sigmoid(z)


def _inproj(x2, mod, norm1_g, w_cat, b_gate, S, da_width, rw_cols, tm=256):
    N, D = x2.shape
    n_gate = b_gate.shape[0]
    tpb = S // tm
    kern = functools.partial(_inproj_kernel, da_width=da_width, rw_cols=rw_cols)
    row = lambda w: pl.BlockSpec((tm, w), lambda i: (i, 0))
    return pl.pallas_call(
        kern,
        grid=(N // tm,),
        in_specs=[row(D),
                  pl.BlockSpec((None, 6, D), lambda i: (i // tpb, 0, 0)),
                  pl.BlockSpec((1, D), lambda i: (0, 0)),
                  pl.BlockSpec(w_cat.shape, lambda i: (0, 0)),
                  pl.BlockSpec((1, n_gate), lambda i: (0, 0))],
        out_specs=[row(da_width), row(da_width), row(da_width), row(rw_cols), row(n_gate)],
        out_shape=[jax.ShapeDtypeStruct((N, da_width), F32),
                   jax.ShapeDtypeStruct((N, da_width), F32),
                   jax.ShapeDtypeStruct((N, da_width), BF16),
                   jax.ShapeDtypeStruct((N, rw_cols), F32),
                   jax.ShapeDtypeStruct((N, n_gate), F32)],
        compiler_params=_cp("parallel"),
        name="inproj",
    )(x2, mod, norm1_g.reshape(1, D), w_cat, b_gate.reshape(1, n_gate))


def _qkprep_kernel(q_ref, k_ref, pos_ref, invf_ref, qg_ref, kg_ref, qo_ref, ko_ref, *, scale):
    tm = q_ref.shape[0]
    lane = lax.broadcasted_iota(I32, (tm, LANES), 1)
    first = lane < DA_HEAD_DIM
    lo_half = (lane & (DA_HEAD_DIM - 1)) < DA_HEAD_DIM // 2
    ang = pos_ref[...].astype(F32) * invf_ref[...]
    cos = jnp.cos(ang)
    sin = jnp.sin(ang)
    sin = jnp.where(lo_half, -sin, sin)

    def one(src, dst, g_ref, mult):
        for blk in range(src.shape[1] // LANES):
            x = src[:, blk * LANES:(blk + 1) * LANES]
            xx = x * x
            s_all = jnp.sum(xx, axis=-1, keepdims=True)
            s_first = jnp.sum(jnp.where(first, xx, 0.0), axis=-1, keepdims=True)
            ms = jnp.where(first, s_first, s_all - s_first) * (1.0 / DA_HEAD_DIM)
            xn = x * lax.rsqrt(ms + NORM_EPS) * g_ref[...]
            rot = jnp.where(lo_half, pltpu.roll(xn, LANES - DA_HEAD_DIM // 2, axis=1),
                            pltpu.roll(xn, DA_HEAD_DIM // 2, axis=1))
            dst[:, blk * LANES:(blk + 1) * LANES] = ((xn * cos + rot * sin) * mult).astype(dst.dtype)

    one(q_ref, qo_ref, qg_ref, scale)
    one(k_ref, ko_ref, kg_ref, 1.0)


def _qkprep(q, k, pos2, q_norm_g, k_norm_g, tm=512):
    N, W = q.shape
    d = DA_HEAD_DIM
    inv_freq = 1.0 / (ROPE_THETA ** (jnp.arange(0, d, 2, dtype=F32) / d))
    invf = jnp.tile(inv_freq, LANES // (d // 2)).reshape(1, LANES)
    row = pl.BlockSpec((tm, W), lambda i: (i, 0))
    vec = pl.BlockSpec((1, LANES), lambda i: (0, 0))
    return pl.pallas_call(
        functools.partial(_qkprep_kernel, scale=d ** -0.5 * math.log2(math.e)),
        grid=(N // tm,),
        in_specs=[row, row, pl.BlockSpec((tm, 1), lambda i: (i, 0)), vec, vec, vec],
        out_specs=[row, row],
        out_shape=[jax.ShapeDtypeStruct((N, W), BF16)] * 2,
        compiler_params=_cp("parallel"),
        name="qkprep",
    )(q, k, pos2, invf, jnp.tile(q_norm_g, 2).reshape(1, LANES), jnp.tile(k_norm_g, 2).reshape(1, LANES))


def _attn_kernel(flag_ref, q_ref, k_ref, v_ref, lam_ref, sg_ref, o_ref, qz_ref, m_ref, l_ref, lp_ref, acc_ref,
                 *, tq, lambda_init):
    i = pl.program_id(2)
    lane = lax.broadcasted_iota(I32, (tq, LANES), 1)
    q = q_ref[...]
    zero = jnp.zeros_like(q)
    qz_ref[0:tq, :] = jnp.where(lane < DA_HEAD_DIM, q, zero)
    qz_ref[tq:, :] = jnp.where(lane >= DA_HEAD_DIM, q, zero)
    acc_ref[...] = jnp.zeros_like(acc_ref)
    bounded = flag_ref[0] == 1

    def scores(j, masked):
        off = pl.multiple_of(j * tq, tq)
        s = _nt(qz_ref[...], k_ref[pl.ds(off, tq), :])
        if masked:
            row = lax.broadcasted_iota(I32, s.shape, 0)
            col = lax.broadcasted_iota(I32, s.shape, 1)
            s = jnp.where((col // CHUNK) <= ((row & (tq - 1)) // CHUNK), s, NEG)
        return s, off

    def plain_step(j, masked):
        s, off = scores(j, masked)
        pr = jnp.exp2(s)
        part = pr[:, 0:LANES]
        for cblk in range(1, tq // LANES):
            part = part + pr[:, cblk * LANES:(cblk + 1) * LANES]
        lp_ref[...] += part
        acc_ref[...] += jnp.dot(pr.astype(BF16), v_ref[pl.ds(off, tq), :], preferred_element_type=F32)

    def online_step(j, masked):
        s, off = scores(j, masked)
        m_old = m_ref[...]
        m_new = jnp.maximum(m_old, jnp.max(s, axis=-1, keepdims=True))
        alpha = jnp.exp2(m_old - m_new)
        pr = jnp.exp2(s - m_new)
        l_ref[...] = alpha * l_ref[...] + jnp.sum(pr, axis=-1, keepdims=True)
        acc_ref[...] = alpha * acc_ref[...] + jnp.dot(pr.astype(BF16), v_ref[pl.ds(off, tq), :],
                                                      preferred_element_type=F32)
        m_ref[...] = m_new

    def run(step):
        def body(j, carry):
            step(j, False)
            return carry
        lax.fori_loop(0, i, body, 0)
        step(i, True)

    @pl.when(bounded)
    def _():
        lp_ref[...] = jnp.zeros_like(lp_ref)
        run(plain_step)
        l_ref[...] = jnp.sum(lp_ref[...], axis=-1, keepdims=True)

    @pl.when(jnp.logical_not(bounded))
    def _():
        m_ref[...] = jnp.full_like(m_ref, NEG)
        l_ref[...] = jnp.zeros_like(l_ref)
        run(online_step)

    lv = lam_ref[...]
    lam = (jnp.exp(jnp.sum(lv[0:1] * lv[1:2], keepdims=True))
           - jnp.exp(jnp.sum(lv[2:3] * lv[3:4], keepdims=True)) + lambda_init)
    o1 = acc_ref[0:tq, :] / l_ref[0:tq, :]
    o2 = acc_ref[tq:, :] / l_ref[tq:, :]
    o = o1 - lam * o2
    o = o * lax.rsqrt(jnp.mean(o * o, axis=-1, keepdims=True) + SUBLN_EPS) * sg_ref[...]
    o_ref[...] = (o * (1.0 - lambda_init)).astype(o_ref.dtype)


def _diff_attention(qn, kn, v, score_bound, lam_vecs, subln_g, B, S, lambda_init, tq=512):
    W = qn.shape[1]
    H = W // LANES
    q3 = qn.reshape(B, S, W)
    k3 = kn.reshape(B, S, W)
    v3 = v.reshape(B, S, W)
    flag = (score_bound <= MAX_PLAIN_SCORE).astype(I32).reshape(1)
    qblk = pl.BlockSpec((None, tq, LANES), lambda b, h, i, f: (b, i, h))
    kvblk = pl.BlockSpec((None, S, LANES), lambda b, h, i, f: (b, 0, h))
    grid_spec = pltpu.PrefetchScalarGridSpec(
        num_scalar_prefetch=1,
        grid=(B, H, S // tq),
        in_specs=[qblk, kvblk, kvblk,
                  pl.BlockSpec((4, DA_HEAD_DIM), lambda b, h, i, f: (0, 0)),
                  pl.BlockSpec((1, LANES), lambda b, h, i, f: (0, 0))],
        out_specs=qblk,
        scratch_shapes=[pltpu.VMEM((2 * tq, LANES), BF16),
                        pltpu.VMEM((2 * tq, 1), F32),
                        pltpu.VMEM((2 * tq, 1), F32),
                        pltpu.VMEM((2 * tq, LANES), F32),
                        pltpu.VMEM((2 * tq, LANES), F32)],
    )
    out = pl.pallas_call(
        functools.partial(_attn_kernel, tq=tq, lambda_init=lambda_init),
        grid_spec=grid_spec,
        out_shape=jax.ShapeDtypeStruct((B, S, W), BF16),
        compiler_params=_cp("parallel", "parallel", "arbitrary"),
        name="diff_attn",
    )(flag, q3, k3, v3, lam_vecs, subln_g.reshape(1, LANES))
    return out.reshape(B * S, W)


def _rwkv_prep_kernel(p_ref, prev_ref, mu_ref, w0_ref, w2_ref, a0_ref, a2_ref, g2_ref,
                      r_ref, k_ref, v_ref, a_ref, ld_ref, g_ref, *, width):
    i = pl.program_id(1)
    p = p_ref[...]
    last_prev = jnp.where(i > 0, prev_ref[7:8, :], 0.0)
    rowi = lax.broadcasted_iota(I32, p.shape, 0)
    prev = jnp.where(rowi == 0, last_prev, pltpu.roll(p, 1, axis=0))
    xs = p + (prev - p) * mu_ref[...]
    r_ref[...] = xs[:, 0:width]
    k_ref[...] = xs[:, width:2 * width]
    v_ref[...] = xs[:, 2 * width:3 * width]
    xwa = xs[:, 3 * width:3 * width + LANES]
    xg = xs[:, 3 * width + LANES:]
    z = w0_ref[...] + _fdot(jnp.tanh(xwa), w2_ref[...])
    w = -(jnp.maximum(-z, 0.0) + jnp.log(1.0 + jnp.exp(-jnp.abs(z)))) - 0.5
    ld_ref[...] = -jnp.exp(w)
    a_ref[...] = _sigmoid(a0_ref[...] + _fdot(xwa, a2_ref[...]))
    g_ref[...] = _fdot(_sigmoid(xg), g2_ref[...])


def _rwkv_prep(prw3, mu, w_decay0, w_decay2, a0, a2, g2, width, tm=256):
    B, S, C = prw3.shape
    dl, al = w_decay2.shape[0], a2.shape[0]
    assert dl + al == LANES and g2.shape[0] == LANES
    w2p = jnp.zeros((LANES, width), F32).at[:dl].set(w_decay2)
    a2p = jnp.zeros((LANES, width), F32).at[dl:].set(a2)
    vec = lambda n: pl.BlockSpec((1, n), lambda b, i: (0, 0))
    mat = pl.BlockSpec((LANES, width), lambda b, i: (0, 0))
    out = pl.BlockSpec((None, tm, width), lambda b, i: (b, i, 0))
    return pl.pallas_call(
        functools.partial(_rwkv_prep_kernel, width=width),
        grid=(B, S // tm),
        in_specs=[pl.BlockSpec((None, tm, C), lambda b, i: (b, i, 0)),
                  pl.BlockSpec((None, 8, C), lambda b, i: (b, jnp.maximum(i * (tm // 8) - 1, 0), 0)),
                  vec(C), vec(width), mat, vec(width), mat, mat],
        out_specs=[out] * 6,
        out_shape=[jax.ShapeDtypeStruct((B, S, width), F32)] * 6,
        compiler_params=_cp("parallel", "parallel"),
        name="rwkv_prep",
    )(prw3, prw3, mu.reshape(1, C), w_decay0.reshape(1, width), w2p, a0.reshape(1, width), a2p, g2)


def _stackmask(m):
    lane = lax.broadcasted_iota(I32, m.shape, 1)
    z = jnp.zeros_like(m)
    return jnp.concatenate([jnp.where(lane < RWKV_HEAD, m, z), jnp.where(lane >= RWKV_HEAD, m, z)], axis=0)


def _pair_sum(x, first):
    s1 = jnp.sum(jnp.where(first, x, 0.0), axis=-1, keepdims=True)
    s2 = jnp.sum(jnp.where(first, 0.0, x), axis=-1, keepdims=True)
    return jnp.where(first, s1, s2)


def _rwkv_scan_kernel(r_ref, k_ref, v_ref, a_ref, ld_ref, g_ref, kk_ref, ka_ref, rk_ref, lnw_ref, lnb_ref,
                      o_ref, s_ref, *, L):
    tm, W = r_ref.shape
    n_chunks = tm // L
    n_pairs = W // LANES
    hd = RWKV_HEAD
    bf = lambda t: t.astype(BF16)

    @pl.when(pl.program_id(1) == 0)
    def _():
        s_ref[...] = jnp.zeros_like(s_ref)

    row = lax.broadcasted_iota(I32, (tm, tm), 0)
    col = lax.broadcasted_iota(I32, (tm, tm), 1)
    tri = jnp.where(jnp.logical_and(col <= row, (col // L) == (row // L)), 1.0, 0.0).astype(BF16)
    ld = ld_ref[...]
    ld_hi = bf(ld)
    rem = ld - ld_hi.astype(F32)
    ld_mid = bf(rem)
    ld_lo = bf(rem - ld_mid.astype(F32))
    c = (jnp.dot(tri, ld_hi, preferred_element_type=F32) + jnp.dot(tri, ld_mid, preferred_element_type=F32)
         + jnp.dot(tri, ld_lo, preferred_element_type=F32))
    ec = jnp.exp(c)
    eci = jnp.exp(-c)
    ecm = jnp.exp(c - ld)
    r = r_ref[...]
    k = k_ref[...]
    v = v_ref[...]
    a = a_ref[...]
    kkr = k * kk_ref[...]
    kmod = k * (1.0 + (a - 1.0) * ka_ref[...])
    brk = r * kmod * rk_ref[...]

    lane = lax.broadcasted_iota(I32, (L, LANES), 1)
    rowl = lax.broadcasted_iota(I32, (L, LANES), 0)
    first = lane < hd
    lane_h = lane & (hd - 1)
    strict = lane_h < rowl
    incl = lane_h <= rowl
    eye = jnp.where(lane_h == rowl, 1.0, 0.0)

    chains = [(ch, p) for ch in range(n_chunks) for p in range(n_pairs)]
    rsl = lambda ch: slice(ch * L, (ch + 1) * L)
    csl = lambda p: slice(p * LANES, (p + 1) * LANES)
    fdot = lambda x, y: jnp.dot(x, y, preferred_element_type=F32)
    at, bt, kt, rt, vh, g_l = {}, {}, {}, {}, {}, {}
    for c_ in chains:
        ch, p = c_
        rs, cs = rsl(ch), csl(p)
        kkh = kkr[rs, cs]
        kkh = kkh / jnp.maximum(jnp.sqrt(_pair_sum(kkh * kkh, first)), 1e-12)
        vh[c_] = v[rs, cs]
        g_l[c_] = ec[ch * L + L - 1:ch * L + L, cs]
        at[c_] = -kkh * ecm[rs, cs]
        bt[c_] = kkh * a[rs, cs] * eci[rs, cs]
        kt[c_] = kmod[rs, cs] * eci[rs, cs]
        rt[c_] = r[rs, cs] * ec[rs, cs]
    gm = {c_: _nt(bf(jnp.concatenate([at[c_], rt[c_]], axis=0)),
                  jnp.concatenate([_stackmask(bf(bt[c_])), _stackmask(bf(kt[c_]))], axis=0)) for c_ in chains}
    a_ab = {c_: jnp.where(strict, gm[c_][:L, :LANES], 0.0) for c_ in chains}
    vsm = {c_: _stackmask(bf(vh[c_])) for c_ in chains}
    cmat = {c_: fdot(bf(jnp.where(strict, gm[c_][:L, LANES:], 0.0)), vsm[c_]) for c_ in chains}
    t_inv = {c_: eye + a_ab[c_] for c_ in chains}
    pw = {c_: bf(a_ab[c_]) for c_ in chains}
    for _ in range(int(math.log2(L)) - 1):
        pw = {c_: bf(fdot(pw[c_], _stackmask(pw[c_]))) for c_ in chains}
        t_inv = {c_: t_inv[c_] + fdot(pw[c_], _stackmask(bf(t_inv[c_]))) for c_ in chains}
    zz = {c_: fdot(bf(t_inv[c_]), jnp.concatenate([_stackmask(bf(at[c_])), _stackmask(bf(cmat[c_]))], axis=1))
          for c_ in chains}
    qy = {c_: fdot(bf(jnp.where(incl, gm[c_][L:, :LANES], 0.0)),
                   jnp.concatenate([_stackmask(bf(zz[c_][:, :LANES])), _stackmask(bf(zz[c_][:, LANES:]))], axis=1))
          for c_ in chains}
    y0 = {c_: qy[c_][:, LANES:] + fdot(bf(jnp.where(incl, gm[c_][L:, LANES:], 0.0)), vsm[c_]) for c_ in chains}
    qa = {c_: bf(jnp.concatenate([rt[c_] + qy[c_][:, :LANES], zz[c_][:, :LANES]], axis=0)) for c_ in chains}
    bkg = {c_: bf(jnp.concatenate([bt[c_] * g_l[c_], kt[c_] * g_l[c_]], axis=0)) for c_ in chains}

    lane_s = lax.broadcasted_iota(I32, (hd, LANES), 1)
    sp = [s_ref[p] for p in range(n_pairs)]
    for ch in range(n_chunks):
        rs = rsl(ch)
        yw = [_nt(qa[ch, p], _stackmask(bf(sp[p]))) for p in range(n_pairs)]
        upd = [_tn(bf(jnp.concatenate([yw[p][L:] + zz[ch, p][:, LANES:], vh[ch, p]], axis=0)), bkg[ch, p])
               for p in range(n_pairs)]
        for p in range(n_pairs):
            cs = csl(p)
            sp[p] = sp[p] * g_l[ch, p] + jnp.where(lane_s < hd, upd[p][:hd], upd[p][hd:])
            y = yw[p][:L] + y0[ch, p]
            mean = _pair_sum(y, first) * (1.0 / hd)
            yc = y - mean
            var = _pair_sum(yc * yc, first) * (1.0 / hd)
            yn = yc * lax.rsqrt(var + GN_EPS) * lnw_ref[:, cs] + lnb_ref[:, cs]
            bonus = _pair_sum(brk[rs, cs], first) * vh[ch, p]
            o_ref[rs, cs] = ((yn + bonus) * g_ref[rs, cs]).astype(o_ref.dtype)
    for p in range(n_pairs):
        s_ref[p] = sp[p]


def _rwkv_scan(r, k, v, a, ld, g, k_k, k_a, r_k, ln_w, ln_b, L=RWKV_CHUNK, tm=256):
    B, S, W = r.shape
    seq = pl.BlockSpec((None, tm, W), lambda b, c: (b, c, 0))
    vec = pl.BlockSpec((1, W), lambda b, c: (0, 0))
    return pl.pallas_call(
        functools.partial(_rwkv_scan_kernel, L=L),
        grid=(B, S // tm),
        in_specs=[seq] * 6 + [vec] * 5,
        out_specs=seq,
        out_shape=jax.ShapeDtypeStruct((B, S, W), BF16),
        scratch_shapes=[pltpu.VMEM((W // LANES, RWKV_HEAD, LANES), F32)],
        compiler_params=_cp("parallel", "arbitrary"),
        name="rwkv_scan",
    )(r, k, v, a, ld, g, k_k.reshape(1, W), k_a.reshape(1, W), r_k.reshape(1, W),
      ln_w.reshape(1, W), ln_b.reshape(1, W))


def _post_kernel(attn_ref, rw_ref, gate_ref, x_ref, mod_ref, wa_ref, wb_ref, wo_ref, g2_ref, wr_ref,
                 sug_ref, sd_ref, base_ref, h2p_ref, lg_ref):
    D = x_ref.shape[1]
    ya = jnp.dot(attn_ref[...], wa_ref[...], preferred_element_type=F32)
    yb = jnp.dot(rw_ref[...], wb_ref[...], preferred_element_type=F32)
    m = gate_ref[:, 0:D] * ya + gate_ref[:, D:] * yb
    x1 = x_ref[...] + mod_ref[2:3, :] * jnp.dot(m.astype(BF16), wo_ref[...], preferred_element_type=F32)
    y = x1 * lax.rsqrt(jnp.mean(x1 * x1, axis=-1, keepdims=True) + NORM_EPS) * g2_ref[...]
    h2 = y * (1.0 + mod_ref[4:5, :]) + mod_ref[3:4, :]
    lg_ref[...] = _nt(wr_ref[...], h2, precision=HI)
    hb = h2.astype(BF16)
    packed = _pack_halves(hb.astype(F32))
    half = packed.shape[1] // 2
    h2p_ref[0] = packed[:, :half]
    h2p_ref[1] = packed[:, half:]
    F = sd_ref.shape[0]
    gu = jnp.dot(hb, sug_ref[...], preferred_element_type=F32)
    shared = jnp.dot((_silu(gu[:, :F]) * gu[:, F:]).astype(BF16), sd_ref[...], preferred_element_type=F32)
    base_ref[...] = x1 + mod_ref[5:6, :] * shared


def _post(attn, rw, gate, x2, mod, wa, wb, wo, norm2_g, w_router_t, sug, sd, S, tm=256):
    N, D = x2.shape
    E = w_router_t.shape[0]
    tpb = S // tm
    row = lambda w: pl.BlockSpec((tm, w), lambda i: (i, 0))
    full = lambda a: pl.BlockSpec(a.shape, lambda i: (0, 0))
    return pl.pallas_call(
        _post_kernel,
        grid=(N // tm,),
        in_specs=[row(attn.shape[1]), row(rw.shape[1]), row(gate.shape[1]), row(D),
                  pl.BlockSpec((None, 6, D), lambda i: (i // tpb, 0, 0)),
                  full(wa), full(wb), full(wo), pl.BlockSpec((1, D), lambda i: (0, 0)), full(w_router_t),
                  full(sug), full(sd)],
        out_specs=[row(D), pl.BlockSpec((2, tm, D // 4), lambda i: (0, i, 0)), pl.BlockSpec((E, tm), lambda i: (0, i))],
        out_shape=[jax.ShapeDtypeStruct((N, D), F32), jax.ShapeDtypeStruct((2, N, D // 4), U32),
                   jax.ShapeDtypeStruct((E, N), F32)],
        compiler_params=_cp("parallel"),
        name="post_mixer",
    )(attn, rw, gate, x2, mod, wa, wb, wo, norm2_g.reshape(1, D), w_router_t, sug, sd)


def _first_argmax(x, idx, sentinel):
    m = jnp.max(x, axis=0, keepdims=True)
    return m, jnp.min(jnp.where(x == m, idx, sentinel), axis=0, keepdims=True)


def _route_kernel(lg_ref, bias_ref, e_ref, w_ref):
    E, T = lg_ref.shape
    gsz = E // N_GROUPS
    scores = _sigmoid(lg_ref[...])
    biased = scores + bias_ref[...]
    ig = lax.broadcasted_iota(I32, (gsz, T), 0)
    gs = []
    for g in range(N_GROUPS):
        blk = biased[g * gsz:(g + 1) * gsz, :]
        m1, i1 = _first_argmax(blk, ig, gsz)
        m2 = jnp.max(jnp.where(ig == i1, -jnp.inf, blk), axis=0, keepdims=True)
        gs.append(m1 + m2)
    gsc = jnp.concatenate(gs, axis=0)
    i8 = lax.broadcasted_iota(I32, (N_GROUPS, T), 0)
    chosen = jnp.zeros((N_GROUPS, T), F32)
    for _ in range(TOPK_GROUPS):
        _, gi = _first_argmax(gsc, i8, N_GROUPS)
        hit = i8 == gi
        chosen = jnp.where(hit, 1.0, chosen)
        gsc = jnp.where(hit, -jnp.inf, gsc)
    masked = jnp.concatenate(
        [jnp.where(chosen[g:g + 1, :] > 0.0, biased[g * gsz:(g + 1) * gsz, :], -jnp.inf)
         for g in range(N_GROUPS)], axis=0)
    ie = lax.broadcasted_iota(I32, (E, T), 0)
    idxs, wts = [], []
    for _ in range(TOP_K):
        _, ei = _first_argmax(masked, ie, E)
        hit = ie == ei
        idxs.append(ei)
        wts.append(jnp.sum(jnp.where(hit, scores, 0.0), axis=0, keepdims=True))
        masked = jnp.where(hit, -jnp.inf, masked)
    wt = jnp.concatenate(wts, axis=0)
    e_ref[...] = jnp.concatenate(idxs, axis=0)
    w_ref[...] = wt / jnp.sum(wt, axis=0, keepdims=True) * ROUTED_SCALE


def _route(logits_t, router_bias, T=512):
    E, N = logits_t.shape
    blk = pl.BlockSpec((TOP_K, T), lambda i: (0, i))
    return pl.pallas_call(
        _route_kernel,
        grid=(N // T,),
        in_specs=[pl.BlockSpec((E, T), lambda i: (0, i)), pl.BlockSpec((E, 1), lambda i: (0, 0))],
        out_specs=[blk, blk],
        out_shape=[jax.ShapeDtypeStruct((TOP_K, N), I32), jax.ShapeDtypeStruct((TOP_K, N), F32)],
        compiler_params=_cp("parallel"),
        name="route",
    )(logits_t, router_bias.reshape(E, 1))


def _rank_kernel(e_ref, rank_ref, cnt_ref, carry_ref, *, n_experts):
    T = e_ref.shape[1]

    @pl.when(pl.program_id(0) == 0)
    def _():
        carry_ref[...] = jnp.zeros_like(carry_ref)

    ie = lax.broadcasted_iota(I32, (n_experts, T), 0)
    e = e_ref[...]
    hits = [ie == e[kk:kk + 1, :] for kk in range(TOP_K)]
    onehot = jnp.zeros((n_experts, T), F32)
    for hsel in hits:
        onehot = onehot + hsel.astype(F32)
    tr = lax.broadcasted_iota(I32, (T, T), 0)
    tc = lax.broadcasted_iota(I32, (T, T), 1)
    before = (tr < tc).astype(BF16)
    base = _bdot(onehot, before) + carry_ref[:, 0:1]
    rank_ref[...] = jnp.concatenate(
        [jnp.sum(jnp.where(hsel, base, 0.0), axis=0, keepdims=True) for hsel in hits], axis=0).astype(I32)
    carry_ref[...] = carry_ref[...] + jnp.sum(onehot, axis=1, keepdims=True)
    cnt_ref[...] = carry_ref[...]


def _ranks(eidx_t, n_experts, T=512):
    N = eidx_t.shape[1]
    blk = pl.BlockSpec((TOP_K, T), lambda i: (0, i))
    cnt = pl.BlockSpec((n_experts, LANES), lambda i: (0, 0))
    return pl.pallas_call(
        functools.partial(_rank_kernel, n_experts=n_experts),
        grid=(N // T,),
        in_specs=[blk],
        out_specs=[blk, cnt],
        out_shape=[jax.ShapeDtypeStruct((TOP_K, N), I32), jax.ShapeDtypeStruct((n_experts, LANES), F32)],
        scratch_shapes=[pltpu.VMEM((n_experts, LANES), F32)],
        compiler_params=_cp("arbitrary"),
        name="ranks",
    )(eidx_t)


def _dest_kernel(e_ref, rank_ref, start_ref, d_ref):
    E = start_ref.shape[0]
    T = e_ref.shape[1]
    ie = lax.broadcasted_iota(I32, (E, T), 0)
    e = e_ref[...]
    start = start_ref[:, 0:1]
    rows = [jnp.sum(jnp.where(ie == e[kk:kk + 1, :], start, 0.0), axis=0, keepdims=True) for kk in range(TOP_K)]
    d_ref[...] = jnp.concatenate(rows, axis=0).astype(I32) + rank_ref[...]


def _dests(eidx_t, rank_t, pstart, T=512):
    N = eidx_t.shape[1]
    E = pstart.shape[0]
    blk = pl.BlockSpec((TOP_K, T), lambda i: (0, i))
    return pl.pallas_call(
        _dest_kernel,
        grid=(N // T,),
        in_specs=[blk, blk, pl.BlockSpec((E, LANES), lambda i: (0, 0))],
        out_specs=blk,
        out_shape=jax.ShapeDtypeStruct((TOP_K, N), I32),
        compiler_params=_cp("parallel"),
        name="dests",
    )(eidx_t, rank_t, jnp.broadcast_to(pstart.astype(F32)[:, None], (E, LANES)))


def _expert_kernel(us_ref, ps_ref, x_hbm, wug_ref, wd_ref, y_hbm,
                   wug_bf, wd_bf, xbuf, ybuf, cnt_ref, xsem, ysem, *, n_rows):
    e = pl.program_id(0)
    _, _, R, Ch = xbuf.shape
    C = 2 * Ch
    F = wd_ref.shape[0]

    def x_copy(slot, half, row):
        return pltpu.make_async_copy(x_hbm.at[half, pl.ds(row, R)], xbuf.at[slot, half], xsem.at[slot])

    def y_copy(slot, half, row):
        return pltpu.make_async_copy(ybuf.at[slot, half], y_hbm.at[half, pl.ds(row, R)], ysem.at[slot])

    def start(copy, slot, row):
        for half in range(2):
            copy(slot, half, row).start()

    def wait(copy, slot):
        for half in range(2):
            copy(slot, half, 0).wait()

    @pl.when(e == 0)
    def _():
        cnt_ref[0] = 0
        start(x_copy, 0, 0)
        ybuf[...] = jnp.zeros_like(ybuf)
        for slot in range(2):
            start(y_copy, slot, n_rows + slot * R)

    wug_bf[...] = wug_ref[...].astype(BF16)
    wd_bf[...] = wd_ref[...].astype(BF16)
    n_valid = us_ref[e + 1] - us_ref[e]
    pbase = ps_ref[e]
    n_tiles = lax.shift_right_logical(n_valid + (R - 1), int(math.log2(R)))
    rowid = lax.broadcasted_iota(I32, (R, C), 0)

    def tile(i, carry):
        n_done = cnt_ref[0]
        slot = n_done & 1
        start(x_copy, 1 - slot, pl.multiple_of(pbase + (i + 1) * R, R))
        wait(x_copy, slot)
        x = jnp.concatenate([xbuf[slot, 0], xbuf[slot, 1]], axis=1)
        x = jnp.where(rowid < n_valid - i * R, x, jnp.uint32(0))
        lo, hi = _unpack_halves(x)
        gu = (jnp.dot(lo.astype(BF16), wug_bf[0:C, :], preferred_element_type=F32)
              + jnp.dot(hi.astype(BF16), wug_bf[C:, :], preferred_element_type=F32))
        hid = (_silu(gu[:, :F]) * gu[:, F:]).astype(BF16)
        y = jnp.dot(hid, wd_bf[...], preferred_element_type=F32)
        wait(y_copy, slot)
        packed = _pack_halves(y.astype(BF16).astype(F32))
        ybuf[slot, 0] = packed[:, :Ch]
        ybuf[slot, 1] = packed[:, Ch:]
        start(y_copy, slot, pl.multiple_of(pbase + i * R, R))
        cnt_ref[0] = n_done + 1
        return carry

    lax.fori_loop(0, n_tiles, tile, 0)

    @pl.when(e == pl.num_programs(0) - 1)
    def _():
        wait(x_copy, cnt_ref[0] & 1)
        for slot in range(2):
            wait(y_copy, slot)
        ybuf[0] = jnp.zeros((2, R, Ch), U32)
        first = lax.shift_right_logical(pbase + n_tiles * R, int(math.log2(R)))
        n_left = n_rows // R - first

        def fill(t, carry):
            start(y_copy, 0, pl.multiple_of((first + t) * R, R))
            return carry

        def drain(t, carry):
            wait(y_copy, 0)
            return carry

        lax.fori_loop(0, n_left, fill, 0)
        lax.fori_loop(0, n_left, drain, 0)


def _experts(ustart, pstart, xg, w_ug, w_d, n_rows, R):
    _, _, Ch = xg.shape
    E, D, F2 = w_ug.shape
    F = w_d.shape[1]
    grid_spec = pltpu.PrefetchScalarGridSpec(
        num_scalar_prefetch=2,
        grid=(E,),
        in_specs=[pl.BlockSpec(memory_space=pl.ANY),
                  pl.BlockSpec((None, D, F2), lambda e, us, ps: (e, 0, 0)),
                  pl.BlockSpec((None, F, D), lambda e, us, ps: (e, 0, 0))],
        out_specs=pl.BlockSpec(memory_space=pl.ANY),
        scratch_shapes=[pltpu.VMEM((D, F2), BF16), pltpu.VMEM((F, D), BF16),
                        pltpu.VMEM((2, 2, R, Ch), U32), pltpu.VMEM((2, 2, R, Ch), U32), pltpu.SMEM((1,), I32),
                        pltpu.SemaphoreType.DMA((2,)), pltpu.SemaphoreType.DMA((2,))],
    )
    return pl.pallas_call(
        functools.partial(_expert_kernel, n_rows=n_rows),
        grid_spec=grid_spec,
        out_shape=jax.ShapeDtypeStruct((2, n_rows + 2 * R, Ch), U32),
        compiler_params=_cp("arbitrary"),
        name="experts",
    )(ustart, pstart, xg, w_ug, w_d)


def _sc_scatter_rows(src, idx, n_rows):
    H, N, C = src.shape
    K = idx.shape[0]
    per_row = N // SC_WINDOW
    mesh = plsc.VectorSubcoreMesh(core_axis_name="c", subcore_axis_name="s")

    @functools.partial(pl.kernel, out_type=jax.ShapeDtypeStruct((H, n_rows, C), src.dtype), mesh=mesh,
                       scratch_types=[])
    def scatter_kernel(x_hbm, i_hbm, o_hbm):
        for h in range(H):
            def body(x_vmem, i_vmem):
                pltpu.sync_copy(x_vmem, o_hbm.at[h].at[i_vmem.at[0]])

            pltpu.emit_pipeline(
                body,
                grid=(K * per_row,),
                in_specs=[pl.BlockSpec((SC_WINDOW, C), lambda i: (i % per_row, 0)),
                          pl.BlockSpec((1, SC_WINDOW), lambda i: (i // per_row, i % per_row))],
                out_specs=[],
                core_axis_name=("c", "s"),
                dimension_semantics=(pltpu.PARALLEL,),
            )(x_hbm.at[h], i_hbm)

    return scatter_kernel(src, idx)


def _sc_gather_rows(src, idx):
    H, _, C = src.shape
    K, N = idx.shape
    per_row = N // SC_WINDOW
    mesh = plsc.VectorSubcoreMesh(core_axis_name="c", subcore_axis_name="s")

    @functools.partial(pl.kernel, out_type=jax.ShapeDtypeStruct((H, K * N, C), src.dtype), mesh=mesh,
                       scratch_types=[])
    def gather_kernel(x_hbm, i_hbm, o_hbm):
        for h in range(H):
            def body(i_vmem, o_vmem):
                pltpu.sync_copy(x_hbm.at[h].at[i_vmem.at[0]], o_vmem)

            pltpu.emit_pipeline(
                body,
                grid=(K * per_row,),
                in_specs=[pl.BlockSpec((1, SC_WINDOW), lambda i: (i // per_row, i % per_row))],
                out_specs=[pl.BlockSpec((SC_WINDOW, C), lambda i: (i, 0))],
                core_axis_name=("c", "s"),
                dimension_semantics=(pltpu.PARALLEL,),
            )(i_hbm, o_hbm.at[h])

    return gather_kernel(src, idx)


def _combine_kernel(*refs):
    y_refs = refs[:2 * TOP_K]
    w_ref, base_ref, mod_ref, o_ref = refs[2 * TOP_K:]
    T = base_ref.shape[0]
    tr = lax.broadcasted_iota(I32, (T, T), 0)
    tc = lax.broadcasted_iota(I32, (T, T), 1)
    wcol = _nt((tr == tc).astype(F32), w_ref[...], precision=HI)
    acc = [None] * 4
    for kk in range(TOP_K):
        wk = wcol[:, kk:kk + 1]
        for half in range(2):
            lo, hi = _unpack_halves(y_refs[2 * kk + half][...])
            for q, val in ((half, lo), (2 + half, hi)):
                acc[q] = val * wk if acc[q] is None else acc[q] + val * wk
    o_ref[...] = base_ref[...] + mod_ref[5:6, :] * jnp.concatenate(acc, axis=1)


def _combine(yg, w_t, base, mod, S, T=256):
    N, D = base.shape
    C = yg.shape[2]
    tpb = S // T
    n_tiles = N // T
    row = pl.BlockSpec((T, D), lambda i: (i, 0))
    piece = lambda kk, half: pl.BlockSpec((None, T, C), lambda i: (half, kk * n_tiles + i, 0))
    return pl.pallas_call(
        _combine_kernel,
        grid=(n_tiles,),
        in_specs=[piece(kk, half) for kk in range(TOP_K) for half in range(2)] + [
            pl.BlockSpec((TOP_K, T), lambda i: (0, i)),
            row,
            pl.BlockSpec((None, 6, D), lambda i: (i // tpb, 0, 0))],
        out_specs=row,
        out_shape=jax.ShapeDtypeStruct((N, D), F32),
        compiler_params=_cp("parallel"),
        name="combine",
    )(*([yg] * (2 * TOP_K)), w_t, base, mod)


def _layer(x, c, positions, layer_idx, w_ada, b_ada, norm1_g, w_in, w_gate, b_gate,
           q_norm_g, k_norm_g, lambda_q1, lambda_k1, lambda_q2, lambda_k2, subln_g,
           rwkv_mu, w_decay0, w_decay2, a0, a2, g2, k_k, k_a, r_k, ln_x_w, ln_x_b,
           w_branch_a, w_branch_b, w_out, norm2_g, w_router, router_bias,
           w_expert_up_gate, w_expert_down, w_shared_up_gate, w_shared_down):
    B, S, D = x.shape
    N = B * S
    E = w_router.shape[1]
    da_width = w_branch_a.shape[0]
    rw_width = w_branch_b.shape[0]
    rw_cols = rwkv_mu.shape[0]
    lambda_init = 0.8 - 0.6 * math.exp(-0.3 * layer_idx)

    mod = _adaln(c, w_ada, b_ada)
    x2 = x.reshape(N, D)
    w_cat = jnp.concatenate([w_in, w_gate], axis=1).astype(BF16)
    q, k, v, prw, gate = _inproj(x2, mod, norm1_g, w_cat, b_gate, S, da_width, rw_cols)

    qn, kn = _qkprep(q, k, positions.reshape(N, 1), q_norm_g, k_norm_g)
    lam_vecs = jnp.stack([lambda_q1, lambda_k1, lambda_q2, lambda_k2])
    score_bound = 1.01 * DA_HEAD_DIM ** 0.5 * jnp.max(jnp.abs(q_norm_g)) * jnp.max(jnp.abs(k_norm_g))
    attn = _diff_attention(qn, kn, v, score_bound, lam_vecs, subln_g, B, S, lambda_init)

    r_, k_, v_, a_, ld_, g_ = _rwkv_prep(prw.reshape(B, S, rw_cols), rwkv_mu, w_decay0, w_decay2,
                                         a0, a2, g2, rw_width)
    rw = _rwkv_scan(r_, k_, v_, a_, ld_, g_, k_k, k_a, r_k.reshape(-1), ln_x_w, ln_x_b).reshape(N, rw_width)

    base, h2p, logits_t = _post(attn, rw, gate, x2, mod, w_branch_a.astype(BF16), w_branch_b.astype(BF16),
                                w_out.astype(BF16), norm2_g, w_router.T,
                                w_shared_up_gate.astype(BF16), w_shared_down.astype(BF16), S)

    eidx_t, w_t = _route(logits_t, router_bias)
    rank_t, counts = _ranks(eidx_t, E)
    R = EXPERT_TILE
    cnt = counts[:, 0].astype(I32)
    ustart = jnp.concatenate([jnp.zeros((1,), I32), jnp.cumsum(cnt)])
    pcnt = (cnt + R - 1) // R * R
    pstart = jnp.cumsum(pcnt) - pcnt
    dest_p = _dests(eidx_t, rank_t, pstart)
    n_rows = (N * TOP_K + E * (R - 1) + R - 1) // R * R
    xg = _sc_scatter_rows(h2p, dest_p, n_rows + R)
    y = _experts(ustart, pstart, xg, w_expert_up_gate, w_expert_down, n_rows, R)
    yg = _sc_gather_rows(y, dest_p)
    out = _combine(yg, w_t, base, mod, S)
    return out.reshape(B, S, D)


def kernel(x, c, positions, w_ada, b_ada, norm1_g, w_in, w_gate, b_gate, q_norm_g, k_norm_g, lambda_q1, lambda_k1, lambda_q2, lambda_k2, subln_g, rwkv_mu, w_decay0, w_decay2, a0, a2, g2, k_k, k_a, r_k, ln_x_w, ln_x_b, w_branch_a, w_branch_b, w_out, norm2_g, w_router, router_bias, w_expert_up_gate, w_expert_down, w_shared_up_gate, w_shared_down):
    for l in range(w_ada.shape[0]):
        x = _layer(x, c, positions, l, w_ada[l], b_ada[l], norm1_g[l], w_in[l], w_gate[l], b_gate[l],
                   q_norm_g[l], k_norm_g[l], lambda_q1[l], lambda_k1[l], lambda_q2[l], lambda_k2[l],
                   subln_g[l], rwkv_mu[l], w_decay0[l], w_decay2[l], a0[l], a2[l], g2[l], k_k[l],
                   k_a[l], r_k[l], ln_x_w[l], ln_x_b[l], w_branch_a[l], w_branch_b[l], w_out[l],
                   norm2_g[l], w_router[l], router_bias[l], w_expert_up_gate[l], w_expert_down[l],
                   w_shared_up_gate[l], w_shared_down[l])
    return x
```

```python
import functools
import math

import jax
import jax.numpy as jnp
from jax import lax
from jax.experimental import pallas as pl
from jax.experimental.pallas import tpu as pltpu
from jax.experimental.pallas import tpu_sc as plsc

F32 = jnp.float32
BF16 = jnp.bfloat16
I32 = jnp.int32
U32 = jnp.uint32
HI = lax.Precision.HIGHEST

CHUNK = 64
ROPE_THETA = 10000.0
NORM_EPS = 1e-6
SUBLN_EPS = 1e-5
DA_HEAD_DIM = 64
RWKV_HEAD = 64
GN_EPS = 64e-5
TOP_K = 8
N_GROUPS = 8
TOPK_GROUPS = 4
ROUTED_SCALE = 2.5
EXPERT_TILE = 256
RWKV_CHUNK = 64
LANES = 128
SC_WINDOW = 128
NEG = -1e30
MAX_PLAIN_SCORE = 40.0
VMEM_LIMIT = 56 * 1024 * 1024


def _cp(*sem):
    return pltpu.CompilerParams(dimension_semantics=sem, vmem_limit_bytes=VMEM_LIMIT)


def _bdot(a, b):
    return jnp.dot(a.astype(BF16), b.astype(BF16), preferred_element_type=F32)


def _fdot(a, b):
    return jnp.dot(a, b, precision=HI, preferred_element_type=F32)


def _nt(a, b, precision=None):
    return lax.dot_general(a, b, (((1,), (1,)), ((), ())), precision=precision,
                           preferred_element_type=F32)


def _tn(a, b, precision=None):
    return lax.dot_general(a, b, (((0,), (0,)), ((), ())), precision=precision,
                           preferred_element_type=F32)


def _pack_halves(x):
    c = x.shape[1] // 2
    lo = lax.bitcast_convert_type(x[:, :c], U32)
    hi = lax.bitcast_convert_type(x[:, c:], U32)
    return (hi & jnp.uint32(0xFFFF0000)) | (lo >> 16)


def _unpack_halves(w):
    lo = lax.bitcast_convert_type(w << 16, F32)
    hi = lax.bitcast_convert_type(w & jnp.uint32(0xFFFF0000), F32)
    return lo, hi


def _sigmoid(x):
    return 1.0 / (1.0 + jnp.exp(-x))


def _silu(x):
    return x * _sigmoid(x)


def _ada_kernel(c_ref, w_ref, b_ref, o_ref):
    o_ref[...] = _fdot(_silu(c_ref[...]), w_ref[...]) + b_ref[...]


def _adaln(c, w_ada, b_ada):
    B, D = c.shape
    rows = -(-B // 8) * 8
    cpad = jnp.zeros((rows, D), F32).at[:B].set(c)
    n_out = w_ada.shape[1]
    out = pl.pallas_call(
        _ada_kernel,
        grid=(n_out // D,),
        in_specs=[pl.BlockSpec((rows, D), lambda j: (0, 0)),
                  pl.BlockSpec((D, D), lambda j: (0, j)),
                  pl.BlockSpec((1, D), lambda j: (0, j))],
        out_specs=pl.BlockSpec((rows, D), lambda j: (0, j)),
        out_shape=jax.ShapeDtypeStruct((rows, n_out), F32),
        compiler_params=_cp("arbitrary"),
        name="adaln",
    )(cpad, w_ada, b_ada.reshape(1, n_out))
    return out[:B].reshape(B, n_out // D, D)


def _inproj_kernel(x_ref, mod_ref, g_ref, w_ref, bg_ref, q_ref, k_ref, v_ref, rw_ref, gate_ref,
                   *, da_width, rw_cols):
    x = x_ref[...]
    y = x * lax.rsqrt(jnp.mean(x * x, axis=-1, keepdims=True) + NORM_EPS) * g_ref[...]
    h = (y * (1.0 + mod_ref[1:2, :]) + mod_ref[0:1, :]).astype(BF16)
    c = 0
    for ref in (q_ref, k_ref, v_ref):
        ref[...] = jnp.dot(h, w_ref[:, c:c + da_width], preferred_element_type=F32).astype(ref.dtype)
        c += da_width
    step = 512
    for o in range(0, rw_cols, step):
        wd = min(step, rw_cols - o)
        rw_ref[:, o:o + wd] = jnp.dot(h, w_ref[:, c + o:c + o + wd], preferred_element_type=F32)
    c += rw_cols
    n_gate = gate_ref.shape[1]
    for o in range(0, n_gate, step):
        z = jnp.dot(h, w_ref[:, c + o:c + o + step], preferred_element_type=F32) + bg_ref[:, o:o + step]
        gate_ref[:, o:o + step] = _sigmoid(z)


def _inproj(x2, mod, norm1_g, w_cat, b_gate, S, da_width, rw_cols, tm=256):
    N, D = x2.shape
    n_gate = b_gate.shape[0]
    tpb = S // tm
    kern = functools.partial(_inproj_kernel, da_width=da_width, rw_cols=rw_cols)
    row = lambda w: pl.BlockSpec((tm, w), lambda i: (i, 0))
    return pl.pallas_call(
        kern,
        grid=(N // tm,),
        in_specs=[row(D),
                  pl.BlockSpec((None, 6, D), lambda i: (i // tpb, 0, 0)),
                  pl.BlockSpec((1, D), lambda i: (0, 0)),
                  pl.BlockSpec(w_cat.shape, lambda i: (0, 0)),
                  pl.BlockSpec((1, n_gate), lambda i: (0, 0))],
        out_specs=[row(da_width), row(da_width), row(da_width), row(rw_cols), row(n_gate)],
        out_shape=[jax.ShapeDtypeStruct((N, da_width), F32),
                   jax.ShapeDtypeStruct((N, da_width), F32),
                   jax.ShapeDtypeStruct((N, da_width), BF16),
                   jax.ShapeDtypeStruct((N, rw_cols), F32),
                   jax.ShapeDtypeStruct((N, n_gate), F32)],
        compiler_params=_cp("parallel"),
        name="inproj",
    )(x2, mod, norm1_g.reshape(1, D), w_cat, b_gate.reshape(1, n_gate))


def _qkprep_kernel(q_ref, k_ref, pos_ref, invf_ref, qg_ref, kg_ref, qo_ref, ko_ref, *, scale):
    tm = q_ref.shape[0]
    lane = lax.broadcasted_iota(I32, (tm, LANES), 1)
    first = lane < DA_HEAD_DIM
    lo_half = (lane & (DA_HEAD_DIM - 1)) < DA_HEAD_DIM // 2
    ang = pos_ref[...].astype(F32) * invf_ref[...]
    cos = jnp.cos(ang)
    sin = jnp.sin(ang)
    sin = jnp.where(lo_half, -sin, sin)

    def one(src, dst, g_ref, mult):
        for blk in range(src.shape[1] // LANES):
            x = src[:, blk * LANES:(blk + 1) * LANES]
            xx = x * x
            s_all = jnp.sum(xx, axis=-1, keepdims=True)
            s_first = jnp.sum(jnp.where(first, xx, 0.0), axis=-1, keepdims=True)
            ms = jnp.where(first, s_first, s_all - s_first) * (1.0 / DA_HEAD_DIM)
            xn = x * lax.rsqrt(ms + NORM_EPS) * g_ref[...]
            rot = jnp.where(lo_half, pltpu.roll(xn, LANES - DA_HEAD_DIM // 2, axis=1),
                            pltpu.roll(xn, DA_HEAD_DIM // 2, axis=1))
            dst[:, blk * LANES:(blk + 1) * LANES] = ((xn * cos + rot * sin) * mult).astype(dst.dtype)

    one(q_ref, qo_ref, qg_ref, scale)
    one(k_ref, ko_ref, kg_ref, 1.0)


def _qkprep(q, k, pos2, q_norm_g, k_norm_g, tm=512):
    N, W = q.shape
    d = DA_HEAD_DIM
    inv_freq = 1.0 / (ROPE_THETA ** (jnp.arange(0, d, 2, dtype=F32) / d))
    invf = jnp.tile(inv_freq, LANES // (d // 2)).reshape(1, LANES)
    row = pl.BlockSpec((tm, W), lambda i: (i, 0))
    vec = pl.BlockSpec((1, LANES), lambda i: (0, 0))
    return pl.pallas_call(
        functools.partial(_qkprep_kernel, scale=d ** -0.5 * math.log2(math.e)),
        grid=(N // tm,),
        in_specs=[row, row, pl.BlockSpec((tm, 1), lambda i: (i, 0)), vec, vec, vec],
        out_specs=[row, row],
        out_shape=[jax.ShapeDtypeStruct((N, W), BF16)] * 2,
        compiler_params=_cp("parallel"),
        name="qkprep",
    )(q, k, pos2, invf, jnp.tile(q_norm_g, 2).reshape(1, LANES), jnp.tile(k_norm_g, 2).reshape(1, LANES))


def _attn_kernel(flag_ref, q_ref, k_ref, v_ref, lam_ref, sg_ref, o_ref, qz_ref, m_ref, l_ref, lp_ref, acc_ref,
                 *, tq, lambda_init):
    i = pl.program_id(2)
    lane = lax.broadcasted_iota(I32, (tq, LANES), 1)
    q = q_ref[...]
    zero = jnp.zeros_like(q)
    qz_ref[0:tq, :] = jnp.where(lane < DA_HEAD_DIM, q, zero)
    qz_ref[tq:, :] = jnp.where(lane >= DA_HEAD_DIM, q, zero)
    acc_ref[...] = jnp.zeros_like(acc_ref)
    bounded = flag_ref[0] == 1

    def scores(j, masked):
        off = pl.multiple_of(j * tq, tq)
        s = _nt(qz_ref[...], k_ref[pl.ds(off, tq), :])
        if masked:
            row = lax.broadcasted_iota(I32, s.shape, 0)
            col = lax.broadcasted_iota(I32, s.shape, 1)
            s = jnp.where((col // CHUNK) <= ((row & (tq - 1)) // CHUNK), s, NEG)
        return s, off

    def plain_step(j, masked):
        s, off = scores(j, masked)
        pr = jnp.exp2(s)
        part = pr[:, 0:LANES]
        for cblk in range(1, tq // LANES):
            part = part + pr[:, cblk * LANES:(cblk + 1) * LANES]
        lp_ref[...] += part
        acc_ref[...] += jnp.dot(pr.astype(BF16), v_ref[pl.ds(off, tq), :], preferred_element_type=F32)

    def online_step(j, masked):
        s, off = scores(j, masked)
        m_old = m_ref[...]
        m_new = jnp.maximum(m_old, jnp.max(s, axis=-1, keepdims=True))
        alpha = jnp.exp2(m_old - m_new)
        pr = jnp.exp2(s - m_new)
        l_ref[...] = alpha * l_ref[...] + jnp.sum(pr, axis=-1, keepdims=True)
        acc_ref[...] = alpha * acc_ref[...] + jnp.dot(pr.astype(BF16), v_ref[pl.ds(off, tq), :],
                                                      preferred_element_type=F32)
        m_ref[...] = m_new

    def run(step):
        def body(j, carry):
            step(j, False)
            return carry
        lax.fori_loop(0, i, body, 0)
        step(i, True)

    @pl.when(bounded)
    def _():
        lp_ref[...] = jnp.zeros_like(lp_ref)
        run(plain_step)
        l_ref[...] = jnp.sum(lp_ref[...], axis=-1, keepdims=True)

    @pl.when(jnp.logical_not(bounded))
    def _():
        m_ref[...] = jnp.full_like(m_ref, NEG)
        l_ref[...] = jnp.zeros_like(l_ref)
        run(online_step)

    lv = lam_ref[...]
    lam = (jnp.exp(jnp.sum(lv[0:1] * lv[1:2], keepdims=True))
           - jnp.exp(jnp.sum(lv[2:3] * lv[3:4], keepdims=True)) + lambda_init)
    o1 = acc_ref[0:tq, :] / l_ref[0:tq, :]
    o2 = acc_ref[tq:, :] / l_ref[tq:, :]
    o = o1 - lam * o2
    o = o * lax.rsqrt(jnp.mean(o * o, axis=-1, keepdims=True) + SUBLN_EPS) * sg_ref[...]
    o_ref[...] = (o * (1.0 - lambda_init)).astype(o_ref.dtype)


def _diff_attention(qn, kn, v, score_bound, lam_vecs, subln_g, B, S, lambda_init, tq=512):
    W = qn.shape[1]
    H = W // LANES
    q3 = qn.reshape(B, S, W)
    k3 = kn.reshape(B, S, W)
    v3 = v.reshape(B, S, W)
    flag = (score_bound <= MAX_PLAIN_SCORE).astype(I32).reshape(1)
    qblk = pl.BlockSpec((None, tq, LANES), lambda b, h, i, f: (b, i, h))
    kvblk = pl.BlockSpec((None, S, LANES), lambda b, h, i, f: (b, 0, h))
    grid_spec = pltpu.PrefetchScalarGridSpec(
        num_scalar_prefetch=1,
        grid=(B, H, S // tq),
        in_specs=[qblk, kvblk, kvblk,
                  pl.BlockSpec((4, DA_HEAD_DIM), lambda b, h, i, f: (0, 0)),
                  pl.BlockSpec((1, LANES), lambda b, h, i, f: (0, 0))],
        out_specs=qblk,
        scratch_shapes=[pltpu.VMEM((2 * tq, LANES), BF16),
                        pltpu.VMEM((2 * tq, 1), F32),
                        pltpu.VMEM((2 * tq, 1), F32),
                        pltpu.VMEM((2 * tq, LANES), F32),
                        pltpu.VMEM((2 * tq, LANES), F32)],
    )
    out = pl.pallas_call(
        functools.partial(_attn_kernel, tq=tq, lambda_init=lambda_init),
        grid_spec=grid_spec,
        out_shape=jax.ShapeDtypeStruct((B, S, W), BF16),
        compiler_params=_cp("parallel", "parallel", "arbitrary"),
        name="diff_attn",
    )(flag, q3, k3, v3, lam_vecs, subln_g.reshape(1, LANES))
    return out.reshape(B * S, W)


def _rwkv_prep_kernel(p_ref, prev_ref, mu_ref, w0_ref, w2_ref, a0_ref, a2_ref, g2_ref,
                      r_ref, k_ref, v_ref, a_ref, ld_ref, g_ref, *, width):
    i = pl.program_id(1)
    p = p_ref[...]
    last_prev = jnp.where(i > 0, prev_ref[7:8, :], 0.0)
    rowi = lax.broadcasted_iota(I32, p.shape, 0)
    prev = jnp.where(rowi == 0, last_prev, pltpu.roll(p, 1, axis=0))
    xs = p + (prev - p) * mu_ref[...]
    r_ref[...] = xs[:, 0:width]
    k_ref[...] = xs[:, width:2 * width]
    v_ref[...] = xs[:, 2 * width:3 * width]
    xwa = xs[:, 3 * width:3 * width + LANES]
    xg = xs[:, 3 * width + LANES:]
    z = w0_ref[...] + _fdot(jnp.tanh(xwa), w2_ref[...])
    w = -(jnp.maximum(-z, 0.0) + jnp.log(1.0 + jnp.exp(-jnp.abs(z)))) - 0.5
    ld_ref[...] = -jnp.exp(w)
    a_ref[...] = _sigmoid(a0_ref[...] + _fdot(xwa, a2_ref[...]))
    g_ref[...] = _fdot(_sigmoid(xg), g2_ref[...])


def _rwkv_prep(prw3, mu, w_decay0, w_decay2, a0, a2, g2, width, tm=256):
    B, S, C = prw3.shape
    dl, al = w_decay2.shape[0], a2.shape[0]
    assert dl + al == LANES and g2.shape[0] == LANES
    w2p = jnp.zeros((LANES, width), F32).at[:dl].set(w_decay2)
    a2p = jnp.zeros((LANES, width), F32).at[dl:].set(a2)
    vec = lambda n: pl.BlockSpec((1, n), lambda b, i: (0, 0))
    mat = pl.BlockSpec((LANES, width), lambda b, i: (0, 0))
    out = pl.BlockSpec((None, tm, width), lambda b, i: (b, i, 0))
    return pl.pallas_call(
        functools.partial(_rwkv_prep_kernel, width=width),
        grid=(B, S // tm),
        in_specs=[pl.BlockSpec((None, tm, C), lambda b, i: (b, i, 0)),
                  pl.BlockSpec((None, 8, C), lambda b, i: (b, jnp.maximum(i * (tm // 8) - 1, 0), 0)),
                  vec(C), vec(width), mat, vec(width), mat, mat],
        out_specs=[out] * 6,
        out_shape=[jax.ShapeDtypeStruct((B, S, width), F32)] * 6,
        compiler_params=_cp("parallel", "parallel"),
        name="rwkv_prep",
    )(prw3, prw3, mu.reshape(1, C), w_decay0.reshape(1, width), w2p, a0.reshape(1, width), a2p, g2)


def _stackmask(m):
    lane = lax.broadcasted_iota(I32, m.shape, 1)
    z = jnp.zeros_like(m)
    return jnp.concatenate([jnp.where(lane < RWKV_HEAD, m, z), jnp.where(lane >= RWKV_HEAD, m, z)], axis=0)


def _pair_sum(x, first):
    s1 = jnp.sum(jnp.where(first, x, 0.0), axis=-1, keepdims=True)
    s2 = jnp.sum(jnp.where(first, 0.0, x), axis=-1, keepdims=True)
    return jnp.where(first, s1, s2)


def _rwkv_scan_kernel(r_ref, k_ref, v_ref, a_ref, ld_ref, g_ref, kk_ref, ka_ref, rk_ref, lnw_ref, lnb_ref,
                      o_ref, s_ref, *, L):
    tm, W = r_ref.shape
    n_chunks = tm // L
    n_pairs = W // LANES
    hd = RWKV_HEAD
    bf = lambda t: t.astype(BF16)

    @pl.when(pl.program_id(1) == 0)
    def _():
        s_ref[...] = jnp.zeros_like(s_ref)

    row = lax.broadcasted_iota(I32, (tm, tm), 0)
    col = lax.broadcasted_iota(I32, (tm, tm), 1)
    tri = jnp.where(jnp.logical_and(col <= row, (col // L) == (row // L)), 1.0, 0.0).astype(BF16)
    ld = ld_ref[...]
    ld_hi = bf(ld)
    rem = ld - ld_hi.astype(F32)
    ld_mid = bf(rem)
    ld_lo = bf(rem - ld_mid.astype(F32))
    c = (jnp.dot(tri, ld_hi, preferred_element_type=F32) + jnp.dot(tri, ld_mid, preferred_element_type=F32)
         + jnp.dot(tri, ld_lo, preferred_element_type=F32))
    ec = jnp.exp(c)
    eci = jnp.exp(-c)
    ecm = jnp.exp(c - ld)
    r = r_ref[...]
    k = k_ref[...]
    v = v_ref[...]
    a = a_ref[...]
    kkr = k * kk_ref[...]
    kmod = k * (1.0 + (a - 1.0) * ka_ref[...])
    brk = r * kmod * rk_ref[...]

    lane = lax.broadcasted_iota(I32, (L, LANES), 1)
    rowl = lax.broadcasted_iota(I32, (L, LANES), 0)
    first = lane < hd
    lane_h = lane & (hd - 1)
    strict = lane_h < rowl
    incl = lane_h <= rowl
    eye = jnp.where(lane_h == rowl, 1.0, 0.0)

    chains = [(ch, p) for ch in range(n_chunks) for p in range(n_pairs)]
    rsl = lambda ch: slice(ch * L, (ch + 1) * L)
    csl = lambda p: slice(p * LANES, (p + 1) * LANES)
    fdot = lambda x, y: jnp.dot(x, y, preferred_element_type=F32)
    at, bt, kt, rt, vh, g_l = {}, {}, {}, {}, {}, {}
    for c_ in chains:
        ch, p = c_
        rs, cs = rsl(ch), csl(p)
        kkh = kkr[rs, cs]
        kkh = kkh / jnp.maximum(jnp.sqrt(_pair_sum(kkh * kkh, first)), 1e-12)
        vh[c_] = v[rs, cs]
        g_l[c_] = ec[ch * L + L - 1:ch * L + L, cs]
        at[c_] = -kkh * ecm[rs, cs]
        bt[c_] = kkh * a[rs, cs] * eci[rs, cs]
        kt[c_] = kmod[rs, cs] * eci[rs, cs]
        rt[c_] = r[rs, cs] * ec[rs, cs]
    gm = {c_: _nt(bf(jnp.concatenate([at[c_], rt[c_]], axis=0)),
                  jnp.concatenate([_stackmask(bf(bt[c_])), _stackmask(bf(kt[c_]))], axis=0)) for c_ in chains}
    a_ab = {c_: jnp.where(strict, gm[c_][:L, :LANES], 0.0) for c_ in chains}
    vsm = {c_: _stackmask(bf(vh[c_])) for c_ in chains}
    cmat = {c_: fdot(bf(jnp.where(strict, gm[c_][:L, LANES:], 0.0)), vsm[c_]) for c_ in chains}
    t_inv = {c_: eye + a_ab[c_] for c_ in chains}
    pw = {c_: bf(a_ab[c_]) for c_ in chains}
    for _ in range(int(math.log2(L)) - 1):
        pw = {c_: bf(fdot(pw[c_], _stackmask(pw[c_]))) for c_ in chains}
        t_inv = {c_: t_inv[c_] + fdot(pw[c_], _stackmask(bf(t_inv[c_]))) for c_ in chains}
    zz = {c_: fdot(bf(t_inv[c_]), jnp.concatenate([_stackmask(bf(at[c_])), _stackmask(bf(cmat[c_]))], axis=1))
          for c_ in chains}
    qy = {c_: fdot(bf(jnp.where(incl, gm[c_][L:, :LANES], 0.0)),
                   jnp.concatenate([_stackmask(bf(zz[c_][:, :LANES])), _stackmask(bf(zz[c_][:, LANES:]))], axis=1))
          for c_ in chains}
    y0 = {c_: qy[c_][:, LANES:] + fdot(bf(jnp.where(incl, gm[c_][L:, LANES:], 0.0)), vsm[c_]) for c_ in chains}
    qa = {c_: bf(jnp.concatenate([rt[c_] + qy[c_][:, :LANES], zz[c_][:, :LANES]], axis=0)) for c_ in chains}
    bkg = {c_: bf(jnp.concatenate([bt[c_] * g_l[c_], kt[c_] * g_l[c_]], axis=0)) for c_ in chains}

    lane_s = lax.broadcasted_iota(I32, (hd, LANES), 1)
    sp = [s_ref[p] for p in range(n_pairs)]
    for ch in range(n_chunks):
        rs = rsl(ch)
        yw = [_nt(qa[ch, p], _stackmask(bf(sp[p]))) for p in range(n_pairs)]
        upd = [_tn(bf(jnp.concatenate([yw[p][L:] + zz[ch, p][:, LANES:], vh[ch, p]], axis=0)), bkg[ch, p])
               for p in range(n_pairs)]
        for p in range(n_pairs):
            cs = csl(p)
            sp[p] = sp[p] * g_l[ch, p] + jnp.where(lane_s < hd, upd[p][:hd], upd[p][hd:])
            y = yw[p][:L] + y0[ch, p]
            mean = _pair_sum(y, first) * (1.0 / hd)
            yc = y - mean
            var = _pair_sum(yc * yc, first) * (1.0 / hd)
            yn = yc * lax.rsqrt(var + GN_EPS) * lnw_ref[:, cs] + lnb_ref[:, cs]
            bonus = _pair_sum(brk[rs, cs], first) * vh[ch, p]
            o_ref[rs, cs] = ((yn + bonus) * g_ref[rs, cs]).astype(o_ref.dtype)
    for p in range(n_pairs):
        s_ref[p] = sp[p]


def _rwkv_scan(r, k, v, a, ld, g, k_k, k_a, r_k, ln_w, ln_b, L=RWKV_CHUNK, tm=256):
    B, S, W = r.shape
    seq = pl.BlockSpec((None, tm, W), lambda b, c: (b, c, 0))
    vec = pl.BlockSpec((1, W), lambda b, c: (0, 0))
    return pl.pallas_call(
        functools.partial(_rwkv_scan_kernel, L=L),
        grid=(B, S // tm),
        in_specs=[seq] * 6 + [vec] * 5,
        out_specs=seq,
        out_shape=jax.ShapeDtypeStruct((B, S, W), BF16),
        scratch_shapes=[pltpu.VMEM((W // LANES, RWKV_HEAD, LANES), F32)],
        compiler_params=_cp("parallel", "arbitrary"),
        name="rwkv_scan",
    )(r, k, v, a, ld, g, k_k.reshape(1, W), k_a.reshape(1, W), r_k.reshape(1, W),
      ln_w.reshape(1, W), ln_b.reshape(1, W))


def _post_kernel(attn_ref, rw_ref, gate_ref, x_ref, mod_ref, wa_ref, wb_ref, wo_ref, g2_ref, wr_ref,
                 sug_ref, sd_ref, base_ref, h2p_ref, lg_ref):
    D = x_ref.shape[1]
    ya = jnp.dot(attn_ref[...], wa_ref[...], preferred_element_type=F32)
    yb = jnp.dot(rw_ref[...], wb_ref[...], preferred_element_type=F32)
    m = gate_ref[:, 0:D] * ya + gate_ref[:, D:] * yb
    x1 = x_ref[...] + mod_ref[2:3, :] * jnp.dot(m.astype(BF16), wo_ref[...], preferred_element_type=F32)
    y = x1 * lax.rsqrt(jnp.mean(x1 * x1, axis=-1, keepdims=True) + NORM_EPS) * g2_ref[...]
    h2 = y * (1.0 + mod_ref[4:5, :]) + mod_ref[3:4, :]
    lg_ref[...] = _nt(wr_ref[...], h2, precision=HI)
    hb = h2.astype(BF16)
    packed = _pack_halves(hb.astype(F32))
    half = packed.shape[1] // 2
    h2p_ref[0] = packed[:, :half]
    h2p_ref[1] = packed[:, half:]
    F = sd_ref.shape[0]
    gu = jnp.dot(hb, sug_ref[...], preferred_element_type=F32)
    shared = jnp.dot((_silu(gu[:, :F]) * gu[:, F:]).astype(BF16), sd_ref[...], preferred_element_type=F32)
    base_ref[...] = x1 + mod_ref[5:6, :] * shared


def _post(attn, rw, gate, x2, mod, wa, wb, wo, norm2_g, w_router_t, sug, sd, S, tm=256):
    N, D = x2.shape
    E = w_router_t.shape[0]
    tpb = S // tm
    row = lambda w: pl.BlockSpec((tm, w), lambda i: (i, 0))
    full = lambda a: pl.BlockSpec(a.shape, lambda i: (0, 0))
    return pl.pallas_call(
        _post_kernel,
        grid=(N // tm,),
        in_specs=[row(attn.shape[1]), row(rw.shape[1]), row(gate.shape[1]), row(D),
                  pl.BlockSpec((None, 6, D), lambda i: (i // tpb, 0, 0)),
                  full(wa), full(wb), full(wo), pl.BlockSpec((1, D), lambda i: (0, 0)), full(w_router_t),
                  full(sug), full(sd)],
        out_specs=[row(D), pl.BlockSpec((2, tm, D // 4), lambda i: (0, i, 0)), pl.BlockSpec((E, tm), lambda i: (0, i))],
        out_shape=[jax.ShapeDtypeStruct((N, D), F32), jax.ShapeDtypeStruct((2, N, D // 4), U32),
                   jax.ShapeDtypeStruct((E, N), F32)],
        compiler_params=_cp("parallel"),
        name="post_mixer",
    )(attn, rw, gate, x2, mod, wa, wb, wo, norm2_g.reshape(1, D), w_router_t, sug, sd)


def _first_argmax(x, idx, sentinel):
    m = jnp.max(x, axis=0, keepdims=True)
    return m, jnp.min(jnp.where(x == m, idx, sentinel), axis=0, keepdims=True)


def _route_kernel(lg_ref, bias_ref, e_ref, w_ref):
    E, T = lg_ref.shape
    gsz = E // N_GROUPS
    scores = _sigmoid(lg_ref[...])
    biased = scores + bias_ref[...]
    ig = lax.broadcasted_iota(I32, (gsz, T), 0)
    gs = []
    for g in range(N_GROUPS):
        blk = biased[g * gsz:(g + 1) * gsz, :]
        m1, i1 = _first_argmax(blk, ig, gsz)
        m2 = jnp.max(jnp.where(ig == i1, -jnp.inf, blk), axis=0, keepdims=True)
        gs.append(m1 + m2)
    gsc = jnp.concatenate(gs, axis=0)
    i8 = lax.broadcasted_iota(I32, (N_GROUPS, T), 0)
    chosen = jnp.zeros((N_GROUPS, T), F32)
    for _ in range(TOPK_GROUPS):
        _, gi = _first_argmax(gsc, i8, N_GROUPS)
        hit = i8 == gi
        chosen = jnp.where(hit, 1.0, chosen)
        gsc = jnp.where(hit, -jnp.inf, gsc)
    masked = jnp.concatenate(
        [jnp.where(chosen[g:g + 1, :] > 0.0, biased[g * gsz:(g + 1) * gsz, :], -jnp.inf)
         for g in range(N_GROUPS)], axis=0)
    ie = lax.broadcasted_iota(I32, (E, T), 0)
    idxs, wts = [], []
    for _ in range(TOP_K):
        _, ei = _first_argmax(masked, ie, E)
        hit = ie == ei
        idxs.append(ei)
        wts.append(jnp.sum(jnp.where(hit, scores, 0.0), axis=0, keepdims=True))
        masked = jnp.where(hit, -jnp.inf, masked)
    wt = jnp.concatenate(wts, axis=0)
    e_ref[...] = jnp.concatenate(idxs, axis=0)
    w_ref[...] = wt / jnp.sum(wt, axis=0, keepdims=True) * ROUTED_SCALE


def _route(logits_t, router_bias, T=512):
    E, N = logits_t.shape
    blk = pl.BlockSpec((TOP_K, T), lambda i: (0, i))
    return pl.pallas_call(
        _route_kernel,
        grid=(N // T,),
        in_specs=[pl.BlockSpec((E, T), lambda i: (0, i)), pl.BlockSpec((E, 1), lambda i: (0, 0))],
        out_specs=[blk, blk],
        out_shape=[jax.ShapeDtypeStruct((TOP_K, N), I32), jax.ShapeDtypeStruct((TOP_K, N), F32)],
        compiler_params=_cp("parallel"),
        name="route",
    )(logits_t, router_bias.reshape(E, 1))


def _rank_kernel(e_ref, rank_ref, cnt_ref, carry_ref, *, n_experts):
    T = e_ref.shape[1]

    @pl.when(pl.program_id(0) == 0)
    def _():
        carry_ref[...] = jnp.zeros_like(carry_ref)

    ie = lax.broadcasted_iota(I32, (n_experts, T), 0)
    e = e_ref[...]
    hits = [ie == e[kk:kk + 1, :] for kk in range(TOP_K)]
    onehot = jnp.zeros((n_experts, T), F32)
    for hsel in hits:
        onehot = onehot + hsel.astype(F32)
    tr = lax.broadcasted_iota(I32, (T, T), 0)
    tc = lax.broadcasted_iota(I32, (T, T), 1)
    before = (tr < tc).astype(BF16)
    base = _bdot(onehot, before) + carry_ref[:, 0:1]
    rank_ref[...] = jnp.concatenate(
        [jnp.sum(jnp.where(hsel, base, 0.0), axis=0, keepdims=True) for hsel in hits], axis=0).astype(I32)
    carry_ref[...] = carry_ref[...] + jnp.sum(onehot, axis=1, keepdims=True)
    cnt_ref[...] = carry_ref[...]


def _ranks(eidx_t, n_experts, T=512):
    N = eidx_t.shape[1]
    blk = pl.BlockSpec((TOP_K, T), lambda i: (0, i))
    cnt = pl.BlockSpec((n_experts, LANES), lambda i: (0, 0))
    return pl.pallas_call(
        functools.partial(_rank_kernel, n_experts=n_experts),
        grid=(N // T,),
        in_specs=[blk],
        out_specs=[blk, cnt],
        out_shape=[jax.ShapeDtypeStruct((TOP_K, N), I32), jax.ShapeDtypeStruct((n_experts, LANES), F32)],
        scratch_shapes=[pltpu.VMEM((n_experts, LANES), F32)],
        compiler_params=_cp("arbitrary"),
        name="ranks",
    )(eidx_t)


def _dest_kernel(e_ref, rank_ref, start_ref, d_ref):
    E = start_ref.shape[0]
    T = e_ref.shape[1]
    ie = lax.broadcasted_iota(I32, (E, T), 0)
    e = e_ref[...]
    start = start_ref[:, 0:1]
    rows = [jnp.sum(jnp.where(ie == e[kk:kk + 1, :], start, 0.0), axis=0, keepdims=True) for kk in range(TOP_K)]
    d_ref[...] = jnp.concatenate(rows, axis=0).astype(I32) + rank_ref[...]


def _dests(eidx_t, rank_t, pstart, T=512):
    N = eidx_t.shape[1]
    E = pstart.shape[0]
    blk = pl.BlockSpec((TOP_K, T), lambda i: (0, i))
    return pl.pallas_call(
        _dest_kernel,
        grid=(N // T,),
        in_specs=[blk, blk, pl.BlockSpec((E, LANES), lambda i: (0, 0))],
        out_specs=blk,
        out_shape=jax.ShapeDtypeStruct((TOP_K, N), I32),
        compiler_params=_cp("parallel"),
        name="dests",
    )(eidx_t, rank_t, jnp.broadcast_to(pstart.astype(F32)[:, None], (E, LANES)))


def _expert_kernel(us_ref, ps_ref, x_hbm, wug_ref, wd_ref, y_hbm,
                   wug_bf, wd_bf, xbuf, ybuf, cnt_ref, xsem, ysem, *, n_rows):
    e = pl.program_id(0)
    _, _, R, Ch = xbuf.shape
    C = 2 * Ch
    F = wd_ref.shape[0]

    def x_copy(slot, half, row):
        return pltpu.make_async_copy(x_hbm.at[half, pl.ds(row, R)], xbuf.at[slot, half], xsem.at[slot])

    def y_copy(slot, half, row):
        return pltpu.make_async_copy(ybuf.at[slot, half], y_hbm.at[half, pl.ds(row, R)], ysem.at[slot])

    def start(copy, slot, row):
        for half in range(2):
            copy(slot, half, row).start(priority=1)

    def wait(copy, slot):
        for half in range(2):
            copy(slot, half, 0).wait()

    @pl.when(e == 0)
    def _():
        cnt_ref[0] = 0
        start(x_copy, 0, 0)
        ybuf[...] = jnp.zeros_like(ybuf)
        for slot in range(2):
            start(y_copy, slot, n_rows + slot * R)

    wug_bf[...] = wug_ref[...].astype(BF16)
    wd_bf[...] = wd_ref[...].astype(BF16)
    n_valid = us_ref[e + 1] - us_ref[e]
    pbase = ps_ref[e]
    n_tiles = lax.shift_right_logical(n_valid + (R - 1), int(math.log2(R)))
    rowid = lax.broadcasted_iota(I32, (R, C), 0)

    def tile(i, carry):
        n_done = cnt_ref[0]
        slot = n_done & 1
        start(x_copy, 1 - slot, pl.multiple_of(pbase + (i + 1) * R, R))
        wait(x_copy, slot)
        x = jnp.concatenate([xbuf[slot, 0], xbuf[slot, 1]], axis=1)
        x = jnp.where(rowid < n_valid - i * R, x, jnp.uint32(0))
        lo, hi = _unpack_halves(x)
        gu = (jnp.dot(lo.astype(BF16), wug_bf[0:C, :], preferred_element_type=F32)
              + jnp.dot(hi.astype(BF16), wug_bf[C:, :], preferred_element_type=F32))
        hid = (_silu(gu[:, :F]) * gu[:, F:]).astype(BF16)
        y = jnp.dot(hid, wd_bf[...], preferred_element_type=F32)
        wait(y_copy, slot)
        packed = _pack_halves(y.astype(BF16).astype(F32))
        ybuf[slot, 0] = packed[:, :Ch]
        ybuf[slot, 1] = packed[:, Ch:]
        start(y_copy, slot, pl.multiple_of(pbase + i * R, R))
        cnt_ref[0] = n_done + 1
        return carry

    lax.fori_loop(0, n_tiles, tile, 0)

    @pl.when(e == pl.num_programs(0) - 1)
    def _():
        wait(x_copy, cnt_ref[0] & 1)
        for slot in range(2):
            wait(y_copy, slot)
        ybuf[0] = jnp.zeros((2, R, Ch), U32)
        first = lax.shift_right_logical(pbase + n_tiles * R, int(math.log2(R)))
        n_left = n_rows // R - first

        def fill(t, carry):
            start(y_copy, 0, pl.multiple_of((first + t) * R, R))
            return carry

        def drain(t, carry):
            wait(y_copy, 0)
            return carry

        lax.fori_loop(0, n_left, fill, 0)
        lax.fori_loop(0, n_left, drain, 0)


def _experts(ustart, pstart, xg, w_ug, w_d, n_rows, R):
    _, _, Ch = xg.shape
    E, D, F2 = w_ug.shape
    F = w_d.shape[1]
    grid_spec = pltpu.PrefetchScalarGridSpec(
        num_scalar_prefetch=2,
        grid=(E,),
        in_specs=[pl.BlockSpec(memory_space=pl.ANY),
                  pl.BlockSpec((None, D, F2), lambda e, us, ps: (e, 0, 0)),
                  pl.BlockSpec((None, F, D), lambda e, us, ps: (e, 0, 0))],
        out_specs=pl.BlockSpec(memory_space=pl.ANY),
        scratch_shapes=[pltpu.VMEM((D, F2), BF16), pltpu.VMEM((F, D), BF16),
                        pltpu.VMEM((2, 2, R, Ch), U32), pltpu.VMEM((2, 2, R, Ch), U32), pltpu.SMEM((1,), I32),
                        pltpu.SemaphoreType.DMA((2,)), pltpu.SemaphoreType.DMA((2,))],
    )
    return pl.pallas_call(
        functools.partial(_expert_kernel, n_rows=n_rows),
        grid_spec=grid_spec,
        out_shape=jax.ShapeDtypeStruct((2, n_rows + 2 * R, Ch), U32),
        compiler_params=_cp("arbitrary"),
        name="experts",
    )(ustart, pstart, xg, w_ug, w_d)


def _sc_scatter_rows(src, idx, n_rows):
    H, N, C = src.shape
    K = idx.shape[0]
    per_row = N // SC_WINDOW
    mesh = plsc.VectorSubcoreMesh(core_axis_name="c", subcore_axis_name="s")

    @functools.partial(pl.kernel, out_type=jax.ShapeDtypeStruct((H, n_rows, C), src.dtype), mesh=mesh,
                       scratch_types=[])
    def scatter_kernel(x_hbm, i_hbm, o_hbm):
        for h in range(H):
            def body(x_vmem, i_vmem):
                pltpu.sync_copy(x_vmem, o_hbm.at[h].at[i_vmem.at[0]])

            pltpu.emit_pipeline(
                body,
                grid=(K * per_row,),
                in_specs=[pl.BlockSpec((SC_WINDOW, C), lambda i: (i % per_row, 0)),
                          pl.BlockSpec((1, SC_WINDOW), lambda i: (i // per_row, i % per_row))],
                out_specs=[],
                core_axis_name=("c", "s"),
                dimension_semantics=(pltpu.PARALLEL,),
            )(x_hbm.at[h], i_hbm)

    return scatter_kernel(src, idx)


def _sc_gather_rows(src, idx):
    H, _, C = src.shape
    K, N = idx.shape
    per_row = N // SC_WINDOW
    mesh = plsc.VectorSubcoreMesh(core_axis_name="c", subcore_axis_name="s")

    @functools.partial(pl.kernel, out_type=jax.ShapeDtypeStruct((H, K * N, C), src.dtype), mesh=mesh,
                       scratch_types=[])
    def gather_kernel(x_hbm, i_hbm, o_hbm):
        for h in range(H):
            def body(i_vmem, o_vmem):
                pltpu.sync_copy(x_hbm.at[h].at[i_vmem.at[0]], o_vmem)

            pltpu.emit_pipeline(
                body,
                grid=(K * per_row,),
                in_specs=[pl.BlockSpec((1, SC_WINDOW), lambda i: (i // per_row, i % per_row))],
                out_specs=[pl.BlockSpec((SC_WINDOW, C), lambda i: (i, 0))],
                core_axis_name=("c", "s"),
                dimension_semantics=(pltpu.PARALLEL,),
            )(i_hbm, o_hbm.at[h])

    return gather_kernel(src, idx)


def _combine_kernel(*refs):
    y_refs = refs[:2 * TOP_K]
    w_ref, base_ref, mod_ref, o_ref = refs[2 * TOP_K:]
    T = base_ref.shape[0]
    tr = lax.broadcasted_iota(I32, (T, T), 0)
    tc = lax.broadcasted_iota(I32, (T, T), 1)
    wcol = _nt((tr == tc).astype(F32), w_ref[...], precision=HI)
    acc = [None] * 4
    for kk in range(TOP_K):
        wk = wcol[:, kk:kk + 1]
        for half in range(2):
            lo, hi = _unpack_halves(y_refs[2 * kk + half][...])
            for q, val in ((half, lo), (2 + half, hi)):
                acc[q] = val * wk if acc[q] is None else acc[q] + val * wk
    o_ref[...] = base_ref[...] + mod_ref[5:6, :] * jnp.concatenate(acc, axis=1)


def _combine(yg, w_t, base, mod, S, T=256):
    N, D = base.shape
    C = yg.shape[2]
    tpb = S // T
    n_tiles = N // T
    row = pl.BlockSpec((T, D), lambda i: (i, 0))
    piece = lambda kk, half: pl.BlockSpec((None, T, C), lambda i: (half, kk * n_tiles + i, 0))
    return pl.pallas_call(
        _combine_kernel,
        grid=(n_tiles,),
        in_specs=[piece(kk, half) for kk in range(TOP_K) for half in range(2)] + [
            pl.BlockSpec((TOP_K, T), lambda i: (0, i)),
            row,
            pl.BlockSpec((None, 6, D), lambda i: (i // tpb, 0, 0))],
        out_specs=row,
        out_shape=jax.ShapeDtypeStruct((N, D), F32),
        compiler_params=_cp("parallel"),
        name="combine",
    )(*([yg] * (2 * TOP_K)), w_t, base, mod)


def _layer(x, c, positions, layer_idx, w_ada, b_ada, norm1_g, w_in, w_gate, b_gate,
           q_norm_g, k_norm_g, lambda_q1, lambda_k1, lambda_q2, lambda_k2, subln_g,
           rwkv_mu, w_decay0, w_decay2, a0, a2, g2, k_k, k_a, r_k, ln_x_w, ln_x_b,
           w_branch_a, w_branch_b, w_out, norm2_g, w_router, router_bias,
           w_expert_up_gate, w_expert_down, w_shared_up_gate, w_shared_down):
    B, S, D = x.shape
    N = B * S
    E = w_router.shape[1]
    da_width = w_branch_a.shape[0]
    rw_width = w_branch_b.shape[0]
    rw_cols = rwkv_mu.shape[0]
    lambda_init = 0.8 - 0.6 * math.exp(-0.3 * layer_idx)

    mod = _adaln(c, w_ada, b_ada)
    x2 = x.reshape(N, D)
    w_cat = jnp.concatenate([w_in, w_gate], axis=1).astype(BF16)
    q, k, v, prw, gate = _inproj(x2, mod, norm1_g, w_cat, b_gate, S, da_width, rw_cols)

    qn, kn = _qkprep(q, k, positions.reshape(N, 1), q_norm_g, k_norm_g)
    lam_vecs = jnp.stack([lambda_q1, lambda_k1, lambda_q2, lambda_k2])
    score_bound = 1.01 * DA_HEAD_DIM ** 0.5 * jnp.max(jnp.abs(q_norm_g)) * jnp.max(jnp.abs(k_norm_g))
    attn = _diff_attention(qn, kn, v, score_bound, lam_vecs, subln_g, B, S, lambda_init)

    r_, k_, v_, a_, ld_, g_ = _rwkv_prep(prw.reshape(B, S, rw_cols), rwkv_mu, w_decay0, w_decay2,
                                         a0, a2, g2, rw_width)
    rw = _rwkv_scan(r_, k_, v_, a_, ld_, g_, k_k, k_a, r_k.reshape(-1), ln_x_w, ln_x_b).reshape(N, rw_width)

    base, h2p, logits_t = _post(attn, rw, gate, x2, mod, w_branch_a.astype(BF16), w_branch_b.astype(BF16),
                                w_out.astype(BF16), norm2_g, w_router.T,
                                w_shared_up_gate.astype(BF16), w_shared_down.astype(BF16), S)

    eidx_t, w_t = _route(logits_t, router_bias)
    rank_t, counts = _ranks(eidx_t, E)
    R = EXPERT_TILE
    cnt = counts[:, 0].astype(I32)
    ustart = jnp.concatenate([jnp.zeros((1,), I32), jnp.cumsum(cnt)])
    pcnt = (cnt + R - 1) // R * R
    pstart = jnp.cumsum(pcnt) - pcnt
    dest_p = _dests(eidx_t, rank_t, pstart)
    n_rows = (N * TOP_K + E * (R - 1) + R - 1) // R * R
    xg = _sc_scatter_rows(h2p, dest_p, n_rows + R)
    y = _experts(ustart, pstart, xg, w_expert_up_gate, w_expert_down, n_rows, R)
    yg = _sc_gather_rows(y, dest_p)
    out = _combine(yg, w_t, base, mod, S)
    return out.reshape(B, S, D)


def kernel(x, c, positions, w_ada, b_ada, norm1_g, w_in, w_gate, b_gate, q_norm_g, k_norm_g, lambda_q1, lambda_k1, lambda_q2, lambda_k2, subln_g, rwkv_mu, w_decay0, w_decay2, a0, a2, g2, k_k, k_a, r_k, ln_x_w, ln_x_b, w_branch_a, w_branch_b, w_out, norm2_g, w_router, router_bias, w_expert_up_gate, w_expert_down, w_shared_up_gate, w_shared_down):
    for l in range(w_ada.shape[0]):
        x = _layer(x, c, positions, l, w_ada[l], b_ada[l], norm1_g[l], w_in[l], w_gate[l], b_gate[l],
                   q_norm_g[l], k_norm_g[l], lambda_q1[l], lambda_k1[l], lambda_q2[l], lambda_k2[l],
                   subln_g[l], rwkv_mu[l], w_decay0[l], w_decay2[l], a0[l], a2[l], g2[l], k_k[l],
                   k_a[l], r_k[l], ln_x_w[l], ln_x_b[l], w_branch_a[l], w_branch_b[l], w_out[l],
                   norm2_g[l], w_router[l], router_bias[l], w_expert_up_gate[l], w_expert_down[l],
                   w_shared_up_gate[l], w_shared_down[l])
    return x
```

```python
import functools
import math

import jax
import jax.numpy as jnp
from jax import lax
from jax.experimental import pallas as pl
from jax.experimental.pallas import tpu as pltpu
from jax.experimental.pallas import tpu_sc as plsc

F32 = jnp.float32
BF16 = jnp.bfloat16
I32 = jnp.int32
U32 = jnp.uint32
HI = lax.Precision.HIGHEST

CHUNK = 64
ROPE_THETA = 10000.0
NORM_EPS = 1e-6
SUBLN_EPS = 1e-5
DA_HEAD_DIM = 64
RWKV_HEAD = 64
GN_EPS = 64e-5
TOP_K = 8
N_GROUPS = 8
TOPK_GROUPS = 4
ROUTED_SCALE = 2.5
EXPERT_TILE = 256
X_SLOTS = 3
RWKV_CHUNK = 64
LANES = 128
SC_WINDOW = 128
NEG = -1e30
MAX_PLAIN_SCORE = 40.0
VMEM_LIMIT = 56 * 1024 * 1024


def _cp(*sem):
    return pltpu.CompilerParams(dimension_semantics=sem, vmem_limit_bytes=VMEM_LIMIT)


def _bdot(a, b):
    return jnp.dot(a.astype(BF16), b.astype(BF16), preferred_element_type=F32)


def _fdot(a, b):
    return jnp.dot(a, b, precision=HI, preferred_element_type=F32)


def _nt(a, b, precision=None):
    return lax.dot_general(a, b, (((1,), (1,)), ((), ())), precision=precision,
                           preferred_element_type=F32)


def _tn(a, b, precision=None):
    return lax.dot_general(a, b, (((0,), (0,)), ((), ())), precision=precision,
                           preferred_element_type=F32)


def _pack_halves(x):
    c = x.shape[1] // 2
    lo = lax.bitcast_convert_type(x[:, :c], U32)
    hi = lax.bitcast_convert_type(x[:, c:], U32)
    return (hi & jnp.uint32(0xFFFF0000)) | (lo >> 16)


def _unpack_halves(w):
    lo = lax.bitcast_convert_type(w << 16, F32)
    hi = lax.bitcast_convert_type(w & jnp.uint32(0xFFFF0000), F32)
    return lo, hi


def _sigmoid(x):
    return 1.0 / (1.0 + jnp.exp(-x))


def _silu(x):
    return x * _sigmoid(x)


def _ada_kernel(c_ref, w_ref, b_ref, o_ref):
    o_ref[...] = _fdot(_silu(c_ref[...]), w_ref[...]) + b_ref[...]


def _adaln(c, w_ada, b_ada):
    B, D = c.shape
    rows = -(-B // 8) * 8
    cpad = jnp.zeros((rows, D), F32).at[:B].set(c)
    n_out = w_ada.shape[1]
    out = pl.pallas_call(
        _ada_kernel,
        grid=(n_out // D,),
        in_specs=[pl.BlockSpec((rows, D), lambda j: (0, 0)),
                  pl.BlockSpec((D, D), lambda j: (0, j)),
                  pl.BlockSpec((1, D), lambda j: (0, j))],
        out_specs=pl.BlockSpec((rows, D), lambda j: (0, j)),
        out_shape=jax.ShapeDtypeStruct((rows, n_out), F32),
        compiler_params=_cp("arbitrary"),
        name="adaln",
    )(cpad, w_ada, b_ada.reshape(1, n_out))
    return out[:B].reshape(B, n_out // D, D)


def _inproj_kernel(x_ref, mod_ref, g_ref, w_ref, bg_ref, q_ref, k_ref, v_ref, rw_ref, gate_ref,
                   *, da_width, rw_cols):
    x = x_ref[...]
    y = x * lax.rsqrt(jnp.mean(x * x, axis=-1, keepdims=True) + NORM_EPS) * g_ref[...]
    h = (y * (1.0 + mod_ref[1:2, :]) + mod_ref[0:1, :]).astype(BF16)
    c = 0
    for ref in (q_ref, k_ref, v_ref):
        ref[...] = jnp.dot(h, w_ref[:, c:c + da_width], preferred_element_type=F32).astype(ref.dtype)
        c += da_width
    step = 512
    for o in range(0, rw_cols, step):
        wd = min(step, rw_cols - o)
        rw_ref[:, o:o + wd] = jnp.dot(h, w_ref[:, c + o:c + o + wd], preferred_element_type=F32)
    c += rw_cols
    n_gate = gate_ref.shape[1]
    for o in range(0, n_gate, step):
        z = jnp.dot(h, w_ref[:, c + o:c + o + step], preferred_element_type=F32) + bg_ref[:, o:o + step]
        gate_ref[:, o:o + step] = _sigmoid(z)


def _inproj(x2, mod, norm1_g, w_cat, b_gate, S, da_width, rw_cols, tm=256):
    N, D = x2.shape
    n_gate = b_gate.shape[0]
    tpb = S // tm
    kern = functools.partial(_inproj_kernel, da_width=da_width, rw_cols=rw_cols)
    row = lambda w: pl.BlockSpec((tm, w), lambda i: (i, 0))
    return pl.pallas_call(
        kern,
        grid=(N // tm,),
        in_specs=[row(D),
                  pl.BlockSpec((None, 6, D), lambda i: (i // tpb, 0, 0)),
                  pl.BlockSpec((1, D), lambda i: (0, 0)),
                  pl.BlockSpec(w_cat.shape, lambda i: (0, 0)),
                  pl.BlockSpec((1, n_gate), lambda i: (0, 0))],
        out_specs=[row(da_width), row(da_width), row(da_width), row(rw_cols), row(n_gate)],
        out_shape=[jax.ShapeDtypeStruct((N, da_width), F32),
                   jax.ShapeDtypeStruct((N, da_width), F32),
                   jax.ShapeDtypeStruct((N, da_width), BF16),
                   jax.ShapeDtypeStruct((N, rw_cols), F32),
                   jax.ShapeDtypeStruct((N, n_gate), F32)],
        compiler_params=_cp("parallel"),
        name="inproj",
    )(x2, mod, norm1_g.reshape(1, D), w_cat, b_gate.reshape(1, n_gate))


def _qkprep_kernel(q_ref, k_ref, pos_ref, invf_ref, qg_ref, kg_ref, qo_ref, ko_ref, *, scale):
    tm = q_ref.shape[0]
    lane = lax.broadcasted_iota(I32, (tm, LANES), 1)
    first = lane < DA_HEAD_DIM
    lo_half = (lane & (DA_HEAD_DIM - 1)) < DA_HEAD_DIM // 2
    ang = pos_ref[...].astype(F32) * invf_ref[...]
    cos = jnp.cos(ang)
    sin = jnp.sin(ang)
    sin = jnp.where(lo_half, -sin, sin)

    def one(src, dst, g_ref, mult):
        for blk in range(src.shape[1] // LANES):
            x = src[:, blk * LANES:(blk + 1) * LANES]
            xx = x * x
            s_all = jnp.sum(xx, axis=-1, keepdims=True)
            s_first = jnp.sum(jnp.where(first, xx, 0.0), axis=-1, keepdims=True)
            ms = jnp.where(first, s_first, s_all - s_first) * (1.0 / DA_HEAD_DIM)
            xn = x * lax.rsqrt(ms + NORM_EPS) * g_ref[...]
            rot = jnp.where(lo_half, pltpu.roll(xn, LANES - DA_HEAD_DIM // 2, axis=1),
                            pltpu.roll(xn, DA_HEAD_DIM // 2, axis=1))
            dst[:, blk * LANES:(blk + 1) * LANES] = ((xn * cos + rot * sin) * mult).astype(dst.dtype)

    one(q_ref, qo_ref, qg_ref, scale)
    one(k_ref, ko_ref, kg_ref, 1.0)


def _qkprep(q, k, pos2, q_norm_g, k_norm_g, tm=512):
    N, W = q.shape
    d = DA_HEAD_DIM
    inv_freq = 1.0 / (ROPE_THETA ** (jnp.arange(0, d, 2, dtype=F32) / d))
    invf = jnp.tile(inv_freq, LANES // (d // 2)).reshape(1, LANES)
    row = pl.BlockSpec((tm, W), lambda i: (i, 0))
    vec = pl.BlockSpec((1, LANES), lambda i: (0, 0))
    return pl.pallas_call(
        functools.partial(_qkprep_kernel, scale=d ** -0.5 * math.log2(math.e)),
        grid=(N // tm,),
        in_specs=[row, row, pl.BlockSpec((tm, 1), lambda i: (i, 0)), vec, vec, vec],
        out_specs=[row, row],
        out_shape=[jax.ShapeDtypeStruct((N, W), BF16)] * 2,
        compiler_params=_cp("parallel"),
        name="qkprep",
    )(q, k, pos2, invf, jnp.tile(q_norm_g, 2).reshape(1, LANES), jnp.tile(k_norm_g, 2).reshape(1, LANES))


def _attn_kernel(flag_ref, q_ref, k_ref, v_ref, lam_ref, sg_ref, o_ref, qz_ref, m_ref, l_ref, lp_ref, acc_ref,
                 *, tq, lambda_init):
    i = pl.program_id(2)
    lane = lax.broadcasted_iota(I32, (tq, LANES), 1)
    q = q_ref[...]
    zero = jnp.zeros_like(q)
    qz_ref[0:tq, :] = jnp.where(lane < DA_HEAD_DIM, q, zero)
    qz_ref[tq:, :] = jnp.where(lane >= DA_HEAD_DIM, q, zero)
    acc_ref[...] = jnp.zeros_like(acc_ref)
    bounded = flag_ref[0] == 1

    def scores(j, masked):
        off = pl.multiple_of(j * tq, tq)
        s = _nt(qz_ref[...], k_ref[pl.ds(off, tq), :])
        if masked:
            row = lax.broadcasted_iota(I32, s.shape, 0)
            col = lax.broadcasted_iota(I32, s.shape, 1)
            s = jnp.where((col // CHUNK) <= ((row & (tq - 1)) // CHUNK), s, NEG)
        return s, off

    def plain_step(j, masked):
        s, off = scores(j, masked)
        pr = jnp.exp2(s)
        part = pr[:, 0:LANES]
        for cblk in range(1, tq // LANES):
            part = part + pr[:, cblk * LANES:(cblk + 1) * LANES]
        lp_ref[...] += part
        acc_ref[...] += jnp.dot(pr.astype(BF16), v_ref[pl.ds(off, tq), :], preferred_element_type=F32)

    def online_step(j, masked):
        s, off = scores(j, masked)
        m_old = m_ref[...]
        m_new = jnp.maximum(m_old, jnp.max(s, axis=-1, keepdims=True))
        alpha = jnp.exp2(m_old - m_new)
        pr = jnp.exp2(s - m_new)
        l_ref[...] = alpha * l_ref[...] + jnp.sum(pr, axis=-1, keepdims=True)
        acc_ref[...] = alpha * acc_ref[...] + jnp.dot(pr.astype(BF16), v_ref[pl.ds(off, tq), :],
                                                      preferred_element_type=F32)
        m_ref[...] = m_new

    def run(step):
        def body(j, carry):
            step(j, False)
            return carry
        lax.fori_loop(0, i, body, 0)
        step(i, True)

    @pl.when(bounded)
    def _():
        lp_ref[...] = jnp.zeros_like(lp_ref)
        run(plain_step)
        l_ref[...] = jnp.sum(lp_ref[...], axis=-1, keepdims=True)

    @pl.when(jnp.logical_not(bounded))
    def _():
        m_ref[...] = jnp.full_like(m_ref, NEG)
        l_ref[...] = jnp.zeros_like(l_ref)
        run(online_step)

    lv = lam_ref[...]
    lam = (jnp.exp(jnp.sum(lv[0:1] * lv[1:2], keepdims=True))
           - jnp.exp(jnp.sum(lv[2:3] * lv[3:4], keepdims=True)) + lambda_init)
    o1 = acc_ref[0:tq, :] / l_ref[0:tq, :]
    o2 = acc_ref[tq:, :] / l_ref[tq:, :]
    o = o1 - lam * o2
    o = o * lax.rsqrt(jnp.mean(o * o, axis=-1, keepdims=True) + SUBLN_EPS) * sg_ref[...]
    o_ref[...] = (o * (1.0 - lambda_init)).astype(o_ref.dtype)


def _diff_attention(qn, kn, v, score_bound, lam_vecs, subln_g, B, S, lambda_init, tq=512):
    W = qn.shape[1]
    H = W // LANES
    q3 = qn.reshape(B, S, W)
    k3 = kn.reshape(B, S, W)
    v3 = v.reshape(B, S, W)
    flag = (score_bound <= MAX_PLAIN_SCORE).astype(I32).reshape(1)
    qblk = pl.BlockSpec((None, tq, LANES), lambda b, h, i, f: (b, i, h))
    kvblk = pl.BlockSpec((None, S, LANES), lambda b, h, i, f: (b, 0, h))
    grid_spec = pltpu.PrefetchScalarGridSpec(
        num_scalar_prefetch=1,
        grid=(B, H, S // tq),
        in_specs=[qblk, kvblk, kvblk,
                  pl.BlockSpec((4, DA_HEAD_DIM), lambda b, h, i, f: (0, 0)),
                  pl.BlockSpec((1, LANES), lambda b, h, i, f: (0, 0))],
        out_specs=qblk,
        scratch_shapes=[pltpu.VMEM((2 * tq, LANES), BF16),
                        pltpu.VMEM((2 * tq, 1), F32),
                        pltpu.VMEM((2 * tq, 1), F32),
                        pltpu.VMEM((2 * tq, LANES), F32),
                        pltpu.VMEM((2 * tq, LANES), F32)],
    )
    out = pl.pallas_call(
        functools.partial(_attn_kernel, tq=tq, lambda_init=lambda_init),
        grid_spec=grid_spec,
        out_shape=jax.ShapeDtypeStruct((B, S, W), BF16),
        compiler_params=_cp("parallel", "parallel", "arbitrary"),
        name="diff_attn",
    )(flag, q3, k3, v3, lam_vecs, subln_g.reshape(1, LANES))
    return out.reshape(B * S, W)


def _rwkv_prep_kernel(p_ref, prev_ref, mu_ref, w0_ref, w2_ref, a0_ref, a2_ref, g2_ref,
                      r_ref, k_ref, v_ref, a_ref, ld_ref, g_ref, *, width):
    i = pl.program_id(1)
    p = p_ref[...]
    last_prev = jnp.where(i > 0, prev_ref[7:8, :], 0.0)
    rowi = lax.broadcasted_iota(I32, p.shape, 0)
    prev = jnp.where(rowi == 0, last_prev, pltpu.roll(p, 1, axis=0))
    xs = p + (prev - p) * mu_ref[...]
    r_ref[...] = xs[:, 0:width]
    k_ref[...] = xs[:, width:2 * width]
    v_ref[...] = xs[:, 2 * width:3 * width]
    xwa = xs[:, 3 * width:3 * width + LANES]
    xg = xs[:, 3 * width + LANES:]
    z = w0_ref[...] + _fdot(jnp.tanh(xwa), w2_ref[...])
    w = -(jnp.maximum(-z, 0.0) + jnp.log(1.0 + jnp.exp(-jnp.abs(z)))) - 0.5
    ld_ref[...] = -jnp.exp(w)
    a_ref[...] = _sigmoid(a0_ref[...] + _fdot(xwa, a2_ref[...]))
    g_ref[...] = _fdot(_sigmoid(xg), g2_ref[...])


def _rwkv_prep(prw3, mu, w_decay0, w_decay2, a0, a2, g2, width, tm=256):
    B, S, C = prw3.shape
    dl, al = w_decay2.shape[0], a2.shape[0]
    assert dl + al == LANES and g2.shape[0] == LANES
    w2p = jnp.zeros((LANES, width), F32).at[:dl].set(w_decay2)
    a2p = jnp.zeros((LANES, width), F32).at[dl:].set(a2)
    vec = lambda n: pl.BlockSpec((1, n), lambda b, i: (0, 0))
    mat = pl.BlockSpec((LANES, width), lambda b, i: (0, 0))
    out = pl.BlockSpec((None, tm, width), lambda b, i: (b, i, 0))
    return pl.pallas_call(
        functools.partial(_rwkv_prep_kernel, width=width),
        grid=(B, S // tm),
        in_specs=[pl.BlockSpec((None, tm, C), lambda b, i: (b, i, 0)),
                  pl.BlockSpec((None, 8, C), lambda b, i: (b, jnp.maximum(i * (tm // 8) - 1, 0), 0)),
                  vec(C), vec(width), mat, vec(width), mat, mat],
        out_specs=[out] * 6,
        out_shape=[jax.ShapeDtypeStruct((B, S, width), F32)] * 6,
        compiler_params=_cp("parallel", "parallel"),
        name="rwkv_prep",
    )(prw3, prw3, mu.reshape(1, C), w_decay0.reshape(1, width), w2p, a0.reshape(1, width), a2p, g2)


def _stackmask(m):
    lane = lax.broadcasted_iota(I32, m.shape, 1)
    z = jnp.zeros_like(m)
    return jnp.concatenate([jnp.where(lane < RWKV_HEAD, m, z), jnp.where(lane >= RWKV_HEAD, m, z)], axis=0)


def _pair_sum(x, first):
    s1 = jnp.sum(jnp.where(first, x, 0.0), axis=-1, keepdims=True)
    s2 = jnp.sum(jnp.where(first, 0.0, x), axis=-1, keepdims=True)
    return jnp.where(first, s1, s2)


def _rwkv_scan_kernel(r_ref, k_ref, v_ref, a_ref, ld_ref, g_ref, kk_ref, ka_ref, rk_ref, lnw_ref, lnb_ref,
                      o_ref, s_ref, *, L):
    tm, W = r_ref.shape
    n_chunks = tm // L
    n_pairs = W // LANES
    hd = RWKV_HEAD
    bf = lambda t: t.astype(BF16)

    @pl.when(pl.program_id(1) == 0)
    def _():
        s_ref[...] = jnp.zeros_like(s_ref)

    row = lax.broadcasted_iota(I32, (tm, tm), 0)
    col = lax.broadcasted_iota(I32, (tm, tm), 1)
    tri = jnp.where(jnp.logical_and(col <= row, (col // L) == (row // L)), 1.0, 0.0).astype(BF16)
    ld = ld_ref[...]
    ld_hi = bf(ld)
    rem = ld - ld_hi.astype(F32)
    ld_mid = bf(rem)
    ld_lo = bf(rem - ld_mid.astype(F32))
    c = (jnp.dot(tri, ld_hi, preferred_element_type=F32) + jnp.dot(tri, ld_mid, preferred_element_type=F32)
         + jnp.dot(tri, ld_lo, preferred_element_type=F32))
    ec = jnp.exp(c)
    eci = jnp.exp(-c)
    ecm = jnp.exp(c - ld)
    r = r_ref[...]
    k = k_ref[...]
    v = v_ref[...]
    a = a_ref[...]
    kkr = k * kk_ref[...]
    kmod = k * (1.0 + (a - 1.0) * ka_ref[...])
    brk = r * kmod * rk_ref[...]

    lane = lax.broadcasted_iota(I32, (L, LANES), 1)
    rowl = lax.broadcasted_iota(I32, (L, LANES), 0)
    first = lane < hd
    lane_h = lane & (hd - 1)
    strict = lane_h < rowl
    incl = lane_h <= rowl
    eye = jnp.where(lane_h == rowl, 1.0, 0.0)

    chains = [(ch, p) for ch in range(n_chunks) for p in range(n_pairs)]
    rsl = lambda ch: slice(ch * L, (ch + 1) * L)
    csl = lambda p: slice(p * LANES, (p + 1) * LANES)
    fdot = lambda x, y: jnp.dot(x, y, preferred_element_type=F32)
    at, bt, kt, rt, vh, g_l = {}, {}, {}, {}, {}, {}
    for c_ in chains:
        ch, p = c_
        rs, cs = rsl(ch), csl(p)
        kkh = kkr[rs, cs]
        kkh = kkh / jnp.maximum(jnp.sqrt(_pair_sum(kkh * kkh, first)), 1e-12)
        vh[c_] = v[rs, cs]
        g_l[c_] = ec[ch * L + L - 1:ch * L + L, cs]
        at[c_] = -kkh * ecm[rs, cs]
        bt[c_] = kkh * a[rs, cs] * eci[rs, cs]
        kt[c_] = kmod[rs, cs] * eci[rs, cs]
        rt[c_] = r[rs, cs] * ec[rs, cs]
    gm = {c_: _nt(bf(jnp.concatenate([at[c_], rt[c_]], axis=0)),
                  jnp.concatenate([_stackmask(bf(bt[c_])), _stackmask(bf(kt[c_]))], axis=0)) for c_ in chains}
    a_ab = {c_: jnp.where(strict, gm[c_][:L, :LANES], 0.0) for c_ in chains}
    vsm = {c_: _stackmask(bf(vh[c_])) for c_ in chains}
    cmat = {c_: fdot(bf(jnp.where(strict, gm[c_][:L, LANES:], 0.0)), vsm[c_]) for c_ in chains}
    t_inv = {c_: eye + a_ab[c_] for c_ in chains}
    pw = {c_: bf(a_ab[c_]) for c_ in chains}
    for _ in range(int(math.log2(L)) - 1):
        pw = {c_: bf(fdot(pw[c_], _stackmask(pw[c_]))) for c_ in chains}
        t_inv = {c_: t_inv[c_] + fdot(pw[c_], _stackmask(bf(t_inv[c_]))) for c_ in chains}
    zz = {c_: fdot(bf(t_inv[c_]), jnp.concatenate([_stackmask(bf(at[c_])), _stackmask(bf(cmat[c_]))], axis=1))
          for c_ in chains}
    qy = {c_: fdot(bf(jnp.where(incl, gm[c_][L:, :LANES], 0.0)),
                   jnp.concatenate([_stackmask(bf(zz[c_][:, :LANES])), _stackmask(bf(zz[c_][:, LANES:]))], axis=1))
          for c_ in chains}
    y0 = {c_: qy[c_][:, LANES:] + fdot(bf(jnp.where(incl, gm[c_][L:, LANES:], 0.0)), vsm[c_]) for c_ in chains}
    qa = {c_: bf(jnp.concatenate([rt[c_] + qy[c_][:, :LANES], zz[c_][:, :LANES]], axis=0)) for c_ in chains}
    bkg = {c_: bf(jnp.concatenate([bt[c_] * g_l[c_], kt[c_] * g_l[c_]], axis=0)) for c_ in chains}

    lane_s = lax.broadcasted_iota(I32, (hd, LANES), 1)
    sp = [s_ref[p] for p in range(n_pairs)]
    for ch in range(n_chunks):
        rs = rsl(ch)
        yw = [_nt(qa[ch, p], _stackmask(bf(sp[p]))) for p in range(n_pairs)]
        upd = [_tn(bf(jnp.concatenate([yw[p][L:] + zz[ch, p][:, LANES:], vh[ch, p]], axis=0)), bkg[ch, p])
               for p in range(n_pairs)]
        for p in range(n_pairs):
            cs = csl(p)
            sp[p] = sp[p] * g_l[ch, p] + jnp.where(lane_s < hd, upd[p][:hd], upd[p][hd:])
            y = yw[p][:L] + y0[ch, p]
            mean = _pair_sum(y, first) * (1.0 / hd)
            yc = y - mean
            var = _pair_sum(yc * yc, first) * (1.0 / hd)
            yn = yc * lax.rsqrt(var + GN_EPS) * lnw_ref[:, cs] + lnb_ref[:, cs]
            bonus = _pair_sum(brk[rs, cs], first) * vh[ch, p]
            o_ref[rs, cs] = ((yn + bonus) * g_ref[rs, cs]).astype(o_ref.dtype)
    for p in range(n_pairs):
        s_ref[p] = sp[p]


def _rwkv_scan(r, k, v, a, ld, g, k_k, k_a, r_k, ln_w, ln_b, L=RWKV_CHUNK, tm=256):
    B, S, W = r.shape
    seq = pl.BlockSpec((None, tm, W), lambda b, c: (b, c, 0))
    vec = pl.BlockSpec((1, W), lambda b, c: (0, 0))
    return pl.pallas_call(
        functools.partial(_rwkv_scan_kernel, L=L),
        grid=(B, S // tm),
        in_specs=[seq] * 6 + [vec] * 5,
        out_specs=seq,
        out_shape=jax.ShapeDtypeStruct((B, S, W), BF16),
        scratch_shapes=[pltpu.VMEM((W // LANES, RWKV_HEAD, LANES), F32)],
        compiler_params=_cp("parallel", "arbitrary"),
        name="rwkv_scan",
    )(r, k, v, a, ld, g, k_k.reshape(1, W), k_a.reshape(1, W), r_k.reshape(1, W),
      ln_w.reshape(1, W), ln_b.reshape(1, W))


def _post_kernel(attn_ref, rw_ref, gate_ref, x_ref, mod_ref, wa_ref, wb_ref, wo_ref, g2_ref, wr_ref,
                 sug_ref, sd_ref, base_ref, h2p_ref, lg_ref):
    D = x_ref.shape[1]
    ya = jnp.dot(attn_ref[...], wa_ref[...], preferred_element_type=F32)
    yb = jnp.dot(rw_ref[...], wb_ref[...], preferred_element_type=F32)
    m = gate_ref[:, 0:D] * ya + gate_ref[:, D:] * yb
    x1 = x_ref[...] + mod_ref[2:3, :] * jnp.dot(m.astype(BF16), wo_ref[...], preferred_element_type=F32)
    y = x1 * lax.rsqrt(jnp.mean(x1 * x1, axis=-1, keepdims=True) + NORM_EPS) * g2_ref[...]
    h2 = y * (1.0 + mod_ref[4:5, :]) + mod_ref[3:4, :]
    lg_ref[...] = _nt(wr_ref[...], h2, precision=HI)
    hb = h2.astype(BF16)
    packed = _pack_halves(hb.astype(F32))
    half = packed.shape[1] // 2
    h2p_ref[0] = packed[:, :half]
    h2p_ref[1] = packed[:, half:]
    F = sd_ref.shape[0]
    gu = jnp.dot(hb, sug_ref[...], preferred_element_type=F32)
    shared = jnp.dot((_silu(gu[:, :F]) * gu[:, F:]).astype(BF16), sd_ref[...], preferred_element_type=F32)
    base_ref[...] = x1 + mod_ref[5:6, :] * shared


def _post(attn, rw, gate, x2, mod, wa, wb, wo, norm2_g, w_router_t, sug, sd, S, tm=256):
    N, D = x2.shape
    E = w_router_t.shape[0]
    tpb = S // tm
    row = lambda w: pl.BlockSpec((tm, w), lambda i: (i, 0))
    full = lambda a: pl.BlockSpec(a.shape, lambda i: (0, 0))
    return pl.pallas_call(
        _post_kernel,
        grid=(N // tm,),
        in_specs=[row(attn.shape[1]), row(rw.shape[1]), row(gate.shape[1]), row(D),
                  pl.BlockSpec((None, 6, D), lambda i: (i // tpb, 0, 0)),
                  full(wa), full(wb), full(wo), pl.BlockSpec((1, D), lambda i: (0, 0)), full(w_router_t),
                  full(sug), full(sd)],
        out_specs=[row(D), pl.BlockSpec((2, tm, D // 4), lambda i: (0, i, 0)), pl.BlockSpec((E, tm), lambda i: (0, i))],
        out_shape=[jax.ShapeDtypeStruct((N, D), F32), jax.ShapeDtypeStruct((2, N, D // 4), U32),
                   jax.ShapeDtypeStruct((E, N), F32)],
        compiler_params=_cp("parallel"),
        name="post_mixer",
    )(attn, rw, gate, x2, mod, wa, wb, wo, norm2_g.reshape(1, D), w_router_t, sug, sd)


def _first_argmax(x, idx, sentinel):
    m = jnp.max(x, axis=0, keepdims=True)
    return m, jnp.min(jnp.where(x == m, idx, sentinel), axis=0, keepdims=True)


def _route_kernel(lg_ref, bias_ref, e_ref, w_ref):
    E, T = lg_ref.shape
    gsz = E // N_GROUPS
    scores = _sigmoid(lg_ref[...])
    biased = scores + bias_ref[...]
    ig = lax.broadcasted_iota(I32, (gsz, T), 0)
    gs = []
    for g in range(N_GROUPS):
        blk = biased[g * gsz:(g + 1) * gsz, :]
        m1, i1 = _first_argmax(blk, ig, gsz)
        m2 = jnp.max(jnp.where(ig == i1, -jnp.inf, blk), axis=0, keepdims=True)
        gs.append(m1 + m2)
    gsc = jnp.concatenate(gs, axis=0)
    i8 = lax.broadcasted_iota(I32, (N_GROUPS, T), 0)
    chosen = jnp.zeros((N_GROUPS, T), F32)
    for _ in range(TOPK_GROUPS):
        _, gi = _first_argmax(gsc, i8, N_GROUPS)
        hit = i8 == gi
        chosen = jnp.where(hit, 1.0, chosen)
        gsc = jnp.where(hit, -jnp.inf, gsc)
    masked = jnp.concatenate(
        [jnp.where(chosen[g:g + 1, :] > 0.0, biased[g * gsz:(g + 1) * gsz, :], -jnp.inf)
         for g in range(N_GROUPS)], axis=0)
    ie = lax.broadcasted_iota(I32, (E, T), 0)
    idxs, wts = [], []
    for _ in range(TOP_K):
        _, ei = _first_argmax(masked, ie, E)
        hit = ie == ei
        idxs.append(ei)
        wts.append(jnp.sum(jnp.where(hit, scores, 0.0), axis=0, keepdims=True))
        masked = jnp.where(hit, -jnp.inf, masked)
    wt = jnp.concatenate(wts, axis=0)
    e_ref[...] = jnp.concatenate(idxs, axis=0)
    w_ref[...] = wt / jnp.sum(wt, axis=0, keepdims=True) * ROUTED_SCALE


def _route(logits_t, router_bias, T=512):
    E, N = logits_t.shape
    blk = pl.BlockSpec((TOP_K, T), lambda i: (0, i))
    return pl.pallas_call(
        _route_kernel,
        grid=(N // T,),
        in_specs=[pl.BlockSpec((E, T), lambda i: (0, i)), pl.BlockSpec((E, 1), lambda i: (0, 0))],
        out_specs=[blk, blk],
        out_shape=[jax.ShapeDtypeStruct((TOP_K, N), I32), jax.ShapeDtypeStruct((TOP_K, N), F32)],
        compiler_params=_cp("parallel"),
        name="route",
    )(logits_t, router_bias.reshape(E, 1))


def _rank_kernel(e_ref, rank_ref, cnt_ref, carry_ref, *, n_experts):
    T = e_ref.shape[1]

    @pl.when(pl.program_id(0) == 0)
    def _():
        carry_ref[...] = jnp.zeros_like(carry_ref)

    ie = lax.broadcasted_iota(I32, (n_experts, T), 0)
    e = e_ref[...]
    hits = [ie == e[kk:kk + 1, :] for kk in range(TOP_K)]
    onehot = jnp.zeros((n_experts, T), F32)
    for hsel in hits:
        onehot = onehot + hsel.astype(F32)
    tr = lax.broadcasted_iota(I32, (T, T), 0)
    tc = lax.broadcasted_iota(I32, (T, T), 1)
    before = (tr < tc).astype(BF16)
    base = _bdot(onehot, before) + carry_ref[:, 0:1]
    rank_ref[...] = jnp.concatenate(
        [jnp.sum(jnp.where(hsel, base, 0.0), axis=0, keepdims=True) for hsel in hits], axis=0).astype(I32)
    carry_ref[...] = carry_ref[...] + jnp.sum(onehot, axis=1, keepdims=True)
    cnt_ref[...] = carry_ref[...]


def _ranks(eidx_t, n_experts, T=512):
    N = eidx_t.shape[1]
    blk = pl.BlockSpec((TOP_K, T), lambda i: (0, i))
    cnt = pl.BlockSpec((n_experts, LANES), lambda i: (0, 0))
    return pl.pallas_call(
        functools.partial(_rank_kernel, n_experts=n_experts),
        grid=(N // T,),
        in_specs=[blk],
        out_specs=[blk, cnt],
        out_shape=[jax.ShapeDtypeStruct((TOP_K, N), I32), jax.ShapeDtypeStruct((n_experts, LANES), F32)],
        scratch_shapes=[pltpu.VMEM((n_experts, LANES), F32)],
        compiler_params=_cp("arbitrary"),
        name="ranks",
    )(eidx_t)


def _dest_kernel(e_ref, rank_ref, start_ref, d_ref):
    E = start_ref.shape[0]
    T = e_ref.shape[1]
    ie = lax.broadcasted_iota(I32, (E, T), 0)
    e = e_ref[...]
    start = start_ref[:, 0:1]
    rows = [jnp.sum(jnp.where(ie == e[kk:kk + 1, :], start, 0.0), axis=0, keepdims=True) for kk in range(TOP_K)]
    d_ref[...] = jnp.concatenate(rows, axis=0).astype(I32) + rank_ref[...]


def _dests(eidx_t, rank_t, pstart, T=512):
    N = eidx_t.shape[1]
    E = pstart.shape[0]
    blk = pl.BlockSpec((TOP_K, T), lambda i: (0, i))
    return pl.pallas_call(
        _dest_kernel,
        grid=(N // T,),
        in_specs=[blk, blk, pl.BlockSpec((E, LANES), lambda i: (0, 0))],
        out_specs=blk,
        out_shape=jax.ShapeDtypeStruct((TOP_K, N), I32),
        compiler_params=_cp("parallel"),
        name="dests",
    )(eidx_t, rank_t, jnp.broadcast_to(pstart.astype(F32)[:, None], (E, LANES)))


def _expert_kernel(us_ref, ps_ref, x_hbm, wug_ref, wd_ref, y_hbm,
                   wug_bf, wd_bf, xbuf, ybuf, cnt_ref, xsem, ysem, *, n_rows):
    e = pl.program_id(0)
    _, _, R, Ch = xbuf.shape
    C = 2 * Ch
    F = wd_ref.shape[0]

    def x_copy(slot, half, row):
        return pltpu.make_async_copy(x_hbm.at[half, pl.ds(row, R)], xbuf.at[slot, half], xsem.at[slot])

    def y_copy(slot, half, row):
        return pltpu.make_async_copy(ybuf.at[slot, half], y_hbm.at[half, pl.ds(row, R)], ysem.at[slot])

    def start(copy, slot, row):
        for half in range(2):
            copy(slot, half, row).start(priority=1)

    def wait(copy, slot):
        for half in range(2):
            copy(slot, half, 0).wait()

    @pl.when(e == 0)
    def _():
        cnt_ref[0] = 0
        for t in range(X_SLOTS - 1):
            start(x_copy, t, t * R)
        ybuf[...] = jnp.zeros_like(ybuf)
        for slot in range(2):
            start(y_copy, slot, n_rows + slot * R)

    wug_bf[...] = wug_ref[...].astype(BF16)
    wd_bf[...] = wd_ref[...].astype(BF16)
    n_valid = us_ref[e + 1] - us_ref[e]
    pbase = ps_ref[e]
    n_tiles = lax.shift_right_logical(n_valid + (R - 1), int(math.log2(R)))
    rowid = lax.broadcasted_iota(I32, (R, C), 0)

    def tile(i, carry):
        n_done = cnt_ref[0]
        slot = n_done & 1
        xslot = lax.rem(n_done, X_SLOTS)
        ahead = n_done + (X_SLOTS - 1)
        start(x_copy, lax.rem(ahead, X_SLOTS), pl.multiple_of(ahead * R, R))
        wait(x_copy, xslot)
        x = jnp.concatenate([xbuf[xslot, 0], xbuf[xslot, 1]], axis=1)
        x = jnp.where(rowid < n_valid - i * R, x, jnp.uint32(0))
        lo, hi = _unpack_halves(x)
        gu = (jnp.dot(lo.astype(BF16), wug_bf[0:C, :], preferred_element_type=F32)
              + jnp.dot(hi.astype(BF16), wug_bf[C:, :], preferred_element_type=F32))
        hid = (_silu(gu[:, :F]) * gu[:, F:]).astype(BF16)
        y = jnp.dot(hid, wd_bf[...], preferred_element_type=F32)
        wait(y_copy, slot)
        packed = _pack_halves(y.astype(BF16).astype(F32))
        ybuf[slot, 0] = packed[:, :Ch]
        ybuf[slot, 1] = packed[:, Ch:]
        start(y_copy, slot, pl.multiple_of(pbase + i * R, R))
        cnt_ref[0] = n_done + 1
        return carry

    lax.fori_loop(0, n_tiles, tile, 0)

    @pl.when(e == pl.num_programs(0) - 1)
    def _():
        for t in range(X_SLOTS - 1):
            wait(x_copy, lax.rem(cnt_ref[0] + t, X_SLOTS))
        for slot in range(2):
            wait(y_copy, slot)
        ybuf[0] = jnp.zeros((2, R, Ch), U32)
        first = lax.shift_right_logical(pbase + n_tiles * R, int(math.log2(R)))
        n_left = n_rows // R - first

        def fill(t, carry):
            start(y_copy, 0, pl.multiple_of((first + t) * R, R))
            return carry

        def drain(t, carry):
            wait(y_copy, 0)
            return carry

        lax.fori_loop(0, n_left, fill, 0)
        lax.fori_loop(0, n_left, drain, 0)


def _experts(ustart, pstart, xg, w_ug, w_d, n_rows, R):
    _, _, Ch = xg.shape
    E, D, F2 = w_ug.shape
    F = w_d.shape[1]
    grid_spec = pltpu.PrefetchScalarGridSpec(
        num_scalar_prefetch=2,
        grid=(E,),
        in_specs=[pl.BlockSpec(memory_space=pl.ANY),
                  pl.BlockSpec((None, D, F2), lambda e, us, ps: (e, 0, 0)),
                  pl.BlockSpec((None, F, D), lambda e, us, ps: (e, 0, 0))],
        out_specs=pl.BlockSpec(memory_space=pl.ANY),
        scratch_shapes=[pltpu.VMEM((D, F2), BF16), pltpu.VMEM((F, D), BF16),
                        pltpu.VMEM((X_SLOTS, 2, R, Ch), U32), pltpu.VMEM((2, 2, R, Ch), U32), pltpu.SMEM((1,), I32),
                        pltpu.SemaphoreType.DMA((X_SLOTS,)), pltpu.SemaphoreType.DMA((2,))],
    )
    return pl.pallas_call(
        functools.partial(_expert_kernel, n_rows=n_rows),
        grid_spec=grid_spec,
        out_shape=jax.ShapeDtypeStruct((2, n_rows + 2 * R, Ch), U32),
        compiler_params=_cp("arbitrary"),
        name="experts",
    )(ustart, pstart, xg, w_ug, w_d)


def _sc_scatter_rows(src, idx, n_rows):
    H, N, C = src.shape
    K = idx.shape[0]
    per_row = N // SC_WINDOW
    mesh = plsc.VectorSubcoreMesh(core_axis_name="c", subcore_axis_name="s")

    @functools.partial(pl.kernel, out_type=jax.ShapeDtypeStruct((H, n_rows, C), src.dtype), mesh=mesh,
                       scratch_types=[])
    def scatter_kernel(x_hbm, i_hbm, o_hbm):
        for h in range(H):
            def body(x_vmem, i_vmem):
                pltpu.sync_copy(x_vmem, o_hbm.at[h].at[i_vmem.at[0]])

            pltpu.emit_pipeline(
                body,
                grid=(K * per_row,),
                in_specs=[pl.BlockSpec((SC_WINDOW, C), lambda i: (i % per_row, 0)),
                          pl.BlockSpec((1, SC_WINDOW), lambda i: (i // per_row, i % per_row))],
                out_specs=[],
                core_axis_name=("c", "s"),
                dimension_semantics=(pltpu.PARALLEL,),
            )(x_hbm.at[h], i_hbm)

    return scatter_kernel(src, idx)


def _sc_gather_rows(src, idx):
    H, _, C = src.shape
    K, N = idx.shape
    per_row = N // SC_WINDOW
    mesh = plsc.VectorSubcoreMesh(core_axis_name="c", subcore_axis_name="s")

    @functools.partial(pl.kernel, out_type=jax.ShapeDtypeStruct((H, K * N, C), src.dtype), mesh=mesh,
                       scratch_types=[])
    def gather_kernel(x_hbm, i_hbm, o_hbm):
        for h in range(H):
            def body(i_vmem, o_vmem):
                pltpu.sync_copy(x_hbm.at[h].at[i_vmem.at[0]], o_vmem)

            pltpu.emit_pipeline(
                body,
                grid=(K * per_row,),
                in_specs=[pl.BlockSpec((1, SC_WINDOW), lambda i: (i // per_row, i % per_row))],
                out_specs=[pl.BlockSpec((SC_WINDOW, C), lambda i: (i, 0))],
                core_axis_name=("c", "s"),
                dimension_semantics=(pltpu.PARALLEL,),
            )(i_hbm, o_hbm.at[h])

    return gather_kernel(src, idx)


def _combine_kernel(*refs):
    y_refs = refs[:2 * TOP_K]
    w_ref, base_ref, mod_ref, o_ref = refs[2 * TOP_K:]
    T = base_ref.shape[0]
    tr = lax.broadcasted_iota(I32, (T, T), 0)
    tc = lax.broadcasted_iota(I32, (T, T), 1)
    wcol = _nt((tr == tc).astype(F32), w_ref[...], precision=HI)
    acc = [None] * 4
    for kk in range(TOP_K):
        wk = wcol[:, kk:kk + 1]
        for half in range(2):
            lo, hi = _unpack_halves(y_refs[2 * kk + half][...])
            for q, val in ((half, lo), (2 + half, hi)):
                acc[q] = val * wk if acc[q] is None else acc[q] + val * wk
    o_ref[...] = base_ref[...] + mod_ref[5:6, :] * jnp.concatenate(acc, axis=1)


def _combine(yg, w_t, base, mod, S, T=256):
    N, D = base.shape
    C = yg.shape[2]
    tpb = S // T
    n_tiles = N // T
    row = pl.BlockSpec((T, D), lambda i: (i, 0))
    piece = lambda kk, half: pl.BlockSpec((None, T, C), lambda i: (half, kk * n_tiles + i, 0))
    return pl.pallas_call(
        _combine_kernel,
        grid=(n_tiles,),
        in_specs=[piece(kk, half) for kk in range(TOP_K) for half in range(2)] + [
            pl.BlockSpec((TOP_K, T), lambda i: (0, i)),
            row,
            pl.BlockSpec((None, 6, D), lambda i: (i // tpb, 0, 0))],
        out_specs=row,
        out_shape=jax.ShapeDtypeStruct((N, D), F32),
        compiler_params=_cp("parallel"),
        name="combine",
    )(*([yg] * (2 * TOP_K)), w_t, base, mod)


def _layer(x, c, positions, layer_idx, w_ada, b_ada, norm1_g, w_in, w_gate, b_gate,
           q_norm_g, k_norm_g, lambda_q1, lambda_k1, lambda_q2, lambda_k2, subln_g,
           rwkv_mu, w_decay0, w_decay2, a0, a2, g2, k_k, k_a, r_k, ln_x_w, ln_x_b,
           w_branch_a, w_branch_b, w_out, norm2_g, w_router, router_bias,
           w_expert_up_gate, w_expert_down, w_shared_up_gate, w_shared_down):
    B, S, D = x.shape
    N = B * S
    E = w_router.shape[1]
    da_width = w_branch_a.shape[0]
    rw_width = w_branch_b.shape[0]
    rw_cols = rwkv_mu.shape[0]
    lambda_init = 0.8 - 0.6 * math.exp(-0.3 * layer_idx)

    mod = _adaln(c, w_ada, b_ada)
    x2 = x.reshape(N, D)
    w_cat = jnp.concatenate([w_in, w_gate], axis=1).astype(BF16)
    q, k, v, prw, gate = _inproj(x2, mod, norm1_g, w_cat, b_gate, S, da_width, rw_cols)

    qn, kn = _qkprep(q, k, positions.reshape(N, 1), q_norm_g, k_norm_g)
    lam_vecs = jnp.stack([lambda_q1, lambda_k1, lambda_q2, lambda_k2])
    score_bound = 1.01 * DA_HEAD_DIM ** 0.5 * jnp.max(jnp.abs(q_norm_g)) * jnp.max(jnp.abs(k_norm_g))
    attn = _diff_attention(qn, kn, v, score_bound, lam_vecs, subln_g, B, S, lambda_init)

    r_, k_, v_, a_, ld_, g_ = _rwkv_prep(prw.reshape(B, S, rw_cols), rwkv_mu, w_decay0, w_decay2,
                                         a0, a2, g2, rw_width)
    rw = _rwkv_scan(r_, k_, v_, a_, ld_, g_, k_k, k_a, r_k.reshape(-1), ln_x_w, ln_x_b).reshape(N, rw_width)

    base, h2p, logits_t = _post(attn, rw, gate, x2, mod, w_branch_a.astype(BF16), w_branch_b.astype(BF16),
                                w_out.astype(BF16), norm2_g, w_router.T,
                                w_shared_up_gate.astype(BF16), w_shared_down.astype(BF16), S)

    eidx_t, w_t = _route(logits_t, router_bias)
    rank_t, counts = _ranks(eidx_t, E)
    R = EXPERT_TILE
    cnt = counts[:, 0].astype(I32)
    ustart = jnp.concatenate([jnp.zeros((1,), I32), jnp.cumsum(cnt)])
    pcnt = (cnt + R - 1) // R * R
    pstart = jnp.cumsum(pcnt) - pcnt
    dest_p = _dests(eidx_t, rank_t, pstart)
    n_rows = (N * TOP_K + E * (R - 1) + R - 1) // R * R
    xg = _sc_scatter_rows(h2p, dest_p, n_rows + (X_SLOTS - 1) * R)
    y = _experts(ustart, pstart, xg, w_expert_up_gate, w_expert_down, n_rows, R)
    yg = _sc_gather_rows(y, dest_p)
    out = _combine(yg, w_t, base, mod, S)
    return out.reshape(B, S, D)


def kernel(x, c, positions, w_ada, b_ada, norm1_g, w_in, w_gate, b_gate, q_norm_g, k_norm_g, lambda_q1, lambda_k1, lambda_q2, lambda_k2, subln_g, rwkv_mu, w_decay0, w_decay2, a0, a2, g2, k_k, k_a, r_k, ln_x_w, ln_x_b, w_branch_a, w_branch_b, w_out, norm2_g, w_router, router_bias, w_expert_up_gate, w_expert_down, w_shared_up_gate, w_shared_down):
    for l in range(w_ada.shape[0]):
        x = _layer(x, c, positions, l, w_ada[l], b_ada[l], norm1_g[l], w_in[l], w_gate[l], b_gate[l],
                   q_norm_g[l], k_norm_g[l], lambda_q1[l], lambda_k1[l], lambda_q2[l], lambda_k2[l],
                   subln_g[l], rwkv_mu[l], w_decay0[l], w_decay2[l], a0[l], a2[l], g2[l], k_k[l],
                   k_a[l], r_k[l], ln_x_w[l], ln_x_b[l], w_branch_a[l], w_branch_b[l], w_out[l],
                   norm2_g[l], w_router[l], router_bias[l], w_expert_up_gate[l], w_expert_down[l],
                   w_shared_up_gate[l], w_shared_down[l])
    return x
```

```python
import functools
import math

import jax
import jax.numpy as jnp
from jax import lax
from jax.experimental import pallas as pl
from jax.experimental.pallas import tpu as pltpu
from jax.experimental.pallas import tpu_sc as plsc

F32 = jnp.float32
BF16 = jnp.bfloat16
I32 = jnp.int32
U32 = jnp.uint32
HI = lax.Precision.HIGHEST

CHUNK = 64
ROPE_THETA = 10000.0
NORM_EPS = 1e-6
SUBLN_EPS = 1e-5
DA_HEAD_DIM = 64
RWKV_HEAD = 64
GN_EPS = 64e-5
TOP_K = 8
N_GROUPS = 8
TOPK_GROUPS = 4
ROUTED_SCALE = 2.5
EXPERT_TILE = 256
X_SLOTS = 4
RWKV_CHUNK = 64
LANES = 128
SC_WINDOW = 128
NEG = -1e30
MAX_PLAIN_SCORE = 40.0
VMEM_LIMIT = 56 * 1024 * 1024


def _cp(*sem):
    return pltpu.CompilerParams(dimension_semantics=sem, vmem_limit_bytes=VMEM_LIMIT)


def _bdot(a, b):
    return jnp.dot(a.astype(BF16), b.astype(BF16), preferred_element_type=F32)


def _fdot(a, b):
    return jnp.dot(a, b, precision=HI, preferred_element_type=F32)


def _nt(a, b, precision=None):
    return lax.dot_general(a, b, (((1,), (1,)), ((), ())), precision=precision,
                           preferred_element_type=F32)


def _tn(a, b, precision=None):
    return lax.dot_general(a, b, (((0,), (0,)), ((), ())), precision=precision,
                           preferred_element_type=F32)


def _pack_halves(x):
    c = x.shape[1] // 2
    lo = lax.bitcast_convert_type(x[:, :c], U32)
    hi = lax.bitcast_convert_type(x[:, c:], U32)
    return (hi & jnp.uint32(0xFFFF0000)) | (lo >> 16)


def _unpack_halves(w):
    lo = lax.bitcast_convert_type(w << 16, F32)
    hi = lax.bitcast_convert_type(w & jnp.uint32(0xFFFF0000), F32)
    return lo, hi


def _sigmoid(x):
    return 1.0 / (1.0 + jnp.exp(-x))


def _silu(x):
    return x * _sigmoid(x)


def _ada_kernel(c_ref, w_ref, b_ref, o_ref):
    o_ref[...] = _fdot(_silu(c_ref[...]), w_ref[...]) + b_ref[...]


def _adaln(c, w_ada, b_ada):
    B, D = c.shape
    rows = -(-B // 8) * 8
    cpad = jnp.zeros((rows, D), F32).at[:B].set(c)
    n_out = w_ada.shape[1]
    out = pl.pallas_call(
        _ada_kernel,
        grid=(n_out // D,),
        in_specs=[pl.BlockSpec((rows, D), lambda j: (0, 0)),
                  pl.BlockSpec((D, D), lambda j: (0, j)),
                  pl.BlockSpec((1, D), lambda j: (0, j))],
        out_specs=pl.BlockSpec((rows, D), lambda j: (0, j)),
        out_shape=jax.ShapeDtypeStruct((rows, n_out), F32),
        compiler_params=_cp("arbitrary"),
        name="adaln",
    )(cpad, w_ada, b_ada.reshape(1, n_out))
    return out[:B].reshape(B, n_out // D, D)


def _inproj_kernel(x_ref, mod_ref, g_ref, w_ref, bg_ref, q_ref, k_ref, v_ref, rw_ref, gate_ref,
                   *, da_width, rw_cols):
    x = x_ref[...]
    y = x * lax.rsqrt(jnp.mean(x * x, axis=-1, keepdims=True) + NORM_EPS) * g_ref[...]
    h = (y * (1.0 + mod_ref[1:2, :]) + mod_ref[0:1, :]).astype(BF16)
    c = 0
    for ref in (q_ref, k_ref, v_ref):
        ref[...] = jnp.dot(h, w_ref[:, c:c + da_width], preferred_element_type=F32).astype(ref.dtype)
        c += da_width
    step = 512
    for o in range(0, rw_cols, step):
        wd = min(step, rw_cols - o)
        rw_ref[:, o:o + wd] = jnp.dot(h, w_ref[:, c + o:c + o + wd], preferred_element_type=F32)
    c += rw_cols
    n_gate = gate_ref.shape[1]
    for o in range(0, n_gate, step):
        z = jnp.dot(h, w_ref[:, c + o:c + o + step], preferred_element_type=F32) + bg_ref[:, o:o + step]
        gate_ref[:, o:o + step] = _sigmoid(z).astype(gate_ref.dtype)


def _inproj(x2, mod, norm1_g, w_cat, b_gate, S, da_width, rw_cols, tm=256):
    N, D = x2.shape
    n_gate = b_gate.shape[0]
    tpb = S // tm
    kern = functools.partial(_inproj_kernel, da_width=da_width, rw_cols=rw_cols)
    row = lambda w: pl.BlockSpec((tm, w), lambda i: (i, 0))
    return pl.pallas_call(
        kern,
        grid=(N // tm,),
        in_specs=[row(D),
                  pl.BlockSpec((None, 6, D), lambda i: (i // tpb, 0, 0)),
                  pl.BlockSpec((1, D), lambda i: (0, 0)),
                  pl.BlockSpec(w_cat.shape, lambda i: (0, 0)),
                  pl.BlockSpec((1, n_gate), lambda i: (0, 0))],
        out_specs=[row(da_width), row(da_width), row(da_width), row(rw_cols), row(n_gate)],
        out_shape=[jax.ShapeDtypeStruct((N, da_width), F32),
                   jax.ShapeDtypeStruct((N, da_width), F32),
                   jax.ShapeDtypeStruct((N, da_width), BF16),
                   jax.ShapeDtypeStruct((N, rw_cols), F32),
                   jax.ShapeDtypeStruct((N, n_gate), BF16)],
        compiler_params=_cp("parallel"),
        name="inproj",
    )(x2, mod, norm1_g.reshape(1, D), w_cat, b_gate.reshape(1, n_gate))


def _qkprep_kernel(q_ref, k_ref, pos_ref, invf_ref, qg_ref, kg_ref, qo_ref, ko_ref, *, scale):
    tm = q_ref.shape[0]
    lane = lax.broadcasted_iota(I32, (tm, LANES), 1)
    first = lane < DA_HEAD_DIM
    lo_half = (lane & (DA_HEAD_DIM - 1)) < DA_HEAD_DIM // 2
    ang = pos_ref[...].astype(F32) * invf_ref[...]
    cos = jnp.cos(ang)
    sin = jnp.sin(ang)
    sin = jnp.where(lo_half, -sin, sin)

    def one(src, dst, g_ref, mult):
        for blk in range(src.shape[1] // LANES):
            x = src[:, blk * LANES:(blk + 1) * LANES]
            xx = x * x
            s_all = jnp.sum(xx, axis=-1, keepdims=True)
            s_first = jnp.sum(jnp.where(first, xx, 0.0), axis=-1, keepdims=True)
            ms = jnp.where(first, s_first, s_all - s_first) * (1.0 / DA_HEAD_DIM)
            xn = x * lax.rsqrt(ms + NORM_EPS) * g_ref[...]
            rot = jnp.where(lo_half, pltpu.roll(xn, LANES - DA_HEAD_DIM // 2, axis=1),
                            pltpu.roll(xn, DA_HEAD_DIM // 2, axis=1))
            dst[:, blk * LANES:(blk + 1) * LANES] = ((xn * cos + rot * sin) * mult).astype(dst.dtype)

    one(q_ref, qo_ref, qg_ref, scale)
    one(k_ref, ko_ref, kg_ref, 1.0)


def _qkprep(q, k, pos2, q_norm_g, k_norm_g, tm=512):
    N, W = q.shape
    d = DA_HEAD_DIM
    inv_freq = 1.0 / (ROPE_THETA ** (jnp.arange(0, d, 2, dtype=F32) / d))
    invf = jnp.tile(inv_freq, LANES // (d // 2)).reshape(1, LANES)
    row = pl.BlockSpec((tm, W), lambda i: (i, 0))
    vec = pl.BlockSpec((1, LANES), lambda i: (0, 0))
    return pl.pallas_call(
        functools.partial(_qkprep_kernel, scale=d ** -0.5 * math.log2(math.e)),
        grid=(N // tm,),
        in_specs=[row, row, pl.BlockSpec((tm, 1), lambda i: (i, 0)), vec, vec, vec],
        out_specs=[row, row],
        out_shape=[jax.ShapeDtypeStruct((N, W), BF16)] * 2,
        compiler_params=_cp("parallel"),
        name="qkprep",
    )(q, k, pos2, invf, jnp.tile(q_norm_g, 2).reshape(1, LANES), jnp.tile(k_norm_g, 2).reshape(1, LANES))


def _attn_kernel(flag_ref, q_ref, k_ref, v_ref, lam_ref, sg_ref, o_ref, qz_ref, m_ref, l_ref, lp_ref, acc_ref,
                 *, tq, lambda_init):
    i = pl.program_id(2)
    lane = lax.broadcasted_iota(I32, (tq, LANES), 1)
    q = q_ref[...]
    zero = jnp.zeros_like(q)
    qz_ref[0:tq, :] = jnp.where(lane < DA_HEAD_DIM, q, zero)
    qz_ref[tq:, :] = jnp.where(lane >= DA_HEAD_DIM, q, zero)
    acc_ref[...] = jnp.zeros_like(acc_ref)
    bounded = flag_ref[0] == 1

    def scores(j, masked):
        off = pl.multiple_of(j * tq, tq)
        s = _nt(qz_ref[...], k_ref[pl.ds(off, tq), :])
        if masked:
            row = lax.broadcasted_iota(I32, s.shape, 0)
            col = lax.broadcasted_iota(I32, s.shape, 1)
            s = jnp.where((col // CHUNK) <= ((row & (tq - 1)) // CHUNK), s, NEG)
        return s, off

    def plain_step(j, masked):
        s, off = scores(j, masked)
        pr = jnp.exp2(s)
        part = pr[:, 0:LANES]
        for cblk in range(1, tq // LANES):
            part = part + pr[:, cblk * LANES:(cblk + 1) * LANES]
        lp_ref[...] += part
        acc_ref[...] += jnp.dot(pr.astype(BF16), v_ref[pl.ds(off, tq), :], preferred_element_type=F32)

    def online_step(j, masked):
        s, off = scores(j, masked)
        m_old = m_ref[...]
        m_new = jnp.maximum(m_old, jnp.max(s, axis=-1, keepdims=True))
        alpha = jnp.exp2(m_old - m_new)
        pr = jnp.exp2(s - m_new)
        l_ref[...] = alpha * l_ref[...] + jnp.sum(pr, axis=-1, keepdims=True)
        acc_ref[...] = alpha * acc_ref[...] + jnp.dot(pr.astype(BF16), v_ref[pl.ds(off, tq), :],
                                                      preferred_element_type=F32)
        m_ref[...] = m_new

    def run(step):
        def body(j, carry):
            step(j, False)
            return carry
        lax.fori_loop(0, i, body, 0)
        step(i, True)

    @pl.when(bounded)
    def _():
        lp_ref[...] = jnp.zeros_like(lp_ref)
        run(plain_step)
        l_ref[...] = jnp.sum(lp_ref[...], axis=-1, keepdims=True)

    @pl.when(jnp.logical_not(bounded))
    def _():
        m_ref[...] = jnp.full_like(m_ref, NEG)
        l_ref[...] = jnp.zeros_like(l_ref)
        run(online_step)

    lv = lam_ref[...]
    lam = (jnp.exp(jnp.sum(lv[0:1] * lv[1:2], keepdims=True))
           - jnp.exp(jnp.sum(lv[2:3] * lv[3:4], keepdims=True)) + lambda_init)
    o1 = acc_ref[0:tq, :] / l_ref[0:tq, :]
    o2 = acc_ref[tq:, :] / l_ref[tq:, :]
    o = o1 - lam * o2
    o = o * lax.rsqrt(jnp.mean(o * o, axis=-1, keepdims=True) + SUBLN_EPS) * sg_ref[...]
    o_ref[...] = (o * (1.0 - lambda_init)).astype(o_ref.dtype)


def _diff_attention(qn, kn, v, score_bound, lam_vecs, subln_g, B, S, lambda_init, tq=512):
    W = qn.shape[1]
    H = W // LANES
    q3 = qn.reshape(B, S, W)
    k3 = kn.reshape(B, S, W)
    v3 = v.reshape(B, S, W)
    flag = (score_bound <= MAX_PLAIN_SCORE).astype(I32).reshape(1)
    qblk = pl.BlockSpec((None, tq, LANES), lambda b, h, i, f: (b, i, h))
    kvblk = pl.BlockSpec((None, S, LANES), lambda b, h, i, f: (b, 0, h))
    grid_spec = pltpu.PrefetchScalarGridSpec(
        num_scalar_prefetch=1,
        grid=(B, H, S // tq),
        in_specs=[qblk, kvblk, kvblk,
                  pl.BlockSpec((4, DA_HEAD_DIM), lambda b, h, i, f: (0, 0)),
                  pl.BlockSpec((1, LANES), lambda b, h, i, f: (0, 0))],
        out_specs=qblk,
        scratch_shapes=[pltpu.VMEM((2 * tq, LANES), BF16),
                        pltpu.VMEM((2 * tq, 1), F32),
                        pltpu.VMEM((2 * tq, 1), F32),
                        pltpu.VMEM((2 * tq, LANES), F32),
                        pltpu.VMEM((2 * tq, LANES), F32)],
    )
    out = pl.pallas_call(
        functools.partial(_attn_kernel, tq=tq, lambda_init=lambda_init),
        grid_spec=grid_spec,
        out_shape=jax.ShapeDtypeStruct((B, S, W), BF16),
        compiler_params=_cp("parallel", "parallel", "arbitrary"),
        name="diff_attn",
    )(flag, q3, k3, v3, lam_vecs, subln_g.reshape(1, LANES))
    return out.reshape(B * S, W)


def _rwkv_prep_kernel(p_ref, prev_ref, mu_ref, w0_ref, w2_ref, a0_ref, a2_ref, g2_ref,
                      r_ref, k_ref, v_ref, a_ref, ld_ref, g_ref, *, width):
    i = pl.program_id(1)
    p = p_ref[...]
    last_prev = jnp.where(i > 0, prev_ref[7:8, :], 0.0)
    rowi = lax.broadcasted_iota(I32, p.shape, 0)
    prev = jnp.where(rowi == 0, last_prev, pltpu.roll(p, 1, axis=0))
    xs = p + (prev - p) * mu_ref[...]
    r_ref[...] = xs[:, 0:width]
    k_ref[...] = xs[:, width:2 * width]
    v_ref[...] = xs[:, 2 * width:3 * width]
    xwa = xs[:, 3 * width:3 * width + LANES]
    xg = xs[:, 3 * width + LANES:]
    z = w0_ref[...] + _bdot(jnp.tanh(xwa), w2_ref[...])
    w = -(jnp.maximum(-z, 0.0) + jnp.log(1.0 + jnp.exp(-jnp.abs(z)))) - 0.5
    ld_ref[...] = -jnp.exp(w)
    a_ref[...] = _sigmoid(a0_ref[...] + _bdot(xwa, a2_ref[...]))
    g_ref[...] = _bdot(_sigmoid(xg), g2_ref[...])


def _rwkv_prep(prw3, mu, w_decay0, w_decay2, a0, a2, g2, width, tm=256):
    B, S, C = prw3.shape
    dl, al = w_decay2.shape[0], a2.shape[0]
    assert dl + al == LANES and g2.shape[0] == LANES
    w2p = jnp.zeros((LANES, width), F32).at[:dl].set(w_decay2)
    a2p = jnp.zeros((LANES, width), F32).at[dl:].set(a2)
    vec = lambda n: pl.BlockSpec((1, n), lambda b, i: (0, 0))
    mat = pl.BlockSpec((LANES, width), lambda b, i: (0, 0))
    out = pl.BlockSpec((None, tm, width), lambda b, i: (b, i, 0))
    return pl.pallas_call(
        functools.partial(_rwkv_prep_kernel, width=width),
        grid=(B, S // tm),
        in_specs=[pl.BlockSpec((None, tm, C), lambda b, i: (b, i, 0)),
                  pl.BlockSpec((None, 8, C), lambda b, i: (b, jnp.maximum(i * (tm // 8) - 1, 0), 0)),
                  vec(C), vec(width), mat, vec(width), mat, mat],
        out_specs=[out] * 6,
        out_shape=[jax.ShapeDtypeStruct((B, S, width), F32)] * 6,
        compiler_params=_cp("parallel", "parallel"),
        name="rwkv_prep",
    )(prw3, prw3, mu.reshape(1, C), w_decay0.reshape(1, width), w2p, a0.reshape(1, width), a2p, g2)


def _stackmask(m):
    lane = lax.broadcasted_iota(I32, m.shape, 1)
    z = jnp.zeros_like(m)
    return jnp.concatenate([jnp.where(lane < RWKV_HEAD, m, z), jnp.where(lane >= RWKV_HEAD, m, z)], axis=0)


def _pair_sum(x, first):
    s1 = jnp.sum(jnp.where(first, x, 0.0), axis=-1, keepdims=True)
    s2 = jnp.sum(jnp.where(first, 0.0, x), axis=-1, keepdims=True)
    return jnp.where(first, s1, s2)


def _rwkv_scan_kernel(r_ref, k_ref, v_ref, a_ref, ld_ref, g_ref, kk_ref, ka_ref, rk_ref, lnw_ref, lnb_ref,
                      o_ref, s_ref, *, L):
    tm, W = r_ref.shape
    n_chunks = tm // L
    n_pairs = W // LANES
    hd = RWKV_HEAD
    bf = lambda t: t.astype(BF16)

    @pl.when(pl.program_id(1) == 0)
    def _():
        s_ref[...] = jnp.zeros_like(s_ref)

    row = lax.broadcasted_iota(I32, (tm, tm), 0)
    col = lax.broadcasted_iota(I32, (tm, tm), 1)
    tri = jnp.where(jnp.logical_and(col <= row, (col // L) == (row // L)), 1.0, 0.0).astype(BF16)
    ld = ld_ref[...]
    ld_hi = bf(ld)
    rem = ld - ld_hi.astype(F32)
    ld_mid = bf(rem)
    ld_lo = bf(rem - ld_mid.astype(F32))
    c = (jnp.dot(tri, ld_hi, preferred_element_type=F32) + jnp.dot(tri, ld_mid, preferred_element_type=F32)
         + jnp.dot(tri, ld_lo, preferred_element_type=F32))
    ec = jnp.exp(c)
    eci = jnp.exp(-c)
    ecm = jnp.exp(c - ld)
    r = r_ref[...]
    k = k_ref[...]
    v = v_ref[...]
    a = a_ref[...]
    kkr = k * kk_ref[...]
    kmod = k * (1.0 + (a - 1.0) * ka_ref[...])
    brk = r * kmod * rk_ref[...]

    lane = lax.broadcasted_iota(I32, (L, LANES), 1)
    rowl = lax.broadcasted_iota(I32, (L, LANES), 0)
    first = lane < hd
    lane_h = lane & (hd - 1)
    strict = lane_h < rowl
    incl = lane_h <= rowl
    eye = jnp.where(lane_h == rowl, 1.0, 0.0)

    chains = [(ch, p) for ch in range(n_chunks) for p in range(n_pairs)]
    rsl = lambda ch: slice(ch * L, (ch + 1) * L)
    csl = lambda p: slice(p * LANES, (p + 1) * LANES)
    fdot = lambda x, y: jnp.dot(x, y, preferred_element_type=F32)
    at, bt, kt, rt, vh, g_l = {}, {}, {}, {}, {}, {}
    for c_ in chains:
        ch, p = c_
        rs, cs = rsl(ch), csl(p)
        kkh = kkr[rs, cs]
        kkh = kkh / jnp.maximum(jnp.sqrt(_pair_sum(kkh * kkh, first)), 1e-12)
        vh[c_] = v[rs, cs]
        g_l[c_] = ec[ch * L + L - 1:ch * L + L, cs]
        at[c_] = -kkh * ecm[rs, cs]
        bt[c_] = kkh * a[rs, cs] * eci[rs, cs]
        kt[c_] = kmod[rs, cs] * eci[rs, cs]
        rt[c_] = r[rs, cs] * ec[rs, cs]
    gm = {c_: _nt(bf(jnp.concatenate([at[c_], rt[c_]], axis=0)),
                  jnp.concatenate([_stackmask(bf(bt[c_])), _stackmask(bf(kt[c_]))], axis=0)) for c_ in chains}
    a_ab = {c_: jnp.where(strict, gm[c_][:L, :LANES], 0.0) for c_ in chains}
    vsm = {c_: _stackmask(bf(vh[c_])) for c_ in chains}
    cmat = {c_: fdot(bf(jnp.where(strict, gm[c_][:L, LANES:], 0.0)), vsm[c_]) for c_ in chains}
    t_inv = {c_: eye + a_ab[c_] for c_ in chains}
    pw = {c_: bf(a_ab[c_]) for c_ in chains}
    for _ in range(int(math.log2(L)) - 1):
        pw = {c_: bf(fdot(pw[c_], _stackmask(pw[c_]))) for c_ in chains}
        t_inv = {c_: t_inv[c_] + fdot(pw[c_], _stackmask(bf(t_inv[c_]))) for c_ in chains}
    zz = {c_: fdot(bf(t_inv[c_]), jnp.concatenate([_stackmask(bf(at[c_])), _stackmask(bf(cmat[c_]))], axis=1))
          for c_ in chains}
    qy = {c_: fdot(bf(jnp.where(incl, gm[c_][L:, :LANES], 0.0)),
                   jnp.concatenate([_stackmask(bf(zz[c_][:, :LANES])), _stackmask(bf(zz[c_][:, LANES:]))], axis=1))
          for c_ in chains}
    y0 = {c_: qy[c_][:, LANES:] + fdot(bf(jnp.where(incl, gm[c_][L:, LANES:], 0.0)), vsm[c_]) for c_ in chains}
    qa = {c_: bf(jnp.concatenate([rt[c_] + qy[c_][:, :LANES], zz[c_][:, :LANES]], axis=0)) for c_ in chains}
    bkg = {c_: bf(jnp.concatenate([bt[c_] * g_l[c_], kt[c_] * g_l[c_]], axis=0)) for c_ in chains}

    lane_s = lax.broadcasted_iota(I32, (hd, LANES), 1)
    sp = [s_ref[p] for p in range(n_pairs)]
    for ch in range(n_chunks):
        rs = rsl(ch)
        yw = [_nt(qa[ch, p], _stackmask(bf(sp[p]))) for p in range(n_pairs)]
        upd = [_tn(bf(jnp.concatenate([yw[p][L:] + zz[ch, p][:, LANES:], vh[ch, p]], axis=0)), bkg[ch, p])
               for p in range(n_pairs)]
        for p in range(n_pairs):
            cs = csl(p)
            sp[p] = sp[p] * g_l[ch, p] + jnp.where(lane_s < hd, upd[p][:hd], upd[p][hd:])
            y = yw[p][:L] + y0[ch, p]
            mean = _pair_sum(y, first) * (1.0 / hd)
            yc = y - mean
            var = _pair_sum(yc * yc, first) * (1.0 / hd)
            yn = yc * lax.rsqrt(var + GN_EPS) * lnw_ref[:, cs] + lnb_ref[:, cs]
            bonus = _pair_sum(brk[rs, cs], first) * vh[ch, p]
            o_ref[rs, cs] = ((yn + bonus) * g_ref[rs, cs]).astype(o_ref.dtype)
    for p in range(n_pairs):
        s_ref[p] = sp[p]


def _rwkv_scan(r, k, v, a, ld, g, k_k, k_a, r_k, ln_w, ln_b, L=RWKV_CHUNK, tm=256):
    B, S, W = r.shape
    seq = pl.BlockSpec((None, tm, W), lambda b, c: (b, c, 0))
    vec = pl.BlockSpec((1, W), lambda b, c: (0, 0))
    return pl.pallas_call(
        functools.partial(_rwkv_scan_kernel, L=L),
        grid=(B, S // tm),
        in_specs=[seq] * 6 + [vec] * 5,
        out_specs=seq,
        out_shape=jax.ShapeDtypeStruct((B, S, W), BF16),
        scratch_shapes=[pltpu.VMEM((W // LANES, RWKV_HEAD, LANES), F32)],
        compiler_params=_cp("parallel", "arbitrary"),
        name="rwkv_scan",
    )(r, k, v, a, ld, g, k_k.reshape(1, W), k_a.reshape(1, W), r_k.reshape(1, W),
      ln_w.reshape(1, W), ln_b.reshape(1, W))


def _post_kernel(attn_ref, rw_ref, gate_ref, x_ref, mod_ref, wa_ref, wb_ref, wo_ref, g2_ref, wrh_ref, wrm_ref,
                 sug_ref, sd_ref, base_ref, h2p_ref, lg_ref):
    D = x_ref.shape[1]
    ya = jnp.dot(attn_ref[...], wa_ref[...], preferred_element_type=F32)
    yb = jnp.dot(rw_ref[...], wb_ref[...], preferred_element_type=F32)
    m = gate_ref[:, 0:D] * ya + gate_ref[:, D:] * yb
    x1 = x_ref[...] + mod_ref[2:3, :] * jnp.dot(m.astype(BF16), wo_ref[...], preferred_element_type=F32)
    y = x1 * lax.rsqrt(jnp.mean(x1 * x1, axis=-1, keepdims=True) + NORM_EPS) * g2_ref[...]
    h2 = y * (1.0 + mod_ref[4:5, :]) + mod_ref[3:4, :]
    hb = h2.astype(BF16)
    hm = (h2 - hb.astype(F32)).astype(BF16)
    lg_ref[...] = _nt(wrh_ref[...], hb) + _nt(wrh_ref[...], hm) + _nt(wrm_ref[...], hb)
    packed = _pack_halves(hb.astype(F32))
    half = packed.shape[1] // 2
    h2p_ref[0] = packed[:, :half]
    h2p_ref[1] = packed[:, half:]
    F = sd_ref.shape[0]
    gu = jnp.dot(hb, sug_ref[...], preferred_element_type=F32)
    shared = jnp.dot((_silu(gu[:, :F]) * gu[:, F:]).astype(BF16), sd_ref[...], preferred_element_type=F32)
    base_ref[...] = x1 + mod_ref[5:6, :] * shared


def _post(attn, rw, gate, x2, mod, wa, wb, wo, norm2_g, w_router_t, sug, sd, S, tm=256):
    N, D = x2.shape
    E = w_router_t.shape[0]
    wr_hi = w_router_t.astype(BF16)
    wr_mid = (w_router_t - wr_hi.astype(F32)).astype(BF16)
    tpb = S // tm
    row = lambda w: pl.BlockSpec((tm, w), lambda i: (i, 0))
    full = lambda a: pl.BlockSpec(a.shape, lambda i: (0, 0))
    return pl.pallas_call(
        _post_kernel,
        grid=(N // tm,),
        in_specs=[row(attn.shape[1]), row(rw.shape[1]), row(gate.shape[1]), row(D),
                  pl.BlockSpec((None, 6, D), lambda i: (i // tpb, 0, 0)),
                  full(wa), full(wb), full(wo), pl.BlockSpec((1, D), lambda i: (0, 0)), full(wr_hi), full(wr_mid),
                  full(sug), full(sd)],
        out_specs=[row(D), pl.BlockSpec((2, tm, D // 4), lambda i: (0, i, 0)), pl.BlockSpec((E, tm), lambda i: (0, i))],
        out_shape=[jax.ShapeDtypeStruct((N, D), F32), jax.ShapeDtypeStruct((2, N, D // 4), U32),
                   jax.ShapeDtypeStruct((E, N), F32)],
        compiler_params=_cp("parallel"),
        name="post_mixer",
    )(attn, rw, gate, x2, mod, wa, wb, wo, norm2_g.reshape(1, D), wr_hi, wr_mid, sug, sd)


def _first_argmax(x, idx, sentinel):
    m = jnp.max(x, axis=0, keepdims=True)
    return m, jnp.min(jnp.where(x == m, idx, sentinel), axis=0, keepdims=True)


def _route_kernel(lg_ref, bias_ref, e_ref, w_ref):
    E, T = lg_ref.shape
    gsz = E // N_GROUPS
    scores = _sigmoid(lg_ref[...])
    biased = scores + bias_ref[...]
    ig = lax.broadcasted_iota(I32, (gsz, T), 0)
    gs = []
    for g in range(N_GROUPS):
        blk = biased[g * gsz:(g + 1) * gsz, :]
        m1, i1 = _first_argmax(blk, ig, gsz)
        m2 = jnp.max(jnp.where(ig == i1, -jnp.inf, blk), axis=0, keepdims=True)
        gs.append(m1 + m2)
    gsc = jnp.concatenate(gs, axis=0)
    i8 = lax.broadcasted_iota(I32, (N_GROUPS, T), 0)
    chosen = jnp.zeros((N_GROUPS, T), F32)
    for _ in range(TOPK_GROUPS):
        _, gi = _first_argmax(gsc, i8, N_GROUPS)
        hit = i8 == gi
        chosen = jnp.where(hit, 1.0, chosen)
        gsc = jnp.where(hit, -jnp.inf, gsc)
    masked = jnp.concatenate(
        [jnp.where(chosen[g:g + 1, :] > 0.0, biased[g * gsz:(g + 1) * gsz, :], -jnp.inf)
         for g in range(N_GROUPS)], axis=0)
    ie = lax.broadcasted_iota(I32, (E, T), 0)
    idxs, wts = [], []
    for _ in range(TOP_K):
        _, ei = _first_argmax(masked, ie, E)
        hit = ie == ei
        idxs.append(ei)
        wts.append(jnp.sum(jnp.where(hit, scores, 0.0), axis=0, keepdims=True))
        masked = jnp.where(hit, -jnp.inf, masked)
    wt = jnp.concatenate(wts, axis=0)
    e_ref[...] = jnp.concatenate(idxs, axis=0)
    w_ref[...] = wt / jnp.sum(wt, axis=0, keepdims=True) * ROUTED_SCALE


def _route(logits_t, router_bias, T=512):
    E, N = logits_t.shape
    blk = pl.BlockSpec((TOP_K, T), lambda i: (0, i))
    return pl.pallas_call(
        _route_kernel,
        grid=(N // T,),
        in_specs=[pl.BlockSpec((E, T), lambda i: (0, i)), pl.BlockSpec((E, 1), lambda i: (0, 0))],
        out_specs=[blk, blk],
        out_shape=[jax.ShapeDtypeStruct((TOP_K, N), I32), jax.ShapeDtypeStruct((TOP_K, N), F32)],
        compiler_params=_cp("parallel"),
        name="route",
    )(logits_t, router_bias.reshape(E, 1))


def _rank_kernel(e_ref, rank_ref, cnt_ref, carry_ref, *, n_experts):
    T = e_ref.shape[1]

    @pl.when(pl.program_id(0) == 0)
    def _():
        carry_ref[...] = jnp.zeros_like(carry_ref)

    ie = lax.broadcasted_iota(I32, (n_experts, T), 0)
    e = e_ref[...]
    hits = [ie == e[kk:kk + 1, :] for kk in range(TOP_K)]
    onehot = jnp.zeros((n_experts, T), F32)
    for hsel in hits:
        onehot = onehot + hsel.astype(F32)
    tr = lax.broadcasted_iota(I32, (T, T), 0)
    tc = lax.broadcasted_iota(I32, (T, T), 1)
    before = (tr < tc).astype(BF16)
    base = _bdot(onehot, before) + carry_ref[:, 0:1]
    rank_ref[...] = jnp.concatenate(
        [jnp.sum(jnp.where(hsel, base, 0.0), axis=0, keepdims=True) for hsel in hits], axis=0).astype(I32)
    carry_ref[...] = carry_ref[...] + jnp.sum(onehot, axis=1, keepdims=True)
    cnt_ref[...] = carry_ref[...]


def _ranks(eidx_t, n_experts, T=512):
    N = eidx_t.shape[1]
    blk = pl.BlockSpec((TOP_K, T), lambda i: (0, i))
    cnt = pl.BlockSpec((n_experts, LANES), lambda i: (0, 0))
    return pl.pallas_call(
        functools.partial(_rank_kernel, n_experts=n_experts),
        grid=(N // T,),
        in_specs=[blk],
        out_specs=[blk, cnt],
        out_shape=[jax.ShapeDtypeStruct((TOP_K, N), I32), jax.ShapeDtypeStruct((n_experts, LANES), F32)],
        scratch_shapes=[pltpu.VMEM((n_experts, LANES), F32)],
        compiler_params=_cp("arbitrary"),
        name="ranks",
    )(eidx_t)


def _dest_kernel(e_ref, rank_ref, start_ref, d_ref):
    E = start_ref.shape[0]
    T = e_ref.shape[1]
    ie = lax.broadcasted_iota(I32, (E, T), 0)
    e = e_ref[...]
    start = start_ref[:, 0:1]
    rows = [jnp.sum(jnp.where(ie == e[kk:kk + 1, :], start, 0.0), axis=0, keepdims=True) for kk in range(TOP_K)]
    d_ref[...] = jnp.concatenate(rows, axis=0).astype(I32) + rank_ref[...]


def _dests(eidx_t, rank_t, pstart, T=512):
    N = eidx_t.shape[1]
    E = pstart.shape[0]
    blk = pl.BlockSpec((TOP_K, T), lambda i: (0, i))
    return pl.pallas_call(
        _dest_kernel,
        grid=(N // T,),
        in_specs=[blk, blk, pl.BlockSpec((E, LANES), lambda i: (0, 0))],
        out_specs=blk,
        out_shape=jax.ShapeDtypeStruct((TOP_K, N), I32),
        compiler_params=_cp("parallel"),
        name="dests",
    )(eidx_t, rank_t, jnp.broadcast_to(pstart.astype(F32)[:, None], (E, LANES)))


def _expert_kernel(us_ref, ps_ref, x_hbm, wug_ref, wd_ref, y_hbm,
                   wug_bf, wd_bf, xbuf, ybuf, cnt_ref, xsem, ysem, *, n_rows):
    e = pl.program_id(0)
    _, _, R, Ch = xbuf.shape
    C = 2 * Ch
    F = wd_ref.shape[0]

    def x_copy(slot, half, row):
        return pltpu.make_async_copy(x_hbm.at[half, pl.ds(row, R)], xbuf.at[slot, half], xsem.at[slot])

    def y_copy(slot, half, row):
        return pltpu.make_async_copy(ybuf.at[slot, half], y_hbm.at[half, pl.ds(row, R)], ysem.at[slot])

    def start(copy, slot, row):
        for half in range(2):
            copy(slot, half, row).start(priority=1)

    def wait(copy, slot):
        for half in range(2):
            copy(slot, half, 0).wait()

    @pl.when(e == 0)
    def _():
        cnt_ref[0] = 0
        for t in range(X_SLOTS - 1):
            start(x_copy, t, t * R)
        ybuf[...] = jnp.zeros_like(ybuf)
        for slot in range(2):
            start(y_copy, slot, n_rows + slot * R)

    wug_bf[...] = wug_ref[...].astype(BF16)
    wd_bf[...] = wd_ref[...].astype(BF16)
    n_valid = us_ref[e + 1] - us_ref[e]
    pbase = ps_ref[e]
    n_tiles = lax.shift_right_logical(n_valid + (R - 1), int(math.log2(R)))
    rowid = lax.broadcasted_iota(I32, (R, C), 0)

    def tile(i, carry):
        n_done = cnt_ref[0]
        slot = n_done & 1
        xslot = lax.rem(n_done, X_SLOTS)
        ahead = n_done + (X_SLOTS - 1)
        start(x_copy, lax.rem(ahead, X_SLOTS), pl.multiple_of(ahead * R, R))
        wait(x_copy, xslot)
        x = jnp.concatenate([xbuf[xslot, 0], xbuf[xslot, 1]], axis=1)
        x = jnp.where(rowid < n_valid - i * R, x, jnp.uint32(0))
        lo, hi = _unpack_halves(x)
        gu = (jnp.dot(lo.astype(BF16), wug_bf[0:C, :], preferred_element_type=F32)
              + jnp.dot(hi.astype(BF16), wug_bf[C:, :], preferred_element_type=F32))
        hid = (_silu(gu[:, :F]) * gu[:, F:]).astype(BF16)
        y = jnp.dot(hid, wd_bf[...], preferred_element_type=F32)
        wait(y_copy, slot)
        packed = _pack_halves(y.astype(BF16).astype(F32))
        ybuf[slot, 0] = packed[:, :Ch]
        ybuf[slot, 1] = packed[:, Ch:]
        start(y_copy, slot, pl.multiple_of(pbase + i * R, R))
        cnt_ref[0] = n_done + 1
        return carry

    lax.fori_loop(0, n_tiles, tile, 0)

    @pl.when(e == pl.num_programs(0) - 1)
    def _():
        for t in range(X_SLOTS - 1):
            wait(x_copy, lax.rem(cnt_ref[0] + t, X_SLOTS))
        for slot in range(2):
            wait(y_copy, slot)
        ybuf[0] = jnp.zeros((2, R, Ch), U32)
        first = lax.shift_right_logical(pbase + n_tiles * R, int(math.log2(R)))
        n_left = n_rows // R - first

        def fill(t, carry):
            start(y_copy, 0, pl.multiple_of((first + t) * R, R))
            return carry

        def drain(t, carry):
            wait(y_copy, 0)
            return carry

        lax.fori_loop(0, n_left, fill, 0)
        lax.fori_loop(0, n_left, drain, 0)


def _experts(ustart, pstart, xg, w_ug, w_d, n_rows, R):
    _, _, Ch = xg.shape
    E, D, F2 = w_ug.shape
    F = w_d.shape[1]
    grid_spec = pltpu.PrefetchScalarGridSpec(
        num_scalar_prefetch=2,
        grid=(E,),
        in_specs=[pl.BlockSpec(memory_space=pl.ANY),
                  pl.BlockSpec((None, D, F2), lambda e, us, ps: (e, 0, 0)),
                  pl.BlockSpec((None, F, D), lambda e, us, ps: (e, 0, 0))],
        out_specs=pl.BlockSpec(memory_space=pl.ANY),
        scratch_shapes=[pltpu.VMEM((D, F2), BF16), pltpu.VMEM((F, D), BF16),
                        pltpu.VMEM((X_SLOTS, 2, R, Ch), U32), pltpu.VMEM((2, 2, R, Ch), U32), pltpu.SMEM((1,), I32),
                        pltpu.SemaphoreType.DMA((X_SLOTS,)), pltpu.SemaphoreType.DMA((2,))],
    )
    return pl.pallas_call(
        functools.partial(_expert_kernel, n_rows=n_rows),
        grid_spec=grid_spec,
        out_shape=jax.ShapeDtypeStruct((2, n_rows + 2 * R, Ch), U32),
        compiler_params=_cp("arbitrary"),
        name="experts",
    )(ustart, pstart, xg, w_ug, w_d)


def _sc_scatter_rows(src, idx, n_rows):
    H, N, C = src.shape
    K = idx.shape[0]
    per_row = N // SC_WINDOW
    mesh = plsc.VectorSubcoreMesh(core_axis_name="c", subcore_axis_name="s")

    @functools.partial(pl.kernel, out_type=jax.ShapeDtypeStruct((H, n_rows, C), src.dtype), mesh=mesh,
                       scratch_types=[])
    def scatter_kernel(x_hbm, i_hbm, o_hbm):
        for h in range(H):
            def body(x_vmem, i_vmem):
                pltpu.sync_copy(x_vmem, o_hbm.at[h].at[i_vmem.at[0]])

            pltpu.emit_pipeline(
                body,
                grid=(K * per_row,),
                in_specs=[pl.BlockSpec((SC_WINDOW, C), lambda i: (i % per_row, 0)),
                          pl.BlockSpec((1, SC_WINDOW), lambda i: (i // per_row, i % per_row))],
                out_specs=[],
                core_axis_name=("c", "s"),
                dimension_semantics=(pltpu.PARALLEL,),
            )(x_hbm.at[h], i_hbm)

    return scatter_kernel(src, idx)


def _sc_gather_rows(src, idx):
    H, _, C = src.shape
    K, N = idx.shape
    per_row = N // SC_WINDOW
    mesh = plsc.VectorSubcoreMesh(core_axis_name="c", subcore_axis_name="s")

    @functools.partial(pl.kernel, out_type=jax.ShapeDtypeStruct((H, K * N, C), src.dtype), mesh=mesh,
                       scratch_types=[])
    def gather_kernel(x_hbm, i_hbm, o_hbm):
        for h in range(H):
            def body(i_vmem, o_vmem):
                pltpu.sync_copy(x_hbm.at[h].at[i_vmem.at[0]], o_vmem)

            pltpu.emit_pipeline(
                body,
                grid=(K * per_row,),
                in_specs=[pl.BlockSpec((1, SC_WINDOW), lambda i: (i // per_row, i % per_row))],
                out_specs=[pl.BlockSpec((SC_WINDOW, C), lambda i: (i, 0))],
                core_axis_name=("c", "s"),
                dimension_semantics=(pltpu.PARALLEL,),
            )(i_hbm, o_hbm.at[h])

    return gather_kernel(src, idx)


def _combine_kernel(*refs):
    y_refs = refs[:2 * TOP_K]
    w_ref, base_ref, mod_ref, o_ref = refs[2 * TOP_K:]
    T = base_ref.shape[0]
    tr = lax.broadcasted_iota(I32, (T, T), 0)
    tc = lax.broadcasted_iota(I32, (T, T), 1)
    wcol = _nt((tr == tc).astype(F32), w_ref[...], precision=HI)
    acc = [None] * 4
    for kk in range(TOP_K):
        wk = wcol[:, kk:kk + 1]
        for half in range(2):
            lo, hi = _unpack_halves(y_refs[2 * kk + half][...])
            for q, val in ((half, lo), (2 + half, hi)):
                acc[q] = val * wk if acc[q] is None else acc[q] + val * wk
    o_ref[...] = base_ref[...] + mod_ref[5:6, :] * jnp.concatenate(acc, axis=1)


def _combine(yg, w_t, base, mod, S, T=256):
    N, D = base.shape
    C = yg.shape[2]
    tpb = S // T
    n_tiles = N // T
    row = pl.BlockSpec((T, D), lambda i: (i, 0))
    piece = lambda kk, half: pl.BlockSpec((None, T, C), lambda i: (half, kk * n_tiles + i, 0))
    return pl.pallas_call(
        _combine_kernel,
        grid=(n_tiles,),
        in_specs=[piece(kk, half) for kk in range(TOP_K) for half in range(2)] + [
            pl.BlockSpec((TOP_K, T), lambda i: (0, i)),
            row,
            pl.BlockSpec((None, 6, D), lambda i: (i // tpb, 0, 0))],
        out_specs=row,
        out_shape=jax.ShapeDtypeStruct((N, D), F32),
        compiler_params=_cp("parallel"),
        name="combine",
    )(*([yg] * (2 * TOP_K)), w_t, base, mod)


def _layer(x, c, positions, layer_idx, w_ada, b_ada, norm1_g, w_in, w_gate, b_gate,
           q_norm_g, k_norm_g, lambda_q1, lambda_k1, lambda_q2, lambda_k2, subln_g,
           rwkv_mu, w_decay0, w_decay2, a0, a2, g2, k_k, k_a, r_k, ln_x_w, ln_x_b,
           w_branch_a, w_branch_b, w_out, norm2_g, w_router, router_bias,
           w_expert_up_gate, w_expert_down, w_shared_up_gate, w_shared_down):
    B, S, D = x.shape
    N = B * S
    E = w_router.shape[1]
    da_width = w_branch_a.shape[0]
    rw_width = w_branch_b.shape[0]
    rw_cols = rwkv_mu.shape[0]
    lambda_init = 0.8 - 0.6 * math.exp(-0.3 * layer_idx)

    mod = _adaln(c, w_ada, b_ada)
    x2 = x.reshape(N, D)
    w_cat = jnp.concatenate([w_in, w_gate], axis=1).astype(BF16)
    q, k, v, prw, gate = _inproj(x2, mod, norm1_g, w_cat, b_gate, S, da_width, rw_cols)

    qn, kn = _qkprep(q, k, positions.reshape(N, 1), q_norm_g, k_norm_g)
    lam_vecs = jnp.stack([lambda_q1, lambda_k1, lambda_q2, lambda_k2])
    score_bound = 1.01 * DA_HEAD_DIM ** 0.5 * jnp.max(jnp.abs(q_norm_g)) * jnp.max(jnp.abs(k_norm_g))
    attn = _diff_attention(qn, kn, v, score_bound, lam_vecs, subln_g, B, S, lambda_init)

    r_, k_, v_, a_, ld_, g_ = _rwkv_prep(prw.reshape(B, S, rw_cols), rwkv_mu, w_decay0, w_decay2,
                                         a0, a2, g2, rw_width)
    rw = _rwkv_scan(r_, k_, v_, a_, ld_, g_, k_k, k_a, r_k.reshape(-1), ln_x_w, ln_x_b).reshape(N, rw_width)

    base, h2p, logits_t = _post(attn, rw, gate, x2, mod, w_branch_a.astype(BF16), w_branch_b.astype(BF16),
                                w_out.astype(BF16), norm2_g, w_router.T,
                                w_shared_up_gate.astype(BF16), w_shared_down.astype(BF16), S)

    eidx_t, w_t = _route(logits_t, router_bias)
    rank_t, counts = _ranks(eidx_t, E)
    R = EXPERT_TILE
    cnt = counts[:, 0].astype(I32)
    ustart = jnp.concatenate([jnp.zeros((1,), I32), jnp.cumsum(cnt)])
    pcnt = (cnt + R - 1) // R * R
    pstart = jnp.cumsum(pcnt) - pcnt
    dest_p = _dests(eidx_t, rank_t, pstart)
    n_rows = (N * TOP_K + E * (R - 1) + R - 1) // R * R
    xg = _sc_scatter_rows(h2p, dest_p, n_rows + (X_SLOTS - 1) * R)
    y = _experts(ustart, pstart, xg, w_expert_up_gate, w_expert_down, n_rows, R)
    yg = _sc_gather_rows(y, dest_p)
    out = _combine(yg, w_t, base, mod, S)
    return out.reshape(B, S, D)


def kernel(x, c, positions, w_ada, b_ada, norm1_g, w_in, w_gate, b_gate, q_norm_g, k_norm_g, lambda_q1, lambda_k1, lambda_q2, lambda_k2, subln_g, rwkv_mu, w_decay0, w_decay2, a0, a2, g2, k_k, k_a, r_k, ln_x_w, ln_x_b, w_branch_a, w_branch_b, w_out, norm2_g, w_router, router_bias, w_expert_up_gate, w_expert_down, w_shared_up_gate, w_shared_down):
    for l in range(w_ada.shape[0]):
        x = _layer(x, c, positions, l, w_ada[l], b_ada[l], norm1_g[l], w_in[l], w_gate[l], b_gate[l],
                   q_norm_g[l], k_norm_g[l], lambda_q1[l], lambda_k1[l], lambda_q2[l], lambda_k2[l],
                   subln_g[l], rwkv_mu[l], w_decay0[l], w_decay2[l], a0[l], a2[l], g2[l], k_k[l],
                   k_a[l], r_k[l], ln_x_w[l], ln_x_b[l], w_branch_a[l], w_branch_b[l], w_out[l],
                   norm2_g[l], w_router[l], router_bias[l], w_expert_up_gate[l], w_expert_down[l],
                   w_shared_up_gate[l], w_shared_down[l])
    return x
```

```python
import functools
import math

import jax
import jax.numpy as jnp
from jax import lax
from jax.experimental import pallas as pl
from jax.experimental.pallas import tpu as pltpu
from jax.experimental.pallas import tpu_sc as plsc

F32 = jnp.float32
BF16 = jnp.bfloat16
I32 = jnp.int32
U32 = jnp.uint32
HI = lax.Precision.HIGHEST

CHUNK = 64
ROPE_THETA = 10000.0
NORM_EPS = 1e-6
SUBLN_EPS = 1e-5
DA_HEAD_DIM = 64
RWKV_HEAD = 64
GN_EPS = 64e-5
TOP_K = 8
N_GROUPS = 8
TOPK_GROUPS = 4
ROUTED_SCALE = 2.5
EXPERT_TILE = 256
X_SLOTS = 4
RWKV_CHUNK = 64
LANES = 128
SC_WINDOW = 128
NEG = -1e30
MAX_PLAIN_SCORE = 40.0
VMEM_LIMIT = 56 * 1024 * 1024


def _cp(*sem):
    return pltpu.CompilerParams(dimension_semantics=sem, vmem_limit_bytes=VMEM_LIMIT)


def _bdot(a, b):
    return jnp.dot(a.astype(BF16), b.astype(BF16), preferred_element_type=F32)


def _fdot(a, b):
    return jnp.dot(a, b, precision=HI, preferred_element_type=F32)


def _nt(a, b, precision=None):
    return lax.dot_general(a, b, (((1,), (1,)), ((), ())), precision=precision,
                           preferred_element_type=F32)


def _tn(a, b, precision=None):
    return lax.dot_general(a, b, (((0,), (0,)), ((), ())), precision=precision,
                           preferred_element_type=F32)


def _pack_halves(x):
    c = x.shape[1] // 2
    lo = lax.bitcast_convert_type(x[:, :c], U32)
    hi = lax.bitcast_convert_type(x[:, c:], U32)
    return (hi & jnp.uint32(0xFFFF0000)) | (lo >> 16)


def _unpack_halves(w):
    lo = lax.bitcast_convert_type(w << 16, F32)
    hi = lax.bitcast_convert_type(w & jnp.uint32(0xFFFF0000), F32)
    return lo, hi


def _sigmoid(x):
    return 1.0 / (1.0 + jnp.exp(-x))


def _silu(x):
    return x * _sigmoid(x)


def _ada_kernel(c_ref, w_ref, b_ref, o_ref):
    o_ref[...] = _fdot(_silu(c_ref[...]), w_ref[...]) + b_ref[...]


def _adaln(c, w_ada, b_ada):
    B, D = c.shape
    rows = -(-B // 8) * 8
    cpad = jnp.zeros((rows, D), F32).at[:B].set(c)
    n_out = w_ada.shape[1]
    out = pl.pallas_call(
        _ada_kernel,
        grid=(n_out // D,),
        in_specs=[pl.BlockSpec((rows, D), lambda j: (0, 0)),
                  pl.BlockSpec((D, D), lambda j: (0, j)),
                  pl.BlockSpec((1, D), lambda j: (0, j))],
        out_specs=pl.BlockSpec((rows, D), lambda j: (0, j)),
        out_shape=jax.ShapeDtypeStruct((rows, n_out), F32),
        compiler_params=_cp("arbitrary"),
        name="adaln",
    )(cpad, w_ada, b_ada.reshape(1, n_out))
    return out[:B].reshape(B, n_out // D, D)


def _mixer_in_kernel(x_ref, xprev_ref, mod_ref, g_ref, w_ref, bg_ref, pos_ref, invf_ref, qg_ref, kg_ref,
                     mu_ref, w0_ref, w2_ref, a0_ref, a2_ref, g2_ref,
                     qn_ref, kn_ref, v_ref, r_ref, k_ref, vr_ref, a_ref, ld_ref, gr_ref, gate_ref,
                     *, da_width, rw_cols, rw_width, q_scale, tiles_per_seq):
    tm = x_ref.shape[0]

    def modulated(x):
        y = x * lax.rsqrt(jnp.mean(x * x, axis=-1, keepdims=True) + NORM_EPS) * g_ref[...]
        return (y * (1.0 + mod_ref[1:2, :]) + mod_ref[0:1, :]).astype(BF16)

    def proj(hb, c0, width, step=512):
        parts = [jnp.dot(hb, w_ref[:, c0 + o:c0 + min(o + step, width)], preferred_element_type=F32)
                 for o in range(0, width, step)]
        return parts[0] if len(parts) == 1 else jnp.concatenate(parts, axis=1)

    h = modulated(x_ref[...])

    lane = lax.broadcasted_iota(I32, (tm, LANES), 1)
    first = lane < DA_HEAD_DIM
    lo_half = (lane & (DA_HEAD_DIM - 1)) < DA_HEAD_DIM // 2
    ang = pos_ref[...].astype(F32) * invf_ref[...]
    cos = jnp.cos(ang)
    sin = jnp.sin(ang)
    sin = jnp.where(lo_half, -sin, sin)
    for c0, dst, gn_ref, mult in ((0, qn_ref, qg_ref, q_scale), (da_width, kn_ref, kg_ref, 1.0)):
        raw = proj(h, c0, da_width)
        for blk in range(da_width // LANES):
            x = raw[:, blk * LANES:(blk + 1) * LANES]
            xx = x * x
            s_first = jnp.sum(jnp.where(first, xx, 0.0), axis=-1, keepdims=True)
            s_second = jnp.sum(jnp.where(first, 0.0, xx), axis=-1, keepdims=True)
            ms = jnp.where(first, s_first, s_second) * (1.0 / DA_HEAD_DIM)
            xn = x * lax.rsqrt(ms + NORM_EPS) * gn_ref[...]
            rot = jnp.where(lo_half, pltpu.roll(xn, LANES - DA_HEAD_DIM // 2, axis=1),
                            pltpu.roll(xn, DA_HEAD_DIM // 2, axis=1))
            dst[:, blk * LANES:(blk + 1) * LANES] = ((xn * cos + rot * sin) * mult).astype(dst.dtype)
    v_ref[...] = proj(h, 2 * da_width, da_width).astype(v_ref.dtype)

    c_rw = 3 * da_width
    p = proj(h, c_rw, rw_cols)
    p_before = proj(modulated(xprev_ref[...]), c_rw, rw_cols)
    seq_start = (pl.program_id(0) % tiles_per_seq) == 0
    last_prev = jnp.where(seq_start, 0.0, p_before[7:8, :])
    rowi = lax.broadcasted_iota(I32, p.shape, 0)
    prev = jnp.where(rowi == 0, last_prev, pltpu.roll(p, 1, axis=0))
    xs = p + (prev - p) * mu_ref[...]
    width = rw_width
    r_ref[...] = xs[:, 0:width]
    k_ref[...] = xs[:, width:2 * width]
    vr_ref[...] = xs[:, 2 * width:3 * width]
    xwa = xs[:, 3 * width:3 * width + LANES]
    xg = xs[:, 3 * width + LANES:]
    z = w0_ref[...] + _bdot(jnp.tanh(xwa), w2_ref[...])
    w = -(jnp.maximum(-z, 0.0) + jnp.log(1.0 + jnp.exp(-jnp.abs(z)))) - 0.5
    ld_ref[...] = -jnp.exp(w)
    a_ref[...] = _sigmoid(a0_ref[...] + _bdot(xwa, a2_ref[...]))
    gr_ref[...] = _bdot(_sigmoid(xg), g2_ref[...])

    gate_ref[...] = _sigmoid(proj(h, c_rw + rw_cols, gate_ref.shape[1]) + bg_ref[...]).astype(gate_ref.dtype)


def _mixer_in(x2, pos2, mod, norm1_g, w_cat, b_gate, q_norm_g, k_norm_g, mu, w_decay0, w_decay2, a0, a2, g2,
              S, da_width, rw_width, tm=256):
    N, D = x2.shape
    n_gate = b_gate.shape[0]
    rw_cols = mu.shape[0]
    tpb = S // tm
    d = DA_HEAD_DIM
    inv_freq = 1.0 / (ROPE_THETA ** (jnp.arange(0, d, 2, dtype=F32) / d))
    invf = jnp.tile(inv_freq, LANES // (d // 2)).reshape(1, LANES)
    dl, al = w_decay2.shape[0], a2.shape[0]
    assert dl + al == LANES and g2.shape[0] == LANES
    w2p = jnp.zeros((LANES, rw_width), F32).at[:dl].set(w_decay2)
    a2p = jnp.zeros((LANES, rw_width), F32).at[dl:].set(a2)
    kern = functools.partial(_mixer_in_kernel, da_width=da_width, rw_cols=rw_cols, rw_width=rw_width,
                             q_scale=d ** -0.5 * math.log2(math.e),
                             tiles_per_seq=tpb)
    row = lambda w: pl.BlockSpec((tm, w), lambda i: (i, 0))
    vec = lambda n: pl.BlockSpec((1, n), lambda i: (0, 0))
    mat = pl.BlockSpec((LANES, rw_width), lambda i: (0, 0))
    f32 = lambda w: jax.ShapeDtypeStruct((N, w), F32)
    bf16 = lambda w: jax.ShapeDtypeStruct((N, w), BF16)
    return pl.pallas_call(
        kern,
        grid=(N // tm,),
        in_specs=[row(D),
                  pl.BlockSpec((8, D), lambda i: (jnp.maximum(i * (tm // 8) - 1, 0), 0)),
                  pl.BlockSpec((None, 6, D), lambda i: (i // tpb, 0, 0)),
                  vec(D),
                  pl.BlockSpec(w_cat.shape, lambda i: (0, 0)),
                  vec(n_gate),
                  pl.BlockSpec((tm, 1), lambda i: (i, 0)),
                  vec(LANES), vec(LANES), vec(LANES),
                  vec(rw_cols), vec(rw_width), mat, vec(rw_width), mat, mat],
        out_specs=[row(da_width)] * 3 + [row(rw_width)] * 6 + [row(n_gate)],
        out_shape=[bf16(da_width)] * 3 + [f32(rw_width)] * 6 + [bf16(n_gate)],
        compiler_params=_cp("parallel"),
        name="mixer_in",
    )(x2, x2, mod, norm1_g.reshape(1, D), w_cat, b_gate.reshape(1, n_gate), pos2, invf,
      jnp.tile(q_norm_g, 2).reshape(1, LANES), jnp.tile(k_norm_g, 2).reshape(1, LANES),
      mu.reshape(1, rw_cols), w_decay0.reshape(1, rw_width), w2p, a0.reshape(1, rw_width), a2p, g2)


def _attn_kernel(flag_ref, q_ref, k_ref, v_ref, lam_ref, sg_ref, o_ref, qz_ref, m_ref, l_ref, lp_ref, acc_ref,
                 *, tq, lambda_init):
    i = pl.program_id(2)
    lane = lax.broadcasted_iota(I32, (tq, LANES), 1)
    q = q_ref[...]
    zero = jnp.zeros_like(q)
    qz_ref[0:tq, :] = jnp.where(lane < DA_HEAD_DIM, q, zero)
    qz_ref[tq:, :] = jnp.where(lane >= DA_HEAD_DIM, q, zero)
    acc_ref[...] = jnp.zeros_like(acc_ref)
    bounded = flag_ref[0] == 1

    def scores(j, masked):
        off = pl.multiple_of(j * tq, tq)
        s = _nt(qz_ref[...], k_ref[pl.ds(off, tq), :])
        if masked:
            row = lax.broadcasted_iota(I32, s.shape, 0)
            col = lax.broadcasted_iota(I32, s.shape, 1)
            s = jnp.where((col // CHUNK) <= ((row & (tq - 1)) // CHUNK), s, NEG)
        return s, off

    def plain_step(j, masked):
        s, off = scores(j, masked)
        pr = jnp.exp2(s)
        part = pr[:, 0:LANES]
        for cblk in range(1, tq // LANES):
            part = part + pr[:, cblk * LANES:(cblk + 1) * LANES]
        lp_ref[...] += part
        acc_ref[...] += jnp.dot(pr.astype(BF16), v_ref[pl.ds(off, tq), :], preferred_element_type=F32)

    def online_step(j, masked):
        s, off = scores(j, masked)
        m_old = m_ref[...]
        m_new = jnp.maximum(m_old, jnp.max(s, axis=-1, keepdims=True))
        alpha = jnp.exp2(m_old - m_new)
        pr = jnp.exp2(s - m_new)
        l_ref[...] = alpha * l_ref[...] + jnp.sum(pr, axis=-1, keepdims=True)
        acc_ref[...] = alpha * acc_ref[...] + jnp.dot(pr.astype(BF16), v_ref[pl.ds(off, tq), :],
                                                      preferred_element_type=F32)
        m_ref[...] = m_new

    def run(step):
        def body(j, carry):
            step(j, False)
            return carry
        lax.fori_loop(0, i, body, 0)
        step(i, True)

    @pl.when(bounded)
    def _():
        lp_ref[...] = jnp.zeros_like(lp_ref)
        run(plain_step)
        l_ref[...] = jnp.sum(lp_ref[...], axis=-1, keepdims=True)

    @pl.when(jnp.logical_not(bounded))
    def _():
        m_ref[...] = jnp.full_like(m_ref, NEG)
        l_ref[...] = jnp.zeros_like(l_ref)
        run(online_step)

    lv = lam_ref[...]
    lam = (jnp.exp(jnp.sum(lv[0:1] * lv[1:2], keepdims=True))
           - jnp.exp(jnp.sum(lv[2:3] * lv[3:4], keepdims=True)) + lambda_init)
    o1 = acc_ref[0:tq, :] / l_ref[0:tq, :]
    o2 = acc_ref[tq:, :] / l_ref[tq:, :]
    o = o1 - lam * o2
    o = o * lax.rsqrt(jnp.mean(o * o, axis=-1, keepdims=True) + SUBLN_EPS) * sg_ref[...]
    o_ref[...] = (o * (1.0 - lambda_init)).astype(o_ref.dtype)


def _diff_attention(qn, kn, v, score_bound, lam_vecs, subln_g, B, S, lambda_init, tq=512):
    W = qn.shape[1]
    H = W // LANES
    q3 = qn.reshape(B, S, W)
    k3 = kn.reshape(B, S, W)
    v3 = v.reshape(B, S, W)
    flag = (score_bound <= MAX_PLAIN_SCORE).astype(I32).reshape(1)
    qblk = pl.BlockSpec((None, tq, LANES), lambda b, h, i, f: (b, i, h))
    kvblk = pl.BlockSpec((None, S, LANES), lambda b, h, i, f: (b, 0, h))
    grid_spec = pltpu.PrefetchScalarGridSpec(
        num_scalar_prefetch=1,
        grid=(B, H, S // tq),
        in_specs=[qblk, kvblk, kvblk,
                  pl.BlockSpec((4, DA_HEAD_DIM), lambda b, h, i, f: (0, 0)),
                  pl.BlockSpec((1, LANES), lambda b, h, i, f: (0, 0))],
        out_specs=qblk,
        scratch_shapes=[pltpu.VMEM((2 * tq, LANES), BF16),
                        pltpu.VMEM((2 * tq, 1), F32),
                        pltpu.VMEM((2 * tq, 1), F32),
                        pltpu.VMEM((2 * tq, LANES), F32),
                        pltpu.VMEM((2 * tq, LANES), F32)],
    )
    out = pl.pallas_call(
        functools.partial(_attn_kernel, tq=tq, lambda_init=lambda_init),
        grid_spec=grid_spec,
        out_shape=jax.ShapeDtypeStruct((B, S, W), BF16),
        compiler_params=_cp("parallel", "parallel", "arbitrary"),
        name="diff_attn",
    )(flag, q3, k3, v3, lam_vecs, subln_g.reshape(1, LANES))
    return out.reshape(B * S, W)


def _stackmask(m):
    lane = lax.broadcasted_iota(I32, m.shape, 1)
    z = jnp.zeros_like(m)
    return jnp.concatenate([jnp.where(lane < RWKV_HEAD, m, z), jnp.where(lane >= RWKV_HEAD, m, z)], axis=0)


def _pair_sum(x, first):
    s1 = jnp.sum(jnp.where(first, x, 0.0), axis=-1, keepdims=True)
    s2 = jnp.sum(jnp.where(first, 0.0, x), axis=-1, keepdims=True)
    return jnp.where(first, s1, s2)


def _rwkv_scan_kernel(r_ref, k_ref, v_ref, a_ref, ld_ref, g_ref, kk_ref, ka_ref, rk_ref, lnw_ref, lnb_ref,
                      o_ref, s_ref, *, L):
    tm, W = r_ref.shape
    n_chunks = tm // L
    n_pairs = W // LANES
    hd = RWKV_HEAD
    bf = lambda t: t.astype(BF16)

    @pl.when(pl.program_id(1) == 0)
    def _():
        s_ref[...] = jnp.zeros_like(s_ref)

    row = lax.broadcasted_iota(I32, (tm, tm), 0)
    col = lax.broadcasted_iota(I32, (tm, tm), 1)
    tri = jnp.where(jnp.logical_and(col <= row, (col // L) == (row // L)), 1.0, 0.0).astype(BF16)
    ld = ld_ref[...]
    ld_hi = bf(ld)
    rem = ld - ld_hi.astype(F32)
    ld_mid = bf(rem)
    ld_lo = bf(rem - ld_mid.astype(F32))
    c = (jnp.dot(tri, ld_hi, preferred_element_type=F32) + jnp.dot(tri, ld_mid, preferred_element_type=F32)
         + jnp.dot(tri, ld_lo, preferred_element_type=F32))
    ec = jnp.exp(c)
    eci = jnp.exp(-c)
    ecm = jnp.exp(c - ld)
    r = r_ref[...]
    k = k_ref[...]
    v = v_ref[...]
    a = a_ref[...]
    kkr = k * kk_ref[...]
    kmod = k * (1.0 + (a - 1.0) * ka_ref[...])
    brk = r * kmod * rk_ref[...]

    lane = lax.broadcasted_iota(I32, (L, LANES), 1)
    rowl = lax.broadcasted_iota(I32, (L, LANES), 0)
    first = lane < hd
    lane_h = lane & (hd - 1)
    strict = lane_h < rowl
    incl = lane_h <= rowl
    eye = jnp.where(lane_h == rowl, 1.0, 0.0)

    chains = [(ch, p) for ch in range(n_chunks) for p in range(n_pairs)]
    rsl = lambda ch: slice(ch * L, (ch + 1) * L)
    csl = lambda p: slice(p * LANES, (p + 1) * LANES)
    fdot = lambda x, y: jnp.dot(x, y, preferred_element_type=F32)
    at, bt, kt, rt, vh, g_l = {}, {}, {}, {}, {}, {}
    for c_ in chains:
        ch, p = c_
        rs, cs = rsl(ch), csl(p)
        kkh = kkr[rs, cs]
        kkh = kkh / jnp.maximum(jnp.sqrt(_pair_sum(kkh * kkh, first)), 1e-12)
        vh[c_] = v[rs, cs]
        g_l[c_] = ec[ch * L + L - 1:ch * L + L, cs]
        at[c_] = -kkh * ecm[rs, cs]
        bt[c_] = kkh * a[rs, cs] * eci[rs, cs]
        kt[c_] = kmod[rs, cs] * eci[rs, cs]
        rt[c_] = r[rs, cs] * ec[rs, cs]
    gm = {c_: _nt(bf(jnp.concatenate([at[c_], rt[c_]], axis=0)),
                  jnp.concatenate([_stackmask(bf(bt[c_])), _stackmask(bf(kt[c_]))], axis=0)) for c_ in chains}
    a_ab = {c_: jnp.where(strict, gm[c_][:L, :LANES], 0.0) for c_ in chains}
    vsm = {c_: _stackmask(bf(vh[c_])) for c_ in chains}
    cmat = {c_: fdot(bf(jnp.where(strict, gm[c_][:L, LANES:], 0.0)), vsm[c_]) for c_ in chains}
    t_inv = {c_: eye + a_ab[c_] for c_ in chains}
    pw = {c_: bf(a_ab[c_]) for c_ in chains}
    for _ in range(int(math.log2(L)) - 1):
        pw = {c_: bf(fdot(pw[c_], _stackmask(pw[c_]))) for c_ in chains}
        t_inv = {c_: t_inv[c_] + fdot(pw[c_], _stackmask(bf(t_inv[c_]))) for c_ in chains}
    zz = {c_: fdot(bf(t_inv[c_]), jnp.concatenate([_stackmask(bf(at[c_])), _stackmask(bf(cmat[c_]))], axis=1))
          for c_ in chains}
    qy = {c_: fdot(bf(jnp.where(incl, gm[c_][L:, :LANES], 0.0)),
                   jnp.concatenate([_stackmask(bf(zz[c_][:, :LANES])), _stackmask(bf(zz[c_][:, LANES:]))], axis=1))
          for c_ in chains}
    y0 = {c_: qy[c_][:, LANES:] + fdot(bf(jnp.where(incl, gm[c_][L:, LANES:], 0.0)), vsm[c_]) for c_ in chains}
    qa = {c_: bf(jnp.concatenate([rt[c_] + qy[c_][:, :LANES], zz[c_][:, :LANES]], axis=0)) for c_ in chains}
    bkg = {c_: bf(jnp.concatenate([bt[c_] * g_l[c_], kt[c_] * g_l[c_]], axis=0)) for c_ in chains}

    lane_s = lax.broadcasted_iota(I32, (hd, LANES), 1)
    sp = [s_ref[p] for p in range(n_pairs)]
    for ch in range(n_chunks):
        rs = rsl(ch)
        yw = [_nt(qa[ch, p], _stackmask(bf(sp[p]))) for p in range(n_pairs)]
        upd = [_tn(bf(jnp.concatenate([yw[p][L:] + zz[ch, p][:, LANES:], vh[ch, p]], axis=0)), bkg[ch, p])
               for p in range(n_pairs)]
        for p in range(n_pairs):
            cs = csl(p)
            sp[p] = sp[p] * g_l[ch, p] + jnp.where(lane_s < hd, upd[p][:hd], upd[p][hd:])
            y = yw[p][:L] + y0[ch, p]
            mean = _pair_sum(y, first) * (1.0 / hd)
            yc = y - mean
            var = _pair_sum(yc * yc, first) * (1.0 / hd)
            yn = yc * lax.rsqrt(var + GN_EPS) * lnw_ref[:, cs] + lnb_ref[:, cs]
            bonus = _pair_sum(brk[rs, cs], first) * vh[ch, p]
            o_ref[rs, cs] = ((yn + bonus) * g_ref[rs, cs]).astype(o_ref.dtype)
    for p in range(n_pairs):
        s_ref[p] = sp[p]


def _rwkv_scan(r, k, v, a, ld, g, k_k, k_a, r_k, ln_w, ln_b, L=RWKV_CHUNK, tm=256):
    B, S, W = r.shape
    seq = pl.BlockSpec((None, tm, W), lambda b, c: (b, c, 0))
    vec = pl.BlockSpec((1, W), lambda b, c: (0, 0))
    return pl.pallas_call(
        functools.partial(_rwkv_scan_kernel, L=L),
        grid=(B, S // tm),
        in_specs=[seq] * 6 + [vec] * 5,
        out_specs=seq,
        out_shape=jax.ShapeDtypeStruct((B, S, W), BF16),
        scratch_shapes=[pltpu.VMEM((W // LANES, RWKV_HEAD, LANES), F32)],
        compiler_params=_cp("parallel", "arbitrary"),
        name="rwkv_scan",
    )(r, k, v, a, ld, g, k_k.reshape(1, W), k_a.reshape(1, W), r_k.reshape(1, W),
      ln_w.reshape(1, W), ln_b.reshape(1, W))


def _post_kernel(attn_ref, rw_ref, gate_ref, x_ref, mod_ref, wa_ref, wb_ref, wo_ref, g2_ref, wrh_ref, wrm_ref,
                 sug_ref, sd_ref, base_ref, h2p_ref, lg_ref):
    D = x_ref.shape[1]
    ya = jnp.dot(attn_ref[...], wa_ref[...], preferred_element_type=F32)
    yb = jnp.dot(rw_ref[...], wb_ref[...], preferred_element_type=F32)
    m = gate_ref[:, 0:D] * ya + gate_ref[:, D:] * yb
    x1 = x_ref[...] + mod_ref[2:3, :] * jnp.dot(m.astype(BF16), wo_ref[...], preferred_element_type=F32)
    y = x1 * lax.rsqrt(jnp.mean(x1 * x1, axis=-1, keepdims=True) + NORM_EPS) * g2_ref[...]
    h2 = y * (1.0 + mod_ref[4:5, :]) + mod_ref[3:4, :]
    hb = h2.astype(BF16)
    hm = (h2 - hb.astype(F32)).astype(BF16)
    lg_ref[...] = _nt(wrh_ref[...], hb) + _nt(wrh_ref[...], hm) + _nt(wrm_ref[...], hb)
    packed = _pack_halves(hb.astype(F32))
    half = packed.shape[1] // 2
    h2p_ref[0] = packed[:, :half]
    h2p_ref[1] = packed[:, half:]
    F = sd_ref.shape[0]
    gu = jnp.dot(hb, sug_ref[...], preferred_element_type=F32)
    shared = jnp.dot((_silu(gu[:, :F]) * gu[:, F:]).astype(BF16), sd_ref[...], preferred_element_type=F32)
    base_ref[...] = x1 + mod_ref[5:6, :] * shared


def _post(attn, rw, gate, x2, mod, wa, wb, wo, norm2_g, w_router_t, sug, sd, S, tm=256):
    N, D = x2.shape
    E = w_router_t.shape[0]
    wr_hi = w_router_t.astype(BF16)
    wr_mid = (w_router_t - wr_hi.astype(F32)).astype(BF16)
    tpb = S // tm
    row = lambda w: pl.BlockSpec((tm, w), lambda i: (i, 0))
    full = lambda a: pl.BlockSpec(a.shape, lambda i: (0, 0))
    return pl.pallas_call(
        _post_kernel,
        grid=(N // tm,),
        in_specs=[row(attn.shape[1]), row(rw.shape[1]), row(gate.shape[1]), row(D),
                  pl.BlockSpec((None, 6, D), lambda i: (i // tpb, 0, 0)),
                  full(wa), full(wb), full(wo), pl.BlockSpec((1, D), lambda i: (0, 0)), full(wr_hi), full(wr_mid),
                  full(sug), full(sd)],
        out_specs=[row(D), pl.BlockSpec((2, tm, D // 4), lambda i: (0, i, 0)), pl.BlockSpec((E, tm), lambda i: (0, i))],
        out_shape=[jax.ShapeDtypeStruct((N, D), F32), jax.ShapeDtypeStruct((2, N, D // 4), U32),
                   jax.ShapeDtypeStruct((E, N), F32)],
        compiler_params=_cp("parallel"),
        name="post_mixer",
    )(attn, rw, gate, x2, mod, wa, wb, wo, norm2_g.reshape(1, D), wr_hi, wr_mid, sug, sd)


def _first_argmax(x, idx, sentinel):
    m = jnp.max(x, axis=0, keepdims=True)
    return m, jnp.min(jnp.where(x == m, idx, sentinel), axis=0, keepdims=True)


def _route_kernel(lg_ref, bias_ref, e_ref, w_ref):
    E, T = lg_ref.shape
    gsz = E // N_GROUPS
    scores = _sigmoid(lg_ref[...])
    biased = scores + bias_ref[...]
    ig = lax.broadcasted_iota(I32, (gsz, T), 0)
    gs = []
    for g in range(N_GROUPS):
        blk = biased[g * gsz:(g + 1) * gsz, :]
        m1, i1 = _first_argmax(blk, ig, gsz)
        m2 = jnp.max(jnp.where(ig == i1, -jnp.inf, blk), axis=0, keepdims=True)
        gs.append(m1 + m2)
    gsc = jnp.concatenate(gs, axis=0)
    i8 = lax.broadcasted_iota(I32, (N_GROUPS, T), 0)
    chosen = jnp.zeros((N_GROUPS, T), F32)
    for _ in range(TOPK_GROUPS):
        _, gi = _first_argmax(gsc, i8, N_GROUPS)
        hit = i8 == gi
        chosen = jnp.where(hit, 1.0, chosen)
        gsc = jnp.where(hit, -jnp.inf, gsc)
    masked = jnp.concatenate(
        [jnp.where(chosen[g:g + 1, :] > 0.0, biased[g * gsz:(g + 1) * gsz, :], -jnp.inf)
         for g in range(N_GROUPS)], axis=0)
    ie = lax.broadcasted_iota(I32, (E, T), 0)
    idxs, wts = [], []
    for _ in range(TOP_K):
        _, ei = _first_argmax(masked, ie, E)
        hit = ie == ei
        idxs.append(ei)
        wts.append(jnp.sum(jnp.where(hit, scores, 0.0), axis=0, keepdims=True))
        masked = jnp.where(hit, -jnp.inf, masked)
    wt = jnp.concatenate(wts, axis=0)
    e_ref[...] = jnp.concatenate(idxs, axis=0)
    w_ref[...] = wt / jnp.sum(wt, axis=0, keepdims=True) * ROUTED_SCALE


def _route(logits_t, router_bias, T=512):
    E, N = logits_t.shape
    blk = pl.BlockSpec((TOP_K, T), lambda i: (0, i))
    return pl.pallas_call(
        _route_kernel,
        grid=(N // T,),
        in_specs=[pl.BlockSpec((E, T), lambda i: (0, i)), pl.BlockSpec((E, 1), lambda i: (0, 0))],
        out_specs=[blk, blk],
        out_shape=[jax.ShapeDtypeStruct((TOP_K, N), I32), jax.ShapeDtypeStruct((TOP_K, N), F32)],
        compiler_params=_cp("parallel"),
        name="route",
    )(logits_t, router_bias.reshape(E, 1))


def _rank_kernel(e_ref, rank_ref, cnt_ref, carry_ref, *, n_experts):
    T = e_ref.shape[1]

    @pl.when(pl.program_id(0) == 0)
    def _():
        carry_ref[...] = jnp.zeros_like(carry_ref)

    ie = lax.broadcasted_iota(I32, (n_experts, T), 0)
    e = e_ref[...]
    hits = [ie == e[kk:kk + 1, :] for kk in range(TOP_K)]
    onehot = jnp.zeros((n_experts, T), F32)
    for hsel in hits:
        onehot = onehot + hsel.astype(F32)
    tr = lax.broadcasted_iota(I32, (T, T), 0)
    tc = lax.broadcasted_iota(I32, (T, T), 1)
    before = (tr < tc).astype(BF16)
    base = _bdot(onehot, before) + carry_ref[:, 0:1]
    rank_ref[...] = jnp.concatenate(
        [jnp.sum(jnp.where(hsel, base, 0.0), axis=0, keepdims=True) for hsel in hits], axis=0).astype(I32)
    carry_ref[...] = carry_ref[...] + jnp.sum(onehot, axis=1, keepdims=True)
    cnt_ref[...] = carry_ref[...]


def _ranks(eidx_t, n_experts, T=512):
    N = eidx_t.shape[1]
    blk = pl.BlockSpec((TOP_K, T), lambda i: (0, i))
    cnt = pl.BlockSpec((n_experts, LANES), lambda i: (0, 0))
    return pl.pallas_call(
        functools.partial(_rank_kernel, n_experts=n_experts),
        grid=(N // T,),
        in_specs=[blk],
        out_specs=[blk, cnt],
        out_shape=[jax.ShapeDtypeStruct((TOP_K, N), I32), jax.ShapeDtypeStruct((n_experts, LANES), F32)],
        scratch_shapes=[pltpu.VMEM((n_experts, LANES), F32)],
        compiler_params=_cp("arbitrary"),
        name="ranks",
    )(eidx_t)


def _dest_kernel(e_ref, rank_ref, start_ref, d_ref):
    E = start_ref.shape[0]
    T = e_ref.shape[1]
    ie = lax.broadcasted_iota(I32, (E, T), 0)
    e = e_ref[...]
    start = start_ref[:, 0:1]
    rows = [jnp.sum(jnp.where(ie == e[kk:kk + 1, :], start, 0.0), axis=0, keepdims=True) for kk in range(TOP_K)]
    d_ref[...] = jnp.concatenate(rows, axis=0).astype(I32) + rank_ref[...]


def _dests(eidx_t, rank_t, pstart, T=512):
    N = eidx_t.shape[1]
    E = pstart.shape[0]
    blk = pl.BlockSpec((TOP_K, T), lambda i: (0, i))
    return pl.pallas_call(
        _dest_kernel,
        grid=(N // T,),
        in_specs=[blk, blk, pl.BlockSpec((E, LANES), lambda i: (0, 0))],
        out_specs=blk,
        out_shape=jax.ShapeDtypeStruct((TOP_K, N), I32),
        compiler_params=_cp("parallel"),
        name="dests",
    )(eidx_t, rank_t, jnp.broadcast_to(pstart.astype(F32)[:, None], (E, LANES)))


def _expert_kernel(us_ref, ps_ref, x_hbm, wug_ref, wd_ref, y_hbm,
                   wug_bf, wd_bf, xbuf, ybuf, cnt_ref, xsem, ysem, *, n_rows):
    e = pl.program_id(0)
    _, _, R, Ch = xbuf.shape
    C = 2 * Ch
    F = wd_ref.shape[0]

    def x_copy(slot, half, row):
        return pltpu.make_async_copy(x_hbm.at[half, pl.ds(row, R)], xbuf.at[slot, half], xsem.at[slot])

    def y_copy(slot, half, row):
        return pltpu.make_async_copy(ybuf.at[slot, half], y_hbm.at[half, pl.ds(row, R)], ysem.at[slot])

    def start(copy, slot, row):
        for half in range(2):
            copy(slot, half, row).start(priority=1)

    def wait(copy, slot):
        for half in range(2):
            copy(slot, half, 0).wait()

    @pl.when(e == 0)
    def _():
        cnt_ref[0] = 0
        for t in range(X_SLOTS - 1):
            start(x_copy, t, t * R)
        ybuf[...] = jnp.zeros_like(ybuf)
        for slot in range(2):
            start(y_copy, slot, n_rows + slot * R)

    wug_bf[...] = wug_ref[...].astype(BF16)
    wd_bf[...] = wd_ref[...].astype(BF16)
    n_valid = us_ref[e + 1] - us_ref[e]
    pbase = ps_ref[e]
    n_tiles = lax.shift_right_logical(n_valid + (R - 1), int(math.log2(R)))
    rowid = lax.broadcasted_iota(I32, (R, C), 0)

    def tile(i, carry):
        n_done = cnt_ref[0]
        slot = n_done & 1
        xslot = lax.rem(n_done, X_SLOTS)
        ahead = n_done + (X_SLOTS - 1)
        start(x_copy, lax.rem(ahead, X_SLOTS), pl.multiple_of(ahead * R, R))
        wait(x_copy, xslot)
        x = jnp.concatenate([xbuf[xslot, 0], xbuf[xslot, 1]], axis=1)
        x = jnp.where(rowid < n_valid - i * R, x, jnp.uint32(0))
        lo, hi = _unpack_halves(x)
        gu = (jnp.dot(lo.astype(BF16), wug_bf[0:C, :], preferred_element_type=F32)
              + jnp.dot(hi.astype(BF16), wug_bf[C:, :], preferred_element_type=F32))
        hid = (_silu(gu[:, :F]) * gu[:, F:]).astype(BF16)
        y = jnp.dot(hid, wd_bf[...], preferred_element_type=F32)
        wait(y_copy, slot)
        packed = _pack_halves(y.astype(BF16).astype(F32))
        ybuf[slot, 0] = packed[:, :Ch]
        ybuf[slot, 1] = packed[:, Ch:]
        start(y_copy, slot, pl.multiple_of(pbase + i * R, R))
        cnt_ref[0] = n_done + 1
        return carry

    lax.fori_loop(0, n_tiles, tile, 0)

    @pl.when(e == pl.num_programs(0) - 1)
    def _():
        for t in range(X_SLOTS - 1):
            wait(x_copy, lax.rem(cnt_ref[0] + t, X_SLOTS))
        for slot in range(2):
            wait(y_copy, slot)
        ybuf[0] = jnp.zeros((2, R, Ch), U32)
        first = lax.shift_right_logical(pbase + n_tiles * R, int(math.log2(R)))
        n_left = n_rows // R - first

        def fill(t, carry):
            start(y_copy, 0, pl.multiple_of((first + t) * R, R))
            return carry

        def drain(t, carry):
            wait(y_copy, 0)
            return carry

        lax.fori_loop(0, n_left, fill, 0)
        lax.fori_loop(0, n_left, drain, 0)


def _experts(ustart, pstart, xg, w_ug, w_d, n_rows, R):
    _, _, Ch = xg.shape
    E, D, F2 = w_ug.shape
    F = w_d.shape[1]
    grid_spec = pltpu.PrefetchScalarGridSpec(
        num_scalar_prefetch=2,
        grid=(E,),
        in_specs=[pl.BlockSpec(memory_space=pl.ANY),
                  pl.BlockSpec((None, D, F2), lambda e, us, ps: (e, 0, 0)),
                  pl.BlockSpec((None, F, D), lambda e, us, ps: (e, 0, 0))],
        out_specs=pl.BlockSpec(memory_space=pl.ANY),
        scratch_shapes=[pltpu.VMEM((D, F2), BF16), pltpu.VMEM((F, D), BF16),
                        pltpu.VMEM((X_SLOTS, 2, R, Ch), U32), pltpu.VMEM((2, 2, R, Ch), U32), pltpu.SMEM((1,), I32),
                        pltpu.SemaphoreType.DMA((X_SLOTS,)), pltpu.SemaphoreType.DMA((2,))],
    )
    return pl.pallas_call(
        functools.partial(_expert_kernel, n_rows=n_rows),
        grid_spec=grid_spec,
        out_shape=jax.ShapeDtypeStruct((2, n_rows + 2 * R, Ch), U32),
        compiler_params=_cp("arbitrary"),
        name="experts",
    )(ustart, pstart, xg, w_ug, w_d)


def _sc_scatter_rows(src, idx, n_rows):
    H, N, C = src.shape
    K = idx.shape[0]
    per_row = N // SC_WINDOW
    mesh = plsc.VectorSubcoreMesh(core_axis_name="c", subcore_axis_name="s")

    @functools.partial(pl.kernel, out_type=jax.ShapeDtypeStruct((H, n_rows, C), src.dtype), mesh=mesh,
                       scratch_types=[])
    def scatter_kernel(x_hbm, i_hbm, o_hbm):
        for h in range(H):
            def body(x_vmem, i_vmem):
                pltpu.sync_copy(x_vmem, o_hbm.at[h].at[i_vmem.at[0]])

            pltpu.emit_pipeline(
                body,
                grid=(K * per_row,),
                in_specs=[pl.BlockSpec((SC_WINDOW, C), lambda i: (i % per_row, 0)),
                          pl.BlockSpec((1, SC_WINDOW), lambda i: (i // per_row, i % per_row))],
                out_specs=[],
                core_axis_name=("c", "s"),
                dimension_semantics=(pltpu.PARALLEL,),
            )(x_hbm.at[h], i_hbm)

    return scatter_kernel(src, idx)


def _sc_gather_rows(src, idx):
    H, _, C = src.shape
    K, N = idx.shape
    per_row = N // SC_WINDOW
    mesh = plsc.VectorSubcoreMesh(core_axis_name="c", subcore_axis_name="s")

    @functools.partial(pl.kernel, out_type=jax.ShapeDtypeStruct((H, K * N, C), src.dtype), mesh=mesh,
                       scratch_types=[])
    def gather_kernel(x_hbm, i_hbm, o_hbm):
        for h in range(H):
            def body(i_vmem, o_vmem):
                pltpu.sync_copy(x_hbm.at[h].at[i_vmem.at[0]], o_vmem)

            pltpu.emit_pipeline(
                body,
                grid=(K * per_row,),
                in_specs=[pl.BlockSpec((1, SC_WINDOW), lambda i: (i // per_row, i % per_row))],
                out_specs=[pl.BlockSpec((SC_WINDOW, C), lambda i: (i, 0))],
                core_axis_name=("c", "s"),
                dimension_semantics=(pltpu.PARALLEL,),
            )(i_hbm, o_hbm.at[h])

    return gather_kernel(src, idx)


def _combine_kernel(*refs):
    y_refs = refs[:2 * TOP_K]
    w_ref, base_ref, mod_ref, o_ref = refs[2 * TOP_K:]
    T = base_ref.shape[0]
    tr = lax.broadcasted_iota(I32, (T, T), 0)
    tc = lax.broadcasted_iota(I32, (T, T), 1)
    wcol = _nt((tr == tc).astype(F32), w_ref[...], precision=HI)
    acc = [None] * 4
    for kk in range(TOP_K):
        wk = wcol[:, kk:kk + 1]
        for half in range(2):
            lo, hi = _unpack_halves(y_refs[2 * kk + half][...])
            for q, val in ((half, lo), (2 + half, hi)):
                acc[q] = val * wk if acc[q] is None else acc[q] + val * wk
    o_ref[...] = base_ref[...] + mod_ref[5:6, :] * jnp.concatenate(acc, axis=1)


def _combine(yg, w_t, base, mod, S, T=256):
    N, D = base.shape
    C = yg.shape[2]
    tpb = S // T
    n_tiles = N // T
    row = pl.BlockSpec((T, D), lambda i: (i, 0))
    piece = lambda kk, half: pl.BlockSpec((None, T, C), lambda i: (half, kk * n_tiles + i, 0))
    return pl.pallas_call(
        _combine_kernel,
        grid=(n_tiles,),
        in_specs=[piece(kk, half) for kk in range(TOP_K) for half in range(2)] + [
            pl.BlockSpec((TOP_K, T), lambda i: (0, i)),
            row,
            pl.BlockSpec((None, 6, D), lambda i: (i // tpb, 0, 0))],
        out_specs=row,
        out_shape=jax.ShapeDtypeStruct((N, D), F32),
        compiler_params=_cp("parallel"),
        name="combine",
    )(*([yg] * (2 * TOP_K)), w_t, base, mod)


def _layer(x, c, positions, layer_idx, w_ada, b_ada, norm1_g, w_in, w_gate, b_gate,
           q_norm_g, k_norm_g, lambda_q1, lambda_k1, lambda_q2, lambda_k2, subln_g,
           rwkv_mu, w_decay0, w_decay2, a0, a2, g2, k_k, k_a, r_k, ln_x_w, ln_x_b,
           w_branch_a, w_branch_b, w_out, norm2_g, w_router, router_bias,
           w_expert_up_gate, w_expert_down, w_shared_up_gate, w_shared_down):
    B, S, D = x.shape
    N = B * S
    E = w_router.shape[1]
    da_width = w_branch_a.shape[0]
    rw_width = w_branch_b.shape[0]
    lambda_init = 0.8 - 0.6 * math.exp(-0.3 * layer_idx)

    mod = _adaln(c, w_ada, b_ada)
    x2 = x.reshape(N, D)
    w_cat = jnp.concatenate([w_in, w_gate], axis=1).astype(BF16)
    qn, kn, v, r_, k_, v_, a_, ld_, g_, gate = _mixer_in(
        x2, positions.reshape(N, 1), mod, norm1_g, w_cat, b_gate, q_norm_g, k_norm_g,
        rwkv_mu, w_decay0, w_decay2, a0, a2, g2, S, da_width, rw_width)

    lam_vecs = jnp.stack([lambda_q1, lambda_k1, lambda_q2, lambda_k2])
    score_bound = 1.01 * DA_HEAD_DIM ** 0.5 * jnp.max(jnp.abs(q_norm_g)) * jnp.max(jnp.abs(k_norm_g))
    attn = _diff_attention(qn, kn, v, score_bound, lam_vecs, subln_g, B, S, lambda_init)

    seq = lambda t: t.reshape(B, S, rw_width)
    rw = _rwkv_scan(seq(r_), seq(k_), seq(v_), seq(a_), seq(ld_), seq(g_), k_k, k_a, r_k.reshape(-1),
                    ln_x_w, ln_x_b).reshape(N, rw_width)

    base, h2p, logits_t = _post(attn, rw, gate, x2, mod, w_branch_a.astype(BF16), w_branch_b.astype(BF16),
                                w_out.astype(BF16), norm2_g, w_router.T,
                                w_shared_up_gate.astype(BF16), w_shared_down.astype(BF16), S)

    eidx_t, w_t = _route(logits_t, router_bias)
    rank_t, counts = _ranks(eidx_t, E)
    R = EXPERT_TILE
    cnt = counts[:, 0].astype(I32)
    ustart = jnp.concatenate([jnp.zeros((1,), I32), jnp.cumsum(cnt)])
    pcnt = (cnt + R - 1) // R * R
    pstart = jnp.cumsum(pcnt) - pcnt
    dest_p = _dests(eidx_t, rank_t, pstart)
    n_rows = (N * TOP_K + E * (R - 1) + R - 1) // R * R
    xg = _sc_scatter_rows(h2p, dest_p, n_rows + (X_SLOTS - 1) * R)
    y = _experts(ustart, pstart, xg, w_expert_up_gate, w_expert_down, n_rows, R)
    yg = _sc_gather_rows(y, dest_p)
    out = _combine(yg, w_t, base, mod, S)
    return out.reshape(B, S, D)


def kernel(x, c, positions, w_ada, b_ada, norm1_g, w_in, w_gate, b_gate, q_norm_g, k_norm_g, lambda_q1, lambda_k1, lambda_q2, lambda_k2, subln_g, rwkv_mu, w_decay0, w_decay2, a0, a2, g2, k_k, k_a, r_k, ln_x_w, ln_x_b, w_branch_a, w_branch_b, w_out, norm2_g, w_router, router_bias, w_expert_up_gate, w_expert_down, w_shared_up_gate, w_shared_down):
    for l in range(w_ada.shape[0]):
        x = _layer(x, c, positions, l, w_ada[l], b_ada[l], norm1_g[l], w_in[l], w_gate[l], b_gate[l],
                   q_norm_g[l], k_norm_g[l], lambda_q1[l], lambda_k1[l], lambda_q2[l], lambda_k2[l],
                   subln_g[l], rwkv_mu[l], w_decay0[l], w_decay2[l], a0[l], a2[l], g2[l], k_k[l],
                   k_a[l], r_k[l], ln_x_w[l], ln_x_b[l], w_branch_a[l], w_branch_b[l], w_out[l],
                   norm2_g[l], w_router[l], router_bias[l], w_expert_up_gate[l], w_expert_down[l],
                   w_shared_up_gate[l], w_shared_down[l])
    return x
```

```python
import functools
import math

import jax
import jax.numpy as jnp
from jax import lax
from jax.experimental import pallas as pl
from jax.experimental.pallas import tpu as pltpu
from jax.experimental.pallas import tpu_sc as plsc

F32 = jnp.float32
BF16 = jnp.bfloat16
I32 = jnp.int32
U32 = jnp.uint32
HI = lax.Precision.HIGHEST

CHUNK = 64
ROPE_THETA = 10000.0
NORM_EPS = 1e-6
SUBLN_EPS = 1e-5
DA_HEAD_DIM = 64
RWKV_HEAD = 64
GN_EPS = 64e-5
TOP_K = 8
N_GROUPS = 8
TOPK_GROUPS = 4
ROUTED_SCALE = 2.5
EXPERT_TILE = 256
X_SLOTS = 4
RWKV_CHUNK = 64
LANES = 128
SC_WINDOW = 128
NEG = -1e30
MAX_PLAIN_SCORE = 40.0
VMEM_LIMIT = 56 * 1024 * 1024


def _cp(*sem):
    return pltpu.CompilerParams(dimension_semantics=sem, vmem_limit_bytes=VMEM_LIMIT)


def _bdot(a, b):
    return jnp.dot(a.astype(BF16), b.astype(BF16), preferred_element_type=F32)


def _fdot(a, b):
    return jnp.dot(a, b, precision=HI, preferred_element_type=F32)


def _nt(a, b, precision=None):
    return lax.dot_general(a, b, (((1,), (1,)), ((), ())), precision=precision,
                           preferred_element_type=F32)


def _tn(a, b, precision=None):
    return lax.dot_general(a, b, (((0,), (0,)), ((), ())), precision=precision,
                           preferred_element_type=F32)


def _pack_halves(x):
    c = x.shape[1] // 2
    lo = lax.bitcast_convert_type(x[:, :c], U32)
    hi = lax.bitcast_convert_type(x[:, c:], U32)
    return (hi & jnp.uint32(0xFFFF0000)) | (lo >> 16)


def _unpack_halves(w):
    lo = lax.bitcast_convert_type(w << 16, F32)
    hi = lax.bitcast_convert_type(w & jnp.uint32(0xFFFF0000), F32)
    return lo, hi


def _sigmoid(x):
    return 1.0 / (1.0 + jnp.exp(-x))


def _silu(x):
    return x * _sigmoid(x)


def _ada_kernel(c_ref, w_ref, b_ref, o_ref):
    o_ref[...] = _fdot(_silu(c_ref[...]), w_ref[...]) + b_ref[...]


def _adaln(c, w_ada, b_ada):
    B, D = c.shape
    rows = -(-B // 8) * 8
    cpad = jnp.zeros((rows, D), F32).at[:B].set(c)
    n_out = w_ada.shape[1]
    out = pl.pallas_call(
        _ada_kernel,
        grid=(n_out // D,),
        in_specs=[pl.BlockSpec((rows, D), lambda j: (0, 0)),
                  pl.BlockSpec((D, D), lambda j: (0, j)),
                  pl.BlockSpec((1, D), lambda j: (0, j))],
        out_specs=pl.BlockSpec((rows, D), lambda j: (0, j)),
        out_shape=jax.ShapeDtypeStruct((rows, n_out), F32),
        compiler_params=_cp("arbitrary"),
        name="adaln",
    )(cpad, w_ada, b_ada.reshape(1, n_out))
    return out[:B].reshape(B, n_out // D, D)


def _mixer_in_kernel(x_ref, xprev_ref, mod_ref, g_ref, w_ref, bg_ref, pos_ref, invf_ref, qg_ref, kg_ref,
                     mu_ref, w0_ref, w2_ref, a0_ref, a2_ref, g2_ref,
                     qn_ref, kn_ref, v_ref, r_ref, k_ref, vr_ref, a_ref, ld_ref, gr_ref, gate_ref,
                     *, da_width, rw_cols, rw_width, q_scale, tiles_per_seq):
    tm = x_ref.shape[0]

    def modulated(x):
        y = x * lax.rsqrt(jnp.mean(x * x, axis=-1, keepdims=True) + NORM_EPS) * g_ref[...]
        return (y * (1.0 + mod_ref[1:2, :]) + mod_ref[0:1, :]).astype(BF16)

    def proj(hb, c0, width, step=512):
        parts = [jnp.dot(hb, w_ref[:, c0 + o:c0 + min(o + step, width)], preferred_element_type=F32)
                 for o in range(0, width, step)]
        return parts[0] if len(parts) == 1 else jnp.concatenate(parts, axis=1)

    h = modulated(x_ref[...])

    lane = lax.broadcasted_iota(I32, (tm, LANES), 1)
    first = lane < DA_HEAD_DIM
    lo_half = (lane & (DA_HEAD_DIM - 1)) < DA_HEAD_DIM // 2
    ang = pos_ref[...].astype(F32) * invf_ref[...]
    cos = jnp.cos(ang)
    sin = jnp.sin(ang)
    sin = jnp.where(lo_half, -sin, sin)
    for c0, dst, gn_ref, mult in ((0, qn_ref, qg_ref, q_scale), (da_width, kn_ref, kg_ref, 1.0)):
        raw = proj(h, c0, da_width)
        for blk in range(da_width // LANES):
            x = raw[:, blk * LANES:(blk + 1) * LANES]
            xx = x * x
            s_first = jnp.sum(jnp.where(first, xx, 0.0), axis=-1, keepdims=True)
            s_second = jnp.sum(jnp.where(first, 0.0, xx), axis=-1, keepdims=True)
            ms = jnp.where(first, s_first, s_second) * (1.0 / DA_HEAD_DIM)
            xn = x * lax.rsqrt(ms + NORM_EPS) * gn_ref[...]
            rot = jnp.where(lo_half, pltpu.roll(xn, LANES - DA_HEAD_DIM // 2, axis=1),
                            pltpu.roll(xn, DA_HEAD_DIM // 2, axis=1))
            dst[:, blk * LANES:(blk + 1) * LANES] = ((xn * cos + rot * sin) * mult).astype(dst.dtype)
    v_ref[...] = proj(h, 2 * da_width, da_width).astype(v_ref.dtype)

    c_rw = 3 * da_width
    p = proj(h, c_rw, rw_cols)
    p_before = proj(modulated(xprev_ref[...]), c_rw, rw_cols)
    seq_start = (pl.program_id(0) % tiles_per_seq) == 0
    last_prev = jnp.where(seq_start, 0.0, p_before[7:8, :])
    rowi = lax.broadcasted_iota(I32, p.shape, 0)
    prev = jnp.where(rowi == 0, last_prev, pltpu.roll(p, 1, axis=0))
    xs = p + (prev - p) * mu_ref[...]
    width = rw_width
    r_ref[...] = xs[:, 0:width]
    k_ref[...] = xs[:, width:2 * width]
    vr_ref[...] = xs[:, 2 * width:3 * width]
    xwa = xs[:, 3 * width:3 * width + LANES]
    xg = xs[:, 3 * width + LANES:]
    z = w0_ref[...] + _bdot(jnp.tanh(xwa), w2_ref[...])
    w = -(jnp.maximum(-z, 0.0) + jnp.log(1.0 + jnp.exp(-jnp.abs(z)))) - 0.5
    ld_ref[...] = -jnp.exp(w)
    a_ref[...] = _sigmoid(a0_ref[...] + _bdot(xwa, a2_ref[...]))
    gr_ref[...] = _bdot(_sigmoid(xg), g2_ref[...])

    gate_ref[...] = _sigmoid(proj(h, c_rw + rw_cols, gate_ref.shape[1]) + bg_ref[...]).astype(gate_ref.dtype)


def _mixer_in(x2, pos2, mod, norm1_g, w_cat, b_gate, q_norm_g, k_norm_g, mu, w_decay0, w_decay2, a0, a2, g2,
              S, da_width, rw_width, tm=256):
    N, D = x2.shape
    n_gate = b_gate.shape[0]
    rw_cols = mu.shape[0]
    tpb = S // tm
    d = DA_HEAD_DIM
    inv_freq = 1.0 / (ROPE_THETA ** (jnp.arange(0, d, 2, dtype=F32) / d))
    invf = jnp.tile(inv_freq, LANES // (d // 2)).reshape(1, LANES)
    dl, al = w_decay2.shape[0], a2.shape[0]
    assert dl + al == LANES and g2.shape[0] == LANES
    w2p = jnp.zeros((LANES, rw_width), F32).at[:dl].set(w_decay2)
    a2p = jnp.zeros((LANES, rw_width), F32).at[dl:].set(a2)
    kern = functools.partial(_mixer_in_kernel, da_width=da_width, rw_cols=rw_cols, rw_width=rw_width,
                             q_scale=d ** -0.5 * math.log2(math.e),
                             tiles_per_seq=tpb)
    row = lambda w: pl.BlockSpec((tm, w), lambda i: (i, 0))
    vec = lambda n: pl.BlockSpec((1, n), lambda i: (0, 0))
    mat = pl.BlockSpec((LANES, rw_width), lambda i: (0, 0))
    f32 = lambda w: jax.ShapeDtypeStruct((N, w), F32)
    bf16 = lambda w: jax.ShapeDtypeStruct((N, w), BF16)
    return pl.pallas_call(
        kern,
        grid=(N // tm,),
        in_specs=[row(D),
                  pl.BlockSpec((8, D), lambda i: (jnp.maximum(i * (tm // 8) - 1, 0), 0)),
                  pl.BlockSpec((None, 6, D), lambda i: (i // tpb, 0, 0)),
                  vec(D),
                  pl.BlockSpec(w_cat.shape, lambda i: (0, 0)),
                  vec(n_gate),
                  pl.BlockSpec((tm, 1), lambda i: (i, 0)),
                  vec(LANES), vec(LANES), vec(LANES),
                  vec(rw_cols), vec(rw_width), mat, vec(rw_width), mat, mat],
        out_specs=[row(da_width)] * 3 + [row(rw_width)] * 6 + [row(n_gate)],
        out_shape=[bf16(da_width)] * 3 + [f32(rw_width)] * 6 + [bf16(n_gate)],
        compiler_params=_cp("parallel"),
        name="mixer_in",
    )(x2, x2, mod, norm1_g.reshape(1, D), w_cat, b_gate.reshape(1, n_gate), pos2, invf,
      jnp.tile(q_norm_g, 2).reshape(1, LANES), jnp.tile(k_norm_g, 2).reshape(1, LANES),
      mu.reshape(1, rw_cols), w_decay0.reshape(1, rw_width), w2p, a0.reshape(1, rw_width), a2p, g2)


def _attn_kernel(flag_ref, q_ref, k_ref, v_ref, lam_ref, sg_ref, o_ref, qz_ref, m_ref, l_ref, lp_ref, acc_ref,
                 *, tq, lambda_init):
    i = pl.program_id(2)
    lane = lax.broadcasted_iota(I32, (tq, LANES), 1)
    q = q_ref[...]
    zero = jnp.zeros_like(q)
    qz_ref[0:tq, :] = jnp.where(lane < DA_HEAD_DIM, q, zero)
    qz_ref[tq:, :] = jnp.where(lane >= DA_HEAD_DIM, q, zero)
    acc_ref[...] = jnp.zeros_like(acc_ref)
    bounded = flag_ref[0] == 1

    def scores(j, masked):
        off = pl.multiple_of(j * tq, tq)
        s = _nt(qz_ref[...], k_ref[pl.ds(off, tq), :])
        if masked:
            row = lax.broadcasted_iota(I32, s.shape, 0)
            col = lax.broadcasted_iota(I32, s.shape, 1)
            s = jnp.where((col // CHUNK) <= ((row & (tq - 1)) // CHUNK), s, NEG)
        return s, off

    def plain_step(j, masked):
        s, off = scores(j, masked)
        pr = jnp.exp2(s)
        part = pr[:, 0:LANES]
        for cblk in range(1, tq // LANES):
            part = part + pr[:, cblk * LANES:(cblk + 1) * LANES]
        lp_ref[...] += part
        acc_ref[...] += jnp.dot(pr.astype(BF16), v_ref[pl.ds(off, tq), :], preferred_element_type=F32)

    def online_step(j, masked):
        s, off = scores(j, masked)
        m_old = m_ref[...]
        m_new = jnp.maximum(m_old, jnp.max(s, axis=-1, keepdims=True))
        alpha = jnp.exp2(m_old - m_new)
        pr = jnp.exp2(s - m_new)
        l_ref[...] = alpha * l_ref[...] + jnp.sum(pr, axis=-1, keepdims=True)
        acc_ref[...] = alpha * acc_ref[...] + jnp.dot(pr.astype(BF16), v_ref[pl.ds(off, tq), :],
                                                      preferred_element_type=F32)
        m_ref[...] = m_new

    def run(step):
        def body(j, carry):
            step(j, False)
            return carry
        lax.fori_loop(0, i, body, 0)
        step(i, True)

    @pl.when(bounded)
    def _():
        lp_ref[...] = jnp.zeros_like(lp_ref)
        run(plain_step)
        l_ref[...] = jnp.sum(lp_ref[...], axis=-1, keepdims=True)

    @pl.when(jnp.logical_not(bounded))
    def _():
        m_ref[...] = jnp.full_like(m_ref, NEG)
        l_ref[...] = jnp.zeros_like(l_ref)
        run(online_step)

    lv = lam_ref[...]
    lam = (jnp.exp(jnp.sum(lv[0:1] * lv[1:2], keepdims=True))
           - jnp.exp(jnp.sum(lv[2:3] * lv[3:4], keepdims=True)) + lambda_init)
    o1 = acc_ref[0:tq, :] / l_ref[0:tq, :]
    o2 = acc_ref[tq:, :] / l_ref[tq:, :]
    o = o1 - lam * o2
    o = o * lax.rsqrt(jnp.mean(o * o, axis=-1, keepdims=True) + SUBLN_EPS) * sg_ref[...]
    o_ref[...] = (o * (1.0 - lambda_init)).astype(o_ref.dtype)


def _diff_attention(qn, kn, v, score_bound, lam_vecs, subln_g, B, S, lambda_init, tq=512):
    W = qn.shape[1]
    H = W // LANES
    q3 = qn.reshape(B, S, W)
    k3 = kn.reshape(B, S, W)
    v3 = v.reshape(B, S, W)
    flag = (score_bound <= MAX_PLAIN_SCORE).astype(I32).reshape(1)
    qblk = pl.BlockSpec((None, tq, LANES), lambda b, h, i, f: (b, i, h))
    kvblk = pl.BlockSpec((None, S, LANES), lambda b, h, i, f: (b, 0, h))
    grid_spec = pltpu.PrefetchScalarGridSpec(
        num_scalar_prefetch=1,
        grid=(B, H, S // tq),
        in_specs=[qblk, kvblk, kvblk,
                  pl.BlockSpec((4, DA_HEAD_DIM), lambda b, h, i, f: (0, 0)),
                  pl.BlockSpec((1, LANES), lambda b, h, i, f: (0, 0))],
        out_specs=qblk,
        scratch_shapes=[pltpu.VMEM((2 * tq, LANES), BF16),
                        pltpu.VMEM((2 * tq, 1), F32),
                        pltpu.VMEM((2 * tq, 1), F32),
                        pltpu.VMEM((2 * tq, LANES), F32),
                        pltpu.VMEM((2 * tq, LANES), F32)],
    )
    out = pl.pallas_call(
        functools.partial(_attn_kernel, tq=tq, lambda_init=lambda_init),
        grid_spec=grid_spec,
        out_shape=jax.ShapeDtypeStruct((B, S, W), BF16),
        compiler_params=_cp("parallel", "parallel", "arbitrary"),
        name="diff_attn",
    )(flag, q3, k3, v3, lam_vecs, subln_g.reshape(1, LANES))
    return out.reshape(B * S, W)


def _stackmask(m):
    lane = lax.broadcasted_iota(I32, m.shape, 1)
    z = jnp.zeros_like(m)
    return jnp.concatenate([jnp.where(lane < RWKV_HEAD, m, z), jnp.where(lane >= RWKV_HEAD, m, z)], axis=0)


def _pair_sum(x, first):
    s1 = jnp.sum(jnp.where(first, x, 0.0), axis=-1, keepdims=True)
    s2 = jnp.sum(jnp.where(first, 0.0, x), axis=-1, keepdims=True)
    return jnp.where(first, s1, s2)


def _rwkv_scan_kernel(r_ref, k_ref, v_ref, a_ref, ld_ref, g_ref, kk_ref, ka_ref, rk_ref, lnw_ref, lnb_ref,
                      o_ref, s_ref, *, L):
    tm, W = r_ref.shape
    n_chunks = tm // L
    n_pairs = W // LANES
    hd = RWKV_HEAD
    bf = lambda t: t.astype(BF16)

    @pl.when(pl.program_id(1) == 0)
    def _():
        s_ref[...] = jnp.zeros_like(s_ref)

    row = lax.broadcasted_iota(I32, (tm, tm), 0)
    col = lax.broadcasted_iota(I32, (tm, tm), 1)
    tri = jnp.where(jnp.logical_and(col <= row, (col // L) == (row // L)), 1.0, 0.0).astype(BF16)
    ld = ld_ref[...]
    ld_hi = bf(ld)
    rem = ld - ld_hi.astype(F32)
    ld_mid = bf(rem)
    ld_lo = bf(rem - ld_mid.astype(F32))
    c = (jnp.dot(tri, ld_hi, preferred_element_type=F32) + jnp.dot(tri, ld_mid, preferred_element_type=F32)
         + jnp.dot(tri, ld_lo, preferred_element_type=F32))
    ec = jnp.exp(c)
    eci = jnp.exp(-c)
    ecm = jnp.exp(c - ld)
    r = r_ref[...]
    k = k_ref[...]
    v = v_ref[...]
    a = a_ref[...]
    kkr = k * kk_ref[...]
    kmod = k * (1.0 + (a - 1.0) * ka_ref[...])
    brk = r * kmod * rk_ref[...]

    lane = lax.broadcasted_iota(I32, (L, LANES), 1)
    rowl = lax.broadcasted_iota(I32, (L, LANES), 0)
    first = lane < hd
    lane_h = lane & (hd - 1)
    strict = lane_h < rowl
    incl = lane_h <= rowl
    eye = jnp.where(lane_h == rowl, 1.0, 0.0)

    chains = [(ch, p) for ch in range(n_chunks) for p in range(n_pairs)]
    rsl = lambda ch: slice(ch * L, (ch + 1) * L)
    csl = lambda p: slice(p * LANES, (p + 1) * LANES)
    fdot = lambda x, y: jnp.dot(x, y, preferred_element_type=F32)
    at, bt, kt, rt, vh, g_l = {}, {}, {}, {}, {}, {}
    for c_ in chains:
        ch, p = c_
        rs, cs = rsl(ch), csl(p)
        kkh = kkr[rs, cs]
        kkh = kkh / jnp.maximum(jnp.sqrt(_pair_sum(kkh * kkh, first)), 1e-12)
        vh[c_] = v[rs, cs]
        g_l[c_] = ec[ch * L + L - 1:ch * L + L, cs]
        at[c_] = -kkh * ecm[rs, cs]
        bt[c_] = kkh * a[rs, cs] * eci[rs, cs]
        kt[c_] = kmod[rs, cs] * eci[rs, cs]
        rt[c_] = r[rs, cs] * ec[rs, cs]
    gm = {c_: _nt(bf(jnp.concatenate([at[c_], rt[c_]], axis=0)),
                  jnp.concatenate([_stackmask(bf(bt[c_])), _stackmask(bf(kt[c_]))], axis=0)) for c_ in chains}
    a_ab = {c_: jnp.where(strict, gm[c_][:L, :LANES], 0.0) for c_ in chains}
    vsm = {c_: _stackmask(bf(vh[c_])) for c_ in chains}
    cmat = {c_: fdot(bf(jnp.where(strict, gm[c_][:L, LANES:], 0.0)), vsm[c_]) for c_ in chains}
    t_inv = {c_: eye + a_ab[c_] for c_ in chains}
    pw = {c_: bf(a_ab[c_]) for c_ in chains}
    for _ in range(int(math.log2(L)) - 1):
        pw = {c_: bf(fdot(pw[c_], _stackmask(pw[c_]))) for c_ in chains}
        t_inv = {c_: t_inv[c_] + fdot(pw[c_], _stackmask(bf(t_inv[c_]))) for c_ in chains}
    zz = {c_: fdot(bf(t_inv[c_]), jnp.concatenate([_stackmask(bf(at[c_])), _stackmask(bf(cmat[c_]))], axis=1))
          for c_ in chains}
    qy = {c_: fdot(bf(jnp.where(incl, gm[c_][L:, :LANES], 0.0)),
                   jnp.concatenate([_stackmask(bf(zz[c_][:, :LANES])), _stackmask(bf(zz[c_][:, LANES:]))], axis=1))
          for c_ in chains}
    y0 = {c_: qy[c_][:, LANES:] + fdot(bf(jnp.where(incl, gm[c_][L:, LANES:], 0.0)), vsm[c_]) for c_ in chains}
    qa = {c_: bf(jnp.concatenate([rt[c_] + qy[c_][:, :LANES], zz[c_][:, :LANES]], axis=0)) for c_ in chains}
    bkg = {c_: bf(jnp.concatenate([bt[c_] * g_l[c_], kt[c_] * g_l[c_]], axis=0)) for c_ in chains}

    lane_s = lax.broadcasted_iota(I32, (hd, LANES), 1)
    sp = [s_ref[p] for p in range(n_pairs)]
    for ch in range(n_chunks):
        rs = rsl(ch)
        yw = [_nt(qa[ch, p], _stackmask(bf(sp[p]))) for p in range(n_pairs)]
        upd = [_tn(bf(jnp.concatenate([yw[p][L:] + zz[ch, p][:, LANES:], vh[ch, p]], axis=0)), bkg[ch, p])
               for p in range(n_pairs)]
        for p in range(n_pairs):
            cs = csl(p)
            sp[p] = sp[p] * g_l[ch, p] + jnp.where(lane_s < hd, upd[p][:hd], upd[p][hd:])
            y = yw[p][:L] + y0[ch, p]
            mean = _pair_sum(y, first) * (1.0 / hd)
            yc = y - mean
            var = _pair_sum(yc * yc, first) * (1.0 / hd)
            yn = yc * lax.rsqrt(var + GN_EPS) * lnw_ref[:, cs] + lnb_ref[:, cs]
            bonus = _pair_sum(brk[rs, cs], first) * vh[ch, p]
            o_ref[rs, cs] = ((yn + bonus) * g_ref[rs, cs]).astype(o_ref.dtype)
    for p in range(n_pairs):
        s_ref[p] = sp[p]


def _rwkv_scan(r, k, v, a, ld, g, k_k, k_a, r_k, ln_w, ln_b, L=RWKV_CHUNK, tm=256):
    B, S, W = r.shape
    seq = pl.BlockSpec((None, tm, W), lambda b, c: (b, c, 0))
    vec = pl.BlockSpec((1, W), lambda b, c: (0, 0))
    return pl.pallas_call(
        functools.partial(_rwkv_scan_kernel, L=L),
        grid=(B, S // tm),
        in_specs=[seq] * 6 + [vec] * 5,
        out_specs=seq,
        out_shape=jax.ShapeDtypeStruct((B, S, W), BF16),
        scratch_shapes=[pltpu.VMEM((W // LANES, RWKV_HEAD, LANES), F32)],
        compiler_params=_cp("parallel", "arbitrary"),
        name="rwkv_scan",
    )(r, k, v, a, ld, g, k_k.reshape(1, W), k_a.reshape(1, W), r_k.reshape(1, W),
      ln_w.reshape(1, W), ln_b.reshape(1, W))


def _post_kernel(attn_ref, rw_ref, gate_ref, x_ref, mod_ref, wa_ref, wb_ref, wo_ref, g2_ref, wrh_ref, wrm_ref,
                 bias_ref, sug_ref, sd_ref, base_ref, h2p_ref, e_ref, w_ref, rank_ref, cnt_ref, carry_ref):
    D = x_ref.shape[1]

    @pl.when(pl.program_id(0) == 0)
    def _():
        carry_ref[...] = jnp.zeros_like(carry_ref)

    ya = jnp.dot(attn_ref[...], wa_ref[...], preferred_element_type=F32)
    yb = jnp.dot(rw_ref[...], wb_ref[...], preferred_element_type=F32)
    m = gate_ref[:, 0:D] * ya + gate_ref[:, D:] * yb
    x1 = x_ref[...] + mod_ref[2:3, :] * jnp.dot(m.astype(BF16), wo_ref[...], preferred_element_type=F32)
    y = x1 * lax.rsqrt(jnp.mean(x1 * x1, axis=-1, keepdims=True) + NORM_EPS) * g2_ref[...]
    h2 = y * (1.0 + mod_ref[4:5, :]) + mod_ref[3:4, :]
    hb = h2.astype(BF16)
    hm = (h2 - hb.astype(F32)).astype(BF16)
    logits = _nt(wrh_ref[...], hb) + _nt(wrh_ref[...], hm) + _nt(wrm_ref[...], hb)
    eidx, wts = _route_tokens(logits, bias_ref[...])
    e_ref[...] = eidx
    w_ref[...] = wts
    rank_ref[...] = _rank_tokens(eidx, carry_ref)
    cnt_ref[...] = carry_ref[...]
    packed = _pack_halves(hb.astype(F32))
    half = packed.shape[1] // 2
    h2p_ref[0] = packed[:, :half]
    h2p_ref[1] = packed[:, half:]
    F = sd_ref.shape[0]
    gu = jnp.dot(hb, sug_ref[...], preferred_element_type=F32)
    shared = jnp.dot((_silu(gu[:, :F]) * gu[:, F:]).astype(BF16), sd_ref[...], preferred_element_type=F32)
    base_ref[...] = x1 + mod_ref[5:6, :] * shared


def _post(attn, rw, gate, x2, mod, wa, wb, wo, norm2_g, w_router_t, router_bias, sug, sd, S, tm=256):
    N, D = x2.shape
    E = w_router_t.shape[0]
    wr_hi = w_router_t.astype(BF16)
    wr_mid = (w_router_t - wr_hi.astype(F32)).astype(BF16)
    tpb = S // tm
    row = lambda w: pl.BlockSpec((tm, w), lambda i: (i, 0))
    full = lambda a: pl.BlockSpec(a.shape, lambda i: (0, 0))
    slots = pl.BlockSpec((TOP_K, tm), lambda i: (0, i))
    return pl.pallas_call(
        _post_kernel,
        grid=(N // tm,),
        in_specs=[row(attn.shape[1]), row(rw.shape[1]), row(gate.shape[1]), row(D),
                  pl.BlockSpec((None, 6, D), lambda i: (i // tpb, 0, 0)),
                  full(wa), full(wb), full(wo), pl.BlockSpec((1, D), lambda i: (0, 0)), full(wr_hi), full(wr_mid),
                  pl.BlockSpec((E, 1), lambda i: (0, 0)), full(sug), full(sd)],
        out_specs=[row(D), pl.BlockSpec((2, tm, D // 4), lambda i: (0, i, 0)), slots, slots, slots,
                   pl.BlockSpec((E, LANES), lambda i: (0, 0))],
        out_shape=[jax.ShapeDtypeStruct((N, D), F32), jax.ShapeDtypeStruct((2, N, D // 4), U32),
                   jax.ShapeDtypeStruct((TOP_K, N), I32), jax.ShapeDtypeStruct((TOP_K, N), F32),
                   jax.ShapeDtypeStruct((TOP_K, N), I32), jax.ShapeDtypeStruct((E, LANES), F32)],
        scratch_shapes=[pltpu.VMEM((E, LANES), F32)],
        compiler_params=_cp("arbitrary"),
        name="post_mixer",
    )(attn, rw, gate, x2, mod, wa, wb, wo, norm2_g.reshape(1, D), wr_hi, wr_mid, router_bias.reshape(E, 1), sug, sd)


def _first_argmax(x, idx, sentinel):
    m = jnp.max(x, axis=0, keepdims=True)
    return m, jnp.min(jnp.where(x == m, idx, sentinel), axis=0, keepdims=True)


def _route_tokens(logits, bias):
    E, T = logits.shape
    gsz = E // N_GROUPS
    scores = _sigmoid(logits)
    biased = scores + bias
    ig = lax.broadcasted_iota(I32, (gsz, T), 0)
    gs = []
    for g in range(N_GROUPS):
        blk = biased[g * gsz:(g + 1) * gsz, :]
        m1, i1 = _first_argmax(blk, ig, gsz)
        m2 = jnp.max(jnp.where(ig == i1, -jnp.inf, blk), axis=0, keepdims=True)
        gs.append(m1 + m2)
    gsc = jnp.concatenate(gs, axis=0)
    i8 = lax.broadcasted_iota(I32, (N_GROUPS, T), 0)
    chosen = jnp.zeros((N_GROUPS, T), F32)
    for _ in range(TOPK_GROUPS):
        _, gi = _first_argmax(gsc, i8, N_GROUPS)
        hit = i8 == gi
        chosen = jnp.where(hit, 1.0, chosen)
        gsc = jnp.where(hit, -jnp.inf, gsc)
    masked = jnp.concatenate(
        [jnp.where(chosen[g:g + 1, :] > 0.0, biased[g * gsz:(g + 1) * gsz, :], -jnp.inf)
         for g in range(N_GROUPS)], axis=0)
    ie = lax.broadcasted_iota(I32, (E, T), 0)
    idxs, wts = [], []
    for _ in range(TOP_K):
        _, ei = _first_argmax(masked, ie, E)
        hit = ie == ei
        idxs.append(ei)
        wts.append(jnp.sum(jnp.where(hit, scores, 0.0), axis=0, keepdims=True))
        masked = jnp.where(hit, -jnp.inf, masked)
    wt = jnp.concatenate(wts, axis=0)
    return jnp.concatenate(idxs, axis=0), wt / jnp.sum(wt, axis=0, keepdims=True) * ROUTED_SCALE


def _rank_tokens(e, carry_ref):
    n_experts = carry_ref.shape[0]
    T = e.shape[1]
    ie = lax.broadcasted_iota(I32, (n_experts, T), 0)
    hits = [ie == e[kk:kk + 1, :] for kk in range(TOP_K)]
    onehot = jnp.zeros((n_experts, T), F32)
    for hsel in hits:
        onehot = onehot + hsel.astype(F32)
    tr = lax.broadcasted_iota(I32, (T, T), 0)
    tc = lax.broadcasted_iota(I32, (T, T), 1)
    before = (tr < tc).astype(BF16)
    base = _bdot(onehot, before) + carry_ref[:, 0:1]
    rank = jnp.concatenate(
        [jnp.sum(jnp.where(hsel, base, 0.0), axis=0, keepdims=True) for hsel in hits], axis=0).astype(I32)
    carry_ref[...] = carry_ref[...] + jnp.sum(onehot, axis=1, keepdims=True)
    return rank


def _dest_kernel(e_ref, rank_ref, start_ref, d_ref):
    E = start_ref.shape[0]
    T = e_ref.shape[1]
    ie = lax.broadcasted_iota(I32, (E, T), 0)
    e = e_ref[...]
    start = start_ref[:, 0:1]
    rows = [jnp.sum(jnp.where(ie == e[kk:kk + 1, :], start, 0.0), axis=0, keepdims=True) for kk in range(TOP_K)]
    d_ref[...] = jnp.concatenate(rows, axis=0).astype(I32) + rank_ref[...]


def _dests(eidx_t, rank_t, pstart, T=512):
    N = eidx_t.shape[1]
    E = pstart.shape[0]
    blk = pl.BlockSpec((TOP_K, T), lambda i: (0, i))
    return pl.pallas_call(
        _dest_kernel,
        grid=(N // T,),
        in_specs=[blk, blk, pl.BlockSpec((E, LANES), lambda i: (0, 0))],
        out_specs=blk,
        out_shape=jax.ShapeDtypeStruct((TOP_K, N), I32),
        compiler_params=_cp("parallel"),
        name="dests",
    )(eidx_t, rank_t, jnp.broadcast_to(pstart.astype(F32)[:, None], (E, LANES)))


def _expert_kernel(us_ref, ps_ref, x_hbm, wug_ref, wd_ref, y_hbm,
                   wug_bf, wd_bf, xbuf, ybuf, cnt_ref, xsem, ysem, *, n_rows):
    e = pl.program_id(0)
    _, _, R, Ch = xbuf.shape
    C = 2 * Ch
    F = wd_ref.shape[0]

    def x_copy(slot, half, row):
        return pltpu.make_async_copy(x_hbm.at[half, pl.ds(row, R)], xbuf.at[slot, half], xsem.at[slot])

    def y_copy(slot, half, row):
        return pltpu.make_async_copy(ybuf.at[slot, half], y_hbm.at[half, pl.ds(row, R)], ysem.at[slot])

    def start(copy, slot, row):
        for half in range(2):
            copy(slot, half, row).start(priority=1)

    def wait(copy, slot):
        for half in range(2):
            copy(slot, half, 0).wait()

    @pl.when(e == 0)
    def _():
        cnt_ref[0] = 0
        for t in range(X_SLOTS - 1):
            start(x_copy, t, t * R)
        ybuf[...] = jnp.zeros_like(ybuf)
        for slot in range(2):
            start(y_copy, slot, n_rows + slot * R)

    wug_bf[...] = wug_ref[...].astype(BF16)
    wd_bf[...] = wd_ref[...].astype(BF16)
    n_valid = us_ref[e + 1] - us_ref[e]
    pbase = ps_ref[e]
    n_tiles = lax.shift_right_logical(n_valid + (R - 1), int(math.log2(R)))
    rowid = lax.broadcasted_iota(I32, (R, C), 0)

    def tile(i, carry):
        n_done = cnt_ref[0]
        slot = n_done & 1
        xslot = lax.rem(n_done, X_SLOTS)
        ahead = n_done + (X_SLOTS - 1)
        start(x_copy, lax.rem(ahead, X_SLOTS), pl.multiple_of(ahead * R, R))
        wait(x_copy, xslot)
        x = jnp.concatenate([xbuf[xslot, 0], xbuf[xslot, 1]], axis=1)
        x = jnp.where(rowid < n_valid - i * R, x, jnp.uint32(0))
        lo, hi = _unpack_halves(x)
        gu = (jnp.dot(lo.astype(BF16), wug_bf[0:C, :], preferred_element_type=F32)
              + jnp.dot(hi.astype(BF16), wug_bf[C:, :], preferred_element_type=F32))
        hid = (_silu(gu[:, :F]) * gu[:, F:]).astype(BF16)
        y = jnp.dot(hid, wd_bf[...], preferred_element_type=F32)
        wait(y_copy, slot)
        packed = _pack_halves(y.astype(BF16).astype(F32))
        ybuf[slot, 0] = packed[:, :Ch]
        ybuf[slot, 1] = packed[:, Ch:]
        start(y_copy, slot, pl.multiple_of(pbase + i * R, R))
        cnt_ref[0] = n_done + 1
        return carry

    lax.fori_loop(0, n_tiles, tile, 0)

    @pl.when(e == pl.num_programs(0) - 1)
    def _():
        for t in range(X_SLOTS - 1):
            wait(x_copy, lax.rem(cnt_ref[0] + t, X_SLOTS))
        for slot in range(2):
            wait(y_copy, slot)
        ybuf[0] = jnp.zeros((2, R, Ch), U32)
        first = lax.shift_right_logical(pbase + n_tiles * R, int(math.log2(R)))
        n_left = n_rows // R - first

        def fill(t, carry):
            start(y_copy, 0, pl.multiple_of((first + t) * R, R))
            return carry

        def drain(t, carry):
            wait(y_copy, 0)
            return carry

        lax.fori_loop(0, n_left, fill, 0)
        lax.fori_loop(0, n_left, drain, 0)


def _experts(ustart, pstart, xg, w_ug, w_d, n_rows, R):
    _, _, Ch = xg.shape
    E, D, F2 = w_ug.shape
    F = w_d.shape[1]
    grid_spec = pltpu.PrefetchScalarGridSpec(
        num_scalar_prefetch=2,
        grid=(E,),
        in_specs=[pl.BlockSpec(memory_space=pl.ANY),
                  pl.BlockSpec((None, D, F2), lambda e, us, ps: (e, 0, 0)),
                  pl.BlockSpec((None, F, D), lambda e, us, ps: (e, 0, 0))],
        out_specs=pl.BlockSpec(memory_space=pl.ANY),
        scratch_shapes=[pltpu.VMEM((D, F2), BF16), pltpu.VMEM((F, D), BF16),
                        pltpu.VMEM((X_SLOTS, 2, R, Ch), U32), pltpu.VMEM((2, 2, R, Ch), U32), pltpu.SMEM((1,), I32),
                        pltpu.SemaphoreType.DMA((X_SLOTS,)), pltpu.SemaphoreType.DMA((2,))],
    )
    return pl.pallas_call(
        functools.partial(_expert_kernel, n_rows=n_rows),
        grid_spec=grid_spec,
        out_shape=jax.ShapeDtypeStruct((2, n_rows + 2 * R, Ch), U32),
        compiler_params=_cp("arbitrary"),
        name="experts",
    )(ustart, pstart, xg, w_ug, w_d)


def _sc_scatter_rows(src, idx, n_rows):
    H, N, C = src.shape
    K = idx.shape[0]
    per_row = N // SC_WINDOW
    mesh = plsc.VectorSubcoreMesh(core_axis_name="c", subcore_axis_name="s")

    @functools.partial(pl.kernel, out_type=jax.ShapeDtypeStruct((H, n_rows, C), src.dtype), mesh=mesh,
                       scratch_types=[])
    def scatter_kernel(x_hbm, i_hbm, o_hbm):
        for h in range(H):
            def body(x_vmem, i_vmem):
                pltpu.sync_copy(x_vmem, o_hbm.at[h].at[i_vmem.at[0]])

            pltpu.emit_pipeline(
                body,
                grid=(K * per_row,),
                in_specs=[pl.BlockSpec((SC_WINDOW, C), lambda i: (i % per_row, 0)),
                          pl.BlockSpec((1, SC_WINDOW), lambda i: (i // per_row, i % per_row))],
                out_specs=[],
                core_axis_name=("c", "s"),
                dimension_semantics=(pltpu.PARALLEL,),
            )(x_hbm.at[h], i_hbm)

    return scatter_kernel(src, idx)


def _sc_gather_rows(src, idx):
    H, _, C = src.shape
    K, N = idx.shape
    per_row = N // SC_WINDOW
    mesh = plsc.VectorSubcoreMesh(core_axis_name="c", subcore_axis_name="s")

    @functools.partial(pl.kernel, out_type=jax.ShapeDtypeStruct((H, K * N, C), src.dtype), mesh=mesh,
                       scratch_types=[])
    def gather_kernel(x_hbm, i_hbm, o_hbm):
        for h in range(H):
            def body(i_vmem, o_vmem):
                pltpu.sync_copy(x_hbm.at[h].at[i_vmem.at[0]], o_vmem)

            pltpu.emit_pipeline(
                body,
                grid=(K * per_row,),
                in_specs=[pl.BlockSpec((1, SC_WINDOW), lambda i: (i // per_row, i % per_row))],
                out_specs=[pl.BlockSpec((SC_WINDOW, C), lambda i: (i, 0))],
                core_axis_name=("c", "s"),
                dimension_semantics=(pltpu.PARALLEL,),
            )(i_hbm, o_hbm.at[h])

    return gather_kernel(src, idx)


def _combine_kernel(*refs):
    y_refs = refs[:2 * TOP_K]
    w_ref, base_ref, mod_ref, o_ref = refs[2 * TOP_K:]
    T = base_ref.shape[0]
    tr = lax.broadcasted_iota(I32, (T, T), 0)
    tc = lax.broadcasted_iota(I32, (T, T), 1)
    wcol = _nt((tr == tc).astype(F32), w_ref[...], precision=HI)
    acc = [None] * 4
    for kk in range(TOP_K):
        wk = wcol[:, kk:kk + 1]
        for half in range(2):
            lo, hi = _unpack_halves(y_refs[2 * kk + half][...])
            for q, val in ((half, lo), (2 + half, hi)):
                acc[q] = val * wk if acc[q] is None else acc[q] + val * wk
    o_ref[...] = base_ref[...] + mod_ref[5:6, :] * jnp.concatenate(acc, axis=1)


def _combine(yg, w_t, base, mod, S, T=256):
    N, D = base.shape
    C = yg.shape[2]
    tpb = S // T
    n_tiles = N // T
    row = pl.BlockSpec((T, D), lambda i: (i, 0))
    piece = lambda kk, half: pl.BlockSpec((None, T, C), lambda i: (half, kk * n_tiles + i, 0))
    return pl.pallas_call(
        _combine_kernel,
        grid=(n_tiles,),
        in_specs=[piece(kk, half) for kk in range(TOP_K) for half in range(2)] + [
            pl.BlockSpec((TOP_K, T), lambda i: (0, i)),
            row,
            pl.BlockSpec((None, 6, D), lambda i: (i // tpb, 0, 0))],
        out_specs=row,
        out_shape=jax.ShapeDtypeStruct((N, D), F32),
        compiler_params=_cp("parallel"),
        name="combine",
    )(*([yg] * (2 * TOP_K)), w_t, base, mod)


def _layer(x, c, positions, layer_idx, w_ada, b_ada, norm1_g, w_in, w_gate, b_gate,
           q_norm_g, k_norm_g, lambda_q1, lambda_k1, lambda_q2, lambda_k2, subln_g,
           rwkv_mu, w_decay0, w_decay2, a0, a2, g2, k_k, k_a, r_k, ln_x_w, ln_x_b,
           w_branch_a, w_branch_b, w_out, norm2_g, w_router, router_bias,
           w_expert_up_gate, w_expert_down, w_shared_up_gate, w_shared_down):
    B, S, D = x.shape
    N = B * S
    E = w_router.shape[1]
    da_width = w_branch_a.shape[0]
    rw_width = w_branch_b.shape[0]
    lambda_init = 0.8 - 0.6 * math.exp(-0.3 * layer_idx)

    mod = _adaln(c, w_ada, b_ada)
    x2 = x.reshape(N, D)
    w_cat = jnp.concatenate([w_in, w_gate], axis=1).astype(BF16)
    qn, kn, v, r_, k_, v_, a_, ld_, g_, gate = _mixer_in(
        x2, positions.reshape(N, 1), mod, norm1_g, w_cat, b_gate, q_norm_g, k_norm_g,
        rwkv_mu, w_decay0, w_decay2, a0, a2, g2, S, da_width, rw_width)

    lam_vecs = jnp.stack([lambda_q1, lambda_k1, lambda_q2, lambda_k2])
    score_bound = 1.01 * DA_HEAD_DIM ** 0.5 * jnp.max(jnp.abs(q_norm_g)) * jnp.max(jnp.abs(k_norm_g))
    attn = _diff_attention(qn, kn, v, score_bound, lam_vecs, subln_g, B, S, lambda_init)

    seq = lambda t: t.reshape(B, S, rw_width)
    rw = _rwkv_scan(seq(r_), seq(k_), seq(v_), seq(a_), seq(ld_), seq(g_), k_k, k_a, r_k.reshape(-1),
                    ln_x_w, ln_x_b).reshape(N, rw_width)

    base, h2p, eidx_t, w_t, rank_t, counts = _post(
        attn, rw, gate, x2, mod, w_branch_a.astype(BF16), w_branch_b.astype(BF16), w_out.astype(BF16), norm2_g,
        w_router.T, router_bias, w_shared_up_gate.astype(BF16), w_shared_down.astype(BF16), S)
    R = EXPERT_TILE
    cnt = counts[:, 0].astype(I32)
    ustart = jnp.concatenate([jnp.zeros((1,), I32), jnp.cumsum(cnt)])
    pcnt = (cnt + R - 1) // R * R
    pstart = jnp.cumsum(pcnt) - pcnt
    dest_p = _dests(eidx_t, rank_t, pstart)
    n_rows = (N * TOP_K + E * (R - 1) + R - 1) // R * R
    xg = _sc_scatter_rows(h2p, dest_p, n_rows + (X_SLOTS - 1) * R)
    y = _experts(ustart, pstart, xg, w_expert_up_gate, w_expert_down, n_rows, R)
    yg = _sc_gather_rows(y, dest_p)
    out = _combine(yg, w_t, base, mod, S)
    return out.reshape(B, S, D)


def kernel(x, c, positions, w_ada, b_ada, norm1_g, w_in, w_gate, b_gate, q_norm_g, k_norm_g, lambda_q1, lambda_k1, lambda_q2, lambda_k2, subln_g, rwkv_mu, w_decay0, w_decay2, a0, a2, g2, k_k, k_a, r_k, ln_x_w, ln_x_b, w_branch_a, w_branch_b, w_out, norm2_g, w_router, router_bias, w_expert_up_gate, w_expert_down, w_shared_up_gate, w_shared_down):
    for l in range(w_ada.shape[0]):
        x = _layer(x, c, positions, l, w_ada[l], b_ada[l], norm1_g[l], w_in[l], w_gate[l], b_gate[l],
                   q_norm_g[l], k_norm_g[l], lambda_q1[l], lambda_k1[l], lambda_q2[l], lambda_k2[l],
                   subln_g[l], rwkv_mu[l], w_decay0[l], w_decay2[l], a0[l], a2[l], g2[l], k_k[l],
                   k_a[l], r_k[l], ln_x_w[l], ln_x_b[l], w_branch_a[l], w_branch_b[l], w_out[l],
                   norm2_g[l], w_router[l], router_bias[l], w_expert_up_gate[l], w_expert_down[l],
                   w_shared_up_gate[l], w_shared_down[l])
    return x
```

```python
import functools
import math

import jax
import jax.numpy as jnp
from jax import lax
from jax.experimental import pallas as pl
from jax.experimental.pallas import tpu as pltpu
from jax.experimental.pallas import tpu_sc as plsc

F32 = jnp.float32
BF16 = jnp.bfloat16
I32 = jnp.int32
U32 = jnp.uint32
HI = lax.Precision.HIGHEST

CHUNK = 64
ROPE_THETA = 10000.0
NORM_EPS = 1e-6
SUBLN_EPS = 1e-5
DA_HEAD_DIM = 64
RWKV_HEAD = 64
GN_EPS = 64e-5
TOP_K = 8
N_GROUPS = 8
TOPK_GROUPS = 4
ROUTED_SCALE = 2.5
EXPERT_TILE = 256
X_SLOTS = 4
RWKV_CHUNK = 64
LANES = 128
SC_WINDOW = 128
NEG = -1e30
MAX_PLAIN_SCORE = 40.0
VMEM_LIMIT = 56 * 1024 * 1024


def _cp(*sem):
    return pltpu.CompilerParams(dimension_semantics=sem, vmem_limit_bytes=VMEM_LIMIT)


def _bdot(a, b):
    return jnp.dot(a.astype(BF16), b.astype(BF16), preferred_element_type=F32)


def _fdot(a, b):
    return jnp.dot(a, b, precision=HI, preferred_element_type=F32)


def _nt(a, b, precision=None):
    return lax.dot_general(a, b, (((1,), (1,)), ((), ())), precision=precision,
                           preferred_element_type=F32)


def _tn(a, b, precision=None):
    return lax.dot_general(a, b, (((0,), (0,)), ((), ())), precision=precision,
                           preferred_element_type=F32)


def _pack_halves(x):
    c = x.shape[1] // 2
    lo = lax.bitcast_convert_type(x[:, :c], U32)
    hi = lax.bitcast_convert_type(x[:, c:], U32)
    return (hi & jnp.uint32(0xFFFF0000)) | (lo >> 16)


def _unpack_halves(w):
    lo = lax.bitcast_convert_type(w << 16, F32)
    hi = lax.bitcast_convert_type(w & jnp.uint32(0xFFFF0000), F32)
    return lo, hi


def _sigmoid(x):
    return 1.0 / (1.0 + jnp.exp(-x))


def _silu(x):
    return x * _sigmoid(x)


def _ada_kernel(c_ref, w_ref, b_ref, o_ref):
    o_ref[...] = _fdot(_silu(c_ref[...]), w_ref[...]) + b_ref[...]


def _adaln(c, w_ada, b_ada):
    B, D = c.shape
    rows = -(-B // 8) * 8
    cpad = jnp.zeros((rows, D), F32).at[:B].set(c)
    n_out = w_ada.shape[1]
    out = pl.pallas_call(
        _ada_kernel,
        grid=(n_out // D,),
        in_specs=[pl.BlockSpec((rows, D), lambda j: (0, 0)),
                  pl.BlockSpec((D, D), lambda j: (0, j)),
                  pl.BlockSpec((1, D), lambda j: (0, j))],
        out_specs=pl.BlockSpec((rows, D), lambda j: (0, j)),
        out_shape=jax.ShapeDtypeStruct((rows, n_out), F32),
        compiler_params=_cp("arbitrary"),
        name="adaln",
    )(cpad, w_ada, b_ada.reshape(1, n_out))
    return out[:B].reshape(B, n_out // D, D)


def _mixer_in_kernel(x_ref, xprev_ref, mod_ref, g_ref, w_ref, bg_ref, pos_ref, invf_ref, qg_ref, kg_ref,
                     mu_ref, w0_ref, w2_ref, a0_ref, a2_ref, g2_ref,
                     qn_ref, kn_ref, v_ref, r_ref, k_ref, vr_ref, a_ref, ld_ref, gr_ref, gate_ref,
                     *, da_width, rw_cols, rw_width, q_scale, tiles_per_seq):
    tm = x_ref.shape[0]

    def modulated(x):
        y = x * lax.rsqrt(jnp.mean(x * x, axis=-1, keepdims=True) + NORM_EPS) * g_ref[...]
        return (y * (1.0 + mod_ref[1:2, :]) + mod_ref[0:1, :]).astype(BF16)

    def proj(hb, c0, width, step=512):
        parts = [jnp.dot(hb, w_ref[:, c0 + o:c0 + min(o + step, width)], preferred_element_type=F32)
                 for o in range(0, width, step)]
        return parts[0] if len(parts) == 1 else jnp.concatenate(parts, axis=1)

    h = modulated(x_ref[...])

    lane = lax.broadcasted_iota(I32, (tm, LANES), 1)
    first = lane < DA_HEAD_DIM
    lo_half = (lane & (DA_HEAD_DIM - 1)) < DA_HEAD_DIM // 2
    ang = pos_ref[...].astype(F32) * invf_ref[...]
    cos = jnp.cos(ang)
    sin = jnp.sin(ang)
    sin = jnp.where(lo_half, -sin, sin)
    for c0, dst, gn_ref, mult in ((0, qn_ref, qg_ref, q_scale), (da_width, kn_ref, kg_ref, 1.0)):
        raw = proj(h, c0, da_width)
        for blk in range(da_width // LANES):
            x = raw[:, blk * LANES:(blk + 1) * LANES]
            xx = x * x
            s_first = jnp.sum(jnp.where(first, xx, 0.0), axis=-1, keepdims=True)
            s_second = jnp.sum(jnp.where(first, 0.0, xx), axis=-1, keepdims=True)
            ms = jnp.where(first, s_first, s_second) * (1.0 / DA_HEAD_DIM)
            xn = x * lax.rsqrt(ms + NORM_EPS) * gn_ref[...]
            rot = jnp.where(lo_half, pltpu.roll(xn, LANES - DA_HEAD_DIM // 2, axis=1),
                            pltpu.roll(xn, DA_HEAD_DIM // 2, axis=1))
            dst[:, blk * LANES:(blk + 1) * LANES] = ((xn * cos + rot * sin) * mult).astype(dst.dtype)
    v_ref[...] = proj(h, 2 * da_width, da_width).astype(v_ref.dtype)

    c_rw = 3 * da_width
    p = proj(h, c_rw, rw_cols)
    p_before = proj(modulated(xprev_ref[...]), c_rw, rw_cols)
    seq_start = (pl.program_id(0) % tiles_per_seq) == 0
    last_prev = jnp.where(seq_start, 0.0, p_before[7:8, :])
    rowi = lax.broadcasted_iota(I32, p.shape, 0)
    prev = jnp.where(rowi == 0, last_prev, pltpu.roll(p, 1, axis=0))
    xs = p + (prev - p) * mu_ref[...]
    width = rw_width
    r_ref[...] = xs[:, 0:width]
    k_ref[...] = xs[:, width:2 * width]
    vr_ref[...] = xs[:, 2 * width:3 * width]
    xwa = xs[:, 3 * width:3 * width + LANES]
    xg = xs[:, 3 * width + LANES:]
    z = w0_ref[...] + _bdot(jnp.tanh(xwa), w2_ref[...])
    w = -(jnp.maximum(-z, 0.0) + jnp.log(1.0 + jnp.exp(-jnp.abs(z)))) - 0.5
    ld_ref[...] = -jnp.exp(w)
    a_ref[...] = _sigmoid(a0_ref[...] + _bdot(xwa, a2_ref[...]))
    gr_ref[...] = _bdot(_sigmoid(xg), g2_ref[...])

    gate_ref[...] = _sigmoid(proj(h, c_rw + rw_cols, gate_ref.shape[1]) + bg_ref[...]).astype(gate_ref.dtype)


def _mixer_in(x2, pos2, mod, norm1_g, w_cat, b_gate, q_norm_g, k_norm_g, mu, w_decay0, w_decay2, a0, a2, g2,
              S, da_width, rw_width, tm=256):
    N, D = x2.shape
    n_gate = b_gate.shape[0]
    rw_cols = mu.shape[0]
    tpb = S // tm
    d = DA_HEAD_DIM
    inv_freq = 1.0 / (ROPE_THETA ** (jnp.arange(0, d, 2, dtype=F32) / d))
    invf = jnp.tile(inv_freq, LANES // (d // 2)).reshape(1, LANES)
    dl, al = w_decay2.shape[0], a2.shape[0]
    assert dl + al == LANES and g2.shape[0] == LANES
    w2p = jnp.zeros((LANES, rw_width), F32).at[:dl].set(w_decay2)
    a2p = jnp.zeros((LANES, rw_width), F32).at[dl:].set(a2)
    kern = functools.partial(_mixer_in_kernel, da_width=da_width, rw_cols=rw_cols, rw_width=rw_width,
                             q_scale=d ** -0.5 * math.log2(math.e),
                             tiles_per_seq=tpb)
    row = lambda w: pl.BlockSpec((tm, w), lambda i: (i, 0))
    vec = lambda n: pl.BlockSpec((1, n), lambda i: (0, 0))
    mat = pl.BlockSpec((LANES, rw_width), lambda i: (0, 0))
    f32 = lambda w: jax.ShapeDtypeStruct((N, w), F32)
    bf16 = lambda w: jax.ShapeDtypeStruct((N, w), BF16)
    return pl.pallas_call(
        kern,
        grid=(N // tm,),
        in_specs=[row(D),
                  pl.BlockSpec((8, D), lambda i: (jnp.maximum(i * (tm // 8) - 1, 0), 0)),
                  pl.BlockSpec((None, 6, D), lambda i: (i // tpb, 0, 0)),
                  vec(D),
                  pl.BlockSpec(w_cat.shape, lambda i: (0, 0)),
                  vec(n_gate),
                  pl.BlockSpec((tm, 1), lambda i: (i, 0)),
                  vec(LANES), vec(LANES), vec(LANES),
                  vec(rw_cols), vec(rw_width), mat, vec(rw_width), mat, mat],
        out_specs=[row(da_width)] * 3 + [row(rw_width)] * 6 + [row(n_gate)],
        out_shape=[bf16(da_width)] * 3 + [f32(rw_width)] * 6 + [bf16(n_gate)],
        compiler_params=_cp("parallel"),
        name="mixer_in",
    )(x2, x2, mod, norm1_g.reshape(1, D), w_cat, b_gate.reshape(1, n_gate), pos2, invf,
      jnp.tile(q_norm_g, 2).reshape(1, LANES), jnp.tile(k_norm_g, 2).reshape(1, LANES),
      mu.reshape(1, rw_cols), w_decay0.reshape(1, rw_width), w2p, a0.reshape(1, rw_width), a2p, g2)


def _attn_kernel(flag_ref, q_ref, k_ref, v_ref, lam_ref, sg_ref, o_ref, qz_ref, m_ref, l_ref, lp_ref, acc_ref,
                 *, tq, lambda_init):
    i = pl.program_id(2)
    lane = lax.broadcasted_iota(I32, (tq, LANES), 1)
    q = q_ref[...]
    zero = jnp.zeros_like(q)
    qz_ref[0:tq, :] = jnp.where(lane < DA_HEAD_DIM, q, zero)
    qz_ref[tq:, :] = jnp.where(lane >= DA_HEAD_DIM, q, zero)
    acc_ref[...] = jnp.zeros_like(acc_ref)
    bounded = flag_ref[0] == 1

    def scores(j, masked):
        off = pl.multiple_of(j * tq, tq)
        s = _nt(qz_ref[...], k_ref[pl.ds(off, tq), :])
        if masked:
            row = lax.broadcasted_iota(I32, s.shape, 0)
            col = lax.broadcasted_iota(I32, s.shape, 1)
            s = jnp.where((col // CHUNK) <= ((row & (tq - 1)) // CHUNK), s, NEG)
        return s, off

    def plain_step(j, masked):
        s, off = scores(j, masked)
        pr = jnp.exp2(s)
        part = pr[:, 0:LANES]
        for cblk in range(1, tq // LANES):
            part = part + pr[:, cblk * LANES:(cblk + 1) * LANES]
        lp_ref[...] += part
        acc_ref[...] += jnp.dot(pr.astype(BF16), v_ref[pl.ds(off, tq), :], preferred_element_type=F32)

    def online_step(j, masked):
        s, off = scores(j, masked)
        m_old = m_ref[...]
        m_new = jnp.maximum(m_old, jnp.max(s, axis=-1, keepdims=True))
        alpha = jnp.exp2(m_old - m_new)
        pr = jnp.exp2(s - m_new)
        l_ref[...] = alpha * l_ref[...] + jnp.sum(pr, axis=-1, keepdims=True)
        acc_ref[...] = alpha * acc_ref[...] + jnp.dot(pr.astype(BF16), v_ref[pl.ds(off, tq), :],
                                                      preferred_element_type=F32)
        m_ref[...] = m_new

    def run(step):
        def body(j, carry):
            step(j, False)
            return carry
        lax.fori_loop(0, i, body, 0)
        step(i, True)

    @pl.when(bounded)
    def _():
        lp_ref[...] = jnp.zeros_like(lp_ref)
        run(plain_step)
        l_ref[...] = jnp.sum(lp_ref[...], axis=-1, keepdims=True)

    @pl.when(jnp.logical_not(bounded))
    def _():
        m_ref[...] = jnp.full_like(m_ref, NEG)
        l_ref[...] = jnp.zeros_like(l_ref)
        run(online_step)

    lv = lam_ref[...]
    lam = (jnp.exp(jnp.sum(lv[0:1] * lv[1:2], keepdims=True))
           - jnp.exp(jnp.sum(lv[2:3] * lv[3:4], keepdims=True)) + lambda_init)
    o1 = acc_ref[0:tq, :] / l_ref[0:tq, :]
    o2 = acc_ref[tq:, :] / l_ref[tq:, :]
    o = o1 - lam * o2
    o = o * lax.rsqrt(jnp.mean(o * o, axis=-1, keepdims=True) + SUBLN_EPS) * sg_ref[...]
    o_ref[...] = (o * (1.0 - lambda_init)).astype(o_ref.dtype)


def _diff_attention(qn, kn, v, score_bound, lam_vecs, subln_g, B, S, lambda_init, tq=512):
    W = qn.shape[1]
    H = W // LANES
    q3 = qn.reshape(B, S, W)
    k3 = kn.reshape(B, S, W)
    v3 = v.reshape(B, S, W)
    flag = (score_bound <= MAX_PLAIN_SCORE).astype(I32).reshape(1)
    qblk = pl.BlockSpec((None, tq, LANES), lambda b, h, i, f: (b, i, h))
    kvblk = pl.BlockSpec((None, S, LANES), lambda b, h, i, f: (b, 0, h))
    grid_spec = pltpu.PrefetchScalarGridSpec(
        num_scalar_prefetch=1,
        grid=(B, H, S // tq),
        in_specs=[qblk, kvblk, kvblk,
                  pl.BlockSpec((4, DA_HEAD_DIM), lambda b, h, i, f: (0, 0)),
                  pl.BlockSpec((1, LANES), lambda b, h, i, f: (0, 0))],
        out_specs=qblk,
        scratch_shapes=[pltpu.VMEM((2 * tq, LANES), BF16),
                        pltpu.VMEM((2 * tq, 1), F32),
                        pltpu.VMEM((2 * tq, 1), F32),
                        pltpu.VMEM((2 * tq, LANES), F32),
                        pltpu.VMEM((2 * tq, LANES), F32)],
    )
    out = pl.pallas_call(
        functools.partial(_attn_kernel, tq=tq, lambda_init=lambda_init),
        grid_spec=grid_spec,
        out_shape=jax.ShapeDtypeStruct((B, S, W), BF16),
        compiler_params=_cp("parallel", "parallel", "arbitrary"),
        name="diff_attn",
    )(flag, q3, k3, v3, lam_vecs, subln_g.reshape(1, LANES))
    return out.reshape(B * S, W)


def _stackmask(m):
    lane = lax.broadcasted_iota(I32, m.shape, 1)
    z = jnp.zeros_like(m)
    return jnp.concatenate([jnp.where(lane < RWKV_HEAD, m, z), jnp.where(lane >= RWKV_HEAD, m, z)], axis=0)


def _pair_sum(x, first):
    s1 = jnp.sum(jnp.where(first, x, 0.0), axis=-1, keepdims=True)
    s2 = jnp.sum(jnp.where(first, 0.0, x), axis=-1, keepdims=True)
    return jnp.where(first, s1, s2)


def _rwkv_scan_kernel(r_ref, k_ref, v_ref, a_ref, ld_ref, g_ref, kk_ref, ka_ref, rk_ref, lnw_ref, lnb_ref,
                      o_ref, s_ref, *, L):
    tm, W = r_ref.shape
    n_chunks = tm // L
    n_pairs = W // LANES
    hd = RWKV_HEAD
    bf = lambda t: t.astype(BF16)

    @pl.when(pl.program_id(1) == 0)
    def _():
        s_ref[...] = jnp.zeros_like(s_ref)

    row = lax.broadcasted_iota(I32, (tm, tm), 0)
    col = lax.broadcasted_iota(I32, (tm, tm), 1)
    tri = jnp.where(jnp.logical_and(col <= row, (col // L) == (row // L)), 1.0, 0.0).astype(BF16)
    ld = ld_ref[...]
    ld_hi = bf(ld)
    rem = ld - ld_hi.astype(F32)
    ld_mid = bf(rem)
    ld_lo = bf(rem - ld_mid.astype(F32))
    c = (jnp.dot(tri, ld_hi, preferred_element_type=F32) + jnp.dot(tri, ld_mid, preferred_element_type=F32)
         + jnp.dot(tri, ld_lo, preferred_element_type=F32))
    ec = jnp.exp(c)
    eci = jnp.exp(-c)
    ecm = jnp.exp(c - ld)
    r = r_ref[...]
    k = k_ref[...]
    v = v_ref[...]
    a = a_ref[...]
    kkr = k * kk_ref[...]
    kmod = k * (1.0 + (a - 1.0) * ka_ref[...])
    brk = r * kmod * rk_ref[...]

    lane = lax.broadcasted_iota(I32, (L, LANES), 1)
    rowl = lax.broadcasted_iota(I32, (L, LANES), 0)
    first = lane < hd
    lane_h = lane & (hd - 1)
    strict = lane_h < rowl
    incl = lane_h <= rowl
    eye = jnp.where(lane_h == rowl, 1.0, 0.0)

    chains = [(ch, p) for ch in range(n_chunks) for p in range(n_pairs)]
    rsl = lambda ch: slice(ch * L, (ch + 1) * L)
    csl = lambda p: slice(p * LANES, (p + 1) * LANES)
    fdot = lambda x, y: jnp.dot(x, y, preferred_element_type=F32)
    at, bt, kt, rt, vh, g_l = {}, {}, {}, {}, {}, {}
    for c_ in chains:
        ch, p = c_
        rs, cs = rsl(ch), csl(p)
        kkh = kkr[rs, cs]
        kkh = kkh / jnp.maximum(jnp.sqrt(_pair_sum(kkh * kkh, first)), 1e-12)
        vh[c_] = v[rs, cs]
        g_l[c_] = ec[ch * L + L - 1:ch * L + L, cs]
        at[c_] = -kkh * ecm[rs, cs]
        bt[c_] = kkh * a[rs, cs] * eci[rs, cs]
        kt[c_] = kmod[rs, cs] * eci[rs, cs]
        rt[c_] = r[rs, cs] * ec[rs, cs]
    gm = {c_: _nt(bf(jnp.concatenate([at[c_], rt[c_]], axis=0)),
                  jnp.concatenate([_stackmask(bf(bt[c_])), _stackmask(bf(kt[c_]))], axis=0)) for c_ in chains}
    a_ab = {c_: jnp.where(strict, gm[c_][:L, :LANES], 0.0) for c_ in chains}
    vsm = {c_: _stackmask(bf(vh[c_])) for c_ in chains}
    cmat = {c_: fdot(bf(jnp.where(strict, gm[c_][:L, LANES:], 0.0)), vsm[c_]) for c_ in chains}
    t_inv = {c_: eye + a_ab[c_] for c_ in chains}
    pw = {c_: bf(a_ab[c_]) for c_ in chains}
    for _ in range(int(math.log2(L)) - 1):
        pw = {c_: bf(fdot(pw[c_], _stackmask(pw[c_]))) for c_ in chains}
        t_inv = {c_: t_inv[c_] + fdot(pw[c_], _stackmask(bf(t_inv[c_]))) for c_ in chains}
    zz = {c_: fdot(bf(t_inv[c_]), jnp.concatenate([_stackmask(bf(at[c_])), _stackmask(bf(cmat[c_]))], axis=1))
          for c_ in chains}
    qy = {c_: fdot(bf(jnp.where(incl, gm[c_][L:, :LANES], 0.0)),
                   jnp.concatenate([_stackmask(bf(zz[c_][:, :LANES])), _stackmask(bf(zz[c_][:, LANES:]))], axis=1))
          for c_ in chains}
    y0 = {c_: qy[c_][:, LANES:] + fdot(bf(jnp.where(incl, gm[c_][L:, LANES:], 0.0)), vsm[c_]) for c_ in chains}
    qa = {c_: bf(jnp.concatenate([rt[c_] + qy[c_][:, :LANES], zz[c_][:, :LANES]], axis=0)) for c_ in chains}
    bkg = {c_: bf(jnp.concatenate([bt[c_] * g_l[c_], kt[c_] * g_l[c_]], axis=0)) for c_ in chains}

    lane_s = lax.broadcasted_iota(I32, (hd, LANES), 1)
    sp = [s_ref[p] for p in range(n_pairs)]
    for ch in range(n_chunks):
        rs = rsl(ch)
        yw = [_nt(qa[ch, p], _stackmask(bf(sp[p]))) for p in range(n_pairs)]
        upd = [_tn(bf(jnp.concatenate([yw[p][L:] + zz[ch, p][:, LANES:], vh[ch, p]], axis=0)), bkg[ch, p])
               for p in range(n_pairs)]
        for p in range(n_pairs):
            cs = csl(p)
            sp[p] = sp[p] * g_l[ch, p] + jnp.where(lane_s < hd, upd[p][:hd], upd[p][hd:])
            y = yw[p][:L] + y0[ch, p]
            mean = _pair_sum(y, first) * (1.0 / hd)
            yc = y - mean
            var = _pair_sum(yc * yc, first) * (1.0 / hd)
            yn = yc * lax.rsqrt(var + GN_EPS) * lnw_ref[:, cs] + lnb_ref[:, cs]
            bonus = _pair_sum(brk[rs, cs], first) * vh[ch, p]
            o_ref[rs, cs] = ((yn + bonus) * g_ref[rs, cs]).astype(o_ref.dtype)
    for p in range(n_pairs):
        s_ref[p] = sp[p]


def _rwkv_scan(r, k, v, a, ld, g, k_k, k_a, r_k, ln_w, ln_b, L=RWKV_CHUNK, tm=256):
    B, S, W = r.shape
    seq = pl.BlockSpec((None, tm, W), lambda b, c: (b, c, 0))
    vec = pl.BlockSpec((1, W), lambda b, c: (0, 0))
    return pl.pallas_call(
        functools.partial(_rwkv_scan_kernel, L=L),
        grid=(B, S // tm),
        in_specs=[seq] * 6 + [vec] * 5,
        out_specs=seq,
        out_shape=jax.ShapeDtypeStruct((B, S, W), BF16),
        scratch_shapes=[pltpu.VMEM((W // LANES, RWKV_HEAD, LANES), F32)],
        compiler_params=_cp("parallel", "arbitrary"),
        name="rwkv_scan",
    )(r, k, v, a, ld, g, k_k.reshape(1, W), k_a.reshape(1, W), r_k.reshape(1, W),
      ln_w.reshape(1, W), ln_b.reshape(1, W))


def _post_kernel(attn_ref, rw_ref, gate_ref, x_ref, mod_ref, wa_ref, wb_ref, wo_ref, g2_ref, wrh_ref, wrm_ref,
                 sug_ref, sd_ref, base_ref, h2p_ref, lg_ref):
    D = x_ref.shape[1]
    ya = jnp.dot(attn_ref[...], wa_ref[...], preferred_element_type=F32)
    yb = jnp.dot(rw_ref[...], wb_ref[...], preferred_element_type=F32)
    m = gate_ref[:, 0:D] * ya + gate_ref[:, D:] * yb
    x1 = x_ref[...] + mod_ref[2:3, :] * jnp.dot(m.astype(BF16), wo_ref[...], preferred_element_type=F32)
    y = x1 * lax.rsqrt(jnp.mean(x1 * x1, axis=-1, keepdims=True) + NORM_EPS) * g2_ref[...]
    h2 = y * (1.0 + mod_ref[4:5, :]) + mod_ref[3:4, :]
    hb = h2.astype(BF16)
    hm = (h2 - hb.astype(F32)).astype(BF16)
    lg_ref[...] = _nt(wrh_ref[...], hb) + _nt(wrh_ref[...], hm) + _nt(wrm_ref[...], hb)
    packed = _pack_halves(hb.astype(F32))
    half = packed.shape[1] // 2
    h2p_ref[0] = packed[:, :half]
    h2p_ref[1] = packed[:, half:]
    F = sd_ref.shape[0]
    gu = jnp.dot(hb, sug_ref[...], preferred_element_type=F32)
    shared = jnp.dot((_silu(gu[:, :F]) * gu[:, F:]).astype(BF16), sd_ref[...], preferred_element_type=F32)
    base_ref[...] = x1 + mod_ref[5:6, :] * shared


def _post(attn, rw, gate, x2, mod, wa, wb, wo, norm2_g, w_router_t, sug, sd, S, tm=256):
    N, D = x2.shape
    E = w_router_t.shape[0]
    wr_hi = w_router_t.astype(BF16)
    wr_mid = (w_router_t - wr_hi.astype(F32)).astype(BF16)
    tpb = S // tm
    row = lambda w: pl.BlockSpec((tm, w), lambda i: (i, 0))
    full = lambda a: pl.BlockSpec(a.shape, lambda i: (0, 0))
    return pl.pallas_call(
        _post_kernel,
        grid=(N // tm,),
        in_specs=[row(attn.shape[1]), row(rw.shape[1]), row(gate.shape[1]), row(D),
                  pl.BlockSpec((None, 6, D), lambda i: (i // tpb, 0, 0)),
                  full(wa), full(wb), full(wo), pl.BlockSpec((1, D), lambda i: (0, 0)), full(wr_hi), full(wr_mid),
                  full(sug), full(sd)],
        out_specs=[row(D), pl.BlockSpec((2, tm, D // 4), lambda i: (0, i, 0)), pl.BlockSpec((E, tm), lambda i: (0, i))],
        out_shape=[jax.ShapeDtypeStruct((N, D), F32), jax.ShapeDtypeStruct((2, N, D // 4), U32),
                   jax.ShapeDtypeStruct((E, N), F32)],
        compiler_params=_cp("parallel"),
        name="post_mixer",
    )(attn, rw, gate, x2, mod, wa, wb, wo, norm2_g.reshape(1, D), wr_hi, wr_mid, sug, sd)


def _first_argmax(x, idx, sentinel):
    m = jnp.max(x, axis=0, keepdims=True)
    return m, jnp.min(jnp.where(x == m, idx, sentinel), axis=0, keepdims=True)


def _route_kernel(lg_ref, bias_ref, e_ref, w_ref):
    E, T = lg_ref.shape
    gsz = E // N_GROUPS
    scores = _sigmoid(lg_ref[...])
    biased = scores + bias_ref[...]
    ig = lax.broadcasted_iota(I32, (gsz, T), 0)
    gs = []
    for g in range(N_GROUPS):
        blk = biased[g * gsz:(g + 1) * gsz, :]
        m1, i1 = _first_argmax(blk, ig, gsz)
        m2 = jnp.max(jnp.where(ig == i1, -jnp.inf, blk), axis=0, keepdims=True)
        gs.append(m1 + m2)
    gsc = jnp.concatenate(gs, axis=0)
    i8 = lax.broadcasted_iota(I32, (N_GROUPS, T), 0)
    chosen = jnp.zeros((N_GROUPS, T), F32)
    for _ in range(TOPK_GROUPS):
        _, gi = _first_argmax(gsc, i8, N_GROUPS)
        hit = i8 == gi
        chosen = jnp.where(hit, 1.0, chosen)
        gsc = jnp.where(hit, -jnp.inf, gsc)
    masked = jnp.concatenate(
        [jnp.where(chosen[g:g + 1, :] > 0.0, biased[g * gsz:(g + 1) * gsz, :], -jnp.inf)
         for g in range(N_GROUPS)], axis=0)
    ie = lax.broadcasted_iota(I32, (E, T), 0)
    idxs, wts = [], []
    for _ in range(TOP_K):
        _, ei = _first_argmax(masked, ie, E)
        hit = ie == ei
        idxs.append(ei)
        wts.append(jnp.sum(jnp.where(hit, scores, 0.0), axis=0, keepdims=True))
        masked = jnp.where(hit, -jnp.inf, masked)
    wt = jnp.concatenate(wts, axis=0)
    e_ref[...] = jnp.concatenate(idxs, axis=0)
    w_ref[...] = wt / jnp.sum(wt, axis=0, keepdims=True) * ROUTED_SCALE


def _route(logits_t, router_bias, T=512):
    E, N = logits_t.shape
    blk = pl.BlockSpec((TOP_K, T), lambda i: (0, i))
    return pl.pallas_call(
        _route_kernel,
        grid=(N // T,),
        in_specs=[pl.BlockSpec((E, T), lambda i: (0, i)), pl.BlockSpec((E, 1), lambda i: (0, 0))],
        out_specs=[blk, blk],
        out_shape=[jax.ShapeDtypeStruct((TOP_K, N), I32), jax.ShapeDtypeStruct((TOP_K, N), F32)],
        compiler_params=_cp("parallel"),
        name="route",
    )(logits_t, router_bias.reshape(E, 1))


def _rank_kernel(e_ref, rank_ref, cnt_ref, carry_ref, *, n_experts):
    T = e_ref.shape[1]

    @pl.when(pl.program_id(0) == 0)
    def _():
        carry_ref[...] = jnp.zeros_like(carry_ref)

    ie = lax.broadcasted_iota(I32, (n_experts, T), 0)
    e = e_ref[...]
    hits = [ie == e[kk:kk + 1, :] for kk in range(TOP_K)]
    onehot = jnp.zeros((n_experts, T), F32)
    for hsel in hits:
        onehot = onehot + hsel.astype(F32)
    tr = lax.broadcasted_iota(I32, (T, T), 0)
    tc = lax.broadcasted_iota(I32, (T, T), 1)
    before = (tr < tc).astype(BF16)
    base = _bdot(onehot, before) + carry_ref[:, 0:1]
    rank_ref[...] = jnp.concatenate(
        [jnp.sum(jnp.where(hsel, base, 0.0), axis=0, keepdims=True) for hsel in hits], axis=0).astype(I32)
    carry_ref[...] = carry_ref[...] + jnp.sum(onehot, axis=1, keepdims=True)
    cnt_ref[...] = carry_ref[...]


def _ranks(eidx_t, n_experts, T=512):
    N = eidx_t.shape[1]
    blk = pl.BlockSpec((TOP_K, T), lambda i: (0, i))
    cnt = pl.BlockSpec((n_experts, LANES), lambda i: (0, 0))
    return pl.pallas_call(
        functools.partial(_rank_kernel, n_experts=n_experts),
        grid=(N // T,),
        in_specs=[blk],
        out_specs=[blk, cnt],
        out_shape=[jax.ShapeDtypeStruct((TOP_K, N), I32), jax.ShapeDtypeStruct((n_experts, LANES), F32)],
        scratch_shapes=[pltpu.VMEM((n_experts, LANES), F32)],
        compiler_params=_cp("arbitrary"),
        name="ranks",
    )(eidx_t)


def _dest_kernel(e_ref, rank_ref, start_ref, d_ref):
    E = start_ref.shape[0]
    T = e_ref.shape[1]
    ie = lax.broadcasted_iota(I32, (E, T), 0)
    e = e_ref[...]
    start = start_ref[:, 0:1]
    rows = [jnp.sum(jnp.where(ie == e[kk:kk + 1, :], start, 0.0), axis=0, keepdims=True) for kk in range(TOP_K)]
    d_ref[...] = jnp.concatenate(rows, axis=0).astype(I32) + rank_ref[...]


def _dests(eidx_t, rank_t, pstart, T=512):
    N = eidx_t.shape[1]
    E = pstart.shape[0]
    blk = pl.BlockSpec((TOP_K, T), lambda i: (0, i))
    return pl.pallas_call(
        _dest_kernel,
        grid=(N // T,),
        in_specs=[blk, blk, pl.BlockSpec((E, LANES), lambda i: (0, 0))],
        out_specs=blk,
        out_shape=jax.ShapeDtypeStruct((TOP_K, N), I32),
        compiler_params=_cp("parallel"),
        name="dests",
    )(eidx_t, rank_t, jnp.broadcast_to(pstart.astype(F32)[:, None], (E, LANES)))


def _expert_kernel(us_ref, ps_ref, x_hbm, wug_ref, wd_ref, y_hbm,
                   wug_bf, wd_bf, xbuf, ybuf, cnt_ref, xsem, ysem, *, n_rows):
    e = pl.program_id(0)
    _, _, R, Ch = xbuf.shape
    C = 2 * Ch
    F = wd_ref.shape[0]

    def x_copy(slot, half, row):
        return pltpu.make_async_copy(x_hbm.at[half, pl.ds(row, R)], xbuf.at[slot, half], xsem.at[slot])

    def y_copy(slot, half, row):
        return pltpu.make_async_copy(ybuf.at[slot, half], y_hbm.at[half, pl.ds(row, R)], ysem.at[slot])

    def start(copy, slot, row):
        for half in range(2):
            copy(slot, half, row).start(priority=1)

    def wait(copy, slot):
        for half in range(2):
            copy(slot, half, 0).wait()

    @pl.when(e == 0)
    def _():
        cnt_ref[0] = 0
        for t in range(X_SLOTS - 1):
            start(x_copy, t, t * R)
        ybuf[...] = jnp.zeros_like(ybuf)
        for slot in range(2):
            start(y_copy, slot, n_rows + slot * R)

    wug_bf[...] = wug_ref[...].astype(BF16)
    wd_bf[...] = wd_ref[...].astype(BF16)
    n_valid = us_ref[e + 1] - us_ref[e]
    pbase = ps_ref[e]
    n_tiles = lax.shift_right_logical(n_valid + (R - 1), int(math.log2(R)))
    rowid = lax.broadcasted_iota(I32, (R, C), 0)

    def tile(i, carry):
        n_done = cnt_ref[0]
        slot = n_done & 1
        xslot = lax.rem(n_done, X_SLOTS)
        ahead = n_done + (X_SLOTS - 1)
        start(x_copy, lax.rem(ahead, X_SLOTS), pl.multiple_of(ahead * R, R))
        wait(x_copy, xslot)
        x = jnp.concatenate([xbuf[xslot, 0], xbuf[xslot, 1]], axis=1)
        x = jnp.where(rowid < n_valid - i * R, x, jnp.uint32(0))
        lo, hi = _unpack_halves(x)
        gu = (jnp.dot(lo.astype(BF16), wug_bf[0:C, :], preferred_element_type=F32)
              + jnp.dot(hi.astype(BF16), wug_bf[C:, :], preferred_element_type=F32))
        hid = (_silu(gu[:, :F]) * gu[:, F:]).astype(BF16)
        y = jnp.dot(hid, wd_bf[...], preferred_element_type=F32)
        wait(y_copy, slot)
        packed = _pack_halves(y.astype(BF16).astype(F32))
        ybuf[slot, 0] = packed[:, :Ch]
        ybuf[slot, 1] = packed[:, Ch:]
        start(y_copy, slot, pl.multiple_of(pbase + i * R, R))
        cnt_ref[0] = n_done + 1
        return carry

    lax.fori_loop(0, n_tiles, tile, 0)

    @pl.when(e == pl.num_programs(0) - 1)
    def _():
        for t in range(X_SLOTS - 1):
            wait(x_copy, lax.rem(cnt_ref[0] + t, X_SLOTS))
        for slot in range(2):
            wait(y_copy, slot)
        ybuf[0] = jnp.zeros((2, R, Ch), U32)
        first = lax.shift_right_logical(pbase + n_tiles * R, int(math.log2(R)))
        n_left = n_rows // R - first

        def fill(t, carry):
            start(y_copy, 0, pl.multiple_of((first + t) * R, R))
            return carry

        def drain(t, carry):
            wait(y_copy, 0)
            return carry

        lax.fori_loop(0, n_left, fill, 0)
        lax.fori_loop(0, n_left, drain, 0)


def _experts(ustart, pstart, xg, w_ug, w_d, n_rows, R):
    _, _, Ch = xg.shape
    E, D, F2 = w_ug.shape
    F = w_d.shape[1]
    grid_spec = pltpu.PrefetchScalarGridSpec(
        num_scalar_prefetch=2,
        grid=(E,),
        in_specs=[pl.BlockSpec(memory_space=pl.ANY),
                  pl.BlockSpec((None, D, F2), lambda e, us, ps: (e, 0, 0)),
                  pl.BlockSpec((None, F, D), lambda e, us, ps: (e, 0, 0))],
        out_specs=pl.BlockSpec(memory_space=pl.ANY),
        scratch_shapes=[pltpu.VMEM((D, F2), BF16), pltpu.VMEM((F, D), BF16),
                        pltpu.VMEM((X_SLOTS, 2, R, Ch), U32), pltpu.VMEM((2, 2, R, Ch), U32), pltpu.SMEM((1,), I32),
                        pltpu.SemaphoreType.DMA((X_SLOTS,)), pltpu.SemaphoreType.DMA((2,))],
    )
    return pl.pallas_call(
        functools.partial(_expert_kernel, n_rows=n_rows),
        grid_spec=grid_spec,
        out_shape=jax.ShapeDtypeStruct((2, n_rows + 2 * R, Ch), U32),
        compiler_params=_cp("arbitrary"),
        name="experts",
    )(ustart, pstart, xg, w_ug, w_d)


def _sc_scatter_rows(src, idx, n_rows):
    H, N, C = src.shape
    K = idx.shape[0]
    per_row = N // SC_WINDOW
    mesh = plsc.VectorSubcoreMesh(core_axis_name="c", subcore_axis_name="s")

    @functools.partial(pl.kernel, out_type=jax.ShapeDtypeStruct((H, n_rows, C), src.dtype), mesh=mesh,
                       scratch_types=[])
    def scatter_kernel(x_hbm, i_hbm, o_hbm):
        for h in range(H):
            def body(x_vmem, i_vmem):
                pltpu.sync_copy(x_vmem, o_hbm.at[h].at[i_vmem.at[0]])

            pltpu.emit_pipeline(
                body,
                grid=(K * per_row,),
                in_specs=[pl.BlockSpec((SC_WINDOW, C), lambda i: (i % per_row, 0)),
                          pl.BlockSpec((1, SC_WINDOW), lambda i: (i // per_row, i % per_row))],
                out_specs=[],
                core_axis_name=("c", "s"),
                dimension_semantics=(pltpu.PARALLEL,),
            )(x_hbm.at[h], i_hbm)

    return scatter_kernel(src, idx)


def _sc_gather_rows(src, idx):
    H, _, C = src.shape
    K, N = idx.shape
    per_row = N // SC_WINDOW
    mesh = plsc.VectorSubcoreMesh(core_axis_name="c", subcore_axis_name="s")

    @functools.partial(pl.kernel, out_type=jax.ShapeDtypeStruct((H, K * N, C), src.dtype), mesh=mesh,
                       scratch_types=[])
    def gather_kernel(x_hbm, i_hbm, o_hbm):
        for h in range(H):
            def body(i_vmem, o_vmem):
                pltpu.sync_copy(x_hbm.at[h].at[i_vmem.at[0]], o_vmem)

            pltpu.emit_pipeline(
                body,
                grid=(K * per_row,),
                in_specs=[pl.BlockSpec((1, SC_WINDOW), lambda i: (i // per_row, i % per_row))],
                out_specs=[pl.BlockSpec((SC_WINDOW, C), lambda i: (i, 0))],
                core_axis_name=("c", "s"),
                dimension_semantics=(pltpu.PARALLEL,),
            )(i_hbm, o_hbm.at[h])

    return gather_kernel(src, idx)


def _combine_kernel(*refs):
    y_refs = refs[:2 * TOP_K]
    w_ref, base_ref, mod_ref = refs[2 * TOP_K:2 * TOP_K + 3]
    o_ref = refs[-1]
    T = base_ref.shape[0]
    tr = lax.broadcasted_iota(I32, (T, T), 0)
    tc = lax.broadcasted_iota(I32, (T, T), 1)
    wcol = _nt((tr == tc).astype(F32), w_ref[...], precision=HI)
    acc = [None] * 4
    for kk in range(TOP_K):
        wk = wcol[:, kk:kk + 1]
        for half in range(2):
            lo, hi = _unpack_halves(y_refs[2 * kk + half][...])
            for q, val in ((half, lo), (2 + half, hi)):
                acc[q] = val * wk if acc[q] is None else acc[q] + val * wk
    o_ref[...] = base_ref[...] + mod_ref[5:6, :] * jnp.concatenate(acc, axis=1)


def _combine(yg_parts, w_t, base, mod, S, T=256):
    N, D = base.shape
    tpb = S // T
    out = None
    first_tile = 0
    for yg in yg_parts:
        C = yg.shape[2]
        tiles = yg.shape[1] // (TOP_K * T)
        row = pl.BlockSpec((T, D), lambda i, o=first_tile: (o + i, 0))
        piece = lambda kk, half, n=tiles: pl.BlockSpec((None, T, C), lambda i: (half, kk * n + i, 0))
        in_specs = [piece(kk, half) for kk in range(TOP_K) for half in range(2)] + [
            pl.BlockSpec((TOP_K, T), lambda i, o=first_tile: (0, o + i)),
            row,
            pl.BlockSpec((None, 6, D), lambda i, o=first_tile: ((o + i) // tpb, 0, 0))]
        args = [yg] * (2 * TOP_K) + [w_t, base, mod]
        aliases = {}
        if out is not None:
            in_specs.append(pl.BlockSpec(memory_space=pl.ANY))
            args.append(out)
            aliases = {len(args) - 1: 0}
        out = pl.pallas_call(
            _combine_kernel,
            grid=(tiles,),
            in_specs=in_specs,
            out_specs=row,
            out_shape=jax.ShapeDtypeStruct((N, D), F32),
            input_output_aliases=aliases,
            compiler_params=_cp("parallel"),
            name="combine",
        )(*args)
        first_tile += tiles
    return out


def _layer(x, c, positions, layer_idx, w_ada, b_ada, norm1_g, w_in, w_gate, b_gate,
           q_norm_g, k_norm_g, lambda_q1, lambda_k1, lambda_q2, lambda_k2, subln_g,
           rwkv_mu, w_decay0, w_decay2, a0, a2, g2, k_k, k_a, r_k, ln_x_w, ln_x_b,
           w_branch_a, w_branch_b, w_out, norm2_g, w_router, router_bias,
           w_expert_up_gate, w_expert_down, w_shared_up_gate, w_shared_down):
    B, S, D = x.shape
    N = B * S
    E = w_router.shape[1]
    da_width = w_branch_a.shape[0]
    rw_width = w_branch_b.shape[0]
    lambda_init = 0.8 - 0.6 * math.exp(-0.3 * layer_idx)

    mod = _adaln(c, w_ada, b_ada)
    x2 = x.reshape(N, D)
    w_cat = jnp.concatenate([w_in, w_gate], axis=1).astype(BF16)
    qn, kn, v, r_, k_, v_, a_, ld_, g_, gate = _mixer_in(
        x2, positions.reshape(N, 1), mod, norm1_g, w_cat, b_gate, q_norm_g, k_norm_g,
        rwkv_mu, w_decay0, w_decay2, a0, a2, g2, S, da_width, rw_width)

    lam_vecs = jnp.stack([lambda_q1, lambda_k1, lambda_q2, lambda_k2])
    score_bound = 1.01 * DA_HEAD_DIM ** 0.5 * jnp.max(jnp.abs(q_norm_g)) * jnp.max(jnp.abs(k_norm_g))
    attn = _diff_attention(qn, kn, v, score_bound, lam_vecs, subln_g, B, S, lambda_init)

    seq = lambda t: t.reshape(B, S, rw_width)
    rw = _rwkv_scan(seq(r_), seq(k_), seq(v_), seq(a_), seq(ld_), seq(g_), k_k, k_a, r_k.reshape(-1),
                    ln_x_w, ln_x_b).reshape(N, rw_width)

    base, h2p, logits_t = _post(attn, rw, gate, x2, mod, w_branch_a.astype(BF16), w_branch_b.astype(BF16),
                                w_out.astype(BF16), norm2_g, w_router.T,
                                w_shared_up_gate.astype(BF16), w_shared_down.astype(BF16), S)

    eidx_t, w_t = _route(logits_t, router_bias)
    rank_t, counts = _ranks(eidx_t, E)
    R = EXPERT_TILE
    cnt = counts[:, 0].astype(I32)
    ustart = jnp.concatenate([jnp.zeros((1,), I32), jnp.cumsum(cnt)])
    pcnt = (cnt + R - 1) // R * R
    pstart = jnp.cumsum(pcnt) - pcnt
    dest_p = _dests(eidx_t, rank_t, pstart)
    n_rows = (N * TOP_K + E * (R - 1) + R - 1) // R * R
    xg = _sc_scatter_rows(h2p, dest_p, n_rows + (X_SLOTS - 1) * R)
    y = _experts(ustart, pstart, xg, w_expert_up_gate, w_expert_down, n_rows, R)
    half_n = N // 2
    yg_parts = [_sc_gather_rows(y, dest_p[:, lo:lo + half_n]) for lo in (0, half_n)]
    out = _combine(yg_parts, w_t, base, mod, S)
    return out.reshape(B, S, D)


def kernel(x, c, positions, w_ada, b_ada, norm1_g, w_in, w_gate, b_gate, q_norm_g, k_norm_g, lambda_q1, lambda_k1, lambda_q2, lambda_k2, subln_g, rwkv_mu, w_decay0, w_decay2, a0, a2, g2, k_k, k_a, r_k, ln_x_w, ln_x_b, w_branch_a, w_branch_b, w_out, norm2_g, w_router, router_bias, w_expert_up_gate, w_expert_down, w_shared_up_gate, w_shared_down):
    for l in range(w_ada.shape[0]):
        x = _layer(x, c, positions, l, w_ada[l], b_ada[l], norm1_g[l], w_in[l], w_gate[l], b_gate[l],
                   q_norm_g[l], k_norm_g[l], lambda_q1[l], lambda_k1[l], lambda_q2[l], lambda_k2[l],
                   subln_g[l], rwkv_mu[l], w_decay0[l], w_decay2[l], a0[l], a2[l], g2[l], k_k[l],
                   k_a[l], r_k[l], ln_x_w[l], ln_x_b[l], w_branch_a[l], w_branch_b[l], w_out[l],
                   norm2_g[l], w_router[l], router_bias[l], w_expert_up_gate[l], w_expert_down[l],
                   w_shared_up_gate[l], w_shared_down[l])
    return x
```

```python
import functools
import math

import jax
import jax.numpy as jnp
from jax import lax
from jax.experimental import pallas as pl
from jax.experimental.pallas import tpu as pltpu
from jax.experimental.pallas import tpu_sc as plsc

F32 = jnp.float32
BF16 = jnp.bfloat16
I32 = jnp.int32
U32 = jnp.uint32
HI = lax.Precision.HIGHEST

CHUNK = 64
ROPE_THETA = 10000.0
NORM_EPS = 1e-6
SUBLN_EPS = 1e-5
DA_HEAD_DIM = 64
RWKV_HEAD = 64
GN_EPS = 64e-5
TOP_K = 8
N_GROUPS = 8
TOPK_GROUPS = 4
ROUTED_SCALE = 2.5
EXPERT_TILE = 256
X_SLOTS = 4
RWKV_CHUNK = 64
LANES = 128
SC_WINDOW = 128
NEG = -1e30
MAX_PLAIN_SCORE = 40.0
VMEM_LIMIT = 56 * 1024 * 1024


def _cp(*sem):
    return pltpu.CompilerParams(dimension_semantics=sem, vmem_limit_bytes=VMEM_LIMIT)


def _bdot(a, b):
    return jnp.dot(a.astype(BF16), b.astype(BF16), preferred_element_type=F32)


def _fdot(a, b):
    return jnp.dot(a, b, precision=HI, preferred_element_type=F32)


def _nt(a, b, precision=None):
    return lax.dot_general(a, b, (((1,), (1,)), ((), ())), precision=precision,
                           preferred_element_type=F32)


def _tn(a, b, precision=None):
    return lax.dot_general(a, b, (((0,), (0,)), ((), ())), precision=precision,
                           preferred_element_type=F32)


def _pack_halves(x):
    c = x.shape[1] // 2
    lo = lax.bitcast_convert_type(x[:, :c], U32)
    hi = lax.bitcast_convert_type(x[:, c:], U32)
    return (hi & jnp.uint32(0xFFFF0000)) | (lo >> 16)


def _unpack_halves(w):
    lo = lax.bitcast_convert_type(w << 16, F32)
    hi = lax.bitcast_convert_type(w & jnp.uint32(0xFFFF0000), F32)
    return lo, hi


def _sigmoid(x):
    return 1.0 / (1.0 + jnp.exp(-x))


def _silu(x):
    return x * _sigmoid(x)


def _ada_kernel(c_ref, w_ref, b_ref, o_ref):
    o_ref[...] = _fdot(_silu(c_ref[...]), w_ref[...]) + b_ref[...]


def _adaln(c, w_ada, b_ada):
    B, D = c.shape
    rows = -(-B // 8) * 8
    cpad = jnp.zeros((rows, D), F32).at[:B].set(c)
    n_out = w_ada.shape[1]
    out = pl.pallas_call(
        _ada_kernel,
        grid=(n_out // D,),
        in_specs=[pl.BlockSpec((rows, D), lambda j: (0, 0)),
                  pl.BlockSpec((D, D), lambda j: (0, j)),
                  pl.BlockSpec((1, D), lambda j: (0, j))],
        out_specs=pl.BlockSpec((rows, D), lambda j: (0, j)),
        out_shape=jax.ShapeDtypeStruct((rows, n_out), F32),
        compiler_params=_cp("arbitrary"),
        name="adaln",
    )(cpad, w_ada, b_ada.reshape(1, n_out))
    return out[:B].reshape(B, n_out // D, D)


def _mixer_in_kernel(x_ref, xprev_ref, mod_ref, g_ref, w_ref, bg_ref, pos_ref, invf_ref, qg_ref, kg_ref,
                     mu_ref, w0_ref, w2_ref, a0_ref, a2_ref, g2_ref,
                     qn_ref, kn_ref, v_ref, r_ref, k_ref, vr_ref, a_ref, ld_ref, gr_ref, gate_ref,
                     *, da_width, rw_cols, rw_width, q_scale, tiles_per_seq):
    tm = x_ref.shape[0]

    def modulated(x):
        y = x * lax.rsqrt(jnp.mean(x * x, axis=-1, keepdims=True) + NORM_EPS) * g_ref[...]
        return (y * (1.0 + mod_ref[1:2, :]) + mod_ref[0:1, :]).astype(BF16)

    def proj(hb, c0, width, step=512):
        parts = [jnp.dot(hb, w_ref[:, c0 + o:c0 + min(o + step, width)], preferred_element_type=F32)
                 for o in range(0, width, step)]
        return parts[0] if len(parts) == 1 else jnp.concatenate(parts, axis=1)

    h = modulated(x_ref[...])

    lane = lax.broadcasted_iota(I32, (tm, LANES), 1)
    first = lane < DA_HEAD_DIM
    lo_half = (lane & (DA_HEAD_DIM - 1)) < DA_HEAD_DIM // 2
    ang = pos_ref[...].astype(F32) * invf_ref[...]
    cos = jnp.cos(ang)
    sin = jnp.sin(ang)
    sin = jnp.where(lo_half, -sin, sin)
    for c0, dst, gn_ref, mult in ((0, qn_ref, qg_ref, q_scale), (da_width, kn_ref, kg_ref, 1.0)):
        raw = proj(h, c0, da_width)
        for blk in range(da_width // LANES):
            x = raw[:, blk * LANES:(blk + 1) * LANES]
            xx = x * x
            s_first = jnp.sum(jnp.where(first, xx, 0.0), axis=-1, keepdims=True)
            s_second = jnp.sum(jnp.where(first, 0.0, xx), axis=-1, keepdims=True)
            ms = jnp.where(first, s_first, s_second) * (1.0 / DA_HEAD_DIM)
            xn = x * lax.rsqrt(ms + NORM_EPS) * gn_ref[...]
            rot = jnp.where(lo_half, pltpu.roll(xn, LANES - DA_HEAD_DIM // 2, axis=1),
                            pltpu.roll(xn, DA_HEAD_DIM // 2, axis=1))
            dst[:, blk * LANES:(blk + 1) * LANES] = ((xn * cos + rot * sin) * mult).astype(dst.dtype)
    v_ref[...] = proj(h, 2 * da_width, da_width).astype(v_ref.dtype)

    c_rw = 3 * da_width
    p = proj(h, c_rw, rw_cols)
    p_before = proj(modulated(xprev_ref[...]), c_rw, rw_cols)
    seq_start = (pl.program_id(0) % tiles_per_seq) == 0
    last_prev = jnp.where(seq_start, 0.0, p_before[7:8, :])
    rowi = lax.broadcasted_iota(I32, p.shape, 0)
    prev = jnp.where(rowi == 0, last_prev, pltpu.roll(p, 1, axis=0))
    xs = p + (prev - p) * mu_ref[...]
    width = rw_width
    r_ref[...] = xs[:, 0:width]
    k_ref[...] = xs[:, width:2 * width]
    vr_ref[...] = xs[:, 2 * width:3 * width]
    xwa = xs[:, 3 * width:3 * width + LANES]
    xg = xs[:, 3 * width + LANES:]
    z = w0_ref[...] + _bdot(jnp.tanh(xwa), w2_ref[...])
    w = -(jnp.maximum(-z, 0.0) + jnp.log(1.0 + jnp.exp(-jnp.abs(z)))) - 0.5
    ld_ref[...] = -jnp.exp(w)
    a_ref[...] = _sigmoid(a0_ref[...] + _bdot(xwa, a2_ref[...]))
    gr_ref[...] = _bdot(_sigmoid(xg), g2_ref[...])

    gate_ref[...] = _sigmoid(proj(h, c_rw + rw_cols, gate_ref.shape[1]) + bg_ref[...]).astype(gate_ref.dtype)


def _mixer_in(x2, pos2, mod, norm1_g, w_cat, b_gate, q_norm_g, k_norm_g, mu, w_decay0, w_decay2, a0, a2, g2,
              S, da_width, rw_width, tm=512):
    N, D = x2.shape
    n_gate = b_gate.shape[0]
    rw_cols = mu.shape[0]
    tpb = S // tm
    d = DA_HEAD_DIM
    inv_freq = 1.0 / (ROPE_THETA ** (jnp.arange(0, d, 2, dtype=F32) / d))
    invf = jnp.tile(inv_freq, LANES // (d // 2)).reshape(1, LANES)
    dl, al = w_decay2.shape[0], a2.shape[0]
    assert dl + al == LANES and g2.shape[0] == LANES
    w2p = jnp.zeros((LANES, rw_width), F32).at[:dl].set(w_decay2)
    a2p = jnp.zeros((LANES, rw_width), F32).at[dl:].set(a2)
    kern = functools.partial(_mixer_in_kernel, da_width=da_width, rw_cols=rw_cols, rw_width=rw_width,
                             q_scale=d ** -0.5 * math.log2(math.e),
                             tiles_per_seq=tpb)
    row = lambda w: pl.BlockSpec((tm, w), lambda i: (i, 0))
    vec = lambda n: pl.BlockSpec((1, n), lambda i: (0, 0))
    mat = pl.BlockSpec((LANES, rw_width), lambda i: (0, 0))
    f32 = lambda w: jax.ShapeDtypeStruct((N, w), F32)
    bf16 = lambda w: jax.ShapeDtypeStruct((N, w), BF16)
    return pl.pallas_call(
        kern,
        grid=(N // tm,),
        in_specs=[row(D),
                  pl.BlockSpec((8, D), lambda i: (jnp.maximum(i * (tm // 8) - 1, 0), 0)),
                  pl.BlockSpec((None, 6, D), lambda i: (i // tpb, 0, 0)),
                  vec(D),
                  pl.BlockSpec(w_cat.shape, lambda i: (0, 0)),
                  vec(n_gate),
                  pl.BlockSpec((tm, 1), lambda i: (i, 0)),
                  vec(LANES), vec(LANES), vec(LANES),
                  vec(rw_cols), vec(rw_width), mat, vec(rw_width), mat, mat],
        out_specs=[row(da_width)] * 3 + [row(rw_width)] * 6 + [row(n_gate)],
        out_shape=[bf16(da_width)] * 3 + [f32(rw_width)] * 6 + [bf16(n_gate)],
        compiler_params=_cp("parallel"),
        name="mixer_in",
    )(x2, x2, mod, norm1_g.reshape(1, D), w_cat, b_gate.reshape(1, n_gate), pos2, invf,
      jnp.tile(q_norm_g, 2).reshape(1, LANES), jnp.tile(k_norm_g, 2).reshape(1, LANES),
      mu.reshape(1, rw_cols), w_decay0.reshape(1, rw_width), w2p, a0.reshape(1, rw_width), a2p, g2)


def _attn_kernel(flag_ref, q_ref, k_ref, v_ref, lam_ref, sg_ref, o_ref, qz_ref, m_ref, l_ref, lp_ref, acc_ref,
                 *, tq, lambda_init):
    i = pl.program_id(2)
    lane = lax.broadcasted_iota(I32, (tq, LANES), 1)
    q = q_ref[...]
    zero = jnp.zeros_like(q)
    qz_ref[0:tq, :] = jnp.where(lane < DA_HEAD_DIM, q, zero)
    qz_ref[tq:, :] = jnp.where(lane >= DA_HEAD_DIM, q, zero)
    acc_ref[...] = jnp.zeros_like(acc_ref)
    bounded = flag_ref[0] == 1

    def scores(j, masked):
        off = pl.multiple_of(j * tq, tq)
        s = _nt(qz_ref[...], k_ref[pl.ds(off, tq), :])
        if masked:
            row = lax.broadcasted_iota(I32, s.shape, 0)
            col = lax.broadcasted_iota(I32, s.shape, 1)
            s = jnp.where((col // CHUNK) <= ((row & (tq - 1)) // CHUNK), s, NEG)
        return s, off

    def plain_step(j, masked):
        s, off = scores(j, masked)
        pr = jnp.exp2(s)
        part = pr[:, 0:LANES]
        for cblk in range(1, tq // LANES):
            part = part + pr[:, cblk * LANES:(cblk + 1) * LANES]
        lp_ref[...] += part
        acc_ref[...] += jnp.dot(pr.astype(BF16), v_ref[pl.ds(off, tq), :], preferred_element_type=F32)

    def online_step(j, masked):
        s, off = scores(j, masked)
        m_old = m_ref[...]
        m_new = jnp.maximum(m_old, jnp.max(s, axis=-1, keepdims=True))
        alpha = jnp.exp2(m_old - m_new)
        pr = jnp.exp2(s - m_new)
        l_ref[...] = alpha * l_ref[...] + jnp.sum(pr, axis=-1, keepdims=True)
        acc_ref[...] = alpha * acc_ref[...] + jnp.dot(pr.astype(BF16), v_ref[pl.ds(off, tq), :],
                                                      preferred_element_type=F32)
        m_ref[...] = m_new

    def run(step):
        def body(j, carry):
            step(j, False)
            return carry
        lax.fori_loop(0, i, body, 0)
        step(i, True)

    @pl.when(bounded)
    def _():
        lp_ref[...] = jnp.zeros_like(lp_ref)
        run(plain_step)
        l_ref[...] = jnp.sum(lp_ref[...], axis=-1, keepdims=True)

    @pl.when(jnp.logical_not(bounded))
    def _():
        m_ref[...] = jnp.full_like(m_ref, NEG)
        l_ref[...] = jnp.zeros_like(l_ref)
        run(online_step)

    lv = lam_ref[...]
    lam = (jnp.exp(jnp.sum(lv[0:1] * lv[1:2], keepdims=True))
           - jnp.exp(jnp.sum(lv[2:3] * lv[3:4], keepdims=True)) + lambda_init)
    o1 = acc_ref[0:tq, :] / l_ref[0:tq, :]
    o2 = acc_ref[tq:, :] / l_ref[tq:, :]
    o = o1 - lam * o2
    o = o * lax.rsqrt(jnp.mean(o * o, axis=-1, keepdims=True) + SUBLN_EPS) * sg_ref[...]
    o_ref[...] = (o * (1.0 - lambda_init)).astype(o_ref.dtype)


def _diff_attention(qn, kn, v, score_bound, lam_vecs, subln_g, B, S, lambda_init, tq=512):
    W = qn.shape[1]
    H = W // LANES
    q3 = qn.reshape(B, S, W)
    k3 = kn.reshape(B, S, W)
    v3 = v.reshape(B, S, W)
    flag = (score_bound <= MAX_PLAIN_SCORE).astype(I32).reshape(1)
    qblk = pl.BlockSpec((None, tq, LANES), lambda b, h, i, f: (b, i, h))
    kvblk = pl.BlockSpec((None, S, LANES), lambda b, h, i, f: (b, 0, h))
    grid_spec = pltpu.PrefetchScalarGridSpec(
        num_scalar_prefetch=1,
        grid=(B, H, S // tq),
        in_specs=[qblk, kvblk, kvblk,
                  pl.BlockSpec((4, DA_HEAD_DIM), lambda b, h, i, f: (0, 0)),
                  pl.BlockSpec((1, LANES), lambda b, h, i, f: (0, 0))],
        out_specs=qblk,
        scratch_shapes=[pltpu.VMEM((2 * tq, LANES), BF16),
                        pltpu.VMEM((2 * tq, 1), F32),
                        pltpu.VMEM((2 * tq, 1), F32),
                        pltpu.VMEM((2 * tq, LANES), F32),
                        pltpu.VMEM((2 * tq, LANES), F32)],
    )
    out = pl.pallas_call(
        functools.partial(_attn_kernel, tq=tq, lambda_init=lambda_init),
        grid_spec=grid_spec,
        out_shape=jax.ShapeDtypeStruct((B, S, W), BF16),
        compiler_params=_cp("parallel", "parallel", "arbitrary"),
        name="diff_attn",
    )(flag, q3, k3, v3, lam_vecs, subln_g.reshape(1, LANES))
    return out.reshape(B * S, W)


def _stackmask(m):
    lane = lax.broadcasted_iota(I32, m.shape, 1)
    z = jnp.zeros_like(m)
    return jnp.concatenate([jnp.where(lane < RWKV_HEAD, m, z), jnp.where(lane >= RWKV_HEAD, m, z)], axis=0)


def _pair_sum(x, first):
    s1 = jnp.sum(jnp.where(first, x, 0.0), axis=-1, keepdims=True)
    s2 = jnp.sum(jnp.where(first, 0.0, x), axis=-1, keepdims=True)
    return jnp.where(first, s1, s2)


def _rwkv_scan_kernel(r_ref, k_ref, v_ref, a_ref, ld_ref, g_ref, kk_ref, ka_ref, rk_ref, lnw_ref, lnb_ref,
                      o_ref, s_ref, *, L):
    tm, W = r_ref.shape
    n_chunks = tm // L
    n_pairs = W // LANES
    hd = RWKV_HEAD
    bf = lambda t: t.astype(BF16)

    @pl.when(pl.program_id(1) == 0)
    def _():
        s_ref[...] = jnp.zeros_like(s_ref)

    row = lax.broadcasted_iota(I32, (tm, tm), 0)
    col = lax.broadcasted_iota(I32, (tm, tm), 1)
    tri = jnp.where(jnp.logical_and(col <= row, (col // L) == (row // L)), 1.0, 0.0).astype(BF16)
    ld = ld_ref[...]
    ld_hi = bf(ld)
    rem = ld - ld_hi.astype(F32)
    ld_mid = bf(rem)
    ld_lo = bf(rem - ld_mid.astype(F32))
    c = (jnp.dot(tri, ld_hi, preferred_element_type=F32) + jnp.dot(tri, ld_mid, preferred_element_type=F32)
         + jnp.dot(tri, ld_lo, preferred_element_type=F32))
    ec = jnp.exp(c)
    eci = jnp.exp(-c)
    ecm = jnp.exp(c - ld)
    r = r_ref[...]
    k = k_ref[...]
    v = v_ref[...]
    a = a_ref[...]
    kkr = k * kk_ref[...]
    kmod = k * (1.0 + (a - 1.0) * ka_ref[...])
    brk = r * kmod * rk_ref[...]

    lane = lax.broadcasted_iota(I32, (L, LANES), 1)
    rowl = lax.broadcasted_iota(I32, (L, LANES), 0)
    first = lane < hd
    lane_h = lane & (hd - 1)
    strict = lane_h < rowl
    incl = lane_h <= rowl
    eye = jnp.where(lane_h == rowl, 1.0, 0.0)

    chains = [(ch, p) for ch in range(n_chunks) for p in range(n_pairs)]
    rsl = lambda ch: slice(ch * L, (ch + 1) * L)
    csl = lambda p: slice(p * LANES, (p + 1) * LANES)
    fdot = lambda x, y: jnp.dot(x, y, preferred_element_type=F32)
    at, bt, kt, rt, vh, g_l = {}, {}, {}, {}, {}, {}
    for c_ in chains:
        ch, p = c_
        rs, cs = rsl(ch), csl(p)
        kkh = kkr[rs, cs]
        kkh = kkh / jnp.maximum(jnp.sqrt(_pair_sum(kkh * kkh, first)), 1e-12)
        vh[c_] = v[rs, cs]
        g_l[c_] = ec[ch * L + L - 1:ch * L + L, cs]
        at[c_] = -kkh * ecm[rs, cs]
        bt[c_] = kkh * a[rs, cs] * eci[rs, cs]
        kt[c_] = kmod[rs, cs] * eci[rs, cs]
        rt[c_] = r[rs, cs] * ec[rs, cs]
    gm = {c_: _nt(bf(jnp.concatenate([at[c_], rt[c_]], axis=0)),
                  jnp.concatenate([_stackmask(bf(bt[c_])), _stackmask(bf(kt[c_]))], axis=0)) for c_ in chains}
    a_ab = {c_: jnp.where(strict, gm[c_][:L, :LANES], 0.0) for c_ in chains}
    vsm = {c_: _stackmask(bf(vh[c_])) for c_ in chains}
    cmat = {c_: fdot(bf(jnp.where(strict, gm[c_][:L, LANES:], 0.0)), vsm[c_]) for c_ in chains}
    t_inv = {c_: eye + a_ab[c_] for c_ in chains}
    pw = {c_: bf(a_ab[c_]) for c_ in chains}
    for _ in range(int(math.log2(L)) - 1):
        pw = {c_: bf(fdot(pw[c_], _stackmask(pw[c_]))) for c_ in chains}
        t_inv = {c_: t_inv[c_] + fdot(pw[c_], _stackmask(bf(t_inv[c_]))) for c_ in chains}
    zz = {c_: fdot(bf(t_inv[c_]), jnp.concatenate([_stackmask(bf(at[c_])), _stackmask(bf(cmat[c_]))], axis=1))
          for c_ in chains}
    qy = {c_: fdot(bf(jnp.where(incl, gm[c_][L:, :LANES], 0.0)),
                   jnp.concatenate([_stackmask(bf(zz[c_][:, :LANES])), _stackmask(bf(zz[c_][:, LANES:]))], axis=1))
          for c_ in chains}
    y0 = {c_: qy[c_][:, LANES:] + fdot(bf(jnp.where(incl, gm[c_][L:, LANES:], 0.0)), vsm[c_]) for c_ in chains}
    qa = {c_: bf(jnp.concatenate([rt[c_] + qy[c_][:, :LANES], zz[c_][:, :LANES]], axis=0)) for c_ in chains}
    bkg = {c_: bf(jnp.concatenate([bt[c_] * g_l[c_], kt[c_] * g_l[c_]], axis=0)) for c_ in chains}

    lane_s = lax.broadcasted_iota(I32, (hd, LANES), 1)
    sp = [s_ref[p] for p in range(n_pairs)]
    for ch in range(n_chunks):
        rs = rsl(ch)
        yw = [_nt(qa[ch, p], _stackmask(bf(sp[p]))) for p in range(n_pairs)]
        upd = [_tn(bf(jnp.concatenate([yw[p][L:] + zz[ch, p][:, LANES:], vh[ch, p]], axis=0)), bkg[ch, p])
               for p in range(n_pairs)]
        for p in range(n_pairs):
            cs = csl(p)
            sp[p] = sp[p] * g_l[ch, p] + jnp.where(lane_s < hd, upd[p][:hd], upd[p][hd:])
            y = yw[p][:L] + y0[ch, p]
            mean = _pair_sum(y, first) * (1.0 / hd)
            yc = y - mean
            var = _pair_sum(yc * yc, first) * (1.0 / hd)
            yn = yc * lax.rsqrt(var + GN_EPS) * lnw_ref[:, cs] + lnb_ref[:, cs]
            bonus = _pair_sum(brk[rs, cs], first) * vh[ch, p]
            o_ref[rs, cs] = ((yn + bonus) * g_ref[rs, cs]).astype(o_ref.dtype)
    for p in range(n_pairs):
        s_ref[p] = sp[p]


def _rwkv_scan(r, k, v, a, ld, g, k_k, k_a, r_k, ln_w, ln_b, L=RWKV_CHUNK, tm=512):
    B, S, W = r.shape
    seq = pl.BlockSpec((None, tm, W), lambda b, c: (b, c, 0))
    vec = pl.BlockSpec((1, W), lambda b, c: (0, 0))
    return pl.pallas_call(
        functools.partial(_rwkv_scan_kernel, L=L),
        grid=(B, S // tm),
        in_specs=[seq] * 6 + [vec] * 5,
        out_specs=seq,
        out_shape=jax.ShapeDtypeStruct((B, S, W), BF16),
        scratch_shapes=[pltpu.VMEM((W // LANES, RWKV_HEAD, LANES), F32)],
        compiler_params=_cp("parallel", "arbitrary"),
        name="rwkv_scan",
    )(r, k, v, a, ld, g, k_k.reshape(1, W), k_a.reshape(1, W), r_k.reshape(1, W),
      ln_w.reshape(1, W), ln_b.reshape(1, W))


def _post_kernel(attn_ref, rw_ref, gate_ref, x_ref, mod_ref, wa_ref, wb_ref, wo_ref, g2_ref, wrh_ref, wrm_ref,
                 sug_ref, sd_ref, base_ref, h2p_ref, lg_ref):
    D = x_ref.shape[1]
    ya = jnp.dot(attn_ref[...], wa_ref[...], preferred_element_type=F32)
    yb = jnp.dot(rw_ref[...], wb_ref[...], preferred_element_type=F32)
    m = gate_ref[:, 0:D] * ya + gate_ref[:, D:] * yb
    x1 = x_ref[...] + mod_ref[2:3, :] * jnp.dot(m.astype(BF16), wo_ref[...], preferred_element_type=F32)
    y = x1 * lax.rsqrt(jnp.mean(x1 * x1, axis=-1, keepdims=True) + NORM_EPS) * g2_ref[...]
    h2 = y * (1.0 + mod_ref[4:5, :]) + mod_ref[3:4, :]
    hb = h2.astype(BF16)
    hm = (h2 - hb.astype(F32)).astype(BF16)
    lg_ref[...] = _nt(wrh_ref[...], hb) + _nt(wrh_ref[...], hm) + _nt(wrm_ref[...], hb)
    packed = _pack_halves(hb.astype(F32))
    half = packed.shape[1] // 2
    h2p_ref[0] = packed[:, :half]
    h2p_ref[1] = packed[:, half:]
    F = sd_ref.shape[0]
    gu = jnp.dot(hb, sug_ref[...], preferred_element_type=F32)
    shared = jnp.dot((_silu(gu[:, :F]) * gu[:, F:]).astype(BF16), sd_ref[...], preferred_element_type=F32)
    base_ref[...] = x1 + mod_ref[5:6, :] * shared


def _post(attn, rw, gate, x2, mod, wa, wb, wo, norm2_g, w_router_t, sug, sd, S, tm=512):
    N, D = x2.shape
    E = w_router_t.shape[0]
    wr_hi = w_router_t.astype(BF16)
    wr_mid = (w_router_t - wr_hi.astype(F32)).astype(BF16)
    tpb = S // tm
    row = lambda w: pl.BlockSpec((tm, w), lambda i: (i, 0))
    full = lambda a: pl.BlockSpec(a.shape, lambda i: (0, 0))
    return pl.pallas_call(
        _post_kernel,
        grid=(N // tm,),
        in_specs=[row(attn.shape[1]), row(rw.shape[1]), row(gate.shape[1]), row(D),
                  pl.BlockSpec((None, 6, D), lambda i: (i // tpb, 0, 0)),
                  full(wa), full(wb), full(wo), pl.BlockSpec((1, D), lambda i: (0, 0)), full(wr_hi), full(wr_mid),
                  full(sug), full(sd)],
        out_specs=[row(D), pl.BlockSpec((2, tm, D // 4), lambda i: (0, i, 0)), pl.BlockSpec((E, tm), lambda i: (0, i))],
        out_shape=[jax.ShapeDtypeStruct((N, D), F32), jax.ShapeDtypeStruct((2, N, D // 4), U32),
                   jax.ShapeDtypeStruct((E, N), F32)],
        compiler_params=_cp("parallel"),
        name="post_mixer",
    )(attn, rw, gate, x2, mod, wa, wb, wo, norm2_g.reshape(1, D), wr_hi, wr_mid, sug, sd)


def _first_argmax(x, idx, sentinel):
    m = jnp.max(x, axis=0, keepdims=True)
    return m, jnp.min(jnp.where(x == m, idx, sentinel), axis=0, keepdims=True)


def _route_kernel(lg_ref, bias_ref, e_ref, w_ref):
    E, T = lg_ref.shape
    gsz = E // N_GROUPS
    scores = _sigmoid(lg_ref[...])
    biased = scores + bias_ref[...]
    ig = lax.broadcasted_iota(I32, (gsz, T), 0)
    gs = []
    for g in range(N_GROUPS):
        blk = biased[g * gsz:(g + 1) * gsz, :]
        m1, i1 = _first_argmax(blk, ig, gsz)
        m2 = jnp.max(jnp.where(ig == i1, -jnp.inf, blk), axis=0, keepdims=True)
        gs.append(m1 + m2)
    gsc = jnp.concatenate(gs, axis=0)
    i8 = lax.broadcasted_iota(I32, (N_GROUPS, T), 0)
    chosen = jnp.zeros((N_GROUPS, T), F32)
    for _ in range(TOPK_GROUPS):
        _, gi = _first_argmax(gsc, i8, N_GROUPS)
        hit = i8 == gi
        chosen = jnp.where(hit, 1.0, chosen)
        gsc = jnp.where(hit, -jnp.inf, gsc)
    masked = jnp.concatenate(
        [jnp.where(chosen[g:g + 1, :] > 0.0, biased[g * gsz:(g + 1) * gsz, :], -jnp.inf)
         for g in range(N_GROUPS)], axis=0)
    ie = lax.broadcasted_iota(I32, (E, T), 0)
    idxs, wts = [], []
    for _ in range(TOP_K):
        _, ei = _first_argmax(masked, ie, E)
        hit = ie == ei
        idxs.append(ei)
        wts.append(jnp.sum(jnp.where(hit, scores, 0.0), axis=0, keepdims=True))
        masked = jnp.where(hit, -jnp.inf, masked)
    wt = jnp.concatenate(wts, axis=0)
    e_ref[...] = jnp.concatenate(idxs, axis=0)
    w_ref[...] = wt / jnp.sum(wt, axis=0, keepdims=True) * ROUTED_SCALE


def _route(logits_t, router_bias, T=512):
    E, N = logits_t.shape
    blk = pl.BlockSpec((TOP_K, T), lambda i: (0, i))
    return pl.pallas_call(
        _route_kernel,
        grid=(N // T,),
        in_specs=[pl.BlockSpec((E, T), lambda i: (0, i)), pl.BlockSpec((E, 1), lambda i: (0, 0))],
        out_specs=[blk, blk],
        out_shape=[jax.ShapeDtypeStruct((TOP_K, N), I32), jax.ShapeDtypeStruct((TOP_K, N), F32)],
        compiler_params=_cp("parallel"),
        name="route",
    )(logits_t, router_bias.reshape(E, 1))


def _rank_kernel(e_ref, rank_ref, cnt_ref, carry_ref, *, n_experts):
    T = e_ref.shape[1]

    @pl.when(pl.program_id(0) == 0)
    def _():
        carry_ref[...] = jnp.zeros_like(carry_ref)

    ie = lax.broadcasted_iota(I32, (n_experts, T), 0)
    e = e_ref[...]
    hits = [ie == e[kk:kk + 1, :] for kk in range(TOP_K)]
    onehot = jnp.zeros((n_experts, T), F32)
    for hsel in hits:
        onehot = onehot + hsel.astype(F32)
    tr = lax.broadcasted_iota(I32, (T, T), 0)
    tc = lax.broadcasted_iota(I32, (T, T), 1)
    before = (tr < tc).astype(BF16)
    base = _bdot(onehot, before) + carry_ref[:, 0:1]
    rank_ref[...] = jnp.concatenate(
        [jnp.sum(jnp.where(hsel, base, 0.0), axis=0, keepdims=True) for hsel in hits], axis=0).astype(I32)
    carry_ref[...] = carry_ref[...] + jnp.sum(onehot, axis=1, keepdims=True)
    cnt_ref[...] = carry_ref[...]


def _ranks(eidx_t, n_experts, T=512):
    N = eidx_t.shape[1]
    blk = pl.BlockSpec((TOP_K, T), lambda i: (0, i))
    cnt = pl.BlockSpec((n_experts, LANES), lambda i: (0, 0))
    return pl.pallas_call(
        functools.partial(_rank_kernel, n_experts=n_experts),
        grid=(N // T,),
        in_specs=[blk],
        out_specs=[blk, cnt],
        out_shape=[jax.ShapeDtypeStruct((TOP_K, N), I32), jax.ShapeDtypeStruct((n_experts, LANES), F32)],
        scratch_shapes=[pltpu.VMEM((n_experts, LANES), F32)],
        compiler_params=_cp("arbitrary"),
        name="ranks",
    )(eidx_t)


def _dest_kernel(e_ref, rank_ref, start_ref, d_ref):
    E = start_ref.shape[0]
    T = e_ref.shape[1]
    ie = lax.broadcasted_iota(I32, (E, T), 0)
    e = e_ref[...]
    start = start_ref[:, 0:1]
    rows = [jnp.sum(jnp.where(ie == e[kk:kk + 1, :], start, 0.0), axis=0, keepdims=True) for kk in range(TOP_K)]
    d_ref[...] = jnp.concatenate(rows, axis=0).astype(I32) + rank_ref[...]


def _dests(eidx_t, rank_t, pstart, T=512):
    N = eidx_t.shape[1]
    E = pstart.shape[0]
    blk = pl.BlockSpec((TOP_K, T), lambda i: (0, i))
    return pl.pallas_call(
        _dest_kernel,
        grid=(N // T,),
        in_specs=[blk, blk, pl.BlockSpec((E, LANES), lambda i: (0, 0))],
        out_specs=blk,
        out_shape=jax.ShapeDtypeStruct((TOP_K, N), I32),
        compiler_params=_cp("parallel"),
        name="dests",
    )(eidx_t, rank_t, jnp.broadcast_to(pstart.astype(F32)[:, None], (E, LANES)))


def _expert_kernel(us_ref, ps_ref, x_hbm, wug_ref, wd_ref, y_hbm,
                   wug_bf, wd_bf, xbuf, ybuf, cnt_ref, xsem, ysem, *, n_rows):
    e = pl.program_id(0)
    _, _, R, Ch = xbuf.shape
    C = 2 * Ch
    F = wd_ref.shape[0]

    def x_copy(slot, half, row):
        return pltpu.make_async_copy(x_hbm.at[half, pl.ds(row, R)], xbuf.at[slot, half], xsem.at[slot])

    def y_copy(slot, half, row):
        return pltpu.make_async_copy(ybuf.at[slot, half], y_hbm.at[half, pl.ds(row, R)], ysem.at[slot])

    def start(copy, slot, row):
        for half in range(2):
            copy(slot, half, row).start(priority=1)

    def wait(copy, slot):
        for half in range(2):
            copy(slot, half, 0).wait()

    @pl.when(e == 0)
    def _():
        cnt_ref[0] = 0
        for t in range(X_SLOTS - 1):
            start(x_copy, t, t * R)
        ybuf[...] = jnp.zeros_like(ybuf)
        for slot in range(2):
            start(y_copy, slot, n_rows + slot * R)

    wug_bf[...] = wug_ref[...].astype(BF16)
    wd_bf[...] = wd_ref[...].astype(BF16)
    n_valid = us_ref[e + 1] - us_ref[e]
    pbase = ps_ref[e]
    n_tiles = lax.shift_right_logical(n_valid + (R - 1), int(math.log2(R)))
    rowid = lax.broadcasted_iota(I32, (R, C), 0)

    def tile(i, carry):
        n_done = cnt_ref[0]
        slot = n_done & 1
        xslot = lax.rem(n_done, X_SLOTS)
        ahead = n_done + (X_SLOTS - 1)
        start(x_copy, lax.rem(ahead, X_SLOTS), pl.multiple_of(ahead * R, R))
        wait(x_copy, xslot)
        x = jnp.concatenate([xbuf[xslot, 0], xbuf[xslot, 1]], axis=1)
        x = jnp.where(rowid < n_valid - i * R, x, jnp.uint32(0))
        lo, hi = _unpack_halves(x)
        gu = (jnp.dot(lo.astype(BF16), wug_bf[0:C, :], preferred_element_type=F32)
              + jnp.dot(hi.astype(BF16), wug_bf[C:, :], preferred_element_type=F32))
        hid = (_silu(gu[:, :F]) * gu[:, F:]).astype(BF16)
        y = jnp.dot(hid, wd_bf[...], preferred_element_type=F32)
        wait(y_copy, slot)
        packed = _pack_halves(y.astype(BF16).astype(F32))
        ybuf[slot, 0] = packed[:, :Ch]
        ybuf[slot, 1] = packed[:, Ch:]
        start(y_copy, slot, pl.multiple_of(pbase + i * R, R))
        cnt_ref[0] = n_done + 1
        return carry

    lax.fori_loop(0, n_tiles, tile, 0)

    @pl.when(e == pl.num_programs(0) - 1)
    def _():
        for t in range(X_SLOTS - 1):
            wait(x_copy, lax.rem(cnt_ref[0] + t, X_SLOTS))
        for slot in range(2):
            wait(y_copy, slot)
        ybuf[0] = jnp.zeros((2, R, Ch), U32)
        first = lax.shift_right_logical(pbase + n_tiles * R, int(math.log2(R)))
        n_left = n_rows // R - first

        def fill(t, carry):
            start(y_copy, 0, pl.multiple_of((first + t) * R, R))
            return carry

        def drain(t, carry):
            wait(y_copy, 0)
            return carry

        lax.fori_loop(0, n_left, fill, 0)
        lax.fori_loop(0, n_left, drain, 0)


def _experts(ustart, pstart, xg, w_ug, w_d, n_rows, R):
    _, _, Ch = xg.shape
    E, D, F2 = w_ug.shape
    F = w_d.shape[1]
    grid_spec = pltpu.PrefetchScalarGridSpec(
        num_scalar_prefetch=2,
        grid=(E,),
        in_specs=[pl.BlockSpec(memory_space=pl.ANY),
                  pl.BlockSpec((None, D, F2), lambda e, us, ps: (e, 0, 0)),
                  pl.BlockSpec((None, F, D), lambda e, us, ps: (e, 0, 0))],
        out_specs=pl.BlockSpec(memory_space=pl.ANY),
        scratch_shapes=[pltpu.VMEM((D, F2), BF16), pltpu.VMEM((F, D), BF16),
                        pltpu.VMEM((X_SLOTS, 2, R, Ch), U32), pltpu.VMEM((2, 2, R, Ch), U32), pltpu.SMEM((1,), I32),
                        pltpu.SemaphoreType.DMA((X_SLOTS,)), pltpu.SemaphoreType.DMA((2,))],
    )
    return pl.pallas_call(
        functools.partial(_expert_kernel, n_rows=n_rows),
        grid_spec=grid_spec,
        out_shape=jax.ShapeDtypeStruct((2, n_rows + 2 * R, Ch), U32),
        compiler_params=_cp("arbitrary"),
        name="experts",
    )(ustart, pstart, xg, w_ug, w_d)


def _sc_scatter_rows(src, idx, n_rows):
    H, N, C = src.shape
    K = idx.shape[0]
    per_row = N // SC_WINDOW
    mesh = plsc.VectorSubcoreMesh(core_axis_name="c", subcore_axis_name="s")

    @functools.partial(pl.kernel, out_type=jax.ShapeDtypeStruct((H, n_rows, C), src.dtype), mesh=mesh,
                       scratch_types=[])
    def scatter_kernel(x_hbm, i_hbm, o_hbm):
        for h in range(H):
            def body(x_vmem, i_vmem):
                pltpu.sync_copy(x_vmem, o_hbm.at[h].at[i_vmem.at[0]])

            pltpu.emit_pipeline(
                body,
                grid=(K * per_row,),
                in_specs=[pl.BlockSpec((SC_WINDOW, C), lambda i: (i % per_row, 0)),
                          pl.BlockSpec((1, SC_WINDOW), lambda i: (i // per_row, i % per_row))],
                out_specs=[],
                core_axis_name=("c", "s"),
                dimension_semantics=(pltpu.PARALLEL,),
            )(x_hbm.at[h], i_hbm)

    return scatter_kernel(src, idx)


def _sc_gather_rows(src, idx):
    H, _, C = src.shape
    K, N = idx.shape
    per_row = N // SC_WINDOW
    mesh = plsc.VectorSubcoreMesh(core_axis_name="c", subcore_axis_name="s")

    @functools.partial(pl.kernel, out_type=jax.ShapeDtypeStruct((H, K * N, C), src.dtype), mesh=mesh,
                       scratch_types=[])
    def gather_kernel(x_hbm, i_hbm, o_hbm):
        for h in range(H):
            def body(i_vmem, o_vmem):
                pltpu.sync_copy(x_hbm.at[h].at[i_vmem.at[0]], o_vmem)

            pltpu.emit_pipeline(
                body,
                grid=(K * per_row,),
                in_specs=[pl.BlockSpec((1, SC_WINDOW), lambda i: (i // per_row, i % per_row))],
                out_specs=[pl.BlockSpec((SC_WINDOW, C), lambda i: (i, 0))],
                core_axis_name=("c", "s"),
                dimension_semantics=(pltpu.PARALLEL,),
            )(i_hbm, o_hbm.at[h])

    return gather_kernel(src, idx)


def _combine_kernel(*refs):
    y_refs = refs[:2 * TOP_K]
    w_ref, base_ref, mod_ref, o_ref = refs[2 * TOP_K:]
    T = base_ref.shape[0]
    tr = lax.broadcasted_iota(I32, (T, T), 0)
    tc = lax.broadcasted_iota(I32, (T, T), 1)
    wcol = _nt((tr == tc).astype(F32), w_ref[...], precision=HI)
    acc = [None] * 4
    for kk in range(TOP_K):
        wk = wcol[:, kk:kk + 1]
        for half in range(2):
            lo, hi = _unpack_halves(y_refs[2 * kk + half][...])
            for q, val in ((half, lo), (2 + half, hi)):
                acc[q] = val * wk if acc[q] is None else acc[q] + val * wk
    o_ref[...] = base_ref[...] + mod_ref[5:6, :] * jnp.concatenate(acc, axis=1)


def _combine(yg, w_t, base, mod, S, T=512):
    N, D = base.shape
    C = yg.shape[2]
    tpb = S // T
    n_tiles = N // T
    row = pl.BlockSpec((T, D), lambda i: (i, 0))
    piece = lambda kk, half: pl.BlockSpec((None, T, C), lambda i: (half, kk * n_tiles + i, 0))
    return pl.pallas_call(
        _combine_kernel,
        grid=(n_tiles,),
        in_specs=[piece(kk, half) for kk in range(TOP_K) for half in range(2)] + [
            pl.BlockSpec((TOP_K, T), lambda i: (0, i)),
            row,
            pl.BlockSpec((None, 6, D), lambda i: (i // tpb, 0, 0))],
        out_specs=row,
        out_shape=jax.ShapeDtypeStruct((N, D), F32),
        compiler_params=_cp("parallel"),
        name="combine",
    )(*([yg] * (2 * TOP_K)), w_t, base, mod)


def _layer(x, c, positions, layer_idx, w_ada, b_ada, norm1_g, w_in, w_gate, b_gate,
           q_norm_g, k_norm_g, lambda_q1, lambda_k1, lambda_q2, lambda_k2, subln_g,
           rwkv_mu, w_decay0, w_decay2, a0, a2, g2, k_k, k_a, r_k, ln_x_w, ln_x_b,
           w_branch_a, w_branch_b, w_out, norm2_g, w_router, router_bias,
           w_expert_up_gate, w_expert_down, w_shared_up_gate, w_shared_down):
    B, S, D = x.shape
    N = B * S
    E = w_router.shape[1]
    da_width = w_branch_a.shape[0]
    rw_width = w_branch_b.shape[0]
    lambda_init = 0.8 - 0.6 * math.exp(-0.3 * layer_idx)

    mod = _adaln(c, w_ada, b_ada)
    x2 = x.reshape(N, D)
    w_cat = jnp.concatenate([w_in, w_gate], axis=1).astype(BF16)
    qn, kn, v, r_, k_, v_, a_, ld_, g_, gate = _mixer_in(
        x2, positions.reshape(N, 1), mod, norm1_g, w_cat, b_gate, q_norm_g, k_norm_g,
        rwkv_mu, w_decay0, w_decay2, a0, a2, g2, S, da_width, rw_width)

    lam_vecs = jnp.stack([lambda_q1, lambda_k1, lambda_q2, lambda_k2])
    score_bound = 1.01 * DA_HEAD_DIM ** 0.5 * jnp.max(jnp.abs(q_norm_g)) * jnp.max(jnp.abs(k_norm_g))
    attn = _diff_attention(qn, kn, v, score_bound, lam_vecs, subln_g, B, S, lambda_init)

    seq = lambda t: t.reshape(B, S, rw_width)
    rw = _rwkv_scan(seq(r_), seq(k_), seq(v_), seq(a_), seq(ld_), seq(g_), k_k, k_a, r_k.reshape(-1),
                    ln_x_w, ln_x_b).reshape(N, rw_width)

    base, h2p, logits_t = _post(attn, rw, gate, x2, mod, w_branch_a.astype(BF16), w_branch_b.astype(BF16),
                                w_out.astype(BF16), norm2_g, w_router.T,
                                w_shared_up_gate.astype(BF16), w_shared_down.astype(BF16), S)

    eidx_t, w_t = _route(logits_t, router_bias)
    rank_t, counts = _ranks(eidx_t, E)
    R = EXPERT_TILE
    cnt = counts[:, 0].astype(I32)
    ustart = jnp.concatenate([jnp.zeros((1,), I32), jnp.cumsum(cnt)])
    pcnt = (cnt + R - 1) // R * R
    pstart = jnp.cumsum(pcnt) - pcnt
    dest_p = _dests(eidx_t, rank_t, pstart)
    n_rows = (N * TOP_K + E * (R - 1) + R - 1) // R * R
    xg = _sc_scatter_rows(h2p, dest_p, n_rows + (X_SLOTS - 1) * R)
    y = _experts(ustart, pstart, xg, w_expert_up_gate, w_expert_down, n_rows, R)
    yg = _sc_gather_rows(y, dest_p)
    out = _combine(yg, w_t, base, mod, S)
    return out.reshape(B, S, D)


def kernel(x, c, positions, w_ada, b_ada, norm1_g, w_in, w_gate, b_gate, q_norm_g, k_norm_g, lambda_q1, lambda_k1, lambda_q2, lambda_k2, subln_g, rwkv_mu, w_decay0, w_decay2, a0, a2, g2, k_k, k_a, r_k, ln_x_w, ln_x_b, w_branch_a, w_branch_b, w_out, norm2_g, w_router, router_bias, w_expert_up_gate, w_expert_down, w_shared_up_gate, w_shared_down):
    for l in range(w_ada.shape[0]):
        x = _layer(x, c, positions, l, w_ada[l], b_ada[l], norm1_g[l], w_in[l], w_gate[l], b_gate[l],
                   q_norm_g[l], k_norm_g[l], lambda_q1[l], lambda_k1[l], lambda_q2[l], lambda_k2[l],
                   subln_g[l], rwkv_mu[l], w_decay0[l], w_decay2[l], a0[l], a2[l], g2[l], k_k[l],
                   k_a[l], r_k[l], ln_x_w[l], ln_x_b[l], w_branch_a[l], w_branch_b[l], w_out[l],
                   norm2_g[l], w_router[l], router_bias[l], w_expert_up_gate[l], w_expert_down[l],
                   w_shared_up_gate[l], w_shared_down[l])
    return x
```

```python
import functools
import math

import jax
import jax.numpy as jnp
from jax import lax
from jax.experimental import pallas as pl
from jax.experimental.pallas import tpu as pltpu
from jax.experimental.pallas import tpu_sc as plsc

F32 = jnp.float32
BF16 = jnp.bfloat16
I32 = jnp.int32
U32 = jnp.uint32
HI = lax.Precision.HIGHEST

CHUNK = 64
ROPE_THETA = 10000.0
NORM_EPS = 1e-6
SUBLN_EPS = 1e-5
DA_HEAD_DIM = 64
RWKV_HEAD = 64
GN_EPS = 64e-5
TOP_K = 8
N_GROUPS = 8
TOPK_GROUPS = 4
ROUTED_SCALE = 2.5
EXPERT_TILE = 256
X_SLOTS = 6
X_AHEAD = X_SLOTS - 2
RWKV_CHUNK = 64
LANES = 128
SC_WINDOW = 128
NEG = -1e30
MAX_PLAIN_SCORE = 40.0
VMEM_LIMIT = 56 * 1024 * 1024


def _cp(*sem):
    return pltpu.CompilerParams(dimension_semantics=sem, vmem_limit_bytes=VMEM_LIMIT)


def _bdot(a, b):
    return jnp.dot(a.astype(BF16), b.astype(BF16), preferred_element_type=F32)


def _fdot(a, b):
    return jnp.dot(a, b, precision=HI, preferred_element_type=F32)


def _nt(a, b, precision=None):
    return lax.dot_general(a, b, (((1,), (1,)), ((), ())), precision=precision,
                           preferred_element_type=F32)


def _tn(a, b, precision=None):
    return lax.dot_general(a, b, (((0,), (0,)), ((), ())), precision=precision,
                           preferred_element_type=F32)


def _pack_halves(x):
    c = x.shape[1] // 2
    lo = lax.bitcast_convert_type(x[:, :c], U32)
    hi = lax.bitcast_convert_type(x[:, c:], U32)
    return (hi & jnp.uint32(0xFFFF0000)) | (lo >> 16)


def _unpack_halves(w):
    lo = lax.bitcast_convert_type(w << 16, F32)
    hi = lax.bitcast_convert_type(w & jnp.uint32(0xFFFF0000), F32)
    return lo, hi


def _sigmoid(x):
    return 1.0 / (1.0 + jnp.exp(-x))


def _silu(x):
    return x * _sigmoid(x)


def _ada_kernel(c_ref, w_ref, b_ref, o_ref):
    o_ref[...] = _fdot(_silu(c_ref[...]), w_ref[...]) + b_ref[...]


def _adaln(c, w_ada, b_ada):
    B, D = c.shape
    rows = -(-B // 8) * 8
    cpad = jnp.zeros((rows, D), F32).at[:B].set(c)
    n_out = w_ada.shape[1]
    out = pl.pallas_call(
        _ada_kernel,
        grid=(n_out // D,),
        in_specs=[pl.BlockSpec((rows, D), lambda j: (0, 0)),
                  pl.BlockSpec((D, D), lambda j: (0, j)),
                  pl.BlockSpec((1, D), lambda j: (0, j))],
        out_specs=pl.BlockSpec((rows, D), lambda j: (0, j)),
        out_shape=jax.ShapeDtypeStruct((rows, n_out), F32),
        compiler_params=_cp("arbitrary"),
        name="adaln",
    )(cpad, w_ada, b_ada.reshape(1, n_out))
    return out[:B].reshape(B, n_out // D, D)


def _mixer_in_kernel(x_ref, xprev_ref, mod_ref, g_ref, w_ref, bg_ref, pos_ref, invf_ref, qg_ref, kg_ref,
                     mu_ref, w0_ref, w2_ref, a0_ref, a2_ref, g2_ref,
                     qn_ref, kn_ref, v_ref, r_ref, k_ref, vr_ref, a_ref, ld_ref, gr_ref, gate_ref,
                     *, da_width, rw_cols, rw_width, q_scale, tiles_per_seq):
    tm = x_ref.shape[0]

    def modulated(x):
        y = x * lax.rsqrt(jnp.mean(x * x, axis=-1, keepdims=True) + NORM_EPS) * g_ref[...]
        return (y * (1.0 + mod_ref[1:2, :]) + mod_ref[0:1, :]).astype(BF16)

    def proj(hb, c0, width, step=512):
        parts = [jnp.dot(hb, w_ref[:, c0 + o:c0 + min(o + step, width)], preferred_element_type=F32)
                 for o in range(0, width, step)]
        return parts[0] if len(parts) == 1 else jnp.concatenate(parts, axis=1)

    h = modulated(x_ref[...])

    lane = lax.broadcasted_iota(I32, (tm, LANES), 1)
    first = lane < DA_HEAD_DIM
    lo_half = (lane & (DA_HEAD_DIM - 1)) < DA_HEAD_DIM // 2
    ang = pos_ref[...].astype(F32) * invf_ref[...]
    cos = jnp.cos(ang)
    sin = jnp.sin(ang)
    sin = jnp.where(lo_half, -sin, sin)
    for c0, dst, gn_ref, mult in ((0, qn_ref, qg_ref, q_scale), (da_width, kn_ref, kg_ref, 1.0)):
        raw = proj(h, c0, da_width)
        for blk in range(da_width // LANES):
            x = raw[:, blk * LANES:(blk + 1) * LANES]
            xx = x * x
            s_first = jnp.sum(jnp.where(first, xx, 0.0), axis=-1, keepdims=True)
            s_second = jnp.sum(jnp.where(first, 0.0, xx), axis=-1, keepdims=True)
            ms = jnp.where(first, s_first, s_second) * (1.0 / DA_HEAD_DIM)
            xn = x * lax.rsqrt(ms + NORM_EPS) * gn_ref[...]
            rot = jnp.where(lo_half, pltpu.roll(xn, LANES - DA_HEAD_DIM // 2, axis=1),
                            pltpu.roll(xn, DA_HEAD_DIM // 2, axis=1))
            dst[:, blk * LANES:(blk + 1) * LANES] = ((xn * cos + rot * sin) * mult).astype(dst.dtype)
    v_ref[...] = proj(h, 2 * da_width, da_width).astype(v_ref.dtype)

    c_rw = 3 * da_width
    p = proj(h, c_rw, rw_cols)
    p_before = proj(modulated(xprev_ref[...]), c_rw, rw_cols)
    seq_start = (pl.program_id(0) % tiles_per_seq) == 0
    last_prev = jnp.where(seq_start, 0.0, p_before[7:8, :])
    rowi = lax.broadcasted_iota(I32, p.shape, 0)
    prev = jnp.where(rowi == 0, last_prev, pltpu.roll(p, 1, axis=0))
    xs = p + (prev - p) * mu_ref[...]
    width = rw_width
    r_ref[...] = xs[:, 0:width]
    k_ref[...] = xs[:, width:2 * width]
    vr_ref[...] = xs[:, 2 * width:3 * width]
    xwa = xs[:, 3 * width:3 * width + LANES]
    xg = xs[:, 3 * width + LANES:]
    z = w0_ref[...] + _bdot(jnp.tanh(xwa), w2_ref[...])
    w = -(jnp.maximum(-z, 0.0) + jnp.log(1.0 + jnp.exp(-jnp.abs(z)))) - 0.5
    ld_ref[...] = -jnp.exp(w)
    a_ref[...] = _sigmoid(a0_ref[...] + _bdot(xwa, a2_ref[...]))
    gr_ref[...] = _bdot(_sigmoid(xg), g2_ref[...])

    gate_ref[...] = _sigmoid(proj(h, c_rw + rw_cols, gate_ref.shape[1]) + bg_ref[...]).astype(gate_ref.dtype)


def _mixer_in(x2, pos2, mod, norm1_g, w_cat, b_gate, q_norm_g, k_norm_g, mu, w_decay0, w_decay2, a0, a2, g2,
              S, da_width, rw_width, tm=512):
    N, D = x2.shape
    n_gate = b_gate.shape[0]
    rw_cols = mu.shape[0]
    tpb = S // tm
    d = DA_HEAD_DIM
    inv_freq = 1.0 / (ROPE_THETA ** (jnp.arange(0, d, 2, dtype=F32) / d))
    invf = jnp.tile(inv_freq, LANES // (d // 2)).reshape(1, LANES)
    dl, al = w_decay2.shape[0], a2.shape[0]
    assert dl + al == LANES and g2.shape[0] == LANES
    w2p = jnp.zeros((LANES, rw_width), F32).at[:dl].set(w_decay2)
    a2p = jnp.zeros((LANES, rw_width), F32).at[dl:].set(a2)
    kern = functools.partial(_mixer_in_kernel, da_width=da_width, rw_cols=rw_cols, rw_width=rw_width,
                             q_scale=d ** -0.5 * math.log2(math.e),
                             tiles_per_seq=tpb)
    row = lambda w: pl.BlockSpec((tm, w), lambda i: (i, 0))
    vec = lambda n: pl.BlockSpec((1, n), lambda i: (0, 0))
    mat = pl.BlockSpec((LANES, rw_width), lambda i: (0, 0))
    f32 = lambda w: jax.ShapeDtypeStruct((N, w), F32)
    bf16 = lambda w: jax.ShapeDtypeStruct((N, w), BF16)
    return pl.pallas_call(
        kern,
        grid=(N // tm,),
        in_specs=[row(D),
                  pl.BlockSpec((8, D), lambda i: (jnp.maximum(i * (tm // 8) - 1, 0), 0)),
                  pl.BlockSpec((None, 6, D), lambda i: (i // tpb, 0, 0)),
                  vec(D),
                  pl.BlockSpec(w_cat.shape, lambda i: (0, 0)),
                  vec(n_gate),
                  pl.BlockSpec((tm, 1), lambda i: (i, 0)),
                  vec(LANES), vec(LANES), vec(LANES),
                  vec(rw_cols), vec(rw_width), mat, vec(rw_width), mat, mat],
        out_specs=[row(da_width)] * 3 + [row(rw_width)] * 6 + [row(n_gate)],
        out_shape=[bf16(da_width)] * 3 + [f32(rw_width)] * 6 + [bf16(n_gate)],
        compiler_params=_cp("parallel"),
        name="mixer_in",
    )(x2, x2, mod, norm1_g.reshape(1, D), w_cat, b_gate.reshape(1, n_gate), pos2, invf,
      jnp.tile(q_norm_g, 2).reshape(1, LANES), jnp.tile(k_norm_g, 2).reshape(1, LANES),
      mu.reshape(1, rw_cols), w_decay0.reshape(1, rw_width), w2p, a0.reshape(1, rw_width), a2p, g2)


def _attn_kernel(flag_ref, q_ref, k_ref, v_ref, vt_ref, lam_ref, sg_ref, sgc_ref, o_ref,
                 qz_ref, m_ref, l_ref, acc_ref, lpt_ref, acct_ref, *, tq, lambda_init):
    i = pl.program_id(2)
    lane = lax.broadcasted_iota(I32, (tq, LANES), 1)
    q = q_ref[...]
    zero = jnp.zeros_like(q)
    qz_ref[0:tq, :] = jnp.where(lane < DA_HEAD_DIM, q, zero)
    qz_ref[tq:, :] = jnp.where(lane >= DA_HEAD_DIM, q, zero)
    bounded = flag_ref[0] == 1
    lv = lam_ref[...]
    lam = (jnp.exp(jnp.sum(lv[0:1] * lv[1:2], keepdims=True))
           - jnp.exp(jnp.sum(lv[2:3] * lv[3:4], keepdims=True)) + lambda_init)

    def run(step):
        def body(j, carry):
            step(j, False)
            return carry
        lax.fori_loop(0, i, body, 0)
        step(i, True)

    def plain_step(j, masked):
        off = pl.multiple_of(j * tq, tq)
        st = _nt(k_ref[pl.ds(off, tq), :], qz_ref[...])
        if masked:
            row = lax.broadcasted_iota(I32, st.shape, 0)
            col = lax.broadcasted_iota(I32, st.shape, 1)
            st = jnp.where((row // CHUNK) <= ((col & (tq - 1)) // CHUNK), st, NEG)
        pt = jnp.exp2(st)
        part = pt[0:8, :]
        for g in range(1, tq // 8):
            part = part + pt[8 * g:8 * g + 8, :]
        lpt_ref[...] += part
        acct_ref[...] += jnp.dot(vt_ref[:, pl.ds(off, tq)], pt.astype(BF16), preferred_element_type=F32)

    @pl.when(bounded)
    def _():
        lpt_ref[...] = jnp.zeros_like(lpt_ref)
        acct_ref[...] = jnp.zeros_like(acct_ref)
        run(plain_step)
        lsum = jnp.sum(lpt_ref[...], axis=0, keepdims=True)
        ot = acct_ref[:, 0:tq] / lsum[:, 0:tq] - lam * (acct_ref[:, tq:] / lsum[:, tq:])
        ot = ot * lax.rsqrt(jnp.mean(ot * ot, axis=0, keepdims=True) + SUBLN_EPS) * sgc_ref[...]
        o_ref[...] = (ot * (1.0 - lambda_init)).T.astype(o_ref.dtype)

    def online_step(j, masked):
        off = pl.multiple_of(j * tq, tq)
        s = _nt(qz_ref[...], k_ref[pl.ds(off, tq), :])
        if masked:
            row = lax.broadcasted_iota(I32, s.shape, 0)
            col = lax.broadcasted_iota(I32, s.shape, 1)
            s = jnp.where((col // CHUNK) <= ((row & (tq - 1)) // CHUNK), s, NEG)
        m_old = m_ref[...]
        m_new = jnp.maximum(m_old, jnp.max(s, axis=-1, keepdims=True))
        alpha = jnp.exp2(m_old - m_new)
        pr = jnp.exp2(s - m_new)
        l_ref[...] = alpha * l_ref[...] + jnp.sum(pr, axis=-1, keepdims=True)
        acc_ref[...] = alpha * acc_ref[...] + jnp.dot(pr.astype(BF16), v_ref[pl.ds(off, tq), :],
                                                      preferred_element_type=F32)
        m_ref[...] = m_new

    @pl.when(jnp.logical_not(bounded))
    def _():
        m_ref[...] = jnp.full_like(m_ref, NEG)
        l_ref[...] = jnp.zeros_like(l_ref)
        acc_ref[...] = jnp.zeros_like(acc_ref)
        run(online_step)
        o = acc_ref[0:tq, :] / l_ref[0:tq, :] - lam * (acc_ref[tq:, :] / l_ref[tq:, :])
        o = o * lax.rsqrt(jnp.mean(o * o, axis=-1, keepdims=True) + SUBLN_EPS) * sg_ref[...]
        o_ref[...] = (o * (1.0 - lambda_init)).astype(o_ref.dtype)


def _diff_attention(qn, kn, v, score_bound, lam_vecs, subln_g, B, S, lambda_init, tq=512):
    W = qn.shape[1]
    H = W // LANES
    q3 = qn.reshape(B, S, W)
    k3 = kn.reshape(B, S, W)
    v3 = v.reshape(B, S, W)
    vt3 = v3.transpose(0, 2, 1)
    flag = (score_bound <= MAX_PLAIN_SCORE).astype(I32).reshape(1)
    qblk = pl.BlockSpec((None, tq, LANES), lambda b, h, i, f: (b, i, h))
    kvblk = pl.BlockSpec((None, S, LANES), lambda b, h, i, f: (b, 0, h))
    grid_spec = pltpu.PrefetchScalarGridSpec(
        num_scalar_prefetch=1,
        grid=(B, H, S // tq),
        in_specs=[qblk, kvblk, kvblk,
                  pl.BlockSpec((None, LANES, S), lambda b, h, i, f: (b, h, 0)),
                  pl.BlockSpec((4, DA_HEAD_DIM), lambda b, h, i, f: (0, 0)),
                  pl.BlockSpec((1, LANES), lambda b, h, i, f: (0, 0)),
                  pl.BlockSpec((LANES, 1), lambda b, h, i, f: (0, 0))],
        out_specs=qblk,
        scratch_shapes=[pltpu.VMEM((2 * tq, LANES), BF16),
                        pltpu.VMEM((2 * tq, 1), F32),
                        pltpu.VMEM((2 * tq, 1), F32),
                        pltpu.VMEM((2 * tq, LANES), F32),
                        pltpu.VMEM((8, 2 * tq), F32),
                        pltpu.VMEM((LANES, 2 * tq), F32)],
    )
    out = pl.pallas_call(
        functools.partial(_attn_kernel, tq=tq, lambda_init=lambda_init),
        grid_spec=grid_spec,
        out_shape=jax.ShapeDtypeStruct((B, S, W), BF16),
        compiler_params=_cp("parallel", "parallel", "arbitrary"),
        name="diff_attn",
    )(flag, q3, k3, v3, vt3, lam_vecs, subln_g.reshape(1, LANES), subln_g.reshape(LANES, 1))
    return out.reshape(B * S, W)


def _stackmask(m):
    lane = lax.broadcasted_iota(I32, m.shape, 1)
    z = jnp.zeros_like(m)
    return jnp.concatenate([jnp.where(lane < RWKV_HEAD, m, z), jnp.where(lane >= RWKV_HEAD, m, z)], axis=0)


def _pair_sum(x, first):
    s1 = jnp.sum(jnp.where(first, x, 0.0), axis=-1, keepdims=True)
    s2 = jnp.sum(jnp.where(first, 0.0, x), axis=-1, keepdims=True)
    return jnp.where(first, s1, s2)


def _rwkv_scan_kernel(r_ref, k_ref, v_ref, a_ref, ld_ref, g_ref, kk_ref, ka_ref, rk_ref, lnw_ref, lnb_ref,
                      o_ref, s_ref, *, L):
    tm, W = r_ref.shape
    n_chunks = tm // L
    n_pairs = W // LANES
    hd = RWKV_HEAD
    bf = lambda t: t.astype(BF16)

    @pl.when(pl.program_id(1) == 0)
    def _():
        s_ref[...] = jnp.zeros_like(s_ref)

    row = lax.broadcasted_iota(I32, (tm, tm), 0)
    col = lax.broadcasted_iota(I32, (tm, tm), 1)
    tri = jnp.where(jnp.logical_and(col <= row, (col // L) == (row // L)), 1.0, 0.0).astype(BF16)
    ld = ld_ref[...]
    ld_hi = bf(ld)
    rem = ld - ld_hi.astype(F32)
    ld_mid = bf(rem)
    ld_lo = bf(rem - ld_mid.astype(F32))
    c = (jnp.dot(tri, ld_hi, preferred_element_type=F32) + jnp.dot(tri, ld_mid, preferred_element_type=F32)
         + jnp.dot(tri, ld_lo, preferred_element_type=F32))
    ec = jnp.exp(c)
    eci = jnp.exp(-c)
    ecm = jnp.exp(c - ld)
    r = r_ref[...]
    k = k_ref[...]
    v = v_ref[...]
    a = a_ref[...]
    kkr = k * kk_ref[...]
    kmod = k * (1.0 + (a - 1.0) * ka_ref[...])
    brk = r * kmod * rk_ref[...]

    lane = lax.broadcasted_iota(I32, (L, LANES), 1)
    rowl = lax.broadcasted_iota(I32, (L, LANES), 0)
    first = lane < hd
    lane_h = lane & (hd - 1)
    strict = lane_h < rowl
    incl = lane_h <= rowl
    eye = jnp.where(lane_h == rowl, 1.0, 0.0)

    chains = [(ch, p) for ch in range(n_chunks) for p in range(n_pairs)]
    rsl = lambda ch: slice(ch * L, (ch + 1) * L)
    csl = lambda p: slice(p * LANES, (p + 1) * LANES)
    fdot = lambda x, y: jnp.dot(x, y, preferred_element_type=F32)
    at, bt, kt, rt, vh, g_l = {}, {}, {}, {}, {}, {}
    for c_ in chains:
        ch, p = c_
        rs, cs = rsl(ch), csl(p)
        kkh = kkr[rs, cs]
        kkh = kkh / jnp.maximum(jnp.sqrt(_pair_sum(kkh * kkh, first)), 1e-12)
        vh[c_] = v[rs, cs]
        g_l[c_] = ec[ch * L + L - 1:ch * L + L, cs]
        at[c_] = -kkh * ecm[rs, cs]
        bt[c_] = kkh * a[rs, cs] * eci[rs, cs]
        kt[c_] = kmod[rs, cs] * eci[rs, cs]
        rt[c_] = r[rs, cs] * ec[rs, cs]
    gm = {c_: _nt(bf(jnp.concatenate([at[c_], rt[c_]], axis=0)),
                  jnp.concatenate([_stackmask(bf(bt[c_])), _stackmask(bf(kt[c_]))], axis=0)) for c_ in chains}
    a_ab = {c_: jnp.where(strict, gm[c_][:L, :LANES], 0.0) for c_ in chains}
    vsm = {c_: _stackmask(bf(vh[c_])) for c_ in chains}
    cmat = {c_: fdot(bf(jnp.where(strict, gm[c_][:L, LANES:], 0.0)), vsm[c_]) for c_ in chains}
    t_inv = {c_: eye + a_ab[c_] for c_ in chains}
    pw = {c_: bf(a_ab[c_]) for c_ in chains}
    for _ in range(int(math.log2(L)) - 1):
        pw = {c_: bf(fdot(pw[c_], _stackmask(pw[c_]))) for c_ in chains}
        t_inv = {c_: t_inv[c_] + fdot(pw[c_], _stackmask(bf(t_inv[c_]))) for c_ in chains}
    zz = {c_: fdot(bf(t_inv[c_]), jnp.concatenate([_stackmask(bf(at[c_])), _stackmask(bf(cmat[c_]))], axis=1))
          for c_ in chains}
    qy = {c_: fdot(bf(jnp.where(incl, gm[c_][L:, :LANES], 0.0)),
                   jnp.concatenate([_stackmask(bf(zz[c_][:, :LANES])), _stackmask(bf(zz[c_][:, LANES:]))], axis=1))
          for c_ in chains}
    y0 = {c_: qy[c_][:, LANES:] + fdot(bf(jnp.where(incl, gm[c_][L:, LANES:], 0.0)), vsm[c_]) for c_ in chains}
    qa = {c_: bf(jnp.concatenate([rt[c_] + qy[c_][:, :LANES], zz[c_][:, :LANES]], axis=0)) for c_ in chains}
    bkg = {c_: bf(jnp.concatenate([bt[c_] * g_l[c_], kt[c_] * g_l[c_]], axis=0)) for c_ in chains}

    lane_s = lax.broadcasted_iota(I32, (hd, LANES), 1)
    sp = [s_ref[p] for p in range(n_pairs)]
    for ch in range(n_chunks):
        rs = rsl(ch)
        yw = [_nt(qa[ch, p], _stackmask(bf(sp[p]))) for p in range(n_pairs)]
        upd = [_tn(bf(jnp.concatenate([yw[p][L:] + zz[ch, p][:, LANES:], vh[ch, p]], axis=0)), bkg[ch, p])
               for p in range(n_pairs)]
        for p in range(n_pairs):
            cs = csl(p)
            sp[p] = sp[p] * g_l[ch, p] + jnp.where(lane_s < hd, upd[p][:hd], upd[p][hd:])
            y = yw[p][:L] + y0[ch, p]
            mean = _pair_sum(y, first) * (1.0 / hd)
            yc = y - mean
            var = _pair_sum(yc * yc, first) * (1.0 / hd)
            yn = yc * lax.rsqrt(var + GN_EPS) * lnw_ref[:, cs] + lnb_ref[:, cs]
            bonus = _pair_sum(brk[rs, cs], first) * vh[ch, p]
            o_ref[rs, cs] = ((yn + bonus) * g_ref[rs, cs]).astype(o_ref.dtype)
    for p in range(n_pairs):
        s_ref[p] = sp[p]


def _rwkv_scan(r, k, v, a, ld, g, k_k, k_a, r_k, ln_w, ln_b, L=RWKV_CHUNK, tm=512):
    B, S, W = r.shape
    seq = pl.BlockSpec((None, tm, W), lambda b, c: (b, c, 0))
    vec = pl.BlockSpec((1, W), lambda b, c: (0, 0))
    return pl.pallas_call(
        functools.partial(_rwkv_scan_kernel, L=L),
        grid=(B, S // tm),
        in_specs=[seq] * 6 + [vec] * 5,
        out_specs=seq,
        out_shape=jax.ShapeDtypeStruct((B, S, W), BF16),
        scratch_shapes=[pltpu.VMEM((W // LANES, RWKV_HEAD, LANES), F32)],
        compiler_params=_cp("parallel", "arbitrary"),
        name="rwkv_scan",
    )(r, k, v, a, ld, g, k_k.reshape(1, W), k_a.reshape(1, W), r_k.reshape(1, W),
      ln_w.reshape(1, W), ln_b.reshape(1, W))


def _post_kernel(attn_ref, rw_ref, gate_ref, x_ref, mod_ref, wa_ref, wb_ref, wo_ref, g2_ref, wrh_ref, wrm_ref,
                 sug_ref, sd_ref, base_ref, h2p_ref, lg_ref):
    D = x_ref.shape[1]
    ya = jnp.dot(attn_ref[...], wa_ref[...], preferred_element_type=F32)
    yb = jnp.dot(rw_ref[...], wb_ref[...], preferred_element_type=F32)
    m = gate_ref[:, 0:D] * ya + gate_ref[:, D:] * yb
    x1 = x_ref[...] + mod_ref[2:3, :] * jnp.dot(m.astype(BF16), wo_ref[...], preferred_element_type=F32)
    y = x1 * lax.rsqrt(jnp.mean(x1 * x1, axis=-1, keepdims=True) + NORM_EPS) * g2_ref[...]
    h2 = y * (1.0 + mod_ref[4:5, :]) + mod_ref[3:4, :]
    hb = h2.astype(BF16)
    hm = (h2 - hb.astype(F32)).astype(BF16)
    lg_ref[...] = _nt(wrh_ref[...], hb) + _nt(wrh_ref[...], hm) + _nt(wrm_ref[...], hb)
    packed = _pack_halves(hb.astype(F32))
    half = packed.shape[1] // 2
    h2p_ref[0] = packed[:, :half]
    h2p_ref[1] = packed[:, half:]
    F = sd_ref.shape[0]
    gu = jnp.dot(hb, sug_ref[...], preferred_element_type=F32)
    shared = jnp.dot((_silu(gu[:, :F]) * gu[:, F:]).astype(BF16), sd_ref[...], preferred_element_type=F32)
    base_ref[...] = x1 + mod_ref[5:6, :] * shared


def _post(attn, rw, gate, x2, mod, wa, wb, wo, norm2_g, w_router_t, sug, sd, S, tm=512):
    N, D = x2.shape
    E = w_router_t.shape[0]
    wr_hi = w_router_t.astype(BF16)
    wr_mid = (w_router_t - wr_hi.astype(F32)).astype(BF16)
    tpb = S // tm
    row = lambda w: pl.BlockSpec((tm, w), lambda i: (i, 0))
    full = lambda a: pl.BlockSpec(a.shape, lambda i: (0, 0))
    return pl.pallas_call(
        _post_kernel,
        grid=(N // tm,),
        in_specs=[row(attn.shape[1]), row(rw.shape[1]), row(gate.shape[1]), row(D),
                  pl.BlockSpec((None, 6, D), lambda i: (i // tpb, 0, 0)),
                  full(wa), full(wb), full(wo), pl.BlockSpec((1, D), lambda i: (0, 0)), full(wr_hi), full(wr_mid),
                  full(sug), full(sd)],
        out_specs=[row(D), pl.BlockSpec((2, tm, D // 4), lambda i: (0, i, 0)), pl.BlockSpec((E, tm), lambda i: (0, i))],
        out_shape=[jax.ShapeDtypeStruct((N, D), F32), jax.ShapeDtypeStruct((2, N, D // 4), U32),
                   jax.ShapeDtypeStruct((E, N), F32)],
        compiler_params=_cp("parallel"),
        name="post_mixer",
    )(attn, rw, gate, x2, mod, wa, wb, wo, norm2_g.reshape(1, D), wr_hi, wr_mid, sug, sd)


def _first_argmax(x, idx, sentinel):
    m = jnp.max(x, axis=0, keepdims=True)
    return m, jnp.min(jnp.where(x == m, idx, sentinel), axis=0, keepdims=True)


def _route_kernel(lg_ref, bias_ref, e_ref, w_ref):
    E, T = lg_ref.shape
    gsz = E // N_GROUPS
    scores = _sigmoid(lg_ref[...])
    biased = scores + bias_ref[...]
    ig = lax.broadcasted_iota(I32, (gsz, T), 0)
    gs = []
    for g in range(N_GROUPS):
        blk = biased[g * gsz:(g + 1) * gsz, :]
        m1, i1 = _first_argmax(blk, ig, gsz)
        m2 = jnp.max(jnp.where(ig == i1, -jnp.inf, blk), axis=0, keepdims=True)
        gs.append(m1 + m2)
    gsc = jnp.concatenate(gs, axis=0)
    i8 = lax.broadcasted_iota(I32, (N_GROUPS, T), 0)
    chosen = jnp.zeros((N_GROUPS, T), F32)
    for _ in range(TOPK_GROUPS):
        _, gi = _first_argmax(gsc, i8, N_GROUPS)
        hit = i8 == gi
        chosen = jnp.where(hit, 1.0, chosen)
        gsc = jnp.where(hit, -jnp.inf, gsc)
    masked = jnp.concatenate(
        [jnp.where(chosen[g:g + 1, :] > 0.0, biased[g * gsz:(g + 1) * gsz, :], -jnp.inf)
         for g in range(N_GROUPS)], axis=0)
    ie = lax.broadcasted_iota(I32, (E, T), 0)
    idxs, wts = [], []
    for _ in range(TOP_K):
        _, ei = _first_argmax(masked, ie, E)
        hit = ie == ei
        idxs.append(ei)
        wts.append(jnp.sum(jnp.where(hit, scores, 0.0), axis=0, keepdims=True))
        masked = jnp.where(hit, -jnp.inf, masked)
    wt = jnp.concatenate(wts, axis=0)
    e_ref[...] = jnp.concatenate(idxs, axis=0)
    w_ref[...] = wt / jnp.sum(wt, axis=0, keepdims=True) * ROUTED_SCALE


def _route(logits_t, router_bias, T=512):
    E, N = logits_t.shape
    blk = pl.BlockSpec((TOP_K, T), lambda i: (0, i))
    return pl.pallas_call(
        _route_kernel,
        grid=(N // T,),
        in_specs=[pl.BlockSpec((E, T), lambda i: (0, i)), pl.BlockSpec((E, 1), lambda i: (0, 0))],
        out_specs=[blk, blk],
        out_shape=[jax.ShapeDtypeStruct((TOP_K, N), I32), jax.ShapeDtypeStruct((TOP_K, N), F32)],
        compiler_params=_cp("parallel"),
        name="route",
    )(logits_t, router_bias.reshape(E, 1))


def _rank_kernel(e_ref, rank_ref, cnt_ref, carry_ref, *, n_experts):
    T = e_ref.shape[1]

    @pl.when(pl.program_id(0) == 0)
    def _():
        carry_ref[...] = jnp.zeros_like(carry_ref)

    ie = lax.broadcasted_iota(I32, (n_experts, T), 0)
    e = e_ref[...]
    hits = [ie == e[kk:kk + 1, :] for kk in range(TOP_K)]
    onehot = jnp.zeros((n_experts, T), F32)
    for hsel in hits:
        onehot = onehot + hsel.astype(F32)
    tr = lax.broadcasted_iota(I32, (T, T), 0)
    tc = lax.broadcasted_iota(I32, (T, T), 1)
    before = (tr < tc).astype(BF16)
    base = _bdot(onehot, before) + carry_ref[:, 0:1]
    rank_ref[...] = jnp.concatenate(
        [jnp.sum(jnp.where(hsel, base, 0.0), axis=0, keepdims=True) for hsel in hits], axis=0).astype(I32)
    carry_ref[...] = carry_ref[...] + jnp.sum(onehot, axis=1, keepdims=True)
    cnt_ref[...] = carry_ref[...]


def _ranks(eidx_t, n_experts, T=512):
    N = eidx_t.shape[1]
    blk = pl.BlockSpec((TOP_K, T), lambda i: (0, i))
    cnt = pl.BlockSpec((n_experts, LANES), lambda i: (0, 0))
    return pl.pallas_call(
        functools.partial(_rank_kernel, n_experts=n_experts),
        grid=(N // T,),
        in_specs=[blk],
        out_specs=[blk, cnt],
        out_shape=[jax.ShapeDtypeStruct((TOP_K, N), I32), jax.ShapeDtypeStruct((n_experts, LANES), F32)],
        scratch_shapes=[pltpu.VMEM((n_experts, LANES), F32)],
        compiler_params=_cp("arbitrary"),
        name="ranks",
    )(eidx_t)


def _dest_kernel(e_ref, rank_ref, start_ref, d_ref):
    E = start_ref.shape[0]
    T = e_ref.shape[1]
    ie = lax.broadcasted_iota(I32, (E, T), 0)
    e = e_ref[...]
    start = start_ref[:, 0:1]
    rows = [jnp.sum(jnp.where(ie == e[kk:kk + 1, :], start, 0.0), axis=0, keepdims=True) for kk in range(TOP_K)]
    d_ref[...] = jnp.concatenate(rows, axis=0).astype(I32) + rank_ref[...]


def _dests(eidx_t, rank_t, pstart, T=512):
    N = eidx_t.shape[1]
    E = pstart.shape[0]
    blk = pl.BlockSpec((TOP_K, T), lambda i: (0, i))
    return pl.pallas_call(
        _dest_kernel,
        grid=(N // T,),
        in_specs=[blk, blk, pl.BlockSpec((E, LANES), lambda i: (0, 0))],
        out_specs=blk,
        out_shape=jax.ShapeDtypeStruct((TOP_K, N), I32),
        compiler_params=_cp("parallel"),
        name="dests",
    )(eidx_t, rank_t, jnp.broadcast_to(pstart.astype(F32)[:, None], (E, LANES)))


def _expert_kernel(us_ref, ps_ref, x_hbm, wug_ref, wd_ref, y_hbm,
                   wug_bf, wd_bf, xbuf, ybuf, cnt_ref, xsem, ysem, *, n_rows):
    e = pl.program_id(0)
    _, _, R, Ch = xbuf.shape
    C = 2 * Ch
    F = wd_ref.shape[0]

    def x_copy(slot, half, row):
        return pltpu.make_async_copy(x_hbm.at[half, pl.ds(row, R)], xbuf.at[slot, half], xsem.at[slot])

    def y_copy(slot, half, row):
        return pltpu.make_async_copy(ybuf.at[slot, half], y_hbm.at[half, pl.ds(row, R)], ysem.at[slot])

    def start(copy, slot, row):
        for half in range(2):
            copy(slot, half, row).start(priority=1)

    def wait(copy, slot):
        for half in range(2):
            copy(slot, half, 0).wait()

    @pl.when(e == 0)
    def _():
        cnt_ref[0] = 0
        for t in range(X_AHEAD):
            start(x_copy, t, t * R)
        ybuf[...] = jnp.zeros_like(ybuf)
        for slot in range(2):
            start(y_copy, slot, n_rows + slot * R)

    wug_bf[...] = wug_ref[...].astype(BF16)
    wd_bf[...] = wd_ref[...].astype(BF16)
    n_valid = us_ref[e + 1] - us_ref[e]
    pbase = ps_ref[e]
    n_tiles = lax.shift_right_logical(n_valid + (R - 1), int(math.log2(R)))

    def make_tile(nb):
        rowid = lax.broadcasted_iota(I32, (nb * R, C), 0)

        def tile(blk0):
            n_done = cnt_ref[0]
            xslots = [lax.rem(n_done + b, X_SLOTS) for b in range(nb)]
            for b in range(nb):
                ahead = n_done + X_AHEAD + b
                start(x_copy, lax.rem(ahead, X_SLOTS), pl.multiple_of(ahead * R, R))
            for b in range(nb):
                wait(x_copy, xslots[b])
            x = jnp.concatenate([jnp.concatenate([xbuf[xs, 0], xbuf[xs, 1]], axis=1) for xs in xslots], axis=0)
            x = jnp.where(rowid < n_valid - blk0 * R, x, jnp.uint32(0))
            lo, hi = _unpack_halves(x)
            gu = (jnp.dot(lo.astype(BF16), wug_bf[0:C, :], preferred_element_type=F32)
                  + jnp.dot(hi.astype(BF16), wug_bf[C:, :], preferred_element_type=F32))
            hid = (_silu(gu[:, :F]) * gu[:, F:]).astype(BF16)
            y = jnp.dot(hid, wd_bf[...], preferred_element_type=F32)
            packed = _pack_halves(y.astype(BF16).astype(F32))
            for b in range(nb):
                slot = (n_done + b) & 1
                wait(y_copy, slot)
                ybuf[slot, 0] = packed[b * R:(b + 1) * R, :Ch]
                ybuf[slot, 1] = packed[b * R:(b + 1) * R, Ch:]
                start(y_copy, slot, pl.multiple_of(pbase + (blk0 + b) * R, R))
            cnt_ref[0] = n_done + nb

        return tile

    pair_tile, single_tile = make_tile(2), make_tile(1)

    def pairs(j, carry):
        pair_tile(2 * j)
        return carry

    lax.fori_loop(0, lax.shift_right_logical(n_tiles, 1), pairs, 0)

    @pl.when((n_tiles & 1) == 1)
    def _():
        single_tile(n_tiles - 1)

    @pl.when(e == pl.num_programs(0) - 1)
    def _():
        for t in range(X_AHEAD):
            wait(x_copy, lax.rem(cnt_ref[0] + t, X_SLOTS))
        for slot in range(2):
            wait(y_copy, slot)
        ybuf[0] = jnp.zeros((2, R, Ch), U32)
        first = lax.shift_right_logical(pbase + n_tiles * R, int(math.log2(R)))
        n_left = n_rows // R - first

        def fill(t, carry):
            start(y_copy, 0, pl.multiple_of((first + t) * R, R))
            return carry

        def drain(t, carry):
            wait(y_copy, 0)
            return carry

        lax.fori_loop(0, n_left, fill, 0)
        lax.fori_loop(0, n_left, drain, 0)


def _experts(ustart, pstart, xg, w_ug, w_d, n_rows, R):
    _, _, Ch = xg.shape
    E, D, F2 = w_ug.shape
    F = w_d.shape[1]
    grid_spec = pltpu.PrefetchScalarGridSpec(
        num_scalar_prefetch=2,
        grid=(E,),
        in_specs=[pl.BlockSpec(memory_space=pl.ANY),
                  pl.BlockSpec((None, D, F2), lambda e, us, ps: (e, 0, 0)),
                  pl.BlockSpec((None, F, D), lambda e, us, ps: (e, 0, 0))],
        out_specs=pl.BlockSpec(memory_space=pl.ANY),
        scratch_shapes=[pltpu.VMEM((D, F2), BF16), pltpu.VMEM((F, D), BF16),
                        pltpu.VMEM((X_SLOTS, 2, R, Ch), U32), pltpu.VMEM((2, 2, R, Ch), U32), pltpu.SMEM((1,), I32),
                        pltpu.SemaphoreType.DMA((X_SLOTS,)), pltpu.SemaphoreType.DMA((2,))],
    )
    return pl.pallas_call(
        functools.partial(_expert_kernel, n_rows=n_rows),
        grid_spec=grid_spec,
        out_shape=jax.ShapeDtypeStruct((2, n_rows + 2 * R, Ch), U32),
        compiler_params=_cp("arbitrary"),
        name="experts",
    )(ustart, pstart, xg, w_ug, w_d)


def _sc_scatter_rows(src, idx, n_rows):
    H, N, C = src.shape
    K = idx.shape[0]
    per_row = N // SC_WINDOW
    mesh = plsc.VectorSubcoreMesh(core_axis_name="c", subcore_axis_name="s")

    @functools.partial(pl.kernel, out_type=jax.ShapeDtypeStruct((H, n_rows, C), src.dtype), mesh=mesh,
                       scratch_types=[])
    def scatter_kernel(x_hbm, i_hbm, o_hbm):
        for h in range(H):
            def body(x_vmem, i_vmem):
                pltpu.sync_copy(x_vmem, o_hbm.at[h].at[i_vmem.at[0]])

            pltpu.emit_pipeline(
                body,
                grid=(K * per_row,),
                in_specs=[pl.BlockSpec((SC_WINDOW, C), lambda i: (i % per_row, 0)),
                          pl.BlockSpec((1, SC_WINDOW), lambda i: (i // per_row, i % per_row))],
                out_specs=[],
                core_axis_name=("c", "s"),
                dimension_semantics=(pltpu.PARALLEL,),
            )(x_hbm.at[h], i_hbm)

    return scatter_kernel(src, idx)


def _sc_gather_rows(src, idx):
    H, _, C = src.shape
    K, N = idx.shape
    per_row = N // SC_WINDOW
    mesh = plsc.VectorSubcoreMesh(core_axis_name="c", subcore_axis_name="s")

    @functools.partial(pl.kernel, out_type=jax.ShapeDtypeStruct((H, K * N, C), src.dtype), mesh=mesh,
                       scratch_types=[])
    def gather_kernel(x_hbm, i_hbm, o_hbm):
        for h in range(H):
            def body(i_vmem, o_vmem):
                pltpu.sync_copy(x_hbm.at[h].at[i_vmem.at[0]], o_vmem)

            pltpu.emit_pipeline(
                body,
                grid=(K * per_row,),
                in_specs=[pl.BlockSpec((1, SC_WINDOW), lambda i: (i // per_row, i % per_row))],
                out_specs=[pl.BlockSpec((SC_WINDOW, C), lambda i: (i, 0))],
                core_axis_name=("c", "s"),
                dimension_semantics=(pltpu.PARALLEL,),
            )(i_hbm, o_hbm.at[h])

    return gather_kernel(src, idx)


def _combine_kernel(*refs):
    y_refs = refs[:2 * TOP_K]
    w_ref, base_ref, mod_ref, o_ref = refs[2 * TOP_K:]
    T = base_ref.shape[0]
    tr = lax.broadcasted_iota(I32, (T, T), 0)
    tc = lax.broadcasted_iota(I32, (T, T), 1)
    wcol = _nt((tr == tc).astype(F32), w_ref[...], precision=HI)
    acc = [None] * 4
    for kk in range(TOP_K):
        wk = wcol[:, kk:kk + 1]
        for half in range(2):
            lo, hi = _unpack_halves(y_refs[2 * kk + half][...])
            for q, val in ((half, lo), (2 + half, hi)):
                acc[q] = val * wk if acc[q] is None else acc[q] + val * wk
    o_ref[...] = base_ref[...] + mod_ref[5:6, :] * jnp.concatenate(acc, axis=1)


def _combine(yg, w_t, base, mod, S, T=512):
    N, D = base.shape
    C = yg.shape[2]
    tpb = S // T
    n_tiles = N // T
    row = pl.BlockSpec((T, D), lambda i: (i, 0))
    piece = lambda kk, half: pl.BlockSpec((None, T, C), lambda i: (half, kk * n_tiles + i, 0))
    return pl.pallas_call(
        _combine_kernel,
        grid=(n_tiles,),
        in_specs=[piece(kk, half) for kk in range(TOP_K) for half in range(2)] + [
            pl.BlockSpec((TOP_K, T), lambda i: (0, i)),
            row,
            pl.BlockSpec((None, 6, D), lambda i: (i // tpb, 0, 0))],
        out_specs=row,
        out_shape=jax.ShapeDtypeStruct((N, D), F32),
        compiler_params=_cp("parallel"),
        name="combine",
    )(*([yg] * (2 * TOP_K)), w_t, base, mod)


def _layer(x, c, positions, layer_idx, w_ada, b_ada, norm1_g, w_in, w_gate, b_gate,
           q_norm_g, k_norm_g, lambda_q1, lambda_k1, lambda_q2, lambda_k2, subln_g,
           rwkv_mu, w_decay0, w_decay2, a0, a2, g2, k_k, k_a, r_k, ln_x_w, ln_x_b,
           w_branch_a, w_branch_b, w_out, norm2_g, w_router, router_bias,
           w_expert_up_gate, w_expert_down, w_shared_up_gate, w_shared_down):
    B, S, D = x.shape
    N = B * S
    E = w_router.shape[1]
    da_width = w_branch_a.shape[0]
    rw_width = w_branch_b.shape[0]
    lambda_init = 0.8 - 0.6 * math.exp(-0.3 * layer_idx)

    mod = _adaln(c, w_ada, b_ada)
    x2 = x.reshape(N, D)
    w_cat = jnp.concatenate([w_in, w_gate], axis=1).astype(BF16)
    qn, kn, v, r_, k_, v_, a_, ld_, g_, gate = _mixer_in(
        x2, positions.reshape(N, 1), mod, norm1_g, w_cat, b_gate, q_norm_g, k_norm_g,
        rwkv_mu, w_decay0, w_decay2, a0, a2, g2, S, da_width, rw_width)

    lam_vecs = jnp.stack([lambda_q1, lambda_k1, lambda_q2, lambda_k2])
    score_bound = 1.01 * DA_HEAD_DIM ** 0.5 * jnp.max(jnp.abs(q_norm_g)) * jnp.max(jnp.abs(k_norm_g))
    attn = _diff_attention(qn, kn, v, score_bound, lam_vecs, subln_g, B, S, lambda_init)

    seq = lambda t: t.reshape(B, S, rw_width)
    rw = _rwkv_scan(seq(r_), seq(k_), seq(v_), seq(a_), seq(ld_), seq(g_), k_k, k_a, r_k.reshape(-1),
                    ln_x_w, ln_x_b).reshape(N, rw_width)

    base, h2p, logits_t = _post(attn, rw, gate, x2, mod, w_branch_a.astype(BF16), w_branch_b.astype(BF16),
                                w_out.astype(BF16), norm2_g, w_router.T,
                                w_shared_up_gate.astype(BF16), w_shared_down.astype(BF16), S)

    eidx_t, w_t = _route(logits_t, router_bias)
    rank_t, counts = _ranks(eidx_t, E)
    R = EXPERT_TILE
    cnt = counts[:, 0].astype(I32)
    ustart = jnp.concatenate([jnp.zeros((1,), I32), jnp.cumsum(cnt)])
    pcnt = (cnt + R - 1) // R * R
    pstart = jnp.cumsum(pcnt) - pcnt
    dest_p = _dests(eidx_t, rank_t, pstart)
    n_rows = (N * TOP_K + E * (R - 1) + R - 1) // R * R
    xg = _sc_scatter_rows(h2p, dest_p, n_rows + X_AHEAD * R)
    y = _experts(ustart, pstart, xg, w_expert_up_gate, w_expert_down, n_rows, R)
    yg = _sc_gather_rows(y, dest_p)
    out = _combine(yg, w_t, base, mod, S)
    return out.reshape(B, S, D)


def kernel(x, c, positions, w_ada, b_ada, norm1_g, w_in, w_gate, b_gate, q_norm_g, k_norm_g, lambda_q1, lambda_k1, lambda_q2, lambda_k2, subln_g, rwkv_mu, w_decay0, w_decay2, a0, a2, g2, k_k, k_a, r_k, ln_x_w, ln_x_b, w_branch_a, w_branch_b, w_out, norm2_g, w_router, router_bias, w_expert_up_gate, w_expert_down, w_shared_up_gate, w_shared_down):
    for l in range(w_ada.shape[0]):
        x = _layer(x, c, positions, l, w_ada[l], b_ada[l], norm1_g[l], w_in[l], w_gate[l], b_gate[l],
                   q_norm_g[l], k_norm_g[l], lambda_q1[l], lambda_k1[l], lambda_q2[l], lambda_k2[l],
                   subln_g[l], rwkv_mu[l], w_decay0[l], w_decay2[l], a0[l], a2[l], g2[l], k_k[l],
                   k_a[l], r_k[l], ln_x_w[l], ln_x_b[l], w_branch_a[l], w_branch_b[l], w_out[l],
                   norm2_g[l], w_router[l], router_bias[l], w_expert_up_gate[l], w_expert_down[l],
                   w_shared_up_gate[l], w_shared_down[l])
    return x
```

```python
import functools
import math

import jax
import jax.numpy as jnp
from jax import lax
from jax.experimental import pallas as pl
from jax.experimental.pallas import tpu as pltpu
from jax.experimental.pallas import tpu_sc as plsc

F32 = jnp.float32
BF16 = jnp.bfloat16
I32 = jnp.int32
U32 = jnp.uint32
HI = lax.Precision.HIGHEST

CHUNK = 64
ROPE_THETA = 10000.0
NORM_EPS = 1e-6
SUBLN_EPS = 1e-5
DA_HEAD_DIM = 64
RWKV_HEAD = 64
GN_EPS = 64e-5
TOP_K = 8
N_GROUPS = 8
TOPK_GROUPS = 4
ROUTED_SCALE = 2.5
EXPERT_TILE = 256
X_SLOTS = 6
X_AHEAD = X_SLOTS - 2
RWKV_CHUNK = 64
LANES = 128
SC_WINDOW = 128
NEG = -1e30
MAX_PLAIN_SCORE = 40.0
VMEM_LIMIT = 56 * 1024 * 1024


def _cp(*sem):
    return pltpu.CompilerParams(dimension_semantics=sem, vmem_limit_bytes=VMEM_LIMIT)


def _bdot(a, b):
    return jnp.dot(a.astype(BF16), b.astype(BF16), preferred_element_type=F32)


def _fdot(a, b):
    return jnp.dot(a, b, precision=HI, preferred_element_type=F32)


def _nt(a, b, precision=None):
    return lax.dot_general(a, b, (((1,), (1,)), ((), ())), precision=precision,
                           preferred_element_type=F32)


def _tn(a, b, precision=None):
    return lax.dot_general(a, b, (((0,), (0,)), ((), ())), precision=precision,
                           preferred_element_type=F32)


def _pack_halves(x):
    c = x.shape[1] // 2
    lo = lax.bitcast_convert_type(x[:, :c], U32)
    hi = lax.bitcast_convert_type(x[:, c:], U32)
    return (hi & jnp.uint32(0xFFFF0000)) | (lo >> 16)


def _unpack_halves(w):
    lo = lax.bitcast_convert_type(w << 16, F32)
    hi = lax.bitcast_convert_type(w & jnp.uint32(0xFFFF0000), F32)
    return lo, hi


def _sigmoid(x):
    return 1.0 / (1.0 + jnp.exp(-x))


def _silu(x):
    return x * _sigmoid(x)


def _ada_kernel(c_ref, w_ref, b_ref, o_ref):
    o_ref[...] = _fdot(_silu(c_ref[...]), w_ref[...]) + b_ref[...]


def _adaln(c, w_ada, b_ada):
    B, D = c.shape
    rows = -(-B // 8) * 8
    cpad = jnp.zeros((rows, D), F32).at[:B].set(c)
    n_out = w_ada.shape[1]
    out = pl.pallas_call(
        _ada_kernel,
        grid=(n_out // D,),
        in_specs=[pl.BlockSpec((rows, D), lambda j: (0, 0)),
                  pl.BlockSpec((D, D), lambda j: (0, j)),
                  pl.BlockSpec((1, D), lambda j: (0, j))],
        out_specs=pl.BlockSpec((rows, D), lambda j: (0, j)),
        out_shape=jax.ShapeDtypeStruct((rows, n_out), F32),
        compiler_params=_cp("arbitrary"),
        name="adaln",
    )(cpad, w_ada, b_ada.reshape(1, n_out))
    return out[:B].reshape(B, n_out // D, D)


def _mixer_in_kernel(x_ref, xprev_ref, mod_ref, g_ref, w_ref, bg_ref, pos_ref, invf_ref, qg_ref, kg_ref,
                     mu_ref, w0_ref, w2_ref, a0_ref, a2_ref, g2_ref,
                     qn_ref, kn_ref, v_ref, r_ref, k_ref, vr_ref, a_ref, ld_ref, gr_ref, gate_ref,
                     *, da_width, rw_cols, rw_width, q_scale, tiles_per_seq):
    tm = x_ref.shape[0]

    def modulated(x):
        y = x * lax.rsqrt(jnp.mean(x * x, axis=-1, keepdims=True) + NORM_EPS) * g_ref[...]
        return (y * (1.0 + mod_ref[1:2, :]) + mod_ref[0:1, :]).astype(BF16)

    def proj(hb, c0, width, step=512):
        parts = [jnp.dot(hb, w_ref[:, c0 + o:c0 + min(o + step, width)], preferred_element_type=F32)
                 for o in range(0, width, step)]
        return parts[0] if len(parts) == 1 else jnp.concatenate(parts, axis=1)

    h = modulated(x_ref[...])

    lane = lax.broadcasted_iota(I32, (tm, LANES), 1)
    first = lane < DA_HEAD_DIM
    lo_half = (lane & (DA_HEAD_DIM - 1)) < DA_HEAD_DIM // 2
    ang = pos_ref[...].astype(F32) * invf_ref[...]
    cos = jnp.cos(ang)
    sin = jnp.sin(ang)
    sin = jnp.where(lo_half, -sin, sin)
    for c0, dst, gn_ref, mult in ((0, qn_ref, qg_ref, q_scale), (da_width, kn_ref, kg_ref, 1.0)):
        raw = proj(h, c0, da_width)
        for blk in range(da_width // LANES):
            x = raw[:, blk * LANES:(blk + 1) * LANES]
            xx = x * x
            s_first = jnp.sum(jnp.where(first, xx, 0.0), axis=-1, keepdims=True)
            s_second = jnp.sum(jnp.where(first, 0.0, xx), axis=-1, keepdims=True)
            ms = jnp.where(first, s_first, s_second) * (1.0 / DA_HEAD_DIM)
            xn = x * lax.rsqrt(ms + NORM_EPS) * gn_ref[...]
            rot = jnp.where(lo_half, pltpu.roll(xn, LANES - DA_HEAD_DIM // 2, axis=1),
                            pltpu.roll(xn, DA_HEAD_DIM // 2, axis=1))
            dst[:, blk * LANES:(blk + 1) * LANES] = ((xn * cos + rot * sin) * mult).astype(dst.dtype)
    v_ref[...] = proj(h, 2 * da_width, da_width).astype(v_ref.dtype)

    c_rw = 3 * da_width
    p = proj(h, c_rw, rw_cols)
    p_before = proj(modulated(xprev_ref[...]), c_rw, rw_cols)
    seq_start = (pl.program_id(0) % tiles_per_seq) == 0
    last_prev = jnp.where(seq_start, 0.0, p_before[7:8, :])
    rowi = lax.broadcasted_iota(I32, p.shape, 0)
    prev = jnp.where(rowi == 0, last_prev, pltpu.roll(p, 1, axis=0))
    xs = p + (prev - p) * mu_ref[...]
    width = rw_width
    r_ref[...] = xs[:, 0:width]
    k_ref[...] = xs[:, width:2 * width]
    vr_ref[...] = xs[:, 2 * width:3 * width]
    xwa = xs[:, 3 * width:3 * width + LANES]
    xg = xs[:, 3 * width + LANES:]
    z = w0_ref[...] + _bdot(jnp.tanh(xwa), w2_ref[...])
    w = -(jnp.maximum(-z, 0.0) + jnp.log(1.0 + jnp.exp(-jnp.abs(z)))) - 0.5
    ld_ref[...] = -jnp.exp(w)
    a_ref[...] = _sigmoid(a0_ref[...] + _bdot(xwa, a2_ref[...]))
    gr_ref[...] = _bdot(_sigmoid(xg), g2_ref[...])

    gate_ref[...] = _sigmoid(proj(h, c_rw + rw_cols, gate_ref.shape[1]) + bg_ref[...]).astype(gate_ref.dtype)


def _mixer_in(x2, pos2, mod, norm1_g, w_cat, b_gate, q_norm_g, k_norm_g, mu, w_decay0, w_decay2, a0, a2, g2,
              S, da_width, rw_width, tm=512):
    N, D = x2.shape
    n_gate = b_gate.shape[0]
    rw_cols = mu.shape[0]
    tpb = S // tm
    d = DA_HEAD_DIM
    inv_freq = 1.0 / (ROPE_THETA ** (jnp.arange(0, d, 2, dtype=F32) / d))
    invf = jnp.tile(inv_freq, LANES // (d // 2)).reshape(1, LANES)
    dl, al = w_decay2.shape[0], a2.shape[0]
    assert dl + al == LANES and g2.shape[0] == LANES
    w2p = jnp.zeros((LANES, rw_width), F32).at[:dl].set(w_decay2)
    a2p = jnp.zeros((LANES, rw_width), F32).at[dl:].set(a2)
    kern = functools.partial(_mixer_in_kernel, da_width=da_width, rw_cols=rw_cols, rw_width=rw_width,
                             q_scale=d ** -0.5 * math.log2(math.e),
                             tiles_per_seq=tpb)
    row = lambda w: pl.BlockSpec((tm, w), lambda i: (i, 0))
    vec = lambda n: pl.BlockSpec((1, n), lambda i: (0, 0))
    mat = pl.BlockSpec((LANES, rw_width), lambda i: (0, 0))
    f32 = lambda w: jax.ShapeDtypeStruct((N, w), F32)
    bf16 = lambda w: jax.ShapeDtypeStruct((N, w), BF16)
    return pl.pallas_call(
        kern,
        grid=(N // tm,),
        in_specs=[row(D),
                  pl.BlockSpec((8, D), lambda i: (jnp.maximum(i * (tm // 8) - 1, 0), 0)),
                  pl.BlockSpec((None, 6, D), lambda i: (i // tpb, 0, 0)),
                  vec(D),
                  pl.BlockSpec(w_cat.shape, lambda i: (0, 0)),
                  vec(n_gate),
                  pl.BlockSpec((tm, 1), lambda i: (i, 0)),
                  vec(LANES), vec(LANES), vec(LANES),
                  vec(rw_cols), vec(rw_width), mat, vec(rw_width), mat, mat],
        out_specs=[row(da_width)] * 3 + [row(rw_width)] * 6 + [row(n_gate)],
        out_shape=[bf16(da_width)] * 3 + [f32(rw_width)] * 6 + [bf16(n_gate)],
        compiler_params=_cp("parallel"),
        name="mixer_in",
    )(x2, x2, mod, norm1_g.reshape(1, D), w_cat, b_gate.reshape(1, n_gate), pos2, invf,
      jnp.tile(q_norm_g, 2).reshape(1, LANES), jnp.tile(k_norm_g, 2).reshape(1, LANES),
      mu.reshape(1, rw_cols), w_decay0.reshape(1, rw_width), w2p, a0.reshape(1, rw_width), a2p, g2)


def _attn_kernel(flag_ref, q_ref, k_ref, v_ref, vt_ref, lam_ref, sg_ref, sgc_ref, o_ref,
                 qz_ref, m_ref, l_ref, acc_ref, lpt_ref, acct_ref, *, tq, lambda_init):
    i = pl.program_id(2)
    lane = lax.broadcasted_iota(I32, (tq, LANES), 1)
    q = q_ref[...]
    zero = jnp.zeros_like(q)
    qz_ref[0:tq, :] = jnp.where(lane < DA_HEAD_DIM, q, zero)
    qz_ref[tq:, :] = jnp.where(lane >= DA_HEAD_DIM, q, zero)
    bounded = flag_ref[0] == 1
    lv = lam_ref[...]
    lam = (jnp.exp(jnp.sum(lv[0:1] * lv[1:2], keepdims=True))
           - jnp.exp(jnp.sum(lv[2:3] * lv[3:4], keepdims=True)) + lambda_init)

    def run(step):
        def body(j, carry):
            step(j, False)
            return carry
        lax.fori_loop(0, i, body, 0)
        step(i, True)

    def plain_step(j, masked):
        off = pl.multiple_of(j * tq, tq)
        st = _nt(k_ref[pl.ds(off, tq), :], qz_ref[...])
        if masked:
            row = lax.broadcasted_iota(I32, st.shape, 0)
            col = lax.broadcasted_iota(I32, st.shape, 1)
            st = jnp.where((row // CHUNK) <= ((col & (tq - 1)) // CHUNK), st, NEG)
        pt = jnp.exp2(st)
        part = pt[0:8, :]
        for g in range(1, tq // 8):
            part = part + pt[8 * g:8 * g + 8, :]
        lpt_ref[...] += part
        acct_ref[...] += jnp.dot(vt_ref[:, pl.ds(off, tq)], pt.astype(BF16), preferred_element_type=F32)

    @pl.when(bounded)
    def _():
        lpt_ref[...] = jnp.zeros_like(lpt_ref)
        acct_ref[...] = jnp.zeros_like(acct_ref)

        def two_steps(p, carry):
            plain_step(2 * p, False)
            plain_step(2 * p + 1, False)
            return carry

        lax.fori_loop(0, lax.shift_right_logical(i, 1), two_steps, 0)

        @pl.when((i & 1) == 1)
        def _():
            plain_step(i - 1, False)
            plain_step(i, True)

        @pl.when((i & 1) == 0)
        def _():
            plain_step(i, True)

        lsum = jnp.sum(lpt_ref[...], axis=0, keepdims=True)
        ot = acct_ref[:, 0:tq] / lsum[:, 0:tq] - lam * (acct_ref[:, tq:] / lsum[:, tq:])
        ot = ot * lax.rsqrt(jnp.mean(ot * ot, axis=0, keepdims=True) + SUBLN_EPS) * sgc_ref[...]
        o_ref[...] = (ot * (1.0 - lambda_init)).T.astype(o_ref.dtype)

    def online_step(j, masked):
        off = pl.multiple_of(j * tq, tq)
        s = _nt(qz_ref[...], k_ref[pl.ds(off, tq), :])
        if masked:
            row = lax.broadcasted_iota(I32, s.shape, 0)
            col = lax.broadcasted_iota(I32, s.shape, 1)
            s = jnp.where((col // CHUNK) <= ((row & (tq - 1)) // CHUNK), s, NEG)
        m_old = m_ref[...]
        m_new = jnp.maximum(m_old, jnp.max(s, axis=-1, keepdims=True))
        alpha = jnp.exp2(m_old - m_new)
        pr = jnp.exp2(s - m_new)
        l_ref[...] = alpha * l_ref[...] + jnp.sum(pr, axis=-1, keepdims=True)
        acc_ref[...] = alpha * acc_ref[...] + jnp.dot(pr.astype(BF16), v_ref[pl.ds(off, tq), :],
                                                      preferred_element_type=F32)
        m_ref[...] = m_new

    @pl.when(jnp.logical_not(bounded))
    def _():
        m_ref[...] = jnp.full_like(m_ref, NEG)
        l_ref[...] = jnp.zeros_like(l_ref)
        acc_ref[...] = jnp.zeros_like(acc_ref)
        run(online_step)
        o = acc_ref[0:tq, :] / l_ref[0:tq, :] - lam * (acc_ref[tq:, :] / l_ref[tq:, :])
        o = o * lax.rsqrt(jnp.mean(o * o, axis=-1, keepdims=True) + SUBLN_EPS) * sg_ref[...]
        o_ref[...] = (o * (1.0 - lambda_init)).astype(o_ref.dtype)


def _diff_attention(qn, kn, v, score_bound, lam_vecs, subln_g, B, S, lambda_init, tq=512):
    W = qn.shape[1]
    H = W // LANES
    q3 = qn.reshape(B, S, W)
    k3 = kn.reshape(B, S, W)
    v3 = v.reshape(B, S, W)
    vt3 = v3.transpose(0, 2, 1)
    flag = (score_bound <= MAX_PLAIN_SCORE).astype(I32).reshape(1)
    qblk = pl.BlockSpec((None, tq, LANES), lambda b, h, i, f: (b, i, h))
    kvblk = pl.BlockSpec((None, S, LANES), lambda b, h, i, f: (b, 0, h))
    grid_spec = pltpu.PrefetchScalarGridSpec(
        num_scalar_prefetch=1,
        grid=(B, H, S // tq),
        in_specs=[qblk, kvblk, kvblk,
                  pl.BlockSpec((None, LANES, S), lambda b, h, i, f: (b, h, 0)),
                  pl.BlockSpec((4, DA_HEAD_DIM), lambda b, h, i, f: (0, 0)),
                  pl.BlockSpec((1, LANES), lambda b, h, i, f: (0, 0)),
                  pl.BlockSpec((LANES, 1), lambda b, h, i, f: (0, 0))],
        out_specs=qblk,
        scratch_shapes=[pltpu.VMEM((2 * tq, LANES), BF16),
                        pltpu.VMEM((2 * tq, 1), F32),
                        pltpu.VMEM((2 * tq, 1), F32),
                        pltpu.VMEM((2 * tq, LANES), F32),
                        pltpu.VMEM((8, 2 * tq), F32),
                        pltpu.VMEM((LANES, 2 * tq), F32)],
    )
    out = pl.pallas_call(
        functools.partial(_attn_kernel, tq=tq, lambda_init=lambda_init),
        grid_spec=grid_spec,
        out_shape=jax.ShapeDtypeStruct((B, S, W), BF16),
        compiler_params=_cp("parallel", "parallel", "arbitrary"),
        name="diff_attn",
    )(flag, q3, k3, v3, vt3, lam_vecs, subln_g.reshape(1, LANES), subln_g.reshape(LANES, 1))
    return out.reshape(B * S, W)


def _stackmask(m):
    lane = lax.broadcasted_iota(I32, m.shape, 1)
    z = jnp.zeros_like(m)
    return jnp.concatenate([jnp.where(lane < RWKV_HEAD, m, z), jnp.where(lane >= RWKV_HEAD, m, z)], axis=0)


def _pair_sum(x, first):
    s1 = jnp.sum(jnp.where(first, x, 0.0), axis=-1, keepdims=True)
    s2 = jnp.sum(jnp.where(first, 0.0, x), axis=-1, keepdims=True)
    return jnp.where(first, s1, s2)


def _rwkv_scan_kernel(r_ref, k_ref, v_ref, a_ref, ld_ref, g_ref, kk_ref, ka_ref, rk_ref, lnw_ref, lnb_ref,
                      o_ref, s_ref, *, L):
    tm, W = r_ref.shape
    n_chunks = tm // L
    n_pairs = W // LANES
    hd = RWKV_HEAD
    bf = lambda t: t.astype(BF16)

    @pl.when(pl.program_id(1) == 0)
    def _():
        s_ref[...] = jnp.zeros_like(s_ref)

    row = lax.broadcasted_iota(I32, (tm, tm), 0)
    col = lax.broadcasted_iota(I32, (tm, tm), 1)
    tri = jnp.where(jnp.logical_and(col <= row, (col // L) == (row // L)), 1.0, 0.0).astype(BF16)
    ld = ld_ref[...]
    ld_hi = bf(ld)
    rem = ld - ld_hi.astype(F32)
    ld_mid = bf(rem)
    ld_lo = bf(rem - ld_mid.astype(F32))
    c = (jnp.dot(tri, ld_hi, preferred_element_type=F32) + jnp.dot(tri, ld_mid, preferred_element_type=F32)
         + jnp.dot(tri, ld_lo, preferred_element_type=F32))
    ec = jnp.exp(c)
    eci = jnp.exp(-c)
    ecm = jnp.exp(c - ld)
    r = r_ref[...]
    k = k_ref[...]
    v = v_ref[...]
    a = a_ref[...]
    kkr = k * kk_ref[...]
    kmod = k * (1.0 + (a - 1.0) * ka_ref[...])
    brk = r * kmod * rk_ref[...]

    lane = lax.broadcasted_iota(I32, (L, LANES), 1)
    rowl = lax.broadcasted_iota(I32, (L, LANES), 0)
    first = lane < hd
    lane_h = lane & (hd - 1)
    strict = lane_h < rowl
    incl = lane_h <= rowl
    eye = jnp.where(lane_h == rowl, 1.0, 0.0)

    chains = [(ch, p) for ch in range(n_chunks) for p in range(n_pairs)]
    rsl = lambda ch: slice(ch * L, (ch + 1) * L)
    csl = lambda p: slice(p * LANES, (p + 1) * LANES)
    fdot = lambda x, y: jnp.dot(x, y, preferred_element_type=F32)
    at, bt, kt, rt, vh, g_l = {}, {}, {}, {}, {}, {}
    for c_ in chains:
        ch, p = c_
        rs, cs = rsl(ch), csl(p)
        kkh = kkr[rs, cs]
        kkh = kkh / jnp.maximum(jnp.sqrt(_pair_sum(kkh * kkh, first)), 1e-12)
        vh[c_] = v[rs, cs]
        g_l[c_] = ec[ch * L + L - 1:ch * L + L, cs]
        at[c_] = -kkh * ecm[rs, cs]
        bt[c_] = kkh * a[rs, cs] * eci[rs, cs]
        kt[c_] = kmod[rs, cs] * eci[rs, cs]
        rt[c_] = r[rs, cs] * ec[rs, cs]
    gm = {c_: _nt(bf(jnp.concatenate([at[c_], rt[c_]], axis=0)),
                  jnp.concatenate([_stackmask(bf(bt[c_])), _stackmask(bf(kt[c_]))], axis=0)) for c_ in chains}
    a_ab = {c_: jnp.where(strict, gm[c_][:L, :LANES], 0.0) for c_ in chains}
    vsm = {c_: _stackmask(bf(vh[c_])) for c_ in chains}
    cmat = {c_: fdot(bf(jnp.where(strict, gm[c_][:L, LANES:], 0.0)), vsm[c_]) for c_ in chains}
    t_inv = {c_: eye + a_ab[c_] for c_ in chains}
    pw = {c_: bf(a_ab[c_]) for c_ in chains}
    for _ in range(int(math.log2(L)) - 1):
        pw = {c_: bf(fdot(pw[c_], _stackmask(pw[c_]))) for c_ in chains}
        t_inv = {c_: t_inv[c_] + fdot(pw[c_], _stackmask(bf(t_inv[c_]))) for c_ in chains}
    zz = {c_: fdot(bf(t_inv[c_]), jnp.concatenate([_stackmask(bf(at[c_])), _stackmask(bf(cmat[c_]))], axis=1))
          for c_ in chains}
    qy = {c_: fdot(bf(jnp.where(incl, gm[c_][L:, :LANES], 0.0)),
                   jnp.concatenate([_stackmask(bf(zz[c_][:, :LANES])), _stackmask(bf(zz[c_][:, LANES:]))], axis=1))
          for c_ in chains}
    y0 = {c_: qy[c_][:, LANES:] + fdot(bf(jnp.where(incl, gm[c_][L:, LANES:], 0.0)), vsm[c_]) for c_ in chains}
    qa = {c_: bf(jnp.concatenate([rt[c_] + qy[c_][:, :LANES], zz[c_][:, :LANES]], axis=0)) for c_ in chains}
    bkg = {c_: bf(jnp.concatenate([bt[c_] * g_l[c_], kt[c_] * g_l[c_]], axis=0)) for c_ in chains}

    lane_s = lax.broadcasted_iota(I32, (hd, LANES), 1)
    sp = [s_ref[p] for p in range(n_pairs)]
    for ch in range(n_chunks):
        rs = rsl(ch)
        yw = [_nt(qa[ch, p], _stackmask(bf(sp[p]))) for p in range(n_pairs)]
        upd = [_tn(bf(jnp.concatenate([yw[p][L:] + zz[ch, p][:, LANES:], vh[ch, p]], axis=0)), bkg[ch, p])
               for p in range(n_pairs)]
        for p in range(n_pairs):
            cs = csl(p)
            sp[p] = sp[p] * g_l[ch, p] + jnp.where(lane_s < hd, upd[p][:hd], upd[p][hd:])
            y = yw[p][:L] + y0[ch, p]
            mean = _pair_sum(y, first) * (1.0 / hd)
            yc = y - mean
            var = _pair_sum(yc * yc, first) * (1.0 / hd)
            yn = yc * lax.rsqrt(var + GN_EPS) * lnw_ref[:, cs] + lnb_ref[:, cs]
            bonus = _pair_sum(brk[rs, cs], first) * vh[ch, p]
            o_ref[rs, cs] = ((yn + bonus) * g_ref[rs, cs]).astype(o_ref.dtype)
    for p in range(n_pairs):
        s_ref[p] = sp[p]


def _rwkv_scan(r, k, v, a, ld, g, k_k, k_a, r_k, ln_w, ln_b, L=RWKV_CHUNK, tm=512):
    B, S, W = r.shape
    seq = pl.BlockSpec((None, tm, W), lambda b, c: (b, c, 0))
    vec = pl.BlockSpec((1, W), lambda b, c: (0, 0))
    return pl.pallas_call(
        functools.partial(_rwkv_scan_kernel, L=L),
        grid=(B, S // tm),
        in_specs=[seq] * 6 + [vec] * 5,
        out_specs=seq,
        out_shape=jax.ShapeDtypeStruct((B, S, W), BF16),
        scratch_shapes=[pltpu.VMEM((W // LANES, RWKV_HEAD, LANES), F32)],
        compiler_params=_cp("parallel", "arbitrary"),
        name="rwkv_scan",
    )(r, k, v, a, ld, g, k_k.reshape(1, W), k_a.reshape(1, W), r_k.reshape(1, W),
      ln_w.reshape(1, W), ln_b.reshape(1, W))


def _post_kernel(attn_ref, rw_ref, gate_ref, x_ref, mod_ref, wa_ref, wb_ref, wo_ref, g2_ref, wrh_ref, wrm_ref,
                 sug_ref, sd_ref, base_ref, h2p_ref, lg_ref):
    D = x_ref.shape[1]
    ya = jnp.dot(attn_ref[...], wa_ref[...], preferred_element_type=F32)
    yb = jnp.dot(rw_ref[...], wb_ref[...], preferred_element_type=F32)
    m = gate_ref[:, 0:D] * ya + gate_ref[:, D:] * yb
    x1 = x_ref[...] + mod_ref[2:3, :] * jnp.dot(m.astype(BF16), wo_ref[...], preferred_element_type=F32)
    y = x1 * lax.rsqrt(jnp.mean(x1 * x1, axis=-1, keepdims=True) + NORM_EPS) * g2_ref[...]
    h2 = y * (1.0 + mod_ref[4:5, :]) + mod_ref[3:4, :]
    hb = h2.astype(BF16)
    hm = (h2 - hb.astype(F32)).astype(BF16)
    lg_ref[...] = _nt(wrh_ref[...], hb) + _nt(wrh_ref[...], hm) + _nt(wrm_ref[...], hb)
    packed = _pack_halves(hb.astype(F32))
    half = packed.shape[1] // 2
    h2p_ref[0] = packed[:, :half]
    h2p_ref[1] = packed[:, half:]
    F = sd_ref.shape[0]
    gu = jnp.dot(hb, sug_ref[...], preferred_element_type=F32)
    shared = jnp.dot((_silu(gu[:, :F]) * gu[:, F:]).astype(BF16), sd_ref[...], preferred_element_type=F32)
    base_ref[...] = x1 + mod_ref[5:6, :] * shared


def _post(attn, rw, gate, x2, mod, wa, wb, wo, norm2_g, w_router_t, sug, sd, S, tm=512):
    N, D = x2.shape
    E = w_router_t.shape[0]
    wr_hi = w_router_t.astype(BF16)
    wr_mid = (w_router_t - wr_hi.astype(F32)).astype(BF16)
    tpb = S // tm
    row = lambda w: pl.BlockSpec((tm, w), lambda i: (i, 0))
    full = lambda a: pl.BlockSpec(a.shape, lambda i: (0, 0))
    return pl.pallas_call(
        _post_kernel,
        grid=(N // tm,),
        in_specs=[row(attn.shape[1]), row(rw.shape[1]), row(gate.shape[1]), row(D),
                  pl.BlockSpec((None, 6, D), lambda i: (i // tpb, 0, 0)),
                  full(wa), full(wb), full(wo), pl.BlockSpec((1, D), lambda i: (0, 0)), full(wr_hi), full(wr_mid),
                  full(sug), full(sd)],
        out_specs=[row(D), pl.BlockSpec((2, tm, D // 4), lambda i: (0, i, 0)), pl.BlockSpec((E, tm), lambda i: (0, i))],
        out_shape=[jax.ShapeDtypeStruct((N, D), F32), jax.ShapeDtypeStruct((2, N, D // 4), U32),
                   jax.ShapeDtypeStruct((E, N), F32)],
        compiler_params=_cp("parallel"),
        name="post_mixer",
    )(attn, rw, gate, x2, mod, wa, wb, wo, norm2_g.reshape(1, D), wr_hi, wr_mid, sug, sd)


def _first_argmax(x, idx, sentinel):
    m = jnp.max(x, axis=0, keepdims=True)
    return m, jnp.min(jnp.where(x == m, idx, sentinel), axis=0, keepdims=True)


def _route_kernel(lg_ref, bias_ref, e_ref, w_ref):
    E, T = lg_ref.shape
    gsz = E // N_GROUPS
    scores = _sigmoid(lg_ref[...])
    biased = scores + bias_ref[...]
    ig = lax.broadcasted_iota(I32, (gsz, T), 0)
    gs = []
    for g in range(N_GROUPS):
        blk = biased[g * gsz:(g + 1) * gsz, :]
        m1, i1 = _first_argmax(blk, ig, gsz)
        m2 = jnp.max(jnp.where(ig == i1, -jnp.inf, blk), axis=0, keepdims=True)
        gs.append(m1 + m2)
    gsc = jnp.concatenate(gs, axis=0)
    i8 = lax.broadcasted_iota(I32, (N_GROUPS, T), 0)
    chosen = jnp.zeros((N_GROUPS, T), F32)
    for _ in range(TOPK_GROUPS):
        _, gi = _first_argmax(gsc, i8, N_GROUPS)
        hit = i8 == gi
        chosen = jnp.where(hit, 1.0, chosen)
        gsc = jnp.where(hit, -jnp.inf, gsc)
    masked = jnp.concatenate(
        [jnp.where(chosen[g:g + 1, :] > 0.0, biased[g * gsz:(g + 1) * gsz, :], -jnp.inf)
         for g in range(N_GROUPS)], axis=0)
    ie = lax.broadcasted_iota(I32, (E, T), 0)
    idxs, wts = [], []
    for _ in range(TOP_K):
        _, ei = _first_argmax(masked, ie, E)
        hit = ie == ei
        idxs.append(ei)
        wts.append(jnp.sum(jnp.where(hit, scores, 0.0), axis=0, keepdims=True))
        masked = jnp.where(hit, -jnp.inf, masked)
    wt = jnp.concatenate(wts, axis=0)
    e_ref[...] = jnp.concatenate(idxs, axis=0)
    w_ref[...] = wt / jnp.sum(wt, axis=0, keepdims=True) * ROUTED_SCALE


def _route(logits_t, router_bias, T=512):
    E, N = logits_t.shape
    blk = pl.BlockSpec((TOP_K, T), lambda i: (0, i))
    return pl.pallas_call(
        _route_kernel,
        grid=(N // T,),
        in_specs=[pl.BlockSpec((E, T), lambda i: (0, i)), pl.BlockSpec((E, 1), lambda i: (0, 0))],
        out_specs=[blk, blk],
        out_shape=[jax.ShapeDtypeStruct((TOP_K, N), I32), jax.ShapeDtypeStruct((TOP_K, N), F32)],
        compiler_params=_cp("parallel"),
        name="route",
    )(logits_t, router_bias.reshape(E, 1))


def _rank_kernel(e_ref, rank_ref, cnt_ref, carry_ref, *, n_experts):
    T = e_ref.shape[1]

    @pl.when(pl.program_id(0) == 0)
    def _():
        carry_ref[...] = jnp.zeros_like(carry_ref)

    ie = lax.broadcasted_iota(I32, (n_experts, T), 0)
    e = e_ref[...]
    hits = [ie == e[kk:kk + 1, :] for kk in range(TOP_K)]
    onehot = jnp.zeros((n_experts, T), F32)
    for hsel in hits:
        onehot = onehot + hsel.astype(F32)
    tr = lax.broadcasted_iota(I32, (T, T), 0)
    tc = lax.broadcasted_iota(I32, (T, T), 1)
    before = (tr < tc).astype(BF16)
    base = _bdot(onehot, before) + carry_ref[:, 0:1]
    rank_ref[...] = jnp.concatenate(
        [jnp.sum(jnp.where(hsel, base, 0.0), axis=0, keepdims=True) for hsel in hits], axis=0).astype(I32)
    carry_ref[...] = carry_ref[...] + jnp.sum(onehot, axis=1, keepdims=True)
    cnt_ref[...] = carry_ref[...]


def _ranks(eidx_t, n_experts, T=512):
    N = eidx_t.shape[1]
    blk = pl.BlockSpec((TOP_K, T), lambda i: (0, i))
    cnt = pl.BlockSpec((n_experts, LANES), lambda i: (0, 0))
    return pl.pallas_call(
        functools.partial(_rank_kernel, n_experts=n_experts),
        grid=(N // T,),
        in_specs=[blk],
        out_specs=[blk, cnt],
        out_shape=[jax.ShapeDtypeStruct((TOP_K, N), I32), jax.ShapeDtypeStruct((n_experts, LANES), F32)],
        scratch_shapes=[pltpu.VMEM((n_experts, LANES), F32)],
        compiler_params=_cp("arbitrary"),
        name="ranks",
    )(eidx_t)


def _dest_kernel(e_ref, rank_ref, start_ref, d_ref):
    E = start_ref.shape[0]
    T = e_ref.shape[1]
    ie = lax.broadcasted_iota(I32, (E, T), 0)
    e = e_ref[...]
    start = start_ref[:, 0:1]
    rows = [jnp.sum(jnp.where(ie == e[kk:kk + 1, :], start, 0.0), axis=0, keepdims=True) for kk in range(TOP_K)]
    d_ref[...] = jnp.concatenate(rows, axis=0).astype(I32) + rank_ref[...]


def _dests(eidx_t, rank_t, pstart, T=512):
    N = eidx_t.shape[1]
    E = pstart.shape[0]
    blk = pl.BlockSpec((TOP_K, T), lambda i: (0, i))
    return pl.pallas_call(
        _dest_kernel,
        grid=(N // T,),
        in_specs=[blk, blk, pl.BlockSpec((E, LANES), lambda i: (0, 0))],
        out_specs=blk,
        out_shape=jax.ShapeDtypeStruct((TOP_K, N), I32),
        compiler_params=_cp("parallel"),
        name="dests",
    )(eidx_t, rank_t, jnp.broadcast_to(pstart.astype(F32)[:, None], (E, LANES)))


def _expert_kernel(us_ref, ps_ref, x_hbm, wug_ref, wd_ref, y_hbm,
                   wug_bf, wd_bf, xbuf, ybuf, cnt_ref, xsem, ysem, *, n_rows):
    e = pl.program_id(0)
    _, _, R, Ch = xbuf.shape
    C = 2 * Ch
    F = wd_ref.shape[0]

    def x_copy(slot, half, row):
        return pltpu.make_async_copy(x_hbm.at[half, pl.ds(row, R)], xbuf.at[slot, half], xsem.at[slot])

    def y_copy(slot, half, row):
        return pltpu.make_async_copy(ybuf.at[slot, half], y_hbm.at[half, pl.ds(row, R)], ysem.at[slot])

    def start(copy, slot, row):
        for half in range(2):
            copy(slot, half, row).start(priority=1)

    def wait(copy, slot):
        for half in range(2):
            copy(slot, half, 0).wait()

    @pl.when(e == 0)
    def _():
        cnt_ref[0] = 0
        for t in range(X_AHEAD):
            start(x_copy, t, t * R)
        ybuf[...] = jnp.zeros_like(ybuf)
        for slot in range(2):
            start(y_copy, slot, n_rows + slot * R)

    wug_bf[...] = wug_ref[...].astype(BF16)
    wd_bf[...] = wd_ref[...].astype(BF16)
    n_valid = us_ref[e + 1] - us_ref[e]
    pbase = ps_ref[e]
    n_tiles = lax.shift_right_logical(n_valid + (R - 1), int(math.log2(R)))

    def make_tile(nb):
        rowid = lax.broadcasted_iota(I32, (nb * R, C), 0)

        def tile(blk0):
            n_done = cnt_ref[0]
            xslots = [lax.rem(n_done + b, X_SLOTS) for b in range(nb)]
            for b in range(nb):
                ahead = n_done + X_AHEAD + b
                start(x_copy, lax.rem(ahead, X_SLOTS), pl.multiple_of(ahead * R, R))
            for b in range(nb):
                wait(x_copy, xslots[b])
            x = jnp.concatenate([jnp.concatenate([xbuf[xs, 0], xbuf[xs, 1]], axis=1) for xs in xslots], axis=0)
            x = jnp.where(rowid < n_valid - blk0 * R, x, jnp.uint32(0))
            lo, hi = _unpack_halves(x)
            gu = (jnp.dot(lo.astype(BF16), wug_bf[0:C, :], preferred_element_type=F32)
                  + jnp.dot(hi.astype(BF16), wug_bf[C:, :], preferred_element_type=F32))
            hid = (_silu(gu[:, :F]) * gu[:, F:]).astype(BF16)
            y = jnp.dot(hid, wd_bf[...], preferred_element_type=F32)
            packed = _pack_halves(y.astype(BF16).astype(F32))
            for b in range(nb):
                slot = (n_done + b) & 1
                wait(y_copy, slot)
                ybuf[slot, 0] = packed[b * R:(b + 1) * R, :Ch]
                ybuf[slot, 1] = packed[b * R:(b + 1) * R, Ch:]
                start(y_copy, slot, pl.multiple_of(pbase + (blk0 + b) * R, R))
            cnt_ref[0] = n_done + nb

        return tile

    pair_tile, single_tile = make_tile(2), make_tile(1)

    def pairs(j, carry):
        pair_tile(2 * j)
        return carry

    lax.fori_loop(0, lax.shift_right_logical(n_tiles, 1), pairs, 0)

    @pl.when((n_tiles & 1) == 1)
    def _():
        single_tile(n_tiles - 1)

    @pl.when(e == pl.num_programs(0) - 1)
    def _():
        for t in range(X_AHEAD):
            wait(x_copy, lax.rem(cnt_ref[0] + t, X_SLOTS))
        for slot in range(2):
            wait(y_copy, slot)
        ybuf[0] = jnp.zeros((2, R, Ch), U32)
        first = lax.shift_right_logical(pbase + n_tiles * R, int(math.log2(R)))
        n_left = n_rows // R - first

        def fill(t, carry):
            start(y_copy, 0, pl.multiple_of((first + t) * R, R))
            return carry

        def drain(t, carry):
            wait(y_copy, 0)
            return carry

        lax.fori_loop(0, n_left, fill, 0)
        lax.fori_loop(0, n_left, drain, 0)


def _experts(ustart, pstart, xg, w_ug, w_d, n_rows, R):
    _, _, Ch = xg.shape
    E, D, F2 = w_ug.shape
    F = w_d.shape[1]
    grid_spec = pltpu.PrefetchScalarGridSpec(
        num_scalar_prefetch=2,
        grid=(E,),
        in_specs=[pl.BlockSpec(memory_space=pl.ANY),
                  pl.BlockSpec((None, D, F2), lambda e, us, ps: (e, 0, 0)),
                  pl.BlockSpec((None, F, D), lambda e, us, ps: (e, 0, 0))],
        out_specs=pl.BlockSpec(memory_space=pl.ANY),
        scratch_shapes=[pltpu.VMEM((D, F2), BF16), pltpu.VMEM((F, D), BF16),
                        pltpu.VMEM((X_SLOTS, 2, R, Ch), U32), pltpu.VMEM((2, 2, R, Ch), U32), pltpu.SMEM((1,), I32),
                        pltpu.SemaphoreType.DMA((X_SLOTS,)), pltpu.SemaphoreType.DMA((2,))],
    )
    return pl.pallas_call(
        functools.partial(_expert_kernel, n_rows=n_rows),
        grid_spec=grid_spec,
        out_shape=jax.ShapeDtypeStruct((2, n_rows + 2 * R, Ch), U32),
        compiler_params=_cp("arbitrary"),
        name="experts",
    )(ustart, pstart, xg, w_ug, w_d)


def _sc_scatter_rows(src, idx, n_rows):
    H, N, C = src.shape
    K = idx.shape[0]
    per_row = N // SC_WINDOW
    mesh = plsc.VectorSubcoreMesh(core_axis_name="c", subcore_axis_name="s")

    @functools.partial(pl.kernel, out_type=jax.ShapeDtypeStruct((H, n_rows, C), src.dtype), mesh=mesh,
                       scratch_types=[])
    def scatter_kernel(x_hbm, i_hbm, o_hbm):
        for h in range(H):
            def body(x_vmem, i_vmem):
                pltpu.sync_copy(x_vmem, o_hbm.at[h].at[i_vmem.at[0]])

            pltpu.emit_pipeline(
                body,
                grid=(K * per_row,),
                in_specs=[pl.BlockSpec((SC_WINDOW, C), lambda i: (i % per_row, 0)),
                          pl.BlockSpec((1, SC_WINDOW), lambda i: (i // per_row, i % per_row))],
                out_specs=[],
                core_axis_name=("c", "s"),
                dimension_semantics=(pltpu.PARALLEL,),
            )(x_hbm.at[h], i_hbm)

    return scatter_kernel(src, idx)


def _sc_gather_rows(src, idx):
    H, _, C = src.shape
    K, N = idx.shape
    per_row = N // SC_WINDOW
    mesh = plsc.VectorSubcoreMesh(core_axis_name="c", subcore_axis_name="s")

    @functools.partial(pl.kernel, out_type=jax.ShapeDtypeStruct((H, K * N, C), src.dtype), mesh=mesh,
                       scratch_types=[])
    def gather_kernel(x_hbm, i_hbm, o_hbm):
        for h in range(H):
            def body(i_vmem, o_vmem):
                pltpu.sync_copy(x_hbm.at[h].at[i_vmem.at[0]], o_vmem)

            pltpu.emit_pipeline(
                body,
                grid=(K * per_row,),
                in_specs=[pl.BlockSpec((1, SC_WINDOW), lambda i: (i // per_row, i % per_row))],
                out_specs=[pl.BlockSpec((SC_WINDOW, C), lambda i: (i, 0))],
                core_axis_name=("c", "s"),
                dimension_semantics=(pltpu.PARALLEL,),
            )(i_hbm, o_hbm.at[h])

    return gather_kernel(src, idx)


def _combine_kernel(*refs):
    y_refs = refs[:2 * TOP_K]
    w_ref, base_ref, mod_ref, o_ref = refs[2 * TOP_K:]
    T = base_ref.shape[0]
    tr = lax.broadcasted_iota(I32, (T, T), 0)
    tc = lax.broadcasted_iota(I32, (T, T), 1)
    wcol = _nt((tr == tc).astype(F32), w_ref[...], precision=HI)
    acc = [None] * 4
    for kk in range(TOP_K):
        wk = wcol[:, kk:kk + 1]
        for half in range(2):
            lo, hi = _unpack_halves(y_refs[2 * kk + half][...])
            for q, val in ((half, lo), (2 + half, hi)):
                acc[q] = val * wk if acc[q] is None else acc[q] + val * wk
    o_ref[...] = base_ref[...] + mod_ref[5:6, :] * jnp.concatenate(acc, axis=1)


def _combine(yg, w_t, base, mod, S, T=512):
    N, D = base.shape
    C = yg.shape[2]
    tpb = S // T
    n_tiles = N // T
    row = pl.BlockSpec((T, D), lambda i: (i, 0))
    piece = lambda kk, half: pl.BlockSpec((None, T, C), lambda i: (half, kk * n_tiles + i, 0))
    return pl.pallas_call(
        _combine_kernel,
        grid=(n_tiles,),
        in_specs=[piece(kk, half) for kk in range(TOP_K) for half in range(2)] + [
            pl.BlockSpec((TOP_K, T), lambda i: (0, i)),
            row,
            pl.BlockSpec((None, 6, D), lambda i: (i // tpb, 0, 0))],
        out_specs=row,
        out_shape=jax.ShapeDtypeStruct((N, D), F32),
        compiler_params=_cp("parallel"),
        name="combine",
    )(*([yg] * (2 * TOP_K)), w_t, base, mod)


def _layer(x, c, positions, layer_idx, w_ada, b_ada, norm1_g, w_in, w_gate, b_gate,
           q_norm_g, k_norm_g, lambda_q1, lambda_k1, lambda_q2, lambda_k2, subln_g,
           rwkv_mu, w_decay0, w_decay2, a0, a2, g2, k_k, k_a, r_k, ln_x_w, ln_x_b,
           w_branch_a, w_branch_b, w_out, norm2_g, w_router, router_bias,
           w_expert_up_gate, w_expert_down, w_shared_up_gate, w_shared_down):
    B, S, D = x.shape
    N = B * S
    E = w_router.shape[1]
    da_width = w_branch_a.shape[0]
    rw_width = w_branch_b.shape[0]
    lambda_init = 0.8 - 0.6 * math.exp(-0.3 * layer_idx)

    mod = _adaln(c, w_ada, b_ada)
    x2 = x.reshape(N, D)
    w_cat = jnp.concatenate([w_in, w_gate], axis=1).astype(BF16)
    qn, kn, v, r_, k_, v_, a_, ld_, g_, gate = _mixer_in(
        x2, positions.reshape(N, 1), mod, norm1_g, w_cat, b_gate, q_norm_g, k_norm_g,
        rwkv_mu, w_decay0, w_decay2, a0, a2, g2, S, da_width, rw_width)

    lam_vecs = jnp.stack([lambda_q1, lambda_k1, lambda_q2, lambda_k2])
    score_bound = 1.01 * DA_HEAD_DIM ** 0.5 * jnp.max(jnp.abs(q_norm_g)) * jnp.max(jnp.abs(k_norm_g))
    attn = _diff_attention(qn, kn, v, score_bound, lam_vecs, subln_g, B, S, lambda_init)

    seq = lambda t: t.reshape(B, S, rw_width)
    rw = _rwkv_scan(seq(r_), seq(k_), seq(v_), seq(a_), seq(ld_), seq(g_), k_k, k_a, r_k.reshape(-1),
                    ln_x_w, ln_x_b).reshape(N, rw_width)

    base, h2p, logits_t = _post(attn, rw, gate, x2, mod, w_branch_a.astype(BF16), w_branch_b.astype(BF16),
                                w_out.astype(BF16), norm2_g, w_router.T,
                                w_shared_up_gate.astype(BF16), w_shared_down.astype(BF16), S)

    eidx_t, w_t = _route(logits_t, router_bias)
    rank_t, counts = _ranks(eidx_t, E)
    R = EXPERT_TILE
    cnt = counts[:, 0].astype(I32)
    ustart = jnp.concatenate([jnp.zeros((1,), I32), jnp.cumsum(cnt)])
    pcnt = (cnt + R - 1) // R * R
    pstart = jnp.cumsum(pcnt) - pcnt
    dest_p = _dests(eidx_t, rank_t, pstart)
    n_rows = (N * TOP_K + E * (R - 1) + R - 1) // R * R
    xg = _sc_scatter_rows(h2p, dest_p, n_rows + X_AHEAD * R)
    y = _experts(ustart, pstart, xg, w_expert_up_gate, w_expert_down, n_rows, R)
    yg = _sc_gather_rows(y, dest_p)
    out = _combine(yg, w_t, base, mod, S)
    return out.reshape(B, S, D)


def kernel(x, c, positions, w_ada, b_ada, norm1_g, w_in, w_gate, b_gate, q_norm_g, k_norm_g, lambda_q1, lambda_k1, lambda_q2, lambda_k2, subln_g, rwkv_mu, w_decay0, w_decay2, a0, a2, g2, k_k, k_a, r_k, ln_x_w, ln_x_b, w_branch_a, w_branch_b, w_out, norm2_g, w_router, router_bias, w_expert_up_gate, w_expert_down, w_shared_up_gate, w_shared_down):
    for l in range(w_ada.shape[0]):
        x = _layer(x, c, positions, l, w_ada[l], b_ada[l], norm1_g[l], w_in[l], w_gate[l], b_gate[l],
                   q_norm_g[l], k_norm_g[l], lambda_q1[l], lambda_k1[l], lambda_q2[l], lambda_k2[l],
                   subln_g[l], rwkv_mu[l], w_decay0[l], w_decay2[l], a0[l], a2[l], g2[l], k_k[l],
                   k_a[l], r_k[l], ln_x_w[l], ln_x_b[l], w_branch_a[l], w_branch_b[l], w_out[l],
                   norm2_g[l], w_router[l], router_bias[l], w_expert_up_gate[l], w_expert_down[l],
                   w_shared_up_gate[l], w_shared_down[l])
    return x
```

```python
import functools
import math

import jax
import jax.numpy as jnp
from jax import lax
from jax.experimental import pallas as pl
from jax.experimental.pallas import tpu as pltpu
from jax.experimental.pallas import tpu_sc as plsc

F32 = jnp.float32
BF16 = jnp.bfloat16
I32 = jnp.int32
U32 = jnp.uint32
HI = lax.Precision.HIGHEST

CHUNK = 64
ROPE_THETA = 10000.0
NORM_EPS = 1e-6
SUBLN_EPS = 1e-5
DA_HEAD_DIM = 64
RWKV_HEAD = 64
GN_EPS = 64e-5
TOP_K = 8
N_GROUPS = 8
TOPK_GROUPS = 4
ROUTED_SCALE = 2.5
EXPERT_TILE = 256
X_SLOTS = 6
X_AHEAD = X_SLOTS - 2
RWKV_CHUNK = 64
LANES = 128
SC_WINDOW = 128
NEG = -1e30
MAX_PLAIN_SCORE = 40.0
VMEM_LIMIT = 56 * 1024 * 1024


def _cp(*sem):
    return pltpu.CompilerParams(dimension_semantics=sem, vmem_limit_bytes=VMEM_LIMIT)


def _bdot(a, b):
    return jnp.dot(a.astype(BF16), b.astype(BF16), preferred_element_type=F32)


def _fdot(a, b):
    return jnp.dot(a, b, precision=HI, preferred_element_type=F32)


def _nt(a, b, precision=None):
    return lax.dot_general(a, b, (((1,), (1,)), ((), ())), precision=precision,
                           preferred_element_type=F32)


def _tn(a, b, precision=None):
    return lax.dot_general(a, b, (((0,), (0,)), ((), ())), precision=precision,
                           preferred_element_type=F32)


def _pack_halves(x):
    c = x.shape[1] // 2
    lo = lax.bitcast_convert_type(x[:, :c], U32)
    hi = lax.bitcast_convert_type(x[:, c:], U32)
    return (hi & jnp.uint32(0xFFFF0000)) | (lo >> 16)


def _unpack_halves(w):
    lo = lax.bitcast_convert_type(w << 16, F32)
    hi = lax.bitcast_convert_type(w & jnp.uint32(0xFFFF0000), F32)
    return lo, hi


def _sigmoid(x):
    return 1.0 / (1.0 + jnp.exp(-x))


def _silu(x):
    return x * _sigmoid(x)


def _ada_kernel(c_ref, w_ref, b_ref, o_ref):
    o_ref[...] = _fdot(_silu(c_ref[...]), w_ref[...]) + b_ref[...]


def _adaln(c, w_ada, b_ada):
    B, D = c.shape
    rows = -(-B // 8) * 8
    cpad = jnp.zeros((rows, D), F32).at[:B].set(c)
    n_out = w_ada.shape[1]
    out = pl.pallas_call(
        _ada_kernel,
        grid=(n_out // D,),
        in_specs=[pl.BlockSpec((rows, D), lambda j: (0, 0)),
                  pl.BlockSpec((D, D), lambda j: (0, j)),
                  pl.BlockSpec((1, D), lambda j: (0, j))],
        out_specs=pl.BlockSpec((rows, D), lambda j: (0, j)),
        out_shape=jax.ShapeDtypeStruct((rows, n_out), F32),
        compiler_params=_cp("arbitrary"),
        name="adaln",
    )(cpad, w_ada, b_ada.reshape(1, n_out))
    return out[:B].reshape(B, n_out // D, D)


def _mixer_in_kernel(x_ref, xprev_ref, mod_ref, g_ref, w_ref, bg_ref, pos_ref, invf_ref, qg_ref, kg_ref,
                     mu_ref, w0_ref, w2_ref, a0_ref, a2_ref, g2_ref,
                     qn_ref, kn_ref, v_ref, r_ref, k_ref, vr_ref, a_ref, ld_ref, gr_ref, gate_ref,
                     *, da_width, rw_cols, rw_width, q_scale, tiles_per_seq):
    tm = x_ref.shape[0]

    def modulated(x):
        y = x * lax.rsqrt(jnp.mean(x * x, axis=-1, keepdims=True) + NORM_EPS) * g_ref[...]
        return (y * (1.0 + mod_ref[1:2, :]) + mod_ref[0:1, :]).astype(BF16)

    def proj(hb, c0, width, step=512):
        parts = [jnp.dot(hb, w_ref[:, c0 + o:c0 + min(o + step, width)], preferred_element_type=F32)
                 for o in range(0, width, step)]
        return parts[0] if len(parts) == 1 else jnp.concatenate(parts, axis=1)

    h = modulated(x_ref[...])

    lane = lax.broadcasted_iota(I32, (tm, LANES), 1)
    first = lane < DA_HEAD_DIM
    lo_half = (lane & (DA_HEAD_DIM - 1)) < DA_HEAD_DIM // 2
    ang = pos_ref[...].astype(F32) * invf_ref[...]
    cos = jnp.cos(ang)
    sin = jnp.sin(ang)
    sin = jnp.where(lo_half, -sin, sin)
    for c0, dst, gn_ref, mult in ((0, qn_ref, qg_ref, q_scale), (da_width, kn_ref, kg_ref, 1.0)):
        raw = proj(h, c0, da_width)
        for blk in range(da_width // LANES):
            x = raw[:, blk * LANES:(blk + 1) * LANES]
            xx = x * x
            s_first = jnp.sum(jnp.where(first, xx, 0.0), axis=-1, keepdims=True)
            s_second = jnp.sum(jnp.where(first, 0.0, xx), axis=-1, keepdims=True)
            ms = jnp.where(first, s_first, s_second) * (1.0 / DA_HEAD_DIM)
            xn = x * lax.rsqrt(ms + NORM_EPS) * gn_ref[...]
            rot = jnp.where(lo_half, pltpu.roll(xn, LANES - DA_HEAD_DIM // 2, axis=1),
                            pltpu.roll(xn, DA_HEAD_DIM // 2, axis=1))
            dst[:, blk * LANES:(blk + 1) * LANES] = ((xn * cos + rot * sin) * mult).astype(dst.dtype)
    v_ref[...] = proj(h, 2 * da_width, da_width).astype(v_ref.dtype)

    c_rw = 3 * da_width
    p = proj(h, c_rw, rw_cols)
    p_before = proj(modulated(xprev_ref[...]), c_rw, rw_cols)
    seq_start = (pl.program_id(0) % tiles_per_seq) == 0
    last_prev = jnp.where(seq_start, 0.0, p_before[7:8, :])
    rowi = lax.broadcasted_iota(I32, p.shape, 0)
    prev = jnp.where(rowi == 0, last_prev, pltpu.roll(p, 1, axis=0))
    xs = p + (prev - p) * mu_ref[...]
    width = rw_width
    r_ref[...] = xs[:, 0:width]
    k_ref[...] = xs[:, width:2 * width]
    vr_ref[...] = xs[:, 2 * width:3 * width]
    xwa = xs[:, 3 * width:3 * width + LANES]
    xg = xs[:, 3 * width + LANES:]
    z = w0_ref[...] + _bdot(jnp.tanh(xwa), w2_ref[...])
    w = -(jnp.maximum(-z, 0.0) + jnp.log(1.0 + jnp.exp(-jnp.abs(z)))) - 0.5
    ld_ref[...] = -jnp.exp(w)
    a_ref[...] = _sigmoid(a0_ref[...] + _bdot(xwa, a2_ref[...]))
    gr_ref[...] = _bdot(_sigmoid(xg), g2_ref[...])

    gate_ref[...] = _sigmoid(proj(h, c_rw + rw_cols, gate_ref.shape[1]) + bg_ref[...]).astype(gate_ref.dtype)


def _mixer_in(x2, pos2, mod, norm1_g, w_cat, b_gate, q_norm_g, k_norm_g, mu, w_decay0, w_decay2, a0, a2, g2,
              S, da_width, rw_width, tm=512):
    N, D = x2.shape
    n_gate = b_gate.shape[0]
    rw_cols = mu.shape[0]
    tpb = S // tm
    d = DA_HEAD_DIM
    inv_freq = 1.0 / (ROPE_THETA ** (jnp.arange(0, d, 2, dtype=F32) / d))
    invf = jnp.tile(inv_freq, LANES // (d // 2)).reshape(1, LANES)
    dl, al = w_decay2.shape[0], a2.shape[0]
    assert dl + al == LANES and g2.shape[0] == LANES
    w2p = jnp.zeros((LANES, rw_width), F32).at[:dl].set(w_decay2)
    a2p = jnp.zeros((LANES, rw_width), F32).at[dl:].set(a2)
    kern = functools.partial(_mixer_in_kernel, da_width=da_width, rw_cols=rw_cols, rw_width=rw_width,
                             q_scale=d ** -0.5 * math.log2(math.e),
                             tiles_per_seq=tpb)
    row = lambda w: pl.BlockSpec((tm, w), lambda i: (i, 0))
    vec = lambda n: pl.BlockSpec((1, n), lambda i: (0, 0))
    mat = pl.BlockSpec((LANES, rw_width), lambda i: (0, 0))
    f32 = lambda w: jax.ShapeDtypeStruct((N, w), F32)
    bf16 = lambda w: jax.ShapeDtypeStruct((N, w), BF16)
    return pl.pallas_call(
        kern,
        grid=(N // tm,),
        in_specs=[row(D),
                  pl.BlockSpec((8, D), lambda i: (jnp.maximum(i * (tm // 8) - 1, 0), 0)),
                  pl.BlockSpec((None, 6, D), lambda i: (i // tpb, 0, 0)),
                  vec(D),
                  pl.BlockSpec(w_cat.shape, lambda i: (0, 0)),
                  vec(n_gate),
                  pl.BlockSpec((tm, 1), lambda i: (i, 0)),
                  vec(LANES), vec(LANES), vec(LANES),
                  vec(rw_cols), vec(rw_width), mat, vec(rw_width), mat, mat],
        out_specs=[row(da_width)] * 3 + [row(rw_width)] * 6 + [row(n_gate)],
        out_shape=[bf16(da_width)] * 3 + [f32(rw_width)] * 6 + [bf16(n_gate)],
        compiler_params=_cp("parallel"),
        name="mixer_in",
    )(x2, x2, mod, norm1_g.reshape(1, D), w_cat, b_gate.reshape(1, n_gate), pos2, invf,
      jnp.tile(q_norm_g, 2).reshape(1, LANES), jnp.tile(k_norm_g, 2).reshape(1, LANES),
      mu.reshape(1, rw_cols), w_decay0.reshape(1, rw_width), w2p, a0.reshape(1, rw_width), a2p, g2)


def _attn_kernel(flag_ref, q_ref, k_ref, v_ref, vt_ref, lam_ref, sg_ref, sgc_ref, o_ref,
                 qz_ref, m_ref, l_ref, acc_ref, lpt_ref, acct_ref, *, tq, lambda_init):
    i = pl.program_id(2)
    lane = lax.broadcasted_iota(I32, (tq, LANES), 1)
    q = q_ref[...]
    zero = jnp.zeros_like(q)
    qz_ref[0:tq, :] = jnp.where(lane < DA_HEAD_DIM, q, zero)
    qz_ref[tq:, :] = jnp.where(lane >= DA_HEAD_DIM, q, zero)
    bounded = flag_ref[0] == 1
    lv = lam_ref[...]
    lam = (jnp.exp(jnp.sum(lv[0:1] * lv[1:2], keepdims=True))
           - jnp.exp(jnp.sum(lv[2:3] * lv[3:4], keepdims=True)) + lambda_init)

    def run(step):
        def body(j, carry):
            step(j, False)
            return carry
        lax.fori_loop(0, i, body, 0)
        step(i, True)

    def plain_step(j, masked):
        off = pl.multiple_of(j * tq, tq)
        st = _nt(k_ref[pl.ds(off, tq), :], qz_ref[...])
        if masked:
            row = lax.broadcasted_iota(I32, st.shape, 0)
            col = lax.broadcasted_iota(I32, st.shape, 1)
            st = jnp.where((row // CHUNK) <= ((col & (tq - 1)) // CHUNK), st, NEG)
        pt = jnp.exp2(st)
        part = pt[0:8, :]
        for g in range(1, tq // 8):
            part = part + pt[8 * g:8 * g + 8, :]
        lpt_ref[...] += part
        acct_ref[...] += jnp.dot(vt_ref[:, pl.ds(off, tq)], pt.astype(BF16), preferred_element_type=F32)

    @pl.when(bounded)
    def _():
        lpt_ref[...] = jnp.zeros_like(lpt_ref)
        acct_ref[...] = jnp.zeros_like(acct_ref)

        def two_steps(p, carry):
            plain_step(2 * p, False)
            plain_step(2 * p + 1, False)
            return carry

        lax.fori_loop(0, lax.shift_right_logical(i, 1), two_steps, 0)

        @pl.when((i & 1) == 1)
        def _():
            plain_step(i - 1, False)
            plain_step(i, True)

        @pl.when((i & 1) == 0)
        def _():
            plain_step(i, True)

        lsum = jnp.sum(lpt_ref[...], axis=0, keepdims=True)
        ot = acct_ref[:, 0:tq] / lsum[:, 0:tq] - lam * (acct_ref[:, tq:] / lsum[:, tq:])
        ot = ot * lax.rsqrt(jnp.mean(ot * ot, axis=0, keepdims=True) + SUBLN_EPS) * sgc_ref[...]
        o_ref[...] = (ot * (1.0 - lambda_init)).T.astype(o_ref.dtype)

    def online_step(j, masked):
        off = pl.multiple_of(j * tq, tq)
        s = _nt(qz_ref[...], k_ref[pl.ds(off, tq), :])
        if masked:
            row = lax.broadcasted_iota(I32, s.shape, 0)
            col = lax.broadcasted_iota(I32, s.shape, 1)
            s = jnp.where((col // CHUNK) <= ((row & (tq - 1)) // CHUNK), s, NEG)
        m_old = m_ref[...]
        m_new = jnp.maximum(m_old, jnp.max(s, axis=-1, keepdims=True))
        alpha = jnp.exp2(m_old - m_new)
        pr = jnp.exp2(s - m_new)
        l_ref[...] = alpha * l_ref[...] + jnp.sum(pr, axis=-1, keepdims=True)
        acc_ref[...] = alpha * acc_ref[...] + jnp.dot(pr.astype(BF16), v_ref[pl.ds(off, tq), :],
                                                      preferred_element_type=F32)
        m_ref[...] = m_new

    @pl.when(jnp.logical_not(bounded))
    def _():
        m_ref[...] = jnp.full_like(m_ref, NEG)
        l_ref[...] = jnp.zeros_like(l_ref)
        acc_ref[...] = jnp.zeros_like(acc_ref)
        run(online_step)
        o = acc_ref[0:tq, :] / l_ref[0:tq, :] - lam * (acc_ref[tq:, :] / l_ref[tq:, :])
        o = o * lax.rsqrt(jnp.mean(o * o, axis=-1, keepdims=True) + SUBLN_EPS) * sg_ref[...]
        o_ref[...] = (o * (1.0 - lambda_init)).astype(o_ref.dtype)


def _diff_attention(qn, kn, v, score_bound, lam_vecs, subln_g, B, S, lambda_init, tq=512):
    W = qn.shape[1]
    H = W // LANES
    q3 = qn.reshape(B, S, W)
    k3 = kn.reshape(B, S, W)
    v3 = v.reshape(B, S, W)
    vt3 = v3.transpose(0, 2, 1)
    flag = (score_bound <= MAX_PLAIN_SCORE).astype(I32).reshape(1)
    qblk = pl.BlockSpec((None, tq, LANES), lambda b, h, i, f: (b, i, h))
    kvblk = pl.BlockSpec((None, S, LANES), lambda b, h, i, f: (b, 0, h))
    grid_spec = pltpu.PrefetchScalarGridSpec(
        num_scalar_prefetch=1,
        grid=(B, H, S // tq),
        in_specs=[qblk, kvblk, kvblk,
                  pl.BlockSpec((None, LANES, S), lambda b, h, i, f: (b, h, 0)),
                  pl.BlockSpec((4, DA_HEAD_DIM), lambda b, h, i, f: (0, 0)),
                  pl.BlockSpec((1, LANES), lambda b, h, i, f: (0, 0)),
                  pl.BlockSpec((LANES, 1), lambda b, h, i, f: (0, 0))],
        out_specs=qblk,
        scratch_shapes=[pltpu.VMEM((2 * tq, LANES), BF16),
                        pltpu.VMEM((2 * tq, 1), F32),
                        pltpu.VMEM((2 * tq, 1), F32),
                        pltpu.VMEM((2 * tq, LANES), F32),
                        pltpu.VMEM((8, 2 * tq), F32),
                        pltpu.VMEM((LANES, 2 * tq), F32)],
    )
    out = pl.pallas_call(
        functools.partial(_attn_kernel, tq=tq, lambda_init=lambda_init),
        grid_spec=grid_spec,
        out_shape=jax.ShapeDtypeStruct((B, S, W), BF16),
        compiler_params=_cp("parallel", "parallel", "arbitrary"),
        name="diff_attn",
    )(flag, q3, k3, v3, vt3, lam_vecs, subln_g.reshape(1, LANES), subln_g.reshape(LANES, 1))
    return out.reshape(B * S, W)


def _stackmask(m):
    lane = lax.broadcasted_iota(I32, m.shape, 1)
    z = jnp.zeros_like(m)
    return jnp.concatenate([jnp.where(lane < RWKV_HEAD, m, z), jnp.where(lane >= RWKV_HEAD, m, z)], axis=0)


def _pair_sum(x, first):
    s1 = jnp.sum(jnp.where(first, x, 0.0), axis=-1, keepdims=True)
    s2 = jnp.sum(jnp.where(first, 0.0, x), axis=-1, keepdims=True)
    return jnp.where(first, s1, s2)


def _rwkv_scan_kernel(r_ref, k_ref, v_ref, a_ref, ld_ref, g_ref, kk_ref, ka_ref, rk_ref, lnw_ref, lnb_ref,
                      o_ref, s_ref, *, L):
    tm, W = r_ref.shape
    n_chunks = tm // L
    n_pairs = W // LANES
    hd = RWKV_HEAD
    bf = lambda t: t.astype(BF16)

    @pl.when(pl.program_id(1) == 0)
    def _():
        s_ref[...] = jnp.zeros_like(s_ref)

    row = lax.broadcasted_iota(I32, (tm, tm), 0)
    col = lax.broadcasted_iota(I32, (tm, tm), 1)
    tri = jnp.where(jnp.logical_and(col <= row, (col // L) == (row // L)), 1.0, 0.0).astype(BF16)
    ld = ld_ref[...]
    ld_hi = bf(ld)
    rem = ld - ld_hi.astype(F32)
    ld_mid = bf(rem)
    ld_lo = bf(rem - ld_mid.astype(F32))
    c = (jnp.dot(tri, ld_hi, preferred_element_type=F32) + jnp.dot(tri, ld_mid, preferred_element_type=F32)
         + jnp.dot(tri, ld_lo, preferred_element_type=F32))
    ec = jnp.exp(c)
    eci = jnp.exp(-c)
    ecm = jnp.exp(c - ld)
    r = r_ref[...]
    k = k_ref[...]
    v = v_ref[...]
    a = a_ref[...]
    kkr = k * kk_ref[...]
    kmod = k * (1.0 + (a - 1.0) * ka_ref[...])
    brk = r * kmod * rk_ref[...]

    lane = lax.broadcasted_iota(I32, (L, LANES), 1)
    rowl = lax.broadcasted_iota(I32, (L, LANES), 0)
    first = lane < hd
    lane_h = lane & (hd - 1)
    strict = lane_h < rowl
    incl = lane_h <= rowl
    eye = jnp.where(lane_h == rowl, 1.0, 0.0)

    chains = [(ch, p) for ch in range(n_chunks) for p in range(n_pairs)]
    rsl = lambda ch: slice(ch * L, (ch + 1) * L)
    csl = lambda p: slice(p * LANES, (p + 1) * LANES)
    fdot = lambda x, y: jnp.dot(x, y, preferred_element_type=F32)
    at, bt, kt, rt, vh, g_l = {}, {}, {}, {}, {}, {}
    for c_ in chains:
        ch, p = c_
        rs, cs = rsl(ch), csl(p)
        kkh = kkr[rs, cs]
        kkh = kkh / jnp.maximum(jnp.sqrt(_pair_sum(kkh * kkh, first)), 1e-12)
        vh[c_] = v[rs, cs]
        g_l[c_] = ec[ch * L + L - 1:ch * L + L, cs]
        at[c_] = -kkh * ecm[rs, cs]
        bt[c_] = kkh * a[rs, cs] * eci[rs, cs]
        kt[c_] = kmod[rs, cs] * eci[rs, cs]
        rt[c_] = r[rs, cs] * ec[rs, cs]
    gm = {c_: _nt(bf(jnp.concatenate([at[c_], rt[c_]], axis=0)),
                  jnp.concatenate([_stackmask(bf(bt[c_])), _stackmask(bf(kt[c_]))], axis=0)) for c_ in chains}
    a_ab = {c_: jnp.where(strict, gm[c_][:L, :LANES], 0.0) for c_ in chains}
    vsm = {c_: _stackmask(bf(vh[c_])) for c_ in chains}
    cmat = {c_: fdot(bf(jnp.where(strict, gm[c_][:L, LANES:], 0.0)), vsm[c_]) for c_ in chains}
    t_inv = {c_: eye + a_ab[c_] for c_ in chains}
    pw = {c_: bf(a_ab[c_]) for c_ in chains}
    for _ in range(int(math.log2(L)) - 1):
        pw = {c_: bf(fdot(pw[c_], _stackmask(pw[c_]))) for c_ in chains}
        t_inv = {c_: t_inv[c_] + fdot(pw[c_], _stackmask(bf(t_inv[c_]))) for c_ in chains}
    zz = {c_: fdot(bf(t_inv[c_]), jnp.concatenate([_stackmask(bf(at[c_])), _stackmask(bf(cmat[c_]))], axis=1))
          for c_ in chains}
    qy = {c_: fdot(bf(jnp.where(incl, gm[c_][L:, :LANES], 0.0)),
                   jnp.concatenate([_stackmask(bf(zz[c_][:, :LANES])), _stackmask(bf(zz[c_][:, LANES:]))], axis=1))
          for c_ in chains}
    y0 = {c_: qy[c_][:, LANES:] + fdot(bf(jnp.where(incl, gm[c_][L:, LANES:], 0.0)), vsm[c_]) for c_ in chains}
    qa = {c_: bf(jnp.concatenate([rt[c_] + qy[c_][:, :LANES], zz[c_][:, :LANES]], axis=0)) for c_ in chains}
    bkg = {c_: bf(jnp.concatenate([bt[c_] * g_l[c_], kt[c_] * g_l[c_]], axis=0)) for c_ in chains}

    lane_s = lax.broadcasted_iota(I32, (hd, LANES), 1)
    sp = [s_ref[p] for p in range(n_pairs)]
    for ch in range(n_chunks):
        rs = rsl(ch)
        yw = [_nt(qa[ch, p], _stackmask(bf(sp[p]))) for p in range(n_pairs)]
        upd = [_tn(bf(jnp.concatenate([yw[p][L:] + zz[ch, p][:, LANES:], vh[ch, p]], axis=0)), bkg[ch, p])
               for p in range(n_pairs)]
        for p in range(n_pairs):
            cs = csl(p)
            sp[p] = sp[p] * g_l[ch, p] + jnp.where(lane_s < hd, upd[p][:hd], upd[p][hd:])
            y = yw[p][:L] + y0[ch, p]
            mean = _pair_sum(y, first) * (1.0 / hd)
            yc = y - mean
            var = _pair_sum(yc * yc, first) * (1.0 / hd)
            yn = yc * lax.rsqrt(var + GN_EPS) * lnw_ref[:, cs] + lnb_ref[:, cs]
            bonus = _pair_sum(brk[rs, cs], first) * vh[ch, p]
            o_ref[rs, cs] = ((yn + bonus) * g_ref[rs, cs]).astype(o_ref.dtype)
    for p in range(n_pairs):
        s_ref[p] = sp[p]


def _rwkv_scan(r, k, v, a, ld, g, k_k, k_a, r_k, ln_w, ln_b, L=RWKV_CHUNK, tm=512):
    B, S, W = r.shape
    seq = pl.BlockSpec((None, tm, W), lambda b, c: (b, c, 0))
    vec = pl.BlockSpec((1, W), lambda b, c: (0, 0))
    return pl.pallas_call(
        functools.partial(_rwkv_scan_kernel, L=L),
        grid=(B, S // tm),
        in_specs=[seq] * 6 + [vec] * 5,
        out_specs=seq,
        out_shape=jax.ShapeDtypeStruct((B, S, W), BF16),
        scratch_shapes=[pltpu.VMEM((W // LANES, RWKV_HEAD, LANES), F32)],
        compiler_params=_cp("parallel", "arbitrary"),
        name="rwkv_scan",
    )(r, k, v, a, ld, g, k_k.reshape(1, W), k_a.reshape(1, W), r_k.reshape(1, W),
      ln_w.reshape(1, W), ln_b.reshape(1, W))


def _post_kernel(attn_ref, rw_ref, gate_ref, x_ref, mod_ref, wa_ref, wb_ref, wo_ref, g2_ref, wrh_ref, wrm_ref,
                 sug_ref, sd_ref, base_ref, h2p_ref, lg_ref):
    D = x_ref.shape[1]
    ya = jnp.dot(attn_ref[...], wa_ref[...], preferred_element_type=F32)
    yb = jnp.dot(rw_ref[...], wb_ref[...], preferred_element_type=F32)
    m = gate_ref[:, 0:D] * ya + gate_ref[:, D:] * yb
    x1 = x_ref[...] + mod_ref[2:3, :] * jnp.dot(m.astype(BF16), wo_ref[...], preferred_element_type=F32)
    y = x1 * lax.rsqrt(jnp.mean(x1 * x1, axis=-1, keepdims=True) + NORM_EPS) * g2_ref[...]
    h2 = y * (1.0 + mod_ref[4:5, :]) + mod_ref[3:4, :]
    hb = h2.astype(BF16)
    hm = (h2 - hb.astype(F32)).astype(BF16)
    lg_ref[...] = _nt(wrh_ref[...], hb) + _nt(wrh_ref[...], hm) + _nt(wrm_ref[...], hb)
    packed = _pack_halves(hb.astype(F32))
    half = packed.shape[1] // 2
    h2p_ref[0] = packed[:, :half]
    h2p_ref[1] = packed[:, half:]
    F = sd_ref.shape[0]
    gu = jnp.dot(hb, sug_ref[...], preferred_element_type=F32)
    shared = jnp.dot((_silu(gu[:, :F]) * gu[:, F:]).astype(BF16), sd_ref[...], preferred_element_type=F32)
    base_ref[...] = x1 + mod_ref[5:6, :] * shared


def _post(attn, rw, gate, x2, mod, wa, wb, wo, norm2_g, w_router_t, sug, sd, S, tm=512):
    N, D = x2.shape
    E = w_router_t.shape[0]
    wr_hi = w_router_t.astype(BF16)
    wr_mid = (w_router_t - wr_hi.astype(F32)).astype(BF16)
    tpb = S // tm
    row = lambda w: pl.BlockSpec((tm, w), lambda i: (i, 0))
    full = lambda a: pl.BlockSpec(a.shape, lambda i: (0, 0))
    return pl.pallas_call(
        _post_kernel,
        grid=(N // tm,),
        in_specs=[row(attn.shape[1]), row(rw.shape[1]), row(gate.shape[1]), row(D),
                  pl.BlockSpec((None, 6, D), lambda i: (i // tpb, 0, 0)),
                  full(wa), full(wb), full(wo), pl.BlockSpec((1, D), lambda i: (0, 0)), full(wr_hi), full(wr_mid),
                  full(sug), full(sd)],
        out_specs=[row(D), pl.BlockSpec((2, tm, D // 4), lambda i: (0, i, 0)), pl.BlockSpec((E, tm), lambda i: (0, i))],
        out_shape=[jax.ShapeDtypeStruct((N, D), F32), jax.ShapeDtypeStruct((2, N, D // 4), U32),
                   jax.ShapeDtypeStruct((E, N), F32)],
        compiler_params=_cp("parallel"),
        name="post_mixer",
    )(attn, rw, gate, x2, mod, wa, wb, wo, norm2_g.reshape(1, D), wr_hi, wr_mid, sug, sd)


def _first_argmax(x, idx, sentinel):
    m = jnp.max(x, axis=0, keepdims=True)
    return m, jnp.min(jnp.where(x == m, idx, sentinel), axis=0, keepdims=True)


def _route_kernel(lg_ref, bias_ref, e_ref, w_ref):
    E, T = lg_ref.shape
    gsz = E // N_GROUPS
    scores = _sigmoid(lg_ref[...])
    biased = scores + bias_ref[...]
    ig = lax.broadcasted_iota(I32, (gsz, T), 0)
    gs = []
    for g in range(N_GROUPS):
        blk = biased[g * gsz:(g + 1) * gsz, :]
        m1, i1 = _first_argmax(blk, ig, gsz)
        m2 = jnp.max(jnp.where(ig == i1, -jnp.inf, blk), axis=0, keepdims=True)
        gs.append(m1 + m2)
    gsc = jnp.concatenate(gs, axis=0)
    i8 = lax.broadcasted_iota(I32, (N_GROUPS, T), 0)
    chosen = jnp.zeros((N_GROUPS, T), F32)
    for _ in range(TOPK_GROUPS):
        _, gi = _first_argmax(gsc, i8, N_GROUPS)
        hit = i8 == gi
        chosen = jnp.where(hit, 1.0, chosen)
        gsc = jnp.where(hit, -jnp.inf, gsc)
    masked = jnp.concatenate(
        [jnp.where(chosen[g:g + 1, :] > 0.0, biased[g * gsz:(g + 1) * gsz, :], -jnp.inf)
         for g in range(N_GROUPS)], axis=0)
    ie = lax.broadcasted_iota(I32, (E, T), 0)
    idxs, wts = [], []
    for _ in range(TOP_K):
        _, ei = _first_argmax(masked, ie, E)
        hit = ie == ei
        idxs.append(ei)
        wts.append(jnp.sum(jnp.where(hit, scores, 0.0), axis=0, keepdims=True))
        masked = jnp.where(hit, -jnp.inf, masked)
    wt = jnp.concatenate(wts, axis=0)
    e_ref[...] = jnp.concatenate(idxs, axis=0)
    w_ref[...] = wt / jnp.sum(wt, axis=0, keepdims=True) * ROUTED_SCALE


def _route(logits_t, router_bias, T=512):
    E, N = logits_t.shape
    blk = pl.BlockSpec((TOP_K, T), lambda i: (0, i))
    return pl.pallas_call(
        _route_kernel,
        grid=(N // T,),
        in_specs=[pl.BlockSpec((E, T), lambda i: (0, i)), pl.BlockSpec((E, 1), lambda i: (0, 0))],
        out_specs=[blk, blk],
        out_shape=[jax.ShapeDtypeStruct((TOP_K, N), I32), jax.ShapeDtypeStruct((TOP_K, N), F32)],
        compiler_params=_cp("parallel"),
        name="route",
    )(logits_t, router_bias.reshape(E, 1))


def _rank_kernel(e_ref, rank_ref, cnt_ref, carry_ref, *, n_experts):
    T = e_ref.shape[1]

    @pl.when(pl.program_id(0) == 0)
    def _():
        carry_ref[...] = jnp.zeros_like(carry_ref)

    ie = lax.broadcasted_iota(I32, (n_experts, T), 0)
    e = e_ref[...]
    hits = [ie == e[kk:kk + 1, :] for kk in range(TOP_K)]
    onehot = jnp.zeros((n_experts, T), F32)
    for hsel in hits:
        onehot = onehot + hsel.astype(F32)
    tr = lax.broadcasted_iota(I32, (T, T), 0)
    tc = lax.broadcasted_iota(I32, (T, T), 1)
    before = (tr < tc).astype(BF16)
    base = _bdot(onehot, before) + carry_ref[:, 0:1]
    rank_ref[...] = jnp.concatenate(
        [jnp.sum(jnp.where(hsel, base, 0.0), axis=0, keepdims=True) for hsel in hits], axis=0).astype(I32)
    carry_ref[...] = carry_ref[...] + jnp.sum(onehot, axis=1, keepdims=True)
    cnt_ref[...] = carry_ref[...]


def _ranks(eidx_t, n_experts, T=512):
    N = eidx_t.shape[1]
    blk = pl.BlockSpec((TOP_K, T), lambda i: (0, i))
    cnt = pl.BlockSpec((n_experts, LANES), lambda i: (0, 0))
    return pl.pallas_call(
        functools.partial(_rank_kernel, n_experts=n_experts),
        grid=(N // T,),
        in_specs=[blk],
        out_specs=[blk, cnt],
        out_shape=[jax.ShapeDtypeStruct((TOP_K, N), I32), jax.ShapeDtypeStruct((n_experts, LANES), F32)],
        scratch_shapes=[pltpu.VMEM((n_experts, LANES), F32)],
        compiler_params=_cp("arbitrary"),
        name="ranks",
    )(eidx_t)


def _dest_kernel(e_ref, rank_ref, start_ref, d_ref):
    E = start_ref.shape[0]
    T = e_ref.shape[1]
    ie = lax.broadcasted_iota(I32, (E, T), 0)
    e = e_ref[...]
    start = start_ref[:, 0:1]
    rows = [jnp.sum(jnp.where(ie == e[kk:kk + 1, :], start, 0.0), axis=0, keepdims=True) for kk in range(TOP_K)]
    d_ref[...] = jnp.concatenate(rows, axis=0).astype(I32) + rank_ref[...]


def _dests(eidx_t, rank_t, pstart, T=512):
    N = eidx_t.shape[1]
    E = pstart.shape[0]
    blk = pl.BlockSpec((TOP_K, T), lambda i: (0, i))
    return pl.pallas_call(
        _dest_kernel,
        grid=(N // T,),
        in_specs=[blk, blk, pl.BlockSpec((E, LANES), lambda i: (0, 0))],
        out_specs=blk,
        out_shape=jax.ShapeDtypeStruct((TOP_K, N), I32),
        compiler_params=_cp("parallel"),
        name="dests",
    )(eidx_t, rank_t, jnp.broadcast_to(pstart.astype(F32)[:, None], (E, LANES)))


def _expert_kernel(us_ref, ps_ref, x_hbm, wug_ref, wd_ref, y_hbm,
                   wug_bf, wd_bf, xbuf, ybuf, cnt_ref, xsem, ysem, *, n_rows):
    e = pl.program_id(0)
    _, _, R, Ch = xbuf.shape
    C = 2 * Ch
    F = wd_ref.shape[0]

    def x_copy(slot, half, row):
        return pltpu.make_async_copy(x_hbm.at[half, pl.ds(row, R)], xbuf.at[slot, half], xsem.at[slot])

    def y_copy(slot, half, row):
        return pltpu.make_async_copy(ybuf.at[slot, half], y_hbm.at[half, pl.ds(row, R)], ysem.at[slot])

    def start(copy, slot, row):
        for half in range(2):
            copy(slot, half, row).start(priority=1)

    def wait(copy, slot):
        for half in range(2):
            copy(slot, half, 0).wait()

    @pl.when(e == 0)
    def _():
        cnt_ref[0] = 0
        for t in range(X_AHEAD):
            start(x_copy, t, t * R)
        ybuf[...] = jnp.zeros_like(ybuf)
        for slot in range(2):
            start(y_copy, slot, n_rows + slot * R)

    wug_bf[...] = wug_ref[...].astype(BF16)
    wd_bf[...] = wd_ref[...].astype(BF16)
    n_valid = us_ref[e + 1] - us_ref[e]
    pbase = ps_ref[e]
    n_tiles = lax.shift_right_logical(n_valid + (R - 1), int(math.log2(R)))

    def make_tile(nb):
        rowid = lax.broadcasted_iota(I32, (nb * R, C), 0)

        def tile(blk0):
            n_done = cnt_ref[0]
            xslots = [lax.rem(n_done + b, X_SLOTS) for b in range(nb)]
            for b in range(nb):
                ahead = n_done + X_AHEAD + b
                start(x_copy, lax.rem(ahead, X_SLOTS), pl.multiple_of(ahead * R, R))
            for b in range(nb):
                wait(x_copy, xslots[b])
            x = jnp.concatenate([jnp.concatenate([xbuf[xs, 0], xbuf[xs, 1]], axis=1) for xs in xslots], axis=0)
            x = jnp.where(rowid < n_valid - blk0 * R, x, jnp.uint32(0))
            lo, hi = _unpack_halves(x)
            gu = (jnp.dot(lo.astype(BF16), wug_bf[0:C, :], preferred_element_type=F32)
                  + jnp.dot(hi.astype(BF16), wug_bf[C:, :], preferred_element_type=F32))
            hid = (_silu(gu[:, :F]) * gu[:, F:]).astype(BF16)
            y = jnp.dot(hid, wd_bf[...], preferred_element_type=F32)
            packed = _pack_halves(y.astype(BF16).astype(F32))
            for b in range(nb):
                slot = (n_done + b) & 1
                wait(y_copy, slot)
                ybuf[slot, 0] = packed[b * R:(b + 1) * R, :Ch]
                ybuf[slot, 1] = packed[b * R:(b + 1) * R, Ch:]
                start(y_copy, slot, pl.multiple_of(pbase + (blk0 + b) * R, R))
            cnt_ref[0] = n_done + nb

        return tile

    pair_tile, single_tile = make_tile(2), make_tile(1)

    def pairs(j, carry):
        pair_tile(2 * j)
        return carry

    lax.fori_loop(0, lax.shift_right_logical(n_tiles, 1), pairs, 0)

    @pl.when((n_tiles & 1) == 1)
    def _():
        single_tile(n_tiles - 1)

    @pl.when(e == pl.num_programs(0) - 1)
    def _():
        for t in range(X_AHEAD):
            wait(x_copy, lax.rem(cnt_ref[0] + t, X_SLOTS))
        for slot in range(2):
            wait(y_copy, slot)
        ybuf[0] = jnp.zeros((2, R, Ch), U32)
        first = lax.shift_right_logical(pbase + n_tiles * R, int(math.log2(R)))
        n_left = n_rows // R - first

        def fill(t, carry):
            start(y_copy, 0, pl.multiple_of((first + t) * R, R))
            return carry

        def drain(t, carry):
            wait(y_copy, 0)
            return carry

        lax.fori_loop(0, n_left, fill, 0)
        lax.fori_loop(0, n_left, drain, 0)


def _experts(ustart, pstart, xg, w_ug, w_d, n_rows, R):
    _, _, Ch = xg.shape
    E, D, F2 = w_ug.shape
    F = w_d.shape[1]
    grid_spec = pltpu.PrefetchScalarGridSpec(
        num_scalar_prefetch=2,
        grid=(E,),
        in_specs=[pl.BlockSpec(memory_space=pl.ANY),
                  pl.BlockSpec((None, D, F2), lambda e, us, ps: (e, 0, 0)),
                  pl.BlockSpec((None, F, D), lambda e, us, ps: (e, 0, 0))],
        out_specs=pl.BlockSpec(memory_space=pl.ANY),
        scratch_shapes=[pltpu.VMEM((D, F2), BF16), pltpu.VMEM((F, D), BF16),
                        pltpu.VMEM((X_SLOTS, 2, R, Ch), U32), pltpu.VMEM((2, 2, R, Ch), U32), pltpu.SMEM((1,), I32),
                        pltpu.SemaphoreType.DMA((X_SLOTS,)), pltpu.SemaphoreType.DMA((2,))],
    )
    return pl.pallas_call(
        functools.partial(_expert_kernel, n_rows=n_rows),
        grid_spec=grid_spec,
        out_shape=jax.ShapeDtypeStruct((2, n_rows + 2 * R, Ch), U32),
        compiler_params=_cp("arbitrary"),
        name="experts",
    )(ustart, pstart, xg, w_ug, w_d)


def _sc_scatter_rows(src, idx, n_rows):
    H, N, C = src.shape
    K = idx.shape[0]
    per_row = N // SC_WINDOW
    mesh = plsc.VectorSubcoreMesh(core_axis_name="c", subcore_axis_name="s")

    @functools.partial(pl.kernel, out_type=jax.ShapeDtypeStruct((H, n_rows, C), src.dtype), mesh=mesh,
                       scratch_types=[])
    def scatter_kernel(x_hbm, i_hbm, o_hbm):
        for h in range(H):
            def body(x_vmem, i_vmem):
                for k in range(K):
                    pltpu.sync_copy(x_vmem, o_hbm.at[h].at[i_vmem.at[k]])

            pltpu.emit_pipeline(
                body,
                grid=(per_row,),
                in_specs=[pl.BlockSpec((SC_WINDOW, C), lambda i: (i, 0)),
                          pl.BlockSpec((K, SC_WINDOW), lambda i: (0, i))],
                out_specs=[],
                core_axis_name=("c", "s"),
                dimension_semantics=(pltpu.PARALLEL,),
            )(x_hbm.at[h], i_hbm)

    return scatter_kernel(src, idx)


def _sc_gather_rows(src, idx):
    H, _, C = src.shape
    K, N = idx.shape
    per_row = N // SC_WINDOW
    mesh = plsc.VectorSubcoreMesh(core_axis_name="c", subcore_axis_name="s")

    @functools.partial(pl.kernel, out_type=jax.ShapeDtypeStruct((H, K * N, C), src.dtype), mesh=mesh,
                       scratch_types=[])
    def gather_kernel(x_hbm, i_hbm, o_hbm):
        for h in range(H):
            def body(i_vmem, o_vmem):
                pltpu.sync_copy(x_hbm.at[h].at[i_vmem.at[0]], o_vmem)

            pltpu.emit_pipeline(
                body,
                grid=(K * per_row,),
                in_specs=[pl.BlockSpec((1, SC_WINDOW), lambda i: (i // per_row, i % per_row))],
                out_specs=[pl.BlockSpec((SC_WINDOW, C), lambda i: (i, 0))],
                core_axis_name=("c", "s"),
                dimension_semantics=(pltpu.PARALLEL,),
            )(i_hbm, o_hbm.at[h])

    return gather_kernel(src, idx)


def _combine_kernel(*refs):
    y_refs = refs[:2 * TOP_K]
    w_ref, base_ref, mod_ref, o_ref = refs[2 * TOP_K:]
    T = base_ref.shape[0]
    tr = lax.broadcasted_iota(I32, (T, T), 0)
    tc = lax.broadcasted_iota(I32, (T, T), 1)
    wcol = _nt((tr == tc).astype(F32), w_ref[...], precision=HI)
    acc = [None] * 4
    for kk in range(TOP_K):
        wk = wcol[:, kk:kk + 1]
        for half in range(2):
            lo, hi = _unpack_halves(y_refs[2 * kk + half][...])
            for q, val in ((half, lo), (2 + half, hi)):
                acc[q] = val * wk if acc[q] is None else acc[q] + val * wk
    o_ref[...] = base_ref[...] + mod_ref[5:6, :] * jnp.concatenate(acc, axis=1)


def _combine(yg, w_t, base, mod, S, T=512):
    N, D = base.shape
    C = yg.shape[2]
    tpb = S // T
    n_tiles = N // T
    row = pl.BlockSpec((T, D), lambda i: (i, 0))
    piece = lambda kk, half: pl.BlockSpec((None, T, C), lambda i: (half, kk * n_tiles + i, 0))
    return pl.pallas_call(
        _combine_kernel,
        grid=(n_tiles,),
        in_specs=[piece(kk, half) for kk in range(TOP_K) for half in range(2)] + [
            pl.BlockSpec((TOP_K, T), lambda i: (0, i)),
            row,
            pl.BlockSpec((None, 6, D), lambda i: (i // tpb, 0, 0))],
        out_specs=row,
        out_shape=jax.ShapeDtypeStruct((N, D), F32),
        compiler_params=_cp("parallel"),
        name="combine",
    )(*([yg] * (2 * TOP_K)), w_t, base, mod)


def _layer(x, c, positions, layer_idx, w_ada, b_ada, norm1_g, w_in, w_gate, b_gate,
           q_norm_g, k_norm_g, lambda_q1, lambda_k1, lambda_q2, lambda_k2, subln_g,
           rwkv_mu, w_decay0, w_decay2, a0, a2, g2, k_k, k_a, r_k, ln_x_w, ln_x_b,
           w_branch_a, w_branch_b, w_out, norm2_g, w_router, router_bias,
           w_expert_up_gate, w_expert_down, w_shared_up_gate, w_shared_down):
    B, S, D = x.shape
    N = B * S
    E = w_router.shape[1]
    da_width = w_branch_a.shape[0]
    rw_width = w_branch_b.shape[0]
    lambda_init = 0.8 - 0.6 * math.exp(-0.3 * layer_idx)

    mod = _adaln(c, w_ada, b_ada)
    x2 = x.reshape(N, D)
    w_cat = jnp.concatenate([w_in, w_gate], axis=1).astype(BF16)
    qn, kn, v, r_, k_, v_, a_, ld_, g_, gate = _mixer_in(
        x2, positions.reshape(N, 1), mod, norm1_g, w_cat, b_gate, q_norm_g, k_norm_g,
        rwkv_mu, w_decay0, w_decay2, a0, a2, g2, S, da_width, rw_width)

    lam_vecs = jnp.stack([lambda_q1, lambda_k1, lambda_q2, lambda_k2])
    score_bound = 1.01 * DA_HEAD_DIM ** 0.5 * jnp.max(jnp.abs(q_norm_g)) * jnp.max(jnp.abs(k_norm_g))
    attn = _diff_attention(qn, kn, v, score_bound, lam_vecs, subln_g, B, S, lambda_init)

    seq = lambda t: t.reshape(B, S, rw_width)
    rw = _rwkv_scan(seq(r_), seq(k_), seq(v_), seq(a_), seq(ld_), seq(g_), k_k, k_a, r_k.reshape(-1),
                    ln_x_w, ln_x_b).reshape(N, rw_width)

    base, h2p, logits_t = _post(attn, rw, gate, x2, mod, w_branch_a.astype(BF16), w_branch_b.astype(BF16),
                                w_out.astype(BF16), norm2_g, w_router.T,
                                w_shared_up_gate.astype(BF16), w_shared_down.astype(BF16), S)

    eidx_t, w_t = _route(logits_t, router_bias)
    rank_t, counts = _ranks(eidx_t, E)
    R = EXPERT_TILE
    cnt = counts[:, 0].astype(I32)
    ustart = jnp.concatenate([jnp.zeros((1,), I32), jnp.cumsum(cnt)])
    pcnt = (cnt + R - 1) // R * R
    pstart = jnp.cumsum(pcnt) - pcnt
    dest_p = _dests(eidx_t, rank_t, pstart)
    n_rows = (N * TOP_K + E * (R - 1) + R - 1) // R * R
    xg = _sc_scatter_rows(h2p, dest_p, n_rows + X_AHEAD * R)
    y = _experts(ustart, pstart, xg, w_expert_up_gate, w_expert_down, n_rows, R)
    yg = _sc_gather_rows(y, dest_p)
    out = _combine(yg, w_t, base, mod, S)
    return out.reshape(B, S, D)


def kernel(x, c, positions, w_ada, b_ada, norm1_g, w_in, w_gate, b_gate, q_norm_g, k_norm_g, lambda_q1, lambda_k1, lambda_q2, lambda_k2, subln_g, rwkv_mu, w_decay0, w_decay2, a0, a2, g2, k_k, k_a, r_k, ln_x_w, ln_x_b, w_branch_a, w_branch_b, w_out, norm2_g, w_router, router_bias, w_expert_up_gate, w_expert_down, w_shared_up_gate, w_shared_down):
    for l in range(w_ada.shape[0]):
        x = _layer(x, c, positions, l, w_ada[l], b_ada[l], norm1_g[l], w_in[l], w_gate[l], b_gate[l],
                   q_norm_g[l], k_norm_g[l], lambda_q1[l], lambda_k1[l], lambda_q2[l], lambda_k2[l],
                   subln_g[l], rwkv_mu[l], w_decay0[l], w_decay2[l], a0[l], a2[l], g2[l], k_k[l],
                   k_a[l], r_k[l], ln_x_w[l], ln_x_b[l], w_branch_a[l], w_branch_b[l], w_out[l],
                   norm2_g[l], w_router[l], router_bias[l], w_expert_up_gate[l], w_expert_down[l],
                   w_shared_up_gate[l], w_shared_down[l])
    return x
```

```python
import functools
import math

import jax
import jax.numpy as jnp
from jax import lax
from jax.experimental import pallas as pl
from jax.experimental.pallas import tpu as pltpu
from jax.experimental.pallas import tpu_sc as plsc

F32 = jnp.float32
BF16 = jnp.bfloat16
I32 = jnp.int32
U32 = jnp.uint32
HI = lax.Precision.HIGHEST

CHUNK = 64
ROPE_THETA = 10000.0
NORM_EPS = 1e-6
SUBLN_EPS = 1e-5
DA_HEAD_DIM = 64
RWKV_HEAD = 64
GN_EPS = 64e-5
TOP_K = 8
N_GROUPS = 8
TOPK_GROUPS = 4
ROUTED_SCALE = 2.5
EXPERT_TILE = 256
X_SLOTS = 8
Y_SLOTS = 4
X_AHEAD = X_SLOTS - 2
RWKV_CHUNK = 64
LANES = 128
SC_WINDOW = 128
NEG = -1e30
MAX_PLAIN_SCORE = 40.0
VMEM_LIMIT = 56 * 1024 * 1024


def _cp(*sem):
    return pltpu.CompilerParams(dimension_semantics=sem, vmem_limit_bytes=VMEM_LIMIT)


def _bdot(a, b):
    return jnp.dot(a.astype(BF16), b.astype(BF16), preferred_element_type=F32)


def _fdot(a, b):
    return jnp.dot(a, b, precision=HI, preferred_element_type=F32)


def _nt(a, b, precision=None):
    return lax.dot_general(a, b, (((1,), (1,)), ((), ())), precision=precision,
                           preferred_element_type=F32)


def _tn(a, b, precision=None):
    return lax.dot_general(a, b, (((0,), (0,)), ((), ())), precision=precision,
                           preferred_element_type=F32)


def _pack_halves(x):
    c = x.shape[1] // 2
    lo = lax.bitcast_convert_type(x[:, :c], U32)
    hi = lax.bitcast_convert_type(x[:, c:], U32)
    return (hi & jnp.uint32(0xFFFF0000)) | (lo >> 16)


def _unpack_halves(w):
    lo = lax.bitcast_convert_type(w << 16, F32)
    hi = lax.bitcast_convert_type(w & jnp.uint32(0xFFFF0000), F32)
    return lo, hi


def _sigmoid(x):
    return 1.0 / (1.0 + jnp.exp(-x))


def _silu(x):
    return x * _sigmoid(x)


def _ada_kernel(c_ref, w_ref, b_ref, o_ref):
    o_ref[...] = _fdot(_silu(c_ref[...]), w_ref[...]) + b_ref[...]


def _adaln(c, w_ada, b_ada):
    B, D = c.shape
    rows = -(-B // 8) * 8
    cpad = jnp.zeros((rows, D), F32).at[:B].set(c)
    n_out = w_ada.shape[1]
    out = pl.pallas_call(
        _ada_kernel,
        grid=(n_out // D,),
        in_specs=[pl.BlockSpec((rows, D), lambda j: (0, 0)),
                  pl.BlockSpec((D, D), lambda j: (0, j)),
                  pl.BlockSpec((1, D), lambda j: (0, j))],
        out_specs=pl.BlockSpec((rows, D), lambda j: (0, j)),
        out_shape=jax.ShapeDtypeStruct((rows, n_out), F32),
        compiler_params=_cp("arbitrary"),
        name="adaln",
    )(cpad, w_ada, b_ada.reshape(1, n_out))
    return out[:B].reshape(B, n_out // D, D)


def _mixer_in_kernel(x_ref, xprev_ref, mod_ref, g_ref, w_ref, bg_ref, pos_ref, invf_ref, qg_ref, kg_ref,
                     mu_ref, w0_ref, w2_ref, a0_ref, a2_ref, g2_ref,
                     qn_ref, kn_ref, v_ref, r_ref, k_ref, vr_ref, a_ref, ld_ref, gr_ref, gate_ref,
                     *, da_width, rw_cols, rw_width, q_scale, tiles_per_seq):
    tm = x_ref.shape[0]

    def modulated(x):
        y = x * lax.rsqrt(jnp.mean(x * x, axis=-1, keepdims=True) + NORM_EPS) * g_ref[...]
        return (y * (1.0 + mod_ref[1:2, :]) + mod_ref[0:1, :]).astype(BF16)

    def proj(hb, c0, width, step=512):
        parts = [jnp.dot(hb, w_ref[:, c0 + o:c0 + min(o + step, width)], preferred_element_type=F32)
                 for o in range(0, width, step)]
        return parts[0] if len(parts) == 1 else jnp.concatenate(parts, axis=1)

    h = modulated(x_ref[...])

    lane = lax.broadcasted_iota(I32, (tm, LANES), 1)
    first = lane < DA_HEAD_DIM
    lo_half = (lane & (DA_HEAD_DIM - 1)) < DA_HEAD_DIM // 2
    ang = pos_ref[...].astype(F32) * invf_ref[...]
    cos = jnp.cos(ang)
    sin = jnp.sin(ang)
    sin = jnp.where(lo_half, -sin, sin)
    for c0, dst, gn_ref, mult in ((0, qn_ref, qg_ref, q_scale), (da_width, kn_ref, kg_ref, 1.0)):
        raw = proj(h, c0, da_width)
        for blk in range(da_width // LANES):
            x = raw[:, blk * LANES:(blk + 1) * LANES]
            xx = x * x
            s_first = jnp.sum(jnp.where(first, xx, 0.0), axis=-1, keepdims=True)
            s_second = jnp.sum(jnp.where(first, 0.0, xx), axis=-1, keepdims=True)
            ms = jnp.where(first, s_first, s_second) * (1.0 / DA_HEAD_DIM)
            xn = x * lax.rsqrt(ms + NORM_EPS) * gn_ref[...]
            rot = jnp.where(lo_half, pltpu.roll(xn, LANES - DA_HEAD_DIM // 2, axis=1),
                            pltpu.roll(xn, DA_HEAD_DIM // 2, axis=1))
            dst[:, blk * LANES:(blk + 1) * LANES] = ((xn * cos + rot * sin) * mult).astype(dst.dtype)
    v_ref[...] = proj(h, 2 * da_width, da_width).astype(v_ref.dtype)

    c_rw = 3 * da_width
    p = proj(h, c_rw, rw_cols)
    p_before = proj(modulated(xprev_ref[...]), c_rw, rw_cols)
    seq_start = (pl.program_id(0) % tiles_per_seq) == 0
    last_prev = jnp.where(seq_start, 0.0, p_before[7:8, :])
    rowi = lax.broadcasted_iota(I32, p.shape, 0)
    prev = jnp.where(rowi == 0, last_prev, pltpu.roll(p, 1, axis=0))
    xs = p + (prev - p) * mu_ref[...]
    width = rw_width
    r_ref[...] = xs[:, 0:width]
    k_ref[...] = xs[:, width:2 * width]
    vr_ref[...] = xs[:, 2 * width:3 * width]
    xwa = xs[:, 3 * width:3 * width + LANES]
    xg = xs[:, 3 * width + LANES:]
    z = w0_ref[...] + _bdot(jnp.tanh(xwa), w2_ref[...])
    w = -(jnp.maximum(-z, 0.0) + jnp.log(1.0 + jnp.exp(-jnp.abs(z)))) - 0.5
    ld_ref[...] = -jnp.exp(w)
    a_ref[...] = _sigmoid(a0_ref[...] + _bdot(xwa, a2_ref[...]))
    gr_ref[...] = _bdot(_sigmoid(xg), g2_ref[...])

    gate_ref[...] = _sigmoid(proj(h, c_rw + rw_cols, gate_ref.shape[1]) + bg_ref[...]).astype(gate_ref.dtype)


def _mixer_in(x2, pos2, mod, norm1_g, w_cat, b_gate, q_norm_g, k_norm_g, mu, w_decay0, w_decay2, a0, a2, g2,
              S, da_width, rw_width, tm=512):
    N, D = x2.shape
    n_gate = b_gate.shape[0]
    rw_cols = mu.shape[0]
    tpb = S // tm
    d = DA_HEAD_DIM
    inv_freq = 1.0 / (ROPE_THETA ** (jnp.arange(0, d, 2, dtype=F32) / d))
    invf = jnp.tile(inv_freq, LANES // (d // 2)).reshape(1, LANES)
    dl, al = w_decay2.shape[0], a2.shape[0]
    assert dl + al == LANES and g2.shape[0] == LANES
    w2p = jnp.zeros((LANES, rw_width), F32).at[:dl].set(w_decay2)
    a2p = jnp.zeros((LANES, rw_width), F32).at[dl:].set(a2)
    kern = functools.partial(_mixer_in_kernel, da_width=da_width, rw_cols=rw_cols, rw_width=rw_width,
                             q_scale=d ** -0.5 * math.log2(math.e),
                             tiles_per_seq=tpb)
    row = lambda w: pl.BlockSpec((tm, w), lambda i: (i, 0))
    vec = lambda n: pl.BlockSpec((1, n), lambda i: (0, 0))
    mat = pl.BlockSpec((LANES, rw_width), lambda i: (0, 0))
    f32 = lambda w: jax.ShapeDtypeStruct((N, w), F32)
    bf16 = lambda w: jax.ShapeDtypeStruct((N, w), BF16)
    return pl.pallas_call(
        kern,
        grid=(N // tm,),
        in_specs=[row(D),
                  pl.BlockSpec((8, D), lambda i: (jnp.maximum(i * (tm // 8) - 1, 0), 0)),
                  pl.BlockSpec((None, 6, D), lambda i: (i // tpb, 0, 0)),
                  vec(D),
                  pl.BlockSpec(w_cat.shape, lambda i: (0, 0)),
                  vec(n_gate),
                  pl.BlockSpec((tm, 1), lambda i: (i, 0)),
                  vec(LANES), vec(LANES), vec(LANES),
                  vec(rw_cols), vec(rw_width), mat, vec(rw_width), mat, mat],
        out_specs=[row(da_width)] * 3 + [row(rw_width)] * 6 + [row(n_gate)],
        out_shape=[bf16(da_width)] * 3 + [f32(rw_width)] * 6 + [bf16(n_gate)],
        compiler_params=_cp("parallel"),
        name="mixer_in",
    )(x2, x2, mod, norm1_g.reshape(1, D), w_cat, b_gate.reshape(1, n_gate), pos2, invf,
      jnp.tile(q_norm_g, 2).reshape(1, LANES), jnp.tile(k_norm_g, 2).reshape(1, LANES),
      mu.reshape(1, rw_cols), w_decay0.reshape(1, rw_width), w2p, a0.reshape(1, rw_width), a2p, g2)


def _attn_kernel(flag_ref, q_ref, k_ref, v_ref, vt_ref, lam_ref, sg_ref, sgc_ref, o_ref,
                 qz_ref, m_ref, l_ref, acc_ref, lpt_ref, acct_ref, *, tq, lambda_init):
    i = pl.program_id(2)
    lane = lax.broadcasted_iota(I32, (tq, LANES), 1)
    q = q_ref[...]
    zero = jnp.zeros_like(q)
    qz_ref[0:tq, :] = jnp.where(lane < DA_HEAD_DIM, q, zero)
    qz_ref[tq:, :] = jnp.where(lane >= DA_HEAD_DIM, q, zero)
    bounded = flag_ref[0] == 1
    lv = lam_ref[...]
    lam = (jnp.exp(jnp.sum(lv[0:1] * lv[1:2], keepdims=True))
           - jnp.exp(jnp.sum(lv[2:3] * lv[3:4], keepdims=True)) + lambda_init)

    def run(step):
        def body(j, carry):
            step(j, False)
            return carry
        lax.fori_loop(0, i, body, 0)
        step(i, True)

    def plain_step(j, masked):
        off = pl.multiple_of(j * tq, tq)
        st = _nt(k_ref[pl.ds(off, tq), :], qz_ref[...])
        if masked:
            row = lax.broadcasted_iota(I32, st.shape, 0)
            col = lax.broadcasted_iota(I32, st.shape, 1)
            st = jnp.where((row // CHUNK) <= ((col & (tq - 1)) // CHUNK), st, NEG)
        pt = jnp.exp2(st)
        part = pt[0:8, :]
        for g in range(1, tq // 8):
            part = part + pt[8 * g:8 * g + 8, :]
        lpt_ref[...] += part
        acct_ref[...] += jnp.dot(vt_ref[:, pl.ds(off, tq)], pt.astype(BF16), preferred_element_type=F32)

    @pl.when(bounded)
    def _():
        lpt_ref[...] = jnp.zeros_like(lpt_ref)
        acct_ref[...] = jnp.zeros_like(acct_ref)

        def two_steps(p, carry):
            plain_step(2 * p, False)
            plain_step(2 * p + 1, False)
            return carry

        lax.fori_loop(0, lax.shift_right_logical(i, 1), two_steps, 0)

        @pl.when((i & 1) == 1)
        def _():
            plain_step(i - 1, False)
            plain_step(i, True)

        @pl.when((i & 1) == 0)
        def _():
            plain_step(i, True)

        lsum = jnp.sum(lpt_ref[...], axis=0, keepdims=True)
        ot = acct_ref[:, 0:tq] / lsum[:, 0:tq] - lam * (acct_ref[:, tq:] / lsum[:, tq:])
        ot = ot * lax.rsqrt(jnp.mean(ot * ot, axis=0, keepdims=True) + SUBLN_EPS) * sgc_ref[...]
        o_ref[...] = (ot * (1.0 - lambda_init)).T.astype(o_ref.dtype)

    def online_step(j, masked):
        off = pl.multiple_of(j * tq, tq)
        s = _nt(qz_ref[...], k_ref[pl.ds(off, tq), :])
        if masked:
            row = lax.broadcasted_iota(I32, s.shape, 0)
            col = lax.broadcasted_iota(I32, s.shape, 1)
            s = jnp.where((col // CHUNK) <= ((row & (tq - 1)) // CHUNK), s, NEG)
        m_old = m_ref[...]
        m_new = jnp.maximum(m_old, jnp.max(s, axis=-1, keepdims=True))
        alpha = jnp.exp2(m_old - m_new)
        pr = jnp.exp2(s - m_new)
        l_ref[...] = alpha * l_ref[...] + jnp.sum(pr, axis=-1, keepdims=True)
        acc_ref[...] = alpha * acc_ref[...] + jnp.dot(pr.astype(BF16), v_ref[pl.ds(off, tq), :],
                                                      preferred_element_type=F32)
        m_ref[...] = m_new

    @pl.when(jnp.logical_not(bounded))
    def _():
        m_ref[...] = jnp.full_like(m_ref, NEG)
        l_ref[...] = jnp.zeros_like(l_ref)
        acc_ref[...] = jnp.zeros_like(acc_ref)
        run(online_step)
        o = acc_ref[0:tq, :] / l_ref[0:tq, :] - lam * (acc_ref[tq:, :] / l_ref[tq:, :])
        o = o * lax.rsqrt(jnp.mean(o * o, axis=-1, keepdims=True) + SUBLN_EPS) * sg_ref[...]
        o_ref[...] = (o * (1.0 - lambda_init)).astype(o_ref.dtype)


def _diff_attention(qn, kn, v, score_bound, lam_vecs, subln_g, B, S, lambda_init, tq=512):
    W = qn.shape[1]
    H = W // LANES
    q3 = qn.reshape(B, S, W)
    k3 = kn.reshape(B, S, W)
    v3 = v.reshape(B, S, W)
    vt3 = v3.transpose(0, 2, 1)
    flag = (score_bound <= MAX_PLAIN_SCORE).astype(I32).reshape(1)
    qblk = pl.BlockSpec((None, tq, LANES), lambda b, h, i, f: (b, i, h))
    kvblk = pl.BlockSpec((None, S, LANES), lambda b, h, i, f: (b, 0, h))
    grid_spec = pltpu.PrefetchScalarGridSpec(
        num_scalar_prefetch=1,
        grid=(B, H, S // tq),
        in_specs=[qblk, kvblk, kvblk,
                  pl.BlockSpec((None, LANES, S), lambda b, h, i, f: (b, h, 0)),
                  pl.BlockSpec((4, DA_HEAD_DIM), lambda b, h, i, f: (0, 0)),
                  pl.BlockSpec((1, LANES), lambda b, h, i, f: (0, 0)),
                  pl.BlockSpec((LANES, 1), lambda b, h, i, f: (0, 0))],
        out_specs=qblk,
        scratch_shapes=[pltpu.VMEM((2 * tq, LANES), BF16),
                        pltpu.VMEM((2 * tq, 1), F32),
                        pltpu.VMEM((2 * tq, 1), F32),
                        pltpu.VMEM((2 * tq, LANES), F32),
                        pltpu.VMEM((8, 2 * tq), F32),
                        pltpu.VMEM((LANES, 2 * tq), F32)],
    )
    out = pl.pallas_call(
        functools.partial(_attn_kernel, tq=tq, lambda_init=lambda_init),
        grid_spec=grid_spec,
        out_shape=jax.ShapeDtypeStruct((B, S, W), BF16),
        compiler_params=_cp("parallel", "parallel", "arbitrary"),
        name="diff_attn",
    )(flag, q3, k3, v3, vt3, lam_vecs, subln_g.reshape(1, LANES), subln_g.reshape(LANES, 1))
    return out.reshape(B * S, W)


def _stackmask(m):
    lane = lax.broadcasted_iota(I32, m.shape, 1)
    z = jnp.zeros_like(m)
    return jnp.concatenate([jnp.where(lane < RWKV_HEAD, m, z), jnp.where(lane >= RWKV_HEAD, m, z)], axis=0)


def _pair_sum(x, first):
    s1 = jnp.sum(jnp.where(first, x, 0.0), axis=-1, keepdims=True)
    s2 = jnp.sum(jnp.where(first, 0.0, x), axis=-1, keepdims=True)
    return jnp.where(first, s1, s2)


def _rwkv_scan_kernel(r_ref, k_ref, v_ref, a_ref, ld_ref, g_ref, kk_ref, ka_ref, rk_ref, lnw_ref, lnb_ref,
                      o_ref, s_ref, *, L):
    tm, W = r_ref.shape
    n_chunks = tm // L
    n_pairs = W // LANES
    hd = RWKV_HEAD
    bf = lambda t: t.astype(BF16)

    @pl.when(pl.program_id(1) == 0)
    def _():
        s_ref[...] = jnp.zeros_like(s_ref)

    row = lax.broadcasted_iota(I32, (tm, tm), 0)
    col = lax.broadcasted_iota(I32, (tm, tm), 1)
    tri = jnp.where(jnp.logical_and(col <= row, (col // L) == (row // L)), 1.0, 0.0).astype(BF16)
    ld = ld_ref[...]
    ld_hi = bf(ld)
    rem = ld - ld_hi.astype(F32)
    ld_mid = bf(rem)
    ld_lo = bf(rem - ld_mid.astype(F32))
    c = (jnp.dot(tri, ld_hi, preferred_element_type=F32) + jnp.dot(tri, ld_mid, preferred_element_type=F32)
         + jnp.dot(tri, ld_lo, preferred_element_type=F32))
    ec = jnp.exp(c)
    eci = jnp.exp(-c)
    ecm = jnp.exp(c - ld)
    r = r_ref[...]
    k = k_ref[...]
    v = v_ref[...]
    a = a_ref[...]
    kkr = k * kk_ref[...]
    kmod = k * (1.0 + (a - 1.0) * ka_ref[...])
    brk = r * kmod * rk_ref[...]

    lane = lax.broadcasted_iota(I32, (L, LANES), 1)
    rowl = lax.broadcasted_iota(I32, (L, LANES), 0)
    first = lane < hd
    lane_h = lane & (hd - 1)
    strict = lane_h < rowl
    incl = lane_h <= rowl
    eye = jnp.where(lane_h == rowl, 1.0, 0.0)

    chains = [(ch, p) for ch in range(n_chunks) for p in range(n_pairs)]
    rsl = lambda ch: slice(ch * L, (ch + 1) * L)
    csl = lambda p: slice(p * LANES, (p + 1) * LANES)
    fdot = lambda x, y: jnp.dot(x, y, preferred_element_type=F32)
    at, bt, kt, rt, vh, g_l = {}, {}, {}, {}, {}, {}
    for c_ in chains:
        ch, p = c_
        rs, cs = rsl(ch), csl(p)
        kkh = kkr[rs, cs]
        kkh = kkh / jnp.maximum(jnp.sqrt(_pair_sum(kkh * kkh, first)), 1e-12)
        vh[c_] = v[rs, cs]
        g_l[c_] = ec[ch * L + L - 1:ch * L + L, cs]
        at[c_] = -kkh * ecm[rs, cs]
        bt[c_] = kkh * a[rs, cs] * eci[rs, cs]
        kt[c_] = kmod[rs, cs] * eci[rs, cs]
        rt[c_] = r[rs, cs] * ec[rs, cs]
    gm = {c_: _nt(bf(jnp.concatenate([at[c_], rt[c_]], axis=0)),
                  jnp.concatenate([_stackmask(bf(bt[c_])), _stackmask(bf(kt[c_]))], axis=0)) for c_ in chains}
    a_ab = {c_: jnp.where(strict, gm[c_][:L, :LANES], 0.0) for c_ in chains}
    vsm = {c_: _stackmask(bf(vh[c_])) for c_ in chains}
    cmat = {c_: fdot(bf(jnp.where(strict, gm[c_][:L, LANES:], 0.0)), vsm[c_]) for c_ in chains}
    t_inv = {c_: eye + a_ab[c_] for c_ in chains}
    pw = {c_: bf(a_ab[c_]) for c_ in chains}
    for _ in range(int(math.log2(L)) - 1):
        pw = {c_: bf(fdot(pw[c_], _stackmask(pw[c_]))) for c_ in chains}
        t_inv = {c_: t_inv[c_] + fdot(pw[c_], _stackmask(bf(t_inv[c_]))) for c_ in chains}
    zz = {c_: fdot(bf(t_inv[c_]), jnp.concatenate([_stackmask(bf(at[c_])), _stackmask(bf(cmat[c_]))], axis=1))
          for c_ in chains}
    qy = {c_: fdot(bf(jnp.where(incl, gm[c_][L:, :LANES], 0.0)),
                   jnp.concatenate([_stackmask(bf(zz[c_][:, :LANES])), _stackmask(bf(zz[c_][:, LANES:]))], axis=1))
          for c_ in chains}
    y0 = {c_: qy[c_][:, LANES:] + fdot(bf(jnp.where(incl, gm[c_][L:, LANES:], 0.0)), vsm[c_]) for c_ in chains}
    qa = {c_: bf(jnp.concatenate([rt[c_] + qy[c_][:, :LANES], zz[c_][:, :LANES]], axis=0)) for c_ in chains}
    bkg = {c_: bf(jnp.concatenate([bt[c_] * g_l[c_], kt[c_] * g_l[c_]], axis=0)) for c_ in chains}

    lane_s = lax.broadcasted_iota(I32, (hd, LANES), 1)
    sp = [s_ref[p] for p in range(n_pairs)]
    for ch in range(n_chunks):
        rs = rsl(ch)
        yw = [_nt(qa[ch, p], _stackmask(bf(sp[p]))) for p in range(n_pairs)]
        upd = [_tn(bf(jnp.concatenate([yw[p][L:] + zz[ch, p][:, LANES:], vh[ch, p]], axis=0)), bkg[ch, p])
               for p in range(n_pairs)]
        for p in range(n_pairs):
            cs = csl(p)
            sp[p] = sp[p] * g_l[ch, p] + jnp.where(lane_s < hd, upd[p][:hd], upd[p][hd:])
            y = yw[p][:L] + y0[ch, p]
            mean = _pair_sum(y, first) * (1.0 / hd)
            yc = y - mean
            var = _pair_sum(yc * yc, first) * (1.0 / hd)
            yn = yc * lax.rsqrt(var + GN_EPS) * lnw_ref[:, cs] + lnb_ref[:, cs]
            bonus = _pair_sum(brk[rs, cs], first) * vh[ch, p]
            o_ref[rs, cs] = ((yn + bonus) * g_ref[rs, cs]).astype(o_ref.dtype)
    for p in range(n_pairs):
        s_ref[p] = sp[p]


def _rwkv_scan(r, k, v, a, ld, g, k_k, k_a, r_k, ln_w, ln_b, L=RWKV_CHUNK, tm=512):
    B, S, W = r.shape
    seq = pl.BlockSpec((None, tm, W), lambda b, c: (b, c, 0))
    vec = pl.BlockSpec((1, W), lambda b, c: (0, 0))
    return pl.pallas_call(
        functools.partial(_rwkv_scan_kernel, L=L),
        grid=(B, S // tm),
        in_specs=[seq] * 6 + [vec] * 5,
        out_specs=seq,
        out_shape=jax.ShapeDtypeStruct((B, S, W), BF16),
        scratch_shapes=[pltpu.VMEM((W // LANES, RWKV_HEAD, LANES), F32)],
        compiler_params=_cp("parallel", "arbitrary"),
        name="rwkv_scan",
    )(r, k, v, a, ld, g, k_k.reshape(1, W), k_a.reshape(1, W), r_k.reshape(1, W),
      ln_w.reshape(1, W), ln_b.reshape(1, W))


def _post_kernel(attn_ref, rw_ref, gate_ref, x_ref, mod_ref, wa_ref, wb_ref, wo_ref, g2_ref, wrh_ref, wrm_ref,
                 sug_ref, sd_ref, base_ref, h2p_ref, lg_ref):
    D = x_ref.shape[1]
    ya = jnp.dot(attn_ref[...], wa_ref[...], preferred_element_type=F32)
    yb = jnp.dot(rw_ref[...], wb_ref[...], preferred_element_type=F32)
    m = gate_ref[:, 0:D] * ya + gate_ref[:, D:] * yb
    x1 = x_ref[...] + mod_ref[2:3, :] * jnp.dot(m.astype(BF16), wo_ref[...], preferred_element_type=F32)
    y = x1 * lax.rsqrt(jnp.mean(x1 * x1, axis=-1, keepdims=True) + NORM_EPS) * g2_ref[...]
    h2 = y * (1.0 + mod_ref[4:5, :]) + mod_ref[3:4, :]
    hb = h2.astype(BF16)
    hm = (h2 - hb.astype(F32)).astype(BF16)
    lg_ref[...] = _nt(wrh_ref[...], hb) + _nt(wrh_ref[...], hm) + _nt(wrm_ref[...], hb)
    packed = _pack_halves(hb.astype(F32))
    half = packed.shape[1] // 2
    h2p_ref[0] = packed[:, :half]
    h2p_ref[1] = packed[:, half:]
    F = sd_ref.shape[0]
    gu = jnp.dot(hb, sug_ref[...], preferred_element_type=F32)
    shared = jnp.dot((_silu(gu[:, :F]) * gu[:, F:]).astype(BF16), sd_ref[...], preferred_element_type=F32)
    base_ref[...] = x1 + mod_ref[5:6, :] * shared


def _post(attn, rw, gate, x2, mod, wa, wb, wo, norm2_g, w_router_t, sug, sd, S, tm=512):
    N, D = x2.shape
    E = w_router_t.shape[0]
    wr_hi = w_router_t.astype(BF16)
    wr_mid = (w_router_t - wr_hi.astype(F32)).astype(BF16)
    tpb = S // tm
    row = lambda w: pl.BlockSpec((tm, w), lambda i: (i, 0))
    full = lambda a: pl.BlockSpec(a.shape, lambda i: (0, 0))
    return pl.pallas_call(
        _post_kernel,
        grid=(N // tm,),
        in_specs=[row(attn.shape[1]), row(rw.shape[1]), row(gate.shape[1]), row(D),
                  pl.BlockSpec((None, 6, D), lambda i: (i // tpb, 0, 0)),
                  full(wa), full(wb), full(wo), pl.BlockSpec((1, D), lambda i: (0, 0)), full(wr_hi), full(wr_mid),
                  full(sug), full(sd)],
        out_specs=[row(D), pl.BlockSpec((2, tm, D // 4), lambda i: (0, i, 0)), pl.BlockSpec((E, tm), lambda i: (0, i))],
        out_shape=[jax.ShapeDtypeStruct((N, D), F32), jax.ShapeDtypeStruct((2, N, D // 4), U32),
                   jax.ShapeDtypeStruct((E, N), F32)],
        compiler_params=_cp("parallel"),
        name="post_mixer",
    )(attn, rw, gate, x2, mod, wa, wb, wo, norm2_g.reshape(1, D), wr_hi, wr_mid, sug, sd)


def _first_argmax(x, idx, sentinel):
    m = jnp.max(x, axis=0, keepdims=True)
    return m, jnp.min(jnp.where(x == m, idx, sentinel), axis=0, keepdims=True)


def _route_kernel(lg_ref, bias_ref, e_ref, w_ref):
    E, T = lg_ref.shape
    gsz = E // N_GROUPS
    scores = _sigmoid(lg_ref[...])
    biased = scores + bias_ref[...]
    ig = lax.broadcasted_iota(I32, (gsz, T), 0)
    gs = []
    for g in range(N_GROUPS):
        blk = biased[g * gsz:(g + 1) * gsz, :]
        m1, i1 = _first_argmax(blk, ig, gsz)
        m2 = jnp.max(jnp.where(ig == i1, -jnp.inf, blk), axis=0, keepdims=True)
        gs.append(m1 + m2)
    gsc = jnp.concatenate(gs, axis=0)
    i8 = lax.broadcasted_iota(I32, (N_GROUPS, T), 0)
    chosen = jnp.zeros((N_GROUPS, T), F32)
    for _ in range(TOPK_GROUPS):
        _, gi = _first_argmax(gsc, i8, N_GROUPS)
        hit = i8 == gi
        chosen = jnp.where(hit, 1.0, chosen)
        gsc = jnp.where(hit, -jnp.inf, gsc)
    masked = jnp.concatenate(
        [jnp.where(chosen[g:g + 1, :] > 0.0, biased[g * gsz:(g + 1) * gsz, :], -jnp.inf)
         for g in range(N_GROUPS)], axis=0)
    ie = lax.broadcasted_iota(I32, (E, T), 0)
    idxs, wts = [], []
    for _ in range(TOP_K):
        _, ei = _first_argmax(masked, ie, E)
        hit = ie == ei
        idxs.append(ei)
        wts.append(jnp.sum(jnp.where(hit, scores, 0.0), axis=0, keepdims=True))
        masked = jnp.where(hit, -jnp.inf, masked)
    wt = jnp.concatenate(wts, axis=0)
    e_ref[...] = jnp.concatenate(idxs, axis=0)
    w_ref[...] = wt / jnp.sum(wt, axis=0, keepdims=True) * ROUTED_SCALE


def _route(logits_t, router_bias, T=512):
    E, N = logits_t.shape
    blk = pl.BlockSpec((TOP_K, T), lambda i: (0, i))
    return pl.pallas_call(
        _route_kernel,
        grid=(N // T,),
        in_specs=[pl.BlockSpec((E, T), lambda i: (0, i)), pl.BlockSpec((E, 1), lambda i: (0, 0))],
        out_specs=[blk, blk],
        out_shape=[jax.ShapeDtypeStruct((TOP_K, N), I32), jax.ShapeDtypeStruct((TOP_K, N), F32)],
        compiler_params=_cp("parallel"),
        name="route",
    )(logits_t, router_bias.reshape(E, 1))


def _rank_kernel(e_ref, rank_ref, cnt_ref, carry_ref, *, n_experts):
    T = e_ref.shape[1]

    @pl.when(pl.program_id(0) == 0)
    def _():
        carry_ref[...] = jnp.zeros_like(carry_ref)

    ie = lax.broadcasted_iota(I32, (n_experts, T), 0)
    e = e_ref[...]
    hits = [ie == e[kk:kk + 1, :] for kk in range(TOP_K)]
    onehot = jnp.zeros((n_experts, T), F32)
    for hsel in hits:
        onehot = onehot + hsel.astype(F32)
    tr = lax.broadcasted_iota(I32, (T, T), 0)
    tc = lax.broadcasted_iota(I32, (T, T), 1)
    before = (tr < tc).astype(BF16)
    base = _bdot(onehot, before) + carry_ref[:, 0:1]
    rank_ref[...] = jnp.concatenate(
        [jnp.sum(jnp.where(hsel, base, 0.0), axis=0, keepdims=True) for hsel in hits], axis=0).astype(I32)
    carry_ref[...] = carry_ref[...] + jnp.sum(onehot, axis=1, keepdims=True)
    cnt_ref[...] = carry_ref[...]


def _ranks(eidx_t, n_experts, T=512):
    N = eidx_t.shape[1]
    blk = pl.BlockSpec((TOP_K, T), lambda i: (0, i))
    cnt = pl.BlockSpec((n_experts, LANES), lambda i: (0, 0))
    return pl.pallas_call(
        functools.partial(_rank_kernel, n_experts=n_experts),
        grid=(N // T,),
        in_specs=[blk],
        out_specs=[blk, cnt],
        out_shape=[jax.ShapeDtypeStruct((TOP_K, N), I32), jax.ShapeDtypeStruct((n_experts, LANES), F32)],
        scratch_shapes=[pltpu.VMEM((n_experts, LANES), F32)],
        compiler_params=_cp("arbitrary"),
        name="ranks",
    )(eidx_t)


def _dest_kernel(e_ref, rank_ref, start_ref, d_ref):
    E = start_ref.shape[0]
    T = e_ref.shape[1]
    ie = lax.broadcasted_iota(I32, (E, T), 0)
    e = e_ref[...]
    start = start_ref[:, 0:1]
    rows = [jnp.sum(jnp.where(ie == e[kk:kk + 1, :], start, 0.0), axis=0, keepdims=True) for kk in range(TOP_K)]
    d_ref[...] = jnp.concatenate(rows, axis=0).astype(I32) + rank_ref[...]


def _dests(eidx_t, rank_t, pstart, T=512):
    N = eidx_t.shape[1]
    E = pstart.shape[0]
    blk = pl.BlockSpec((TOP_K, T), lambda i: (0, i))
    return pl.pallas_call(
        _dest_kernel,
        grid=(N // T,),
        in_specs=[blk, blk, pl.BlockSpec((E, LANES), lambda i: (0, 0))],
        out_specs=blk,
        out_shape=jax.ShapeDtypeStruct((TOP_K, N), I32),
        compiler_params=_cp("parallel"),
        name="dests",
    )(eidx_t, rank_t, jnp.broadcast_to(pstart.astype(F32)[:, None], (E, LANES)))


def _expert_kernel(us_ref, ps_ref, x_hbm, wug_ref, wd_ref, y_hbm,
                   wug_bf, wd_bf, xbuf, ybuf, cnt_ref, xsem, ysem, *, n_rows):
    e = pl.program_id(0)
    _, _, R, Ch = xbuf.shape
    C = 2 * Ch
    F = wd_ref.shape[0]

    def x_copy(slot, half, row):
        return pltpu.make_async_copy(x_hbm.at[half, pl.ds(row, R)], xbuf.at[slot, half], xsem.at[slot])

    def y_copy(slot, half, row):
        return pltpu.make_async_copy(ybuf.at[slot, half], y_hbm.at[half, pl.ds(row, R)], ysem.at[slot])

    def start(copy, slot, row):
        for half in range(2):
            copy(slot, half, row).start(priority=1)

    def wait(copy, slot):
        for half in range(2):
            copy(slot, half, 0).wait()

    @pl.when(e == 0)
    def _():
        cnt_ref[0] = 0
        for t in range(X_AHEAD):
            start(x_copy, t, t * R)
        ybuf[...] = jnp.zeros_like(ybuf)
        for slot in range(Y_SLOTS):
            start(y_copy, slot, n_rows + slot * R)

    wug_bf[...] = wug_ref[...].astype(BF16)
    wd_bf[...] = wd_ref[...].astype(BF16)
    n_valid = us_ref[e + 1] - us_ref[e]
    pbase = ps_ref[e]
    n_tiles = lax.shift_right_logical(n_valid + (R - 1), int(math.log2(R)))

    def make_tile(nb):
        rowid = lax.broadcasted_iota(I32, (nb * R, C), 0)

        def tile(blk0):
            n_done = cnt_ref[0]
            xslots = [lax.rem(n_done + b, X_SLOTS) for b in range(nb)]
            for b in range(nb):
                ahead = n_done + X_AHEAD + b
                start(x_copy, lax.rem(ahead, X_SLOTS), pl.multiple_of(ahead * R, R))
            for b in range(nb):
                wait(x_copy, xslots[b])
            x = jnp.concatenate([jnp.concatenate([xbuf[xs, 0], xbuf[xs, 1]], axis=1) for xs in xslots], axis=0)
            x = jnp.where(rowid < n_valid - blk0 * R, x, jnp.uint32(0))
            lo, hi = _unpack_halves(x)
            gu = (jnp.dot(lo.astype(BF16), wug_bf[0:C, :], preferred_element_type=F32)
                  + jnp.dot(hi.astype(BF16), wug_bf[C:, :], preferred_element_type=F32))
            hid = (_silu(gu[:, :F]) * gu[:, F:]).astype(BF16)
            y = jnp.dot(hid, wd_bf[...], preferred_element_type=F32)
            packed = _pack_halves(y.astype(BF16).astype(F32))
            for b in range(nb):
                slot = lax.rem(n_done + b, Y_SLOTS)
                wait(y_copy, slot)
                ybuf[slot, 0] = packed[b * R:(b + 1) * R, :Ch]
                ybuf[slot, 1] = packed[b * R:(b + 1) * R, Ch:]
                start(y_copy, slot, pl.multiple_of(pbase + (blk0 + b) * R, R))
            cnt_ref[0] = n_done + nb

        return tile

    pair_tile, single_tile = make_tile(2), make_tile(1)

    def pairs(j, carry):
        pair_tile(2 * j)
        return carry

    lax.fori_loop(0, lax.shift_right_logical(n_tiles, 1), pairs, 0)

    @pl.when((n_tiles & 1) == 1)
    def _():
        single_tile(n_tiles - 1)

    @pl.when(e == pl.num_programs(0) - 1)
    def _():
        for t in range(X_AHEAD):
            wait(x_copy, lax.rem(cnt_ref[0] + t, X_SLOTS))
        for slot in range(Y_SLOTS):
            wait(y_copy, slot)
        ybuf[0] = jnp.zeros((2, R, Ch), U32)
        first = lax.shift_right_logical(pbase + n_tiles * R, int(math.log2(R)))
        n_left = n_rows // R - first

        def fill(t, carry):
            start(y_copy, 0, pl.multiple_of((first + t) * R, R))
            return carry

        def drain(t, carry):
            wait(y_copy, 0)
            return carry

        lax.fori_loop(0, n_left, fill, 0)
        lax.fori_loop(0, n_left, drain, 0)


def _experts(ustart, pstart, xg, w_ug, w_d, n_rows, R):
    _, _, Ch = xg.shape
    E, D, F2 = w_ug.shape
    F = w_d.shape[1]
    grid_spec = pltpu.PrefetchScalarGridSpec(
        num_scalar_prefetch=2,
        grid=(E,),
        in_specs=[pl.BlockSpec(memory_space=pl.ANY),
                  pl.BlockSpec((None, D, F2), lambda e, us, ps: (e, 0, 0)),
                  pl.BlockSpec((None, F, D), lambda e, us, ps: (e, 0, 0))],
        out_specs=pl.BlockSpec(memory_space=pl.ANY),
        scratch_shapes=[pltpu.VMEM((D, F2), BF16), pltpu.VMEM((F, D), BF16),
                        pltpu.VMEM((X_SLOTS, 2, R, Ch), U32), pltpu.VMEM((Y_SLOTS, 2, R, Ch), U32),
                        pltpu.SMEM((1,), I32), pltpu.SemaphoreType.DMA((X_SLOTS,)), pltpu.SemaphoreType.DMA((Y_SLOTS,))],
    )
    return pl.pallas_call(
        functools.partial(_expert_kernel, n_rows=n_rows),
        grid_spec=grid_spec,
        out_shape=jax.ShapeDtypeStruct((2, n_rows + Y_SLOTS * R, Ch), U32),
        compiler_params=_cp("arbitrary"),
        name="experts",
    )(ustart, pstart, xg, w_ug, w_d)


def _sc_scatter_rows(src, idx, n_rows):
    H, N, C = src.shape
    K = idx.shape[0]
    per_row = N // SC_WINDOW
    mesh = plsc.VectorSubcoreMesh(core_axis_name="c", subcore_axis_name="s")

    @functools.partial(pl.kernel, out_type=jax.ShapeDtypeStruct((H, n_rows, C), src.dtype), mesh=mesh,
                       scratch_types=[])
    def scatter_kernel(x_hbm, i_hbm, o_hbm):
        for h in range(H):
            def body(x_vmem, i_vmem):
                for k in range(K):
                    pltpu.sync_copy(x_vmem, o_hbm.at[h].at[i_vmem.at[k]])

            pltpu.emit_pipeline(
                body,
                grid=(per_row,),
                in_specs=[pl.BlockSpec((SC_WINDOW, C), lambda i: (i, 0)),
                          pl.BlockSpec((K, SC_WINDOW), lambda i: (0, i))],
                out_specs=[],
                core_axis_name=("c", "s"),
                dimension_semantics=(pltpu.PARALLEL,),
            )(x_hbm.at[h], i_hbm)

    return scatter_kernel(src, idx)


def _sc_gather_rows(src, idx):
    H, _, C = src.shape
    K, N = idx.shape
    per_row = N // SC_WINDOW
    mesh = plsc.VectorSubcoreMesh(core_axis_name="c", subcore_axis_name="s")

    @functools.partial(pl.kernel, out_type=jax.ShapeDtypeStruct((H, K * N, C), src.dtype), mesh=mesh,
                       scratch_types=[])
    def gather_kernel(x_hbm, i_hbm, o_hbm):
        for h in range(H):
            def body(i_vmem, o_vmem):
                pltpu.sync_copy(x_hbm.at[h].at[i_vmem.at[0]], o_vmem)

            pltpu.emit_pipeline(
                body,
                grid=(K * per_row,),
                in_specs=[pl.BlockSpec((1, SC_WINDOW), lambda i: (i // per_row, i % per_row))],
                out_specs=[pl.BlockSpec((SC_WINDOW, C), lambda i: (i, 0))],
                core_axis_name=("c", "s"),
                dimension_semantics=(pltpu.PARALLEL,),
            )(i_hbm, o_hbm.at[h])

    return gather_kernel(src, idx)


def _combine_kernel(*refs):
    y_refs = refs[:2 * TOP_K]
    w_ref, base_ref, mod_ref, o_ref = refs[2 * TOP_K:]
    T = base_ref.shape[0]
    tr = lax.broadcasted_iota(I32, (T, T), 0)
    tc = lax.broadcasted_iota(I32, (T, T), 1)
    wcol = _nt((tr == tc).astype(F32), w_ref[...], precision=HI)
    acc = [None] * 4
    for kk in range(TOP_K):
        wk = wcol[:, kk:kk + 1]
        for half in range(2):
            lo, hi = _unpack_halves(y_refs[2 * kk + half][...])
            for q, val in ((half, lo), (2 + half, hi)):
                acc[q] = val * wk if acc[q] is None else acc[q] + val * wk
    o_ref[...] = base_ref[...] + mod_ref[5:6, :] * jnp.concatenate(acc, axis=1)


def _combine(yg, w_t, base, mod, S, T=512):
    N, D = base.shape
    C = yg.shape[2]
    tpb = S // T
    n_tiles = N // T
    row = pl.BlockSpec((T, D), lambda i: (i, 0))
    piece = lambda kk, half: pl.BlockSpec((None, T, C), lambda i: (half, kk * n_tiles + i, 0))
    return pl.pallas_call(
        _combine_kernel,
        grid=(n_tiles,),
        in_specs=[piece(kk, half) for kk in range(TOP_K) for half in range(2)] + [
            pl.BlockSpec((TOP_K, T), lambda i: (0, i)),
            row,
            pl.BlockSpec((None, 6, D), lambda i: (i // tpb, 0, 0))],
        out_specs=row,
        out_shape=jax.ShapeDtypeStruct((N, D), F32),
        compiler_params=_cp("parallel"),
        name="combine",
    )(*([yg] * (2 * TOP_K)), w_t, base, mod)


def _layer(x, c, positions, layer_idx, w_ada, b_ada, norm1_g, w_in, w_gate, b_gate,
           q_norm_g, k_norm_g, lambda_q1, lambda_k1, lambda_q2, lambda_k2, subln_g,
           rwkv_mu, w_decay0, w_decay2, a0, a2, g2, k_k, k_a, r_k, ln_x_w, ln_x_b,
           w_branch_a, w_branch_b, w_out, norm2_g, w_router, router_bias,
           w_expert_up_gate, w_expert_down, w_shared_up_gate, w_shared_down):
    B, S, D = x.shape
    N = B * S
    E = w_router.shape[1]
    da_width = w_branch_a.shape[0]
    rw_width = w_branch_b.shape[0]
    lambda_init = 0.8 - 0.6 * math.exp(-0.3 * layer_idx)

    mod = _adaln(c, w_ada, b_ada)
    x2 = x.reshape(N, D)
    w_cat = jnp.concatenate([w_in, w_gate], axis=1).astype(BF16)
    qn, kn, v, r_, k_, v_, a_, ld_, g_, gate = _mixer_in(
        x2, positions.reshape(N, 1), mod, norm1_g, w_cat, b_gate, q_norm_g, k_norm_g,
        rwkv_mu, w_decay0, w_decay2, a0, a2, g2, S, da_width, rw_width)

    lam_vecs = jnp.stack([lambda_q1, lambda_k1, lambda_q2, lambda_k2])
    score_bound = 1.01 * DA_HEAD_DIM ** 0.5 * jnp.max(jnp.abs(q_norm_g)) * jnp.max(jnp.abs(k_norm_g))
    attn = _diff_attention(qn, kn, v, score_bound, lam_vecs, subln_g, B, S, lambda_init)

    seq = lambda t: t.reshape(B, S, rw_width)
    rw = _rwkv_scan(seq(r_), seq(k_), seq(v_), seq(a_), seq(ld_), seq(g_), k_k, k_a, r_k.reshape(-1),
                    ln_x_w, ln_x_b).reshape(N, rw_width)

    base, h2p, logits_t = _post(attn, rw, gate, x2, mod, w_branch_a.astype(BF16), w_branch_b.astype(BF16),
                                w_out.astype(BF16), norm2_g, w_router.T,
                                w_shared_up_gate.astype(BF16), w_shared_down.astype(BF16), S)

    eidx_t, w_t = _route(logits_t, router_bias)
    rank_t, counts = _ranks(eidx_t, E)
    R = EXPERT_TILE
    cnt = counts[:, 0].astype(I32)
    ustart = jnp.concatenate([jnp.zeros((1,), I32), jnp.cumsum(cnt)])
    pcnt = (cnt + R - 1) // R * R
    pstart = jnp.cumsum(pcnt) - pcnt
    dest_p = _dests(eidx_t, rank_t, pstart)
    n_rows = (N * TOP_K + E * (R - 1) + R - 1) // R * R
    xg = _sc_scatter_rows(h2p, dest_p, n_rows + X_AHEAD * R)
    y = _experts(ustart, pstart, xg, w_expert_up_gate, w_expert_down, n_rows, R)
    yg = _sc_gather_rows(y, dest_p)
    out = _combine(yg, w_t, base, mod, S)
    return out.reshape(B, S, D)


def kernel(x, c, positions, w_ada, b_ada, norm1_g, w_in, w_gate, b_gate, q_norm_g, k_norm_g, lambda_q1, lambda_k1, lambda_q2, lambda_k2, subln_g, rwkv_mu, w_decay0, w_decay2, a0, a2, g2, k_k, k_a, r_k, ln_x_w, ln_x_b, w_branch_a, w_branch_b, w_out, norm2_g, w_router, router_bias, w_expert_up_gate, w_expert_down, w_shared_up_gate, w_shared_down):
    for l in range(w_ada.shape[0]):
        x = _layer(x, c, positions, l, w_ada[l], b_ada[l], norm1_g[l], w_in[l], w_gate[l], b_gate[l],
                   q_norm_g[l], k_norm_g[l], lambda_q1[l], lambda_k1[l], lambda_q2[l], lambda_k2[l],
                   subln_g[l], rwkv_mu[l], w_decay0[l], w_decay2[l], a0[l], a2[l], g2[l], k_k[l],
                   k_a[l], r_k[l], ln_x_w[l], ln_x_b[l], w_branch_a[l], w_branch_b[l], w_out[l],
                   norm2_g[l], w_router[l], router_bias[l], w_expert_up_gate[l], w_expert_down[l],
                   w_shared_up_gate[l], w_shared_down[l])
    return x
```

```python
import functools
import math

import jax
import jax.numpy as jnp
from jax import lax
from jax.experimental import pallas as pl
from jax.experimental.pallas import tpu as pltpu
from jax.experimental.pallas import tpu_sc as plsc

F32 = jnp.float32
BF16 = jnp.bfloat16
I32 = jnp.int32
U32 = jnp.uint32
HI = lax.Precision.HIGHEST

CHUNK = 64
ROPE_THETA = 10000.0
NORM_EPS = 1e-6
SUBLN_EPS = 1e-5
DA_HEAD_DIM = 64
RWKV_HEAD = 64
GN_EPS = 64e-5
TOP_K = 8
N_GROUPS = 8
TOPK_GROUPS = 4
ROUTED_SCALE = 2.5
EXPERT_TILE = 256
W_SLOTS = 3
X_SLOTS = 6
X_AHEAD = X_SLOTS - 2
RWKV_CHUNK = 64
LANES = 128
SC_WINDOW = 128
NEG = -1e30
MAX_PLAIN_SCORE = 40.0
VMEM_LIMIT = 56 * 1024 * 1024


def _cp(*sem):
    return pltpu.CompilerParams(dimension_semantics=sem, vmem_limit_bytes=VMEM_LIMIT)


def _bdot(a, b):
    return jnp.dot(a.astype(BF16), b.astype(BF16), preferred_element_type=F32)


def _fdot(a, b):
    return jnp.dot(a, b, precision=HI, preferred_element_type=F32)


def _nt(a, b, precision=None):
    return lax.dot_general(a, b, (((1,), (1,)), ((), ())), precision=precision,
                           preferred_element_type=F32)


def _tn(a, b, precision=None):
    return lax.dot_general(a, b, (((0,), (0,)), ((), ())), precision=precision,
                           preferred_element_type=F32)


def _pack_halves(x):
    c = x.shape[1] // 2
    lo = lax.bitcast_convert_type(x[:, :c], U32)
    hi = lax.bitcast_convert_type(x[:, c:], U32)
    return (hi & jnp.uint32(0xFFFF0000)) | (lo >> 16)


def _unpack_halves(w):
    lo = lax.bitcast_convert_type(w << 16, F32)
    hi = lax.bitcast_convert_type(w & jnp.uint32(0xFFFF0000), F32)
    return lo, hi


def _sigmoid(x):
    return 1.0 / (1.0 + jnp.exp(-x))


def _silu(x):
    return x * _sigmoid(x)


def _ada_kernel(c_ref, w_ref, b_ref, o_ref):
    o_ref[...] = _fdot(_silu(c_ref[...]), w_ref[...]) + b_ref[...]


def _adaln(c, w_ada, b_ada):
    B, D = c.shape
    rows = -(-B // 8) * 8
    cpad = jnp.zeros((rows, D), F32).at[:B].set(c)
    n_out = w_ada.shape[1]
    out = pl.pallas_call(
        _ada_kernel,
        grid=(n_out // D,),
        in_specs=[pl.BlockSpec((rows, D), lambda j: (0, 0)),
                  pl.BlockSpec((D, D), lambda j: (0, j)),
                  pl.BlockSpec((1, D), lambda j: (0, j))],
        out_specs=pl.BlockSpec((rows, D), lambda j: (0, j)),
        out_shape=jax.ShapeDtypeStruct((rows, n_out), F32),
        compiler_params=_cp("arbitrary"),
        name="adaln",
    )(cpad, w_ada, b_ada.reshape(1, n_out))
    return out[:B].reshape(B, n_out // D, D)


def _mixer_in_kernel(x_ref, xprev_ref, mod_ref, g_ref, w_ref, bg_ref, pos_ref, invf_ref, qg_ref, kg_ref,
                     mu_ref, w0_ref, w2_ref, a0_ref, a2_ref, g2_ref,
                     qn_ref, kn_ref, v_ref, r_ref, k_ref, vr_ref, a_ref, ld_ref, gr_ref, gate_ref,
                     *, da_width, rw_cols, rw_width, q_scale, tiles_per_seq):
    tm = x_ref.shape[0]

    def modulated(x):
        y = x * lax.rsqrt(jnp.mean(x * x, axis=-1, keepdims=True) + NORM_EPS) * g_ref[...]
        return (y * (1.0 + mod_ref[1:2, :]) + mod_ref[0:1, :]).astype(BF16)

    def proj(hb, c0, width, step=512):
        parts = [jnp.dot(hb, w_ref[:, c0 + o:c0 + min(o + step, width)], preferred_element_type=F32)
                 for o in range(0, width, step)]
        return parts[0] if len(parts) == 1 else jnp.concatenate(parts, axis=1)

    h = modulated(x_ref[...])

    lane = lax.broadcasted_iota(I32, (tm, LANES), 1)
    first = lane < DA_HEAD_DIM
    lo_half = (lane & (DA_HEAD_DIM - 1)) < DA_HEAD_DIM // 2
    ang = pos_ref[...].astype(F32) * invf_ref[...]
    cos = jnp.cos(ang)
    sin = jnp.sin(ang)
    sin = jnp.where(lo_half, -sin, sin)
    for c0, dst, gn_ref, mult in ((0, qn_ref, qg_ref, q_scale), (da_width, kn_ref, kg_ref, 1.0)):
        raw = proj(h, c0, da_width)
        for blk in range(da_width // LANES):
            x = raw[:, blk * LANES:(blk + 1) * LANES]
            xx = x * x
            s_first = jnp.sum(jnp.where(first, xx, 0.0), axis=-1, keepdims=True)
            s_second = jnp.sum(jnp.where(first, 0.0, xx), axis=-1, keepdims=True)
            ms = jnp.where(first, s_first, s_second) * (1.0 / DA_HEAD_DIM)
            xn = x * lax.rsqrt(ms + NORM_EPS) * gn_ref[...]
            rot = jnp.where(lo_half, pltpu.roll(xn, LANES - DA_HEAD_DIM // 2, axis=1),
                            pltpu.roll(xn, DA_HEAD_DIM // 2, axis=1))
            dst[:, blk * LANES:(blk + 1) * LANES] = ((xn * cos + rot * sin) * mult).astype(dst.dtype)
    v_ref[...] = proj(h, 2 * da_width, da_width).astype(v_ref.dtype)

    c_rw = 3 * da_width
    p = proj(h, c_rw, rw_cols)
    p_before = proj(modulated(xprev_ref[...]), c_rw, rw_cols)
    seq_start = (pl.program_id(0) % tiles_per_seq) == 0
    last_prev = jnp.where(seq_start, 0.0, p_before[7:8, :])
    rowi = lax.broadcasted_iota(I32, p.shape, 0)
    prev = jnp.where(rowi == 0, last_prev, pltpu.roll(p, 1, axis=0))
    xs = p + (prev - p) * mu_ref[...]
    width = rw_width
    r_ref[...] = xs[:, 0:width]
    k_ref[...] = xs[:, width:2 * width]
    vr_ref[...] = xs[:, 2 * width:3 * width]
    xwa = xs[:, 3 * width:3 * width + LANES]
    xg = xs[:, 3 * width + LANES:]
    z = w0_ref[...] + _bdot(jnp.tanh(xwa), w2_ref[...])
    w = -(jnp.maximum(-z, 0.0) + jnp.log(1.0 + jnp.exp(-jnp.abs(z)))) - 0.5
    ld_ref[...] = -jnp.exp(w)
    a_ref[...] = _sigmoid(a0_ref[...] + _bdot(xwa, a2_ref[...]))
    gr_ref[...] = _bdot(_sigmoid(xg), g2_ref[...])

    gate_ref[...] = _sigmoid(proj(h, c_rw + rw_cols, gate_ref.shape[1]) + bg_ref[...]).astype(gate_ref.dtype)


def _mixer_in(x2, pos2, mod, norm1_g, w_cat, b_gate, q_norm_g, k_norm_g, mu, w_decay0, w_decay2, a0, a2, g2,
              S, da_width, rw_width, tm=512):
    N, D = x2.shape
    n_gate = b_gate.shape[0]
    rw_cols = mu.shape[0]
    tpb = S // tm
    d = DA_HEAD_DIM
    inv_freq = 1.0 / (ROPE_THETA ** (jnp.arange(0, d, 2, dtype=F32) / d))
    invf = jnp.tile(inv_freq, LANES // (d // 2)).reshape(1, LANES)
    dl, al = w_decay2.shape[0], a2.shape[0]
    assert dl + al == LANES and g2.shape[0] == LANES
    w2p = jnp.zeros((LANES, rw_width), F32).at[:dl].set(w_decay2)
    a2p = jnp.zeros((LANES, rw_width), F32).at[dl:].set(a2)
    kern = functools.partial(_mixer_in_kernel, da_width=da_width, rw_cols=rw_cols, rw_width=rw_width,
                             q_scale=d ** -0.5 * math.log2(math.e),
                             tiles_per_seq=tpb)
    row = lambda w: pl.BlockSpec((tm, w), lambda i: (i, 0))
    vec = lambda n: pl.BlockSpec((1, n), lambda i: (0, 0))
    mat = pl.BlockSpec((LANES, rw_width), lambda i: (0, 0))
    f32 = lambda w: jax.ShapeDtypeStruct((N, w), F32)
    bf16 = lambda w: jax.ShapeDtypeStruct((N, w), BF16)
    return pl.pallas_call(
        kern,
        grid=(N // tm,),
        in_specs=[row(D),
                  pl.BlockSpec((8, D), lambda i: (jnp.maximum(i * (tm // 8) - 1, 0), 0)),
                  pl.BlockSpec((None, 6, D), lambda i: (i // tpb, 0, 0)),
                  vec(D),
                  pl.BlockSpec(w_cat.shape, lambda i: (0, 0)),
                  vec(n_gate),
                  pl.BlockSpec((tm, 1), lambda i: (i, 0)),
                  vec(LANES), vec(LANES), vec(LANES),
                  vec(rw_cols), vec(rw_width), mat, vec(rw_width), mat, mat],
        out_specs=[row(da_width)] * 3 + [row(rw_width)] * 6 + [row(n_gate)],
        out_shape=[bf16(da_width)] * 3 + [f32(rw_width)] * 6 + [bf16(n_gate)],
        compiler_params=_cp("parallel"),
        name="mixer_in",
    )(x2, x2, mod, norm1_g.reshape(1, D), w_cat, b_gate.reshape(1, n_gate), pos2, invf,
      jnp.tile(q_norm_g, 2).reshape(1, LANES), jnp.tile(k_norm_g, 2).reshape(1, LANES),
      mu.reshape(1, rw_cols), w_decay0.reshape(1, rw_width), w2p, a0.reshape(1, rw_width), a2p, g2)


def _attn_kernel(flag_ref, q_ref, k_ref, v_ref, vt_ref, lam_ref, sg_ref, sgc_ref, o_ref,
                 qz_ref, m_ref, l_ref, acc_ref, lpt_ref, acct_ref, *, tq, lambda_init):
    i = pl.program_id(2)
    lane = lax.broadcasted_iota(I32, (tq, LANES), 1)
    q = q_ref[...]
    zero = jnp.zeros_like(q)
    qz_ref[0:tq, :] = jnp.where(lane < DA_HEAD_DIM, q, zero)
    qz_ref[tq:, :] = jnp.where(lane >= DA_HEAD_DIM, q, zero)
    bounded = flag_ref[0] == 1
    lv = lam_ref[...]
    lam = (jnp.exp(jnp.sum(lv[0:1] * lv[1:2], keepdims=True))
           - jnp.exp(jnp.sum(lv[2:3] * lv[3:4], keepdims=True)) + lambda_init)

    def run(step):
        def body(j, carry):
            step(j, False)
            return carry
        lax.fori_loop(0, i, body, 0)
        step(i, True)

    def plain_step(j, masked):
        off = pl.multiple_of(j * tq, tq)
        st = _nt(k_ref[pl.ds(off, tq), :], qz_ref[...])
        if masked:
            row = lax.broadcasted_iota(I32, st.shape, 0)
            col = lax.broadcasted_iota(I32, st.shape, 1)
            st = jnp.where((row // CHUNK) <= ((col & (tq - 1)) // CHUNK), st, NEG)
        pt = jnp.exp2(st)
        part = pt[0:8, :]
        for g in range(1, tq // 8):
            part = part + pt[8 * g:8 * g + 8, :]
        lpt_ref[...] += part
        acct_ref[...] += jnp.dot(vt_ref[:, pl.ds(off, tq)], pt.astype(BF16), preferred_element_type=F32)

    @pl.when(bounded)
    def _():
        lpt_ref[...] = jnp.zeros_like(lpt_ref)
        acct_ref[...] = jnp.zeros_like(acct_ref)

        def two_steps(p, carry):
            plain_step(2 * p, False)
            plain_step(2 * p + 1, False)
            return carry

        lax.fori_loop(0, lax.shift_right_logical(i, 1), two_steps, 0)

        @pl.when((i & 1) == 1)
        def _():
            plain_step(i - 1, False)
            plain_step(i, True)

        @pl.when((i & 1) == 0)
        def _():
            plain_step(i, True)

        lsum = jnp.sum(lpt_ref[...], axis=0, keepdims=True)
        ot = acct_ref[:, 0:tq] / lsum[:, 0:tq] - lam * (acct_ref[:, tq:] / lsum[:, tq:])
        ot = ot * lax.rsqrt(jnp.mean(ot * ot, axis=0, keepdims=True) + SUBLN_EPS) * sgc_ref[...]
        o_ref[...] = (ot * (1.0 - lambda_init)).T.astype(o_ref.dtype)

    def online_step(j, masked):
        off = pl.multiple_of(j * tq, tq)
        s = _nt(qz_ref[...], k_ref[pl.ds(off, tq), :])
        if masked:
            row = lax.broadcasted_iota(I32, s.shape, 0)
            col = lax.broadcasted_iota(I32, s.shape, 1)
            s = jnp.where((col // CHUNK) <= ((row & (tq - 1)) // CHUNK), s, NEG)
        m_old = m_ref[...]
        m_new = jnp.maximum(m_old, jnp.max(s, axis=-1, keepdims=True))
        alpha = jnp.exp2(m_old - m_new)
        pr = jnp.exp2(s - m_new)
        l_ref[...] = alpha * l_ref[...] + jnp.sum(pr, axis=-1, keepdims=True)
        acc_ref[...] = alpha * acc_ref[...] + jnp.dot(pr.astype(BF16), v_ref[pl.ds(off, tq), :],
                                                      preferred_element_type=F32)
        m_ref[...] = m_new

    @pl.when(jnp.logical_not(bounded))
    def _():
        m_ref[...] = jnp.full_like(m_ref, NEG)
        l_ref[...] = jnp.zeros_like(l_ref)
        acc_ref[...] = jnp.zeros_like(acc_ref)
        run(online_step)
        o = acc_ref[0:tq, :] / l_ref[0:tq, :] - lam * (acc_ref[tq:, :] / l_ref[tq:, :])
        o = o * lax.rsqrt(jnp.mean(o * o, axis=-1, keepdims=True) + SUBLN_EPS) * sg_ref[...]
        o_ref[...] = (o * (1.0 - lambda_init)).astype(o_ref.dtype)


def _diff_attention(qn, kn, v, score_bound, lam_vecs, subln_g, B, S, lambda_init, tq=512):
    W = qn.shape[1]
    H = W // LANES
    q3 = qn.reshape(B, S, W)
    k3 = kn.reshape(B, S, W)
    v3 = v.reshape(B, S, W)
    vt3 = v3.transpose(0, 2, 1)
    flag = (score_bound <= MAX_PLAIN_SCORE).astype(I32).reshape(1)
    qblk = pl.BlockSpec((None, tq, LANES), lambda b, h, i, f: (b, i, h))
    kvblk = pl.BlockSpec((None, S, LANES), lambda b, h, i, f: (b, 0, h))
    grid_spec = pltpu.PrefetchScalarGridSpec(
        num_scalar_prefetch=1,
        grid=(B, H, S // tq),
        in_specs=[qblk, kvblk, kvblk,
                  pl.BlockSpec((None, LANES, S), lambda b, h, i, f: (b, h, 0)),
                  pl.BlockSpec((4, DA_HEAD_DIM), lambda b, h, i, f: (0, 0)),
                  pl.BlockSpec((1, LANES), lambda b, h, i, f: (0, 0)),
                  pl.BlockSpec((LANES, 1), lambda b, h, i, f: (0, 0))],
        out_specs=qblk,
        scratch_shapes=[pltpu.VMEM((2 * tq, LANES), BF16),
                        pltpu.VMEM((2 * tq, 1), F32),
                        pltpu.VMEM((2 * tq, 1), F32),
                        pltpu.VMEM((2 * tq, LANES), F32),
                        pltpu.VMEM((8, 2 * tq), F32),
                        pltpu.VMEM((LANES, 2 * tq), F32)],
    )
    out = pl.pallas_call(
        functools.partial(_attn_kernel, tq=tq, lambda_init=lambda_init),
        grid_spec=grid_spec,
        out_shape=jax.ShapeDtypeStruct((B, S, W), BF16),
        compiler_params=_cp("parallel", "parallel", "arbitrary"),
        name="diff_attn",
    )(flag, q3, k3, v3, vt3, lam_vecs, subln_g.reshape(1, LANES), subln_g.reshape(LANES, 1))
    return out.reshape(B * S, W)


def _stackmask(m):
    lane = lax.broadcasted_iota(I32, m.shape, 1)
    z = jnp.zeros_like(m)
    return jnp.concatenate([jnp.where(lane < RWKV_HEAD, m, z), jnp.where(lane >= RWKV_HEAD, m, z)], axis=0)


def _pair_sum(x, first):
    s1 = jnp.sum(jnp.where(first, x, 0.0), axis=-1, keepdims=True)
    s2 = jnp.sum(jnp.where(first, 0.0, x), axis=-1, keepdims=True)
    return jnp.where(first, s1, s2)


def _rwkv_scan_kernel(r_ref, k_ref, v_ref, a_ref, ld_ref, g_ref, kk_ref, ka_ref, rk_ref, lnw_ref, lnb_ref,
                      o_ref, s_ref, *, L):
    tm, W = r_ref.shape
    n_chunks = tm // L
    n_pairs = W // LANES
    hd = RWKV_HEAD
    bf = lambda t: t.astype(BF16)

    @pl.when(pl.program_id(1) == 0)
    def _():
        s_ref[...] = jnp.zeros_like(s_ref)

    row = lax.broadcasted_iota(I32, (tm, tm), 0)
    col = lax.broadcasted_iota(I32, (tm, tm), 1)
    tri = jnp.where(jnp.logical_and(col <= row, (col // L) == (row // L)), 1.0, 0.0).astype(BF16)
    ld = ld_ref[...]
    ld_hi = bf(ld)
    rem = ld - ld_hi.astype(F32)
    ld_mid = bf(rem)
    ld_lo = bf(rem - ld_mid.astype(F32))
    c = (jnp.dot(tri, ld_hi, preferred_element_type=F32) + jnp.dot(tri, ld_mid, preferred_element_type=F32)
         + jnp.dot(tri, ld_lo, preferred_element_type=F32))
    ec = jnp.exp(c)
    eci = jnp.exp(-c)
    ecm = jnp.exp(c - ld)
    r = r_ref[...]
    k = k_ref[...]
    v = v_ref[...]
    a = a_ref[...]
    kkr = k * kk_ref[...]
    kmod = k * (1.0 + (a - 1.0) * ka_ref[...])
    brk = r * kmod * rk_ref[...]

    lane = lax.broadcasted_iota(I32, (L, LANES), 1)
    rowl = lax.broadcasted_iota(I32, (L, LANES), 0)
    first = lane < hd
    lane_h = lane & (hd - 1)
    strict = lane_h < rowl
    incl = lane_h <= rowl
    eye = jnp.where(lane_h == rowl, 1.0, 0.0)

    chains = [(ch, p) for ch in range(n_chunks) for p in range(n_pairs)]
    rsl = lambda ch: slice(ch * L, (ch + 1) * L)
    csl = lambda p: slice(p * LANES, (p + 1) * LANES)
    fdot = lambda x, y: jnp.dot(x, y, preferred_element_type=F32)
    at, bt, kt, rt, vh, g_l = {}, {}, {}, {}, {}, {}
    for c_ in chains:
        ch, p = c_
        rs, cs = rsl(ch), csl(p)
        kkh = kkr[rs, cs]
        kkh = kkh / jnp.maximum(jnp.sqrt(_pair_sum(kkh * kkh, first)), 1e-12)
        vh[c_] = v[rs, cs]
        g_l[c_] = ec[ch * L + L - 1:ch * L + L, cs]
        at[c_] = -kkh * ecm[rs, cs]
        bt[c_] = kkh * a[rs, cs] * eci[rs, cs]
        kt[c_] = kmod[rs, cs] * eci[rs, cs]
        rt[c_] = r[rs, cs] * ec[rs, cs]
    gm = {c_: _nt(bf(jnp.concatenate([at[c_], rt[c_]], axis=0)),
                  jnp.concatenate([_stackmask(bf(bt[c_])), _stackmask(bf(kt[c_]))], axis=0)) for c_ in chains}
    a_ab = {c_: jnp.where(strict, gm[c_][:L, :LANES], 0.0) for c_ in chains}
    vsm = {c_: _stackmask(bf(vh[c_])) for c_ in chains}
    cmat = {c_: fdot(bf(jnp.where(strict, gm[c_][:L, LANES:], 0.0)), vsm[c_]) for c_ in chains}
    t_inv = {c_: eye + a_ab[c_] for c_ in chains}
    pw = {c_: bf(a_ab[c_]) for c_ in chains}
    for _ in range(int(math.log2(L)) - 1):
        pw = {c_: bf(fdot(pw[c_], _stackmask(pw[c_]))) for c_ in chains}
        t_inv = {c_: t_inv[c_] + fdot(pw[c_], _stackmask(bf(t_inv[c_]))) for c_ in chains}
    zz = {c_: fdot(bf(t_inv[c_]), jnp.concatenate([_stackmask(bf(at[c_])), _stackmask(bf(cmat[c_]))], axis=1))
          for c_ in chains}
    qy = {c_: fdot(bf(jnp.where(incl, gm[c_][L:, :LANES], 0.0)),
                   jnp.concatenate([_stackmask(bf(zz[c_][:, :LANES])), _stackmask(bf(zz[c_][:, LANES:]))], axis=1))
          for c_ in chains}
    y0 = {c_: qy[c_][:, LANES:] + fdot(bf(jnp.where(incl, gm[c_][L:, LANES:], 0.0)), vsm[c_]) for c_ in chains}
    qa = {c_: bf(jnp.concatenate([rt[c_] + qy[c_][:, :LANES], zz[c_][:, :LANES]], axis=0)) for c_ in chains}
    bkg = {c_: bf(jnp.concatenate([bt[c_] * g_l[c_], kt[c_] * g_l[c_]], axis=0)) for c_ in chains}

    lane_s = lax.broadcasted_iota(I32, (hd, LANES), 1)
    sp = [s_ref[p] for p in range(n_pairs)]
    for ch in range(n_chunks):
        rs = rsl(ch)
        yw = [_nt(qa[ch, p], _stackmask(bf(sp[p]))) for p in range(n_pairs)]
        upd = [_tn(bf(jnp.concatenate([yw[p][L:] + zz[ch, p][:, LANES:], vh[ch, p]], axis=0)), bkg[ch, p])
               for p in range(n_pairs)]
        for p in range(n_pairs):
            cs = csl(p)
            sp[p] = sp[p] * g_l[ch, p] + jnp.where(lane_s < hd, upd[p][:hd], upd[p][hd:])
            y = yw[p][:L] + y0[ch, p]
            mean = _pair_sum(y, first) * (1.0 / hd)
            yc = y - mean
            var = _pair_sum(yc * yc, first) * (1.0 / hd)
            yn = yc * lax.rsqrt(var + GN_EPS) * lnw_ref[:, cs] + lnb_ref[:, cs]
            bonus = _pair_sum(brk[rs, cs], first) * vh[ch, p]
            o_ref[rs, cs] = ((yn + bonus) * g_ref[rs, cs]).astype(o_ref.dtype)
    for p in range(n_pairs):
        s_ref[p] = sp[p]


def _rwkv_scan(r, k, v, a, ld, g, k_k, k_a, r_k, ln_w, ln_b, L=RWKV_CHUNK, tm=512):
    B, S, W = r.shape
    seq = pl.BlockSpec((None, tm, W), lambda b, c: (b, c, 0))
    vec = pl.BlockSpec((1, W), lambda b, c: (0, 0))
    return pl.pallas_call(
        functools.partial(_rwkv_scan_kernel, L=L),
        grid=(B, S // tm),
        in_specs=[seq] * 6 + [vec] * 5,
        out_specs=seq,
        out_shape=jax.ShapeDtypeStruct((B, S, W), BF16),
        scratch_shapes=[pltpu.VMEM((W // LANES, RWKV_HEAD, LANES), F32)],
        compiler_params=_cp("parallel", "arbitrary"),
        name="rwkv_scan",
    )(r, k, v, a, ld, g, k_k.reshape(1, W), k_a.reshape(1, W), r_k.reshape(1, W),
      ln_w.reshape(1, W), ln_b.reshape(1, W))


def _post_kernel(attn_ref, rw_ref, gate_ref, x_ref, mod_ref, wa_ref, wb_ref, wo_ref, g2_ref, wrh_ref, wrm_ref,
                 sug_ref, sd_ref, base_ref, h2p_ref, lg_ref):
    D = x_ref.shape[1]
    ya = jnp.dot(attn_ref[...], wa_ref[...], preferred_element_type=F32)
    yb = jnp.dot(rw_ref[...], wb_ref[...], preferred_element_type=F32)
    m = gate_ref[:, 0:D] * ya + gate_ref[:, D:] * yb
    x1 = x_ref[...] + mod_ref[2:3, :] * jnp.dot(m.astype(BF16), wo_ref[...], preferred_element_type=F32)
    y = x1 * lax.rsqrt(jnp.mean(x1 * x1, axis=-1, keepdims=True) + NORM_EPS) * g2_ref[...]
    h2 = y * (1.0 + mod_ref[4:5, :]) + mod_ref[3:4, :]
    hb = h2.astype(BF16)
    hm = (h2 - hb.astype(F32)).astype(BF16)
    lg_ref[...] = _nt(wrh_ref[...], hb) + _nt(wrh_ref[...], hm) + _nt(wrm_ref[...], hb)
    packed = _pack_halves(hb.astype(F32))
    half = packed.shape[1] // 2
    h2p_ref[0] = packed[:, :half]
    h2p_ref[1] = packed[:, half:]
    F = sd_ref.shape[0]
    gu = jnp.dot(hb, sug_ref[...], preferred_element_type=F32)
    shared = jnp.dot((_silu(gu[:, :F]) * gu[:, F:]).astype(BF16), sd_ref[...], preferred_element_type=F32)
    base_ref[...] = x1 + mod_ref[5:6, :] * shared


def _post(attn, rw, gate, x2, mod, wa, wb, wo, norm2_g, w_router_t, sug, sd, S, tm=512):
    N, D = x2.shape
    E = w_router_t.shape[0]
    wr_hi = w_router_t.astype(BF16)
    wr_mid = (w_router_t - wr_hi.astype(F32)).astype(BF16)
    tpb = S // tm
    row = lambda w: pl.BlockSpec((tm, w), lambda i: (i, 0))
    full = lambda a: pl.BlockSpec(a.shape, lambda i: (0, 0))
    return pl.pallas_call(
        _post_kernel,
        grid=(N // tm,),
        in_specs=[row(attn.shape[1]), row(rw.shape[1]), row(gate.shape[1]), row(D),
                  pl.BlockSpec((None, 6, D), lambda i: (i // tpb, 0, 0)),
                  full(wa), full(wb), full(wo), pl.BlockSpec((1, D), lambda i: (0, 0)), full(wr_hi), full(wr_mid),
                  full(sug), full(sd)],
        out_specs=[row(D), pl.BlockSpec((2, tm, D // 4), lambda i: (0, i, 0)), pl.BlockSpec((E, tm), lambda i: (0, i))],
        out_shape=[jax.ShapeDtypeStruct((N, D), F32), jax.ShapeDtypeStruct((2, N, D // 4), U32),
                   jax.ShapeDtypeStruct((E, N), F32)],
        compiler_params=_cp("parallel"),
        name="post_mixer",
    )(attn, rw, gate, x2, mod, wa, wb, wo, norm2_g.reshape(1, D), wr_hi, wr_mid, sug, sd)


def _first_argmax(x, idx, sentinel):
    m = jnp.max(x, axis=0, keepdims=True)
    return m, jnp.min(jnp.where(x == m, idx, sentinel), axis=0, keepdims=True)


def _route_kernel(lg_ref, bias_ref, e_ref, w_ref):
    E, T = lg_ref.shape
    gsz = E // N_GROUPS
    scores = _sigmoid(lg_ref[...])
    biased = scores + bias_ref[...]
    ig = lax.broadcasted_iota(I32, (gsz, T), 0)
    gs = []
    for g in range(N_GROUPS):
        blk = biased[g * gsz:(g + 1) * gsz, :]
        m1, i1 = _first_argmax(blk, ig, gsz)
        m2 = jnp.max(jnp.where(ig == i1, -jnp.inf, blk), axis=0, keepdims=True)
        gs.append(m1 + m2)
    gsc = jnp.concatenate(gs, axis=0)
    i8 = lax.broadcasted_iota(I32, (N_GROUPS, T), 0)
    chosen = jnp.zeros((N_GROUPS, T), F32)
    for _ in range(TOPK_GROUPS):
        _, gi = _first_argmax(gsc, i8, N_GROUPS)
        hit = i8 == gi
        chosen = jnp.where(hit, 1.0, chosen)
        gsc = jnp.where(hit, -jnp.inf, gsc)
    masked = jnp.concatenate(
        [jnp.where(chosen[g:g + 1, :] > 0.0, biased[g * gsz:(g + 1) * gsz, :], -jnp.inf)
         for g in range(N_GROUPS)], axis=0)
    ie = lax.broadcasted_iota(I32, (E, T), 0)
    idxs, wts = [], []
    for _ in range(TOP_K):
        _, ei = _first_argmax(masked, ie, E)
        hit = ie == ei
        idxs.append(ei)
        wts.append(jnp.sum(jnp.where(hit, scores, 0.0), axis=0, keepdims=True))
        masked = jnp.where(hit, -jnp.inf, masked)
    wt = jnp.concatenate(wts, axis=0)
    e_ref[...] = jnp.concatenate(idxs, axis=0)
    w_ref[...] = wt / jnp.sum(wt, axis=0, keepdims=True) * ROUTED_SCALE


def _route(logits_t, router_bias, T=512):
    E, N = logits_t.shape
    blk = pl.BlockSpec((TOP_K, T), lambda i: (0, i))
    return pl.pallas_call(
        _route_kernel,
        grid=(N // T,),
        in_specs=[pl.BlockSpec((E, T), lambda i: (0, i)), pl.BlockSpec((E, 1), lambda i: (0, 0))],
        out_specs=[blk, blk],
        out_shape=[jax.ShapeDtypeStruct((TOP_K, N), I32), jax.ShapeDtypeStruct((TOP_K, N), F32)],
        compiler_params=_cp("parallel"),
        name="route",
    )(logits_t, router_bias.reshape(E, 1))


def _rank_kernel(e_ref, rank_ref, cnt_ref, carry_ref, *, n_experts):
    T = e_ref.shape[1]

    @pl.when(pl.program_id(0) == 0)
    def _():
        carry_ref[...] = jnp.zeros_like(carry_ref)

    ie = lax.broadcasted_iota(I32, (n_experts, T), 0)
    e = e_ref[...]
    hits = [ie == e[kk:kk + 1, :] for kk in range(TOP_K)]
    onehot = jnp.zeros((n_experts, T), F32)
    for hsel in hits:
        onehot = onehot + hsel.astype(F32)
    tr = lax.broadcasted_iota(I32, (T, T), 0)
    tc = lax.broadcasted_iota(I32, (T, T), 1)
    before = (tr < tc).astype(BF16)
    base = _bdot(onehot, before) + carry_ref[:, 0:1]
    rank_ref[...] = jnp.concatenate(
        [jnp.sum(jnp.where(hsel, base, 0.0), axis=0, keepdims=True) for hsel in hits], axis=0).astype(I32)
    carry_ref[...] = carry_ref[...] + jnp.sum(onehot, axis=1, keepdims=True)
    cnt_ref[...] = carry_ref[...]


def _ranks(eidx_t, n_experts, T=512):
    N = eidx_t.shape[1]
    blk = pl.BlockSpec((TOP_K, T), lambda i: (0, i))
    cnt = pl.BlockSpec((n_experts, LANES), lambda i: (0, 0))
    return pl.pallas_call(
        functools.partial(_rank_kernel, n_experts=n_experts),
        grid=(N // T,),
        in_specs=[blk],
        out_specs=[blk, cnt],
        out_shape=[jax.ShapeDtypeStruct((TOP_K, N), I32), jax.ShapeDtypeStruct((n_experts, LANES), F32)],
        scratch_shapes=[pltpu.VMEM((n_experts, LANES), F32)],
        compiler_params=_cp("arbitrary"),
        name="ranks",
    )(eidx_t)


def _dest_kernel(e_ref, rank_ref, start_ref, d_ref):
    E = start_ref.shape[0]
    T = e_ref.shape[1]
    ie = lax.broadcasted_iota(I32, (E, T), 0)
    e = e_ref[...]
    start = start_ref[:, 0:1]
    rows = [jnp.sum(jnp.where(ie == e[kk:kk + 1, :], start, 0.0), axis=0, keepdims=True) for kk in range(TOP_K)]
    d_ref[...] = jnp.concatenate(rows, axis=0).astype(I32) + rank_ref[...]


def _dests(eidx_t, rank_t, pstart, T=512):
    N = eidx_t.shape[1]
    E = pstart.shape[0]
    blk = pl.BlockSpec((TOP_K, T), lambda i: (0, i))
    return pl.pallas_call(
        _dest_kernel,
        grid=(N // T,),
        in_specs=[blk, blk, pl.BlockSpec((E, LANES), lambda i: (0, 0))],
        out_specs=blk,
        out_shape=jax.ShapeDtypeStruct((TOP_K, N), I32),
        compiler_params=_cp("parallel"),
        name="dests",
    )(eidx_t, rank_t, jnp.broadcast_to(pstart.astype(F32)[:, None], (E, LANES)))


def _expert_kernel(us_ref, ps_ref, x_hbm, wug_hbm, wd_hbm, y_hbm,
                   wug_f32, wd_f32, wug_bf, wd_bf, xbuf, ybuf, cnt_ref, wsem, xsem, ysem, *, n_rows):
    e = pl.program_id(0)
    n_experts = pl.num_programs(0)
    _, _, R, Ch = xbuf.shape
    C = 2 * Ch
    F = wd_hbm.shape[1]

    def w_copies(expert, slot):
        return (pltpu.make_async_copy(wug_hbm.at[expert], wug_f32.at[slot], wsem.at[slot]),
                pltpu.make_async_copy(wd_hbm.at[expert], wd_f32.at[slot], wsem.at[slot]))

    def x_copy(slot, half, row):
        return pltpu.make_async_copy(x_hbm.at[half, pl.ds(row, R)], xbuf.at[slot, half], xsem.at[slot])

    def y_copy(slot, half, row):
        return pltpu.make_async_copy(ybuf.at[slot, half], y_hbm.at[half, pl.ds(row, R)], ysem.at[slot])

    def start(copy, slot, row):
        for half in range(2):
            copy(slot, half, row).start(priority=1)

    def wait(copy, slot):
        for half in range(2):
            copy(slot, half, 0).wait()

    @pl.when(e == 0)
    def _():
        for ahead in range(W_SLOTS - 1):
            for cp in w_copies(ahead, ahead):
                cp.start()
        cnt_ref[0] = 0
        for t in range(X_AHEAD):
            start(x_copy, t, t * R)
        ybuf[...] = jnp.zeros_like(ybuf)
        for slot in range(2):
            start(y_copy, slot, n_rows + slot * R)

    @pl.when(e + (W_SLOTS - 1) < n_experts)
    def _():
        for cp in w_copies(e + (W_SLOTS - 1), lax.rem(e + (W_SLOTS - 1), W_SLOTS)):
            cp.start()

    wslot = lax.rem(e, W_SLOTS)
    for cp in w_copies(e, wslot):
        cp.wait()
    wug_bf[...] = wug_f32[wslot].astype(BF16)
    wd_bf[...] = wd_f32[wslot].astype(BF16)
    n_valid = us_ref[e + 1] - us_ref[e]
    pbase = ps_ref[e]
    n_tiles = lax.shift_right_logical(n_valid + (R - 1), int(math.log2(R)))

    def make_tile(nb):
        rowid = lax.broadcasted_iota(I32, (nb * R, C), 0)

        def tile(blk0):
            n_done = cnt_ref[0]
            xslots = [lax.rem(n_done + b, X_SLOTS) for b in range(nb)]
            for b in range(nb):
                ahead = n_done + X_AHEAD + b
                start(x_copy, lax.rem(ahead, X_SLOTS), pl.multiple_of(ahead * R, R))
            for b in range(nb):
                wait(x_copy, xslots[b])
            x = jnp.concatenate([jnp.concatenate([xbuf[xs, 0], xbuf[xs, 1]], axis=1) for xs in xslots], axis=0)
            x = jnp.where(rowid < n_valid - blk0 * R, x, jnp.uint32(0))
            lo, hi = _unpack_halves(x)
            gu = (jnp.dot(lo.astype(BF16), wug_bf[0:C, :], preferred_element_type=F32)
                  + jnp.dot(hi.astype(BF16), wug_bf[C:, :], preferred_element_type=F32))
            hid = (_silu(gu[:, :F]) * gu[:, F:]).astype(BF16)
            y = jnp.dot(hid, wd_bf[...], preferred_element_type=F32)
            packed = _pack_halves(y.astype(BF16).astype(F32))
            for b in range(nb):
                slot = (n_done + b) & 1
                wait(y_copy, slot)
                ybuf[slot, 0] = packed[b * R:(b + 1) * R, :Ch]
                ybuf[slot, 1] = packed[b * R:(b + 1) * R, Ch:]
                start(y_copy, slot, pl.multiple_of(pbase + (blk0 + b) * R, R))
            cnt_ref[0] = n_done + nb

        return tile

    pair_tile, single_tile = make_tile(2), make_tile(1)

    def pairs(j, carry):
        pair_tile(2 * j)
        return carry

    lax.fori_loop(0, lax.shift_right_logical(n_tiles, 1), pairs, 0)

    @pl.when((n_tiles & 1) == 1)
    def _():
        single_tile(n_tiles - 1)

    @pl.when(e == pl.num_programs(0) - 1)
    def _():
        for t in range(X_AHEAD):
            wait(x_copy, lax.rem(cnt_ref[0] + t, X_SLOTS))
        for slot in range(2):
            wait(y_copy, slot)
        ybuf[0] = jnp.zeros((2, R, Ch), U32)
        first = lax.shift_right_logical(pbase + n_tiles * R, int(math.log2(R)))
        n_left = n_rows // R - first

        def fill(t, carry):
            start(y_copy, 0, pl.multiple_of((first + t) * R, R))
            return carry

        def drain(t, carry):
            wait(y_copy, 0)
            return carry

        lax.fori_loop(0, n_left, fill, 0)
        lax.fori_loop(0, n_left, drain, 0)


def _experts(ustart, pstart, xg, w_ug, w_d, n_rows, R):
    _, _, Ch = xg.shape
    E, D, F2 = w_ug.shape
    F = w_d.shape[1]
    grid_spec = pltpu.PrefetchScalarGridSpec(
        num_scalar_prefetch=2,
        grid=(E,),
        in_specs=[pl.BlockSpec(memory_space=pl.ANY)] * 3,
        out_specs=pl.BlockSpec(memory_space=pl.ANY),
        scratch_shapes=[pltpu.VMEM((W_SLOTS, D, F2), F32), pltpu.VMEM((W_SLOTS, F, D), F32),
                        pltpu.VMEM((D, F2), BF16), pltpu.VMEM((F, D), BF16),
                        pltpu.VMEM((X_SLOTS, 2, R, Ch), U32), pltpu.VMEM((2, 2, R, Ch), U32), pltpu.SMEM((1,), I32),
                        pltpu.SemaphoreType.DMA((W_SLOTS,)), pltpu.SemaphoreType.DMA((X_SLOTS,)),
                        pltpu.SemaphoreType.DMA((2,))],
    )
    return pl.pallas_call(
        functools.partial(_expert_kernel, n_rows=n_rows),
        grid_spec=grid_spec,
        out_shape=jax.ShapeDtypeStruct((2, n_rows + 2 * R, Ch), U32),
        compiler_params=_cp("arbitrary"),
        name="experts",
    )(ustart, pstart, xg, w_ug, w_d)


def _sc_scatter_rows(src, idx, n_rows):
    H, N, C = src.shape
    K = idx.shape[0]
    per_row = N // SC_WINDOW
    mesh = plsc.VectorSubcoreMesh(core_axis_name="c", subcore_axis_name="s")

    @functools.partial(pl.kernel, out_type=jax.ShapeDtypeStruct((H, n_rows, C), src.dtype), mesh=mesh,
                       scratch_types=[])
    def scatter_kernel(x_hbm, i_hbm, o_hbm):
        for h in range(H):
            def body(x_vmem, i_vmem):
                for k in range(K):
                    pltpu.sync_copy(x_vmem, o_hbm.at[h].at[i_vmem.at[k]])

            pltpu.emit_pipeline(
                body,
                grid=(per_row,),
                in_specs=[pl.BlockSpec((SC_WINDOW, C), lambda i: (i, 0)),
                          pl.BlockSpec((K, SC_WINDOW), lambda i: (0, i))],
                out_specs=[],
                core_axis_name=("c", "s"),
                dimension_semantics=(pltpu.PARALLEL,),
            )(x_hbm.at[h], i_hbm)

    return scatter_kernel(src, idx)


def _sc_gather_rows(src, idx):
    H, _, C = src.shape
    K, N = idx.shape
    per_row = N // SC_WINDOW
    mesh = plsc.VectorSubcoreMesh(core_axis_name="c", subcore_axis_name="s")

    @functools.partial(pl.kernel, out_type=jax.ShapeDtypeStruct((H, K * N, C), src.dtype), mesh=mesh,
                       scratch_types=[])
    def gather_kernel(x_hbm, i_hbm, o_hbm):
        for h in range(H):
            def body(i_vmem, o_vmem):
                pltpu.sync_copy(x_hbm.at[h].at[i_vmem.at[0]], o_vmem)

            pltpu.emit_pipeline(
                body,
                grid=(K * per_row,),
                in_specs=[pl.BlockSpec((1, SC_WINDOW), lambda i: (i // per_row, i % per_row))],
                out_specs=[pl.BlockSpec((SC_WINDOW, C), lambda i: (i, 0))],
                core_axis_name=("c", "s"),
                dimension_semantics=(pltpu.PARALLEL,),
            )(i_hbm, o_hbm.at[h])

    return gather_kernel(src, idx)


def _combine_kernel(*refs):
    y_refs = refs[:2 * TOP_K]
    w_ref, base_ref, mod_ref, o_ref = refs[2 * TOP_K:]
    T = base_ref.shape[0]
    tr = lax.broadcasted_iota(I32, (T, T), 0)
    tc = lax.broadcasted_iota(I32, (T, T), 1)
    wcol = _nt((tr == tc).astype(F32), w_ref[...], precision=HI)
    acc = [None] * 4
    for kk in range(TOP_K):
        wk = wcol[:, kk:kk + 1]
        for half in range(2):
            lo, hi = _unpack_halves(y_refs[2 * kk + half][...])
            for q, val in ((half, lo), (2 + half, hi)):
                acc[q] = val * wk if acc[q] is None else acc[q] + val * wk
    o_ref[...] = base_ref[...] + mod_ref[5:6, :] * jnp.concatenate(acc, axis=1)


def _combine(yg, w_t, base, mod, S, T=512):
    N, D = base.shape
    C = yg.shape[2]
    tpb = S // T
    n_tiles = N // T
    row = pl.BlockSpec((T, D), lambda i: (i, 0))
    piece = lambda kk, half: pl.BlockSpec((None, T, C), lambda i: (half, kk * n_tiles + i, 0))
    return pl.pallas_call(
        _combine_kernel,
        grid=(n_tiles,),
        in_specs=[piece(kk, half) for kk in range(TOP_K) for half in range(2)] + [
            pl.BlockSpec((TOP_K, T), lambda i: (0, i)),
            row,
            pl.BlockSpec((None, 6, D), lambda i: (i // tpb, 0, 0))],
        out_specs=row,
        out_shape=jax.ShapeDtypeStruct((N, D), F32),
        compiler_params=_cp("parallel"),
        name="combine",
    )(*([yg] * (2 * TOP_K)), w_t, base, mod)


def _layer(x, c, positions, layer_idx, w_ada, b_ada, norm1_g, w_in, w_gate, b_gate,
           q_norm_g, k_norm_g, lambda_q1, lambda_k1, lambda_q2, lambda_k2, subln_g,
           rwkv_mu, w_decay0, w_decay2, a0, a2, g2, k_k, k_a, r_k, ln_x_w, ln_x_b,
           w_branch_a, w_branch_b, w_out, norm2_g, w_router, router_bias,
           w_expert_up_gate, w_expert_down, w_shared_up_gate, w_shared_down):
    B, S, D = x.shape
    N = B * S
    E = w_router.shape[1]
    da_width = w_branch_a.shape[0]
    rw_width = w_branch_b.shape[0]
    lambda_init = 0.8 - 0.6 * math.exp(-0.3 * layer_idx)

    mod = _adaln(c, w_ada, b_ada)
    x2 = x.reshape(N, D)
    w_cat = jnp.concatenate([w_in, w_gate], axis=1).astype(BF16)
    qn, kn, v, r_, k_, v_, a_, ld_, g_, gate = _mixer_in(
        x2, positions.reshape(N, 1), mod, norm1_g, w_cat, b_gate, q_norm_g, k_norm_g,
        rwkv_mu, w_decay0, w_decay2, a0, a2, g2, S, da_width, rw_width)

    lam_vecs = jnp.stack([lambda_q1, lambda_k1, lambda_q2, lambda_k2])
    score_bound = 1.01 * DA_HEAD_DIM ** 0.5 * jnp.max(jnp.abs(q_norm_g)) * jnp.max(jnp.abs(k_norm_g))
    attn = _diff_attention(qn, kn, v, score_bound, lam_vecs, subln_g, B, S, lambda_init)

    seq = lambda t: t.reshape(B, S, rw_width)
    rw = _rwkv_scan(seq(r_), seq(k_), seq(v_), seq(a_), seq(ld_), seq(g_), k_k, k_a, r_k.reshape(-1),
                    ln_x_w, ln_x_b).reshape(N, rw_width)

    base, h2p, logits_t = _post(attn, rw, gate, x2, mod, w_branch_a.astype(BF16), w_branch_b.astype(BF16),
                                w_out.astype(BF16), norm2_g, w_router.T,
                                w_shared_up_gate.astype(BF16), w_shared_down.astype(BF16), S)

    eidx_t, w_t = _route(logits_t, router_bias)
    rank_t, counts = _ranks(eidx_t, E)
    R = EXPERT_TILE
    cnt = counts[:, 0].astype(I32)
    ustart = jnp.concatenate([jnp.zeros((1,), I32), jnp.cumsum(cnt)])
    pcnt = (cnt + R - 1) // R * R
    pstart = jnp.cumsum(pcnt) - pcnt
    dest_p = _dests(eidx_t, rank_t, pstart)
    n_rows = (N * TOP_K + E * (R - 1) + R - 1) // R * R
    xg = _sc_scatter_rows(h2p, dest_p, n_rows + X_AHEAD * R)
    y = _experts(ustart, pstart, xg, w_expert_up_gate, w_expert_down, n_rows, R)
    yg = _sc_gather_rows(y, dest_p)
    out = _combine(yg, w_t, base, mod, S)
    return out.reshape(B, S, D)


def kernel(x, c, positions, w_ada, b_ada, norm1_g, w_in, w_gate, b_gate, q_norm_g, k_norm_g, lambda_q1, lambda_k1, lambda_q2, lambda_k2, subln_g, rwkv_mu, w_decay0, w_decay2, a0, a2, g2, k_k, k_a, r_k, ln_x_w, ln_x_b, w_branch_a, w_branch_b, w_out, norm2_g, w_router, router_bias, w_expert_up_gate, w_expert_down, w_shared_up_gate, w_shared_down):
    for l in range(w_ada.shape[0]):
        x = _layer(x, c, positions, l, w_ada[l], b_ada[l], norm1_g[l], w_in[l], w_gate[l], b_gate[l],
                   q_norm_g[l], k_norm_g[l], lambda_q1[l], lambda_k1[l], lambda_q2[l], lambda_k2[l],
                   subln_g[l], rwkv_mu[l], w_decay0[l], w_decay2[l], a0[l], a2[l], g2[l], k_k[l],
                   k_a[l], r_k[l], ln_x_w[l], ln_x_b[l], w_branch_a[l], w_branch_b[l], w_out[l],
                   norm2_g[l], w_router[l], router_bias[l], w_expert_up_gate[l], w_expert_down[l],
                   w_shared_up_gate[l], w_shared_down[l])
    return x
```

```python
import functools
import math

import jax
import jax.numpy as jnp
from jax import lax
from jax.experimental import pallas as pl
from jax.experimental.pallas import tpu as pltpu
from jax.experimental.pallas import tpu_sc as plsc

F32 = jnp.float32
BF16 = jnp.bfloat16
I32 = jnp.int32
U32 = jnp.uint32
HI = lax.Precision.HIGHEST

CHUNK = 64
ROPE_THETA = 10000.0
NORM_EPS = 1e-6
SUBLN_EPS = 1e-5
DA_HEAD_DIM = 64
RWKV_HEAD = 64
GN_EPS = 64e-5
TOP_K = 8
N_GROUPS = 8
TOPK_GROUPS = 4
ROUTED_SCALE = 2.5
EXPERT_TILE = 256
W_SLOTS = 4
SMALL_TILE = 64
X_SLOTS = 6
X_AHEAD = X_SLOTS - 2
RWKV_CHUNK = 64
LANES = 128
SC_WINDOW = 128
NEG = -1e30
MAX_PLAIN_SCORE = 40.0
VMEM_LIMIT = 56 * 1024 * 1024


def _cp(*sem):
    return pltpu.CompilerParams(dimension_semantics=sem, vmem_limit_bytes=VMEM_LIMIT)


def _bdot(a, b):
    return jnp.dot(a.astype(BF16), b.astype(BF16), preferred_element_type=F32)


def _fdot(a, b):
    return jnp.dot(a, b, precision=HI, preferred_element_type=F32)


def _nt(a, b, precision=None):
    return lax.dot_general(a, b, (((1,), (1,)), ((), ())), precision=precision,
                           preferred_element_type=F32)


def _tn(a, b, precision=None):
    return lax.dot_general(a, b, (((0,), (0,)), ((), ())), precision=precision,
                           preferred_element_type=F32)


def _pack_halves(x):
    c = x.shape[1] // 2
    lo = lax.bitcast_convert_type(x[:, :c], U32)
    hi = lax.bitcast_convert_type(x[:, c:], U32)
    return (hi & jnp.uint32(0xFFFF0000)) | (lo >> 16)


def _unpack_halves(w):
    lo = lax.bitcast_convert_type(w << 16, F32)
    hi = lax.bitcast_convert_type(w & jnp.uint32(0xFFFF0000), F32)
    return lo, hi


def _sigmoid(x):
    return 1.0 / (1.0 + jnp.exp(-x))


def _silu(x):
    return x * _sigmoid(x)


def _ada_kernel(c_ref, w_ref, b_ref, o_ref):
    o_ref[...] = _fdot(_silu(c_ref[...]), w_ref[...]) + b_ref[...]


def _adaln(c, w_ada, b_ada):
    B, D = c.shape
    rows = -(-B // 8) * 8
    cpad = jnp.zeros((rows, D), F32).at[:B].set(c)
    n_out = w_ada.shape[1]
    out = pl.pallas_call(
        _ada_kernel,
        grid=(n_out // D,),
        in_specs=[pl.BlockSpec((rows, D), lambda j: (0, 0)),
                  pl.BlockSpec((D, D), lambda j: (0, j)),
                  pl.BlockSpec((1, D), lambda j: (0, j))],
        out_specs=pl.BlockSpec((rows, D), lambda j: (0, j)),
        out_shape=jax.ShapeDtypeStruct((rows, n_out), F32),
        compiler_params=_cp("arbitrary"),
        name="adaln",
    )(cpad, w_ada, b_ada.reshape(1, n_out))
    return out[:B].reshape(B, n_out // D, D)


def _mixer_in_kernel(x_ref, xprev_ref, mod_ref, g_ref, w_ref, bg_ref, pos_ref, invf_ref, qg_ref, kg_ref,
                     mu_ref, w0_ref, w2_ref, a0_ref, a2_ref, g2_ref,
                     qn_ref, kn_ref, v_ref, r_ref, k_ref, vr_ref, a_ref, ld_ref, gr_ref, gate_ref,
                     *, da_width, rw_cols, rw_width, q_scale, tiles_per_seq):
    tm = x_ref.shape[0]

    def modulated(x):
        y = x * lax.rsqrt(jnp.mean(x * x, axis=-1, keepdims=True) + NORM_EPS) * g_ref[...]
        return (y * (1.0 + mod_ref[1:2, :]) + mod_ref[0:1, :]).astype(BF16)

    def proj(hb, c0, width, step=512):
        parts = [jnp.dot(hb, w_ref[:, c0 + o:c0 + min(o + step, width)], preferred_element_type=F32)
                 for o in range(0, width, step)]
        return parts[0] if len(parts) == 1 else jnp.concatenate(parts, axis=1)

    h = modulated(x_ref[...])

    lane = lax.broadcasted_iota(I32, (tm, LANES), 1)
    first = lane < DA_HEAD_DIM
    lo_half = (lane & (DA_HEAD_DIM - 1)) < DA_HEAD_DIM // 2
    ang = pos_ref[...].astype(F32) * invf_ref[...]
    cos = jnp.cos(ang)
    sin = jnp.sin(ang)
    sin = jnp.where(lo_half, -sin, sin)
    for c0, dst, gn_ref, mult in ((0, qn_ref, qg_ref, q_scale), (da_width, kn_ref, kg_ref, 1.0)):
        raw = proj(h, c0, da_width)
        for blk in range(da_width // LANES):
            x = raw[:, blk * LANES:(blk + 1) * LANES]
            xx = x * x
            s_first = jnp.sum(jnp.where(first, xx, 0.0), axis=-1, keepdims=True)
            s_second = jnp.sum(jnp.where(first, 0.0, xx), axis=-1, keepdims=True)
            ms = jnp.where(first, s_first, s_second) * (1.0 / DA_HEAD_DIM)
            xn = x * lax.rsqrt(ms + NORM_EPS) * gn_ref[...]
            rot = jnp.where(lo_half, pltpu.roll(xn, LANES - DA_HEAD_DIM // 2, axis=1),
                            pltpu.roll(xn, DA_HEAD_DIM // 2, axis=1))
            dst[:, blk * LANES:(blk + 1) * LANES] = ((xn * cos + rot * sin) * mult).astype(dst.dtype)
    v_ref[...] = proj(h, 2 * da_width, da_width).astype(v_ref.dtype)

    c_rw = 3 * da_width
    p = proj(h, c_rw, rw_cols)
    p_before = proj(modulated(xprev_ref[...]), c_rw, rw_cols)
    seq_start = (pl.program_id(0) % tiles_per_seq) == 0
    last_prev = jnp.where(seq_start, 0.0, p_before[7:8, :])
    rowi = lax.broadcasted_iota(I32, p.shape, 0)
    prev = jnp.where(rowi == 0, last_prev, pltpu.roll(p, 1, axis=0))
    xs = p + (prev - p) * mu_ref[...]
    width = rw_width
    r_ref[...] = xs[:, 0:width]
    k_ref[...] = xs[:, width:2 * width]
    vr_ref[...] = xs[:, 2 * width:3 * width]
    xwa = xs[:, 3 * width:3 * width + LANES]
    xg = xs[:, 3 * width + LANES:]
    z = w0_ref[...] + _bdot(jnp.tanh(xwa), w2_ref[...])
    w = -(jnp.maximum(-z, 0.0) + jnp.log(1.0 + jnp.exp(-jnp.abs(z)))) - 0.5
    ld_ref[...] = -jnp.exp(w)
    a_ref[...] = _sigmoid(a0_ref[...] + _bdot(xwa, a2_ref[...]))
    gr_ref[...] = _bdot(_sigmoid(xg), g2_ref[...])

    gate_ref[...] = _sigmoid(proj(h, c_rw + rw_cols, gate_ref.shape[1]) + bg_ref[...]).astype(gate_ref.dtype)


def _mixer_in(x2, pos2, mod, norm1_g, w_cat, b_gate, q_norm_g, k_norm_g, mu, w_decay0, w_decay2, a0, a2, g2,
              S, da_width, rw_width, tm=512):
    N, D = x2.shape
    n_gate = b_gate.shape[0]
    rw_cols = mu.shape[0]
    tpb = S // tm
    d = DA_HEAD_DIM
    inv_freq = 1.0 / (ROPE_THETA ** (jnp.arange(0, d, 2, dtype=F32) / d))
    invf = jnp.tile(inv_freq, LANES // (d // 2)).reshape(1, LANES)
    dl, al = w_decay2.shape[0], a2.shape[0]
    assert dl + al == LANES and g2.shape[0] == LANES
    w2p = jnp.zeros((LANES, rw_width), F32).at[:dl].set(w_decay2)
    a2p = jnp.zeros((LANES, rw_width), F32).at[dl:].set(a2)
    kern = functools.partial(_mixer_in_kernel, da_width=da_width, rw_cols=rw_cols, rw_width=rw_width,
                             q_scale=d ** -0.5 * math.log2(math.e),
                             tiles_per_seq=tpb)
    row = lambda w: pl.BlockSpec((tm, w), lambda i: (i, 0))
    vec = lambda n: pl.BlockSpec((1, n), lambda i: (0, 0))
    mat = pl.BlockSpec((LANES, rw_width), lambda i: (0, 0))
    f32 = lambda w: jax.ShapeDtypeStruct((N, w), F32)
    bf16 = lambda w: jax.ShapeDtypeStruct((N, w), BF16)
    return pl.pallas_call(
        kern,
        grid=(N // tm,),
        in_specs=[row(D),
                  pl.BlockSpec((8, D), lambda i: (jnp.maximum(i * (tm // 8) - 1, 0), 0)),
                  pl.BlockSpec((None, 6, D), lambda i: (i // tpb, 0, 0)),
                  vec(D),
                  pl.BlockSpec(w_cat.shape, lambda i: (0, 0)),
                  vec(n_gate),
                  pl.BlockSpec((tm, 1), lambda i: (i, 0)),
                  vec(LANES), vec(LANES), vec(LANES),
                  vec(rw_cols), vec(rw_width), mat, vec(rw_width), mat, mat],
        out_specs=[row(da_width)] * 3 + [row(rw_width)] * 6 + [row(n_gate)],
        out_shape=[bf16(da_width)] * 3 + [f32(rw_width)] * 6 + [bf16(n_gate)],
        compiler_params=_cp("parallel"),
        name="mixer_in",
    )(x2, x2, mod, norm1_g.reshape(1, D), w_cat, b_gate.reshape(1, n_gate), pos2, invf,
      jnp.tile(q_norm_g, 2).reshape(1, LANES), jnp.tile(k_norm_g, 2).reshape(1, LANES),
      mu.reshape(1, rw_cols), w_decay0.reshape(1, rw_width), w2p, a0.reshape(1, rw_width), a2p, g2)


def _attn_kernel(flag_ref, q_ref, k_ref, v_ref, vt_ref, lam_ref, sg_ref, sgc_ref, o_ref,
                 qz_ref, m_ref, l_ref, acc_ref, lpt_ref, acct_ref, *, tq, lambda_init):
    i = pl.program_id(2)
    lane = lax.broadcasted_iota(I32, (tq, LANES), 1)
    q = q_ref[...]
    zero = jnp.zeros_like(q)
    qz_ref[0:tq, :] = jnp.where(lane < DA_HEAD_DIM, q, zero)
    qz_ref[tq:, :] = jnp.where(lane >= DA_HEAD_DIM, q, zero)
    bounded = flag_ref[0] == 1
    lv = lam_ref[...]
    lam = (jnp.exp(jnp.sum(lv[0:1] * lv[1:2], keepdims=True))
           - jnp.exp(jnp.sum(lv[2:3] * lv[3:4], keepdims=True)) + lambda_init)

    def run(step):
        def body(j, carry):
            step(j, False)
            return carry
        lax.fori_loop(0, i, body, 0)
        step(i, True)

    def plain_step(j, masked):
        off = pl.multiple_of(j * tq, tq)
        st = _nt(k_ref[pl.ds(off, tq), :], qz_ref[...])
        if masked:
            row = lax.broadcasted_iota(I32, st.shape, 0)
            col = lax.broadcasted_iota(I32, st.shape, 1)
            st = jnp.where((row // CHUNK) <= ((col & (tq - 1)) // CHUNK), st, NEG)
        pt = jnp.exp2(st)
        part = pt[0:8, :]
        for g in range(1, tq // 8):
            part = part + pt[8 * g:8 * g + 8, :]
        lpt_ref[...] += part
        acct_ref[...] += jnp.dot(vt_ref[:, pl.ds(off, tq)], pt.astype(BF16), preferred_element_type=F32)

    @pl.when(bounded)
    def _():
        lpt_ref[...] = jnp.zeros_like(lpt_ref)
        acct_ref[...] = jnp.zeros_like(acct_ref)

        def two_steps(p, carry):
            plain_step(2 * p, False)
            plain_step(2 * p + 1, False)
            return carry

        lax.fori_loop(0, lax.shift_right_logical(i, 1), two_steps, 0)

        @pl.when((i & 1) == 1)
        def _():
            plain_step(i - 1, False)
            plain_step(i, True)

        @pl.when((i & 1) == 0)
        def _():
            plain_step(i, True)

        lsum = jnp.sum(lpt_ref[...], axis=0, keepdims=True)
        ot = acct_ref[:, 0:tq] / lsum[:, 0:tq] - lam * (acct_ref[:, tq:] / lsum[:, tq:])
        ot = ot * lax.rsqrt(jnp.mean(ot * ot, axis=0, keepdims=True) + SUBLN_EPS) * sgc_ref[...]
        o_ref[...] = (ot * (1.0 - lambda_init)).T.astype(o_ref.dtype)

    def online_step(j, masked):
        off = pl.multiple_of(j * tq, tq)
        s = _nt(qz_ref[...], k_ref[pl.ds(off, tq), :])
        if masked:
            row = lax.broadcasted_iota(I32, s.shape, 0)
            col = lax.broadcasted_iota(I32, s.shape, 1)
            s = jnp.where((col // CHUNK) <= ((row & (tq - 1)) // CHUNK), s, NEG)
        m_old = m_ref[...]
        m_new = jnp.maximum(m_old, jnp.max(s, axis=-1, keepdims=True))
        alpha = jnp.exp2(m_old - m_new)
        pr = jnp.exp2(s - m_new)
        l_ref[...] = alpha * l_ref[...] + jnp.sum(pr, axis=-1, keepdims=True)
        acc_ref[...] = alpha * acc_ref[...] + jnp.dot(pr.astype(BF16), v_ref[pl.ds(off, tq), :],
                                                      preferred_element_type=F32)
        m_ref[...] = m_new

    @pl.when(jnp.logical_not(bounded))
    def _():
        m_ref[...] = jnp.full_like(m_ref, NEG)
        l_ref[...] = jnp.zeros_like(l_ref)
        acc_ref[...] = jnp.zeros_like(acc_ref)
        run(online_step)
        o = acc_ref[0:tq, :] / l_ref[0:tq, :] - lam * (acc_ref[tq:, :] / l_ref[tq:, :])
        o = o * lax.rsqrt(jnp.mean(o * o, axis=-1, keepdims=True) + SUBLN_EPS) * sg_ref[...]
        o_ref[...] = (o * (1.0 - lambda_init)).astype(o_ref.dtype)


def _diff_attention(qn, kn, v, score_bound, lam_vecs, subln_g, B, S, lambda_init, tq=512):
    W = qn.shape[1]
    H = W // LANES
    q3 = qn.reshape(B, S, W)
    k3 = kn.reshape(B, S, W)
    v3 = v.reshape(B, S, W)
    vt3 = v3.transpose(0, 2, 1)
    flag = (score_bound <= MAX_PLAIN_SCORE).astype(I32).reshape(1)
    qblk = pl.BlockSpec((None, tq, LANES), lambda b, h, i, f: (b, i, h))
    kvblk = pl.BlockSpec((None, S, LANES), lambda b, h, i, f: (b, 0, h))
    grid_spec = pltpu.PrefetchScalarGridSpec(
        num_scalar_prefetch=1,
        grid=(B, H, S // tq),
        in_specs=[qblk, kvblk, kvblk,
                  pl.BlockSpec((None, LANES, S), lambda b, h, i, f: (b, h, 0)),
                  pl.BlockSpec((4, DA_HEAD_DIM), lambda b, h, i, f: (0, 0)),
                  pl.BlockSpec((1, LANES), lambda b, h, i, f: (0, 0)),
                  pl.BlockSpec((LANES, 1), lambda b, h, i, f: (0, 0))],
        out_specs=qblk,
        scratch_shapes=[pltpu.VMEM((2 * tq, LANES), BF16),
                        pltpu.VMEM((2 * tq, 1), F32),
                        pltpu.VMEM((2 * tq, 1), F32),
                        pltpu.VMEM((2 * tq, LANES), F32),
                        pltpu.VMEM((8, 2 * tq), F32),
                        pltpu.VMEM((LANES, 2 * tq), F32)],
    )
    out = pl.pallas_call(
        functools.partial(_attn_kernel, tq=tq, lambda_init=lambda_init),
        grid_spec=grid_spec,
        out_shape=jax.ShapeDtypeStruct((B, S, W), BF16),
        compiler_params=_cp("parallel", "parallel", "arbitrary"),
        name="diff_attn",
    )(flag, q3, k3, v3, vt3, lam_vecs, subln_g.reshape(1, LANES), subln_g.reshape(LANES, 1))
    return out.reshape(B * S, W)


def _stackmask(m):
    lane = lax.broadcasted_iota(I32, m.shape, 1)
    z = jnp.zeros_like(m)
    return jnp.concatenate([jnp.where(lane < RWKV_HEAD, m, z), jnp.where(lane >= RWKV_HEAD, m, z)], axis=0)


def _pair_sum(x, first):
    s1 = jnp.sum(jnp.where(first, x, 0.0), axis=-1, keepdims=True)
    s2 = jnp.sum(jnp.where(first, 0.0, x), axis=-1, keepdims=True)
    return jnp.where(first, s1, s2)


def _rwkv_scan_kernel(r_ref, k_ref, v_ref, a_ref, ld_ref, g_ref, kk_ref, ka_ref, rk_ref, lnw_ref, lnb_ref,
                      o_ref, s_ref, *, L):
    tm, W = r_ref.shape
    n_chunks = tm // L
    n_pairs = W // LANES
    hd = RWKV_HEAD
    bf = lambda t: t.astype(BF16)

    @pl.when(pl.program_id(1) == 0)
    def _():
        s_ref[...] = jnp.zeros_like(s_ref)

    row = lax.broadcasted_iota(I32, (tm, tm), 0)
    col = lax.broadcasted_iota(I32, (tm, tm), 1)
    tri = jnp.where(jnp.logical_and(col <= row, (col // L) == (row // L)), 1.0, 0.0).astype(BF16)
    ld = ld_ref[...]
    ld_hi = bf(ld)
    rem = ld - ld_hi.astype(F32)
    ld_mid = bf(rem)
    ld_lo = bf(rem - ld_mid.astype(F32))
    c = (jnp.dot(tri, ld_hi, preferred_element_type=F32) + jnp.dot(tri, ld_mid, preferred_element_type=F32)
         + jnp.dot(tri, ld_lo, preferred_element_type=F32))
    ec = jnp.exp(c)
    eci = jnp.exp(-c)
    ecm = jnp.exp(c - ld)
    r = r_ref[...]
    k = k_ref[...]
    v = v_ref[...]
    a = a_ref[...]
    kkr = k * kk_ref[...]
    kmod = k * (1.0 + (a - 1.0) * ka_ref[...])
    brk = r * kmod * rk_ref[...]

    lane = lax.broadcasted_iota(I32, (L, LANES), 1)
    rowl = lax.broadcasted_iota(I32, (L, LANES), 0)
    first = lane < hd
    lane_h = lane & (hd - 1)
    strict = lane_h < rowl
    incl = lane_h <= rowl
    eye = jnp.where(lane_h == rowl, 1.0, 0.0)

    chains = [(ch, p) for ch in range(n_chunks) for p in range(n_pairs)]
    rsl = lambda ch: slice(ch * L, (ch + 1) * L)
    csl = lambda p: slice(p * LANES, (p + 1) * LANES)
    fdot = lambda x, y: jnp.dot(x, y, preferred_element_type=F32)
    at, bt, kt, rt, vh, g_l = {}, {}, {}, {}, {}, {}
    for c_ in chains:
        ch, p = c_
        rs, cs = rsl(ch), csl(p)
        kkh = kkr[rs, cs]
        kkh = kkh / jnp.maximum(jnp.sqrt(_pair_sum(kkh * kkh, first)), 1e-12)
        vh[c_] = v[rs, cs]
        g_l[c_] = ec[ch * L + L - 1:ch * L + L, cs]
        at[c_] = -kkh * ecm[rs, cs]
        bt[c_] = kkh * a[rs, cs] * eci[rs, cs]
        kt[c_] = kmod[rs, cs] * eci[rs, cs]
        rt[c_] = r[rs, cs] * ec[rs, cs]
    gm = {c_: _nt(bf(jnp.concatenate([at[c_], rt[c_]], axis=0)),
                  jnp.concatenate([_stackmask(bf(bt[c_])), _stackmask(bf(kt[c_]))], axis=0)) for c_ in chains}
    a_ab = {c_: jnp.where(strict, gm[c_][:L, :LANES], 0.0) for c_ in chains}
    vsm = {c_: _stackmask(bf(vh[c_])) for c_ in chains}
    cmat = {c_: fdot(bf(jnp.where(strict, gm[c_][:L, LANES:], 0.0)), vsm[c_]) for c_ in chains}
    t_inv = {c_: eye + a_ab[c_] for c_ in chains}
    pw = {c_: bf(a_ab[c_]) for c_ in chains}
    for _ in range(int(math.log2(L)) - 1):
        pw = {c_: bf(fdot(pw[c_], _stackmask(pw[c_]))) for c_ in chains}
        t_inv = {c_: t_inv[c_] + fdot(pw[c_], _stackmask(bf(t_inv[c_]))) for c_ in chains}
    zz = {c_: fdot(bf(t_inv[c_]), jnp.concatenate([_stackmask(bf(at[c_])), _stackmask(bf(cmat[c_]))], axis=1))
          for c_ in chains}
    qy = {c_: fdot(bf(jnp.where(incl, gm[c_][L:, :LANES], 0.0)),
                   jnp.concatenate([_stackmask(bf(zz[c_][:, :LANES])), _stackmask(bf(zz[c_][:, LANES:]))], axis=1))
          for c_ in chains}
    y0 = {c_: qy[c_][:, LANES:] + fdot(bf(jnp.where(incl, gm[c_][L:, LANES:], 0.0)), vsm[c_]) for c_ in chains}
    qa = {c_: bf(jnp.concatenate([rt[c_] + qy[c_][:, :LANES], zz[c_][:, :LANES]], axis=0)) for c_ in chains}
    bkg = {c_: bf(jnp.concatenate([bt[c_] * g_l[c_], kt[c_] * g_l[c_]], axis=0)) for c_ in chains}

    lane_s = lax.broadcasted_iota(I32, (hd, LANES), 1)
    sp = [s_ref[p] for p in range(n_pairs)]
    for ch in range(n_chunks):
        rs = rsl(ch)
        yw = [_nt(qa[ch, p], _stackmask(bf(sp[p]))) for p in range(n_pairs)]
        upd = [_tn(bf(jnp.concatenate([yw[p][L:] + zz[ch, p][:, LANES:], vh[ch, p]], axis=0)), bkg[ch, p])
               for p in range(n_pairs)]
        for p in range(n_pairs):
            cs = csl(p)
            sp[p] = sp[p] * g_l[ch, p] + jnp.where(lane_s < hd, upd[p][:hd], upd[p][hd:])
            y = yw[p][:L] + y0[ch, p]
            mean = _pair_sum(y, first) * (1.0 / hd)
            yc = y - mean
            var = _pair_sum(yc * yc, first) * (1.0 / hd)
            yn = yc * lax.rsqrt(var + GN_EPS) * lnw_ref[:, cs] + lnb_ref[:, cs]
            bonus = _pair_sum(brk[rs, cs], first) * vh[ch, p]
            o_ref[rs, cs] = ((yn + bonus) * g_ref[rs, cs]).astype(o_ref.dtype)
    for p in range(n_pairs):
        s_ref[p] = sp[p]


def _rwkv_scan(r, k, v, a, ld, g, k_k, k_a, r_k, ln_w, ln_b, L=RWKV_CHUNK, tm=512):
    B, S, W = r.shape
    seq = pl.BlockSpec((None, tm, W), lambda b, c: (b, c, 0))
    vec = pl.BlockSpec((1, W), lambda b, c: (0, 0))
    return pl.pallas_call(
        functools.partial(_rwkv_scan_kernel, L=L),
        grid=(B, S // tm),
        in_specs=[seq] * 6 + [vec] * 5,
        out_specs=seq,
        out_shape=jax.ShapeDtypeStruct((B, S, W), BF16),
        scratch_shapes=[pltpu.VMEM((W // LANES, RWKV_HEAD, LANES), F32)],
        compiler_params=_cp("parallel", "arbitrary"),
        name="rwkv_scan",
    )(r, k, v, a, ld, g, k_k.reshape(1, W), k_a.reshape(1, W), r_k.reshape(1, W),
      ln_w.reshape(1, W), ln_b.reshape(1, W))


def _post_kernel(attn_ref, rw_ref, gate_ref, x_ref, mod_ref, wa_ref, wb_ref, wo_ref, g2_ref, wrh_ref, wrm_ref,
                 sug_ref, sd_ref, base_ref, h2p_ref, lg_ref):
    D = x_ref.shape[1]
    ya = jnp.dot(attn_ref[...], wa_ref[...], preferred_element_type=F32)
    yb = jnp.dot(rw_ref[...], wb_ref[...], preferred_element_type=F32)
    m = gate_ref[:, 0:D] * ya + gate_ref[:, D:] * yb
    x1 = x_ref[...] + mod_ref[2:3, :] * jnp.dot(m.astype(BF16), wo_ref[...], preferred_element_type=F32)
    y = x1 * lax.rsqrt(jnp.mean(x1 * x1, axis=-1, keepdims=True) + NORM_EPS) * g2_ref[...]
    h2 = y * (1.0 + mod_ref[4:5, :]) + mod_ref[3:4, :]
    hb = h2.astype(BF16)
    hm = (h2 - hb.astype(F32)).astype(BF16)
    lg_ref[...] = _nt(wrh_ref[...], hb) + _nt(wrh_ref[...], hm) + _nt(wrm_ref[...], hb)
    packed = _pack_halves(hb.astype(F32))
    half = packed.shape[1] // 2
    h2p_ref[0] = packed[:, :half]
    h2p_ref[1] = packed[:, half:]
    F = sd_ref.shape[0]
    gu = jnp.dot(hb, sug_ref[...], preferred_element_type=F32)
    shared = jnp.dot((_silu(gu[:, :F]) * gu[:, F:]).astype(BF16), sd_ref[...], preferred_element_type=F32)
    base_ref[...] = x1 + mod_ref[5:6, :] * shared


def _post(attn, rw, gate, x2, mod, wa, wb, wo, norm2_g, w_router_t, sug, sd, S, tm=512):
    N, D = x2.shape
    E = w_router_t.shape[0]
    wr_hi = w_router_t.astype(BF16)
    wr_mid = (w_router_t - wr_hi.astype(F32)).astype(BF16)
    tpb = S // tm
    row = lambda w: pl.BlockSpec((tm, w), lambda i: (i, 0))
    full = lambda a: pl.BlockSpec(a.shape, lambda i: (0, 0))
    return pl.pallas_call(
        _post_kernel,
        grid=(N // tm,),
        in_specs=[row(attn.shape[1]), row(rw.shape[1]), row(gate.shape[1]), row(D),
                  pl.BlockSpec((None, 6, D), lambda i: (i // tpb, 0, 0)),
                  full(wa), full(wb), full(wo), pl.BlockSpec((1, D), lambda i: (0, 0)), full(wr_hi), full(wr_mid),
                  full(sug), full(sd)],
        out_specs=[row(D), pl.BlockSpec((2, tm, D // 4), lambda i: (0, i, 0)), pl.BlockSpec((E, tm), lambda i: (0, i))],
        out_shape=[jax.ShapeDtypeStruct((N, D), F32), jax.ShapeDtypeStruct((2, N, D // 4), U32),
                   jax.ShapeDtypeStruct((E, N), F32)],
        compiler_params=_cp("parallel"),
        name="post_mixer",
    )(attn, rw, gate, x2, mod, wa, wb, wo, norm2_g.reshape(1, D), wr_hi, wr_mid, sug, sd)


def _first_argmax(x, idx, sentinel):
    m = jnp.max(x, axis=0, keepdims=True)
    return m, jnp.min(jnp.where(x == m, idx, sentinel), axis=0, keepdims=True)


def _route_kernel(lg_ref, bias_ref, e_ref, w_ref):
    E, T = lg_ref.shape
    gsz = E // N_GROUPS
    scores = _sigmoid(lg_ref[...])
    biased = scores + bias_ref[...]
    ig = lax.broadcasted_iota(I32, (gsz, T), 0)
    gs = []
    for g in range(N_GROUPS):
        blk = biased[g * gsz:(g + 1) * gsz, :]
        m1, i1 = _first_argmax(blk, ig, gsz)
        m2 = jnp.max(jnp.where(ig == i1, -jnp.inf, blk), axis=0, keepdims=True)
        gs.append(m1 + m2)
    gsc = jnp.concatenate(gs, axis=0)
    i8 = lax.broadcasted_iota(I32, (N_GROUPS, T), 0)
    chosen = jnp.zeros((N_GROUPS, T), F32)
    for _ in range(TOPK_GROUPS):
        _, gi = _first_argmax(gsc, i8, N_GROUPS)
        hit = i8 == gi
        chosen = jnp.where(hit, 1.0, chosen)
        gsc = jnp.where(hit, -jnp.inf, gsc)
    masked = jnp.concatenate(
        [jnp.where(chosen[g:g + 1, :] > 0.0, biased[g * gsz:(g + 1) * gsz, :], -jnp.inf)
         for g in range(N_GROUPS)], axis=0)
    ie = lax.broadcasted_iota(I32, (E, T), 0)
    idxs, wts = [], []
    for _ in range(TOP_K):
        _, ei = _first_argmax(masked, ie, E)
        hit = ie == ei
        idxs.append(ei)
        wts.append(jnp.sum(jnp.where(hit, scores, 0.0), axis=0, keepdims=True))
        masked = jnp.where(hit, -jnp.inf, masked)
    wt = jnp.concatenate(wts, axis=0)
    e_ref[...] = jnp.concatenate(idxs, axis=0)
    w_ref[...] = wt / jnp.sum(wt, axis=0, keepdims=True) * ROUTED_SCALE


def _route(logits_t, router_bias, T=512):
    E, N = logits_t.shape
    blk = pl.BlockSpec((TOP_K, T), lambda i: (0, i))
    return pl.pallas_call(
        _route_kernel,
        grid=(N // T,),
        in_specs=[pl.BlockSpec((E, T), lambda i: (0, i)), pl.BlockSpec((E, 1), lambda i: (0, 0))],
        out_specs=[blk, blk],
        out_shape=[jax.ShapeDtypeStruct((TOP_K, N), I32), jax.ShapeDtypeStruct((TOP_K, N), F32)],
        compiler_params=_cp("parallel"),
        name="route",
    )(logits_t, router_bias.reshape(E, 1))


def _rank_kernel(e_ref, rank_ref, cnt_ref, carry_ref, *, n_experts):
    T = e_ref.shape[1]

    @pl.when(pl.program_id(0) == 0)
    def _():
        carry_ref[...] = jnp.zeros_like(carry_ref)

    ie = lax.broadcasted_iota(I32, (n_experts, T), 0)
    e = e_ref[...]
    hits = [ie == e[kk:kk + 1, :] for kk in range(TOP_K)]
    onehot = jnp.zeros((n_experts, T), F32)
    for hsel in hits:
        onehot = onehot + hsel.astype(F32)
    tr = lax.broadcasted_iota(I32, (T, T), 0)
    tc = lax.broadcasted_iota(I32, (T, T), 1)
    before = (tr < tc).astype(BF16)
    base = _bdot(onehot, before) + carry_ref[:, 0:1]
    rank_ref[...] = jnp.concatenate(
        [jnp.sum(jnp.where(hsel, base, 0.0), axis=0, keepdims=True) for hsel in hits], axis=0).astype(I32)
    carry_ref[...] = carry_ref[...] + jnp.sum(onehot, axis=1, keepdims=True)
    cnt_ref[...] = carry_ref[...]


def _ranks(eidx_t, n_experts, T=512):
    N = eidx_t.shape[1]
    blk = pl.BlockSpec((TOP_K, T), lambda i: (0, i))
    cnt = pl.BlockSpec((n_experts, LANES), lambda i: (0, 0))
    return pl.pallas_call(
        functools.partial(_rank_kernel, n_experts=n_experts),
        grid=(N // T,),
        in_specs=[blk],
        out_specs=[blk, cnt],
        out_shape=[jax.ShapeDtypeStruct((TOP_K, N), I32), jax.ShapeDtypeStruct((n_experts, LANES), F32)],
        scratch_shapes=[pltpu.VMEM((n_experts, LANES), F32)],
        compiler_params=_cp("arbitrary"),
        name="ranks",
    )(eidx_t)


def _dest_kernel(e_ref, rank_ref, start_ref, d_ref):
    E = start_ref.shape[0]
    T = e_ref.shape[1]
    ie = lax.broadcasted_iota(I32, (E, T), 0)
    e = e_ref[...]
    start = start_ref[:, 0:1]
    rows = [jnp.sum(jnp.where(ie == e[kk:kk + 1, :], start, 0.0), axis=0, keepdims=True) for kk in range(TOP_K)]
    d_ref[...] = jnp.concatenate(rows, axis=0).astype(I32) + rank_ref[...]


def _dests(eidx_t, rank_t, pstart, T=512):
    N = eidx_t.shape[1]
    E = pstart.shape[0]
    blk = pl.BlockSpec((TOP_K, T), lambda i: (0, i))
    return pl.pallas_call(
        _dest_kernel,
        grid=(N // T,),
        in_specs=[blk, blk, pl.BlockSpec((E, LANES), lambda i: (0, 0))],
        out_specs=blk,
        out_shape=jax.ShapeDtypeStruct((TOP_K, N), I32),
        compiler_params=_cp("parallel"),
        name="dests",
    )(eidx_t, rank_t, jnp.broadcast_to(pstart.astype(F32)[:, None], (E, LANES)))


def _expert_kernel(us_ref, ps_ref, x_hbm, wug_hbm, wd_hbm, y_hbm,
                   wug_f32, wd_f32, wug_bf, wd_bf, xbuf, ybuf, cnt_ref, wsem, xsem, ysem, *, n_rows):
    e = pl.program_id(0)
    n_experts = pl.num_programs(0)
    _, _, R, Ch = xbuf.shape
    C = 2 * Ch
    F = wd_hbm.shape[1]

    def w_copies(expert, slot):
        return (pltpu.make_async_copy(wug_hbm.at[expert], wug_f32.at[slot], wsem.at[slot]),
                pltpu.make_async_copy(wd_hbm.at[expert], wd_f32.at[slot], wsem.at[slot]))

    def x_copy(slot, half, row):
        return pltpu.make_async_copy(x_hbm.at[half, pl.ds(row, R)], xbuf.at[slot, half], xsem.at[slot])

    def y_copy(slot, half, row):
        return pltpu.make_async_copy(ybuf.at[slot, half], y_hbm.at[half, pl.ds(row, R)], ysem.at[slot])

    def start(copy, slot, row):
        for half in range(2):
            copy(slot, half, row).start(priority=1)

    def wait(copy, slot):
        for half in range(2):
            copy(slot, half, 0).wait()

    @pl.when(e == 0)
    def _():
        for ahead in range(W_SLOTS - 1):
            for cp in w_copies(ahead, ahead):
                cp.start()
        cnt_ref[0] = 0
        for t in range(X_AHEAD):
            start(x_copy, t, t * R)
        ybuf[...] = jnp.zeros_like(ybuf)
        for slot in range(2):
            start(y_copy, slot, n_rows + slot * R)

    @pl.when(e + (W_SLOTS - 1) < n_experts)
    def _():
        for cp in w_copies(e + (W_SLOTS - 1), lax.rem(e + (W_SLOTS - 1), W_SLOTS)):
            cp.start()

    wslot = lax.rem(e, W_SLOTS)
    for cp in w_copies(e, wslot):
        cp.wait()
    wug_bf[...] = wug_f32[wslot].astype(BF16)
    wd_bf[...] = wd_f32[wslot].astype(BF16)
    n_valid = us_ref[e + 1] - us_ref[e]
    pbase = ps_ref[e]
    n_tiles = lax.shift_right_logical(n_valid + (R - 1), int(math.log2(R)))

    def make_tile(nb, rows=None):
        rows = nb * R if rows is None else rows
        rowid = lax.broadcasted_iota(I32, (rows, C), 0)

        def tile(blk0):
            n_done = cnt_ref[0]
            xslots = [lax.rem(n_done + b, X_SLOTS) for b in range(nb)]
            for b in range(nb):
                ahead = n_done + X_AHEAD + b
                start(x_copy, lax.rem(ahead, X_SLOTS), pl.multiple_of(ahead * R, R))
            for b in range(nb):
                wait(x_copy, xslots[b])
            x = jnp.concatenate([jnp.concatenate([xbuf[xs, 0, 0:min(rows, R)], xbuf[xs, 1, 0:min(rows, R)]], axis=1)
                                 for xs in xslots], axis=0)
            x = jnp.where(rowid < n_valid - blk0 * R, x, jnp.uint32(0))
            lo, hi = _unpack_halves(x)
            gu = (jnp.dot(lo.astype(BF16), wug_bf[0:C, :], preferred_element_type=F32)
                  + jnp.dot(hi.astype(BF16), wug_bf[C:, :], preferred_element_type=F32))
            hid = (_silu(gu[:, :F]) * gu[:, F:]).astype(BF16)
            y = jnp.dot(hid, wd_bf[...], preferred_element_type=F32)
            packed = _pack_halves(y.astype(BF16).astype(F32))
            for b in range(nb):
                slot = (n_done + b) & 1
                wait(y_copy, slot)
                ybuf[slot, 0, 0:min(rows, R)] = packed[b * R:b * R + min(rows, R), :Ch]
                ybuf[slot, 1, 0:min(rows, R)] = packed[b * R:b * R + min(rows, R), Ch:]
                start(y_copy, slot, pl.multiple_of(pbase + (blk0 + b) * R, R))
            cnt_ref[0] = n_done + nb

        return tile

    pair_tile, single_tile, small_tile = make_tile(2), make_tile(1), make_tile(1, SMALL_TILE)

    def pairs(j, carry):
        pair_tile(2 * j)
        return carry

    lax.fori_loop(0, lax.shift_right_logical(n_tiles, 1), pairs, 0)

    odd = (n_tiles & 1) == 1
    last_rows = n_valid - (n_tiles - 1) * R

    @pl.when(jnp.logical_and(odd, last_rows > SMALL_TILE))
    def _():
        single_tile(n_tiles - 1)

    @pl.when(jnp.logical_and(odd, last_rows <= SMALL_TILE))
    def _():
        small_tile(n_tiles - 1)

    @pl.when(e == pl.num_programs(0) - 1)
    def _():
        for t in range(X_AHEAD):
            wait(x_copy, lax.rem(cnt_ref[0] + t, X_SLOTS))
        for slot in range(2):
            wait(y_copy, slot)
        ybuf[0] = jnp.zeros((2, R, Ch), U32)
        first = lax.shift_right_logical(pbase + n_tiles * R, int(math.log2(R)))
        n_left = n_rows // R - first

        def fill(t, carry):
            start(y_copy, 0, pl.multiple_of((first + t) * R, R))
            return carry

        def drain(t, carry):
            wait(y_copy, 0)
            return carry

        lax.fori_loop(0, n_left, fill, 0)
        lax.fori_loop(0, n_left, drain, 0)


def _experts(ustart, pstart, xg, w_ug, w_d, n_rows, R):
    _, _, Ch = xg.shape
    E, D, F2 = w_ug.shape
    F = w_d.shape[1]
    grid_spec = pltpu.PrefetchScalarGridSpec(
        num_scalar_prefetch=2,
        grid=(E,),
        in_specs=[pl.BlockSpec(memory_space=pl.ANY)] * 3,
        out_specs=pl.BlockSpec(memory_space=pl.ANY),
        scratch_shapes=[pltpu.VMEM((W_SLOTS, D, F2), F32), pltpu.VMEM((W_SLOTS, F, D), F32),
                        pltpu.VMEM((D, F2), BF16), pltpu.VMEM((F, D), BF16),
                        pltpu.VMEM((X_SLOTS, 2, R, Ch), U32), pltpu.VMEM((2, 2, R, Ch), U32), pltpu.SMEM((1,), I32),
                        pltpu.SemaphoreType.DMA((W_SLOTS,)), pltpu.SemaphoreType.DMA((X_SLOTS,)),
                        pltpu.SemaphoreType.DMA((2,))],
    )
    return pl.pallas_call(
        functools.partial(_expert_kernel, n_rows=n_rows),
        grid_spec=grid_spec,
        out_shape=jax.ShapeDtypeStruct((2, n_rows + 2 * R, Ch), U32),
        compiler_params=_cp("arbitrary"),
        name="experts",
    )(ustart, pstart, xg, w_ug, w_d)


def _sc_scatter_rows(src, idx, n_rows):
    H, N, C = src.shape
    K = idx.shape[0]
    per_row = N // SC_WINDOW
    mesh = plsc.VectorSubcoreMesh(core_axis_name="c", subcore_axis_name="s")

    @functools.partial(pl.kernel, out_type=jax.ShapeDtypeStruct((H, n_rows, C), src.dtype), mesh=mesh,
                       scratch_types=[])
    def scatter_kernel(x_hbm, i_hbm, o_hbm):
        for h in range(H):
            def body(x_vmem, i_vmem):
                for k in range(K):
                    pltpu.sync_copy(x_vmem, o_hbm.at[h].at[i_vmem.at[k]])

            pltpu.emit_pipeline(
                body,
                grid=(per_row,),
                in_specs=[pl.BlockSpec((SC_WINDOW, C), lambda i: (i, 0)),
                          pl.BlockSpec((K, SC_WINDOW), lambda i: (0, i))],
                out_specs=[],
                core_axis_name=("c", "s"),
                dimension_semantics=(pltpu.PARALLEL,),
            )(x_hbm.at[h], i_hbm)

    return scatter_kernel(src, idx)


def _sc_gather_rows(src, idx):
    H, _, C = src.shape
    K, N = idx.shape
    per_row = N // SC_WINDOW
    mesh = plsc.VectorSubcoreMesh(core_axis_name="c", subcore_axis_name="s")

    @functools.partial(pl.kernel, out_type=jax.ShapeDtypeStruct((H, K * N, C), src.dtype), mesh=mesh,
                       scratch_types=[])
    def gather_kernel(x_hbm, i_hbm, o_hbm):
        for h in range(H):
            def body(i_vmem, o_vmem):
                pltpu.sync_copy(x_hbm.at[h].at[i_vmem.at[0]], o_vmem)

            pltpu.emit_pipeline(
                body,
                grid=(K * per_row,),
                in_specs=[pl.BlockSpec((1, SC_WINDOW), lambda i: (i // per_row, i % per_row))],
                out_specs=[pl.BlockSpec((SC_WINDOW, C), lambda i: (i, 0))],
                core_axis_name=("c", "s"),
                dimension_semantics=(pltpu.PARALLEL,),
            )(i_hbm, o_hbm.at[h])

    return gather_kernel(src, idx)


def _combine_kernel(*refs):
    y_refs = refs[:2 * TOP_K]
    w_ref, base_ref, mod_ref, o_ref = refs[2 * TOP_K:]
    T = base_ref.shape[0]
    tr = lax.broadcasted_iota(I32, (T, T), 0)
    tc = lax.broadcasted_iota(I32, (T, T), 1)
    wcol = _nt((tr == tc).astype(F32), w_ref[...], precision=HI)
    acc = [None] * 4
    for kk in range(TOP_K):
        wk = wcol[:, kk:kk + 1]
        for half in range(2):
            lo, hi = _unpack_halves(y_refs[2 * kk + half][...])
            for q, val in ((half, lo), (2 + half, hi)):
                acc[q] = val * wk if acc[q] is None else acc[q] + val * wk
    o_ref[...] = base_ref[...] + mod_ref[5:6, :] * jnp.concatenate(acc, axis=1)


def _combine(yg, w_t, base, mod, S, T=512):
    N, D = base.shape
    C = yg.shape[2]
    tpb = S // T
    n_tiles = N // T
    row = pl.BlockSpec((T, D), lambda i: (i, 0))
    piece = lambda kk, half: pl.BlockSpec((None, T, C), lambda i: (half, kk * n_tiles + i, 0))
    return pl.pallas_call(
        _combine_kernel,
        grid=(n_tiles,),
        in_specs=[piece(kk, half) for kk in range(TOP_K) for half in range(2)] + [
            pl.BlockSpec((TOP_K, T), lambda i: (0, i)),
            row,
            pl.BlockSpec((None, 6, D), lambda i: (i // tpb, 0, 0))],
        out_specs=row,
        out_shape=jax.ShapeDtypeStruct((N, D), F32),
        compiler_params=_cp("parallel"),
        name="combine",
    )(*([yg] * (2 * TOP_K)), w_t, base, mod)


def _layer(x, c, positions, layer_idx, w_ada, b_ada, norm1_g, w_in, w_gate, b_gate,
           q_norm_g, k_norm_g, lambda_q1, lambda_k1, lambda_q2, lambda_k2, subln_g,
           rwkv_mu, w_decay0, w_decay2, a0, a2, g2, k_k, k_a, r_k, ln_x_w, ln_x_b,
           w_branch_a, w_branch_b, w_out, norm2_g, w_router, router_bias,
           w_expert_up_gate, w_expert_down, w_shared_up_gate, w_shared_down):
    B, S, D = x.shape
    N = B * S
    E = w_router.shape[1]
    da_width = w_branch_a.shape[0]
    rw_width = w_branch_b.shape[0]
    lambda_init = 0.8 - 0.6 * math.exp(-0.3 * layer_idx)

    mod = _adaln(c, w_ada, b_ada)
    x2 = x.reshape(N, D)
    w_cat = jnp.concatenate([w_in, w_gate], axis=1).astype(BF16)
    qn, kn, v, r_, k_, v_, a_, ld_, g_, gate = _mixer_in(
        x2, positions.reshape(N, 1), mod, norm1_g, w_cat, b_gate, q_norm_g, k_norm_g,
        rwkv_mu, w_decay0, w_decay2, a0, a2, g2, S, da_width, rw_width)

    lam_vecs = jnp.stack([lambda_q1, lambda_k1, lambda_q2, lambda_k2])
    score_bound = 1.01 * DA_HEAD_DIM ** 0.5 * jnp.max(jnp.abs(q_norm_g)) * jnp.max(jnp.abs(k_norm_g))
    attn = _diff_attention(qn, kn, v, score_bound, lam_vecs, subln_g, B, S, lambda_init)

    seq = lambda t: t.reshape(B, S, rw_width)
    rw = _rwkv_scan(seq(r_), seq(k_), seq(v_), seq(a_), seq(ld_), seq(g_), k_k, k_a, r_k.reshape(-1),
                    ln_x_w, ln_x_b).reshape(N, rw_width)

    base, h2p, logits_t = _post(attn, rw, gate, x2, mod, w_branch_a.astype(BF16), w_branch_b.astype(BF16),
                                w_out.astype(BF16), norm2_g, w_router.T,
                                w_shared_up_gate.astype(BF16), w_shared_down.astype(BF16), S)

    eidx_t, w_t = _route(logits_t, router_bias)
    rank_t, counts = _ranks(eidx_t, E)
    R = EXPERT_TILE
    cnt = counts[:, 0].astype(I32)
    ustart = jnp.concatenate([jnp.zeros((1,), I32), jnp.cumsum(cnt)])
    pcnt = (cnt + R - 1) // R * R
    pstart = jnp.cumsum(pcnt) - pcnt
    dest_p = _dests(eidx_t, rank_t, pstart)
    n_rows = (N * TOP_K + E * (R - 1) + R - 1) // R * R
    xg = _sc_scatter_rows(h2p, dest_p, n_rows + X_AHEAD * R)
    y = _experts(ustart, pstart, xg, w_expert_up_gate, w_expert_down, n_rows, R)
    yg = _sc_gather_rows(y, dest_p)
    out = _combine(yg, w_t, base, mod, S)
    return out.reshape(B, S, D)


def kernel(x, c, positions, w_ada, b_ada, norm1_g, w_in, w_gate, b_gate, q_norm_g, k_norm_g, lambda_q1, lambda_k1, lambda_q2, lambda_k2, subln_g, rwkv_mu, w_decay0, w_decay2, a0, a2, g2, k_k, k_a, r_k, ln_x_w, ln_x_b, w_branch_a, w_branch_b, w_out, norm2_g, w_router, router_bias, w_expert_up_gate, w_expert_down, w_shared_up_gate, w_shared_down):
    for l in range(w_ada.shape[0]):
        x = _layer(x, c, positions, l, w_ada[l], b_ada[l], norm1_g[l], w_in[l], w_gate[l], b_gate[l],
                   q_norm_g[l], k_norm_g[l], lambda_q1[l], lambda_k1[l], lambda_q2[l], lambda_k2[l],
                   subln_g[l], rwkv_mu[l], w_decay0[l], w_decay2[l], a0[l], a2[l], g2[l], k_k[l],
                   k_a[l], r_k[l], ln_x_w[l], ln_x_b[l], w_branch_a[l], w_branch_b[l], w_out[l],
                   norm2_g[l], w_router[l], router_bias[l], w_expert_up_gate[l], w_expert_down[l],
                   w_shared_up_gate[l], w_shared_down[l])
    return x
```

```python
import functools
import math

import jax
import jax.numpy as jnp
from jax import lax
from jax.experimental import pallas as pl
from jax.experimental.pallas import tpu as pltpu
from jax.experimental.pallas import tpu_sc as plsc

F32 = jnp.float32
BF16 = jnp.bfloat16
I32 = jnp.int32
U32 = jnp.uint32
HI = lax.Precision.HIGHEST

CHUNK = 64
ROPE_THETA = 10000.0
NORM_EPS = 1e-6
SUBLN_EPS = 1e-5
DA_HEAD_DIM = 64
RWKV_HEAD = 64
GN_EPS = 64e-5
TOP_K = 8
N_GROUPS = 8
TOPK_GROUPS = 4
ROUTED_SCALE = 2.5
EXPERT_TILE = 256
W_SLOTS = 3
X_SLOTS = 6
X_AHEAD = X_SLOTS - 2
RWKV_CHUNK = 64
LANES = 128
SC_WINDOW = 128
NEG = -1e30
MAX_PLAIN_SCORE = 40.0
VMEM_LIMIT = 56 * 1024 * 1024


def _cp(*sem):
    return pltpu.CompilerParams(dimension_semantics=sem, vmem_limit_bytes=VMEM_LIMIT)


def _bdot(a, b):
    return jnp.dot(a.astype(BF16), b.astype(BF16), preferred_element_type=F32)


def _fdot(a, b):
    return jnp.dot(a, b, precision=HI, preferred_element_type=F32)


def _nt(a, b, precision=None):
    return lax.dot_general(a, b, (((1,), (1,)), ((), ())), precision=precision,
                           preferred_element_type=F32)


def _tn(a, b, precision=None):
    return lax.dot_general(a, b, (((0,), (0,)), ((), ())), precision=precision,
                           preferred_element_type=F32)


def _pack_halves(x):
    c = x.shape[1] // 2
    lo = lax.bitcast_convert_type(x[:, :c], U32)
    hi = lax.bitcast_convert_type(x[:, c:], U32)
    return (hi & jnp.uint32(0xFFFF0000)) | (lo >> 16)


def _unpack_halves(w):
    lo = lax.bitcast_convert_type(w << 16, F32)
    hi = lax.bitcast_convert_type(w & jnp.uint32(0xFFFF0000), F32)
    return lo, hi


def _sigmoid(x):
    return 1.0 / (1.0 + jnp.exp(-x))


def _silu(x):
    return x * _sigmoid(x)


def _ada_kernel(c_ref, w_ref, b_ref, o_ref):
    o_ref[...] = _fdot(_silu(c_ref[...]), w_ref[...]) + b_ref[...]


def _adaln(c, w_ada, b_ada):
    B, D = c.shape
    rows = -(-B // 8) * 8
    cpad = jnp.zeros((rows, D), F32).at[:B].set(c)
    n_out = w_ada.shape[1]
    out = pl.pallas_call(
        _ada_kernel,
        grid=(n_out // D,),
        in_specs=[pl.BlockSpec((rows, D), lambda j: (0, 0)),
                  pl.BlockSpec((D, D), lambda j: (0, j)),
                  pl.BlockSpec((1, D), lambda j: (0, j))],
        out_specs=pl.BlockSpec((rows, D), lambda j: (0, j)),
        out_shape=jax.ShapeDtypeStruct((rows, n_out), F32),
        compiler_params=_cp("arbitrary"),
        name="adaln",
    )(cpad, w_ada, b_ada.reshape(1, n_out))
    return out[:B].reshape(B, n_out // D, D)


def _mixer_in_kernel(x_ref, xprev_ref, mod_ref, g_ref, w_ref, bg_ref, pos_ref, invf_ref, qg_ref, kg_ref,
                     mu_ref, w0_ref, w2_ref, a0_ref, a2_ref, g2_ref,
                     qn_ref, kn_ref, v_ref, r_ref, k_ref, vr_ref, a_ref, ld_ref, gr_ref, gate_ref,
                     *, da_width, rw_cols, rw_width, q_scale, tiles_per_seq):
    tm = x_ref.shape[0]

    def modulated(x):
        y = x * lax.rsqrt(jnp.mean(x * x, axis=-1, keepdims=True) + NORM_EPS) * g_ref[...]
        return (y * (1.0 + mod_ref[1:2, :]) + mod_ref[0:1, :]).astype(BF16)

    def proj(hb, c0, width, step=512):
        parts = [jnp.dot(hb, w_ref[:, c0 + o:c0 + min(o + step, width)], preferred_element_type=F32)
                 for o in range(0, width, step)]
        return parts[0] if len(parts) == 1 else jnp.concatenate(parts, axis=1)

    h = modulated(x_ref[...])

    lane = lax.broadcasted_iota(I32, (tm, LANES), 1)
    first = lane < DA_HEAD_DIM
    lo_half = (lane & (DA_HEAD_DIM - 1)) < DA_HEAD_DIM // 2
    ang = pos_ref[...].astype(F32) * invf_ref[...]
    cos = jnp.cos(ang)
    sin = jnp.sin(ang)
    sin = jnp.where(lo_half, -sin, sin)
    for c0, dst, gn_ref, mult in ((0, qn_ref, qg_ref, q_scale), (da_width, kn_ref, kg_ref, 1.0)):
        raw = proj(h, c0, da_width)
        for blk in range(da_width // LANES):
            x = raw[:, blk * LANES:(blk + 1) * LANES]
            xx = x * x
            s_first = jnp.sum(jnp.where(first, xx, 0.0), axis=-1, keepdims=True)
            s_second = jnp.sum(jnp.where(first, 0.0, xx), axis=-1, keepdims=True)
            ms = jnp.where(first, s_first, s_second) * (1.0 / DA_HEAD_DIM)
            xn = x * lax.rsqrt(ms + NORM_EPS) * gn_ref[...]
            rot = jnp.where(lo_half, pltpu.roll(xn, LANES - DA_HEAD_DIM // 2, axis=1),
                            pltpu.roll(xn, DA_HEAD_DIM // 2, axis=1))
            dst[:, blk * LANES:(blk + 1) * LANES] = ((xn * cos + rot * sin) * mult).astype(dst.dtype)
    v_ref[...] = proj(h, 2 * da_width, da_width).astype(v_ref.dtype)

    c_rw = 3 * da_width
    p = proj(h, c_rw, rw_cols)
    p_before = proj(modulated(xprev_ref[...]), c_rw, rw_cols)
    seq_start = (pl.program_id(0) % tiles_per_seq) == 0
    last_prev = jnp.where(seq_start, 0.0, p_before[7:8, :])
    rowi = lax.broadcasted_iota(I32, p.shape, 0)
    prev = jnp.where(rowi == 0, last_prev, pltpu.roll(p, 1, axis=0))
    xs = p + (prev - p) * mu_ref[...]
    width = rw_width
    r_ref[...] = xs[:, 0:width]
    k_ref[...] = xs[:, width:2 * width]
    vr_ref[...] = xs[:, 2 * width:3 * width]
    xwa = xs[:, 3 * width:3 * width + LANES]
    xg = xs[:, 3 * width + LANES:]
    z = w0_ref[...] + _bdot(jnp.tanh(xwa), w2_ref[...])
    w = -(jnp.maximum(-z, 0.0) + jnp.log(1.0 + jnp.exp(-jnp.abs(z)))) - 0.5
    ld_ref[...] = -jnp.exp(w)
    a_ref[...] = _sigmoid(a0_ref[...] + _bdot(xwa, a2_ref[...]))
    gr_ref[...] = _bdot(_sigmoid(xg), g2_ref[...])

    gate_ref[...] = _sigmoid(proj(h, c_rw + rw_cols, gate_ref.shape[1]) + bg_ref[...]).astype(gate_ref.dtype)


def _mixer_in(x2, pos2, mod, norm1_g, w_cat, b_gate, q_norm_g, k_norm_g, mu, w_decay0, w_decay2, a0, a2, g2,
              S, da_width, rw_width, tm=512):
    N, D = x2.shape
    n_gate = b_gate.shape[0]
    rw_cols = mu.shape[0]
    tpb = S // tm
    d = DA_HEAD_DIM
    inv_freq = 1.0 / (ROPE_THETA ** (jnp.arange(0, d, 2, dtype=F32) / d))
    invf = jnp.tile(inv_freq, LANES // (d // 2)).reshape(1, LANES)
    dl, al = w_decay2.shape[0], a2.shape[0]
    assert dl + al == LANES and g2.shape[0] == LANES
    w2p = jnp.zeros((LANES, rw_width), F32).at[:dl].set(w_decay2)
    a2p = jnp.zeros((LANES, rw_width), F32).at[dl:].set(a2)
    kern = functools.partial(_mixer_in_kernel, da_width=da_width, rw_cols=rw_cols, rw_width=rw_width,
                             q_scale=d ** -0.5 * math.log2(math.e),
                             tiles_per_seq=tpb)
    row = lambda w: pl.BlockSpec((tm, w), lambda i: (i, 0))
    vec = lambda n: pl.BlockSpec((1, n), lambda i: (0, 0))
    mat = pl.BlockSpec((LANES, rw_width), lambda i: (0, 0))
    f32 = lambda w: jax.ShapeDtypeStruct((N, w), F32)
    bf16 = lambda w: jax.ShapeDtypeStruct((N, w), BF16)
    return pl.pallas_call(
        kern,
        grid=(N // tm,),
        in_specs=[row(D),
                  pl.BlockSpec((8, D), lambda i: (jnp.maximum(i * (tm // 8) - 1, 0), 0)),
                  pl.BlockSpec((None, 6, D), lambda i: (i // tpb, 0, 0)),
                  vec(D),
                  pl.BlockSpec(w_cat.shape, lambda i: (0, 0)),
                  vec(n_gate),
                  pl.BlockSpec((tm, 1), lambda i: (i, 0)),
                  vec(LANES), vec(LANES), vec(LANES),
                  vec(rw_cols), vec(rw_width), mat, vec(rw_width), mat, mat],
        out_specs=[row(da_width)] * 3 + [row(rw_width)] * 6 + [row(n_gate)],
        out_shape=[bf16(da_width)] * 3 + [f32(rw_width)] * 6 + [bf16(n_gate)],
        compiler_params=_cp("parallel"),
        name="mixer_in",
    )(x2, x2, mod, norm1_g.reshape(1, D), w_cat, b_gate.reshape(1, n_gate), pos2, invf,
      jnp.tile(q_norm_g, 2).reshape(1, LANES), jnp.tile(k_norm_g, 2).reshape(1, LANES),
      mu.reshape(1, rw_cols), w_decay0.reshape(1, rw_width), w2p, a0.reshape(1, rw_width), a2p, g2)


def _attn_kernel(flag_ref, q_ref, k_ref, v_ref, vt_ref, lam_ref, sg_ref, sgc_ref, o_ref,
                 qz_ref, m_ref, l_ref, acc_ref, lpt_ref, acct_ref, *, tq, lambda_init):
    i = pl.program_id(2)
    lane = lax.broadcasted_iota(I32, (tq, LANES), 1)
    q = q_ref[...]
    zero = jnp.zeros_like(q)
    qz_ref[0:tq, :] = jnp.where(lane < DA_HEAD_DIM, q, zero)
    qz_ref[tq:, :] = jnp.where(lane >= DA_HEAD_DIM, q, zero)
    bounded = flag_ref[0] == 1
    lv = lam_ref[...]
    lam = (jnp.exp(jnp.sum(lv[0:1] * lv[1:2], keepdims=True))
           - jnp.exp(jnp.sum(lv[2:3] * lv[3:4], keepdims=True)) + lambda_init)

    def run(step):
        def body(j, carry):
            step(j, False)
            return carry
        lax.fori_loop(0, i, body, 0)
        step(i, True)

    def plain_step(j, masked):
        off = pl.multiple_of(j * tq, tq)
        st = _nt(k_ref[pl.ds(off, tq), :], qz_ref[...])
        if masked:
            row = lax.broadcasted_iota(I32, st.shape, 0)
            col = lax.broadcasted_iota(I32, st.shape, 1)
            st = jnp.where((row // CHUNK) <= ((col & (tq - 1)) // CHUNK), st, NEG)
        pt = jnp.exp2(st)
        part = pt[0:8, :]
        for g in range(1, tq // 8):
            part = part + pt[8 * g:8 * g + 8, :]
        lpt_ref[...] += part
        acct_ref[...] += jnp.dot(vt_ref[:, pl.ds(off, tq)], pt.astype(BF16), preferred_element_type=F32)

    @pl.when(bounded)
    def _():
        lpt_ref[...] = jnp.zeros_like(lpt_ref)
        acct_ref[...] = jnp.zeros_like(acct_ref)

        def two_steps(p, carry):
            plain_step(2 * p, False)
            plain_step(2 * p + 1, False)
            return carry

        lax.fori_loop(0, lax.shift_right_logical(i, 1), two_steps, 0)

        @pl.when((i & 1) == 1)
        def _():
            plain_step(i - 1, False)
            plain_step(i, True)

        @pl.when((i & 1) == 0)
        def _():
            plain_step(i, True)

        lsum = jnp.sum(lpt_ref[...], axis=0, keepdims=True)
        ot = acct_ref[:, 0:tq] / lsum[:, 0:tq] - lam * (acct_ref[:, tq:] / lsum[:, tq:])
        ot = ot * lax.rsqrt(jnp.mean(ot * ot, axis=0, keepdims=True) + SUBLN_EPS) * sgc_ref[...]
        o_ref[...] = (ot * (1.0 - lambda_init)).T.astype(o_ref.dtype)

    def online_step(j, masked):
        off = pl.multiple_of(j * tq, tq)
        s = _nt(qz_ref[...], k_ref[pl.ds(off, tq), :])
        if masked:
            row = lax.broadcasted_iota(I32, s.shape, 0)
            col = lax.broadcasted_iota(I32, s.shape, 1)
            s = jnp.where((col // CHUNK) <= ((row & (tq - 1)) // CHUNK), s, NEG)
        m_old = m_ref[...]
        m_new = jnp.maximum(m_old, jnp.max(s, axis=-1, keepdims=True))
        alpha = jnp.exp2(m_old - m_new)
        pr = jnp.exp2(s - m_new)
        l_ref[...] = alpha * l_ref[...] + jnp.sum(pr, axis=-1, keepdims=True)
        acc_ref[...] = alpha * acc_ref[...] + jnp.dot(pr.astype(BF16), v_ref[pl.ds(off, tq), :],
                                                      preferred_element_type=F32)
        m_ref[...] = m_new

    @pl.when(jnp.logical_not(bounded))
    def _():
        m_ref[...] = jnp.full_like(m_ref, NEG)
        l_ref[...] = jnp.zeros_like(l_ref)
        acc_ref[...] = jnp.zeros_like(acc_ref)
        run(online_step)
        o = acc_ref[0:tq, :] / l_ref[0:tq, :] - lam * (acc_ref[tq:, :] / l_ref[tq:, :])
        o = o * lax.rsqrt(jnp.mean(o * o, axis=-1, keepdims=True) + SUBLN_EPS) * sg_ref[...]
        o_ref[...] = (o * (1.0 - lambda_init)).astype(o_ref.dtype)


def _diff_attention(qn, kn, v, score_bound, lam_vecs, subln_g, B, S, lambda_init, tq=512):
    W = qn.shape[1]
    H = W // LANES
    q3 = qn.reshape(B, S, W)
    k3 = kn.reshape(B, S, W)
    v3 = v.reshape(B, S, W)
    vt3 = v3.transpose(0, 2, 1)
    flag = (score_bound <= MAX_PLAIN_SCORE).astype(I32).reshape(1)
    qblk = pl.BlockSpec((None, tq, LANES), lambda b, h, i, f: (b, i, h))
    kvblk = pl.BlockSpec((None, S, LANES), lambda b, h, i, f: (b, 0, h))
    grid_spec = pltpu.PrefetchScalarGridSpec(
        num_scalar_prefetch=1,
        grid=(B, H, S // tq),
        in_specs=[qblk, kvblk, kvblk,
                  pl.BlockSpec((None, LANES, S), lambda b, h, i, f: (b, h, 0)),
                  pl.BlockSpec((4, DA_HEAD_DIM), lambda b, h, i, f: (0, 0)),
                  pl.BlockSpec((1, LANES), lambda b, h, i, f: (0, 0)),
                  pl.BlockSpec((LANES, 1), lambda b, h, i, f: (0, 0))],
        out_specs=qblk,
        scratch_shapes=[pltpu.VMEM((2 * tq, LANES), BF16),
                        pltpu.VMEM((2 * tq, 1), F32),
                        pltpu.VMEM((2 * tq, 1), F32),
                        pltpu.VMEM((2 * tq, LANES), F32),
                        pltpu.VMEM((8, 2 * tq), F32),
                        pltpu.VMEM((LANES, 2 * tq), F32)],
    )
    out = pl.pallas_call(
        functools.partial(_attn_kernel, tq=tq, lambda_init=lambda_init),
        grid_spec=grid_spec,
        out_shape=jax.ShapeDtypeStruct((B, S, W), BF16),
        compiler_params=_cp("parallel", "parallel", "arbitrary"),
        name="diff_attn",
    )(flag, q3, k3, v3, vt3, lam_vecs, subln_g.reshape(1, LANES), subln_g.reshape(LANES, 1))
    return out.reshape(B * S, W)


def _stackmask(m):
    lane = lax.broadcasted_iota(I32, m.shape, 1)
    z = jnp.zeros_like(m)
    return jnp.concatenate([jnp.where(lane < RWKV_HEAD, m, z), jnp.where(lane >= RWKV_HEAD, m, z)], axis=0)


def _pair_sum(x, first):
    s1 = jnp.sum(jnp.where(first, x, 0.0), axis=-1, keepdims=True)
    s2 = jnp.sum(jnp.where(first, 0.0, x), axis=-1, keepdims=True)
    return jnp.where(first, s1, s2)


def _rwkv_scan_kernel(r_ref, k_ref, v_ref, a_ref, ld_ref, g_ref, kk_ref, ka_ref, rk_ref, lnw_ref, lnb_ref,
                      o_ref, s_ref, *, L):
    tm, W = r_ref.shape
    n_chunks = tm // L
    n_pairs = W // LANES
    hd = RWKV_HEAD
    bf = lambda t: t.astype(BF16)

    @pl.when(pl.program_id(1) == 0)
    def _():
        s_ref[...] = jnp.zeros_like(s_ref)

    row = lax.broadcasted_iota(I32, (tm, tm), 0)
    col = lax.broadcasted_iota(I32, (tm, tm), 1)
    tri = jnp.where(jnp.logical_and(col <= row, (col // L) == (row // L)), 1.0, 0.0).astype(BF16)
    ld = ld_ref[...]
    ld_hi = bf(ld)
    rem = ld - ld_hi.astype(F32)
    ld_mid = bf(rem)
    ld_lo = bf(rem - ld_mid.astype(F32))
    c = (jnp.dot(tri, ld_hi, preferred_element_type=F32) + jnp.dot(tri, ld_mid, preferred_element_type=F32)
         + jnp.dot(tri, ld_lo, preferred_element_type=F32))
    ec = jnp.exp(c)
    eci = jnp.exp(-c)
    ecm = jnp.exp(c - ld)
    r = r_ref[...]
    k = k_ref[...]
    v = v_ref[...]
    a = a_ref[...]
    kkr = k * kk_ref[...]
    kmod = k * (1.0 + (a - 1.0) * ka_ref[...])
    brk = r * kmod * rk_ref[...]

    lane = lax.broadcasted_iota(I32, (L, LANES), 1)
    rowl = lax.broadcasted_iota(I32, (L, LANES), 0)
    first = lane < hd
    lane_h = lane & (hd - 1)
    strict = lane_h < rowl
    incl = lane_h <= rowl
    eye = jnp.where(lane_h == rowl, 1.0, 0.0)

    chains = [(ch, p) for ch in range(n_chunks) for p in range(n_pairs)]
    rsl = lambda ch: slice(ch * L, (ch + 1) * L)
    csl = lambda p: slice(p * LANES, (p + 1) * LANES)
    fdot = lambda x, y: jnp.dot(x, y, preferred_element_type=F32)
    at, bt, kt, rt, vh, g_l = {}, {}, {}, {}, {}, {}
    for c_ in chains:
        ch, p = c_
        rs, cs = rsl(ch), csl(p)
        kkh = kkr[rs, cs]
        kkh = kkh / jnp.maximum(jnp.sqrt(_pair_sum(kkh * kkh, first)), 1e-12)
        vh[c_] = v[rs, cs]
        g_l[c_] = ec[ch * L + L - 1:ch * L + L, cs]
        at[c_] = -kkh * ecm[rs, cs]
        bt[c_] = kkh * a[rs, cs] * eci[rs, cs]
        kt[c_] = kmod[rs, cs] * eci[rs, cs]
        rt[c_] = r[rs, cs] * ec[rs, cs]
    gm = {c_: _nt(bf(jnp.concatenate([at[c_], rt[c_]], axis=0)),
                  jnp.concatenate([_stackmask(bf(bt[c_])), _stackmask(bf(kt[c_]))], axis=0)) for c_ in chains}
    a_ab = {c_: jnp.where(strict, gm[c_][:L, :LANES], 0.0) for c_ in chains}
    vsm = {c_: _stackmask(bf(vh[c_])) for c_ in chains}
    cmat = {c_: fdot(bf(jnp.where(strict, gm[c_][:L, LANES:], 0.0)), vsm[c_]) for c_ in chains}
    t_inv = {c_: eye + a_ab[c_] for c_ in chains}
    pw = {c_: bf(a_ab[c_]) for c_ in chains}
    for _ in range(int(math.log2(L)) - 1):
        pw = {c_: bf(fdot(pw[c_], _stackmask(pw[c_]))) for c_ in chains}
        t_inv = {c_: t_inv[c_] + fdot(pw[c_], _stackmask(bf(t_inv[c_]))) for c_ in chains}
    zz = {c_: fdot(bf(t_inv[c_]), jnp.concatenate([_stackmask(bf(at[c_])), _stackmask(bf(cmat[c_]))], axis=1))
          for c_ in chains}
    qy = {c_: fdot(bf(jnp.where(incl, gm[c_][L:, :LANES], 0.0)),
                   jnp.concatenate([_stackmask(bf(zz[c_][:, :LANES])), _stackmask(bf(zz[c_][:, LANES:]))], axis=1))
          for c_ in chains}
    y0 = {c_: qy[c_][:, LANES:] + fdot(bf(jnp.where(incl, gm[c_][L:, LANES:], 0.0)), vsm[c_]) for c_ in chains}
    qa = {c_: bf(jnp.concatenate([rt[c_] + qy[c_][:, :LANES], zz[c_][:, :LANES]], axis=0)) for c_ in chains}
    bkg = {c_: bf(jnp.concatenate([bt[c_] * g_l[c_], kt[c_] * g_l[c_]], axis=0)) for c_ in chains}

    lane_s = lax.broadcasted_iota(I32, (hd, LANES), 1)
    sp = [s_ref[p] for p in range(n_pairs)]
    for ch in range(n_chunks):
        rs = rsl(ch)
        yw = [_nt(qa[ch, p], _stackmask(bf(sp[p]))) for p in range(n_pairs)]
        upd = [_tn(bf(jnp.concatenate([yw[p][L:] + zz[ch, p][:, LANES:], vh[ch, p]], axis=0)), bkg[ch, p])
               for p in range(n_pairs)]
        for p in range(n_pairs):
            cs = csl(p)
            sp[p] = sp[p] * g_l[ch, p] + jnp.where(lane_s < hd, upd[p][:hd], upd[p][hd:])
            y = yw[p][:L] + y0[ch, p]
            mean = _pair_sum(y, first) * (1.0 / hd)
            yc = y - mean
            var = _pair_sum(yc * yc, first) * (1.0 / hd)
            yn = yc * lax.rsqrt(var + GN_EPS) * lnw_ref[:, cs] + lnb_ref[:, cs]
            bonus = _pair_sum(brk[rs, cs], first) * vh[ch, p]
            o_ref[rs, cs] = ((yn + bonus) * g_ref[rs, cs]).astype(o_ref.dtype)
    for p in range(n_pairs):
        s_ref[p] = sp[p]


def _rwkv_scan(r, k, v, a, ld, g, k_k, k_a, r_k, ln_w, ln_b, L=RWKV_CHUNK, tm=512):
    B, S, W = r.shape
    seq = pl.BlockSpec((None, tm, W), lambda b, c: (b, c, 0))
    vec = pl.BlockSpec((1, W), lambda b, c: (0, 0))
    return pl.pallas_call(
        functools.partial(_rwkv_scan_kernel, L=L),
        grid=(B, S // tm),
        in_specs=[seq] * 6 + [vec] * 5,
        out_specs=seq,
        out_shape=jax.ShapeDtypeStruct((B, S, W), BF16),
        scratch_shapes=[pltpu.VMEM((W // LANES, RWKV_HEAD, LANES), F32)],
        compiler_params=_cp("parallel", "arbitrary"),
        name="rwkv_scan",
    )(r, k, v, a, ld, g, k_k.reshape(1, W), k_a.reshape(1, W), r_k.reshape(1, W),
      ln_w.reshape(1, W), ln_b.reshape(1, W))


def _post_kernel(attn_ref, rw_ref, gate_ref, x_ref, mod_ref, wa_ref, wb_ref, wo_ref, g2_ref, wrh_ref, wrm_ref,
                 sug_ref, sd_ref, base_ref, h2p_ref, lg_ref):
    D = x_ref.shape[1]
    ya = jnp.dot(attn_ref[...], wa_ref[...], preferred_element_type=F32)
    yb = jnp.dot(rw_ref[...], wb_ref[...], preferred_element_type=F32)
    m = gate_ref[:, 0:D] * ya + gate_ref[:, D:] * yb
    x1 = x_ref[...] + mod_ref[2:3, :] * jnp.dot(m.astype(BF16), wo_ref[...], preferred_element_type=F32)
    y = x1 * lax.rsqrt(jnp.mean(x1 * x1, axis=-1, keepdims=True) + NORM_EPS) * g2_ref[...]
    h2 = y * (1.0 + mod_ref[4:5, :]) + mod_ref[3:4, :]
    hb = h2.astype(BF16)
    hm = (h2 - hb.astype(F32)).astype(BF16)
    lg_ref[...] = _nt(wrh_ref[...], hb) + _nt(wrh_ref[...], hm) + _nt(wrm_ref[...], hb)
    packed = _pack_halves(hb.astype(F32))
    half = packed.shape[1] // 2
    h2p_ref[0] = packed[:, :half]
    h2p_ref[1] = packed[:, half:]
    F = sd_ref.shape[0]
    gu = jnp.dot(hb, sug_ref[...], preferred_element_type=F32)
    shared = jnp.dot((_silu(gu[:, :F]) * gu[:, F:]).astype(BF16), sd_ref[...], preferred_element_type=F32)
    base_ref[...] = x1 + mod_ref[5:6, :] * shared


def _post(attn, rw, gate, x2, mod, wa, wb, wo, norm2_g, w_router_t, sug, sd, S, tm=512):
    N, D = x2.shape
    E = w_router_t.shape[0]
    wr_hi = w_router_t.astype(BF16)
    wr_mid = (w_router_t - wr_hi.astype(F32)).astype(BF16)
    tpb = S // tm
    row = lambda w: pl.BlockSpec((tm, w), lambda i: (i, 0))
    full = lambda a: pl.BlockSpec(a.shape, lambda i: (0, 0))
    return pl.pallas_call(
        _post_kernel,
        grid=(N // tm,),
        in_specs=[row(attn.shape[1]), row(rw.shape[1]), row(gate.shape[1]), row(D),
                  pl.BlockSpec((None, 6, D), lambda i: (i // tpb, 0, 0)),
                  full(wa), full(wb), full(wo), pl.BlockSpec((1, D), lambda i: (0, 0)), full(wr_hi), full(wr_mid),
                  full(sug), full(sd)],
        out_specs=[row(D), pl.BlockSpec((2, tm, D // 4), lambda i: (0, i, 0)), pl.BlockSpec((E, tm), lambda i: (0, i))],
        out_shape=[jax.ShapeDtypeStruct((N, D), F32), jax.ShapeDtypeStruct((2, N, D // 4), U32),
                   jax.ShapeDtypeStruct((E, N), F32)],
        compiler_params=_cp("parallel"),
        name="post_mixer",
    )(attn, rw, gate, x2, mod, wa, wb, wo, norm2_g.reshape(1, D), wr_hi, wr_mid, sug, sd)


def _first_argmax(x, idx, sentinel):
    m = jnp.max(x, axis=0, keepdims=True)
    return m, jnp.min(jnp.where(x == m, idx, sentinel), axis=0, keepdims=True)


def _route_kernel(lg_ref, bias_ref, e_ref, w_ref):
    E, T = lg_ref.shape
    gsz = E // N_GROUPS
    scores = _sigmoid(lg_ref[...])
    biased = scores + bias_ref[...]
    ig = lax.broadcasted_iota(I32, (gsz, T), 0)
    gs = []
    for g in range(N_GROUPS):
        blk = biased[g * gsz:(g + 1) * gsz, :]
        m1, i1 = _first_argmax(blk, ig, gsz)
        m2 = jnp.max(jnp.where(ig == i1, -jnp.inf, blk), axis=0, keepdims=True)
        gs.append(m1 + m2)
    gsc = jnp.concatenate(gs, axis=0)
    i8 = lax.broadcasted_iota(I32, (N_GROUPS, T), 0)
    chosen = jnp.zeros((N_GROUPS, T), F32)
    for _ in range(TOPK_GROUPS):
        _, gi = _first_argmax(gsc, i8, N_GROUPS)
        hit = i8 == gi
        chosen = jnp.where(hit, 1.0, chosen)
        gsc = jnp.where(hit, -jnp.inf, gsc)
    masked = jnp.concatenate(
        [jnp.where(chosen[g:g + 1, :] > 0.0, biased[g * gsz:(g + 1) * gsz, :], -jnp.inf)
         for g in range(N_GROUPS)], axis=0)
    ie = lax.broadcasted_iota(I32, (E, T), 0)
    idxs, wts = [], []
    for _ in range(TOP_K):
        _, ei = _first_argmax(masked, ie, E)
        hit = ie == ei
        idxs.append(ei)
        wts.append(jnp.sum(jnp.where(hit, scores, 0.0), axis=0, keepdims=True))
        masked = jnp.where(hit, -jnp.inf, masked)
    wt = jnp.concatenate(wts, axis=0)
    e_ref[...] = jnp.concatenate(idxs, axis=0)
    w_ref[...] = wt / jnp.sum(wt, axis=0, keepdims=True) * ROUTED_SCALE


def _route(logits_t, router_bias, T=512):
    E, N = logits_t.shape
    blk = pl.BlockSpec((TOP_K, T), lambda i: (0, i))
    return pl.pallas_call(
        _route_kernel,
        grid=(N // T,),
        in_specs=[pl.BlockSpec((E, T), lambda i: (0, i)), pl.BlockSpec((E, 1), lambda i: (0, 0))],
        out_specs=[blk, blk],
        out_shape=[jax.ShapeDtypeStruct((TOP_K, N), I32), jax.ShapeDtypeStruct((TOP_K, N), F32)],
        compiler_params=_cp("parallel"),
        name="route",
    )(logits_t, router_bias.reshape(E, 1))


def _rank_kernel(e_ref, rank_ref, cnt_ref, carry_ref, *, n_experts):
    T = e_ref.shape[1]

    @pl.when(pl.program_id(0) == 0)
    def _():
        carry_ref[...] = jnp.zeros_like(carry_ref)

    ie = lax.broadcasted_iota(I32, (n_experts, T), 0)
    e = e_ref[...]
    hits = [ie == e[kk:kk + 1, :] for kk in range(TOP_K)]
    onehot = jnp.zeros((n_experts, T), F32)
    for hsel in hits:
        onehot = onehot + hsel.astype(F32)
    tr = lax.broadcasted_iota(I32, (T, T), 0)
    tc = lax.broadcasted_iota(I32, (T, T), 1)
    before = (tr < tc).astype(BF16)
    base = _bdot(onehot, before) + carry_ref[:, 0:1]
    rank_ref[...] = jnp.concatenate(
        [jnp.sum(jnp.where(hsel, base, 0.0), axis=0, keepdims=True) for hsel in hits], axis=0).astype(I32)
    carry_ref[...] = carry_ref[...] + jnp.sum(onehot, axis=1, keepdims=True)
    cnt_ref[...] = carry_ref[...]


def _ranks(eidx_t, n_experts, T=1024):
    N = eidx_t.shape[1]
    T = min(T, N)
    blk = pl.BlockSpec((TOP_K, T), lambda i: (0, i))
    cnt = pl.BlockSpec((n_experts, LANES), lambda i: (0, 0))
    return pl.pallas_call(
        functools.partial(_rank_kernel, n_experts=n_experts),
        grid=(N // T,),
        in_specs=[blk],
        out_specs=[blk, cnt],
        out_shape=[jax.ShapeDtypeStruct((TOP_K, N), I32), jax.ShapeDtypeStruct((n_experts, LANES), F32)],
        scratch_shapes=[pltpu.VMEM((n_experts, LANES), F32)],
        compiler_params=_cp("arbitrary"),
        name="ranks",
    )(eidx_t)


def _dest_kernel(e_ref, rank_ref, start_ref, d_ref):
    E = start_ref.shape[0]
    T = e_ref.shape[1]
    ie = lax.broadcasted_iota(I32, (E, T), 0)
    e = e_ref[...]
    start = start_ref[:, 0:1]
    rows = [jnp.sum(jnp.where(ie == e[kk:kk + 1, :], start, 0.0), axis=0, keepdims=True) for kk in range(TOP_K)]
    d_ref[...] = jnp.concatenate(rows, axis=0).astype(I32) + rank_ref[...]


def _dests(eidx_t, rank_t, pstart, T=2048):
    N = eidx_t.shape[1]
    T = min(T, N)
    E = pstart.shape[0]
    blk = pl.BlockSpec((TOP_K, T), lambda i: (0, i))
    return pl.pallas_call(
        _dest_kernel,
        grid=(N // T,),
        in_specs=[blk, blk, pl.BlockSpec((E, LANES), lambda i: (0, 0))],
        out_specs=blk,
        out_shape=jax.ShapeDtypeStruct((TOP_K, N), I32),
        compiler_params=_cp("parallel"),
        name="dests",
    )(eidx_t, rank_t, jnp.broadcast_to(pstart.astype(F32)[:, None], (E, LANES)))


def _expert_kernel(us_ref, ps_ref, x_hbm, wug_hbm, wd_hbm, y_hbm,
                   wug_f32, wd_f32, wug_bf, wd_bf, xbuf, ybuf, cnt_ref, wsem, xsem, ysem, *, n_rows):
    e = pl.program_id(0)
    n_experts = pl.num_programs(0)
    _, _, R, Ch = xbuf.shape
    C = 2 * Ch
    F = wd_hbm.shape[1]

    def w_copies(expert, slot):
        return (pltpu.make_async_copy(wug_hbm.at[expert], wug_f32.at[slot], wsem.at[slot]),
                pltpu.make_async_copy(wd_hbm.at[expert], wd_f32.at[slot], wsem.at[slot]))

    def x_copy(slot, half, row):
        return pltpu.make_async_copy(x_hbm.at[half, pl.ds(row, R)], xbuf.at[slot, half], xsem.at[slot])

    def y_copy(slot, half, row):
        return pltpu.make_async_copy(ybuf.at[slot, half], y_hbm.at[half, pl.ds(row, R)], ysem.at[slot])

    def start(copy, slot, row):
        for half in range(2):
            copy(slot, half, row).start(priority=1)

    def wait(copy, slot):
        for half in range(2):
            copy(slot, half, 0).wait()

    @pl.when(e == 0)
    def _():
        for ahead in range(W_SLOTS - 1):
            for cp in w_copies(ahead, ahead):
                cp.start()
        cnt_ref[0] = 0
        for t in range(X_AHEAD):
            start(x_copy, t, t * R)
        ybuf[...] = jnp.zeros_like(ybuf)
        for slot in range(2):
            start(y_copy, slot, n_rows + slot * R)

    @pl.when(e + (W_SLOTS - 1) < n_experts)
    def _():
        for cp in w_copies(e + (W_SLOTS - 1), lax.rem(e + (W_SLOTS - 1), W_SLOTS)):
            cp.start()

    wslot = lax.rem(e, W_SLOTS)
    for cp in w_copies(e, wslot):
        cp.wait()
    wug_bf[...] = wug_f32[wslot].astype(BF16)
    wd_bf[...] = wd_f32[wslot].astype(BF16)
    n_valid = us_ref[e + 1] - us_ref[e]
    pbase = ps_ref[e]
    n_tiles = lax.shift_right_logical(n_valid + (R - 1), int(math.log2(R)))

    def make_tile(nb):
        rowid = lax.broadcasted_iota(I32, (nb * R, C), 0)

        def tile(blk0):
            n_done = cnt_ref[0]
            xslots = [lax.rem(n_done + b, X_SLOTS) for b in range(nb)]
            for b in range(nb):
                ahead = n_done + X_AHEAD + b
                start(x_copy, lax.rem(ahead, X_SLOTS), pl.multiple_of(ahead * R, R))
            for b in range(nb):
                wait(x_copy, xslots[b])
            x = jnp.concatenate([jnp.concatenate([xbuf[xs, 0], xbuf[xs, 1]], axis=1) for xs in xslots], axis=0)
            x = jnp.where(rowid < n_valid - blk0 * R, x, jnp.uint32(0))
            lo, hi = _unpack_halves(x)
            gu = (jnp.dot(lo.astype(BF16), wug_bf[0:C, :], preferred_element_type=F32)
                  + jnp.dot(hi.astype(BF16), wug_bf[C:, :], preferred_element_type=F32))
            hid = (_silu(gu[:, :F]) * gu[:, F:]).astype(BF16)
            y = jnp.dot(hid, wd_bf[...], preferred_element_type=F32)
            packed = _pack_halves(y.astype(BF16).astype(F32))
            for b in range(nb):
                slot = (n_done + b) & 1
                wait(y_copy, slot)
                ybuf[slot, 0] = packed[b * R:(b + 1) * R, :Ch]
                ybuf[slot, 1] = packed[b * R:(b + 1) * R, Ch:]
                start(y_copy, slot, pl.multiple_of(pbase + (blk0 + b) * R, R))
            cnt_ref[0] = n_done + nb

        return tile

    pair_tile, single_tile = make_tile(2), make_tile(1)

    def pairs(j, carry):
        pair_tile(2 * j)
        return carry

    lax.fori_loop(0, lax.shift_right_logical(n_tiles, 1), pairs, 0)

    @pl.when((n_tiles & 1) == 1)
    def _():
        single_tile(n_tiles - 1)

    @pl.when(e == pl.num_programs(0) - 1)
    def _():
        for t in range(X_AHEAD):
            wait(x_copy, lax.rem(cnt_ref[0] + t, X_SLOTS))
        for slot in range(2):
            wait(y_copy, slot)
        ybuf[0] = jnp.zeros((2, R, Ch), U32)
        first = lax.shift_right_logical(pbase + n_tiles * R, int(math.log2(R)))
        n_left = n_rows // R - first

        def fill(t, carry):
            start(y_copy, 0, pl.multiple_of((first + t) * R, R))
            return carry

        def drain(t, carry):
            wait(y_copy, 0)
            return carry

        lax.fori_loop(0, n_left, fill, 0)
        lax.fori_loop(0, n_left, drain, 0)


def _experts(ustart, pstart, xg, w_ug, w_d, n_rows, R):
    _, _, Ch = xg.shape
    E, D, F2 = w_ug.shape
    F = w_d.shape[1]
    grid_spec = pltpu.PrefetchScalarGridSpec(
        num_scalar_prefetch=2,
        grid=(E,),
        in_specs=[pl.BlockSpec(memory_space=pl.ANY)] * 3,
        out_specs=pl.BlockSpec(memory_space=pl.ANY),
        scratch_shapes=[pltpu.VMEM((W_SLOTS, D, F2), F32), pltpu.VMEM((W_SLOTS, F, D), F32),
                        pltpu.VMEM((D, F2), BF16), pltpu.VMEM((F, D), BF16),
                        pltpu.VMEM((X_SLOTS, 2, R, Ch), U32), pltpu.VMEM((2, 2, R, Ch), U32), pltpu.SMEM((1,), I32),
                        pltpu.SemaphoreType.DMA((W_SLOTS,)), pltpu.SemaphoreType.DMA((X_SLOTS,)),
                        pltpu.SemaphoreType.DMA((2,))],
    )
    return pl.pallas_call(
        functools.partial(_expert_kernel, n_rows=n_rows),
        grid_spec=grid_spec,
        out_shape=jax.ShapeDtypeStruct((2, n_rows + 2 * R, Ch), U32),
        compiler_params=_cp("arbitrary"),
        name="experts",
    )(ustart, pstart, xg, w_ug, w_d)


def _sc_scatter_rows(src, idx, n_rows):
    H, N, C = src.shape
    K = idx.shape[0]
    per_row = N // SC_WINDOW
    mesh = plsc.VectorSubcoreMesh(core_axis_name="c", subcore_axis_name="s")

    @functools.partial(pl.kernel, out_type=jax.ShapeDtypeStruct((H, n_rows, C), src.dtype), mesh=mesh,
                       scratch_types=[])
    def scatter_kernel(x_hbm, i_hbm, o_hbm):
        for h in range(H):
            def body(x_vmem, i_vmem):
                for k in range(K):
                    pltpu.sync_copy(x_vmem, o_hbm.at[h].at[i_vmem.at[k]])

            pltpu.emit_pipeline(
                body,
                grid=(per_row,),
                in_specs=[pl.BlockSpec((SC_WINDOW, C), lambda i: (i, 0)),
                          pl.BlockSpec((K, SC_WINDOW), lambda i: (0, i))],
                out_specs=[],
                core_axis_name=("c", "s"),
                dimension_semantics=(pltpu.PARALLEL,),
            )(x_hbm.at[h], i_hbm)

    return scatter_kernel(src, idx)


def _sc_gather_rows(src, idx):
    H, _, C = src.shape
    K, N = idx.shape
    per_row = N // SC_WINDOW
    mesh = plsc.VectorSubcoreMesh(core_axis_name="c", subcore_axis_name="s")

    @functools.partial(pl.kernel, out_type=jax.ShapeDtypeStruct((H, K * N, C), src.dtype), mesh=mesh,
                       scratch_types=[])
    def gather_kernel(x_hbm, i_hbm, o_hbm):
        for h in range(H):
            def body(i_vmem, o_vmem):
                pltpu.sync_copy(x_hbm.at[h].at[i_vmem.at[0]], o_vmem)

            pltpu.emit_pipeline(
                body,
                grid=(K * per_row,),
                in_specs=[pl.BlockSpec((1, SC_WINDOW), lambda i: (i // per_row, i % per_row))],
                out_specs=[pl.BlockSpec((SC_WINDOW, C), lambda i: (i, 0))],
                core_axis_name=("c", "s"),
                dimension_semantics=(pltpu.PARALLEL,),
            )(i_hbm, o_hbm.at[h])

    return gather_kernel(src, idx)


def _combine_kernel(*refs):
    y_refs = refs[:2 * TOP_K]
    w_ref, base_ref, mod_ref, o_ref = refs[2 * TOP_K:]
    T = base_ref.shape[0]
    tr = lax.broadcasted_iota(I32, (T, T), 0)
    tc = lax.broadcasted_iota(I32, (T, T), 1)
    wcol = _nt((tr == tc).astype(F32), w_ref[...], precision=HI)
    acc = [None] * 4
    for kk in range(TOP_K):
        wk = wcol[:, kk:kk + 1]
        for half in range(2):
            lo, hi = _unpack_halves(y_refs[2 * kk + half][...])
            for q, val in ((half, lo), (2 + half, hi)):
                acc[q] = val * wk if acc[q] is None else acc[q] + val * wk
    o_ref[...] = base_ref[...] + mod_ref[5:6, :] * jnp.concatenate(acc, axis=1)


def _combine(yg, w_t, base, mod, S, T=512):
    N, D = base.shape
    C = yg.shape[2]
    tpb = S // T
    n_tiles = N // T
    row = pl.BlockSpec((T, D), lambda i: (i, 0))
    piece = lambda kk, half: pl.BlockSpec((None, T, C), lambda i: (half, kk * n_tiles + i, 0))
    return pl.pallas_call(
        _combine_kernel,
        grid=(n_tiles,),
        in_specs=[piece(kk, half) for kk in range(TOP_K) for half in range(2)] + [
            pl.BlockSpec((TOP_K, T), lambda i: (0, i)),
            row,
            pl.BlockSpec((None, 6, D), lambda i: (i // tpb, 0, 0))],
        out_specs=row,
        out_shape=jax.ShapeDtypeStruct((N, D), F32),
        compiler_params=_cp("parallel"),
        name="combine",
    )(*([yg] * (2 * TOP_K)), w_t, base, mod)


def _layer(x, c, positions, layer_idx, w_ada, b_ada, norm1_g, w_in, w_gate, b_gate,
           q_norm_g, k_norm_g, lambda_q1, lambda_k1, lambda_q2, lambda_k2, subln_g,
           rwkv_mu, w_decay0, w_decay2, a0, a2, g2, k_k, k_a, r_k, ln_x_w, ln_x_b,
           w_branch_a, w_branch_b, w_out, norm2_g, w_router, router_bias,
           w_expert_up_gate, w_expert_down, w_shared_up_gate, w_shared_down):
    B, S, D = x.shape
    N = B * S
    E = w_router.shape[1]
    da_width = w_branch_a.shape[0]
    rw_width = w_branch_b.shape[0]
    lambda_init = 0.8 - 0.6 * math.exp(-0.3 * layer_idx)

    mod = _adaln(c, w_ada, b_ada)
    x2 = x.reshape(N, D)
    w_cat = jnp.concatenate([w_in, w_gate], axis=1).astype(BF16)
    qn, kn, v, r_, k_, v_, a_, ld_, g_, gate = _mixer_in(
        x2, positions.reshape(N, 1), mod, norm1_g, w_cat, b_gate, q_norm_g, k_norm_g,
        rwkv_mu, w_decay0, w_decay2, a0, a2, g2, S, da_width, rw_width)

    lam_vecs = jnp.stack([lambda_q1, lambda_k1, lambda_q2, lambda_k2])
    score_bound = 1.01 * DA_HEAD_DIM ** 0.5 * jnp.max(jnp.abs(q_norm_g)) * jnp.max(jnp.abs(k_norm_g))
    attn = _diff_attention(qn, kn, v, score_bound, lam_vecs, subln_g, B, S, lambda_init)

    seq = lambda t: t.reshape(B, S, rw_width)
    rw = _rwkv_scan(seq(r_), seq(k_), seq(v_), seq(a_), seq(ld_), seq(g_), k_k, k_a, r_k.reshape(-1),
                    ln_x_w, ln_x_b).reshape(N, rw_width)

    base, h2p, logits_t = _post(attn, rw, gate, x2, mod, w_branch_a.astype(BF16), w_branch_b.astype(BF16),
                                w_out.astype(BF16), norm2_g, w_router.T,
                                w_shared_up_gate.astype(BF16), w_shared_down.astype(BF16), S)

    eidx_t, w_t = _route(logits_t, router_bias)
    rank_t, counts = _ranks(eidx_t, E)
    R = EXPERT_TILE
    cnt = counts[:, 0].astype(I32)
    ustart = jnp.concatenate([jnp.zeros((1,), I32), jnp.cumsum(cnt)])
    pcnt = (cnt + R - 1) // R * R
    pstart = jnp.cumsum(pcnt) - pcnt
    dest_p = _dests(eidx_t, rank_t, pstart)
    n_rows = (N * TOP_K + E * (R - 1) + R - 1) // R * R
    xg = _sc_scatter_rows(h2p, dest_p, n_rows + X_AHEAD * R)
    y = _experts(ustart, pstart, xg, w_expert_up_gate, w_expert_down, n_rows, R)
    yg = _sc_gather_rows(y, dest_p)
    out = _combine(yg, w_t, base, mod, S)
    return out.reshape(B, S, D)


def kernel(x, c, positions, w_ada, b_ada, norm1_g, w_in, w_gate, b_gate, q_norm_g, k_norm_g, lambda_q1, lambda_k1, lambda_q2, lambda_k2, subln_g, rwkv_mu, w_decay0, w_decay2, a0, a2, g2, k_k, k_a, r_k, ln_x_w, ln_x_b, w_branch_a, w_branch_b, w_out, norm2_g, w_router, router_bias, w_expert_up_gate, w_expert_down, w_shared_up_gate, w_shared_down):
    for l in range(w_ada.shape[0]):
        x = _layer(x, c, positions, l, w_ada[l], b_ada[l], norm1_g[l], w_in[l], w_gate[l], b_gate[l],
                   q_norm_g[l], k_norm_g[l], lambda_q1[l], lambda_k1[l], lambda_q2[l], lambda_k2[l],
                   subln_g[l], rwkv_mu[l], w_decay0[l], w_decay2[l], a0[l], a2[l], g2[l], k_k[l],
                   k_a[l], r_k[l], ln_x_w[l], ln_x_b[l], w_branch_a[l], w_branch_b[l], w_out[l],
                   norm2_g[l], w_router[l], router_bias[l], w_expert_up_gate[l], w_expert_down[l],
                   w_shared_up_gate[l], w_shared_down[l])
    return x
```

```python
import functools
import math

import jax
import jax.numpy as jnp
from jax import lax
from jax.experimental import pallas as pl
from jax.experimental.pallas import tpu as pltpu
from jax.experimental.pallas import tpu_sc as plsc

F32 = jnp.float32
BF16 = jnp.bfloat16
I32 = jnp.int32
U32 = jnp.uint32
HI = lax.Precision.HIGHEST

CHUNK = 64
ROPE_THETA = 10000.0
NORM_EPS = 1e-6
SUBLN_EPS = 1e-5
DA_HEAD_DIM = 64
RWKV_HEAD = 64
GN_EPS = 64e-5
TOP_K = 8
N_GROUPS = 8
TOPK_GROUPS = 4
ROUTED_SCALE = 2.5
EXPERT_TILE = 256
W_SLOTS = 3
X_SLOTS = 6
X_AHEAD = X_SLOTS - 2
RWKV_CHUNK = 64
LANES = 128
SC_WINDOW = 128
NEG = -1e30
MAX_PLAIN_SCORE = 40.0
VMEM_LIMIT = 56 * 1024 * 1024


def _cp(*sem):
    return pltpu.CompilerParams(dimension_semantics=sem, vmem_limit_bytes=VMEM_LIMIT)


def _bdot(a, b):
    return jnp.dot(a.astype(BF16), b.astype(BF16), preferred_element_type=F32)


def _fdot(a, b):
    return jnp.dot(a, b, precision=HI, preferred_element_type=F32)


def _nt(a, b, precision=None):
    return lax.dot_general(a, b, (((1,), (1,)), ((), ())), precision=precision,
                           preferred_element_type=F32)


def _tn(a, b, precision=None):
    return lax.dot_general(a, b, (((0,), (0,)), ((), ())), precision=precision,
                           preferred_element_type=F32)


def _pack_halves(x):
    c = x.shape[1] // 2
    lo = lax.bitcast_convert_type(x[:, :c], U32)
    hi = lax.bitcast_convert_type(x[:, c:], U32)
    return (hi & jnp.uint32(0xFFFF0000)) | (lo >> 16)


def _unpack_halves(w):
    lo = lax.bitcast_convert_type(w << 16, F32)
    hi = lax.bitcast_convert_type(w & jnp.uint32(0xFFFF0000), F32)
    return lo, hi


def _sigmoid(x):
    return 1.0 / (1.0 + jnp.exp(-x))


def _silu(x):
    return x * _sigmoid(x)


def _ada_kernel(c_ref, w_ref, b_ref, o_ref):
    o_ref[...] = _fdot(_silu(c_ref[...]), w_ref[...]) + b_ref[...]


def _adaln(c, w_ada, b_ada):
    B, D = c.shape
    rows = -(-B // 8) * 8
    cpad = jnp.zeros((rows, D), F32).at[:B].set(c)
    n_out = w_ada.shape[1]
    out = pl.pallas_call(
        _ada_kernel,
        grid=(n_out // D,),
        in_specs=[pl.BlockSpec((rows, D), lambda j: (0, 0)),
                  pl.BlockSpec((D, D), lambda j: (0, j)),
                  pl.BlockSpec((1, D), lambda j: (0, j))],
        out_specs=pl.BlockSpec((rows, D), lambda j: (0, j)),
        out_shape=jax.ShapeDtypeStruct((rows, n_out), F32),
        compiler_params=_cp("arbitrary"),
        name="adaln",
    )(cpad, w_ada, b_ada.reshape(1, n_out))
    return out[:B].reshape(B, n_out // D, D)


def _mixer_in_kernel(x_ref, xprev_ref, mod_ref, g_ref, w_ref, bg_ref, pos_ref, invf_ref, qg_ref, kg_ref,
                     mu_ref, w0_ref, w2_ref, a0_ref, a2_ref, g2_ref,
                     qn_ref, kn_ref, v_ref, r_ref, k_ref, vr_ref, a_ref, ld_ref, gr_ref, gate_ref,
                     *, da_width, rw_cols, rw_width, q_scale, tiles_per_seq):
    tm = x_ref.shape[0]

    def modulated(x):
        y = x * lax.rsqrt(jnp.mean(x * x, axis=-1, keepdims=True) + NORM_EPS) * g_ref[...]
        return (y * (1.0 + mod_ref[1:2, :]) + mod_ref[0:1, :]).astype(BF16)

    def proj(hb, c0, width, step=512):
        parts = [jnp.dot(hb, w_ref[:, c0 + o:c0 + min(o + step, width)], preferred_element_type=F32)
                 for o in range(0, width, step)]
        return parts[0] if len(parts) == 1 else jnp.concatenate(parts, axis=1)

    h = modulated(x_ref[...])

    lane = lax.broadcasted_iota(I32, (tm, LANES), 1)
    first = lane < DA_HEAD_DIM
    lo_half = (lane & (DA_HEAD_DIM - 1)) < DA_HEAD_DIM // 2
    ang = pos_ref[...].astype(F32) * invf_ref[...]
    cos = jnp.cos(ang)
    sin = jnp.sin(ang)
    sin = jnp.where(lo_half, -sin, sin)
    for c0, dst, gn_ref, mult in ((0, qn_ref, qg_ref, q_scale), (da_width, kn_ref, kg_ref, 1.0)):
        raw = proj(h, c0, da_width)
        for blk in range(da_width // LANES):
            x = raw[:, blk * LANES:(blk + 1) * LANES]
            xx = x * x
            s_first = jnp.sum(jnp.where(first, xx, 0.0), axis=-1, keepdims=True)
            s_second = jnp.sum(jnp.where(first, 0.0, xx), axis=-1, keepdims=True)
            ms = jnp.where(first, s_first, s_second) * (1.0 / DA_HEAD_DIM)
            xn = x * lax.rsqrt(ms + NORM_EPS) * gn_ref[...]
            rot = jnp.where(lo_half, pltpu.roll(xn, LANES - DA_HEAD_DIM // 2, axis=1),
                            pltpu.roll(xn, DA_HEAD_DIM // 2, axis=1))
            dst[:, blk * LANES:(blk + 1) * LANES] = ((xn * cos + rot * sin) * mult).astype(dst.dtype)
    v_ref[...] = proj(h, 2 * da_width, da_width).astype(v_ref.dtype)

    c_rw = 3 * da_width
    p = proj(h, c_rw, rw_cols)
    p_before = proj(modulated(xprev_ref[...]), c_rw, rw_cols)
    seq_start = (pl.program_id(0) % tiles_per_seq) == 0
    last_prev = jnp.where(seq_start, 0.0, p_before[7:8, :])
    rowi = lax.broadcasted_iota(I32, p.shape, 0)
    prev = jnp.where(rowi == 0, last_prev, pltpu.roll(p, 1, axis=0))
    xs = p + (prev - p) * mu_ref[...]
    width = rw_width
    r_ref[...] = xs[:, 0:width]
    k_ref[...] = xs[:, width:2 * width]
    vr_ref[...] = xs[:, 2 * width:3 * width]
    xwa = xs[:, 3 * width:3 * width + LANES]
    xg = xs[:, 3 * width + LANES:]
    z = w0_ref[...] + _bdot(jnp.tanh(xwa), w2_ref[...])
    w = -(jnp.maximum(-z, 0.0) + jnp.log(1.0 + jnp.exp(-jnp.abs(z)))) - 0.5
    ld_ref[...] = -jnp.exp(w)
    a_ref[...] = _sigmoid(a0_ref[...] + _bdot(xwa, a2_ref[...]))
    gr_ref[...] = _bdot(_sigmoid(xg), g2_ref[...])

    gate_ref[...] = _sigmoid(proj(h, c_rw + rw_cols, gate_ref.shape[1]) + bg_ref[...]).astype(gate_ref.dtype)


def _mixer_in(x2, pos2, mod, norm1_g, w_cat, b_gate, q_norm_g, k_norm_g, mu, w_decay0, w_decay2, a0, a2, g2,
              S, da_width, rw_width, tm=512):
    N, D = x2.shape
    n_gate = b_gate.shape[0]
    rw_cols = mu.shape[0]
    tpb = S // tm
    d = DA_HEAD_DIM
    inv_freq = 1.0 / (ROPE_THETA ** (jnp.arange(0, d, 2, dtype=F32) / d))
    invf = jnp.tile(inv_freq, LANES // (d // 2)).reshape(1, LANES)
    dl, al = w_decay2.shape[0], a2.shape[0]
    assert dl + al == LANES and g2.shape[0] == LANES
    w2p = jnp.zeros((LANES, rw_width), F32).at[:dl].set(w_decay2)
    a2p = jnp.zeros((LANES, rw_width), F32).at[dl:].set(a2)
    kern = functools.partial(_mixer_in_kernel, da_width=da_width, rw_cols=rw_cols, rw_width=rw_width,
                             q_scale=d ** -0.5 * math.log2(math.e),
                             tiles_per_seq=tpb)
    row = lambda w: pl.BlockSpec((tm, w), lambda i: (i, 0))
    vec = lambda n: pl.BlockSpec((1, n), lambda i: (0, 0))
    mat = pl.BlockSpec((LANES, rw_width), lambda i: (0, 0))
    f32 = lambda w: jax.ShapeDtypeStruct((N, w), F32)
    bf16 = lambda w: jax.ShapeDtypeStruct((N, w), BF16)
    return pl.pallas_call(
        kern,
        grid=(N // tm,),
        in_specs=[row(D),
                  pl.BlockSpec((8, D), lambda i: (jnp.maximum(i * (tm // 8) - 1, 0), 0)),
                  pl.BlockSpec((None, 6, D), lambda i: (i // tpb, 0, 0)),
                  vec(D),
                  pl.BlockSpec(w_cat.shape, lambda i: (0, 0)),
                  vec(n_gate),
                  pl.BlockSpec((tm, 1), lambda i: (i, 0)),
                  vec(LANES), vec(LANES), vec(LANES),
                  vec(rw_cols), vec(rw_width), mat, vec(rw_width), mat, mat],
        out_specs=[row(da_width)] * 3 + [row(rw_width)] * 6 + [row(n_gate)],
        out_shape=[bf16(da_width)] * 3 + [f32(rw_width)] * 6 + [bf16(n_gate)],
        compiler_params=_cp("parallel"),
        name="mixer_in",
    )(x2, x2, mod, norm1_g.reshape(1, D), w_cat, b_gate.reshape(1, n_gate), pos2, invf,
      jnp.tile(q_norm_g, 2).reshape(1, LANES), jnp.tile(k_norm_g, 2).reshape(1, LANES),
      mu.reshape(1, rw_cols), w_decay0.reshape(1, rw_width), w2p, a0.reshape(1, rw_width), a2p, g2)


def _attn_kernel(flag_ref, q_ref, k_ref, v_ref, vt_ref, lam_ref, sg_ref, sgc_ref, o_ref,
                 qz_ref, m_ref, l_ref, acc_ref, lpt_ref, acct_ref, *, tq, lambda_init):
    i = pl.program_id(2)
    lane = lax.broadcasted_iota(I32, (tq, LANES), 1)
    q = q_ref[...]
    zero = jnp.zeros_like(q)
    qz_ref[0:tq, :] = jnp.where(lane < DA_HEAD_DIM, q, zero)
    qz_ref[tq:, :] = jnp.where(lane >= DA_HEAD_DIM, q, zero)
    bounded = flag_ref[0] == 1
    lv = lam_ref[...]
    lam = (jnp.exp(jnp.sum(lv[0:1] * lv[1:2], keepdims=True))
           - jnp.exp(jnp.sum(lv[2:3] * lv[3:4], keepdims=True)) + lambda_init)

    def run(step):
        def body(j, carry):
            step(j, False)
            return carry
        lax.fori_loop(0, i, body, 0)
        step(i, True)

    def plain_step(j, masked):
        off = pl.multiple_of(j * tq, tq)
        st = _nt(k_ref[pl.ds(off, tq), :], qz_ref[...])
        if masked:
            row = lax.broadcasted_iota(I32, st.shape, 0)
            col = lax.broadcasted_iota(I32, st.shape, 1)
            st = jnp.where((row // CHUNK) <= ((col & (tq - 1)) // CHUNK), st, NEG)
        pt = jnp.exp2(st)
        part = pt[0:8, :]
        for g in range(1, tq // 8):
            part = part + pt[8 * g:8 * g + 8, :]
        lpt_ref[...] += part
        acct_ref[...] += jnp.dot(vt_ref[:, pl.ds(off, tq)], pt.astype(BF16), preferred_element_type=F32)

    @pl.when(bounded)
    def _():
        lpt_ref[...] = jnp.zeros_like(lpt_ref)
        acct_ref[...] = jnp.zeros_like(acct_ref)

        def two_steps(p, carry):
            plain_step(2 * p, False)
            plain_step(2 * p + 1, False)
            return carry

        lax.fori_loop(0, lax.shift_right_logical(i, 1), two_steps, 0)

        @pl.when((i & 1) == 1)
        def _():
            plain_step(i - 1, False)
            plain_step(i, True)

        @pl.when((i & 1) == 0)
        def _():
            plain_step(i, True)

        lsum = jnp.sum(lpt_ref[...], axis=0, keepdims=True)
        ot = acct_ref[:, 0:tq] / lsum[:, 0:tq] - lam * (acct_ref[:, tq:] / lsum[:, tq:])
        ot = ot * lax.rsqrt(jnp.mean(ot * ot, axis=0, keepdims=True) + SUBLN_EPS) * sgc_ref[...]
        o_ref[...] = (ot * (1.0 - lambda_init)).T.astype(o_ref.dtype)

    def online_step(j, masked):
        off = pl.multiple_of(j * tq, tq)
        s = _nt(qz_ref[...], k_ref[pl.ds(off, tq), :])
        if masked:
            row = lax.broadcasted_iota(I32, s.shape, 0)
            col = lax.broadcasted_iota(I32, s.shape, 1)
            s = jnp.where((col // CHUNK) <= ((row & (tq - 1)) // CHUNK), s, NEG)
        m_old = m_ref[...]
        m_new = jnp.maximum(m_old, jnp.max(s, axis=-1, keepdims=True))
        alpha = jnp.exp2(m_old - m_new)
        pr = jnp.exp2(s - m_new)
        l_ref[...] = alpha * l_ref[...] + jnp.sum(pr, axis=-1, keepdims=True)
        acc_ref[...] = alpha * acc_ref[...] + jnp.dot(pr.astype(BF16), v_ref[pl.ds(off, tq), :],
                                                      preferred_element_type=F32)
        m_ref[...] = m_new

    @pl.when(jnp.logical_not(bounded))
    def _():
        m_ref[...] = jnp.full_like(m_ref, NEG)
        l_ref[...] = jnp.zeros_like(l_ref)
        acc_ref[...] = jnp.zeros_like(acc_ref)
        run(online_step)
        o = acc_ref[0:tq, :] / l_ref[0:tq, :] - lam * (acc_ref[tq:, :] / l_ref[tq:, :])
        o = o * lax.rsqrt(jnp.mean(o * o, axis=-1, keepdims=True) + SUBLN_EPS) * sg_ref[...]
        o_ref[...] = (o * (1.0 - lambda_init)).astype(o_ref.dtype)


def _diff_attention(qn, kn, v, score_bound, lam_vecs, subln_g, B, S, lambda_init, tq=512):
    W = qn.shape[1]
    H = W // LANES
    q3 = qn.reshape(B, S, W)
    k3 = kn.reshape(B, S, W)
    v3 = v.reshape(B, S, W)
    vt3 = v3.transpose(0, 2, 1)
    flag = (score_bound <= MAX_PLAIN_SCORE).astype(I32).reshape(1)
    qblk = pl.BlockSpec((None, tq, LANES), lambda b, h, i, f: (b, i, h))
    kvblk = pl.BlockSpec((None, S, LANES), lambda b, h, i, f: (b, 0, h))
    grid_spec = pltpu.PrefetchScalarGridSpec(
        num_scalar_prefetch=1,
        grid=(B, H, S // tq),
        in_specs=[qblk, kvblk, kvblk,
                  pl.BlockSpec((None, LANES, S), lambda b, h, i, f: (b, h, 0)),
                  pl.BlockSpec((4, DA_HEAD_DIM), lambda b, h, i, f: (0, 0)),
                  pl.BlockSpec((1, LANES), lambda b, h, i, f: (0, 0)),
                  pl.BlockSpec((LANES, 1), lambda b, h, i, f: (0, 0))],
        out_specs=qblk,
        scratch_shapes=[pltpu.VMEM((2 * tq, LANES), BF16),
                        pltpu.VMEM((2 * tq, 1), F32),
                        pltpu.VMEM((2 * tq, 1), F32),
                        pltpu.VMEM((2 * tq, LANES), F32),
                        pltpu.VMEM((8, 2 * tq), F32),
                        pltpu.VMEM((LANES, 2 * tq), F32)],
    )
    out = pl.pallas_call(
        functools.partial(_attn_kernel, tq=tq, lambda_init=lambda_init),
        grid_spec=grid_spec,
        out_shape=jax.ShapeDtypeStruct((B, S, W), BF16),
        compiler_params=_cp("parallel", "parallel", "arbitrary"),
        name="diff_attn",
    )(flag, q3, k3, v3, vt3, lam_vecs, subln_g.reshape(1, LANES), subln_g.reshape(LANES, 1))
    return out.reshape(B * S, W)


def _stackmask(m):
    lane = lax.broadcasted_iota(I32, m.shape, 1)
    z = jnp.zeros_like(m)
    return jnp.concatenate([jnp.where(lane < RWKV_HEAD, m, z), jnp.where(lane >= RWKV_HEAD, m, z)], axis=0)


def _pair_sum(x, first):
    s1 = jnp.sum(jnp.where(first, x, 0.0), axis=-1, keepdims=True)
    s2 = jnp.sum(jnp.where(first, 0.0, x), axis=-1, keepdims=True)
    return jnp.where(first, s1, s2)


def _rwkv_scan_kernel(r_ref, k_ref, v_ref, a_ref, ld_ref, g_ref, kk_ref, ka_ref, rk_ref, lnw_ref, lnb_ref,
                      o_ref, s_ref, *, L):
    tm, W = r_ref.shape
    n_chunks = tm // L
    n_pairs = W // LANES
    hd = RWKV_HEAD
    bf = lambda t: t.astype(BF16)

    @pl.when(pl.program_id(1) == 0)
    def _():
        s_ref[...] = jnp.zeros_like(s_ref)

    row = lax.broadcasted_iota(I32, (tm, tm), 0)
    col = lax.broadcasted_iota(I32, (tm, tm), 1)
    tri = jnp.where(jnp.logical_and(col <= row, (col // L) == (row // L)), 1.0, 0.0).astype(BF16)
    ld = ld_ref[...]
    ld_hi = bf(ld)
    rem = ld - ld_hi.astype(F32)
    ld_mid = bf(rem)
    ld_lo = bf(rem - ld_mid.astype(F32))
    c = (jnp.dot(tri, ld_hi, preferred_element_type=F32) + jnp.dot(tri, ld_mid, preferred_element_type=F32)
         + jnp.dot(tri, ld_lo, preferred_element_type=F32))
    ec = jnp.exp(c)
    eci = jnp.exp(-c)
    ecm = jnp.exp(c - ld)
    r = r_ref[...]
    k = k_ref[...]
    v = v_ref[...]
    a = a_ref[...]
    kkr = k * kk_ref[...]
    kmod = k * (1.0 + (a - 1.0) * ka_ref[...])
    brk = r * kmod * rk_ref[...]

    lane = lax.broadcasted_iota(I32, (L, LANES), 1)
    rowl = lax.broadcasted_iota(I32, (L, LANES), 0)
    first = lane < hd
    lane_h = lane & (hd - 1)
    strict = lane_h < rowl
    incl = lane_h <= rowl
    eye = jnp.where(lane_h == rowl, 1.0, 0.0)

    chains = [(ch, p) for ch in range(n_chunks) for p in range(n_pairs)]
    rsl = lambda ch: slice(ch * L, (ch + 1) * L)
    csl = lambda p: slice(p * LANES, (p + 1) * LANES)
    fdot = lambda x, y: jnp.dot(x, y, preferred_element_type=F32)
    at, bt, kt, rt, vh, g_l = {}, {}, {}, {}, {}, {}
    for c_ in chains:
        ch, p = c_
        rs, cs = rsl(ch), csl(p)
        kkh = kkr[rs, cs]
        kkh = kkh / jnp.maximum(jnp.sqrt(_pair_sum(kkh * kkh, first)), 1e-12)
        vh[c_] = v[rs, cs]
        g_l[c_] = ec[ch * L + L - 1:ch * L + L, cs]
        at[c_] = -kkh * ecm[rs, cs]
        bt[c_] = kkh * a[rs, cs] * eci[rs, cs]
        kt[c_] = kmod[rs, cs] * eci[rs, cs]
        rt[c_] = r[rs, cs] * ec[rs, cs]
    gm = {c_: _nt(bf(jnp.concatenate([at[c_], rt[c_]], axis=0)),
                  jnp.concatenate([_stackmask(bf(bt[c_])), _stackmask(bf(kt[c_]))], axis=0)) for c_ in chains}
    a_ab = {c_: jnp.where(strict, gm[c_][:L, :LANES], 0.0) for c_ in chains}
    vsm = {c_: _stackmask(bf(vh[c_])) for c_ in chains}
    cmat = {c_: fdot(bf(jnp.where(strict, gm[c_][:L, LANES:], 0.0)), vsm[c_]) for c_ in chains}
    t_inv = {c_: eye + a_ab[c_] for c_ in chains}
    pw = {c_: bf(a_ab[c_]) for c_ in chains}
    for _ in range(int(math.log2(L)) - 1):
        pw = {c_: bf(fdot(pw[c_], _stackmask(pw[c_]))) for c_ in chains}
        t_inv = {c_: t_inv[c_] + fdot(pw[c_], _stackmask(bf(t_inv[c_]))) for c_ in chains}
    zz = {c_: fdot(bf(t_inv[c_]), jnp.concatenate([_stackmask(bf(at[c_])), _stackmask(bf(cmat[c_]))], axis=1))
          for c_ in chains}
    qy = {c_: fdot(bf(jnp.where(incl, gm[c_][L:, :LANES], 0.0)),
                   jnp.concatenate([_stackmask(bf(zz[c_][:, :LANES])), _stackmask(bf(zz[c_][:, LANES:]))], axis=1))
          for c_ in chains}
    y0 = {c_: qy[c_][:, LANES:] + fdot(bf(jnp.where(incl, gm[c_][L:, LANES:], 0.0)), vsm[c_]) for c_ in chains}
    qa = {c_: bf(jnp.concatenate([rt[c_] + qy[c_][:, :LANES], zz[c_][:, :LANES]], axis=0)) for c_ in chains}
    bkg = {c_: bf(jnp.concatenate([bt[c_] * g_l[c_], kt[c_] * g_l[c_]], axis=0)) for c_ in chains}

    lane_s = lax.broadcasted_iota(I32, (hd, LANES), 1)
    sp = [s_ref[p] for p in range(n_pairs)]
    for ch in range(n_chunks):
        rs = rsl(ch)
        yw = [_nt(qa[ch, p], _stackmask(bf(sp[p]))) for p in range(n_pairs)]
        upd = [_tn(bf(jnp.concatenate([yw[p][L:] + zz[ch, p][:, LANES:], vh[ch, p]], axis=0)), bkg[ch, p])
               for p in range(n_pairs)]
        for p in range(n_pairs):
            cs = csl(p)
            sp[p] = sp[p] * g_l[ch, p] + jnp.where(lane_s < hd, upd[p][:hd], upd[p][hd:])
            y = yw[p][:L] + y0[ch, p]
            mean = _pair_sum(y, first) * (1.0 / hd)
            yc = y - mean
            var = _pair_sum(yc * yc, first) * (1.0 / hd)
            yn = yc * lax.rsqrt(var + GN_EPS) * lnw_ref[:, cs] + lnb_ref[:, cs]
            bonus = _pair_sum(brk[rs, cs], first) * vh[ch, p]
            o_ref[rs, cs] = ((yn + bonus) * g_ref[rs, cs]).astype(o_ref.dtype)
    for p in range(n_pairs):
        s_ref[p] = sp[p]


def _rwkv_scan(r, k, v, a, ld, g, k_k, k_a, r_k, ln_w, ln_b, L=RWKV_CHUNK, tm=512):
    B, S, W = r.shape
    seq = pl.BlockSpec((None, tm, W), lambda b, c: (b, c, 0))
    vec = pl.BlockSpec((1, W), lambda b, c: (0, 0))
    return pl.pallas_call(
        functools.partial(_rwkv_scan_kernel, L=L),
        grid=(B, S // tm),
        in_specs=[seq] * 6 + [vec] * 5,
        out_specs=seq,
        out_shape=jax.ShapeDtypeStruct((B, S, W), BF16),
        scratch_shapes=[pltpu.VMEM((W // LANES, RWKV_HEAD, LANES), F32)],
        compiler_params=_cp("parallel", "arbitrary"),
        name="rwkv_scan",
    )(r, k, v, a, ld, g, k_k.reshape(1, W), k_a.reshape(1, W), r_k.reshape(1, W),
      ln_w.reshape(1, W), ln_b.reshape(1, W))


def _post_kernel(attn_ref, rw_ref, gate_ref, x_ref, mod_ref, wa_ref, wb_ref, wo_ref, g2_ref, wrh_ref, wrm_ref,
                 sug_ref, sd_ref, base_ref, h2p_ref, lg_ref):
    D = x_ref.shape[1]
    ya = jnp.dot(attn_ref[...], wa_ref[...], preferred_element_type=F32)
    yb = jnp.dot(rw_ref[...], wb_ref[...], preferred_element_type=F32)
    m = gate_ref[:, 0:D] * ya + gate_ref[:, D:] * yb
    x1 = x_ref[...] + mod_ref[2:3, :] * jnp.dot(m.astype(BF16), wo_ref[...], preferred_element_type=F32)
    y = x1 * lax.rsqrt(jnp.mean(x1 * x1, axis=-1, keepdims=True) + NORM_EPS) * g2_ref[...]
    h2 = y * (1.0 + mod_ref[4:5, :]) + mod_ref[3:4, :]
    hb = h2.astype(BF16)
    hm = (h2 - hb.astype(F32)).astype(BF16)
    lg_ref[...] = _nt(wrh_ref[...], hb) + _nt(wrh_ref[...], hm) + _nt(wrm_ref[...], hb)
    packed = _pack_halves(hb.astype(F32))
    half = packed.shape[1] // 2
    h2p_ref[0] = packed[:, :half]
    h2p_ref[1] = packed[:, half:]
    F = sd_ref.shape[0]
    gu = jnp.dot(hb, sug_ref[...], preferred_element_type=F32)
    shared = jnp.dot((_silu(gu[:, :F]) * gu[:, F:]).astype(BF16), sd_ref[...], preferred_element_type=F32)
    base_ref[...] = x1 + mod_ref[5:6, :] * shared


def _post(attn, rw, gate, x2, mod, wa, wb, wo, norm2_g, w_router_t, sug, sd, S, tm=512):
    N, D = x2.shape
    E = w_router_t.shape[0]
    wr_hi = w_router_t.astype(BF16)
    wr_mid = (w_router_t - wr_hi.astype(F32)).astype(BF16)
    tpb = S // tm
    row = lambda w: pl.BlockSpec((tm, w), lambda i: (i, 0))
    full = lambda a: pl.BlockSpec(a.shape, lambda i: (0, 0))
    return pl.pallas_call(
        _post_kernel,
        grid=(N // tm,),
        in_specs=[row(attn.shape[1]), row(rw.shape[1]), row(gate.shape[1]), row(D),
                  pl.BlockSpec((None, 6, D), lambda i: (i // tpb, 0, 0)),
                  full(wa), full(wb), full(wo), pl.BlockSpec((1, D), lambda i: (0, 0)), full(wr_hi), full(wr_mid),
                  full(sug), full(sd)],
        out_specs=[row(D), pl.BlockSpec((2, tm, D // 4), lambda i: (0, i, 0)), pl.BlockSpec((E, tm), lambda i: (0, i))],
        out_shape=[jax.ShapeDtypeStruct((N, D), F32), jax.ShapeDtypeStruct((2, N, D // 4), U32),
                   jax.ShapeDtypeStruct((E, N), F32)],
        compiler_params=_cp("parallel"),
        name="post_mixer",
    )(attn, rw, gate, x2, mod, wa, wb, wo, norm2_g.reshape(1, D), wr_hi, wr_mid, sug, sd)


def _first_argmax(x, idx, sentinel):
    m = jnp.max(x, axis=0, keepdims=True)
    return m, jnp.min(jnp.where(x == m, idx, sentinel), axis=0, keepdims=True)


def _route_kernel(lg_ref, bias_ref, e_ref, w_ref):
    E, T = lg_ref.shape
    gsz = E // N_GROUPS
    scores = _sigmoid(lg_ref[...])
    biased = scores + bias_ref[...]
    ig = lax.broadcasted_iota(I32, (gsz, T), 0)
    gs = []
    for g in range(N_GROUPS):
        blk = biased[g * gsz:(g + 1) * gsz, :]
        m1, i1 = _first_argmax(blk, ig, gsz)
        m2 = jnp.max(jnp.where(ig == i1, -jnp.inf, blk), axis=0, keepdims=True)
        gs.append(m1 + m2)
    gsc = jnp.concatenate(gs, axis=0)
    i8 = lax.broadcasted_iota(I32, (N_GROUPS, T), 0)
    chosen = jnp.zeros((N_GROUPS, T), F32)
    for _ in range(TOPK_GROUPS):
        _, gi = _first_argmax(gsc, i8, N_GROUPS)
        hit = i8 == gi
        chosen = jnp.where(hit, 1.0, chosen)
        gsc = jnp.where(hit, -jnp.inf, gsc)
    masked = jnp.concatenate(
        [jnp.where(chosen[g:g + 1, :] > 0.0, biased[g * gsz:(g + 1) * gsz, :], -jnp.inf)
         for g in range(N_GROUPS)], axis=0)
    ie = lax.broadcasted_iota(I32, (E, T), 0)
    idxs, wts = [], []
    for _ in range(TOP_K):
        _, ei = _first_argmax(masked, ie, E)
        hit = ie == ei
        idxs.append(ei)
        wts.append(jnp.sum(jnp.where(hit, scores, 0.0), axis=0, keepdims=True))
        masked = jnp.where(hit, -jnp.inf, masked)
    wt = jnp.concatenate(wts, axis=0)
    e_ref[...] = jnp.concatenate(idxs, axis=0)
    w_ref[...] = wt / jnp.sum(wt, axis=0, keepdims=True) * ROUTED_SCALE


def _route(logits_t, router_bias, T=512):
    E, N = logits_t.shape
    blk = pl.BlockSpec((TOP_K, T), lambda i: (0, i))
    return pl.pallas_call(
        _route_kernel,
        grid=(N // T,),
        in_specs=[pl.BlockSpec((E, T), lambda i: (0, i)), pl.BlockSpec((E, 1), lambda i: (0, 0))],
        out_specs=[blk, blk],
        out_shape=[jax.ShapeDtypeStruct((TOP_K, N), I32), jax.ShapeDtypeStruct((TOP_K, N), F32)],
        compiler_params=_cp("parallel"),
        name="route",
    )(logits_t, router_bias.reshape(E, 1))


def _rank_kernel(e_ref, rank_ref, cnt_ref, carry_ref, *, n_experts):
    T = e_ref.shape[1]

    @pl.when(pl.program_id(0) == 0)
    def _():
        carry_ref[...] = jnp.zeros_like(carry_ref)

    ie = lax.broadcasted_iota(I32, (n_experts, T), 0)
    e = e_ref[...]
    hits = [ie == e[kk:kk + 1, :] for kk in range(TOP_K)]
    onehot = jnp.zeros((n_experts, T), F32)
    for hsel in hits:
        onehot = onehot + hsel.astype(F32)
    tr = lax.broadcasted_iota(I32, (T, T), 0)
    tc = lax.broadcasted_iota(I32, (T, T), 1)
    before = (tr < tc).astype(BF16)
    base = _bdot(onehot, before) + carry_ref[:, 0:1]
    rank_ref[...] = jnp.concatenate(
        [jnp.sum(jnp.where(hsel, base, 0.0), axis=0, keepdims=True) for hsel in hits], axis=0).astype(I32)
    carry_ref[...] = carry_ref[...] + jnp.sum(onehot, axis=1, keepdims=True)
    cnt_ref[...] = carry_ref[...]


def _ranks(eidx_t, n_experts, T=512):
    N = eidx_t.shape[1]
    T = min(T, N)
    blk = pl.BlockSpec((TOP_K, T), lambda i: (0, i))
    cnt = pl.BlockSpec((n_experts, LANES), lambda i: (0, 0))
    return pl.pallas_call(
        functools.partial(_rank_kernel, n_experts=n_experts),
        grid=(N // T,),
        in_specs=[blk],
        out_specs=[blk, cnt],
        out_shape=[jax.ShapeDtypeStruct((TOP_K, N), I32), jax.ShapeDtypeStruct((n_experts, LANES), F32)],
        scratch_shapes=[pltpu.VMEM((n_experts, LANES), F32)],
        compiler_params=_cp("arbitrary"),
        name="ranks",
    )(eidx_t)


def _dest_kernel(e_ref, rank_ref, start_ref, d_ref):
    E = start_ref.shape[0]
    T = e_ref.shape[1]
    ie = lax.broadcasted_iota(I32, (E, T), 0)
    e = e_ref[...]
    start = start_ref[:, 0:1]
    rows = [jnp.sum(jnp.where(ie == e[kk:kk + 1, :], start, 0.0), axis=0, keepdims=True) for kk in range(TOP_K)]
    d_ref[...] = jnp.concatenate(rows, axis=0).astype(I32) + rank_ref[...]


def _dests(eidx_t, rank_t, pstart, T=2048):
    N = eidx_t.shape[1]
    T = min(T, N)
    E = pstart.shape[0]
    blk = pl.BlockSpec((TOP_K, T), lambda i: (0, i))
    return pl.pallas_call(
        _dest_kernel,
        grid=(N // T,),
        in_specs=[blk, blk, pl.BlockSpec((E, LANES), lambda i: (0, 0))],
        out_specs=blk,
        out_shape=jax.ShapeDtypeStruct((TOP_K, N), I32),
        compiler_params=_cp("parallel"),
        name="dests",
    )(eidx_t, rank_t, jnp.broadcast_to(pstart.astype(F32)[:, None], (E, LANES)))


def _expert_kernel(us_ref, ps_ref, x_hbm, wug_hbm, wd_hbm, y_hbm,
                   wug_f32, wd_f32, wug_bf, wd_bf, xbuf, ybuf, cnt_ref, wsem, xsem, ysem, *, n_rows):
    e = pl.program_id(0)
    n_experts = pl.num_programs(0)
    _, _, R, Ch = xbuf.shape
    C = 2 * Ch
    F = wd_hbm.shape[1]

    def w_copies(expert, slot):
        return (pltpu.make_async_copy(wug_hbm.at[expert], wug_f32.at[slot], wsem.at[slot]),
                pltpu.make_async_copy(wd_hbm.at[expert], wd_f32.at[slot], wsem.at[slot]))

    def x_copy(slot, half, row):
        return pltpu.make_async_copy(x_hbm.at[half, pl.ds(row, R)], xbuf.at[slot, half], xsem.at[slot])

    def y_copy(slot, half, row):
        return pltpu.make_async_copy(ybuf.at[slot, half], y_hbm.at[half, pl.ds(row, R)], ysem.at[slot])

    def start(copy, slot, row):
        for half in range(2):
            copy(slot, half, row).start()

    def wait(copy, slot):
        for half in range(2):
            copy(slot, half, 0).wait()

    @pl.when(e == 0)
    def _():
        for ahead in range(W_SLOTS - 1):
            for cp in w_copies(ahead, ahead):
                cp.start()
        cnt_ref[0] = 0
        for t in range(X_AHEAD):
            start(x_copy, t, t * R)
        ybuf[...] = jnp.zeros_like(ybuf)
        for slot in range(2):
            start(y_copy, slot, n_rows + slot * R)

    @pl.when(e + (W_SLOTS - 1) < n_experts)
    def _():
        for cp in w_copies(e + (W_SLOTS - 1), lax.rem(e + (W_SLOTS - 1), W_SLOTS)):
            cp.start()

    wslot = lax.rem(e, W_SLOTS)
    for cp in w_copies(e, wslot):
        cp.wait()
    wug_bf[...] = wug_f32[wslot].astype(BF16)
    wd_bf[...] = wd_f32[wslot].astype(BF16)
    n_valid = us_ref[e + 1] - us_ref[e]
    pbase = ps_ref[e]
    n_tiles = lax.shift_right_logical(n_valid + (R - 1), int(math.log2(R)))

    def make_tile(nb):
        rowid = lax.broadcasted_iota(I32, (nb * R, C), 0)

        def tile(blk0):
            n_done = cnt_ref[0]
            xslots = [lax.rem(n_done + b, X_SLOTS) for b in range(nb)]
            for b in range(nb):
                ahead = n_done + X_AHEAD + b
                start(x_copy, lax.rem(ahead, X_SLOTS), pl.multiple_of(ahead * R, R))
            for b in range(nb):
                wait(x_copy, xslots[b])
            x = jnp.concatenate([jnp.concatenate([xbuf[xs, 0], xbuf[xs, 1]], axis=1) for xs in xslots], axis=0)
            x = jnp.where(rowid < n_valid - blk0 * R, x, jnp.uint32(0))
            lo, hi = _unpack_halves(x)
            gu = (jnp.dot(lo.astype(BF16), wug_bf[0:C, :], preferred_element_type=F32)
                  + jnp.dot(hi.astype(BF16), wug_bf[C:, :], preferred_element_type=F32))
            hid = (_silu(gu[:, :F]) * gu[:, F:]).astype(BF16)
            y = jnp.dot(hid, wd_bf[...], preferred_element_type=F32)
            packed = _pack_halves(y.astype(BF16).astype(F32))
            for b in range(nb):
                slot = (n_done + b) & 1
                wait(y_copy, slot)
                ybuf[slot, 0] = packed[b * R:(b + 1) * R, :Ch]
                ybuf[slot, 1] = packed[b * R:(b + 1) * R, Ch:]
                start(y_copy, slot, pl.multiple_of(pbase + (blk0 + b) * R, R))
            cnt_ref[0] = n_done + nb

        return tile

    pair_tile, single_tile = make_tile(2), make_tile(1)

    def pairs(j, carry):
        pair_tile(2 * j)
        return carry

    lax.fori_loop(0, lax.shift_right_logical(n_tiles, 1), pairs, 0)

    @pl.when((n_tiles & 1) == 1)
    def _():
        single_tile(n_tiles - 1)

    @pl.when(e == pl.num_programs(0) - 1)
    def _():
        for t in range(X_AHEAD):
            wait(x_copy, lax.rem(cnt_ref[0] + t, X_SLOTS))
        for slot in range(2):
            wait(y_copy, slot)
        ybuf[0] = jnp.zeros((2, R, Ch), U32)
        first = lax.shift_right_logical(pbase + n_tiles * R, int(math.log2(R)))
        n_left = n_rows // R - first

        def fill(t, carry):
            start(y_copy, 0, pl.multiple_of((first + t) * R, R))
            return carry

        def drain(t, carry):
            wait(y_copy, 0)
            return carry

        lax.fori_loop(0, n_left, fill, 0)
        lax.fori_loop(0, n_left, drain, 0)


def _experts(ustart, pstart, xg, w_ug, w_d, n_rows, R):
    _, _, Ch = xg.shape
    E, D, F2 = w_ug.shape
    F = w_d.shape[1]
    grid_spec = pltpu.PrefetchScalarGridSpec(
        num_scalar_prefetch=2,
        grid=(E,),
        in_specs=[pl.BlockSpec(memory_space=pl.ANY)] * 3,
        out_specs=pl.BlockSpec(memory_space=pl.ANY),
        scratch_shapes=[pltpu.VMEM((W_SLOTS, D, F2), F32), pltpu.VMEM((W_SLOTS, F, D), F32),
                        pltpu.VMEM((D, F2), BF16), pltpu.VMEM((F, D), BF16),
                        pltpu.VMEM((X_SLOTS, 2, R, Ch), U32), pltpu.VMEM((2, 2, R, Ch), U32), pltpu.SMEM((1,), I32),
                        pltpu.SemaphoreType.DMA((W_SLOTS,)), pltpu.SemaphoreType.DMA((X_SLOTS,)),
                        pltpu.SemaphoreType.DMA((2,))],
    )
    return pl.pallas_call(
        functools.partial(_expert_kernel, n_rows=n_rows),
        grid_spec=grid_spec,
        out_shape=jax.ShapeDtypeStruct((2, n_rows + 2 * R, Ch), U32),
        compiler_params=_cp("arbitrary"),
        name="experts",
    )(ustart, pstart, xg, w_ug, w_d)


def _sc_scatter_rows(src, idx, n_rows):
    H, N, C = src.shape
    K = idx.shape[0]
    per_row = N // SC_WINDOW
    mesh = plsc.VectorSubcoreMesh(core_axis_name="c", subcore_axis_name="s")

    @functools.partial(pl.kernel, out_type=jax.ShapeDtypeStruct((H, n_rows, C), src.dtype), mesh=mesh,
                       scratch_types=[])
    def scatter_kernel(x_hbm, i_hbm, o_hbm):
        for h in range(H):
            def body(x_vmem, i_vmem):
                for k in range(K):
                    pltpu.sync_copy(x_vmem, o_hbm.at[h].at[i_vmem.at[k]])

            pltpu.emit_pipeline(
                body,
                grid=(per_row,),
                in_specs=[pl.BlockSpec((SC_WINDOW, C), lambda i: (i, 0)),
                          pl.BlockSpec((K, SC_WINDOW), lambda i: (0, i))],
                out_specs=[],
                core_axis_name=("c", "s"),
                dimension_semantics=(pltpu.PARALLEL,),
            )(x_hbm.at[h], i_hbm)

    return scatter_kernel(src, idx)


def _sc_gather_rows(src, idx):
    H, _, C = src.shape
    K, N = idx.shape
    per_row = N // SC_WINDOW
    mesh = plsc.VectorSubcoreMesh(core_axis_name="c", subcore_axis_name="s")

    @functools.partial(pl.kernel, out_type=jax.ShapeDtypeStruct((H, K * N, C), src.dtype), mesh=mesh,
                       scratch_types=[])
    def gather_kernel(x_hbm, i_hbm, o_hbm):
        for h in range(H):
            def body(i_vmem, o_vmem):
                pltpu.sync_copy(x_hbm.at[h].at[i_vmem.at[0]], o_vmem)

            pltpu.emit_pipeline(
                body,
                grid=(K * per_row,),
                in_specs=[pl.BlockSpec((1, SC_WINDOW), lambda i: (i // per_row, i % per_row))],
                out_specs=[pl.BlockSpec((SC_WINDOW, C), lambda i: (i, 0))],
                core_axis_name=("c", "s"),
                dimension_semantics=(pltpu.PARALLEL,),
            )(i_hbm, o_hbm.at[h])

    return gather_kernel(src, idx)


def _combine_kernel(*refs):
    y_refs = refs[:2 * TOP_K]
    w_ref, base_ref, mod_ref, o_ref = refs[2 * TOP_K:]
    T = base_ref.shape[0]
    tr = lax.broadcasted_iota(I32, (T, T), 0)
    tc = lax.broadcasted_iota(I32, (T, T), 1)
    wcol = _nt((tr == tc).astype(F32), w_ref[...], precision=HI)
    acc = [None] * 4
    for kk in range(TOP_K):
        wk = wcol[:, kk:kk + 1]
        for half in range(2):
            lo, hi = _unpack_halves(y_refs[2 * kk + half][...])
            for q, val in ((half, lo), (2 + half, hi)):
                acc[q] = val * wk if acc[q] is None else acc[q] + val * wk
    o_ref[...] = base_ref[...] + mod_ref[5:6, :] * jnp.concatenate(acc, axis=1)


def _combine(yg, w_t, base, mod, S, T=512):
    N, D = base.shape
    C = yg.shape[2]
    tpb = S // T
    n_tiles = N // T
    row = pl.BlockSpec((T, D), lambda i: (i, 0))
    piece = lambda kk, half: pl.BlockSpec((None, T, C), lambda i: (half, kk * n_tiles + i, 0))
    return pl.pallas_call(
        _combine_kernel,
        grid=(n_tiles,),
        in_specs=[piece(kk, half) for kk in range(TOP_K) for half in range(2)] + [
            pl.BlockSpec((TOP_K, T), lambda i: (0, i)),
            row,
            pl.BlockSpec((None, 6, D), lambda i: (i // tpb, 0, 0))],
        out_specs=row,
        out_shape=jax.ShapeDtypeStruct((N, D), F32),
        compiler_params=_cp("parallel"),
        name="combine",
    )(*([yg] * (2 * TOP_K)), w_t, base, mod)


def _layer(x, c, positions, layer_idx, w_ada, b_ada, norm1_g, w_in, w_gate, b_gate,
           q_norm_g, k_norm_g, lambda_q1, lambda_k1, lambda_q2, lambda_k2, subln_g,
           rwkv_mu, w_decay0, w_decay2, a0, a2, g2, k_k, k_a, r_k, ln_x_w, ln_x_b,
           w_branch_a, w_branch_b, w_out, norm2_g, w_router, router_bias,
           w_expert_up_gate, w_expert_down, w_shared_up_gate, w_shared_down):
    B, S, D = x.shape
    N = B * S
    E = w_router.shape[1]
    da_width = w_branch_a.shape[0]
    rw_width = w_branch_b.shape[0]
    lambda_init = 0.8 - 0.6 * math.exp(-0.3 * layer_idx)

    mod = _adaln(c, w_ada, b_ada)
    x2 = x.reshape(N, D)
    w_cat = jnp.concatenate([w_in, w_gate], axis=1).astype(BF16)
    qn, kn, v, r_, k_, v_, a_, ld_, g_, gate = _mixer_in(
        x2, positions.reshape(N, 1), mod, norm1_g, w_cat, b_gate, q_norm_g, k_norm_g,
        rwkv_mu, w_decay0, w_decay2, a0, a2, g2, S, da_width, rw_width)

    lam_vecs = jnp.stack([lambda_q1, lambda_k1, lambda_q2, lambda_k2])
    score_bound = 1.01 * DA_HEAD_DIM ** 0.5 * jnp.max(jnp.abs(q_norm_g)) * jnp.max(jnp.abs(k_norm_g))
    attn = _diff_attention(qn, kn, v, score_bound, lam_vecs, subln_g, B, S, lambda_init)

    seq = lambda t: t.reshape(B, S, rw_width)
    rw = _rwkv_scan(seq(r_), seq(k_), seq(v_), seq(a_), seq(ld_), seq(g_), k_k, k_a, r_k.reshape(-1),
                    ln_x_w, ln_x_b).reshape(N, rw_width)

    base, h2p, logits_t = _post(attn, rw, gate, x2, mod, w_branch_a.astype(BF16), w_branch_b.astype(BF16),
                                w_out.astype(BF16), norm2_g, w_router.T,
                                w_shared_up_gate.astype(BF16), w_shared_down.astype(BF16), S)

    eidx_t, w_t = _route(logits_t, router_bias)
    rank_t, counts = _ranks(eidx_t, E)
    R = EXPERT_TILE
    cnt = counts[:, 0].astype(I32)
    ustart = jnp.concatenate([jnp.zeros((1,), I32), jnp.cumsum(cnt)])
    pcnt = (cnt + R - 1) // R * R
    pstart = jnp.cumsum(pcnt) - pcnt
    dest_p = _dests(eidx_t, rank_t, pstart)
    n_rows = (N * TOP_K + E * (R - 1) + R - 1) // R * R
    xg = _sc_scatter_rows(h2p, dest_p, n_rows + X_AHEAD * R)
    y = _experts(ustart, pstart, xg, w_expert_up_gate, w_expert_down, n_rows, R)
    yg = _sc_gather_rows(y, dest_p)
    out = _combine(yg, w_t, base, mod, S)
    return out.reshape(B, S, D)


def kernel(x, c, positions, w_ada, b_ada, norm1_g, w_in, w_gate, b_gate, q_norm_g, k_norm_g, lambda_q1, lambda_k1, lambda_q2, lambda_k2, subln_g, rwkv_mu, w_decay0, w_decay2, a0, a2, g2, k_k, k_a, r_k, ln_x_w, ln_x_b, w_branch_a, w_branch_b, w_out, norm2_g, w_router, router_bias, w_expert_up_gate, w_expert_down, w_shared_up_gate, w_shared_down):
    for l in range(w_ada.shape[0]):
        x = _layer(x, c, positions, l, w_ada[l], b_ada[l], norm1_g[l], w_in[l], w_gate[l], b_gate[l],
                   q_norm_g[l], k_norm_g[l], lambda_q1[l], lambda_k1[l], lambda_q2[l], lambda_k2[l],
                   subln_g[l], rwkv_mu[l], w_decay0[l], w_decay2[l], a0[l], a2[l], g2[l], k_k[l],
                   k_a[l], r_k[l], ln_x_w[l], ln_x_b[l], w_branch_a[l], w_branch_b[l], w_out[l],
                   norm2_g[l], w_router[l], router_bias[l], w_expert_up_gate[l], w_expert_down[l],
                   w_shared_up_gate[l], w_shared_down[l])
    return x
```

```python
import functools
import math

import jax
import jax.numpy as jnp
from jax import lax
from jax.experimental import pallas as pl
from jax.experimental.pallas import tpu as pltpu
from jax.experimental.pallas import tpu_sc as plsc

F32 = jnp.float32
BF16 = jnp.bfloat16
I32 = jnp.int32
U32 = jnp.uint32
HI = lax.Precision.HIGHEST

CHUNK = 64
ROPE_THETA = 10000.0
NORM_EPS = 1e-6
SUBLN_EPS = 1e-5
DA_HEAD_DIM = 64
RWKV_HEAD = 64
GN_EPS = 64e-5
TOP_K = 8
N_GROUPS = 8
TOPK_GROUPS = 4
ROUTED_SCALE = 2.5
EXPERT_TILE = 256
W_SLOTS = 3
X_SLOTS = 6
X_AHEAD = X_SLOTS - 2
RWKV_CHUNK = 64
LANES = 128
SC_WINDOW = 128
NEG = -1e30
MAX_PLAIN_SCORE = 40.0
VMEM_LIMIT = 56 * 1024 * 1024


def _cp(*sem):
    return pltpu.CompilerParams(dimension_semantics=sem, vmem_limit_bytes=VMEM_LIMIT)


def _bdot(a, b):
    return jnp.dot(a.astype(BF16), b.astype(BF16), preferred_element_type=F32)


def _fdot(a, b):
    return jnp.dot(a, b, precision=HI, preferred_element_type=F32)


def _nt(a, b, precision=None):
    return lax.dot_general(a, b, (((1,), (1,)), ((), ())), precision=precision,
                           preferred_element_type=F32)


def _tn(a, b, precision=None):
    return lax.dot_general(a, b, (((0,), (0,)), ((), ())), precision=precision,
                           preferred_element_type=F32)


def _pack_halves(x):
    c = x.shape[1] // 2
    lo = lax.bitcast_convert_type(x[:, :c], U32)
    hi = lax.bitcast_convert_type(x[:, c:], U32)
    return (hi & jnp.uint32(0xFFFF0000)) | (lo >> 16)


def _unpack_halves(w):
    lo = lax.bitcast_convert_type(w << 16, F32)
    hi = lax.bitcast_convert_type(w & jnp.uint32(0xFFFF0000), F32)
    return lo, hi


def _sigmoid(x):
    return 1.0 / (1.0 + jnp.exp(-x))


def _silu(x):
    return x * _sigmoid(x)


def _ada_kernel(c_ref, w_ref, b_ref, o_ref):
    o_ref[...] = _fdot(_silu(c_ref[...]), w_ref[...]) + b_ref[...]


def _adaln(c, w_ada, b_ada):
    B, D = c.shape
    rows = -(-B // 8) * 8
    cpad = jnp.zeros((rows, D), F32).at[:B].set(c)
    n_out = w_ada.shape[1]
    out = pl.pallas_call(
        _ada_kernel,
        grid=(n_out // D,),
        in_specs=[pl.BlockSpec((rows, D), lambda j: (0, 0)),
                  pl.BlockSpec((D, D), lambda j: (0, j)),
                  pl.BlockSpec((1, D), lambda j: (0, j))],
        out_specs=pl.BlockSpec((rows, D), lambda j: (0, j)),
        out_shape=jax.ShapeDtypeStruct((rows, n_out), F32),
        compiler_params=_cp("arbitrary"),
        name="adaln",
    )(cpad, w_ada, b_ada.reshape(1, n_out))
    return out[:B].reshape(B, n_out // D, D)


def _mixer_in_kernel(x_ref, xprev_ref, mod_ref, g_ref, w_ref, bg_ref, pos_ref, invf_ref, qg_ref, kg_ref,
                     mu_ref, w0_ref, w2_ref, a0_ref, a2_ref, g2_ref,
                     qn_ref, kn_ref, v_ref, r_ref, k_ref, vr_ref, a_ref, ld_ref, gr_ref, gate_ref,
                     *, da_width, rw_cols, rw_width, q_scale, tiles_per_seq):
    tm = x_ref.shape[0]

    def modulated(x):
        y = x * lax.rsqrt(jnp.mean(x * x, axis=-1, keepdims=True) + NORM_EPS) * g_ref[...]
        return (y * (1.0 + mod_ref[1:2, :]) + mod_ref[0:1, :]).astype(BF16)

    def proj(hb, c0, width, step=512):
        parts = [jnp.dot(hb, w_ref[:, c0 + o:c0 + min(o + step, width)], preferred_element_type=F32)
                 for o in range(0, width, step)]
        return parts[0] if len(parts) == 1 else jnp.concatenate(parts, axis=1)

    h = modulated(x_ref[...])

    lane = lax.broadcasted_iota(I32, (tm, LANES), 1)
    first = lane < DA_HEAD_DIM
    lo_half = (lane & (DA_HEAD_DIM - 1)) < DA_HEAD_DIM // 2
    ang = pos_ref[...].astype(F32) * invf_ref[...]
    cos = jnp.cos(ang)
    sin = jnp.sin(ang)
    sin = jnp.where(lo_half, -sin, sin)
    for c0, dst, gn_ref, mult in ((0, qn_ref, qg_ref, q_scale), (da_width, kn_ref, kg_ref, 1.0)):
        raw = proj(h, c0, da_width)
        for blk in range(da_width // LANES):
            x = raw[:, blk * LANES:(blk + 1) * LANES]
            xx = x * x
            s_first = jnp.sum(jnp.where(first, xx, 0.0), axis=-1, keepdims=True)
            s_second = jnp.sum(jnp.where(first, 0.0, xx), axis=-1, keepdims=True)
            ms = jnp.where(first, s_first, s_second) * (1.0 / DA_HEAD_DIM)
            xn = x * lax.rsqrt(ms + NORM_EPS) * gn_ref[...]
            rot = jnp.where(lo_half, pltpu.roll(xn, LANES - DA_HEAD_DIM // 2, axis=1),
                            pltpu.roll(xn, DA_HEAD_DIM // 2, axis=1))
            dst[:, blk * LANES:(blk + 1) * LANES] = ((xn * cos + rot * sin) * mult).astype(dst.dtype)
    v_ref[...] = proj(h, 2 * da_width, da_width).astype(v_ref.dtype)

    c_rw = 3 * da_width
    p = proj(h, c_rw, rw_cols)
    p_before = proj(modulated(xprev_ref[...]), c_rw, rw_cols)
    seq_start = (pl.program_id(0) % tiles_per_seq) == 0
    last_prev = jnp.where(seq_start, 0.0, p_before[7:8, :])
    rowi = lax.broadcasted_iota(I32, p.shape, 0)
    prev = jnp.where(rowi == 0, last_prev, pltpu.roll(p, 1, axis=0))
    xs = p + (prev - p) * mu_ref[...]
    width = rw_width
    r_ref[...] = xs[:, 0:width]
    k_ref[...] = xs[:, width:2 * width]
    vr_ref[...] = xs[:, 2 * width:3 * width]
    xwa = xs[:, 3 * width:3 * width + LANES]
    xg = xs[:, 3 * width + LANES:]
    z = w0_ref[...] + _bdot(jnp.tanh(xwa), w2_ref[...])
    w = -(jnp.maximum(-z, 0.0) + jnp.log(1.0 + jnp.exp(-jnp.abs(z)))) - 0.5
    ld_ref[...] = -jnp.exp(w)
    a_ref[...] = _sigmoid(a0_ref[...] + _bdot(xwa, a2_ref[...]))
    gr_ref[...] = _bdot(_sigmoid(xg), g2_ref[...])

    gate_ref[...] = _sigmoid(proj(h, c_rw + rw_cols, gate_ref.shape[1]) + bg_ref[...]).astype(gate_ref.dtype)


def _mixer_in(x2, pos2, mod, norm1_g, w_cat, b_gate, q_norm_g, k_norm_g, mu, w_decay0, w_decay2, a0, a2, g2,
              S, da_width, rw_width, tm=512):
    N, D = x2.shape
    n_gate = b_gate.shape[0]
    rw_cols = mu.shape[0]
    tpb = S // tm
    d = DA_HEAD_DIM
    inv_freq = 1.0 / (ROPE_THETA ** (jnp.arange(0, d, 2, dtype=F32) / d))
    invf = jnp.tile(inv_freq, LANES // (d // 2)).reshape(1, LANES)
    dl, al = w_decay2.shape[0], a2.shape[0]
    assert dl + al == LANES and g2.shape[0] == LANES
    w2p = jnp.zeros((LANES, rw_width), F32).at[:dl].set(w_decay2)
    a2p = jnp.zeros((LANES, rw_width), F32).at[dl:].set(a2)
    kern = functools.partial(_mixer_in_kernel, da_width=da_width, rw_cols=rw_cols, rw_width=rw_width,
                             q_scale=d ** -0.5 * math.log2(math.e),
                             tiles_per_seq=tpb)
    row = lambda w: pl.BlockSpec((tm, w), lambda i: (i, 0))
    vec = lambda n: pl.BlockSpec((1, n), lambda i: (0, 0))
    mat = pl.BlockSpec((LANES, rw_width), lambda i: (0, 0))
    f32 = lambda w: jax.ShapeDtypeStruct((N, w), F32)
    bf16 = lambda w: jax.ShapeDtypeStruct((N, w), BF16)
    return pl.pallas_call(
        kern,
        grid=(N // tm,),
        in_specs=[row(D),
                  pl.BlockSpec((8, D), lambda i: (jnp.maximum(i * (tm // 8) - 1, 0), 0)),
                  pl.BlockSpec((None, 6, D), lambda i: (i // tpb, 0, 0)),
                  vec(D),
                  pl.BlockSpec(w_cat.shape, lambda i: (0, 0)),
                  vec(n_gate),
                  pl.BlockSpec((tm, 1), lambda i: (i, 0)),
                  vec(LANES), vec(LANES), vec(LANES),
                  vec(rw_cols), vec(rw_width), mat, vec(rw_width), mat, mat],
        out_specs=[row(da_width)] * 3 + [row(rw_width)] * 6 + [row(n_gate)],
        out_shape=[bf16(da_width)] * 3 + [f32(rw_width)] * 6 + [bf16(n_gate)],
        compiler_params=_cp("parallel"),
        name="mixer_in",
    )(x2, x2, mod, norm1_g.reshape(1, D), w_cat, b_gate.reshape(1, n_gate), pos2, invf,
      jnp.tile(q_norm_g, 2).reshape(1, LANES), jnp.tile(k_norm_g, 2).reshape(1, LANES),
      mu.reshape(1, rw_cols), w_decay0.reshape(1, rw_width), w2p, a0.reshape(1, rw_width), a2p, g2)


def _attn_kernel(flag_ref, q_ref, k_ref, v_ref, vt_ref, lam_ref, sg_ref, sgc_ref, o_ref,
                 qz_ref, m_ref, l_ref, acc_ref, lpt_ref, acct_ref, *, tq, lambda_init):
    i = pl.program_id(2)
    lane = lax.broadcasted_iota(I32, (tq, LANES), 1)
    q = q_ref[...]
    zero = jnp.zeros_like(q)
    qz_ref[0:tq, :] = jnp.where(lane < DA_HEAD_DIM, q, zero)
    qz_ref[tq:, :] = jnp.where(lane >= DA_HEAD_DIM, q, zero)
    bounded = flag_ref[0] == 1
    lv = lam_ref[...]
    lam = (jnp.exp(jnp.sum(lv[0:1] * lv[1:2], keepdims=True))
           - jnp.exp(jnp.sum(lv[2:3] * lv[3:4], keepdims=True)) + lambda_init)

    def run(step):
        def body(j, carry):
            step(j, False)
            return carry
        lax.fori_loop(0, i, body, 0)
        step(i, True)

    def plain_step(j, masked):
        off = pl.multiple_of(j * tq, tq)
        st = _nt(k_ref[pl.ds(off, tq), :], qz_ref[...])
        if masked:
            row = lax.broadcasted_iota(I32, st.shape, 0)
            col = lax.broadcasted_iota(I32, st.shape, 1)
            st = jnp.where((row // CHUNK) <= ((col & (tq - 1)) // CHUNK), st, NEG)
        pt = jnp.exp2(st)
        part = pt[0:8, :]
        for g in range(1, tq // 8):
            part = part + pt[8 * g:8 * g + 8, :]
        lpt_ref[...] += part
        acct_ref[...] += jnp.dot(vt_ref[:, pl.ds(off, tq)], pt.astype(BF16), preferred_element_type=F32)

    @pl.when(bounded)
    def _():
        lpt_ref[...] = jnp.zeros_like(lpt_ref)
        acct_ref[...] = jnp.zeros_like(acct_ref)

        def two_steps(p, carry):
            plain_step(2 * p, False)
            plain_step(2 * p + 1, False)
            return carry

        lax.fori_loop(0, lax.shift_right_logical(i, 1), two_steps, 0)

        @pl.when((i & 1) == 1)
        def _():
            plain_step(i - 1, False)
            plain_step(i, True)

        @pl.when((i & 1) == 0)
        def _():
            plain_step(i, True)

        lsum = jnp.sum(lpt_ref[...], axis=0, keepdims=True)
        ot = acct_ref[:, 0:tq] / lsum[:, 0:tq] - lam * (acct_ref[:, tq:] / lsum[:, tq:])
        ot = ot * lax.rsqrt(jnp.mean(ot * ot, axis=0, keepdims=True) + SUBLN_EPS) * sgc_ref[...]
        o_ref[...] = (ot * (1.0 - lambda_init)).T.astype(o_ref.dtype)

    def online_step(j, masked):
        off = pl.multiple_of(j * tq, tq)
        s = _nt(qz_ref[...], k_ref[pl.ds(off, tq), :])
        if masked:
            row = lax.broadcasted_iota(I32, s.shape, 0)
            col = lax.broadcasted_iota(I32, s.shape, 1)
            s = jnp.where((col // CHUNK) <= ((row & (tq - 1)) // CHUNK), s, NEG)
        m_old = m_ref[...]
        m_new = jnp.maximum(m_old, jnp.max(s, axis=-1, keepdims=True))
        alpha = jnp.exp2(m_old - m_new)
        pr = jnp.exp2(s - m_new)
        l_ref[...] = alpha * l_ref[...] + jnp.sum(pr, axis=-1, keepdims=True)
        acc_ref[...] = alpha * acc_ref[...] + jnp.dot(pr.astype(BF16), v_ref[pl.ds(off, tq), :],
                                                      preferred_element_type=F32)
        m_ref[...] = m_new

    @pl.when(jnp.logical_not(bounded))
    def _():
        m_ref[...] = jnp.full_like(m_ref, NEG)
        l_ref[...] = jnp.zeros_like(l_ref)
        acc_ref[...] = jnp.zeros_like(acc_ref)
        run(online_step)
        o = acc_ref[0:tq, :] / l_ref[0:tq, :] - lam * (acc_ref[tq:, :] / l_ref[tq:, :])
        o = o * lax.rsqrt(jnp.mean(o * o, axis=-1, keepdims=True) + SUBLN_EPS) * sg_ref[...]
        o_ref[...] = (o * (1.0 - lambda_init)).astype(o_ref.dtype)


def _diff_attention(qn, kn, v, score_bound, lam_vecs, subln_g, B, S, lambda_init, tq=512):
    W = qn.shape[1]
    H = W // LANES
    q3 = qn.reshape(B, S, W)
    k3 = kn.reshape(B, S, W)
    v3 = v.reshape(B, S, W)
    vt3 = v3.transpose(0, 2, 1)
    flag = (score_bound <= MAX_PLAIN_SCORE).astype(I32).reshape(1)
    qblk = pl.BlockSpec((None, tq, LANES), lambda b, h, i, f: (b, i, h))
    kvblk = pl.BlockSpec((None, S, LANES), lambda b, h, i, f: (b, 0, h))
    grid_spec = pltpu.PrefetchScalarGridSpec(
        num_scalar_prefetch=1,
        grid=(B, H, S // tq),
        in_specs=[qblk, kvblk, kvblk,
                  pl.BlockSpec((None, LANES, S), lambda b, h, i, f: (b, h, 0)),
                  pl.BlockSpec((4, DA_HEAD_DIM), lambda b, h, i, f: (0, 0)),
                  pl.BlockSpec((1, LANES), lambda b, h, i, f: (0, 0)),
                  pl.BlockSpec((LANES, 1), lambda b, h, i, f: (0, 0))],
        out_specs=qblk,
        scratch_shapes=[pltpu.VMEM((2 * tq, LANES), BF16),
                        pltpu.VMEM((2 * tq, 1), F32),
                        pltpu.VMEM((2 * tq, 1), F32),
                        pltpu.VMEM((2 * tq, LANES), F32),
                        pltpu.VMEM((8, 2 * tq), F32),
                        pltpu.VMEM((LANES, 2 * tq), F32)],
    )
    out = pl.pallas_call(
        functools.partial(_attn_kernel, tq=tq, lambda_init=lambda_init),
        grid_spec=grid_spec,
        out_shape=jax.ShapeDtypeStruct((B, S, W), BF16),
        compiler_params=_cp("parallel", "parallel", "arbitrary"),
        name="diff_attn",
    )(flag, q3, k3, v3, vt3, lam_vecs, subln_g.reshape(1, LANES), subln_g.reshape(LANES, 1))
    return out.reshape(B * S, W)


def _stackmask(m):
    lane = lax.broadcasted_iota(I32, m.shape, 1)
    z = jnp.zeros_like(m)
    return jnp.concatenate([jnp.where(lane < RWKV_HEAD, m, z), jnp.where(lane >= RWKV_HEAD, m, z)], axis=0)


def _pair_sum(x, first):
    s1 = jnp.sum(jnp.where(first, x, 0.0), axis=-1, keepdims=True)
    s2 = jnp.sum(jnp.where(first, 0.0, x), axis=-1, keepdims=True)
    return jnp.where(first, s1, s2)


def _rwkv_scan_kernel(r_ref, k_ref, v_ref, a_ref, ld_ref, g_ref, kk_ref, ka_ref, rk_ref, lnw_ref, lnb_ref,
                      o_ref, s_ref, *, L):
    nb, tb, W = r_ref.shape
    tm = nb * tb
    n_chunks = tm // L
    per_batch = tb // L
    n_pairs = W // LANES
    hd = RWKV_HEAD
    bf = lambda t: t.astype(BF16)
    rows2d = lambda ref: ref[...].reshape(tm, W)

    @pl.when(pl.program_id(0) == 0)
    def _():
        s_ref[...] = jnp.zeros_like(s_ref)

    row = lax.broadcasted_iota(I32, (tm, tm), 0)
    col = lax.broadcasted_iota(I32, (tm, tm), 1)
    tri = jnp.where(jnp.logical_and(col <= row, (col // L) == (row // L)), 1.0, 0.0).astype(BF16)
    ld = rows2d(ld_ref)
    ld_hi = bf(ld)
    rem = ld - ld_hi.astype(F32)
    ld_mid = bf(rem)
    ld_lo = bf(rem - ld_mid.astype(F32))
    c = (jnp.dot(tri, ld_hi, preferred_element_type=F32) + jnp.dot(tri, ld_mid, preferred_element_type=F32)
         + jnp.dot(tri, ld_lo, preferred_element_type=F32))
    ec = jnp.exp(c)
    eci = jnp.exp(-c)
    ecm = jnp.exp(c - ld)
    r = rows2d(r_ref)
    k = rows2d(k_ref)
    v = rows2d(v_ref)
    a = rows2d(a_ref)
    kkr = k * kk_ref[...]
    kmod = k * (1.0 + (a - 1.0) * ka_ref[...])
    brk = r * kmod * rk_ref[...]

    lane = lax.broadcasted_iota(I32, (L, LANES), 1)
    rowl = lax.broadcasted_iota(I32, (L, LANES), 0)
    first = lane < hd
    lane_h = lane & (hd - 1)
    strict = lane_h < rowl
    incl = lane_h <= rowl
    eye = jnp.where(lane_h == rowl, 1.0, 0.0)

    chains = [(ch, p) for ch in range(n_chunks) for p in range(n_pairs)]
    rsl = lambda ch: slice(ch * L, (ch + 1) * L)
    csl = lambda p: slice(p * LANES, (p + 1) * LANES)
    fdot = lambda x, y: jnp.dot(x, y, preferred_element_type=F32)
    at, bt, kt, rt, vh, g_l = {}, {}, {}, {}, {}, {}
    for c_ in chains:
        ch, p = c_
        rs, cs = rsl(ch), csl(p)
        kkh = kkr[rs, cs]
        kkh = kkh / jnp.maximum(jnp.sqrt(_pair_sum(kkh * kkh, first)), 1e-12)
        vh[c_] = v[rs, cs]
        g_l[c_] = ec[ch * L + L - 1:ch * L + L, cs]
        at[c_] = -kkh * ecm[rs, cs]
        bt[c_] = kkh * a[rs, cs] * eci[rs, cs]
        kt[c_] = kmod[rs, cs] * eci[rs, cs]
        rt[c_] = r[rs, cs] * ec[rs, cs]
    gm = {c_: _nt(bf(jnp.concatenate([at[c_], rt[c_]], axis=0)),
                  jnp.concatenate([_stackmask(bf(bt[c_])), _stackmask(bf(kt[c_]))], axis=0)) for c_ in chains}
    a_ab = {c_: jnp.where(strict, gm[c_][:L, :LANES], 0.0) for c_ in chains}
    vsm = {c_: _stackmask(bf(vh[c_])) for c_ in chains}
    cmat = {c_: fdot(bf(jnp.where(strict, gm[c_][:L, LANES:], 0.0)), vsm[c_]) for c_ in chains}
    t_inv = {c_: eye + a_ab[c_] for c_ in chains}
    pw = {c_: bf(a_ab[c_]) for c_ in chains}
    for _ in range(int(math.log2(L)) - 1):
        pw = {c_: bf(fdot(pw[c_], _stackmask(pw[c_]))) for c_ in chains}
        t_inv = {c_: t_inv[c_] + fdot(pw[c_], _stackmask(bf(t_inv[c_]))) for c_ in chains}
    zz = {c_: fdot(bf(t_inv[c_]), jnp.concatenate([_stackmask(bf(at[c_])), _stackmask(bf(cmat[c_]))], axis=1))
          for c_ in chains}
    qy = {c_: fdot(bf(jnp.where(incl, gm[c_][L:, :LANES], 0.0)),
                   jnp.concatenate([_stackmask(bf(zz[c_][:, :LANES])), _stackmask(bf(zz[c_][:, LANES:]))], axis=1))
          for c_ in chains}
    y0 = {c_: qy[c_][:, LANES:] + fdot(bf(jnp.where(incl, gm[c_][L:, LANES:], 0.0)), vsm[c_]) for c_ in chains}
    qa = {c_: bf(jnp.concatenate([rt[c_] + qy[c_][:, :LANES], zz[c_][:, :LANES]], axis=0)) for c_ in chains}
    bkg = {c_: bf(jnp.concatenate([bt[c_] * g_l[c_], kt[c_] * g_l[c_]], axis=0)) for c_ in chains}

    lane_s = lax.broadcasted_iota(I32, (hd, LANES), 1)
    heads = [(b, p) for b in range(nb) for p in range(n_pairs)]
    sp = {bp: s_ref[bp[0] * n_pairs + bp[1]] for bp in heads}
    for j in range(per_batch):
        chunk = lambda b: b * per_batch + j
        yw = {(b, p): _nt(qa[chunk(b), p], _stackmask(bf(sp[b, p]))) for b, p in heads}
        upd = {(b, p): _tn(bf(jnp.concatenate([yw[b, p][L:] + zz[chunk(b), p][:, LANES:], vh[chunk(b), p]],
                                              axis=0)), bkg[chunk(b), p]) for b, p in heads}
        for b, p in heads:
            ch, cs, ts = chunk(b), csl(p), rsl(j)
            sp[b, p] = sp[b, p] * g_l[ch, p] + jnp.where(lane_s < hd, upd[b, p][:hd], upd[b, p][hd:])
            y = yw[b, p][:L] + y0[ch, p]
            mean = _pair_sum(y, first) * (1.0 / hd)
            yc = y - mean
            var = _pair_sum(yc * yc, first) * (1.0 / hd)
            yn = yc * lax.rsqrt(var + GN_EPS) * lnw_ref[:, cs] + lnb_ref[:, cs]
            bonus = _pair_sum(brk[rsl(ch), cs], first) * vh[ch, p]
            o_ref[b, ts, cs] = ((yn + bonus) * g_ref[b, ts, cs]).astype(o_ref.dtype)
    for b, p in heads:
        s_ref[b * n_pairs + p] = sp[b, p]


def _rwkv_scan(r, k, v, a, ld, g, k_k, k_a, r_k, ln_w, ln_b, L=RWKV_CHUNK, tm=128):
    B, S, W = r.shape
    seq = pl.BlockSpec((B, tm, W), lambda c: (0, c, 0))
    vec = pl.BlockSpec((1, W), lambda c: (0, 0))
    return pl.pallas_call(
        functools.partial(_rwkv_scan_kernel, L=L),
        grid=(S // tm,),
        in_specs=[seq] * 6 + [vec] * 5,
        out_specs=seq,
        out_shape=jax.ShapeDtypeStruct((B, S, W), BF16),
        scratch_shapes=[pltpu.VMEM((B * (W // LANES), RWKV_HEAD, LANES), F32)],
        compiler_params=_cp("arbitrary"),
        name="rwkv_scan",
    )(r, k, v, a, ld, g, k_k.reshape(1, W), k_a.reshape(1, W), r_k.reshape(1, W),
      ln_w.reshape(1, W), ln_b.reshape(1, W))


def _post_kernel(attn_ref, rw_ref, gate_ref, x_ref, mod_ref, wa_ref, wb_ref, wo_ref, g2_ref, wrh_ref, wrm_ref,
                 sug_ref, sd_ref, base_ref, h2p_ref, lg_ref):
    D = x_ref.shape[1]
    ya = jnp.dot(attn_ref[...], wa_ref[...], preferred_element_type=F32)
    yb = jnp.dot(rw_ref[...], wb_ref[...], preferred_element_type=F32)
    m = gate_ref[:, 0:D] * ya + gate_ref[:, D:] * yb
    x1 = x_ref[...] + mod_ref[2:3, :] * jnp.dot(m.astype(BF16), wo_ref[...], preferred_element_type=F32)
    y = x1 * lax.rsqrt(jnp.mean(x1 * x1, axis=-1, keepdims=True) + NORM_EPS) * g2_ref[...]
    h2 = y * (1.0 + mod_ref[4:5, :]) + mod_ref[3:4, :]
    hb = h2.astype(BF16)
    hm = (h2 - hb.astype(F32)).astype(BF16)
    lg_ref[...] = _nt(wrh_ref[...], hb) + _nt(wrh_ref[...], hm) + _nt(wrm_ref[...], hb)
    packed = _pack_halves(hb.astype(F32))
    half = packed.shape[1] // 2
    h2p_ref[0] = packed[:, :half]
    h2p_ref[1] = packed[:, half:]
    F = sd_ref.shape[0]
    gu = jnp.dot(hb, sug_ref[...], preferred_element_type=F32)
    shared = jnp.dot((_silu(gu[:, :F]) * gu[:, F:]).astype(BF16), sd_ref[...], preferred_element_type=F32)
    base_ref[...] = x1 + mod_ref[5:6, :] * shared


def _post(attn, rw, gate, x2, mod, wa, wb, wo, norm2_g, w_router_t, sug, sd, S, tm=512):
    N, D = x2.shape
    E = w_router_t.shape[0]
    wr_hi = w_router_t.astype(BF16)
    wr_mid = (w_router_t - wr_hi.astype(F32)).astype(BF16)
    tpb = S // tm
    row = lambda w: pl.BlockSpec((tm, w), lambda i: (i, 0))
    full = lambda a: pl.BlockSpec(a.shape, lambda i: (0, 0))
    return pl.pallas_call(
        _post_kernel,
        grid=(N // tm,),
        in_specs=[row(attn.shape[1]), row(rw.shape[1]), row(gate.shape[1]), row(D),
                  pl.BlockSpec((None, 6, D), lambda i: (i // tpb, 0, 0)),
                  full(wa), full(wb), full(wo), pl.BlockSpec((1, D), lambda i: (0, 0)), full(wr_hi), full(wr_mid),
                  full(sug), full(sd)],
        out_specs=[row(D), pl.BlockSpec((2, tm, D // 4), lambda i: (0, i, 0)), pl.BlockSpec((E, tm), lambda i: (0, i))],
        out_shape=[jax.ShapeDtypeStruct((N, D), F32), jax.ShapeDtypeStruct((2, N, D // 4), U32),
                   jax.ShapeDtypeStruct((E, N), F32)],
        compiler_params=_cp("parallel"),
        name="post_mixer",
    )(attn, rw, gate, x2, mod, wa, wb, wo, norm2_g.reshape(1, D), wr_hi, wr_mid, sug, sd)


def _first_argmax(x, idx, sentinel):
    m = jnp.max(x, axis=0, keepdims=True)
    return m, jnp.min(jnp.where(x == m, idx, sentinel), axis=0, keepdims=True)


def _route_kernel(lg_ref, bias_ref, e_ref, w_ref):
    E, T = lg_ref.shape
    gsz = E // N_GROUPS
    scores = _sigmoid(lg_ref[...])
    biased = scores + bias_ref[...]
    ig = lax.broadcasted_iota(I32, (gsz, T), 0)
    gs = []
    for g in range(N_GROUPS):
        blk = biased[g * gsz:(g + 1) * gsz, :]
        m1, i1 = _first_argmax(blk, ig, gsz)
        m2 = jnp.max(jnp.where(ig == i1, -jnp.inf, blk), axis=0, keepdims=True)
        gs.append(m1 + m2)
    gsc = jnp.concatenate(gs, axis=0)
    i8 = lax.broadcasted_iota(I32, (N_GROUPS, T), 0)
    chosen = jnp.zeros((N_GROUPS, T), F32)
    for _ in range(TOPK_GROUPS):
        _, gi = _first_argmax(gsc, i8, N_GROUPS)
        hit = i8 == gi
        chosen = jnp.where(hit, 1.0, chosen)
        gsc = jnp.where(hit, -jnp.inf, gsc)
    masked = jnp.concatenate(
        [jnp.where(chosen[g:g + 1, :] > 0.0, biased[g * gsz:(g + 1) * gsz, :], -jnp.inf)
         for g in range(N_GROUPS)], axis=0)
    ie = lax.broadcasted_iota(I32, (E, T), 0)
    idxs, wts = [], []
    for _ in range(TOP_K):
        _, ei = _first_argmax(masked, ie, E)
        hit = ie == ei
        idxs.append(ei)
        wts.append(jnp.sum(jnp.where(hit, scores, 0.0), axis=0, keepdims=True))
        masked = jnp.where(hit, -jnp.inf, masked)
    wt = jnp.concatenate(wts, axis=0)
    e_ref[...] = jnp.concatenate(idxs, axis=0)
    w_ref[...] = wt / jnp.sum(wt, axis=0, keepdims=True) * ROUTED_SCALE


def _route(logits_t, router_bias, T=512):
    E, N = logits_t.shape
    blk = pl.BlockSpec((TOP_K, T), lambda i: (0, i))
    return pl.pallas_call(
        _route_kernel,
        grid=(N // T,),
        in_specs=[pl.BlockSpec((E, T), lambda i: (0, i)), pl.BlockSpec((E, 1), lambda i: (0, 0))],
        out_specs=[blk, blk],
        out_shape=[jax.ShapeDtypeStruct((TOP_K, N), I32), jax.ShapeDtypeStruct((TOP_K, N), F32)],
        compiler_params=_cp("parallel"),
        name="route",
    )(logits_t, router_bias.reshape(E, 1))


def _rank_kernel(e_ref, rank_ref, cnt_ref, carry_ref, *, n_experts):
    T = e_ref.shape[1]

    @pl.when(pl.program_id(0) == 0)
    def _():
        carry_ref[...] = jnp.zeros_like(carry_ref)

    ie = lax.broadcasted_iota(I32, (n_experts, T), 0)
    e = e_ref[...]
    hits = [ie == e[kk:kk + 1, :] for kk in range(TOP_K)]
    onehot = jnp.zeros((n_experts, T), F32)
    for hsel in hits:
        onehot = onehot + hsel.astype(F32)
    tr = lax.broadcasted_iota(I32, (T, T), 0)
    tc = lax.broadcasted_iota(I32, (T, T), 1)
    before = (tr < tc).astype(BF16)
    base = _bdot(onehot, before) + carry_ref[:, 0:1]
    rank_ref[...] = jnp.concatenate(
        [jnp.sum(jnp.where(hsel, base, 0.0), axis=0, keepdims=True) for hsel in hits], axis=0).astype(I32)
    carry_ref[...] = carry_ref[...] + jnp.sum(onehot, axis=1, keepdims=True)
    cnt_ref[...] = carry_ref[...]


def _ranks(eidx_t, n_experts, T=512):
    N = eidx_t.shape[1]
    T = min(T, N)
    blk = pl.BlockSpec((TOP_K, T), lambda i: (0, i))
    cnt = pl.BlockSpec((n_experts, LANES), lambda i: (0, 0))
    return pl.pallas_call(
        functools.partial(_rank_kernel, n_experts=n_experts),
        grid=(N // T,),
        in_specs=[blk],
        out_specs=[blk, cnt],
        out_shape=[jax.ShapeDtypeStruct((TOP_K, N), I32), jax.ShapeDtypeStruct((n_experts, LANES), F32)],
        scratch_shapes=[pltpu.VMEM((n_experts, LANES), F32)],
        compiler_params=_cp("arbitrary"),
        name="ranks",
    )(eidx_t)


def _dest_kernel(e_ref, rank_ref, start_ref, d_ref):
    E = start_ref.shape[0]
    T = e_ref.shape[1]
    ie = lax.broadcasted_iota(I32, (E, T), 0)
    e = e_ref[...]
    start = start_ref[:, 0:1]
    rows = [jnp.sum(jnp.where(ie == e[kk:kk + 1, :], start, 0.0), axis=0, keepdims=True) for kk in range(TOP_K)]
    d_ref[...] = jnp.concatenate(rows, axis=0).astype(I32) + rank_ref[...]


def _dests(eidx_t, rank_t, pstart, T=2048):
    N = eidx_t.shape[1]
    T = min(T, N)
    E = pstart.shape[0]
    blk = pl.BlockSpec((TOP_K, T), lambda i: (0, i))
    return pl.pallas_call(
        _dest_kernel,
        grid=(N // T,),
        in_specs=[blk, blk, pl.BlockSpec((E, LANES), lambda i: (0, 0))],
        out_specs=blk,
        out_shape=jax.ShapeDtypeStruct((TOP_K, N), I32),
        compiler_params=_cp("parallel"),
        name="dests",
    )(eidx_t, rank_t, jnp.broadcast_to(pstart.astype(F32)[:, None], (E, LANES)))


def _expert_kernel(us_ref, ps_ref, x_hbm, wug_hbm, wd_hbm, y_hbm,
                   wug_f32, wd_f32, wug_bf, wd_bf, xbuf, ybuf, cnt_ref, wsem, xsem, ysem, *, n_rows):
    e = pl.program_id(0)
    n_experts = pl.num_programs(0)
    _, _, R, Ch = xbuf.shape
    C = 2 * Ch
    F = wd_hbm.shape[1]

    def w_copies(expert, slot):
        return (pltpu.make_async_copy(wug_hbm.at[expert], wug_f32.at[slot], wsem.at[slot]),
                pltpu.make_async_copy(wd_hbm.at[expert], wd_f32.at[slot], wsem.at[slot]))

    def x_copy(slot, half, row):
        return pltpu.make_async_copy(x_hbm.at[half, pl.ds(row, R)], xbuf.at[slot, half], xsem.at[slot])

    def y_copy(slot, half, row):
        return pltpu.make_async_copy(ybuf.at[slot, half], y_hbm.at[half, pl.ds(row, R)], ysem.at[slot])

    def start(copy, slot, row):
        for half in range(2):
            copy(slot, half, row).start()

    def wait(copy, slot):
        for half in range(2):
            copy(slot, half, 0).wait()

    @pl.when(e == 0)
    def _():
        for ahead in range(W_SLOTS - 1):
            for cp in w_copies(ahead, ahead):
                cp.start()
        cnt_ref[0] = 0
        for t in range(X_AHEAD):
            start(x_copy, t, t * R)
        ybuf[...] = jnp.zeros_like(ybuf)
        for slot in range(2):
            start(y_copy, slot, n_rows + slot * R)

    @pl.when(e + (W_SLOTS - 1) < n_experts)
    def _():
        for cp in w_copies(e + (W_SLOTS - 1), lax.rem(e + (W_SLOTS - 1), W_SLOTS)):
            cp.start()

    wslot = lax.rem(e, W_SLOTS)
    for cp in w_copies(e, wslot):
        cp.wait()
    wug_bf[...] = wug_f32[wslot].astype(BF16)
    wd_bf[...] = wd_f32[wslot].astype(BF16)
    n_valid = us_ref[e + 1] - us_ref[e]
    pbase = ps_ref[e]
    n_tiles = lax.shift_right_logical(n_valid + (R - 1), int(math.log2(R)))

    def make_tile(nb):
        rowid = lax.broadcasted_iota(I32, (nb * R, C), 0)

        def tile(blk0):
            n_done = cnt_ref[0]
            xslots = [lax.rem(n_done + b, X_SLOTS) for b in range(nb)]
            for b in range(nb):
                ahead = n_done + X_AHEAD + b
                start(x_copy, lax.rem(ahead, X_SLOTS), pl.multiple_of(ahead * R, R))
            for b in range(nb):
                wait(x_copy, xslots[b])
            x = jnp.concatenate([jnp.concatenate([xbuf[xs, 0], xbuf[xs, 1]], axis=1) for xs in xslots], axis=0)
            x = jnp.where(rowid < n_valid - blk0 * R, x, jnp.uint32(0))
            lo, hi = _unpack_halves(x)
            gu = (jnp.dot(lo.astype(BF16), wug_bf[0:C, :], preferred_element_type=F32)
                  + jnp.dot(hi.astype(BF16), wug_bf[C:, :], preferred_element_type=F32))
            hid = (_silu(gu[:, :F]) * gu[:, F:]).astype(BF16)
            y = jnp.dot(hid, wd_bf[...], preferred_element_type=F32)
            packed = _pack_halves(y.astype(BF16).astype(F32))
            for b in range(nb):
                slot = (n_done + b) & 1
                wait(y_copy, slot)
                ybuf[slot, 0] = packed[b * R:(b + 1) * R, :Ch]
                ybuf[slot, 1] = packed[b * R:(b + 1) * R, Ch:]
                start(y_copy, slot, pl.multiple_of(pbase + (blk0 + b) * R, R))
            cnt_ref[0] = n_done + nb

        return tile

    pair_tile, single_tile = make_tile(2), make_tile(1)

    def pairs(j, carry):
        pair_tile(2 * j)
        return carry

    lax.fori_loop(0, lax.shift_right_logical(n_tiles, 1), pairs, 0)

    @pl.when((n_tiles & 1) == 1)
    def _():
        single_tile(n_tiles - 1)

    @pl.when(e == pl.num_programs(0) - 1)
    def _():
        for t in range(X_AHEAD):
            wait(x_copy, lax.rem(cnt_ref[0] + t, X_SLOTS))
        for slot in range(2):
            wait(y_copy, slot)
        ybuf[0] = jnp.zeros((2, R, Ch), U32)
        first = lax.shift_right_logical(pbase + n_tiles * R, int(math.log2(R)))
        n_left = n_rows // R - first

        def fill(t, carry):
            start(y_copy, 0, pl.multiple_of((first + t) * R, R))
            return carry

        def drain(t, carry):
            wait(y_copy, 0)
            return carry

        lax.fori_loop(0, n_left, fill, 0)
        lax.fori_loop(0, n_left, drain, 0)


def _experts(ustart, pstart, xg, w_ug, w_d, n_rows, R):
    _, _, Ch = xg.shape
    E, D, F2 = w_ug.shape
    F = w_d.shape[1]
    grid_spec = pltpu.PrefetchScalarGridSpec(
        num_scalar_prefetch=2,
        grid=(E,),
        in_specs=[pl.BlockSpec(memory_space=pl.ANY)] * 3,
        out_specs=pl.BlockSpec(memory_space=pl.ANY),
        scratch_shapes=[pltpu.VMEM((W_SLOTS, D, F2), F32), pltpu.VMEM((W_SLOTS, F, D), F32),
                        pltpu.VMEM((D, F2), BF16), pltpu.VMEM((F, D), BF16),
                        pltpu.VMEM((X_SLOTS, 2, R, Ch), U32), pltpu.VMEM((2, 2, R, Ch), U32), pltpu.SMEM((1,), I32),
                        pltpu.SemaphoreType.DMA((W_SLOTS,)), pltpu.SemaphoreType.DMA((X_SLOTS,)),
                        pltpu.SemaphoreType.DMA((2,))],
    )
    return pl.pallas_call(
        functools.partial(_expert_kernel, n_rows=n_rows),
        grid_spec=grid_spec,
        out_shape=jax.ShapeDtypeStruct((2, n_rows + 2 * R, Ch), U32),
        compiler_params=_cp("arbitrary"),
        name="experts",
    )(ustart, pstart, xg, w_ug, w_d)


def _sc_scatter_rows(src, idx, n_rows):
    H, N, C = src.shape
    K = idx.shape[0]
    per_row = N // SC_WINDOW
    mesh = plsc.VectorSubcoreMesh(core_axis_name="c", subcore_axis_name="s")

    @functools.partial(pl.kernel, out_type=jax.ShapeDtypeStruct((H, n_rows, C), src.dtype), mesh=mesh,
                       scratch_types=[])
    def scatter_kernel(x_hbm, i_hbm, o_hbm):
        for h in range(H):
            def body(x_vmem, i_vmem):
                for k in range(K):
                    pltpu.sync_copy(x_vmem, o_hbm.at[h].at[i_vmem.at[k]])

            pltpu.emit_pipeline(
                body,
                grid=(per_row,),
                in_specs=[pl.BlockSpec((SC_WINDOW, C), lambda i: (i, 0)),
                          pl.BlockSpec((K, SC_WINDOW), lambda i: (0, i))],
                out_specs=[],
                core_axis_name=("c", "s"),
                dimension_semantics=(pltpu.PARALLEL,),
            )(x_hbm.at[h], i_hbm)

    return scatter_kernel(src, idx)


def _sc_gather_rows(src, idx):
    H, _, C = src.shape
    K, N = idx.shape
    per_row = N // SC_WINDOW
    mesh = plsc.VectorSubcoreMesh(core_axis_name="c", subcore_axis_name="s")

    @functools.partial(pl.kernel, out_type=jax.ShapeDtypeStruct((H, K * N, C), src.dtype), mesh=mesh,
                       scratch_types=[])
    def gather_kernel(x_hbm, i_hbm, o_hbm):
        for h in range(H):
            def body(i_vmem, o_vmem):
                pltpu.sync_copy(x_hbm.at[h].at[i_vmem.at[0]], o_vmem)

            pltpu.emit_pipeline(
                body,
                grid=(K * per_row,),
                in_specs=[pl.BlockSpec((1, SC_WINDOW), lambda i: (i // per_row, i % per_row))],
                out_specs=[pl.BlockSpec((SC_WINDOW, C), lambda i: (i, 0))],
                core_axis_name=("c", "s"),
                dimension_semantics=(pltpu.PARALLEL,),
            )(i_hbm, o_hbm.at[h])

    return gather_kernel(src, idx)


def _combine_kernel(*refs):
    y_refs = refs[:2 * TOP_K]
    w_ref, base_ref, mod_ref, o_ref = refs[2 * TOP_K:]
    T = base_ref.shape[0]
    tr = lax.broadcasted_iota(I32, (T, T), 0)
    tc = lax.broadcasted_iota(I32, (T, T), 1)
    wcol = _nt((tr == tc).astype(F32), w_ref[...], precision=HI)
    acc = [None] * 4
    for kk in range(TOP_K):
        wk = wcol[:, kk:kk + 1]
        for half in range(2):
            lo, hi = _unpack_halves(y_refs[2 * kk + half][...])
            for q, val in ((half, lo), (2 + half, hi)):
                acc[q] = val * wk if acc[q] is None else acc[q] + val * wk
    o_ref[...] = base_ref[...] + mod_ref[5:6, :] * jnp.concatenate(acc, axis=1)


def _combine(yg, w_t, base, mod, S, T=512):
    N, D = base.shape
    C = yg.shape[2]
    tpb = S // T
    n_tiles = N // T
    row = pl.BlockSpec((T, D), lambda i: (i, 0))
    piece = lambda kk, half: pl.BlockSpec((None, T, C), lambda i: (half, kk * n_tiles + i, 0))
    return pl.pallas_call(
        _combine_kernel,
        grid=(n_tiles,),
        in_specs=[piece(kk, half) for kk in range(TOP_K) for half in range(2)] + [
            pl.BlockSpec((TOP_K, T), lambda i: (0, i)),
            row,
            pl.BlockSpec((None, 6, D), lambda i: (i // tpb, 0, 0))],
        out_specs=row,
        out_shape=jax.ShapeDtypeStruct((N, D), F32),
        compiler_params=_cp("parallel"),
        name="combine",
    )(*([yg] * (2 * TOP_K)), w_t, base, mod)


def _layer(x, c, positions, layer_idx, w_ada, b_ada, norm1_g, w_in, w_gate, b_gate,
           q_norm_g, k_norm_g, lambda_q1, lambda_k1, lambda_q2, lambda_k2, subln_g,
           rwkv_mu, w_decay0, w_decay2, a0, a2, g2, k_k, k_a, r_k, ln_x_w, ln_x_b,
           w_branch_a, w_branch_b, w_out, norm2_g, w_router, router_bias,
           w_expert_up_gate, w_expert_down, w_shared_up_gate, w_shared_down):
    B, S, D = x.shape
    N = B * S
    E = w_router.shape[1]
    da_width = w_branch_a.shape[0]
    rw_width = w_branch_b.shape[0]
    lambda_init = 0.8 - 0.6 * math.exp(-0.3 * layer_idx)

    mod = _adaln(c, w_ada, b_ada)
    x2 = x.reshape(N, D)
    w_cat = jnp.concatenate([w_in, w_gate], axis=1).astype(BF16)
    qn, kn, v, r_, k_, v_, a_, ld_, g_, gate = _mixer_in(
        x2, positions.reshape(N, 1), mod, norm1_g, w_cat, b_gate, q_norm_g, k_norm_g,
        rwkv_mu, w_decay0, w_decay2, a0, a2, g2, S, da_width, rw_width)

    lam_vecs = jnp.stack([lambda_q1, lambda_k1, lambda_q2, lambda_k2])
    score_bound = 1.01 * DA_HEAD_DIM ** 0.5 * jnp.max(jnp.abs(q_norm_g)) * jnp.max(jnp.abs(k_norm_g))
    attn = _diff_attention(qn, kn, v, score_bound, lam_vecs, subln_g, B, S, lambda_init)

    seq = lambda t: t.reshape(B, S, rw_width)
    rw = _rwkv_scan(seq(r_), seq(k_), seq(v_), seq(a_), seq(ld_), seq(g_), k_k, k_a, r_k.reshape(-1),
                    ln_x_w, ln_x_b).reshape(N, rw_width)

    base, h2p, logits_t = _post(attn, rw, gate, x2, mod, w_branch_a.astype(BF16), w_branch_b.astype(BF16),
                                w_out.astype(BF16), norm2_g, w_router.T,
                                w_shared_up_gate.astype(BF16), w_shared_down.astype(BF16), S)

    eidx_t, w_t = _route(logits_t, router_bias)
    rank_t, counts = _ranks(eidx_t, E)
    R = EXPERT_TILE
    cnt = counts[:, 0].astype(I32)
    ustart = jnp.concatenate([jnp.zeros((1,), I32), jnp.cumsum(cnt)])
    pcnt = (cnt + R - 1) // R * R
    pstart = jnp.cumsum(pcnt) - pcnt
    dest_p = _dests(eidx_t, rank_t, pstart)
    n_rows = (N * TOP_K + E * (R - 1) + R - 1) // R * R
    xg = _sc_scatter_rows(h2p, dest_p, n_rows + X_AHEAD * R)
    y = _experts(ustart, pstart, xg, w_expert_up_gate, w_expert_down, n_rows, R)
    yg = _sc_gather_rows(y, dest_p)
    out = _combine(yg, w_t, base, mod, S)
    return out.reshape(B, S, D)


def kernel(x, c, positions, w_ada, b_ada, norm1_g, w_in, w_gate, b_gate, q_norm_g, k_norm_g, lambda_q1, lambda_k1, lambda_q2, lambda_k2, subln_g, rwkv_mu, w_decay0, w_decay2, a0, a2, g2, k_k, k_a, r_k, ln_x_w, ln_x_b, w_branch_a, w_branch_b, w_out, norm2_g, w_router, router_bias, w_expert_up_gate, w_expert_down, w_shared_up_gate, w_shared_down):
    for l in range(w_ada.shape[0]):
        x = _layer(x, c, positions, l, w_ada[l], b_ada[l], norm1_g[l], w_in[l], w_gate[l], b_gate[l],
                   q_norm_g[l], k_norm_g[l], lambda_q1[l], lambda_k1[l], lambda_q2[l], lambda_k2[l],
                   subln_g[l], rwkv_mu[l], w_decay0[l], w_decay2[l], a0[l], a2[l], g2[l], k_k[l],
                   k_a[l], r_k[l], ln_x_w[l], ln_x_b[l], w_branch_a[l], w_branch_b[l], w_out[l],
                   norm2_g[l], w_router[l], router_bias[l], w_expert_up_gate[l], w_expert_down[l],
                   w_shared_up_gate[l], w_shared_down[l])
    return x
```

```python
import functools
import math

import jax
import jax.numpy as jnp
from jax import lax
from jax.experimental import pallas as pl
from jax.experimental.pallas import tpu as pltpu
from jax.experimental.pallas import tpu_sc as plsc

F32 = jnp.float32
BF16 = jnp.bfloat16
I32 = jnp.int32
U32 = jnp.uint32
HI = lax.Precision.HIGHEST

CHUNK = 64
ROPE_THETA = 10000.0
NORM_EPS = 1e-6
SUBLN_EPS = 1e-5
DA_HEAD_DIM = 64
RWKV_HEAD = 64
GN_EPS = 64e-5
TOP_K = 8
N_GROUPS = 8
TOPK_GROUPS = 4
ROUTED_SCALE = 2.5
EXPERT_TILE = 256
W_SLOTS = 3
X_SLOTS = 6
X_AHEAD = X_SLOTS - 2
RWKV_CHUNK = 64
LANES = 128
SC_WINDOW = 128
NEG = -1e30
MAX_PLAIN_SCORE = 40.0
VMEM_LIMIT = 56 * 1024 * 1024


def _cp(*sem):
    return pltpu.CompilerParams(dimension_semantics=sem, vmem_limit_bytes=VMEM_LIMIT)


def _bdot(a, b):
    return jnp.dot(a.astype(BF16), b.astype(BF16), preferred_element_type=F32)


def _fdot(a, b):
    return jnp.dot(a, b, precision=HI, preferred_element_type=F32)


def _nt(a, b, precision=None):
    return lax.dot_general(a, b, (((1,), (1,)), ((), ())), precision=precision,
                           preferred_element_type=F32)


def _tn(a, b, precision=None):
    return lax.dot_general(a, b, (((0,), (0,)), ((), ())), precision=precision,
                           preferred_element_type=F32)


def _pack_halves(x):
    c = x.shape[1] // 2
    lo = lax.bitcast_convert_type(x[:, :c], U32)
    hi = lax.bitcast_convert_type(x[:, c:], U32)
    return (hi & jnp.uint32(0xFFFF0000)) | (lo >> 16)


def _unpack_halves(w):
    lo = lax.bitcast_convert_type(w << 16, F32)
    hi = lax.bitcast_convert_type(w & jnp.uint32(0xFFFF0000), F32)
    return lo, hi


def _sigmoid(x):
    return 1.0 / (1.0 + jnp.exp(-x))


def _silu(x):
    return x * _sigmoid(x)


def _ada_kernel(c_ref, w_ref, b_ref, o_ref):
    o_ref[...] = _fdot(_silu(c_ref[...]), w_ref[...]) + b_ref[...]


def _adaln(c, w_ada, b_ada):
    B, D = c.shape
    rows = -(-B // 8) * 8
    cpad = jnp.zeros((rows, D), F32).at[:B].set(c)
    n_out = w_ada.shape[1]
    out = pl.pallas_call(
        _ada_kernel,
        grid=(n_out // D,),
        in_specs=[pl.BlockSpec((rows, D), lambda j: (0, 0)),
                  pl.BlockSpec((D, D), lambda j: (0, j)),
                  pl.BlockSpec((1, D), lambda j: (0, j))],
        out_specs=pl.BlockSpec((rows, D), lambda j: (0, j)),
        out_shape=jax.ShapeDtypeStruct((rows, n_out), F32),
        compiler_params=_cp("arbitrary"),
        name="adaln",
    )(cpad, w_ada, b_ada.reshape(1, n_out))
    return out[:B].reshape(B, n_out // D, D)


def _mixer_in_kernel(x_ref, xprev_ref, mod_ref, g_ref, w_ref, bg_ref, pos_ref, invf_ref, qg_ref, kg_ref,
                     mu_ref, w0_ref, w2_ref, a0_ref, a2_ref, g2_ref,
                     qn_ref, kn_ref, v_ref, r_ref, k_ref, vr_ref, a_ref, ld_ref, gr_ref, gate_ref,
                     *, da_width, rw_cols, rw_width, q_scale, tiles_per_seq):
    tm = x_ref.shape[0]

    def modulated(x):
        y = x * lax.rsqrt(jnp.mean(x * x, axis=-1, keepdims=True) + NORM_EPS) * g_ref[...]
        return (y * (1.0 + mod_ref[1:2, :]) + mod_ref[0:1, :]).astype(BF16)

    def proj(hb, c0, width, step=512):
        parts = [jnp.dot(hb, w_ref[:, c0 + o:c0 + min(o + step, width)], preferred_element_type=F32)
                 for o in range(0, width, step)]
        return parts[0] if len(parts) == 1 else jnp.concatenate(parts, axis=1)

    h = modulated(x_ref[...])

    lane = lax.broadcasted_iota(I32, (tm, LANES), 1)
    first = lane < DA_HEAD_DIM
    lo_half = (lane & (DA_HEAD_DIM - 1)) < DA_HEAD_DIM // 2
    ang = pos_ref[...].astype(F32) * invf_ref[...]
    cos = jnp.cos(ang)
    sin = jnp.sin(ang)
    sin = jnp.where(lo_half, -sin, sin)
    for c0, dst, gn_ref, mult in ((0, qn_ref, qg_ref, q_scale), (da_width, kn_ref, kg_ref, 1.0)):
        raw = proj(h, c0, da_width)
        for blk in range(da_width // LANES):
            x = raw[:, blk * LANES:(blk + 1) * LANES]
            xx = x * x
            s_first = jnp.sum(jnp.where(first, xx, 0.0), axis=-1, keepdims=True)
            s_second = jnp.sum(jnp.where(first, 0.0, xx), axis=-1, keepdims=True)
            ms = jnp.where(first, s_first, s_second) * (1.0 / DA_HEAD_DIM)
            xn = x * lax.rsqrt(ms + NORM_EPS) * gn_ref[...]
            rot = jnp.where(lo_half, pltpu.roll(xn, LANES - DA_HEAD_DIM // 2, axis=1),
                            pltpu.roll(xn, DA_HEAD_DIM // 2, axis=1))
            dst[:, blk * LANES:(blk + 1) * LANES] = ((xn * cos + rot * sin) * mult).astype(dst.dtype)
    v_ref[...] = proj(h, 2 * da_width, da_width).astype(v_ref.dtype)

    c_rw = 3 * da_width
    p = proj(h, c_rw, rw_cols)
    p_before = proj(modulated(xprev_ref[...]), c_rw, rw_cols)
    seq_start = (pl.program_id(0) % tiles_per_seq) == 0
    last_prev = jnp.where(seq_start, 0.0, p_before[7:8, :])
    rowi = lax.broadcasted_iota(I32, p.shape, 0)
    prev = jnp.where(rowi == 0, last_prev, pltpu.roll(p, 1, axis=0))
    xs = p + (prev - p) * mu_ref[...]
    width = rw_width
    r_ref[...] = xs[:, 0:width]
    k_ref[...] = xs[:, width:2 * width]
    vr_ref[...] = xs[:, 2 * width:3 * width]
    xwa = xs[:, 3 * width:3 * width + LANES]
    xg = xs[:, 3 * width + LANES:]
    z = w0_ref[...] + _bdot(jnp.tanh(xwa), w2_ref[...])
    w = -(jnp.maximum(-z, 0.0) + jnp.log(1.0 + jnp.exp(-jnp.abs(z)))) - 0.5
    ld_ref[...] = -jnp.exp(w)
    a_ref[...] = _sigmoid(a0_ref[...] + _bdot(xwa, a2_ref[...]))
    gr_ref[...] = _bdot(_sigmoid(xg), g2_ref[...])

    gate_ref[...] = _sigmoid(proj(h, c_rw + rw_cols, gate_ref.shape[1]) + bg_ref[...]).astype(gate_ref.dtype)


def _mixer_in(x2, pos2, mod, norm1_g, w_cat, b_gate, q_norm_g, k_norm_g, mu, w_decay0, w_decay2, a0, a2, g2,
              S, da_width, rw_width, tm=512):
    N, D = x2.shape
    n_gate = b_gate.shape[0]
    rw_cols = mu.shape[0]
    tpb = S // tm
    d = DA_HEAD_DIM
    inv_freq = 1.0 / (ROPE_THETA ** (jnp.arange(0, d, 2, dtype=F32) / d))
    invf = jnp.tile(inv_freq, LANES // (d // 2)).reshape(1, LANES)
    dl, al = w_decay2.shape[0], a2.shape[0]
    assert dl + al == LANES and g2.shape[0] == LANES
    w2p = jnp.zeros((LANES, rw_width), F32).at[:dl].set(w_decay2)
    a2p = jnp.zeros((LANES, rw_width), F32).at[dl:].set(a2)
    kern = functools.partial(_mixer_in_kernel, da_width=da_width, rw_cols=rw_cols, rw_width=rw_width,
                             q_scale=d ** -0.5 * math.log2(math.e),
                             tiles_per_seq=tpb)
    row = lambda w: pl.BlockSpec((tm, w), lambda i: (i, 0))
    vec = lambda n: pl.BlockSpec((1, n), lambda i: (0, 0))
    mat = pl.BlockSpec((LANES, rw_width), lambda i: (0, 0))
    f32 = lambda w: jax.ShapeDtypeStruct((N, w), F32)
    bf16 = lambda w: jax.ShapeDtypeStruct((N, w), BF16)
    return pl.pallas_call(
        kern,
        grid=(N // tm,),
        in_specs=[row(D),
                  pl.BlockSpec((8, D), lambda i: (jnp.maximum(i * (tm // 8) - 1, 0), 0)),
                  pl.BlockSpec((None, 6, D), lambda i: (i // tpb, 0, 0)),
                  vec(D),
                  pl.BlockSpec(w_cat.shape, lambda i: (0, 0)),
                  vec(n_gate),
                  pl.BlockSpec((tm, 1), lambda i: (i, 0)),
                  vec(LANES), vec(LANES), vec(LANES),
                  vec(rw_cols), vec(rw_width), mat, vec(rw_width), mat, mat],
        out_specs=[row(da_width)] * 3 + [row(rw_width)] * 6 + [row(n_gate)],
        out_shape=[bf16(da_width)] * 3 + [f32(rw_width)] * 6 + [bf16(n_gate)],
        compiler_params=_cp("parallel"),
        name="mixer_in",
    )(x2, x2, mod, norm1_g.reshape(1, D), w_cat, b_gate.reshape(1, n_gate), pos2, invf,
      jnp.tile(q_norm_g, 2).reshape(1, LANES), jnp.tile(k_norm_g, 2).reshape(1, LANES),
      mu.reshape(1, rw_cols), w_decay0.reshape(1, rw_width), w2p, a0.reshape(1, rw_width), a2p, g2)


def _attn_kernel(flag_ref, q_ref, k_ref, v_ref, vt_ref, lam_ref, sg_ref, sgc_ref, o_ref,
                 qz_ref, m_ref, l_ref, acc_ref, lpt_ref, acct_ref, *, tq, lambda_init):
    i = pl.program_id(2)
    lane = lax.broadcasted_iota(I32, (tq, LANES), 1)
    q = q_ref[...]
    zero = jnp.zeros_like(q)
    qz_ref[0:tq, :] = jnp.where(lane < DA_HEAD_DIM, q, zero)
    qz_ref[tq:, :] = jnp.where(lane >= DA_HEAD_DIM, q, zero)
    bounded = flag_ref[0] == 1
    lv = lam_ref[...]
    lam = (jnp.exp(jnp.sum(lv[0:1] * lv[1:2], keepdims=True))
           - jnp.exp(jnp.sum(lv[2:3] * lv[3:4], keepdims=True)) + lambda_init)

    def run(step):
        def body(j, carry):
            step(j, False)
            return carry
        lax.fori_loop(0, i, body, 0)
        step(i, True)

    def plain_step(j, masked):
        off = pl.multiple_of(j * tq, tq)
        st = _nt(k_ref[pl.ds(off, tq), :], qz_ref[...])
        if masked:
            row = lax.broadcasted_iota(I32, st.shape, 0)
            col = lax.broadcasted_iota(I32, st.shape, 1)
            st = jnp.where((row // CHUNK) <= ((col & (tq - 1)) // CHUNK), st, NEG)
        pt = jnp.exp2(st)
        part = pt[0:8, :]
        for g in range(1, tq // 8):
            part = part + pt[8 * g:8 * g + 8, :]
        lpt_ref[...] += part
        acct_ref[...] += jnp.dot(vt_ref[:, pl.ds(off, tq)], pt.astype(BF16), preferred_element_type=F32)

    @pl.when(bounded)
    def _():
        lpt_ref[...] = jnp.zeros_like(lpt_ref)
        acct_ref[...] = jnp.zeros_like(acct_ref)

        def two_steps(p, carry):
            plain_step(2 * p, False)
            plain_step(2 * p + 1, False)
            return carry

        lax.fori_loop(0, lax.shift_right_logical(i, 1), two_steps, 0)

        @pl.when((i & 1) == 1)
        def _():
            plain_step(i - 1, False)
            plain_step(i, True)

        @pl.when((i & 1) == 0)
        def _():
            plain_step(i, True)

        lsum = jnp.sum(lpt_ref[...], axis=0, keepdims=True)
        ot = acct_ref[:, 0:tq] / lsum[:, 0:tq] - lam * (acct_ref[:, tq:] / lsum[:, tq:])
        ot = ot * lax.rsqrt(jnp.mean(ot * ot, axis=0, keepdims=True) + SUBLN_EPS) * sgc_ref[...]
        o_ref[...] = (ot * (1.0 - lambda_init)).T.astype(o_ref.dtype)

    def online_step(j, masked):
        off = pl.multiple_of(j * tq, tq)
        s = _nt(qz_ref[...], k_ref[pl.ds(off, tq), :])
        if masked:
            row = lax.broadcasted_iota(I32, s.shape, 0)
            col = lax.broadcasted_iota(I32, s.shape, 1)
            s = jnp.where((col // CHUNK) <= ((row & (tq - 1)) // CHUNK), s, NEG)
        m_old = m_ref[...]
        m_new = jnp.maximum(m_old, jnp.max(s, axis=-1, keepdims=True))
        alpha = jnp.exp2(m_old - m_new)
        pr = jnp.exp2(s - m_new)
        l_ref[...] = alpha * l_ref[...] + jnp.sum(pr, axis=-1, keepdims=True)
        acc_ref[...] = alpha * acc_ref[...] + jnp.dot(pr.astype(BF16), v_ref[pl.ds(off, tq), :],
                                                      preferred_element_type=F32)
        m_ref[...] = m_new

    @pl.when(jnp.logical_not(bounded))
    def _():
        m_ref[...] = jnp.full_like(m_ref, NEG)
        l_ref[...] = jnp.zeros_like(l_ref)
        acc_ref[...] = jnp.zeros_like(acc_ref)
        run(online_step)
        o = acc_ref[0:tq, :] / l_ref[0:tq, :] - lam * (acc_ref[tq:, :] / l_ref[tq:, :])
        o = o * lax.rsqrt(jnp.mean(o * o, axis=-1, keepdims=True) + SUBLN_EPS) * sg_ref[...]
        o_ref[...] = (o * (1.0 - lambda_init)).astype(o_ref.dtype)


def _diff_attention(qn, kn, v, score_bound, lam_vecs, subln_g, B, S, lambda_init, tq=1024):
    W = qn.shape[1]
    H = W // LANES
    q3 = qn.reshape(B, S, W)
    k3 = kn.reshape(B, S, W)
    v3 = v.reshape(B, S, W)
    vt3 = v3.transpose(0, 2, 1)
    flag = (score_bound <= MAX_PLAIN_SCORE).astype(I32).reshape(1)
    qblk = pl.BlockSpec((None, tq, LANES), lambda b, h, i, f: (b, i, h))
    kvblk = pl.BlockSpec((None, S, LANES), lambda b, h, i, f: (b, 0, h))
    grid_spec = pltpu.PrefetchScalarGridSpec(
        num_scalar_prefetch=1,
        grid=(B, H, S // tq),
        in_specs=[qblk, kvblk, kvblk,
                  pl.BlockSpec((None, LANES, S), lambda b, h, i, f: (b, h, 0)),
                  pl.BlockSpec((4, DA_HEAD_DIM), lambda b, h, i, f: (0, 0)),
                  pl.BlockSpec((1, LANES), lambda b, h, i, f: (0, 0)),
                  pl.BlockSpec((LANES, 1), lambda b, h, i, f: (0, 0))],
        out_specs=qblk,
        scratch_shapes=[pltpu.VMEM((2 * tq, LANES), BF16),
                        pltpu.VMEM((2 * tq, 1), F32),
                        pltpu.VMEM((2 * tq, 1), F32),
                        pltpu.VMEM((2 * tq, LANES), F32),
                        pltpu.VMEM((8, 2 * tq), F32),
                        pltpu.VMEM((LANES, 2 * tq), F32)],
    )
    out = pl.pallas_call(
        functools.partial(_attn_kernel, tq=tq, lambda_init=lambda_init),
        grid_spec=grid_spec,
        out_shape=jax.ShapeDtypeStruct((B, S, W), BF16),
        compiler_params=_cp("parallel", "parallel", "arbitrary"),
        name="diff_attn",
    )(flag, q3, k3, v3, vt3, lam_vecs, subln_g.reshape(1, LANES), subln_g.reshape(LANES, 1))
    return out.reshape(B * S, W)


def _stackmask(m):
    lane = lax.broadcasted_iota(I32, m.shape, 1)
    z = jnp.zeros_like(m)
    return jnp.concatenate([jnp.where(lane < RWKV_HEAD, m, z), jnp.where(lane >= RWKV_HEAD, m, z)], axis=0)


def _pair_sum(x, first):
    s1 = jnp.sum(jnp.where(first, x, 0.0), axis=-1, keepdims=True)
    s2 = jnp.sum(jnp.where(first, 0.0, x), axis=-1, keepdims=True)
    return jnp.where(first, s1, s2)


def _rwkv_scan_kernel(r_ref, k_ref, v_ref, a_ref, ld_ref, g_ref, kk_ref, ka_ref, rk_ref, lnw_ref, lnb_ref,
                      o_ref, s_ref, *, L):
    nb, tb, W = r_ref.shape
    tm = nb * tb
    n_chunks = tm // L
    per_batch = tb // L
    n_pairs = W // LANES
    hd = RWKV_HEAD
    bf = lambda t: t.astype(BF16)
    rows2d = lambda ref: ref[...].reshape(tm, W)

    @pl.when(pl.program_id(0) == 0)
    def _():
        s_ref[...] = jnp.zeros_like(s_ref)

    row = lax.broadcasted_iota(I32, (tm, tm), 0)
    col = lax.broadcasted_iota(I32, (tm, tm), 1)
    tri = jnp.where(jnp.logical_and(col <= row, (col // L) == (row // L)), 1.0, 0.0).astype(BF16)
    ld = rows2d(ld_ref)
    ld_hi = bf(ld)
    rem = ld - ld_hi.astype(F32)
    ld_mid = bf(rem)
    ld_lo = bf(rem - ld_mid.astype(F32))
    c = (jnp.dot(tri, ld_hi, preferred_element_type=F32) + jnp.dot(tri, ld_mid, preferred_element_type=F32)
         + jnp.dot(tri, ld_lo, preferred_element_type=F32))
    ec = jnp.exp(c)
    eci = jnp.exp(-c)
    ecm = jnp.exp(c - ld)
    r = rows2d(r_ref)
    k = rows2d(k_ref)
    v = rows2d(v_ref)
    a = rows2d(a_ref)
    kkr = k * kk_ref[...]
    kmod = k * (1.0 + (a - 1.0) * ka_ref[...])
    brk = r * kmod * rk_ref[...]

    lane = lax.broadcasted_iota(I32, (L, LANES), 1)
    rowl = lax.broadcasted_iota(I32, (L, LANES), 0)
    first = lane < hd
    lane_h = lane & (hd - 1)
    strict = lane_h < rowl
    incl = lane_h <= rowl
    eye = jnp.where(lane_h == rowl, 1.0, 0.0)

    chains = [(ch, p) for ch in range(n_chunks) for p in range(n_pairs)]
    rsl = lambda ch: slice(ch * L, (ch + 1) * L)
    csl = lambda p: slice(p * LANES, (p + 1) * LANES)
    fdot = lambda x, y: jnp.dot(x, y, preferred_element_type=F32)
    at, bt, kt, rt, vh, g_l = {}, {}, {}, {}, {}, {}
    for c_ in chains:
        ch, p = c_
        rs, cs = rsl(ch), csl(p)
        kkh = kkr[rs, cs]
        kkh = kkh / jnp.maximum(jnp.sqrt(_pair_sum(kkh * kkh, first)), 1e-12)
        vh[c_] = v[rs, cs]
        g_l[c_] = ec[ch * L + L - 1:ch * L + L, cs]
        at[c_] = -kkh * ecm[rs, cs]
        bt[c_] = kkh * a[rs, cs] * eci[rs, cs]
        kt[c_] = kmod[rs, cs] * eci[rs, cs]
        rt[c_] = r[rs, cs] * ec[rs, cs]
    gm = {c_: _nt(bf(jnp.concatenate([at[c_], rt[c_]], axis=0)),
                  jnp.concatenate([_stackmask(bf(bt[c_])), _stackmask(bf(kt[c_]))], axis=0)) for c_ in chains}
    a_ab = {c_: jnp.where(strict, gm[c_][:L, :LANES], 0.0) for c_ in chains}
    vsm = {c_: _stackmask(bf(vh[c_])) for c_ in chains}
    cmat = {c_: fdot(bf(jnp.where(strict, gm[c_][:L, LANES:], 0.0)), vsm[c_]) for c_ in chains}
    t_inv = {c_: eye + a_ab[c_] for c_ in chains}
    pw = {c_: bf(a_ab[c_]) for c_ in chains}
    for _ in range(int(math.log2(L)) - 1):
        pw = {c_: bf(fdot(pw[c_], _stackmask(pw[c_]))) for c_ in chains}
        t_inv = {c_: t_inv[c_] + fdot(pw[c_], _stackmask(bf(t_inv[c_]))) for c_ in chains}
    zz = {c_: fdot(bf(t_inv[c_]), jnp.concatenate([_stackmask(bf(at[c_])), _stackmask(bf(cmat[c_]))], axis=1))
          for c_ in chains}
    qy = {c_: fdot(bf(jnp.where(incl, gm[c_][L:, :LANES], 0.0)),
                   jnp.concatenate([_stackmask(bf(zz[c_][:, :LANES])), _stackmask(bf(zz[c_][:, LANES:]))], axis=1))
          for c_ in chains}
    y0 = {c_: qy[c_][:, LANES:] + fdot(bf(jnp.where(incl, gm[c_][L:, LANES:], 0.0)), vsm[c_]) for c_ in chains}
    qa = {c_: bf(jnp.concatenate([rt[c_] + qy[c_][:, :LANES], zz[c_][:, :LANES]], axis=0)) for c_ in chains}
    bkg = {c_: bf(jnp.concatenate([bt[c_] * g_l[c_], kt[c_] * g_l[c_]], axis=0)) for c_ in chains}

    lane_s = lax.broadcasted_iota(I32, (hd, LANES), 1)
    heads = [(b, p) for b in range(nb) for p in range(n_pairs)]
    sp = {bp: s_ref[bp[0] * n_pairs + bp[1]] for bp in heads}
    for j in range(per_batch):
        chunk = lambda b: b * per_batch + j
        yw = {(b, p): _nt(qa[chunk(b), p], _stackmask(bf(sp[b, p]))) for b, p in heads}
        upd = {(b, p): _tn(bf(jnp.concatenate([yw[b, p][L:] + zz[chunk(b), p][:, LANES:], vh[chunk(b), p]],
                                              axis=0)), bkg[chunk(b), p]) for b, p in heads}
        for b, p in heads:
            ch, cs, ts = chunk(b), csl(p), rsl(j)
            sp[b, p] = sp[b, p] * g_l[ch, p] + jnp.where(lane_s < hd, upd[b, p][:hd], upd[b, p][hd:])
            y = yw[b, p][:L] + y0[ch, p]
            mean = _pair_sum(y, first) * (1.0 / hd)
            yc = y - mean
            var = _pair_sum(yc * yc, first) * (1.0 / hd)
            yn = yc * lax.rsqrt(var + GN_EPS) * lnw_ref[:, cs] + lnb_ref[:, cs]
            bonus = _pair_sum(brk[rsl(ch), cs], first) * vh[ch, p]
            o_ref[b, ts, cs] = ((yn + bonus) * g_ref[b, ts, cs]).astype(o_ref.dtype)
    for b, p in heads:
        s_ref[b * n_pairs + p] = sp[b, p]


def _rwkv_scan(r, k, v, a, ld, g, k_k, k_a, r_k, ln_w, ln_b, L=RWKV_CHUNK, tm=128):
    B, S, W = r.shape
    seq = pl.BlockSpec((B, tm, W), lambda c: (0, c, 0))
    vec = pl.BlockSpec((1, W), lambda c: (0, 0))
    return pl.pallas_call(
        functools.partial(_rwkv_scan_kernel, L=L),
        grid=(S // tm,),
        in_specs=[seq] * 6 + [vec] * 5,
        out_specs=seq,
        out_shape=jax.ShapeDtypeStruct((B, S, W), BF16),
        scratch_shapes=[pltpu.VMEM((B * (W // LANES), RWKV_HEAD, LANES), F32)],
        compiler_params=_cp("arbitrary"),
        name="rwkv_scan",
    )(r, k, v, a, ld, g, k_k.reshape(1, W), k_a.reshape(1, W), r_k.reshape(1, W),
      ln_w.reshape(1, W), ln_b.reshape(1, W))


def _post_kernel(attn_ref, rw_ref, gate_ref, x_ref, mod_ref, wa_ref, wb_ref, wo_ref, g2_ref, wrh_ref, wrm_ref,
                 sug_ref, sd_ref, base_ref, h2p_ref, lg_ref):
    D = x_ref.shape[1]
    ya = jnp.dot(attn_ref[...], wa_ref[...], preferred_element_type=F32)
    yb = jnp.dot(rw_ref[...], wb_ref[...], preferred_element_type=F32)
    m = gate_ref[:, 0:D] * ya + gate_ref[:, D:] * yb
    x1 = x_ref[...] + mod_ref[2:3, :] * jnp.dot(m.astype(BF16), wo_ref[...], preferred_element_type=F32)
    y = x1 * lax.rsqrt(jnp.mean(x1 * x1, axis=-1, keepdims=True) + NORM_EPS) * g2_ref[...]
    h2 = y * (1.0 + mod_ref[4:5, :]) + mod_ref[3:4, :]
    hb = h2.astype(BF16)
    hm = (h2 - hb.astype(F32)).astype(BF16)
    lg_ref[...] = _nt(wrh_ref[...], hb) + _nt(wrh_ref[...], hm) + _nt(wrm_ref[...], hb)
    packed = _pack_halves(hb.astype(F32))
    half = packed.shape[1] // 2
    h2p_ref[0] = packed[:, :half]
    h2p_ref[1] = packed[:, half:]
    F = sd_ref.shape[0]
    gu = jnp.dot(hb, sug_ref[...], preferred_element_type=F32)
    shared = jnp.dot((_silu(gu[:, :F]) * gu[:, F:]).astype(BF16), sd_ref[...], preferred_element_type=F32)
    base_ref[...] = x1 + mod_ref[5:6, :] * shared


def _post(attn, rw, gate, x2, mod, wa, wb, wo, norm2_g, w_router_t, sug, sd, S, tm=512):
    N, D = x2.shape
    E = w_router_t.shape[0]
    wr_hi = w_router_t.astype(BF16)
    wr_mid = (w_router_t - wr_hi.astype(F32)).astype(BF16)
    tpb = S // tm
    row = lambda w: pl.BlockSpec((tm, w), lambda i: (i, 0))
    full = lambda a: pl.BlockSpec(a.shape, lambda i: (0, 0))
    return pl.pallas_call(
        _post_kernel,
        grid=(N // tm,),
        in_specs=[row(attn.shape[1]), row(rw.shape[1]), row(gate.shape[1]), row(D),
                  pl.BlockSpec((None, 6, D), lambda i: (i // tpb, 0, 0)),
                  full(wa), full(wb), full(wo), pl.BlockSpec((1, D), lambda i: (0, 0)), full(wr_hi), full(wr_mid),
                  full(sug), full(sd)],
        out_specs=[row(D), pl.BlockSpec((2, tm, D // 4), lambda i: (0, i, 0)), pl.BlockSpec((E, tm), lambda i: (0, i))],
        out_shape=[jax.ShapeDtypeStruct((N, D), F32), jax.ShapeDtypeStruct((2, N, D // 4), U32),
                   jax.ShapeDtypeStruct((E, N), F32)],
        compiler_params=_cp("parallel"),
        name="post_mixer",
    )(attn, rw, gate, x2, mod, wa, wb, wo, norm2_g.reshape(1, D), wr_hi, wr_mid, sug, sd)


def _first_argmax(x, idx, sentinel):
    m = jnp.max(x, axis=0, keepdims=True)
    return m, jnp.min(jnp.where(x == m, idx, sentinel), axis=0, keepdims=True)


def _route_kernel(lg_ref, bias_ref, e_ref, w_ref):
    E, T = lg_ref.shape
    gsz = E // N_GROUPS
    scores = _sigmoid(lg_ref[...])
    biased = scores + bias_ref[...]
    ig = lax.broadcasted_iota(I32, (gsz, T), 0)
    gs = []
    for g in range(N_GROUPS):
        blk = biased[g * gsz:(g + 1) * gsz, :]
        m1, i1 = _first_argmax(blk, ig, gsz)
        m2 = jnp.max(jnp.where(ig == i1, -jnp.inf, blk), axis=0, keepdims=True)
        gs.append(m1 + m2)
    gsc = jnp.concatenate(gs, axis=0)
    i8 = lax.broadcasted_iota(I32, (N_GROUPS, T), 0)
    chosen = jnp.zeros((N_GROUPS, T), F32)
    for _ in range(TOPK_GROUPS):
        _, gi = _first_argmax(gsc, i8, N_GROUPS)
        hit = i8 == gi
        chosen = jnp.where(hit, 1.0, chosen)
        gsc = jnp.where(hit, -jnp.inf, gsc)
    masked = jnp.concatenate(
        [jnp.where(chosen[g:g + 1, :] > 0.0, biased[g * gsz:(g + 1) * gsz, :], -jnp.inf)
         for g in range(N_GROUPS)], axis=0)
    ie = lax.broadcasted_iota(I32, (E, T), 0)
    idxs, wts = [], []
    for _ in range(TOP_K):
        _, ei = _first_argmax(masked, ie, E)
        hit = ie == ei
        idxs.append(ei)
        wts.append(jnp.sum(jnp.where(hit, scores, 0.0), axis=0, keepdims=True))
        masked = jnp.where(hit, -jnp.inf, masked)
    wt = jnp.concatenate(wts, axis=0)
    e_ref[...] = jnp.concatenate(idxs, axis=0)
    w_ref[...] = wt / jnp.sum(wt, axis=0, keepdims=True) * ROUTED_SCALE


def _route(logits_t, router_bias, T=512):
    E, N = logits_t.shape
    blk = pl.BlockSpec((TOP_K, T), lambda i: (0, i))
    return pl.pallas_call(
        _route_kernel,
        grid=(N // T,),
        in_specs=[pl.BlockSpec((E, T), lambda i: (0, i)), pl.BlockSpec((E, 1), lambda i: (0, 0))],
        out_specs=[blk, blk],
        out_shape=[jax.ShapeDtypeStruct((TOP_K, N), I32), jax.ShapeDtypeStruct((TOP_K, N), F32)],
        compiler_params=_cp("parallel"),
        name="route",
    )(logits_t, router_bias.reshape(E, 1))


def _rank_kernel(e_ref, rank_ref, cnt_ref, carry_ref, *, n_experts):
    T = e_ref.shape[1]

    @pl.when(pl.program_id(0) == 0)
    def _():
        carry_ref[...] = jnp.zeros_like(carry_ref)

    ie = lax.broadcasted_iota(I32, (n_experts, T), 0)
    e = e_ref[...]
    hits = [ie == e[kk:kk + 1, :] for kk in range(TOP_K)]
    onehot = jnp.zeros((n_experts, T), F32)
    for hsel in hits:
        onehot = onehot + hsel.astype(F32)
    tr = lax.broadcasted_iota(I32, (T, T), 0)
    tc = lax.broadcasted_iota(I32, (T, T), 1)
    before = (tr < tc).astype(BF16)
    base = _bdot(onehot, before) + carry_ref[:, 0:1]
    rank_ref[...] = jnp.concatenate(
        [jnp.sum(jnp.where(hsel, base, 0.0), axis=0, keepdims=True) for hsel in hits], axis=0).astype(I32)
    carry_ref[...] = carry_ref[...] + jnp.sum(onehot, axis=1, keepdims=True)
    cnt_ref[...] = carry_ref[...]


def _ranks(eidx_t, n_experts, T=512):
    N = eidx_t.shape[1]
    T = min(T, N)
    blk = pl.BlockSpec((TOP_K, T), lambda i: (0, i))
    cnt = pl.BlockSpec((n_experts, LANES), lambda i: (0, 0))
    return pl.pallas_call(
        functools.partial(_rank_kernel, n_experts=n_experts),
        grid=(N // T,),
        in_specs=[blk],
        out_specs=[blk, cnt],
        out_shape=[jax.ShapeDtypeStruct((TOP_K, N), I32), jax.ShapeDtypeStruct((n_experts, LANES), F32)],
        scratch_shapes=[pltpu.VMEM((n_experts, LANES), F32)],
        compiler_params=_cp("arbitrary"),
        name="ranks",
    )(eidx_t)


def _dest_kernel(e_ref, rank_ref, start_ref, d_ref):
    E = start_ref.shape[0]
    T = e_ref.shape[1]
    ie = lax.broadcasted_iota(I32, (E, T), 0)
    e = e_ref[...]
    start = start_ref[:, 0:1]
    rows = [jnp.sum(jnp.where(ie == e[kk:kk + 1, :], start, 0.0), axis=0, keepdims=True) for kk in range(TOP_K)]
    d_ref[...] = jnp.concatenate(rows, axis=0).astype(I32) + rank_ref[...]


def _dests(eidx_t, rank_t, pstart, T=2048):
    N = eidx_t.shape[1]
    T = min(T, N)
    E = pstart.shape[0]
    blk = pl.BlockSpec((TOP_K, T), lambda i: (0, i))
    return pl.pallas_call(
        _dest_kernel,
        grid=(N // T,),
        in_specs=[blk, blk, pl.BlockSpec((E, LANES), lambda i: (0, 0))],
        out_specs=blk,
        out_shape=jax.ShapeDtypeStruct((TOP_K, N), I32),
        compiler_params=_cp("parallel"),
        name="dests",
    )(eidx_t, rank_t, jnp.broadcast_to(pstart.astype(F32)[:, None], (E, LANES)))


def _expert_kernel(us_ref, ps_ref, x_hbm, wug_hbm, wd_hbm, y_hbm,
                   wug_f32, wd_f32, wug_bf, wd_bf, xbuf, ybuf, cnt_ref, wsem, xsem, ysem, *, n_rows):
    e = pl.program_id(0)
    n_experts = pl.num_programs(0)
    _, _, R, Ch = xbuf.shape
    C = 2 * Ch
    F = wd_hbm.shape[1]

    def w_copies(expert, slot):
        return (pltpu.make_async_copy(wug_hbm.at[expert], wug_f32.at[slot], wsem.at[slot]),
                pltpu.make_async_copy(wd_hbm.at[expert], wd_f32.at[slot], wsem.at[slot]))

    def x_copy(slot, half, row):
        return pltpu.make_async_copy(x_hbm.at[half, pl.ds(row, R)], xbuf.at[slot, half], xsem.at[slot])

    def y_copy(slot, half, row):
        return pltpu.make_async_copy(ybuf.at[slot, half], y_hbm.at[half, pl.ds(row, R)], ysem.at[slot])

    def start(copy, slot, row):
        for half in range(2):
            copy(slot, half, row).start()

    def wait(copy, slot):
        for half in range(2):
            copy(slot, half, 0).wait()

    @pl.when(e == 0)
    def _():
        for ahead in range(W_SLOTS - 1):
            for cp in w_copies(ahead, ahead):
                cp.start()
        cnt_ref[0] = 0
        for t in range(X_AHEAD):
            start(x_copy, t, t * R)
        ybuf[...] = jnp.zeros_like(ybuf)
        for slot in range(2):
            start(y_copy, slot, n_rows + slot * R)

    @pl.when(e + (W_SLOTS - 1) < n_experts)
    def _():
        for cp in w_copies(e + (W_SLOTS - 1), lax.rem(e + (W_SLOTS - 1), W_SLOTS)):
            cp.start()

    wslot = lax.rem(e, W_SLOTS)
    for cp in w_copies(e, wslot):
        cp.wait()
    wug_bf[...] = wug_f32[wslot].astype(BF16)
    wd_bf[...] = wd_f32[wslot].astype(BF16)
    n_valid = us_ref[e + 1] - us_ref[e]
    pbase = ps_ref[e]
    n_tiles = lax.shift_right_logical(n_valid + (R - 1), int(math.log2(R)))

    def make_tile(nb):
        rowid = lax.broadcasted_iota(I32, (nb * R, C), 0)

        def tile(blk0):
            n_done = cnt_ref[0]
            xslots = [lax.rem(n_done + b, X_SLOTS) for b in range(nb)]
            for b in range(nb):
                ahead = n_done + X_AHEAD + b
                start(x_copy, lax.rem(ahead, X_SLOTS), pl.multiple_of(ahead * R, R))
            for b in range(nb):
                wait(x_copy, xslots[b])
            x = jnp.concatenate([jnp.concatenate([xbuf[xs, 0], xbuf[xs, 1]], axis=1) for xs in xslots], axis=0)
            x = jnp.where(rowid < n_valid - blk0 * R, x, jnp.uint32(0))
            lo, hi = _unpack_halves(x)
            gu = (jnp.dot(lo.astype(BF16), wug_bf[0:C, :], preferred_element_type=F32)
                  + jnp.dot(hi.astype(BF16), wug_bf[C:, :], preferred_element_type=F32))
            hid = (_silu(gu[:, :F]) * gu[:, F:]).astype(BF16)
            y = jnp.dot(hid, wd_bf[...], preferred_element_type=F32)
            packed = _pack_halves(y.astype(BF16).astype(F32))
            for b in range(nb):
                slot = (n_done + b) & 1
                wait(y_copy, slot)
                ybuf[slot, 0] = packed[b * R:(b + 1) * R, :Ch]
                ybuf[slot, 1] = packed[b * R:(b + 1) * R, Ch:]
                start(y_copy, slot, pl.multiple_of(pbase + (blk0 + b) * R, R))
            cnt_ref[0] = n_done + nb

        return tile

    pair_tile, single_tile = make_tile(2), make_tile(1)

    def pairs(j, carry):
        pair_tile(2 * j)
        return carry

    lax.fori_loop(0, lax.shift_right_logical(n_tiles, 1), pairs, 0)

    @pl.when((n_tiles & 1) == 1)
    def _():
        single_tile(n_tiles - 1)

    @pl.when(e == pl.num_programs(0) - 1)
    def _():
        for t in range(X_AHEAD):
            wait(x_copy, lax.rem(cnt_ref[0] + t, X_SLOTS))
        for slot in range(2):
            wait(y_copy, slot)
        ybuf[0] = jnp.zeros((2, R, Ch), U32)
        first = lax.shift_right_logical(pbase + n_tiles * R, int(math.log2(R)))
        n_left = n_rows // R - first

        def fill(t, carry):
            start(y_copy, 0, pl.multiple_of((first + t) * R, R))
            return carry

        def drain(t, carry):
            wait(y_copy, 0)
            return carry

        lax.fori_loop(0, n_left, fill, 0)
        lax.fori_loop(0, n_left, drain, 0)


def _experts(ustart, pstart, xg, w_ug, w_d, n_rows, R):
    _, _, Ch = xg.shape
    E, D, F2 = w_ug.shape
    F = w_d.shape[1]
    grid_spec = pltpu.PrefetchScalarGridSpec(
        num_scalar_prefetch=2,
        grid=(E,),
        in_specs=[pl.BlockSpec(memory_space=pl.ANY)] * 3,
        out_specs=pl.BlockSpec(memory_space=pl.ANY),
        scratch_shapes=[pltpu.VMEM((W_SLOTS, D, F2), F32), pltpu.VMEM((W_SLOTS, F, D), F32),
                        pltpu.VMEM((D, F2), BF16), pltpu.VMEM((F, D), BF16),
                        pltpu.VMEM((X_SLOTS, 2, R, Ch), U32), pltpu.VMEM((2, 2, R, Ch), U32), pltpu.SMEM((1,), I32),
                        pltpu.SemaphoreType.DMA((W_SLOTS,)), pltpu.SemaphoreType.DMA((X_SLOTS,)),
                        pltpu.SemaphoreType.DMA((2,))],
    )
    return pl.pallas_call(
        functools.partial(_expert_kernel, n_rows=n_rows),
        grid_spec=grid_spec,
        out_shape=jax.ShapeDtypeStruct((2, n_rows + 2 * R, Ch), U32),
        compiler_params=_cp("arbitrary"),
        name="experts",
    )(ustart, pstart, xg, w_ug, w_d)


def _sc_scatter_rows(src, idx, n_rows):
    H, N, C = src.shape
    K = idx.shape[0]
    per_row = N // SC_WINDOW
    mesh = plsc.VectorSubcoreMesh(core_axis_name="c", subcore_axis_name="s")

    @functools.partial(pl.kernel, out_type=jax.ShapeDtypeStruct((H, n_rows, C), src.dtype), mesh=mesh,
                       scratch_types=[])
    def scatter_kernel(x_hbm, i_hbm, o_hbm):
        for h in range(H):
            def body(x_vmem, i_vmem):
                for k in range(K):
                    pltpu.sync_copy(x_vmem, o_hbm.at[h].at[i_vmem.at[k]])

            pltpu.emit_pipeline(
                body,
                grid=(per_row,),
                in_specs=[pl.BlockSpec((SC_WINDOW, C), lambda i: (i, 0)),
                          pl.BlockSpec((K, SC_WINDOW), lambda i: (0, i))],
                out_specs=[],
                core_axis_name=("c", "s"),
                dimension_semantics=(pltpu.PARALLEL,),
            )(x_hbm.at[h], i_hbm)

    return scatter_kernel(src, idx)


def _sc_gather_rows(src, idx):
    H, _, C = src.shape
    K, N = idx.shape
    per_row = N // SC_WINDOW
    mesh = plsc.VectorSubcoreMesh(core_axis_name="c", subcore_axis_name="s")

    @functools.partial(pl.kernel, out_type=jax.ShapeDtypeStruct((H, K * N, C), src.dtype), mesh=mesh,
                       scratch_types=[])
    def gather_kernel(x_hbm, i_hbm, o_hbm):
        for h in range(H):
            def body(i_vmem, o_vmem):
                pltpu.sync_copy(x_hbm.at[h].at[i_vmem.at[0]], o_vmem)

            pltpu.emit_pipeline(
                body,
                grid=(K * per_row,),
                in_specs=[pl.BlockSpec((1, SC_WINDOW), lambda i: (i // per_row, i % per_row))],
                out_specs=[pl.BlockSpec((SC_WINDOW, C), lambda i: (i, 0))],
                core_axis_name=("c", "s"),
                dimension_semantics=(pltpu.PARALLEL,),
            )(i_hbm, o_hbm.at[h])

    return gather_kernel(src, idx)


def _combine_kernel(*refs):
    y_refs = refs[:2 * TOP_K]
    w_ref, base_ref, mod_ref, o_ref = refs[2 * TOP_K:]
    T = base_ref.shape[0]
    tr = lax.broadcasted_iota(I32, (T, T), 0)
    tc = lax.broadcasted_iota(I32, (T, T), 1)
    wcol = _nt((tr == tc).astype(F32), w_ref[...], precision=HI)
    acc = [None] * 4
    for kk in range(TOP_K):
        wk = wcol[:, kk:kk + 1]
        for half in range(2):
            lo, hi = _unpack_halves(y_refs[2 * kk + half][...])
            for q, val in ((half, lo), (2 + half, hi)):
                acc[q] = val * wk if acc[q] is None else acc[q] + val * wk
    o_ref[...] = base_ref[...] + mod_ref[5:6, :] * jnp.concatenate(acc, axis=1)


def _combine(yg, w_t, base, mod, S, T=512):
    N, D = base.shape
    C = yg.shape[2]
    tpb = S // T
    n_tiles = N // T
    row = pl.BlockSpec((T, D), lambda i: (i, 0))
    piece = lambda kk, half: pl.BlockSpec((None, T, C), lambda i: (half, kk * n_tiles + i, 0))
    return pl.pallas_call(
        _combine_kernel,
        grid=(n_tiles,),
        in_specs=[piece(kk, half) for kk in range(TOP_K) for half in range(2)] + [
            pl.BlockSpec((TOP_K, T), lambda i: (0, i)),
            row,
            pl.BlockSpec((None, 6, D), lambda i: (i // tpb, 0, 0))],
        out_specs=row,
        out_shape=jax.ShapeDtypeStruct((N, D), F32),
        compiler_params=_cp("parallel"),
        name="combine",
    )(*([yg] * (2 * TOP_K)), w_t, base, mod)


def _layer(x, c, positions, layer_idx, w_ada, b_ada, norm1_g, w_in, w_gate, b_gate,
           q_norm_g, k_norm_g, lambda_q1, lambda_k1, lambda_q2, lambda_k2, subln_g,
           rwkv_mu, w_decay0, w_decay2, a0, a2, g2, k_k, k_a, r_k, ln_x_w, ln_x_b,
           w_branch_a, w_branch_b, w_out, norm2_g, w_router, router_bias,
           w_expert_up_gate, w_expert_down, w_shared_up_gate, w_shared_down):
    B, S, D = x.shape
    N = B * S
    E = w_router.shape[1]
    da_width = w_branch_a.shape[0]
    rw_width = w_branch_b.shape[0]
    lambda_init = 0.8 - 0.6 * math.exp(-0.3 * layer_idx)

    mod = _adaln(c, w_ada, b_ada)
    x2 = x.reshape(N, D)
    w_cat = jnp.concatenate([w_in, w_gate], axis=1).astype(BF16)
    qn, kn, v, r_, k_, v_, a_, ld_, g_, gate = _mixer_in(
        x2, positions.reshape(N, 1), mod, norm1_g, w_cat, b_gate, q_norm_g, k_norm_g,
        rwkv_mu, w_decay0, w_decay2, a0, a2, g2, S, da_width, rw_width)

    lam_vecs = jnp.stack([lambda_q1, lambda_k1, lambda_q2, lambda_k2])
    score_bound = 1.01 * DA_HEAD_DIM ** 0.5 * jnp.max(jnp.abs(q_norm_g)) * jnp.max(jnp.abs(k_norm_g))
    attn = _diff_attention(qn, kn, v, score_bound, lam_vecs, subln_g, B, S, lambda_init)

    seq = lambda t: t.reshape(B, S, rw_width)
    rw = _rwkv_scan(seq(r_), seq(k_), seq(v_), seq(a_), seq(ld_), seq(g_), k_k, k_a, r_k.reshape(-1),
                    ln_x_w, ln_x_b).reshape(N, rw_width)

    base, h2p, logits_t = _post(attn, rw, gate, x2, mod, w_branch_a.astype(BF16), w_branch_b.astype(BF16),
                                w_out.astype(BF16), norm2_g, w_router.T,
                                w_shared_up_gate.astype(BF16), w_shared_down.astype(BF16), S)

    eidx_t, w_t = _route(logits_t, router_bias)
    rank_t, counts = _ranks(eidx_t, E)
    R = EXPERT_TILE
    cnt = counts[:, 0].astype(I32)
    ustart = jnp.concatenate([jnp.zeros((1,), I32), jnp.cumsum(cnt)])
    pcnt = (cnt + R - 1) // R * R
    pstart = jnp.cumsum(pcnt) - pcnt
    dest_p = _dests(eidx_t, rank_t, pstart)
    n_rows = (N * TOP_K + E * (R - 1) + R - 1) // R * R
    xg = _sc_scatter_rows(h2p, dest_p, n_rows + X_AHEAD * R)
    y = _experts(ustart, pstart, xg, w_expert_up_gate, w_expert_down, n_rows, R)
    yg = _sc_gather_rows(y, dest_p)
    out = _combine(yg, w_t, base, mod, S)
    return out.reshape(B, S, D)


def kernel(x, c, positions, w_ada, b_ada, norm1_g, w_in, w_gate, b_gate, q_norm_g, k_norm_g, lambda_q1, lambda_k1, lambda_q2, lambda_k2, subln_g, rwkv_mu, w_decay0, w_decay2, a0, a2, g2, k_k, k_a, r_k, ln_x_w, ln_x_b, w_branch_a, w_branch_b, w_out, norm2_g, w_router, router_bias, w_expert_up_gate, w_expert_down, w_shared_up_gate, w_shared_down):
    for l in range(w_ada.shape[0]):
        x = _layer(x, c, positions, l, w_ada[l], b_ada[l], norm1_g[l], w_in[l], w_gate[l], b_gate[l],
                   q_norm_g[l], k_norm_g[l], lambda_q1[l], lambda_k1[l], lambda_q2[l], lambda_k2[l],
                   subln_g[l], rwkv_mu[l], w_decay0[l], w_decay2[l], a0[l], a2[l], g2[l], k_k[l],
                   k_a[l], r_k[l], ln_x_w[l], ln_x_b[l], w_branch_a[l], w_branch_b[l], w_out[l],
                   norm2_g[l], w_router[l], router_bias[l], w_expert_up_gate[l], w_expert_down[l],
                   w_shared_up_gate[l], w_shared_down[l])
    return x
```

```python
import functools
import math

import jax
import jax.numpy as jnp
from jax import lax
from jax.experimental import pallas as pl
from jax.experimental.pallas import tpu as pltpu
from jax.experimental.pallas import tpu_sc as plsc

F32 = jnp.float32
BF16 = jnp.bfloat16
I32 = jnp.int32
U32 = jnp.uint32
HI = lax.Precision.HIGHEST

CHUNK = 64
ROPE_THETA = 10000.0
NORM_EPS = 1e-6
SUBLN_EPS = 1e-5
DA_HEAD_DIM = 64
RWKV_HEAD = 64
GN_EPS = 64e-5
TOP_K = 8
N_GROUPS = 8
TOPK_GROUPS = 4
ROUTED_SCALE = 2.5
EXPERT_TILE = 256
W_SLOTS = 3
X_SLOTS = 6
X_AHEAD = X_SLOTS - 2
RWKV_CHUNK = 64
LANES = 128
SC_WINDOW = 128
NEG = -1e30
MAX_PLAIN_SCORE = 40.0
VMEM_LIMIT = 56 * 1024 * 1024


def _cp(*sem):
    return pltpu.CompilerParams(dimension_semantics=sem, vmem_limit_bytes=VMEM_LIMIT)


def _bdot(a, b):
    return jnp.dot(a.astype(BF16), b.astype(BF16), preferred_element_type=F32)


def _fdot(a, b):
    return jnp.dot(a, b, precision=HI, preferred_element_type=F32)


def _nt(a, b, precision=None):
    return lax.dot_general(a, b, (((1,), (1,)), ((), ())), precision=precision,
                           preferred_element_type=F32)


def _tn(a, b, precision=None):
    return lax.dot_general(a, b, (((0,), (0,)), ((), ())), precision=precision,
                           preferred_element_type=F32)


def _pack_halves(x):
    c = x.shape[1] // 2
    lo = lax.bitcast_convert_type(x[:, :c], U32)
    hi = lax.bitcast_convert_type(x[:, c:], U32)
    return (hi & jnp.uint32(0xFFFF0000)) | (lo >> 16)


def _unpack_halves(w):
    lo = lax.bitcast_convert_type(w << 16, F32)
    hi = lax.bitcast_convert_type(w & jnp.uint32(0xFFFF0000), F32)
    return lo, hi


def _sigmoid(x):
    return 1.0 / (1.0 + jnp.exp(-x))


def _silu(x):
    return x * _sigmoid(x)


def _ada_kernel(c_ref, w_ref, b_ref, o_ref):
    o_ref[...] = _fdot(_silu(c_ref[...]), w_ref[...]) + b_ref[...]


def _adaln(c, w_ada, b_ada):
    B, D = c.shape
    rows = -(-B // 8) * 8
    cpad = jnp.zeros((rows, D), F32).at[:B].set(c)
    n_out = w_ada.shape[1]
    out = pl.pallas_call(
        _ada_kernel,
        grid=(n_out // D,),
        in_specs=[pl.BlockSpec((rows, D), lambda j: (0, 0)),
                  pl.BlockSpec((D, D), lambda j: (0, j)),
                  pl.BlockSpec((1, D), lambda j: (0, j))],
        out_specs=pl.BlockSpec((rows, D), lambda j: (0, j)),
        out_shape=jax.ShapeDtypeStruct((rows, n_out), F32),
        compiler_params=_cp("arbitrary"),
        name="adaln",
    )(cpad, w_ada, b_ada.reshape(1, n_out))
    return out[:B].reshape(B, n_out // D, D)


def _mixer_in_kernel(x_ref, xprev_ref, mod_ref, g_ref, w_ref, bg_ref, pos_ref, invf_ref, qg_ref, kg_ref,
                     mu_ref, w0_ref, w2_ref, a0_ref, a2_ref, g2_ref,
                     qn_ref, kn_ref, v_ref, r_ref, k_ref, vr_ref, a_ref, ld_ref, gr_ref, gate_ref,
                     *, da_width, rw_cols, rw_width, q_scale, tiles_per_seq):
    tm = x_ref.shape[0]

    def modulated(x):
        y = x * lax.rsqrt(jnp.mean(x * x, axis=-1, keepdims=True) + NORM_EPS) * g_ref[...]
        return (y * (1.0 + mod_ref[1:2, :]) + mod_ref[0:1, :]).astype(BF16)

    def proj(hb, c0, width, step=512):
        parts = [jnp.dot(hb, w_ref[:, c0 + o:c0 + min(o + step, width)], preferred_element_type=F32)
                 for o in range(0, width, step)]
        return parts[0] if len(parts) == 1 else jnp.concatenate(parts, axis=1)

    h = modulated(x_ref[...])

    lane = lax.broadcasted_iota(I32, (tm, LANES), 1)
    first = lane < DA_HEAD_DIM
    lo_half = (lane & (DA_HEAD_DIM - 1)) < DA_HEAD_DIM // 2
    ang = pos_ref[...].astype(F32) * invf_ref[...]
    cos = jnp.cos(ang)
    sin = jnp.sin(ang)
    sin = jnp.where(lo_half, -sin, sin)
    for c0, dst, gn_ref, mult in ((0, qn_ref, qg_ref, q_scale), (da_width, kn_ref, kg_ref, 1.0)):
        raw = proj(h, c0, da_width)
        for blk in range(da_width // LANES):
            x = raw[:, blk * LANES:(blk + 1) * LANES]
            xx = x * x
            s_first = jnp.sum(jnp.where(first, xx, 0.0), axis=-1, keepdims=True)
            s_second = jnp.sum(jnp.where(first, 0.0, xx), axis=-1, keepdims=True)
            ms = jnp.where(first, s_first, s_second) * (1.0 / DA_HEAD_DIM)
            xn = x * lax.rsqrt(ms + NORM_EPS) * gn_ref[...]
            rot = jnp.where(lo_half, pltpu.roll(xn, LANES - DA_HEAD_DIM // 2, axis=1),
                            pltpu.roll(xn, DA_HEAD_DIM // 2, axis=1))
            dst[:, blk * LANES:(blk + 1) * LANES] = ((xn * cos + rot * sin) * mult).astype(dst.dtype)
    v_ref[...] = proj(h, 2 * da_width, da_width).astype(v_ref.dtype)

    c_rw = 3 * da_width
    p = proj(h, c_rw, rw_cols)
    p_before = proj(modulated(xprev_ref[...]), c_rw, rw_cols)
    seq_start = (pl.program_id(0) % tiles_per_seq) == 0
    last_prev = jnp.where(seq_start, 0.0, p_before[7:8, :])
    rowi = lax.broadcasted_iota(I32, p.shape, 0)
    prev = jnp.where(rowi == 0, last_prev, pltpu.roll(p, 1, axis=0))
    xs = p + (prev - p) * mu_ref[...]
    width = rw_width
    r_ref[...] = xs[:, 0:width]
    k_ref[...] = xs[:, width:2 * width]
    vr_ref[...] = xs[:, 2 * width:3 * width]
    xwa = xs[:, 3 * width:3 * width + LANES]
    xg = xs[:, 3 * width + LANES:]
    z = w0_ref[...] + _bdot(jnp.tanh(xwa), w2_ref[...])
    w = -(jnp.maximum(-z, 0.0) + jnp.log(1.0 + jnp.exp(-jnp.abs(z)))) - 0.5
    ld_ref[...] = -jnp.exp(w)
    a_ref[...] = _sigmoid(a0_ref[...] + _bdot(xwa, a2_ref[...]))
    gr_ref[...] = _bdot(_sigmoid(xg), g2_ref[...])

    gate_ref[...] = _sigmoid(proj(h, c_rw + rw_cols, gate_ref.shape[1]) + bg_ref[...]).astype(gate_ref.dtype)


def _mixer_in(x2, pos2, mod, norm1_g, w_cat, b_gate, q_norm_g, k_norm_g, mu, w_decay0, w_decay2, a0, a2, g2,
              S, da_width, rw_width, tm=512):
    N, D = x2.shape
    n_gate = b_gate.shape[0]
    rw_cols = mu.shape[0]
    tpb = S // tm
    d = DA_HEAD_DIM
    inv_freq = 1.0 / (ROPE_THETA ** (jnp.arange(0, d, 2, dtype=F32) / d))
    invf = jnp.tile(inv_freq, LANES // (d // 2)).reshape(1, LANES)
    dl, al = w_decay2.shape[0], a2.shape[0]
    assert dl + al == LANES and g2.shape[0] == LANES
    w2p = jnp.zeros((LANES, rw_width), F32).at[:dl].set(w_decay2)
    a2p = jnp.zeros((LANES, rw_width), F32).at[dl:].set(a2)
    kern = functools.partial(_mixer_in_kernel, da_width=da_width, rw_cols=rw_cols, rw_width=rw_width,
                             q_scale=d ** -0.5 * math.log2(math.e),
                             tiles_per_seq=tpb)
    row = lambda w: pl.BlockSpec((tm, w), lambda i: (i, 0))
    vec = lambda n: pl.BlockSpec((1, n), lambda i: (0, 0))
    mat = pl.BlockSpec((LANES, rw_width), lambda i: (0, 0))
    f32 = lambda w: jax.ShapeDtypeStruct((N, w), F32)
    bf16 = lambda w: jax.ShapeDtypeStruct((N, w), BF16)
    return pl.pallas_call(
        kern,
        grid=(N // tm,),
        in_specs=[row(D),
                  pl.BlockSpec((8, D), lambda i: (jnp.maximum(i * (tm // 8) - 1, 0), 0)),
                  pl.BlockSpec((None, 6, D), lambda i: (i // tpb, 0, 0)),
                  vec(D),
                  pl.BlockSpec(w_cat.shape, lambda i: (0, 0)),
                  vec(n_gate),
                  pl.BlockSpec((tm, 1), lambda i: (i, 0)),
                  vec(LANES), vec(LANES), vec(LANES),
                  vec(rw_cols), vec(rw_width), mat, vec(rw_width), mat, mat],
        out_specs=[row(da_width)] * 3 + [row(rw_width)] * 6 + [row(n_gate)],
        out_shape=[bf16(da_width)] * 3 + [f32(rw_width)] * 6 + [bf16(n_gate)],
        compiler_params=_cp("parallel"),
        name="mixer_in",
    )(x2, x2, mod, norm1_g.reshape(1, D), w_cat, b_gate.reshape(1, n_gate), pos2, invf,
      jnp.tile(q_norm_g, 2).reshape(1, LANES), jnp.tile(k_norm_g, 2).reshape(1, LANES),
      mu.reshape(1, rw_cols), w_decay0.reshape(1, rw_width), w2p, a0.reshape(1, rw_width), a2p, g2)


def _attn_kernel(flag_ref, q_ref, k_ref, v_ref, vt_ref, lam_ref, sg_ref, sgc_ref, o_ref,
                 qz_ref, m_ref, l_ref, acc_ref, lpt_ref, acct_ref, *, tq, lambda_init):
    i = pl.program_id(2)
    lane = lax.broadcasted_iota(I32, (tq, LANES), 1)
    q = q_ref[...]
    zero = jnp.zeros_like(q)
    qz_ref[0:tq, :] = jnp.where(lane < DA_HEAD_DIM, q, zero)
    qz_ref[tq:, :] = jnp.where(lane >= DA_HEAD_DIM, q, zero)
    bounded = flag_ref[0] == 1
    lv = lam_ref[...]
    lam = (jnp.exp(jnp.sum(lv[0:1] * lv[1:2], keepdims=True))
           - jnp.exp(jnp.sum(lv[2:3] * lv[3:4], keepdims=True)) + lambda_init)

    def run(step):
        def body(j, carry):
            step(j, False)
            return carry
        lax.fori_loop(0, i, body, 0)
        step(i, True)

    def probs_t(keys, queries, masked, q_mask):
        st = _nt(keys, queries)
        if masked:
            row = lax.broadcasted_iota(I32, st.shape, 0)
            col = lax.broadcasted_iota(I32, st.shape, 1)
            st = jnp.where((row // CHUNK) <= ((col & q_mask) // CHUNK), st, NEG)
        pt = jnp.exp2(st)
        part = pt[0:8, :]
        for g in range(1, st.shape[0] // 8):
            part = part + pt[8 * g:8 * g + 8, :]
        return pt.astype(BF16), part

    def plain_step(j, masked):
        off = pl.multiple_of(j * tq, tq)
        if not masked:
            pt, part = probs_t(k_ref[pl.ds(off, tq), :], qz_ref[...], False, 0)
            lpt_ref[...] += part
            acct_ref[...] += jnp.dot(vt_ref[:, pl.ds(off, tq)], pt, preferred_element_type=F32)
            return
        hq = tq // 2
        pt, part = probs_t(k_ref[pl.ds(off, hq), :], qz_ref[...], True, tq - 1)
        lpt_ref[...] += part
        acct_ref[...] += jnp.dot(vt_ref[:, pl.ds(off, hq)], pt, preferred_element_type=F32)
        off2 = pl.multiple_of(off + hq, hq)
        late_q = jnp.concatenate([qz_ref[hq:tq, :], qz_ref[tq + hq:, :]], axis=0)
        pt, part = probs_t(k_ref[pl.ds(off2, hq), :], late_q, True, hq - 1)
        upd = jnp.dot(vt_ref[:, pl.ds(off2, hq)], pt, preferred_element_type=F32)
        for m in range(2):
            cols = slice(m * tq + hq, (m + 1) * tq)
            lpt_ref[:, cols] += part[:, m * hq:(m + 1) * hq]
            acct_ref[:, cols] += upd[:, m * hq:(m + 1) * hq]

    @pl.when(bounded)
    def _():
        lpt_ref[...] = jnp.zeros_like(lpt_ref)
        acct_ref[...] = jnp.zeros_like(acct_ref)

        def two_steps(p, carry):
            plain_step(2 * p, False)
            plain_step(2 * p + 1, False)
            return carry

        lax.fori_loop(0, lax.shift_right_logical(i, 1), two_steps, 0)

        @pl.when((i & 1) == 1)
        def _():
            plain_step(i - 1, False)
            plain_step(i, True)

        @pl.when((i & 1) == 0)
        def _():
            plain_step(i, True)

        lsum = jnp.sum(lpt_ref[...], axis=0, keepdims=True)
        ot = acct_ref[:, 0:tq] / lsum[:, 0:tq] - lam * (acct_ref[:, tq:] / lsum[:, tq:])
        ot = ot * lax.rsqrt(jnp.mean(ot * ot, axis=0, keepdims=True) + SUBLN_EPS) * sgc_ref[...]
        o_ref[...] = (ot * (1.0 - lambda_init)).T.astype(o_ref.dtype)

    def online_step(j, masked):
        off = pl.multiple_of(j * tq, tq)
        s = _nt(qz_ref[...], k_ref[pl.ds(off, tq), :])
        if masked:
            row = lax.broadcasted_iota(I32, s.shape, 0)
            col = lax.broadcasted_iota(I32, s.shape, 1)
            s = jnp.where((col // CHUNK) <= ((row & (tq - 1)) // CHUNK), s, NEG)
        m_old = m_ref[...]
        m_new = jnp.maximum(m_old, jnp.max(s, axis=-1, keepdims=True))
        alpha = jnp.exp2(m_old - m_new)
        pr = jnp.exp2(s - m_new)
        l_ref[...] = alpha * l_ref[...] + jnp.sum(pr, axis=-1, keepdims=True)
        acc_ref[...] = alpha * acc_ref[...] + jnp.dot(pr.astype(BF16), v_ref[pl.ds(off, tq), :],
                                                      preferred_element_type=F32)
        m_ref[...] = m_new

    @pl.when(jnp.logical_not(bounded))
    def _():
        m_ref[...] = jnp.full_like(m_ref, NEG)
        l_ref[...] = jnp.zeros_like(l_ref)
        acc_ref[...] = jnp.zeros_like(acc_ref)
        run(online_step)
        o = acc_ref[0:tq, :] / l_ref[0:tq, :] - lam * (acc_ref[tq:, :] / l_ref[tq:, :])
        o = o * lax.rsqrt(jnp.mean(o * o, axis=-1, keepdims=True) + SUBLN_EPS) * sg_ref[...]
        o_ref[...] = (o * (1.0 - lambda_init)).astype(o_ref.dtype)


def _diff_attention(qn, kn, v, score_bound, lam_vecs, subln_g, B, S, lambda_init, tq=1024):
    W = qn.shape[1]
    H = W // LANES
    tq = min(tq, S)
    q3 = qn.reshape(B, S, W)
    k3 = kn.reshape(B, S, W)
    v3 = v.reshape(B, S, W)
    vt3 = v3.transpose(0, 2, 1)
    flag = (score_bound <= MAX_PLAIN_SCORE).astype(I32).reshape(1)
    qblk = pl.BlockSpec((None, tq, LANES), lambda b, h, i, f: (b, i, h))
    kvblk = pl.BlockSpec((None, S, LANES), lambda b, h, i, f: (b, 0, h))
    grid_spec = pltpu.PrefetchScalarGridSpec(
        num_scalar_prefetch=1,
        grid=(B, H, S // tq),
        in_specs=[qblk, kvblk, kvblk,
                  pl.BlockSpec((None, LANES, S), lambda b, h, i, f: (b, h, 0)),
                  pl.BlockSpec((4, DA_HEAD_DIM), lambda b, h, i, f: (0, 0)),
                  pl.BlockSpec((1, LANES), lambda b, h, i, f: (0, 0)),
                  pl.BlockSpec((LANES, 1), lambda b, h, i, f: (0, 0))],
        out_specs=qblk,
        scratch_shapes=[pltpu.VMEM((2 * tq, LANES), BF16),
                        pltpu.VMEM((2 * tq, 1), F32),
                        pltpu.VMEM((2 * tq, 1), F32),
                        pltpu.VMEM((2 * tq, LANES), F32),
                        pltpu.VMEM((8, 2 * tq), F32),
                        pltpu.VMEM((LANES, 2 * tq), F32)],
    )
    out = pl.pallas_call(
        functools.partial(_attn_kernel, tq=tq, lambda_init=lambda_init),
        grid_spec=grid_spec,
        out_shape=jax.ShapeDtypeStruct((B, S, W), BF16),
        compiler_params=_cp("parallel", "parallel", "arbitrary"),
        name="diff_attn",
    )(flag, q3, k3, v3, vt3, lam_vecs, subln_g.reshape(1, LANES), subln_g.reshape(LANES, 1))
    return out.reshape(B * S, W)


def _stackmask(m):
    lane = lax.broadcasted_iota(I32, m.shape, 1)
    z = jnp.zeros_like(m)
    return jnp.concatenate([jnp.where(lane < RWKV_HEAD, m, z), jnp.where(lane >= RWKV_HEAD, m, z)], axis=0)


def _pair_sum(x, first):
    s1 = jnp.sum(jnp.where(first, x, 0.0), axis=-1, keepdims=True)
    s2 = jnp.sum(jnp.where(first, 0.0, x), axis=-1, keepdims=True)
    return jnp.where(first, s1, s2)


def _rwkv_scan_kernel(r_ref, k_ref, v_ref, a_ref, ld_ref, g_ref, kk_ref, ka_ref, rk_ref, lnw_ref, lnb_ref,
                      o_ref, s_ref, *, L):
    nb, tb, W = r_ref.shape
    tm = nb * tb
    n_chunks = tm // L
    per_batch = tb // L
    n_pairs = W // LANES
    hd = RWKV_HEAD
    bf = lambda t: t.astype(BF16)
    rows2d = lambda ref: ref[...].reshape(tm, W)

    @pl.when(pl.program_id(0) == 0)
    def _():
        s_ref[...] = jnp.zeros_like(s_ref)

    row = lax.broadcasted_iota(I32, (tm, tm), 0)
    col = lax.broadcasted_iota(I32, (tm, tm), 1)
    tri = jnp.where(jnp.logical_and(col <= row, (col // L) == (row // L)), 1.0, 0.0).astype(BF16)
    ld = rows2d(ld_ref)
    ld_hi = bf(ld)
    rem = ld - ld_hi.astype(F32)
    ld_mid = bf(rem)
    ld_lo = bf(rem - ld_mid.astype(F32))
    c = (jnp.dot(tri, ld_hi, preferred_element_type=F32) + jnp.dot(tri, ld_mid, preferred_element_type=F32)
         + jnp.dot(tri, ld_lo, preferred_element_type=F32))
    ec = jnp.exp(c)
    eci = jnp.exp(-c)
    ecm = jnp.exp(c - ld)
    r = rows2d(r_ref)
    k = rows2d(k_ref)
    v = rows2d(v_ref)
    a = rows2d(a_ref)
    kkr = k * kk_ref[...]
    kmod = k * (1.0 + (a - 1.0) * ka_ref[...])
    brk = r * kmod * rk_ref[...]

    lane = lax.broadcasted_iota(I32, (L, LANES), 1)
    rowl = lax.broadcasted_iota(I32, (L, LANES), 0)
    first = lane < hd
    lane_h = lane & (hd - 1)
    strict = lane_h < rowl
    incl = lane_h <= rowl
    eye = jnp.where(lane_h == rowl, 1.0, 0.0)

    chains = [(ch, p) for ch in range(n_chunks) for p in range(n_pairs)]
    rsl = lambda ch: slice(ch * L, (ch + 1) * L)
    csl = lambda p: slice(p * LANES, (p + 1) * LANES)
    fdot = lambda x, y: jnp.dot(x, y, preferred_element_type=F32)
    at, bt, kt, rt, vh, g_l = {}, {}, {}, {}, {}, {}
    for c_ in chains:
        ch, p = c_
        rs, cs = rsl(ch), csl(p)
        kkh = kkr[rs, cs]
        kkh = kkh / jnp.maximum(jnp.sqrt(_pair_sum(kkh * kkh, first)), 1e-12)
        vh[c_] = v[rs, cs]
        g_l[c_] = ec[ch * L + L - 1:ch * L + L, cs]
        at[c_] = -kkh * ecm[rs, cs]
        bt[c_] = kkh * a[rs, cs] * eci[rs, cs]
        kt[c_] = kmod[rs, cs] * eci[rs, cs]
        rt[c_] = r[rs, cs] * ec[rs, cs]
    gm = {c_: _nt(bf(jnp.concatenate([at[c_], rt[c_]], axis=0)),
                  jnp.concatenate([_stackmask(bf(bt[c_])), _stackmask(bf(kt[c_]))], axis=0)) for c_ in chains}
    a_ab = {c_: jnp.where(strict, gm[c_][:L, :LANES], 0.0) for c_ in chains}
    vsm = {c_: _stackmask(bf(vh[c_])) for c_ in chains}
    cmat = {c_: fdot(bf(jnp.where(strict, gm[c_][:L, LANES:], 0.0)), vsm[c_]) for c_ in chains}
    t_inv = {c_: eye + a_ab[c_] for c_ in chains}
    pw = {c_: bf(a_ab[c_]) for c_ in chains}
    for _ in range(int(math.log2(L)) - 1):
        pw = {c_: bf(fdot(pw[c_], _stackmask(pw[c_]))) for c_ in chains}
        t_inv = {c_: t_inv[c_] + fdot(pw[c_], _stackmask(bf(t_inv[c_]))) for c_ in chains}
    zz = {c_: fdot(bf(t_inv[c_]), jnp.concatenate([_stackmask(bf(at[c_])), _stackmask(bf(cmat[c_]))], axis=1))
          for c_ in chains}
    qy = {c_: fdot(bf(jnp.where(incl, gm[c_][L:, :LANES], 0.0)),
                   jnp.concatenate([_stackmask(bf(zz[c_][:, :LANES])), _stackmask(bf(zz[c_][:, LANES:]))], axis=1))
          for c_ in chains}
    y0 = {c_: qy[c_][:, LANES:] + fdot(bf(jnp.where(incl, gm[c_][L:, LANES:], 0.0)), vsm[c_]) for c_ in chains}
    qa = {c_: bf(jnp.concatenate([rt[c_] + qy[c_][:, :LANES], zz[c_][:, :LANES]], axis=0)) for c_ in chains}
    bkg = {c_: bf(jnp.concatenate([bt[c_] * g_l[c_], kt[c_] * g_l[c_]], axis=0)) for c_ in chains}

    lane_s = lax.broadcasted_iota(I32, (hd, LANES), 1)
    heads = [(b, p) for b in range(nb) for p in range(n_pairs)]
    sp = {bp: s_ref[bp[0] * n_pairs + bp[1]] for bp in heads}
    for j in range(per_batch):
        chunk = lambda b: b * per_batch + j
        yw = {(b, p): _nt(qa[chunk(b), p], _stackmask(bf(sp[b, p]))) for b, p in heads}
        upd = {(b, p): _tn(bf(jnp.concatenate([yw[b, p][L:] + zz[chunk(b), p][:, LANES:], vh[chunk(b), p]],
                                              axis=0)), bkg[chunk(b), p]) for b, p in heads}
        for b, p in heads:
            ch, cs, ts = chunk(b), csl(p), rsl(j)
            sp[b, p] = sp[b, p] * g_l[ch, p] + jnp.where(lane_s < hd, upd[b, p][:hd], upd[b, p][hd:])
            y = yw[b, p][:L] + y0[ch, p]
            mean = _pair_sum(y, first) * (1.0 / hd)
            yc = y - mean
            var = _pair_sum(yc * yc, first) * (1.0 / hd)
            yn = yc * lax.rsqrt(var + GN_EPS) * lnw_ref[:, cs] + lnb_ref[:, cs]
            bonus = _pair_sum(brk[rsl(ch), cs], first) * vh[ch, p]
            o_ref[b, ts, cs] = ((yn + bonus) * g_ref[b, ts, cs]).astype(o_ref.dtype)
    for b, p in heads:
        s_ref[b * n_pairs + p] = sp[b, p]


def _rwkv_scan(r, k, v, a, ld, g, k_k, k_a, r_k, ln_w, ln_b, L=RWKV_CHUNK, tm=128):
    B, S, W = r.shape
    seq = pl.BlockSpec((B, tm, W), lambda c: (0, c, 0))
    vec = pl.BlockSpec((1, W), lambda c: (0, 0))
    return pl.pallas_call(
        functools.partial(_rwkv_scan_kernel, L=L),
        grid=(S // tm,),
        in_specs=[seq] * 6 + [vec] * 5,
        out_specs=seq,
        out_shape=jax.ShapeDtypeStruct((B, S, W), BF16),
        scratch_shapes=[pltpu.VMEM((B * (W // LANES), RWKV_HEAD, LANES), F32)],
        compiler_params=_cp("arbitrary"),
        name="rwkv_scan",
    )(r, k, v, a, ld, g, k_k.reshape(1, W), k_a.reshape(1, W), r_k.reshape(1, W),
      ln_w.reshape(1, W), ln_b.reshape(1, W))


def _post_kernel(attn_ref, rw_ref, gate_ref, x_ref, mod_ref, wa_ref, wb_ref, wo_ref, g2_ref, wrh_ref, wrm_ref,
                 sug_ref, sd_ref, base_ref, h2p_ref, lg_ref):
    D = x_ref.shape[1]
    ya = jnp.dot(attn_ref[...], wa_ref[...], preferred_element_type=F32)
    yb = jnp.dot(rw_ref[...], wb_ref[...], preferred_element_type=F32)
    m = gate_ref[:, 0:D] * ya + gate_ref[:, D:] * yb
    x1 = x_ref[...] + mod_ref[2:3, :] * jnp.dot(m.astype(BF16), wo_ref[...], preferred_element_type=F32)
    y = x1 * lax.rsqrt(jnp.mean(x1 * x1, axis=-1, keepdims=True) + NORM_EPS) * g2_ref[...]
    h2 = y * (1.0 + mod_ref[4:5, :]) + mod_ref[3:4, :]
    hb = h2.astype(BF16)
    hm = (h2 - hb.astype(F32)).astype(BF16)
    lg_ref[...] = _nt(wrh_ref[...], hb) + _nt(wrh_ref[...], hm) + _nt(wrm_ref[...], hb)
    packed = _pack_halves(hb.astype(F32))
    half = packed.shape[1] // 2
    h2p_ref[0] = packed[:, :half]
    h2p_ref[1] = packed[:, half:]
    F = sd_ref.shape[0]
    gu = jnp.dot(hb, sug_ref[...], preferred_element_type=F32)
    shared = jnp.dot((_silu(gu[:, :F]) * gu[:, F:]).astype(BF16), sd_ref[...], preferred_element_type=F32)
    base_ref[...] = x1 + mod_ref[5:6, :] * shared


def _post(attn, rw, gate, x2, mod, wa, wb, wo, norm2_g, w_router_t, sug, sd, S, tm=512):
    N, D = x2.shape
    E = w_router_t.shape[0]
    wr_hi = w_router_t.astype(BF16)
    wr_mid = (w_router_t - wr_hi.astype(F32)).astype(BF16)
    tpb = S // tm
    row = lambda w: pl.BlockSpec((tm, w), lambda i: (i, 0))
    full = lambda a: pl.BlockSpec(a.shape, lambda i: (0, 0))
    return pl.pallas_call(
        _post_kernel,
        grid=(N // tm,),
        in_specs=[row(attn.shape[1]), row(rw.shape[1]), row(gate.shape[1]), row(D),
                  pl.BlockSpec((None, 6, D), lambda i: (i // tpb, 0, 0)),
                  full(wa), full(wb), full(wo), pl.BlockSpec((1, D), lambda i: (0, 0)), full(wr_hi), full(wr_mid),
                  full(sug), full(sd)],
        out_specs=[row(D), pl.BlockSpec((2, tm, D // 4), lambda i: (0, i, 0)), pl.BlockSpec((E, tm), lambda i: (0, i))],
        out_shape=[jax.ShapeDtypeStruct((N, D), F32), jax.ShapeDtypeStruct((2, N, D // 4), U32),
                   jax.ShapeDtypeStruct((E, N), F32)],
        compiler_params=_cp("parallel"),
        name="post_mixer",
    )(attn, rw, gate, x2, mod, wa, wb, wo, norm2_g.reshape(1, D), wr_hi, wr_mid, sug, sd)


def _first_argmax(x, idx, sentinel):
    m = jnp.max(x, axis=0, keepdims=True)
    return m, jnp.min(jnp.where(x == m, idx, sentinel), axis=0, keepdims=True)


def _route_kernel(lg_ref, bias_ref, e_ref, w_ref):
    E, T = lg_ref.shape
    gsz = E // N_GROUPS
    scores = _sigmoid(lg_ref[...])
    biased = scores + bias_ref[...]
    ig = lax.broadcasted_iota(I32, (gsz, T), 0)
    gs = []
    for g in range(N_GROUPS):
        blk = biased[g * gsz:(g + 1) * gsz, :]
        m1, i1 = _first_argmax(blk, ig, gsz)
        m2 = jnp.max(jnp.where(ig == i1, -jnp.inf, blk), axis=0, keepdims=True)
        gs.append(m1 + m2)
    gsc = jnp.concatenate(gs, axis=0)
    i8 = lax.broadcasted_iota(I32, (N_GROUPS, T), 0)
    chosen = jnp.zeros((N_GROUPS, T), F32)
    for _ in range(TOPK_GROUPS):
        _, gi = _first_argmax(gsc, i8, N_GROUPS)
        hit = i8 == gi
        chosen = jnp.where(hit, 1.0, chosen)
        gsc = jnp.where(hit, -jnp.inf, gsc)
    masked = jnp.concatenate(
        [jnp.where(chosen[g:g + 1, :] > 0.0, biased[g * gsz:(g + 1) * gsz, :], -jnp.inf)
         for g in range(N_GROUPS)], axis=0)
    ie = lax.broadcasted_iota(I32, (E, T), 0)
    idxs, wts = [], []
    for _ in range(TOP_K):
        _, ei = _first_argmax(masked, ie, E)
        hit = ie == ei
        idxs.append(ei)
        wts.append(jnp.sum(jnp.where(hit, scores, 0.0), axis=0, keepdims=True))
        masked = jnp.where(hit, -jnp.inf, masked)
    wt = jnp.concatenate(wts, axis=0)
    e_ref[...] = jnp.concatenate(idxs, axis=0)
    w_ref[...] = wt / jnp.sum(wt, axis=0, keepdims=True) * ROUTED_SCALE


def _route(logits_t, router_bias, T=512):
    E, N = logits_t.shape
    blk = pl.BlockSpec((TOP_K, T), lambda i: (0, i))
    return pl.pallas_call(
        _route_kernel,
        grid=(N // T,),
        in_specs=[pl.BlockSpec((E, T), lambda i: (0, i)), pl.BlockSpec((E, 1), lambda i: (0, 0))],
        out_specs=[blk, blk],
        out_shape=[jax.ShapeDtypeStruct((TOP_K, N), I32), jax.ShapeDtypeStruct((TOP_K, N), F32)],
        compiler_params=_cp("parallel"),
        name="route",
    )(logits_t, router_bias.reshape(E, 1))


def _rank_kernel(e_ref, rank_ref, cnt_ref, carry_ref, *, n_experts):
    T = e_ref.shape[1]

    @pl.when(pl.program_id(0) == 0)
    def _():
        carry_ref[...] = jnp.zeros_like(carry_ref)

    ie = lax.broadcasted_iota(I32, (n_experts, T), 0)
    e = e_ref[...]
    hits = [ie == e[kk:kk + 1, :] for kk in range(TOP_K)]
    onehot = jnp.zeros((n_experts, T), F32)
    for hsel in hits:
        onehot = onehot + hsel.astype(F32)
    tr = lax.broadcasted_iota(I32, (T, T), 0)
    tc = lax.broadcasted_iota(I32, (T, T), 1)
    before = (tr < tc).astype(BF16)
    base = _bdot(onehot, before) + carry_ref[:, 0:1]
    rank_ref[...] = jnp.concatenate(
        [jnp.sum(jnp.where(hsel, base, 0.0), axis=0, keepdims=True) for hsel in hits], axis=0).astype(I32)
    carry_ref[...] = carry_ref[...] + jnp.sum(onehot, axis=1, keepdims=True)
    cnt_ref[...] = carry_ref[...]


def _ranks(eidx_t, n_experts, T=512):
    N = eidx_t.shape[1]
    T = min(T, N)
    blk = pl.BlockSpec((TOP_K, T), lambda i: (0, i))
    cnt = pl.BlockSpec((n_experts, LANES), lambda i: (0, 0))
    return pl.pallas_call(
        functools.partial(_rank_kernel, n_experts=n_experts),
        grid=(N // T,),
        in_specs=[blk],
        out_specs=[blk, cnt],
        out_shape=[jax.ShapeDtypeStruct((TOP_K, N), I32), jax.ShapeDtypeStruct((n_experts, LANES), F32)],
        scratch_shapes=[pltpu.VMEM((n_experts, LANES), F32)],
        compiler_params=_cp("arbitrary"),
        name="ranks",
    )(eidx_t)


def _dest_kernel(e_ref, rank_ref, start_ref, d_ref):
    E = start_ref.shape[0]
    T = e_ref.shape[1]
    ie = lax.broadcasted_iota(I32, (E, T), 0)
    e = e_ref[...]
    start = start_ref[:, 0:1]
    rows = [jnp.sum(jnp.where(ie == e[kk:kk + 1, :], start, 0.0), axis=0, keepdims=True) for kk in range(TOP_K)]
    d_ref[...] = jnp.concatenate(rows, axis=0).astype(I32) + rank_ref[...]


def _dests(eidx_t, rank_t, pstart, T=2048):
    N = eidx_t.shape[1]
    T = min(T, N)
    E = pstart.shape[0]
    blk = pl.BlockSpec((TOP_K, T), lambda i: (0, i))
    return pl.pallas_call(
        _dest_kernel,
        grid=(N // T,),
        in_specs=[blk, blk, pl.BlockSpec((E, LANES), lambda i: (0, 0))],
        out_specs=blk,
        out_shape=jax.ShapeDtypeStruct((TOP_K, N), I32),
        compiler_params=_cp("parallel"),
        name="dests",
    )(eidx_t, rank_t, jnp.broadcast_to(pstart.astype(F32)[:, None], (E, LANES)))


def _expert_kernel(us_ref, ps_ref, x_hbm, wug_hbm, wd_hbm, y_hbm,
                   wug_f32, wd_f32, wug_bf, wd_bf, xbuf, ybuf, cnt_ref, wsem, xsem, ysem, *, n_rows):
    e = pl.program_id(0)
    n_experts = pl.num_programs(0)
    _, _, R, Ch = xbuf.shape
    C = 2 * Ch
    F = wd_hbm.shape[1]

    def w_copies(expert, slot):
        return (pltpu.make_async_copy(wug_hbm.at[expert], wug_f32.at[slot], wsem.at[slot]),
                pltpu.make_async_copy(wd_hbm.at[expert], wd_f32.at[slot], wsem.at[slot]))

    def x_copy(slot, half, row):
        return pltpu.make_async_copy(x_hbm.at[half, pl.ds(row, R)], xbuf.at[slot, half], xsem.at[slot])

    def y_copy(slot, half, row):
        return pltpu.make_async_copy(ybuf.at[slot, half], y_hbm.at[half, pl.ds(row, R)], ysem.at[slot])

    def start(copy, slot, row):
        for half in range(2):
            copy(slot, half, row).start()

    def wait(copy, slot):
        for half in range(2):
            copy(slot, half, 0).wait()

    @pl.when(e == 0)
    def _():
        for ahead in range(W_SLOTS - 1):
            for cp in w_copies(ahead, ahead):
                cp.start()
        cnt_ref[0] = 0
        for t in range(X_AHEAD):
            start(x_copy, t, t * R)
        ybuf[...] = jnp.zeros_like(ybuf)
        for slot in range(2):
            start(y_copy, slot, n_rows + slot * R)

    @pl.when(e + (W_SLOTS - 1) < n_experts)
    def _():
        for cp in w_copies(e + (W_SLOTS - 1), lax.rem(e + (W_SLOTS - 1), W_SLOTS)):
            cp.start()

    wslot = lax.rem(e, W_SLOTS)
    for cp in w_copies(e, wslot):
        cp.wait()
    wug_bf[...] = wug_f32[wslot].astype(BF16)
    wd_bf[...] = wd_f32[wslot].astype(BF16)
    n_valid = us_ref[e + 1] - us_ref[e]
    pbase = ps_ref[e]
    n_tiles = lax.shift_right_logical(n_valid + (R - 1), int(math.log2(R)))

    def make_tile(nb):
        rowid = lax.broadcasted_iota(I32, (nb * R, C), 0)

        def tile(blk0):
            n_done = cnt_ref[0]
            xslots = [lax.rem(n_done + b, X_SLOTS) for b in range(nb)]
            for b in range(nb):
                ahead = n_done + X_AHEAD + b
                start(x_copy, lax.rem(ahead, X_SLOTS), pl.multiple_of(ahead * R, R))
            for b in range(nb):
                wait(x_copy, xslots[b])
            x = jnp.concatenate([jnp.concatenate([xbuf[xs, 0], xbuf[xs, 1]], axis=1) for xs in xslots], axis=0)
            x = jnp.where(rowid < n_valid - blk0 * R, x, jnp.uint32(0))
            lo, hi = _unpack_halves(x)
            gu = (jnp.dot(lo.astype(BF16), wug_bf[0:C, :], preferred_element_type=F32)
                  + jnp.dot(hi.astype(BF16), wug_bf[C:, :], preferred_element_type=F32))
            hid = (_silu(gu[:, :F]) * gu[:, F:]).astype(BF16)
            y = jnp.dot(hid, wd_bf[...], preferred_element_type=F32)
            packed = _pack_halves(y.astype(BF16).astype(F32))
            for b in range(nb):
                slot = (n_done + b) & 1
                wait(y_copy, slot)
                ybuf[slot, 0] = packed[b * R:(b + 1) * R, :Ch]
                ybuf[slot, 1] = packed[b * R:(b + 1) * R, Ch:]
                start(y_copy, slot, pl.multiple_of(pbase + (blk0 + b) * R, R))
            cnt_ref[0] = n_done + nb

        return tile

    pair_tile, single_tile = make_tile(2), make_tile(1)

    def pairs(j, carry):
        pair_tile(2 * j)
        return carry

    lax.fori_loop(0, lax.shift_right_logical(n_tiles, 1), pairs, 0)

    @pl.when((n_tiles & 1) == 1)
    def _():
        single_tile(n_tiles - 1)

    @pl.when(e == pl.num_programs(0) - 1)
    def _():
        for t in range(X_AHEAD):
            wait(x_copy, lax.rem(cnt_ref[0] + t, X_SLOTS))
        for slot in range(2):
            wait(y_copy, slot)
        ybuf[0] = jnp.zeros((2, R, Ch), U32)
        first = lax.shift_right_logical(pbase + n_tiles * R, int(math.log2(R)))
        n_left = n_rows // R - first

        def fill(t, carry):
            start(y_copy, 0, pl.multiple_of((first + t) * R, R))
            return carry

        def drain(t, carry):
            wait(y_copy, 0)
            return carry

        lax.fori_loop(0, n_left, fill, 0)
        lax.fori_loop(0, n_left, drain, 0)


def _experts(ustart, pstart, xg, w_ug, w_d, n_rows, R):
    _, _, Ch = xg.shape
    E, D, F2 = w_ug.shape
    F = w_d.shape[1]
    grid_spec = pltpu.PrefetchScalarGridSpec(
        num_scalar_prefetch=2,
        grid=(E,),
        in_specs=[pl.BlockSpec(memory_space=pl.ANY)] * 3,
        out_specs=pl.BlockSpec(memory_space=pl.ANY),
        scratch_shapes=[pltpu.VMEM((W_SLOTS, D, F2), F32), pltpu.VMEM((W_SLOTS, F, D), F32),
                        pltpu.VMEM((D, F2), BF16), pltpu.VMEM((F, D), BF16),
                        pltpu.VMEM((X_SLOTS, 2, R, Ch), U32), pltpu.VMEM((2, 2, R, Ch), U32), pltpu.SMEM((1,), I32),
                        pltpu.SemaphoreType.DMA((W_SLOTS,)), pltpu.SemaphoreType.DMA((X_SLOTS,)),
                        pltpu.SemaphoreType.DMA((2,))],
    )
    return pl.pallas_call(
        functools.partial(_expert_kernel, n_rows=n_rows),
        grid_spec=grid_spec,
        out_shape=jax.ShapeDtypeStruct((2, n_rows + 2 * R, Ch), U32),
        compiler_params=_cp("arbitrary"),
        name="experts",
    )(ustart, pstart, xg, w_ug, w_d)


def _sc_scatter_rows(src, idx, n_rows):
    H, N, C = src.shape
    K = idx.shape[0]
    per_row = N // SC_WINDOW
    mesh = plsc.VectorSubcoreMesh(core_axis_name="c", subcore_axis_name="s")

    @functools.partial(pl.kernel, out_type=jax.ShapeDtypeStruct((H, n_rows, C), src.dtype), mesh=mesh,
                       scratch_types=[])
    def scatter_kernel(x_hbm, i_hbm, o_hbm):
        for h in range(H):
            def body(x_vmem, i_vmem):
                for k in range(K):
                    pltpu.sync_copy(x_vmem, o_hbm.at[h].at[i_vmem.at[k]])

            pltpu.emit_pipeline(
                body,
                grid=(per_row,),
                in_specs=[pl.BlockSpec((SC_WINDOW, C), lambda i: (i, 0)),
                          pl.BlockSpec((K, SC_WINDOW), lambda i: (0, i))],
                out_specs=[],
                core_axis_name=("c", "s"),
                dimension_semantics=(pltpu.PARALLEL,),
            )(x_hbm.at[h], i_hbm)

    return scatter_kernel(src, idx)


def _sc_gather_rows(src, idx):
    H, _, C = src.shape
    K, N = idx.shape
    per_row = N // SC_WINDOW
    mesh = plsc.VectorSubcoreMesh(core_axis_name="c", subcore_axis_name="s")

    @functools.partial(pl.kernel, out_type=jax.ShapeDtypeStruct((H, K * N, C), src.dtype), mesh=mesh,
                       scratch_types=[])
    def gather_kernel(x_hbm, i_hbm, o_hbm):
        for h in range(H):
            def body(i_vmem, o_vmem):
                pltpu.sync_copy(x_hbm.at[h].at[i_vmem.at[0]], o_vmem)

            pltpu.emit_pipeline(
                body,
                grid=(K * per_row,),
                in_specs=[pl.BlockSpec((1, SC_WINDOW), lambda i: (i // per_row, i % per_row))],
                out_specs=[pl.BlockSpec((SC_WINDOW, C), lambda i: (i, 0))],
                core_axis_name=("c", "s"),
                dimension_semantics=(pltpu.PARALLEL,),
            )(i_hbm, o_hbm.at[h])

    return gather_kernel(src, idx)


def _combine_kernel(*refs):
    y_refs = refs[:2 * TOP_K]
    w_ref, base_ref, mod_ref, o_ref = refs[2 * TOP_K:]
    T = base_ref.shape[0]
    tr = lax.broadcasted_iota(I32, (T, T), 0)
    tc = lax.broadcasted_iota(I32, (T, T), 1)
    wcol = _nt((tr == tc).astype(F32), w_ref[...], precision=HI)
    acc = [None] * 4
    for kk in range(TOP_K):
        wk = wcol[:, kk:kk + 1]
        for half in range(2):
            lo, hi = _unpack_halves(y_refs[2 * kk + half][...])
            for q, val in ((half, lo), (2 + half, hi)):
                acc[q] = val * wk if acc[q] is None else acc[q] + val * wk
    o_ref[...] = base_ref[...] + mod_ref[5:6, :] * jnp.concatenate(acc, axis=1)


def _combine(yg, w_t, base, mod, S, T=512):
    N, D = base.shape
    C = yg.shape[2]
    tpb = S // T
    n_tiles = N // T
    row = pl.BlockSpec((T, D), lambda i: (i, 0))
    piece = lambda kk, half: pl.BlockSpec((None, T, C), lambda i: (half, kk * n_tiles + i, 0))
    return pl.pallas_call(
        _combine_kernel,
        grid=(n_tiles,),
        in_specs=[piece(kk, half) for kk in range(TOP_K) for half in range(2)] + [
            pl.BlockSpec((TOP_K, T), lambda i: (0, i)),
            row,
            pl.BlockSpec((None, 6, D), lambda i: (i // tpb, 0, 0))],
        out_specs=row,
        out_shape=jax.ShapeDtypeStruct((N, D), F32),
        compiler_params=_cp("parallel"),
        name="combine",
    )(*([yg] * (2 * TOP_K)), w_t, base, mod)


def _layer(x, c, positions, layer_idx, w_ada, b_ada, norm1_g, w_in, w_gate, b_gate,
           q_norm_g, k_norm_g, lambda_q1, lambda_k1, lambda_q2, lambda_k2, subln_g,
           rwkv_mu, w_decay0, w_decay2, a0, a2, g2, k_k, k_a, r_k, ln_x_w, ln_x_b,
           w_branch_a, w_branch_b, w_out, norm2_g, w_router, router_bias,
           w_expert_up_gate, w_expert_down, w_shared_up_gate, w_shared_down):
    B, S, D = x.shape
    N = B * S
    E = w_router.shape[1]
    da_width = w_branch_a.shape[0]
    rw_width = w_branch_b.shape[0]
    lambda_init = 0.8 - 0.6 * math.exp(-0.3 * layer_idx)

    mod = _adaln(c, w_ada, b_ada)
    x2 = x.reshape(N, D)
    w_cat = jnp.concatenate([w_in, w_gate], axis=1).astype(BF16)
    qn, kn, v, r_, k_, v_, a_, ld_, g_, gate = _mixer_in(
        x2, positions.reshape(N, 1), mod, norm1_g, w_cat, b_gate, q_norm_g, k_norm_g,
        rwkv_mu, w_decay0, w_decay2, a0, a2, g2, S, da_width, rw_width)

    lam_vecs = jnp.stack([lambda_q1, lambda_k1, lambda_q2, lambda_k2])
    score_bound = 1.01 * DA_HEAD_DIM ** 0.5 * jnp.max(jnp.abs(q_norm_g)) * jnp.max(jnp.abs(k_norm_g))
    attn = _diff_attention(qn, kn, v, score_bound, lam_vecs, subln_g, B, S, lambda_init)

    seq = lambda t: t.reshape(B, S, rw_width)
    rw = _rwkv_scan(seq(r_), seq(k_), seq(v_), seq(a_), seq(ld_), seq(g_), k_k, k_a, r_k.reshape(-1),
                    ln_x_w, ln_x_b).reshape(N, rw_width)

    base, h2p, logits_t = _post(attn, rw, gate, x2, mod, w_branch_a.astype(BF16), w_branch_b.astype(BF16),
                                w_out.astype(BF16), norm2_g, w_router.T,
                                w_shared_up_gate.astype(BF16), w_shared_down.astype(BF16), S)

    eidx_t, w_t = _route(logits_t, router_bias)
    rank_t, counts = _ranks(eidx_t, E)
    R = EXPERT_TILE
    cnt = counts[:, 0].astype(I32)
    ustart = jnp.concatenate([jnp.zeros((1,), I32), jnp.cumsum(cnt)])
    pcnt = (cnt + R - 1) // R * R
    pstart = jnp.cumsum(pcnt) - pcnt
    dest_p = _dests(eidx_t, rank_t, pstart)
    n_rows = (N * TOP_K + E * (R - 1) + R - 1) // R * R
    xg = _sc_scatter_rows(h2p, dest_p, n_rows + X_AHEAD * R)
    y = _experts(ustart, pstart, xg, w_expert_up_gate, w_expert_down, n_rows, R)
    yg = _sc_gather_rows(y, dest_p)
    out = _combine(yg, w_t, base, mod, S)
    return out.reshape(B, S, D)


def kernel(x, c, positions, w_ada, b_ada, norm1_g, w_in, w_gate, b_gate, q_norm_g, k_norm_g, lambda_q1, lambda_k1, lambda_q2, lambda_k2, subln_g, rwkv_mu, w_decay0, w_decay2, a0, a2, g2, k_k, k_a, r_k, ln_x_w, ln_x_b, w_branch_a, w_branch_b, w_out, norm2_g, w_router, router_bias, w_expert_up_gate, w_expert_down, w_shared_up_gate, w_shared_down):
    for l in range(w_ada.shape[0]):
        x = _layer(x, c, positions, l, w_ada[l], b_ada[l], norm1_g[l], w_in[l], w_gate[l], b_gate[l],
                   q_norm_g[l], k_norm_g[l], lambda_q1[l], lambda_k1[l], lambda_q2[l], lambda_k2[l],
                   subln_g[l], rwkv_mu[l], w_decay0[l], w_decay2[l], a0[l], a2[l], g2[l], k_k[l],
                   k_a[l], r_k[l], ln_x_w[l], ln_x_b[l], w_branch_a[l], w_branch_b[l], w_out[l],
                   norm2_g[l], w_router[l], router_bias[l], w_expert_up_gate[l], w_expert_down[l],
                   w_shared_up_gate[l], w_shared_down[l])
    return x
```

```python
import functools
import math

import jax
import jax.numpy as jnp
from jax import lax
from jax.experimental import pallas as pl
from jax.experimental.pallas import tpu as pltpu
from jax.experimental.pallas import tpu_sc as plsc

F32 = jnp.float32
BF16 = jnp.bfloat16
I32 = jnp.int32
U32 = jnp.uint32
HI = lax.Precision.HIGHEST

CHUNK = 64
ROPE_THETA = 10000.0
NORM_EPS = 1e-6
SUBLN_EPS = 1e-5
DA_HEAD_DIM = 64
RWKV_HEAD = 64
GN_EPS = 64e-5
TOP_K = 8
N_GROUPS = 8
TOPK_GROUPS = 4
ROUTED_SCALE = 2.5
EXPERT_TILE = 256
W_SLOTS = 3
TILE_BLOCKS = 3
X_SLOTS = 8
X_AHEAD = X_SLOTS - TILE_BLOCKS
Y_SLOTS = 4
RWKV_CHUNK = 64
LANES = 128
SC_WINDOW = 128
NEG = -1e30
MAX_PLAIN_SCORE = 40.0
VMEM_LIMIT = 56 * 1024 * 1024


def _cp(*sem):
    return pltpu.CompilerParams(dimension_semantics=sem, vmem_limit_bytes=VMEM_LIMIT)


def _bdot(a, b):
    return jnp.dot(a.astype(BF16), b.astype(BF16), preferred_element_type=F32)


def _fdot(a, b):
    return jnp.dot(a, b, precision=HI, preferred_element_type=F32)


def _nt(a, b, precision=None):
    return lax.dot_general(a, b, (((1,), (1,)), ((), ())), precision=precision,
                           preferred_element_type=F32)


def _tn(a, b, precision=None):
    return lax.dot_general(a, b, (((0,), (0,)), ((), ())), precision=precision,
                           preferred_element_type=F32)


def _pack_halves(x):
    c = x.shape[1] // 2
    lo = lax.bitcast_convert_type(x[:, :c], U32)
    hi = lax.bitcast_convert_type(x[:, c:], U32)
    return (hi & jnp.uint32(0xFFFF0000)) | (lo >> 16)


def _unpack_halves(w):
    lo = lax.bitcast_convert_type(w << 16, F32)
    hi = lax.bitcast_convert_type(w & jnp.uint32(0xFFFF0000), F32)
    return lo, hi


def _sigmoid(x):
    return 1.0 / (1.0 + jnp.exp(-x))


def _silu(x):
    return x * _sigmoid(x)


def _ada_kernel(c_ref, w_ref, b_ref, o_ref):
    o_ref[...] = _fdot(_silu(c_ref[...]), w_ref[...]) + b_ref[...]


def _adaln(c, w_ada, b_ada):
    B, D = c.shape
    rows = -(-B // 8) * 8
    cpad = jnp.zeros((rows, D), F32).at[:B].set(c)
    n_out = w_ada.shape[1]
    out = pl.pallas_call(
        _ada_kernel,
        grid=(n_out // D,),
        in_specs=[pl.BlockSpec((rows, D), lambda j: (0, 0)),
                  pl.BlockSpec((D, D), lambda j: (0, j)),
                  pl.BlockSpec((1, D), lambda j: (0, j))],
        out_specs=pl.BlockSpec((rows, D), lambda j: (0, j)),
        out_shape=jax.ShapeDtypeStruct((rows, n_out), F32),
        compiler_params=_cp("arbitrary"),
        name="adaln",
    )(cpad, w_ada, b_ada.reshape(1, n_out))
    return out[:B].reshape(B, n_out // D, D)


def _mixer_in_kernel(x_ref, xprev_ref, mod_ref, g_ref, w_ref, bg_ref, pos_ref, invf_ref, qg_ref, kg_ref,
                     mu_ref, w0_ref, w2_ref, a0_ref, a2_ref, g2_ref,
                     qn_ref, kn_ref, v_ref, r_ref, k_ref, vr_ref, a_ref, ld_ref, gr_ref, gate_ref,
                     *, da_width, rw_cols, rw_width, q_scale, tiles_per_seq):
    tm = x_ref.shape[0]

    def modulated(x):
        y = x * lax.rsqrt(jnp.mean(x * x, axis=-1, keepdims=True) + NORM_EPS) * g_ref[...]
        return (y * (1.0 + mod_ref[1:2, :]) + mod_ref[0:1, :]).astype(BF16)

    def proj(hb, c0, width, step=512):
        parts = [jnp.dot(hb, w_ref[:, c0 + o:c0 + min(o + step, width)], preferred_element_type=F32)
                 for o in range(0, width, step)]
        return parts[0] if len(parts) == 1 else jnp.concatenate(parts, axis=1)

    h = modulated(x_ref[...])

    lane = lax.broadcasted_iota(I32, (tm, LANES), 1)
    first = lane < DA_HEAD_DIM
    lo_half = (lane & (DA_HEAD_DIM - 1)) < DA_HEAD_DIM // 2
    ang = pos_ref[...].astype(F32) * invf_ref[...]
    cos = jnp.cos(ang)
    sin = jnp.sin(ang)
    sin = jnp.where(lo_half, -sin, sin)
    for c0, dst, gn_ref, mult in ((0, qn_ref, qg_ref, q_scale), (da_width, kn_ref, kg_ref, 1.0)):
        raw = proj(h, c0, da_width)
        for blk in range(da_width // LANES):
            x = raw[:, blk * LANES:(blk + 1) * LANES]
            xx = x * x
            s_first = jnp.sum(jnp.where(first, xx, 0.0), axis=-1, keepdims=True)
            s_second = jnp.sum(jnp.where(first, 0.0, xx), axis=-1, keepdims=True)
            ms = jnp.where(first, s_first, s_second) * (1.0 / DA_HEAD_DIM)
            xn = x * lax.rsqrt(ms + NORM_EPS) * gn_ref[...]
            rot = jnp.where(lo_half, pltpu.roll(xn, LANES - DA_HEAD_DIM // 2, axis=1),
                            pltpu.roll(xn, DA_HEAD_DIM // 2, axis=1))
            dst[:, blk * LANES:(blk + 1) * LANES] = ((xn * cos + rot * sin) * mult).astype(dst.dtype)
    v_ref[...] = proj(h, 2 * da_width, da_width).astype(v_ref.dtype)

    c_rw = 3 * da_width
    p = proj(h, c_rw, rw_cols)
    p_before = proj(modulated(xprev_ref[...]), c_rw, rw_cols)
    seq_start = (pl.program_id(0) % tiles_per_seq) == 0
    last_prev = jnp.where(seq_start, 0.0, p_before[7:8, :])
    rowi = lax.broadcasted_iota(I32, p.shape, 0)
    prev = jnp.where(rowi == 0, last_prev, pltpu.roll(p, 1, axis=0))
    xs = p + (prev - p) * mu_ref[...]
    width = rw_width
    r_ref[...] = xs[:, 0:width]
    k_ref[...] = xs[:, width:2 * width]
    vr_ref[...] = xs[:, 2 * width:3 * width]
    xwa = xs[:, 3 * width:3 * width + LANES]
    xg = xs[:, 3 * width + LANES:]
    z = w0_ref[...] + _bdot(jnp.tanh(xwa), w2_ref[...])
    w = -(jnp.maximum(-z, 0.0) + jnp.log(1.0 + jnp.exp(-jnp.abs(z)))) - 0.5
    ld_ref[...] = -jnp.exp(w)
    a_ref[...] = _sigmoid(a0_ref[...] + _bdot(xwa, a2_ref[...]))
    gr_ref[...] = _bdot(_sigmoid(xg), g2_ref[...])

    gate_ref[...] = _sigmoid(proj(h, c_rw + rw_cols, gate_ref.shape[1]) + bg_ref[...]).astype(gate_ref.dtype)


def _mixer_in(x2, pos2, mod, norm1_g, w_cat, b_gate, q_norm_g, k_norm_g, mu, w_decay0, w_decay2, a0, a2, g2,
              S, da_width, rw_width, tm=512):
    N, D = x2.shape
    n_gate = b_gate.shape[0]
    rw_cols = mu.shape[0]
    tpb = S // tm
    d = DA_HEAD_DIM
    inv_freq = 1.0 / (ROPE_THETA ** (jnp.arange(0, d, 2, dtype=F32) / d))
    invf = jnp.tile(inv_freq, LANES // (d // 2)).reshape(1, LANES)
    dl, al = w_decay2.shape[0], a2.shape[0]
    assert dl + al == LANES and g2.shape[0] == LANES
    w2p = jnp.zeros((LANES, rw_width), F32).at[:dl].set(w_decay2)
    a2p = jnp.zeros((LANES, rw_width), F32).at[dl:].set(a2)
    kern = functools.partial(_mixer_in_kernel, da_width=da_width, rw_cols=rw_cols, rw_width=rw_width,
                             q_scale=d ** -0.5 * math.log2(math.e),
                             tiles_per_seq=tpb)
    row = lambda w: pl.BlockSpec((tm, w), lambda i: (i, 0))
    vec = lambda n: pl.BlockSpec((1, n), lambda i: (0, 0))
    mat = pl.BlockSpec((LANES, rw_width), lambda i: (0, 0))
    f32 = lambda w: jax.ShapeDtypeStruct((N, w), F32)
    bf16 = lambda w: jax.ShapeDtypeStruct((N, w), BF16)
    return pl.pallas_call(
        kern,
        grid=(N // tm,),
        in_specs=[row(D),
                  pl.BlockSpec((8, D), lambda i: (jnp.maximum(i * (tm // 8) - 1, 0), 0)),
                  pl.BlockSpec((None, 6, D), lambda i: (i // tpb, 0, 0)),
                  vec(D),
                  pl.BlockSpec(w_cat.shape, lambda i: (0, 0)),
                  vec(n_gate),
                  pl.BlockSpec((tm, 1), lambda i: (i, 0)),
                  vec(LANES), vec(LANES), vec(LANES),
                  vec(rw_cols), vec(rw_width), mat, vec(rw_width), mat, mat],
        out_specs=[row(da_width)] * 3 + [row(rw_width)] * 6 + [row(n_gate)],
        out_shape=[bf16(da_width)] * 3 + [f32(rw_width)] * 6 + [bf16(n_gate)],
        compiler_params=_cp("parallel"),
        name="mixer_in",
    )(x2, x2, mod, norm1_g.reshape(1, D), w_cat, b_gate.reshape(1, n_gate), pos2, invf,
      jnp.tile(q_norm_g, 2).reshape(1, LANES), jnp.tile(k_norm_g, 2).reshape(1, LANES),
      mu.reshape(1, rw_cols), w_decay0.reshape(1, rw_width), w2p, a0.reshape(1, rw_width), a2p, g2)


def _attn_kernel(flag_ref, q_ref, k_ref, v_ref, vt_ref, lam_ref, sg_ref, sgc_ref, o_ref,
                 qz_ref, m_ref, l_ref, acc_ref, lpt_ref, acct_ref, *, tq, lambda_init):
    i = pl.program_id(2)
    lane = lax.broadcasted_iota(I32, (tq, LANES), 1)
    q = q_ref[...]
    zero = jnp.zeros_like(q)
    qz_ref[0:tq, :] = jnp.where(lane < DA_HEAD_DIM, q, zero)
    qz_ref[tq:, :] = jnp.where(lane >= DA_HEAD_DIM, q, zero)
    bounded = flag_ref[0] == 1
    lv = lam_ref[...]
    lam = (jnp.exp(jnp.sum(lv[0:1] * lv[1:2], keepdims=True))
           - jnp.exp(jnp.sum(lv[2:3] * lv[3:4], keepdims=True)) + lambda_init)

    def run(step):
        def body(j, carry):
            step(j, False)
            return carry
        lax.fori_loop(0, i, body, 0)
        step(i, True)

    def probs_t(keys, queries, masked, q_mask):
        st = _nt(keys, queries)
        if masked:
            row = lax.broadcasted_iota(I32, st.shape, 0)
            col = lax.broadcasted_iota(I32, st.shape, 1)
            st = jnp.where((row // CHUNK) <= ((col & q_mask) // CHUNK), st, NEG)
        pt = jnp.exp2(st)
        part = pt[0:8, :]
        for g in range(1, st.shape[0] // 8):
            part = part + pt[8 * g:8 * g + 8, :]
        return pt.astype(BF16), part

    def plain_step(j, masked):
        off = pl.multiple_of(j * tq, tq)
        if not masked:
            pt, part = probs_t(k_ref[pl.ds(off, tq), :], qz_ref[...], False, 0)
            lpt_ref[...] += part
            acct_ref[...] += jnp.dot(vt_ref[:, pl.ds(off, tq)], pt, preferred_element_type=F32)
            return
        hq = tq // 2
        pt, part = probs_t(k_ref[pl.ds(off, hq), :], qz_ref[...], True, tq - 1)
        lpt_ref[...] += part
        acct_ref[...] += jnp.dot(vt_ref[:, pl.ds(off, hq)], pt, preferred_element_type=F32)
        off2 = pl.multiple_of(off + hq, hq)
        late_q = jnp.concatenate([qz_ref[hq:tq, :], qz_ref[tq + hq:, :]], axis=0)
        pt, part = probs_t(k_ref[pl.ds(off2, hq), :], late_q, True, hq - 1)
        upd = jnp.dot(vt_ref[:, pl.ds(off2, hq)], pt, preferred_element_type=F32)
        for m in range(2):
            cols = slice(m * tq + hq, (m + 1) * tq)
            lpt_ref[:, cols] += part[:, m * hq:(m + 1) * hq]
            acct_ref[:, cols] += upd[:, m * hq:(m + 1) * hq]

    @pl.when(bounded)
    def _():
        lpt_ref[...] = jnp.zeros_like(lpt_ref)
        acct_ref[...] = jnp.zeros_like(acct_ref)

        def two_steps(p, carry):
            plain_step(2 * p, False)
            plain_step(2 * p + 1, False)
            return carry

        lax.fori_loop(0, lax.shift_right_logical(i, 1), two_steps, 0)

        @pl.when((i & 1) == 1)
        def _():
            plain_step(i - 1, False)
            plain_step(i, True)

        @pl.when((i & 1) == 0)
        def _():
            plain_step(i, True)

        lsum = jnp.sum(lpt_ref[...], axis=0, keepdims=True)
        ot = acct_ref[:, 0:tq] / lsum[:, 0:tq] - lam * (acct_ref[:, tq:] / lsum[:, tq:])
        ot = ot * lax.rsqrt(jnp.mean(ot * ot, axis=0, keepdims=True) + SUBLN_EPS) * sgc_ref[...]
        o_ref[...] = (ot * (1.0 - lambda_init)).T.astype(o_ref.dtype)

    def online_step(j, masked):
        off = pl.multiple_of(j * tq, tq)
        s = _nt(qz_ref[...], k_ref[pl.ds(off, tq), :])
        if masked:
            row = lax.broadcasted_iota(I32, s.shape, 0)
            col = lax.broadcasted_iota(I32, s.shape, 1)
            s = jnp.where((col // CHUNK) <= ((row & (tq - 1)) // CHUNK), s, NEG)
        m_old = m_ref[...]
        m_new = jnp.maximum(m_old, jnp.max(s, axis=-1, keepdims=True))
        alpha = jnp.exp2(m_old - m_new)
        pr = jnp.exp2(s - m_new)
        l_ref[...] = alpha * l_ref[...] + jnp.sum(pr, axis=-1, keepdims=True)
        acc_ref[...] = alpha * acc_ref[...] + jnp.dot(pr.astype(BF16), v_ref[pl.ds(off, tq), :],
                                                      preferred_element_type=F32)
        m_ref[...] = m_new

    @pl.when(jnp.logical_not(bounded))
    def _():
        m_ref[...] = jnp.full_like(m_ref, NEG)
        l_ref[...] = jnp.zeros_like(l_ref)
        acc_ref[...] = jnp.zeros_like(acc_ref)
        run(online_step)
        o = acc_ref[0:tq, :] / l_ref[0:tq, :] - lam * (acc_ref[tq:, :] / l_ref[tq:, :])
        o = o * lax.rsqrt(jnp.mean(o * o, axis=-1, keepdims=True) + SUBLN_EPS) * sg_ref[...]
        o_ref[...] = (o * (1.0 - lambda_init)).astype(o_ref.dtype)


def _diff_attention(qn, kn, v, score_bound, lam_vecs, subln_g, B, S, lambda_init, tq=1024):
    W = qn.shape[1]
    H = W // LANES
    tq = min(tq, S)
    q3 = qn.reshape(B, S, W)
    k3 = kn.reshape(B, S, W)
    v3 = v.reshape(B, S, W)
    vt3 = v3.transpose(0, 2, 1)
    flag = (score_bound <= MAX_PLAIN_SCORE).astype(I32).reshape(1)
    qblk = pl.BlockSpec((None, tq, LANES), lambda b, h, i, f: (b, i, h))
    kvblk = pl.BlockSpec((None, S, LANES), lambda b, h, i, f: (b, 0, h))
    grid_spec = pltpu.PrefetchScalarGridSpec(
        num_scalar_prefetch=1,
        grid=(B, H, S // tq),
        in_specs=[qblk, kvblk, kvblk,
                  pl.BlockSpec((None, LANES, S), lambda b, h, i, f: (b, h, 0)),
                  pl.BlockSpec((4, DA_HEAD_DIM), lambda b, h, i, f: (0, 0)),
                  pl.BlockSpec((1, LANES), lambda b, h, i, f: (0, 0)),
                  pl.BlockSpec((LANES, 1), lambda b, h, i, f: (0, 0))],
        out_specs=qblk,
        scratch_shapes=[pltpu.VMEM((2 * tq, LANES), BF16),
                        pltpu.VMEM((2 * tq, 1), F32),
                        pltpu.VMEM((2 * tq, 1), F32),
                        pltpu.VMEM((2 * tq, LANES), F32),
                        pltpu.VMEM((8, 2 * tq), F32),
                        pltpu.VMEM((LANES, 2 * tq), F32)],
    )
    out = pl.pallas_call(
        functools.partial(_attn_kernel, tq=tq, lambda_init=lambda_init),
        grid_spec=grid_spec,
        out_shape=jax.ShapeDtypeStruct((B, S, W), BF16),
        compiler_params=_cp("parallel", "parallel", "arbitrary"),
        name="diff_attn",
    )(flag, q3, k3, v3, vt3, lam_vecs, subln_g.reshape(1, LANES), subln_g.reshape(LANES, 1))
    return out.reshape(B * S, W)


def _stackmask(m):
    lane = lax.broadcasted_iota(I32, m.shape, 1)
    z = jnp.zeros_like(m)
    return jnp.concatenate([jnp.where(lane < RWKV_HEAD, m, z), jnp.where(lane >= RWKV_HEAD, m, z)], axis=0)


def _pair_sum(x, first):
    s1 = jnp.sum(jnp.where(first, x, 0.0), axis=-1, keepdims=True)
    s2 = jnp.sum(jnp.where(first, 0.0, x), axis=-1, keepdims=True)
    return jnp.where(first, s1, s2)


def _rwkv_scan_kernel(r_ref, k_ref, v_ref, a_ref, ld_ref, g_ref, kk_ref, ka_ref, rk_ref, lnw_ref, lnb_ref,
                      o_ref, s_ref, *, L):
    nb, tb, W = r_ref.shape
    tm = nb * tb
    n_chunks = tm // L
    per_batch = tb // L
    n_pairs = W // LANES
    hd = RWKV_HEAD
    bf = lambda t: t.astype(BF16)
    rows2d = lambda ref: ref[...].reshape(tm, W)

    @pl.when(pl.program_id(0) == 0)
    def _():
        s_ref[...] = jnp.zeros_like(s_ref)

    row = lax.broadcasted_iota(I32, (tm, tm), 0)
    col = lax.broadcasted_iota(I32, (tm, tm), 1)
    tri = jnp.where(jnp.logical_and(col <= row, (col // L) == (row // L)), 1.0, 0.0).astype(BF16)
    ld = rows2d(ld_ref)
    ld_hi = bf(ld)
    rem = ld - ld_hi.astype(F32)
    ld_mid = bf(rem)
    ld_lo = bf(rem - ld_mid.astype(F32))
    c = (jnp.dot(tri, ld_hi, preferred_element_type=F32) + jnp.dot(tri, ld_mid, preferred_element_type=F32)
         + jnp.dot(tri, ld_lo, preferred_element_type=F32))
    ec = jnp.exp(c)
    eci = jnp.exp(-c)
    ecm = jnp.exp(c - ld)
    r = rows2d(r_ref)
    k = rows2d(k_ref)
    v = rows2d(v_ref)
    a = rows2d(a_ref)
    kkr = k * kk_ref[...]
    kmod = k * (1.0 + (a - 1.0) * ka_ref[...])
    brk = r * kmod * rk_ref[...]

    lane = lax.broadcasted_iota(I32, (L, LANES), 1)
    rowl = lax.broadcasted_iota(I32, (L, LANES), 0)
    first = lane < hd
    lane_h = lane & (hd - 1)
    strict = lane_h < rowl
    incl = lane_h <= rowl
    eye = jnp.where(lane_h == rowl, 1.0, 0.0)

    chains = [(ch, p) for ch in range(n_chunks) for p in range(n_pairs)]
    rsl = lambda ch: slice(ch * L, (ch + 1) * L)
    csl = lambda p: slice(p * LANES, (p + 1) * LANES)
    fdot = lambda x, y: jnp.dot(x, y, preferred_element_type=F32)
    at, bt, kt, rt, vh, g_l = {}, {}, {}, {}, {}, {}
    for c_ in chains:
        ch, p = c_
        rs, cs = rsl(ch), csl(p)
        kkh = kkr[rs, cs]
        kkh = kkh / jnp.maximum(jnp.sqrt(_pair_sum(kkh * kkh, first)), 1e-12)
        vh[c_] = v[rs, cs]
        g_l[c_] = ec[ch * L + L - 1:ch * L + L, cs]
        at[c_] = -kkh * ecm[rs, cs]
        bt[c_] = kkh * a[rs, cs] * eci[rs, cs]
        kt[c_] = kmod[rs, cs] * eci[rs, cs]
        rt[c_] = r[rs, cs] * ec[rs, cs]
    gm = {c_: _nt(bf(jnp.concatenate([at[c_], rt[c_]], axis=0)),
                  jnp.concatenate([_stackmask(bf(bt[c_])), _stackmask(bf(kt[c_]))], axis=0)) for c_ in chains}
    a_ab = {c_: jnp.where(strict, gm[c_][:L, :LANES], 0.0) for c_ in chains}
    vsm = {c_: _stackmask(bf(vh[c_])) for c_ in chains}
    cmat = {c_: fdot(bf(jnp.where(strict, gm[c_][:L, LANES:], 0.0)), vsm[c_]) for c_ in chains}
    t_inv = {c_: eye + a_ab[c_] for c_ in chains}
    pw = {c_: bf(a_ab[c_]) for c_ in chains}
    for _ in range(int(math.log2(L)) - 1):
        pw = {c_: bf(fdot(pw[c_], _stackmask(pw[c_]))) for c_ in chains}
        t_inv = {c_: t_inv[c_] + fdot(pw[c_], _stackmask(bf(t_inv[c_]))) for c_ in chains}
    zz = {c_: fdot(bf(t_inv[c_]), jnp.concatenate([_stackmask(bf(at[c_])), _stackmask(bf(cmat[c_]))], axis=1))
          for c_ in chains}
    qy = {c_: fdot(bf(jnp.where(incl, gm[c_][L:, :LANES], 0.0)),
                   jnp.concatenate([_stackmask(bf(zz[c_][:, :LANES])), _stackmask(bf(zz[c_][:, LANES:]))], axis=1))
          for c_ in chains}
    y0 = {c_: qy[c_][:, LANES:] + fdot(bf(jnp.where(incl, gm[c_][L:, LANES:], 0.0)), vsm[c_]) for c_ in chains}
    qa = {c_: bf(jnp.concatenate([rt[c_] + qy[c_][:, :LANES], zz[c_][:, :LANES]], axis=0)) for c_ in chains}
    bkg = {c_: bf(jnp.concatenate([bt[c_] * g_l[c_], kt[c_] * g_l[c_]], axis=0)) for c_ in chains}

    lane_s = lax.broadcasted_iota(I32, (hd, LANES), 1)
    heads = [(b, p) for b in range(nb) for p in range(n_pairs)]
    sp = {bp: s_ref[bp[0] * n_pairs + bp[1]] for bp in heads}
    for j in range(per_batch):
        chunk = lambda b: b * per_batch + j
        yw = {(b, p): _nt(qa[chunk(b), p], _stackmask(bf(sp[b, p]))) for b, p in heads}
        upd = {(b, p): _tn(bf(jnp.concatenate([yw[b, p][L:] + zz[chunk(b), p][:, LANES:], vh[chunk(b), p]],
                                              axis=0)), bkg[chunk(b), p]) for b, p in heads}
        for b, p in heads:
            ch, cs, ts = chunk(b), csl(p), rsl(j)
            sp[b, p] = sp[b, p] * g_l[ch, p] + jnp.where(lane_s < hd, upd[b, p][:hd], upd[b, p][hd:])
            y = yw[b, p][:L] + y0[ch, p]
            mean = _pair_sum(y, first) * (1.0 / hd)
            yc = y - mean
            var = _pair_sum(yc * yc, first) * (1.0 / hd)
            yn = yc * lax.rsqrt(var + GN_EPS) * lnw_ref[:, cs] + lnb_ref[:, cs]
            bonus = _pair_sum(brk[rsl(ch), cs], first) * vh[ch, p]
            o_ref[b, ts, cs] = ((yn + bonus) * g_ref[b, ts, cs]).astype(o_ref.dtype)
    for b, p in heads:
        s_ref[b * n_pairs + p] = sp[b, p]


def _rwkv_scan(r, k, v, a, ld, g, k_k, k_a, r_k, ln_w, ln_b, L=RWKV_CHUNK, tm=128):
    B, S, W = r.shape
    seq = pl.BlockSpec((B, tm, W), lambda c: (0, c, 0))
    vec = pl.BlockSpec((1, W), lambda c: (0, 0))
    return pl.pallas_call(
        functools.partial(_rwkv_scan_kernel, L=L),
        grid=(S // tm,),
        in_specs=[seq] * 6 + [vec] * 5,
        out_specs=seq,
        out_shape=jax.ShapeDtypeStruct((B, S, W), BF16),
        scratch_shapes=[pltpu.VMEM((B * (W // LANES), RWKV_HEAD, LANES), F32)],
        compiler_params=_cp("arbitrary"),
        name="rwkv_scan",
    )(r, k, v, a, ld, g, k_k.reshape(1, W), k_a.reshape(1, W), r_k.reshape(1, W),
      ln_w.reshape(1, W), ln_b.reshape(1, W))


def _post_kernel(attn_ref, rw_ref, gate_ref, x_ref, mod_ref, wa_ref, wb_ref, wo_ref, g2_ref, wrh_ref, wrm_ref,
                 sug_ref, sd_ref, base_ref, h2p_ref, lg_ref):
    D = x_ref.shape[1]
    ya = jnp.dot(attn_ref[...], wa_ref[...], preferred_element_type=F32)
    yb = jnp.dot(rw_ref[...], wb_ref[...], preferred_element_type=F32)
    m = gate_ref[:, 0:D] * ya + gate_ref[:, D:] * yb
    x1 = x_ref[...] + mod_ref[2:3, :] * jnp.dot(m.astype(BF16), wo_ref[...], preferred_element_type=F32)
    y = x1 * lax.rsqrt(jnp.mean(x1 * x1, axis=-1, keepdims=True) + NORM_EPS) * g2_ref[...]
    h2 = y * (1.0 + mod_ref[4:5, :]) + mod_ref[3:4, :]
    hb = h2.astype(BF16)
    hm = (h2 - hb.astype(F32)).astype(BF16)
    lg_ref[...] = _nt(wrh_ref[...], hb) + _nt(wrh_ref[...], hm) + _nt(wrm_ref[...], hb)
    packed = _pack_halves(hb.astype(F32))
    half = packed.shape[1] // 2
    h2p_ref[0] = packed[:, :half]
    h2p_ref[1] = packed[:, half:]
    F = sd_ref.shape[0]
    gu = jnp.dot(hb, sug_ref[...], preferred_element_type=F32)
    shared = jnp.dot((_silu(gu[:, :F]) * gu[:, F:]).astype(BF16), sd_ref[...], preferred_element_type=F32)
    base_ref[...] = x1 + mod_ref[5:6, :] * shared


def _post(attn, rw, gate, x2, mod, wa, wb, wo, norm2_g, w_router_t, sug, sd, S, tm=512):
    N, D = x2.shape
    E = w_router_t.shape[0]
    wr_hi = w_router_t.astype(BF16)
    wr_mid = (w_router_t - wr_hi.astype(F32)).astype(BF16)
    tpb = S // tm
    row = lambda w: pl.BlockSpec((tm, w), lambda i: (i, 0))
    full = lambda a: pl.BlockSpec(a.shape, lambda i: (0, 0))
    return pl.pallas_call(
        _post_kernel,
        grid=(N // tm,),
        in_specs=[row(attn.shape[1]), row(rw.shape[1]), row(gate.shape[1]), row(D),
                  pl.BlockSpec((None, 6, D), lambda i: (i // tpb, 0, 0)),
                  full(wa), full(wb), full(wo), pl.BlockSpec((1, D), lambda i: (0, 0)), full(wr_hi), full(wr_mid),
                  full(sug), full(sd)],
        out_specs=[row(D), pl.BlockSpec((2, tm, D // 4), lambda i: (0, i, 0)), pl.BlockSpec((E, tm), lambda i: (0, i))],
        out_shape=[jax.ShapeDtypeStruct((N, D), F32), jax.ShapeDtypeStruct((2, N, D // 4), U32),
                   jax.ShapeDtypeStruct((E, N), F32)],
        compiler_params=_cp("parallel"),
        name="post_mixer",
    )(attn, rw, gate, x2, mod, wa, wb, wo, norm2_g.reshape(1, D), wr_hi, wr_mid, sug, sd)


def _first_argmax(x, idx, sentinel):
    m = jnp.max(x, axis=0, keepdims=True)
    return m, jnp.min(jnp.where(x == m, idx, sentinel), axis=0, keepdims=True)


def _route_kernel(lg_ref, bias_ref, e_ref, w_ref):
    E, T = lg_ref.shape
    gsz = E // N_GROUPS
    scores = _sigmoid(lg_ref[...])
    biased = scores + bias_ref[...]
    ig = lax.broadcasted_iota(I32, (gsz, T), 0)
    gs = []
    for g in range(N_GROUPS):
        blk = biased[g * gsz:(g + 1) * gsz, :]
        m1, i1 = _first_argmax(blk, ig, gsz)
        m2 = jnp.max(jnp.where(ig == i1, -jnp.inf, blk), axis=0, keepdims=True)
        gs.append(m1 + m2)
    gsc = jnp.concatenate(gs, axis=0)
    i8 = lax.broadcasted_iota(I32, (N_GROUPS, T), 0)
    chosen = jnp.zeros((N_GROUPS, T), F32)
    for _ in range(TOPK_GROUPS):
        _, gi = _first_argmax(gsc, i8, N_GROUPS)
        hit = i8 == gi
        chosen = jnp.where(hit, 1.0, chosen)
        gsc = jnp.where(hit, -jnp.inf, gsc)
    masked = jnp.concatenate(
        [jnp.where(chosen[g:g + 1, :] > 0.0, biased[g * gsz:(g + 1) * gsz, :], -jnp.inf)
         for g in range(N_GROUPS)], axis=0)
    ie = lax.broadcasted_iota(I32, (E, T), 0)
    idxs, wts = [], []
    for _ in range(TOP_K):
        _, ei = _first_argmax(masked, ie, E)
        hit = ie == ei
        idxs.append(ei)
        wts.append(jnp.sum(jnp.where(hit, scores, 0.0), axis=0, keepdims=True))
        masked = jnp.where(hit, -jnp.inf, masked)
    wt = jnp.concatenate(wts, axis=0)
    e_ref[...] = jnp.concatenate(idxs, axis=0)
    w_ref[...] = wt / jnp.sum(wt, axis=0, keepdims=True) * ROUTED_SCALE


def _route(logits_t, router_bias, T=512):
    E, N = logits_t.shape
    blk = pl.BlockSpec((TOP_K, T), lambda i: (0, i))
    return pl.pallas_call(
        _route_kernel,
        grid=(N // T,),
        in_specs=[pl.BlockSpec((E, T), lambda i: (0, i)), pl.BlockSpec((E, 1), lambda i: (0, 0))],
        out_specs=[blk, blk],
        out_shape=[jax.ShapeDtypeStruct((TOP_K, N), I32), jax.ShapeDtypeStruct((TOP_K, N), F32)],
        compiler_params=_cp("parallel"),
        name="route",
    )(logits_t, router_bias.reshape(E, 1))


def _rank_kernel(e_ref, rank_ref, cnt_ref, carry_ref, *, n_experts):
    T = e_ref.shape[1]

    @pl.when(pl.program_id(0) == 0)
    def _():
        carry_ref[...] = jnp.zeros_like(carry_ref)

    ie = lax.broadcasted_iota(I32, (n_experts, T), 0)
    e = e_ref[...]
    hits = [ie == e[kk:kk + 1, :] for kk in range(TOP_K)]
    onehot = jnp.zeros((n_experts, T), F32)
    for hsel in hits:
        onehot = onehot + hsel.astype(F32)
    tr = lax.broadcasted_iota(I32, (T, T), 0)
    tc = lax.broadcasted_iota(I32, (T, T), 1)
    before = (tr < tc).astype(BF16)
    base = _bdot(onehot, before) + carry_ref[:, 0:1]
    rank_ref[...] = jnp.concatenate(
        [jnp.sum(jnp.where(hsel, base, 0.0), axis=0, keepdims=True) for hsel in hits], axis=0).astype(I32)
    carry_ref[...] = carry_ref[...] + jnp.sum(onehot, axis=1, keepdims=True)
    cnt_ref[...] = carry_ref[...]


def _ranks(eidx_t, n_experts, T=512):
    N = eidx_t.shape[1]
    T = min(T, N)
    blk = pl.BlockSpec((TOP_K, T), lambda i: (0, i))
    cnt = pl.BlockSpec((n_experts, LANES), lambda i: (0, 0))
    return pl.pallas_call(
        functools.partial(_rank_kernel, n_experts=n_experts),
        grid=(N // T,),
        in_specs=[blk],
        out_specs=[blk, cnt],
        out_shape=[jax.ShapeDtypeStruct((TOP_K, N), I32), jax.ShapeDtypeStruct((n_experts, LANES), F32)],
        scratch_shapes=[pltpu.VMEM((n_experts, LANES), F32)],
        compiler_params=_cp("arbitrary"),
        name="ranks",
    )(eidx_t)


def _dest_kernel(e_ref, rank_ref, start_ref, d_ref):
    E = start_ref.shape[0]
    T = e_ref.shape[1]
    ie = lax.broadcasted_iota(I32, (E, T), 0)
    e = e_ref[...]
    start = start_ref[:, 0:1]
    rows = [jnp.sum(jnp.where(ie == e[kk:kk + 1, :], start, 0.0), axis=0, keepdims=True) for kk in range(TOP_K)]
    d_ref[...] = jnp.concatenate(rows, axis=0).astype(I32) + rank_ref[...]


def _dests(eidx_t, rank_t, pstart, T=2048):
    N = eidx_t.shape[1]
    T = min(T, N)
    E = pstart.shape[0]
    blk = pl.BlockSpec((TOP_K, T), lambda i: (0, i))
    return pl.pallas_call(
        _dest_kernel,
        grid=(N // T,),
        in_specs=[blk, blk, pl.BlockSpec((E, LANES), lambda i: (0, 0))],
        out_specs=blk,
        out_shape=jax.ShapeDtypeStruct((TOP_K, N), I32),
        compiler_params=_cp("parallel"),
        name="dests",
    )(eidx_t, rank_t, jnp.broadcast_to(pstart.astype(F32)[:, None], (E, LANES)))


def _expert_kernel(us_ref, ps_ref, x_hbm, wug_hbm, wd_hbm, y_hbm,
                   wug_f32, wd_f32, wug_bf, wd_bf, xbuf, ybuf, cnt_ref, wsem, xsem, ysem, *, n_rows):
    e = pl.program_id(0)
    n_experts = pl.num_programs(0)
    _, _, R, Ch = xbuf.shape
    C = 2 * Ch
    F = wd_hbm.shape[1]

    def w_copies(expert, slot):
        return (pltpu.make_async_copy(wug_hbm.at[expert], wug_f32.at[slot], wsem.at[slot]),
                pltpu.make_async_copy(wd_hbm.at[expert], wd_f32.at[slot], wsem.at[slot]))

    def x_copy(slot, half, row):
        return pltpu.make_async_copy(x_hbm.at[half, pl.ds(row, R)], xbuf.at[slot, half], xsem.at[slot])

    def y_copy(slot, half, row):
        return pltpu.make_async_copy(ybuf.at[slot, half], y_hbm.at[half, pl.ds(row, R)], ysem.at[slot])

    def start(copy, slot, row):
        for half in range(2):
            copy(slot, half, row).start()

    def wait(copy, slot):
        for half in range(2):
            copy(slot, half, 0).wait()

    @pl.when(e == 0)
    def _():
        for ahead in range(W_SLOTS - 1):
            for cp in w_copies(ahead, ahead):
                cp.start()
        cnt_ref[0] = 0
        for t in range(X_AHEAD):
            start(x_copy, t, t * R)
        ybuf[...] = jnp.zeros_like(ybuf)
        for slot in range(Y_SLOTS):
            start(y_copy, slot, n_rows + slot * R)

    @pl.when(e + (W_SLOTS - 1) < n_experts)
    def _():
        for cp in w_copies(e + (W_SLOTS - 1), lax.rem(e + (W_SLOTS - 1), W_SLOTS)):
            cp.start()

    wslot = lax.rem(e, W_SLOTS)
    for cp in w_copies(e, wslot):
        cp.wait()
    wug_bf[...] = wug_f32[wslot].astype(BF16)
    wd_bf[...] = wd_f32[wslot].astype(BF16)
    n_valid = us_ref[e + 1] - us_ref[e]
    pbase = ps_ref[e]
    n_tiles = lax.shift_right_logical(n_valid + (R - 1), int(math.log2(R)))

    def make_tile(nb):
        rowid = lax.broadcasted_iota(I32, (nb * R, C), 0)

        def tile(blk0):
            n_done = cnt_ref[0]
            xslots = [lax.rem(n_done + b, X_SLOTS) for b in range(nb)]
            for b in range(nb):
                ahead = n_done + X_AHEAD + b
                start(x_copy, lax.rem(ahead, X_SLOTS), pl.multiple_of(ahead * R, R))
            for b in range(nb):
                wait(x_copy, xslots[b])
            x = jnp.concatenate([jnp.concatenate([xbuf[xs, 0], xbuf[xs, 1]], axis=1) for xs in xslots], axis=0)
            x = jnp.where(rowid < n_valid - blk0 * R, x, jnp.uint32(0))
            lo, hi = _unpack_halves(x)
            gu = (jnp.dot(lo.astype(BF16), wug_bf[0:C, :], preferred_element_type=F32)
                  + jnp.dot(hi.astype(BF16), wug_bf[C:, :], preferred_element_type=F32))
            hid = (_silu(gu[:, :F]) * gu[:, F:]).astype(BF16)
            y = jnp.dot(hid, wd_bf[...], preferred_element_type=F32)
            packed = _pack_halves(y.astype(BF16).astype(F32))
            for b in range(nb):
                slot = lax.rem(n_done + b, Y_SLOTS)
                wait(y_copy, slot)
                ybuf[slot, 0] = packed[b * R:(b + 1) * R, :Ch]
                ybuf[slot, 1] = packed[b * R:(b + 1) * R, Ch:]
                start(y_copy, slot, pl.multiple_of(pbase + (blk0 + b) * R, R))
            cnt_ref[0] = n_done + nb

        return tile

    pair_tile, single_tile, triple_tile = make_tile(2), make_tile(1), make_tile(3)
    odd = (n_tiles & 1) == 1
    triple = jnp.logical_and(odd, n_tiles >= 3)
    n_pairs = lax.shift_right_logical(n_tiles - jnp.where(triple, 3, 0), 1)

    def pairs(j, carry):
        pair_tile(2 * j)
        return carry

    lax.fori_loop(0, n_pairs, pairs, 0)

    @pl.when(triple)
    def _():
        triple_tile(n_tiles - 3)

    @pl.when(jnp.logical_and(odd, jnp.logical_not(triple)))
    def _():
        single_tile(n_tiles - 1)

    @pl.when(e == pl.num_programs(0) - 1)
    def _():
        for t in range(X_AHEAD):
            wait(x_copy, lax.rem(cnt_ref[0] + t, X_SLOTS))
        for slot in range(Y_SLOTS):
            wait(y_copy, slot)
        ybuf[0] = jnp.zeros((2, R, Ch), U32)
        first = lax.shift_right_logical(pbase + n_tiles * R, int(math.log2(R)))
        n_left = n_rows // R - first

        def fill(t, carry):
            start(y_copy, 0, pl.multiple_of((first + t) * R, R))
            return carry

        def drain(t, carry):
            wait(y_copy, 0)
            return carry

        lax.fori_loop(0, n_left, fill, 0)
        lax.fori_loop(0, n_left, drain, 0)


def _experts(ustart, pstart, xg, w_ug, w_d, n_rows, R):
    _, _, Ch = xg.shape
    E, D, F2 = w_ug.shape
    F = w_d.shape[1]
    grid_spec = pltpu.PrefetchScalarGridSpec(
        num_scalar_prefetch=2,
        grid=(E,),
        in_specs=[pl.BlockSpec(memory_space=pl.ANY)] * 3,
        out_specs=pl.BlockSpec(memory_space=pl.ANY),
        scratch_shapes=[pltpu.VMEM((W_SLOTS, D, F2), F32), pltpu.VMEM((W_SLOTS, F, D), F32),
                        pltpu.VMEM((D, F2), BF16), pltpu.VMEM((F, D), BF16),
                        pltpu.VMEM((X_SLOTS, 2, R, Ch), U32), pltpu.VMEM((Y_SLOTS, 2, R, Ch), U32), pltpu.SMEM((1,), I32),
                        pltpu.SemaphoreType.DMA((W_SLOTS,)), pltpu.SemaphoreType.DMA((X_SLOTS,)),
                        pltpu.SemaphoreType.DMA((Y_SLOTS,))],
    )
    return pl.pallas_call(
        functools.partial(_expert_kernel, n_rows=n_rows),
        grid_spec=grid_spec,
        out_shape=jax.ShapeDtypeStruct((2, n_rows + Y_SLOTS * R, Ch), U32),
        compiler_params=_cp("arbitrary"),
        name="experts",
    )(ustart, pstart, xg, w_ug, w_d)


def _sc_scatter_rows(src, idx, n_rows):
    H, N, C = src.shape
    K = idx.shape[0]
    per_row = N // SC_WINDOW
    mesh = plsc.VectorSubcoreMesh(core_axis_name="c", subcore_axis_name="s")

    @functools.partial(pl.kernel, out_type=jax.ShapeDtypeStruct((H, n_rows, C), src.dtype), mesh=mesh,
                       scratch_types=[])
    def scatter_kernel(x_hbm, i_hbm, o_hbm):
        for h in range(H):
            def body(x_vmem, i_vmem):
                for k in range(K):
                    pltpu.sync_copy(x_vmem, o_hbm.at[h].at[i_vmem.at[k]])

            pltpu.emit_pipeline(
                body,
                grid=(per_row,),
                in_specs=[pl.BlockSpec((SC_WINDOW, C), lambda i: (i, 0)),
                          pl.BlockSpec((K, SC_WINDOW), lambda i: (0, i))],
                out_specs=[],
                core_axis_name=("c", "s"),
                dimension_semantics=(pltpu.PARALLEL,),
            )(x_hbm.at[h], i_hbm)

    return scatter_kernel(src, idx)


def _sc_gather_rows(src, idx):
    H, _, C = src.shape
    K, N = idx.shape
    per_row = N // SC_WINDOW
    mesh = plsc.VectorSubcoreMesh(core_axis_name="c", subcore_axis_name="s")

    @functools.partial(pl.kernel, out_type=jax.ShapeDtypeStruct((H, K * N, C), src.dtype), mesh=mesh,
                       scratch_types=[])
    def gather_kernel(x_hbm, i_hbm, o_hbm):
        for h in range(H):
            def body(i_vmem, o_vmem):
                pltpu.sync_copy(x_hbm.at[h].at[i_vmem.at[0]], o_vmem)

            pltpu.emit_pipeline(
                body,
                grid=(K * per_row,),
                in_specs=[pl.BlockSpec((1, SC_WINDOW), lambda i: (i // per_row, i % per_row))],
                out_specs=[pl.BlockSpec((SC_WINDOW, C), lambda i: (i, 0))],
                core_axis_name=("c", "s"),
                dimension_semantics=(pltpu.PARALLEL,),
            )(i_hbm, o_hbm.at[h])

    return gather_kernel(src, idx)


def _combine_kernel(*refs):
    y_refs = refs[:2 * TOP_K]
    w_ref, base_ref, mod_ref, o_ref = refs[2 * TOP_K:]
    T = base_ref.shape[0]
    tr = lax.broadcasted_iota(I32, (T, T), 0)
    tc = lax.broadcasted_iota(I32, (T, T), 1)
    wcol = _nt((tr == tc).astype(F32), w_ref[...], precision=HI)
    acc = [None] * 4
    for kk in range(TOP_K):
        wk = wcol[:, kk:kk + 1]
        for half in range(2):
            lo, hi = _unpack_halves(y_refs[2 * kk + half][...])
            for q, val in ((half, lo), (2 + half, hi)):
                acc[q] = val * wk if acc[q] is None else acc[q] + val * wk
    o_ref[...] = base_ref[...] + mod_ref[5:6, :] * jnp.concatenate(acc, axis=1)


def _combine(yg, w_t, base, mod, S, T=512):
    N, D = base.shape
    C = yg.shape[2]
    tpb = S // T
    n_tiles = N // T
    row = pl.BlockSpec((T, D), lambda i: (i, 0))
    piece = lambda kk, half: pl.BlockSpec((None, T, C), lambda i: (half, kk * n_tiles + i, 0))
    return pl.pallas_call(
        _combine_kernel,
        grid=(n_tiles,),
        in_specs=[piece(kk, half) for kk in range(TOP_K) for half in range(2)] + [
            pl.BlockSpec((TOP_K, T), lambda i: (0, i)),
            row,
            pl.BlockSpec((None, 6, D), lambda i: (i // tpb, 0, 0))],
        out_specs=row,
        out_shape=jax.ShapeDtypeStruct((N, D), F32),
        compiler_params=_cp("parallel"),
        name="combine",
    )(*([yg] * (2 * TOP_K)), w_t, base, mod)


def _layer(x, c, positions, layer_idx, w_ada, b_ada, norm1_g, w_in, w_gate, b_gate,
           q_norm_g, k_norm_g, lambda_q1, lambda_k1, lambda_q2, lambda_k2, subln_g,
           rwkv_mu, w_decay0, w_decay2, a0, a2, g2, k_k, k_a, r_k, ln_x_w, ln_x_b,
           w_branch_a, w_branch_b, w_out, norm2_g, w_router, router_bias,
           w_expert_up_gate, w_expert_down, w_shared_up_gate, w_shared_down):
    B, S, D = x.shape
    N = B * S
    E = w_router.shape[1]
    da_width = w_branch_a.shape[0]
    rw_width = w_branch_b.shape[0]
    lambda_init = 0.8 - 0.6 * math.exp(-0.3 * layer_idx)

    mod = _adaln(c, w_ada, b_ada)
    x2 = x.reshape(N, D)
    w_cat = jnp.concatenate([w_in, w_gate], axis=1).astype(BF16)
    qn, kn, v, r_, k_, v_, a_, ld_, g_, gate = _mixer_in(
        x2, positions.reshape(N, 1), mod, norm1_g, w_cat, b_gate, q_norm_g, k_norm_g,
        rwkv_mu, w_decay0, w_decay2, a0, a2, g2, S, da_width, rw_width)

    lam_vecs = jnp.stack([lambda_q1, lambda_k1, lambda_q2, lambda_k2])
    score_bound = 1.01 * DA_HEAD_DIM ** 0.5 * jnp.max(jnp.abs(q_norm_g)) * jnp.max(jnp.abs(k_norm_g))
    attn = _diff_attention(qn, kn, v, score_bound, lam_vecs, subln_g, B, S, lambda_init)

    seq = lambda t: t.reshape(B, S, rw_width)
    rw = _rwkv_scan(seq(r_), seq(k_), seq(v_), seq(a_), seq(ld_), seq(g_), k_k, k_a, r_k.reshape(-1),
                    ln_x_w, ln_x_b).reshape(N, rw_width)

    base, h2p, logits_t = _post(attn, rw, gate, x2, mod, w_branch_a.astype(BF16), w_branch_b.astype(BF16),
                                w_out.astype(BF16), norm2_g, w_router.T,
                                w_shared_up_gate.astype(BF16), w_shared_down.astype(BF16), S)

    eidx_t, w_t = _route(logits_t, router_bias)
    rank_t, counts = _ranks(eidx_t, E)
    R = EXPERT_TILE
    cnt = counts[:, 0].astype(I32)
    ustart = jnp.concatenate([jnp.zeros((1,), I32), jnp.cumsum(cnt)])
    pcnt = (cnt + R - 1) // R * R
    pstart = jnp.cumsum(pcnt) - pcnt
    dest_p = _dests(eidx_t, rank_t, pstart)
    n_rows = (N * TOP_K + E * (R - 1) + R - 1) // R * R
    xg = _sc_scatter_rows(h2p, dest_p, n_rows + X_AHEAD * R)
    y = _experts(ustart, pstart, xg, w_expert_up_gate, w_expert_down, n_rows, R)
    yg = _sc_gather_rows(y, dest_p)
    out = _combine(yg, w_t, base, mod, S)
    return out.reshape(B, S, D)


def kernel(x, c, positions, w_ada, b_ada, norm1_g, w_in, w_gate, b_gate, q_norm_g, k_norm_g, lambda_q1, lambda_k1, lambda_q2, lambda_k2, subln_g, rwkv_mu, w_decay0, w_decay2, a0, a2, g2, k_k, k_a, r_k, ln_x_w, ln_x_b, w_branch_a, w_branch_b, w_out, norm2_g, w_router, router_bias, w_expert_up_gate, w_expert_down, w_shared_up_gate, w_shared_down):
    for l in range(w_ada.shape[0]):
        x = _layer(x, c, positions, l, w_ada[l], b_ada[l], norm1_g[l], w_in[l], w_gate[l], b_gate[l],
                   q_norm_g[l], k_norm_g[l], lambda_q1[l], lambda_k1[l], lambda_q2[l], lambda_k2[l],
                   subln_g[l], rwkv_mu[l], w_decay0[l], w_decay2[l], a0[l], a2[l], g2[l], k_k[l],
                   k_a[l], r_k[l], ln_x_w[l], ln_x_b[l], w_branch_a[l], w_branch_b[l], w_out[l],
                   norm2_g[l], w_router[l], router_bias[l], w_expert_up_gate[l], w_expert_down[l],
                   w_shared_up_gate[l], w_shared_down[l])
    return x
```

```python
import functools
import math

import jax
import jax.numpy as jnp
from jax import lax
from jax.experimental import pallas as pl
from jax.experimental.pallas import tpu as pltpu
from jax.experimental.pallas import tpu_sc as plsc

F32 = jnp.float32
BF16 = jnp.bfloat16
I32 = jnp.int32
U32 = jnp.uint32
HI = lax.Precision.HIGHEST

CHUNK = 64
ROPE_THETA = 10000.0
NORM_EPS = 1e-6
SUBLN_EPS = 1e-5
DA_HEAD_DIM = 64
RWKV_HEAD = 64
GN_EPS = 64e-5
TOP_K = 8
N_GROUPS = 8
TOPK_GROUPS = 4
ROUTED_SCALE = 2.5
EXPERT_TILE = 128
W_SLOTS = 3
TILE_BLOCKS = 5
X_SLOTS = 13
X_AHEAD = X_SLOTS - TILE_BLOCKS
Y_SLOTS = 8
RWKV_CHUNK = 64
LANES = 128
SC_WINDOW = 128
NEG = -1e30
MAX_PLAIN_SCORE = 40.0
VMEM_LIMIT = 56 * 1024 * 1024


def _cp(*sem):
    return pltpu.CompilerParams(dimension_semantics=sem, vmem_limit_bytes=VMEM_LIMIT)


def _bdot(a, b):
    return jnp.dot(a.astype(BF16), b.astype(BF16), preferred_element_type=F32)


def _fdot(a, b):
    return jnp.dot(a, b, precision=HI, preferred_element_type=F32)


def _nt(a, b, precision=None):
    return lax.dot_general(a, b, (((1,), (1,)), ((), ())), precision=precision,
                           preferred_element_type=F32)


def _tn(a, b, precision=None):
    return lax.dot_general(a, b, (((0,), (0,)), ((), ())), precision=precision,
                           preferred_element_type=F32)


def _pack_halves(x):
    c = x.shape[1] // 2
    lo = lax.bitcast_convert_type(x[:, :c], U32)
    hi = lax.bitcast_convert_type(x[:, c:], U32)
    return (hi & jnp.uint32(0xFFFF0000)) | (lo >> 16)


def _unpack_halves(w):
    lo = lax.bitcast_convert_type(w << 16, F32)
    hi = lax.bitcast_convert_type(w & jnp.uint32(0xFFFF0000), F32)
    return lo, hi


def _sigmoid(x):
    return 1.0 / (1.0 + jnp.exp(-x))


def _silu(x):
    return x * _sigmoid(x)


def _ada_kernel(c_ref, w_ref, b_ref, o_ref):
    o_ref[...] = _fdot(_silu(c_ref[...]), w_ref[...]) + b_ref[...]


def _adaln(c, w_ada, b_ada):
    B, D = c.shape
    rows = -(-B // 8) * 8
    cpad = jnp.zeros((rows, D), F32).at[:B].set(c)
    n_out = w_ada.shape[1]
    out = pl.pallas_call(
        _ada_kernel,
        grid=(n_out // D,),
        in_specs=[pl.BlockSpec((rows, D), lambda j: (0, 0)),
                  pl.BlockSpec((D, D), lambda j: (0, j)),
                  pl.BlockSpec((1, D), lambda j: (0, j))],
        out_specs=pl.BlockSpec((rows, D), lambda j: (0, j)),
        out_shape=jax.ShapeDtypeStruct((rows, n_out), F32),
        compiler_params=_cp("arbitrary"),
        name="adaln",
    )(cpad, w_ada, b_ada.reshape(1, n_out))
    return out[:B].reshape(B, n_out // D, D)


def _mixer_in_kernel(x_ref, xprev_ref, mod_ref, g_ref, w_ref, bg_ref, pos_ref, invf_ref, qg_ref, kg_ref,
                     mu_ref, w0_ref, w2_ref, a0_ref, a2_ref, g2_ref,
                     qn_ref, kn_ref, v_ref, r_ref, k_ref, vr_ref, a_ref, ld_ref, gr_ref, gate_ref,
                     *, da_width, rw_cols, rw_width, q_scale, tiles_per_seq):
    tm = x_ref.shape[0]

    def modulated(x):
        y = x * lax.rsqrt(jnp.mean(x * x, axis=-1, keepdims=True) + NORM_EPS) * g_ref[...]
        return (y * (1.0 + mod_ref[1:2, :]) + mod_ref[0:1, :]).astype(BF16)

    def proj(hb, c0, width, step=512):
        parts = [jnp.dot(hb, w_ref[:, c0 + o:c0 + min(o + step, width)], preferred_element_type=F32)
                 for o in range(0, width, step)]
        return parts[0] if len(parts) == 1 else jnp.concatenate(parts, axis=1)

    h = modulated(x_ref[...])

    lane = lax.broadcasted_iota(I32, (tm, LANES), 1)
    first = lane < DA_HEAD_DIM
    lo_half = (lane & (DA_HEAD_DIM - 1)) < DA_HEAD_DIM // 2
    ang = pos_ref[...].astype(F32) * invf_ref[...]
    cos = jnp.cos(ang)
    sin = jnp.sin(ang)
    sin = jnp.where(lo_half, -sin, sin)
    for c0, dst, gn_ref, mult in ((0, qn_ref, qg_ref, q_scale), (da_width, kn_ref, kg_ref, 1.0)):
        raw = proj(h, c0, da_width)
        for blk in range(da_width // LANES):
            x = raw[:, blk * LANES:(blk + 1) * LANES]
            xx = x * x
            s_first = jnp.sum(jnp.where(first, xx, 0.0), axis=-1, keepdims=True)
            s_second = jnp.sum(jnp.where(first, 0.0, xx), axis=-1, keepdims=True)
            ms = jnp.where(first, s_first, s_second) * (1.0 / DA_HEAD_DIM)
            xn = x * lax.rsqrt(ms + NORM_EPS) * gn_ref[...]
            rot = jnp.where(lo_half, pltpu.roll(xn, LANES - DA_HEAD_DIM // 2, axis=1),
                            pltpu.roll(xn, DA_HEAD_DIM // 2, axis=1))
            dst[:, blk * LANES:(blk + 1) * LANES] = ((xn * cos + rot * sin) * mult).astype(dst.dtype)
    v_ref[...] = proj(h, 2 * da_width, da_width).astype(v_ref.dtype)

    c_rw = 3 * da_width
    p = proj(h, c_rw, rw_cols)
    p_before = proj(modulated(xprev_ref[...]), c_rw, rw_cols)
    seq_start = (pl.program_id(0) % tiles_per_seq) == 0
    last_prev = jnp.where(seq_start, 0.0, p_before[7:8, :])
    rowi = lax.broadcasted_iota(I32, p.shape, 0)
    prev = jnp.where(rowi == 0, last_prev, pltpu.roll(p, 1, axis=0))
    xs = p + (prev - p) * mu_ref[...]
    width = rw_width
    r_ref[...] = xs[:, 0:width]
    k_ref[...] = xs[:, width:2 * width]
    vr_ref[...] = xs[:, 2 * width:3 * width]
    xwa = xs[:, 3 * width:3 * width + LANES]
    xg = xs[:, 3 * width + LANES:]
    z = w0_ref[...] + _bdot(jnp.tanh(xwa), w2_ref[...])
    w = -(jnp.maximum(-z, 0.0) + jnp.log(1.0 + jnp.exp(-jnp.abs(z)))) - 0.5
    ld_ref[...] = -jnp.exp(w)
    a_ref[...] = _sigmoid(a0_ref[...] + _bdot(xwa, a2_ref[...]))
    gr_ref[...] = _bdot(_sigmoid(xg), g2_ref[...])

    gate_ref[...] = _sigmoid(proj(h, c_rw + rw_cols, gate_ref.shape[1]) + bg_ref[...]).astype(gate_ref.dtype)


def _mixer_in(x2, pos2, mod, norm1_g, w_cat, b_gate, q_norm_g, k_norm_g, mu, w_decay0, w_decay2, a0, a2, g2,
              S, da_width, rw_width, tm=512):
    N, D = x2.shape
    n_gate = b_gate.shape[0]
    rw_cols = mu.shape[0]
    tpb = S // tm
    d = DA_HEAD_DIM
    inv_freq = 1.0 / (ROPE_THETA ** (jnp.arange(0, d, 2, dtype=F32) / d))
    invf = jnp.tile(inv_freq, LANES // (d // 2)).reshape(1, LANES)
    dl, al = w_decay2.shape[0], a2.shape[0]
    assert dl + al == LANES and g2.shape[0] == LANES
    w2p = jnp.zeros((LANES, rw_width), F32).at[:dl].set(w_decay2)
    a2p = jnp.zeros((LANES, rw_width), F32).at[dl:].set(a2)
    kern = functools.partial(_mixer_in_kernel, da_width=da_width, rw_cols=rw_cols, rw_width=rw_width,
                             q_scale=d ** -0.5 * math.log2(math.e),
                             tiles_per_seq=tpb)
    row = lambda w: pl.BlockSpec((tm, w), lambda i: (i, 0))
    vec = lambda n: pl.BlockSpec((1, n), lambda i: (0, 0))
    mat = pl.BlockSpec((LANES, rw_width), lambda i: (0, 0))
    f32 = lambda w: jax.ShapeDtypeStruct((N, w), F32)
    bf16 = lambda w: jax.ShapeDtypeStruct((N, w), BF16)
    return pl.pallas_call(
        kern,
        grid=(N // tm,),
        in_specs=[row(D),
                  pl.BlockSpec((8, D), lambda i: (jnp.maximum(i * (tm // 8) - 1, 0), 0)),
                  pl.BlockSpec((None, 6, D), lambda i: (i // tpb, 0, 0)),
                  vec(D),
                  pl.BlockSpec(w_cat.shape, lambda i: (0, 0)),
                  vec(n_gate),
                  pl.BlockSpec((tm, 1), lambda i: (i, 0)),
                  vec(LANES), vec(LANES), vec(LANES),
                  vec(rw_cols), vec(rw_width), mat, vec(rw_width), mat, mat],
        out_specs=[row(da_width)] * 3 + [row(rw_width)] * 6 + [row(n_gate)],
        out_shape=[bf16(da_width)] * 3 + [f32(rw_width)] * 6 + [bf16(n_gate)],
        compiler_params=_cp("parallel"),
        name="mixer_in",
    )(x2, x2, mod, norm1_g.reshape(1, D), w_cat, b_gate.reshape(1, n_gate), pos2, invf,
      jnp.tile(q_norm_g, 2).reshape(1, LANES), jnp.tile(k_norm_g, 2).reshape(1, LANES),
      mu.reshape(1, rw_cols), w_decay0.reshape(1, rw_width), w2p, a0.reshape(1, rw_width), a2p, g2)


def _attn_kernel(flag_ref, q_ref, k_ref, v_ref, vt_ref, lam_ref, sg_ref, sgc_ref, o_ref,
                 qz_ref, m_ref, l_ref, acc_ref, lpt_ref, acct_ref, *, tq, lambda_init):
    i = pl.program_id(2)
    lane = lax.broadcasted_iota(I32, (tq, LANES), 1)
    q = q_ref[...]
    zero = jnp.zeros_like(q)
    qz_ref[0:tq, :] = jnp.where(lane < DA_HEAD_DIM, q, zero)
    qz_ref[tq:, :] = jnp.where(lane >= DA_HEAD_DIM, q, zero)
    bounded = flag_ref[0] == 1
    lv = lam_ref[...]
    lam = (jnp.exp(jnp.sum(lv[0:1] * lv[1:2], keepdims=True))
           - jnp.exp(jnp.sum(lv[2:3] * lv[3:4], keepdims=True)) + lambda_init)

    def run(step):
        def body(j, carry):
            step(j, False)
            return carry
        lax.fori_loop(0, i, body, 0)
        step(i, True)

    def probs_t(keys, queries, masked, q_mask):
        st = _nt(keys, queries)
        if masked:
            row = lax.broadcasted_iota(I32, st.shape, 0)
            col = lax.broadcasted_iota(I32, st.shape, 1)
            st = jnp.where((row // CHUNK) <= ((col & q_mask) // CHUNK), st, NEG)
        pt = jnp.exp2(st)
        part = pt[0:8, :]
        for g in range(1, st.shape[0] // 8):
            part = part + pt[8 * g:8 * g + 8, :]
        return pt.astype(BF16), part

    def plain_step(j, masked):
        off = pl.multiple_of(j * tq, tq)
        if not masked:
            pt, part = probs_t(k_ref[pl.ds(off, tq), :], qz_ref[...], False, 0)
            lpt_ref[...] += part
            acct_ref[...] += jnp.dot(vt_ref[:, pl.ds(off, tq)], pt, preferred_element_type=F32)
            return
        hq = tq // 2
        pt, part = probs_t(k_ref[pl.ds(off, hq), :], qz_ref[...], True, tq - 1)
        lpt_ref[...] += part
        acct_ref[...] += jnp.dot(vt_ref[:, pl.ds(off, hq)], pt, preferred_element_type=F32)
        off2 = pl.multiple_of(off + hq, hq)
        late_q = jnp.concatenate([qz_ref[hq:tq, :], qz_ref[tq + hq:, :]], axis=0)
        pt, part = probs_t(k_ref[pl.ds(off2, hq), :], late_q, True, hq - 1)
        upd = jnp.dot(vt_ref[:, pl.ds(off2, hq)], pt, preferred_element_type=F32)
        for m in range(2):
            cols = slice(m * tq + hq, (m + 1) * tq)
            lpt_ref[:, cols] += part[:, m * hq:(m + 1) * hq]
            acct_ref[:, cols] += upd[:, m * hq:(m + 1) * hq]

    @pl.when(bounded)
    def _():
        lpt_ref[...] = jnp.zeros_like(lpt_ref)
        acct_ref[...] = jnp.zeros_like(acct_ref)

        def two_steps(p, carry):
            plain_step(2 * p, False)
            plain_step(2 * p + 1, False)
            return carry

        lax.fori_loop(0, lax.shift_right_logical(i, 1), two_steps, 0)

        @pl.when((i & 1) == 1)
        def _():
            plain_step(i - 1, False)
            plain_step(i, True)

        @pl.when((i & 1) == 0)
        def _():
            plain_step(i, True)

        lsum = jnp.sum(lpt_ref[...], axis=0, keepdims=True)
        ot = acct_ref[:, 0:tq] / lsum[:, 0:tq] - lam * (acct_ref[:, tq:] / lsum[:, tq:])
        ot = ot * lax.rsqrt(jnp.mean(ot * ot, axis=0, keepdims=True) + SUBLN_EPS) * sgc_ref[...]
        o_ref[...] = (ot * (1.0 - lambda_init)).T.astype(o_ref.dtype)

    def online_step(j, masked):
        off = pl.multiple_of(j * tq, tq)
        s = _nt(qz_ref[...], k_ref[pl.ds(off, tq), :])
        if masked:
            row = lax.broadcasted_iota(I32, s.shape, 0)
            col = lax.broadcasted_iota(I32, s.shape, 1)
            s = jnp.where((col // CHUNK) <= ((row & (tq - 1)) // CHUNK), s, NEG)
        m_old = m_ref[...]
        m_new = jnp.maximum(m_old, jnp.max(s, axis=-1, keepdims=True))
        alpha = jnp.exp2(m_old - m_new)
        pr = jnp.exp2(s - m_new)
        l_ref[...] = alpha * l_ref[...] + jnp.sum(pr, axis=-1, keepdims=True)
        acc_ref[...] = alpha * acc_ref[...] + jnp.dot(pr.astype(BF16), v_ref[pl.ds(off, tq), :],
                                                      preferred_element_type=F32)
        m_ref[...] = m_new

    @pl.when(jnp.logical_not(bounded))
    def _():
        m_ref[...] = jnp.full_like(m_ref, NEG)
        l_ref[...] = jnp.zeros_like(l_ref)
        acc_ref[...] = jnp.zeros_like(acc_ref)
        run(online_step)
        o = acc_ref[0:tq, :] / l_ref[0:tq, :] - lam * (acc_ref[tq:, :] / l_ref[tq:, :])
        o = o * lax.rsqrt(jnp.mean(o * o, axis=-1, keepdims=True) + SUBLN_EPS) * sg_ref[...]
        o_ref[...] = (o * (1.0 - lambda_init)).astype(o_ref.dtype)


def _diff_attention(qn, kn, v, score_bound, lam_vecs, subln_g, B, S, lambda_init, tq=1024):
    W = qn.shape[1]
    H = W // LANES
    tq = min(tq, S)
    q3 = qn.reshape(B, S, W)
    k3 = kn.reshape(B, S, W)
    v3 = v.reshape(B, S, W)
    vt3 = v3.transpose(0, 2, 1)
    flag = (score_bound <= MAX_PLAIN_SCORE).astype(I32).reshape(1)
    qblk = pl.BlockSpec((None, tq, LANES), lambda b, h, i, f: (b, i, h))
    kvblk = pl.BlockSpec((None, S, LANES), lambda b, h, i, f: (b, 0, h))
    grid_spec = pltpu.PrefetchScalarGridSpec(
        num_scalar_prefetch=1,
        grid=(B, H, S // tq),
        in_specs=[qblk, kvblk, kvblk,
                  pl.BlockSpec((None, LANES, S), lambda b, h, i, f: (b, h, 0)),
                  pl.BlockSpec((4, DA_HEAD_DIM), lambda b, h, i, f: (0, 0)),
                  pl.BlockSpec((1, LANES), lambda b, h, i, f: (0, 0)),
                  pl.BlockSpec((LANES, 1), lambda b, h, i, f: (0, 0))],
        out_specs=qblk,
        scratch_shapes=[pltpu.VMEM((2 * tq, LANES), BF16),
                        pltpu.VMEM((2 * tq, 1), F32),
                        pltpu.VMEM((2 * tq, 1), F32),
                        pltpu.VMEM((2 * tq, LANES), F32),
                        pltpu.VMEM((8, 2 * tq), F32),
                        pltpu.VMEM((LANES, 2 * tq), F32)],
    )
    out = pl.pallas_call(
        functools.partial(_attn_kernel, tq=tq, lambda_init=lambda_init),
        grid_spec=grid_spec,
        out_shape=jax.ShapeDtypeStruct((B, S, W), BF16),
        compiler_params=_cp("parallel", "parallel", "arbitrary"),
        name="diff_attn",
    )(flag, q3, k3, v3, vt3, lam_vecs, subln_g.reshape(1, LANES), subln_g.reshape(LANES, 1))
    return out.reshape(B * S, W)


def _stackmask(m):
    lane = lax.broadcasted_iota(I32, m.shape, 1)
    z = jnp.zeros_like(m)
    return jnp.concatenate([jnp.where(lane < RWKV_HEAD, m, z), jnp.where(lane >= RWKV_HEAD, m, z)], axis=0)


def _pair_sum(x, first):
    s1 = jnp.sum(jnp.where(first, x, 0.0), axis=-1, keepdims=True)
    s2 = jnp.sum(jnp.where(first, 0.0, x), axis=-1, keepdims=True)
    return jnp.where(first, s1, s2)


def _rwkv_scan_kernel(r_ref, k_ref, v_ref, a_ref, ld_ref, g_ref, kk_ref, ka_ref, rk_ref, lnw_ref, lnb_ref,
                      o_ref, s_ref, *, L):
    nb, tb, W = r_ref.shape
    tm = nb * tb
    n_chunks = tm // L
    per_batch = tb // L
    n_pairs = W // LANES
    hd = RWKV_HEAD
    bf = lambda t: t.astype(BF16)
    rows2d = lambda ref: ref[...].reshape(tm, W)

    @pl.when(pl.program_id(0) == 0)
    def _():
        s_ref[...] = jnp.zeros_like(s_ref)

    row = lax.broadcasted_iota(I32, (tm, tm), 0)
    col = lax.broadcasted_iota(I32, (tm, tm), 1)
    tri = jnp.where(jnp.logical_and(col <= row, (col // L) == (row // L)), 1.0, 0.0).astype(BF16)
    ld = rows2d(ld_ref)
    ld_hi = bf(ld)
    rem = ld - ld_hi.astype(F32)
    ld_mid = bf(rem)
    ld_lo = bf(rem - ld_mid.astype(F32))
    c = (jnp.dot(tri, ld_hi, preferred_element_type=F32) + jnp.dot(tri, ld_mid, preferred_element_type=F32)
         + jnp.dot(tri, ld_lo, preferred_element_type=F32))
    ec = jnp.exp(c)
    eci = jnp.exp(-c)
    ecm = jnp.exp(c - ld)
    r = rows2d(r_ref)
    k = rows2d(k_ref)
    v = rows2d(v_ref)
    a = rows2d(a_ref)
    kkr = k * kk_ref[...]
    kmod = k * (1.0 + (a - 1.0) * ka_ref[...])
    brk = r * kmod * rk_ref[...]

    lane = lax.broadcasted_iota(I32, (L, LANES), 1)
    rowl = lax.broadcasted_iota(I32, (L, LANES), 0)
    first = lane < hd
    lane_h = lane & (hd - 1)
    strict = lane_h < rowl
    incl = lane_h <= rowl
    eye = jnp.where(lane_h == rowl, 1.0, 0.0)

    chains = [(ch, p) for ch in range(n_chunks) for p in range(n_pairs)]
    rsl = lambda ch: slice(ch * L, (ch + 1) * L)
    csl = lambda p: slice(p * LANES, (p + 1) * LANES)
    fdot = lambda x, y: jnp.dot(x, y, preferred_element_type=F32)
    at, bt, kt, rt, vh, g_l = {}, {}, {}, {}, {}, {}
    for c_ in chains:
        ch, p = c_
        rs, cs = rsl(ch), csl(p)
        kkh = kkr[rs, cs]
        kkh = kkh / jnp.maximum(jnp.sqrt(_pair_sum(kkh * kkh, first)), 1e-12)
        vh[c_] = v[rs, cs]
        g_l[c_] = ec[ch * L + L - 1:ch * L + L, cs]
        at[c_] = -kkh * ecm[rs, cs]
        bt[c_] = kkh * a[rs, cs] * eci[rs, cs]
        kt[c_] = kmod[rs, cs] * eci[rs, cs]
        rt[c_] = r[rs, cs] * ec[rs, cs]
    gm = {c_: _nt(bf(jnp.concatenate([at[c_], rt[c_]], axis=0)),
                  jnp.concatenate([_stackmask(bf(bt[c_])), _stackmask(bf(kt[c_]))], axis=0)) for c_ in chains}
    a_ab = {c_: jnp.where(strict, gm[c_][:L, :LANES], 0.0) for c_ in chains}
    vsm = {c_: _stackmask(bf(vh[c_])) for c_ in chains}
    cmat = {c_: fdot(bf(jnp.where(strict, gm[c_][:L, LANES:], 0.0)), vsm[c_]) for c_ in chains}
    t_inv = {c_: eye + a_ab[c_] for c_ in chains}
    pw = {c_: bf(a_ab[c_]) for c_ in chains}
    for _ in range(int(math.log2(L)) - 1):
        pw = {c_: bf(fdot(pw[c_], _stackmask(pw[c_]))) for c_ in chains}
        t_inv = {c_: t_inv[c_] + fdot(pw[c_], _stackmask(bf(t_inv[c_]))) for c_ in chains}
    zz = {c_: fdot(bf(t_inv[c_]), jnp.concatenate([_stackmask(bf(at[c_])), _stackmask(bf(cmat[c_]))], axis=1))
          for c_ in chains}
    qy = {c_: fdot(bf(jnp.where(incl, gm[c_][L:, :LANES], 0.0)),
                   jnp.concatenate([_stackmask(bf(zz[c_][:, :LANES])), _stackmask(bf(zz[c_][:, LANES:]))], axis=1))
          for c_ in chains}
    y0 = {c_: qy[c_][:, LANES:] + fdot(bf(jnp.where(incl, gm[c_][L:, LANES:], 0.0)), vsm[c_]) for c_ in chains}
    qa = {c_: bf(jnp.concatenate([rt[c_] + qy[c_][:, :LANES], zz[c_][:, :LANES]], axis=0)) for c_ in chains}
    bkg = {c_: bf(jnp.concatenate([bt[c_] * g_l[c_], kt[c_] * g_l[c_]], axis=0)) for c_ in chains}

    lane_s = lax.broadcasted_iota(I32, (hd, LANES), 1)
    heads = [(b, p) for b in range(nb) for p in range(n_pairs)]
    sp = {bp: s_ref[bp[0] * n_pairs + bp[1]] for bp in heads}
    for j in range(per_batch):
        chunk = lambda b: b * per_batch + j
        yw = {(b, p): _nt(qa[chunk(b), p], _stackmask(bf(sp[b, p]))) for b, p in heads}
        upd = {(b, p): _tn(bf(jnp.concatenate([yw[b, p][L:] + zz[chunk(b), p][:, LANES:], vh[chunk(b), p]],
                                              axis=0)), bkg[chunk(b), p]) for b, p in heads}
        for b, p in heads:
            ch, cs, ts = chunk(b), csl(p), rsl(j)
            sp[b, p] = sp[b, p] * g_l[ch, p] + jnp.where(lane_s < hd, upd[b, p][:hd], upd[b, p][hd:])
            y = yw[b, p][:L] + y0[ch, p]
            mean = _pair_sum(y, first) * (1.0 / hd)
            yc = y - mean
            var = _pair_sum(yc * yc, first) * (1.0 / hd)
            yn = yc * lax.rsqrt(var + GN_EPS) * lnw_ref[:, cs] + lnb_ref[:, cs]
            bonus = _pair_sum(brk[rsl(ch), cs], first) * vh[ch, p]
            o_ref[b, ts, cs] = ((yn + bonus) * g_ref[b, ts, cs]).astype(o_ref.dtype)
    for b, p in heads:
        s_ref[b * n_pairs + p] = sp[b, p]


def _rwkv_scan(r, k, v, a, ld, g, k_k, k_a, r_k, ln_w, ln_b, L=RWKV_CHUNK, tm=128):
    B, S, W = r.shape
    seq = pl.BlockSpec((B, tm, W), lambda c: (0, c, 0))
    vec = pl.BlockSpec((1, W), lambda c: (0, 0))
    return pl.pallas_call(
        functools.partial(_rwkv_scan_kernel, L=L),
        grid=(S // tm,),
        in_specs=[seq] * 6 + [vec] * 5,
        out_specs=seq,
        out_shape=jax.ShapeDtypeStruct((B, S, W), BF16),
        scratch_shapes=[pltpu.VMEM((B * (W // LANES), RWKV_HEAD, LANES), F32)],
        compiler_params=_cp("arbitrary"),
        name="rwkv_scan",
    )(r, k, v, a, ld, g, k_k.reshape(1, W), k_a.reshape(1, W), r_k.reshape(1, W),
      ln_w.reshape(1, W), ln_b.reshape(1, W))


def _post_kernel(attn_ref, rw_ref, gate_ref, x_ref, mod_ref, wa_ref, wb_ref, wo_ref, g2_ref, wrh_ref, wrm_ref,
                 sug_ref, sd_ref, base_ref, h2p_ref, lg_ref):
    D = x_ref.shape[1]
    ya = jnp.dot(attn_ref[...], wa_ref[...], preferred_element_type=F32)
    yb = jnp.dot(rw_ref[...], wb_ref[...], preferred_element_type=F32)
    m = gate_ref[:, 0:D] * ya + gate_ref[:, D:] * yb
    x1 = x_ref[...] + mod_ref[2:3, :] * jnp.dot(m.astype(BF16), wo_ref[...], preferred_element_type=F32)
    y = x1 * lax.rsqrt(jnp.mean(x1 * x1, axis=-1, keepdims=True) + NORM_EPS) * g2_ref[...]
    h2 = y * (1.0 + mod_ref[4:5, :]) + mod_ref[3:4, :]
    hb = h2.astype(BF16)
    hm = (h2 - hb.astype(F32)).astype(BF16)
    lg_ref[...] = _nt(wrh_ref[...], hb) + _nt(wrh_ref[...], hm) + _nt(wrm_ref[...], hb)
    packed = _pack_halves(hb.astype(F32))
    half = packed.shape[1] // 2
    h2p_ref[0] = packed[:, :half]
    h2p_ref[1] = packed[:, half:]
    F = sd_ref.shape[0]
    gu = jnp.dot(hb, sug_ref[...], preferred_element_type=F32)
    shared = jnp.dot((_silu(gu[:, :F]) * gu[:, F:]).astype(BF16), sd_ref[...], preferred_element_type=F32)
    base_ref[...] = x1 + mod_ref[5:6, :] * shared


def _post(attn, rw, gate, x2, mod, wa, wb, wo, norm2_g, w_router_t, sug, sd, S, tm=512):
    N, D = x2.shape
    E = w_router_t.shape[0]
    wr_hi = w_router_t.astype(BF16)
    wr_mid = (w_router_t - wr_hi.astype(F32)).astype(BF16)
    tpb = S // tm
    row = lambda w: pl.BlockSpec((tm, w), lambda i: (i, 0))
    full = lambda a: pl.BlockSpec(a.shape, lambda i: (0, 0))
    return pl.pallas_call(
        _post_kernel,
        grid=(N // tm,),
        in_specs=[row(attn.shape[1]), row(rw.shape[1]), row(gate.shape[1]), row(D),
                  pl.BlockSpec((None, 6, D), lambda i: (i // tpb, 0, 0)),
                  full(wa), full(wb), full(wo), pl.BlockSpec((1, D), lambda i: (0, 0)), full(wr_hi), full(wr_mid),
                  full(sug), full(sd)],
        out_specs=[row(D), pl.BlockSpec((2, tm, D // 4), lambda i: (0, i, 0)), pl.BlockSpec((E, tm), lambda i: (0, i))],
        out_shape=[jax.ShapeDtypeStruct((N, D), F32), jax.ShapeDtypeStruct((2, N, D // 4), U32),
                   jax.ShapeDtypeStruct((E, N), F32)],
        compiler_params=_cp("parallel"),
        name="post_mixer",
    )(attn, rw, gate, x2, mod, wa, wb, wo, norm2_g.reshape(1, D), wr_hi, wr_mid, sug, sd)


def _first_argmax(x, idx, sentinel):
    m = jnp.max(x, axis=0, keepdims=True)
    return m, jnp.min(jnp.where(x == m, idx, sentinel), axis=0, keepdims=True)


def _route_kernel(lg_ref, bias_ref, e_ref, w_ref):
    E, T = lg_ref.shape
    gsz = E // N_GROUPS
    scores = _sigmoid(lg_ref[...])
    biased = scores + bias_ref[...]
    ig = lax.broadcasted_iota(I32, (gsz, T), 0)
    gs = []
    for g in range(N_GROUPS):
        blk = biased[g * gsz:(g + 1) * gsz, :]
        m1, i1 = _first_argmax(blk, ig, gsz)
        m2 = jnp.max(jnp.where(ig == i1, -jnp.inf, blk), axis=0, keepdims=True)
        gs.append(m1 + m2)
    gsc = jnp.concatenate(gs, axis=0)
    i8 = lax.broadcasted_iota(I32, (N_GROUPS, T), 0)
    chosen = jnp.zeros((N_GROUPS, T), F32)
    for _ in range(TOPK_GROUPS):
        _, gi = _first_argmax(gsc, i8, N_GROUPS)
        hit = i8 == gi
        chosen = jnp.where(hit, 1.0, chosen)
        gsc = jnp.where(hit, -jnp.inf, gsc)
    masked = jnp.concatenate(
        [jnp.where(chosen[g:g + 1, :] > 0.0, biased[g * gsz:(g + 1) * gsz, :], -jnp.inf)
         for g in range(N_GROUPS)], axis=0)
    ie = lax.broadcasted_iota(I32, (E, T), 0)
    idxs, wts = [], []
    for _ in range(TOP_K):
        _, ei = _first_argmax(masked, ie, E)
        hit = ie == ei
        idxs.append(ei)
        wts.append(jnp.sum(jnp.where(hit, scores, 0.0), axis=0, keepdims=True))
        masked = jnp.where(hit, -jnp.inf, masked)
    wt = jnp.concatenate(wts, axis=0)
    e_ref[...] = jnp.concatenate(idxs, axis=0)
    w_ref[...] = wt / jnp.sum(wt, axis=0, keepdims=True) * ROUTED_SCALE


def _route(logits_t, router_bias, T=512):
    E, N = logits_t.shape
    blk = pl.BlockSpec((TOP_K, T), lambda i: (0, i))
    return pl.pallas_call(
        _route_kernel,
        grid=(N // T,),
        in_specs=[pl.BlockSpec((E, T), lambda i: (0, i)), pl.BlockSpec((E, 1), lambda i: (0, 0))],
        out_specs=[blk, blk],
        out_shape=[jax.ShapeDtypeStruct((TOP_K, N), I32), jax.ShapeDtypeStruct((TOP_K, N), F32)],
        compiler_params=_cp("parallel"),
        name="route",
    )(logits_t, router_bias.reshape(E, 1))


def _rank_kernel(e_ref, rank_ref, cnt_ref, carry_ref, *, n_experts):
    T = e_ref.shape[1]

    @pl.when(pl.program_id(0) == 0)
    def _():
        carry_ref[...] = jnp.zeros_like(carry_ref)

    ie = lax.broadcasted_iota(I32, (n_experts, T), 0)
    e = e_ref[...]
    hits = [ie == e[kk:kk + 1, :] for kk in range(TOP_K)]
    onehot = jnp.zeros((n_experts, T), F32)
    for hsel in hits:
        onehot = onehot + hsel.astype(F32)
    tr = lax.broadcasted_iota(I32, (T, T), 0)
    tc = lax.broadcasted_iota(I32, (T, T), 1)
    before = (tr < tc).astype(BF16)
    base = _bdot(onehot, before) + carry_ref[:, 0:1]
    rank_ref[...] = jnp.concatenate(
        [jnp.sum(jnp.where(hsel, base, 0.0), axis=0, keepdims=True) for hsel in hits], axis=0).astype(I32)
    carry_ref[...] = carry_ref[...] + jnp.sum(onehot, axis=1, keepdims=True)
    cnt_ref[...] = carry_ref[...]


def _ranks(eidx_t, n_experts, T=512):
    N = eidx_t.shape[1]
    T = min(T, N)
    blk = pl.BlockSpec((TOP_K, T), lambda i: (0, i))
    cnt = pl.BlockSpec((n_experts, LANES), lambda i: (0, 0))
    return pl.pallas_call(
        functools.partial(_rank_kernel, n_experts=n_experts),
        grid=(N // T,),
        in_specs=[blk],
        out_specs=[blk, cnt],
        out_shape=[jax.ShapeDtypeStruct((TOP_K, N), I32), jax.ShapeDtypeStruct((n_experts, LANES), F32)],
        scratch_shapes=[pltpu.VMEM((n_experts, LANES), F32)],
        compiler_params=_cp("arbitrary"),
        name="ranks",
    )(eidx_t)


def _dest_kernel(e_ref, rank_ref, start_ref, d_ref):
    E = start_ref.shape[0]
    T = e_ref.shape[1]
    ie = lax.broadcasted_iota(I32, (E, T), 0)
    e = e_ref[...]
    start = start_ref[:, 0:1]
    rows = [jnp.sum(jnp.where(ie == e[kk:kk + 1, :], start, 0.0), axis=0, keepdims=True) for kk in range(TOP_K)]
    d_ref[...] = jnp.concatenate(rows, axis=0).astype(I32) + rank_ref[...]


def _dests(eidx_t, rank_t, pstart, T=2048):
    N = eidx_t.shape[1]
    T = min(T, N)
    E = pstart.shape[0]
    blk = pl.BlockSpec((TOP_K, T), lambda i: (0, i))
    return pl.pallas_call(
        _dest_kernel,
        grid=(N // T,),
        in_specs=[blk, blk, pl.BlockSpec((E, LANES), lambda i: (0, 0))],
        out_specs=blk,
        out_shape=jax.ShapeDtypeStruct((TOP_K, N), I32),
        compiler_params=_cp("parallel"),
        name="dests",
    )(eidx_t, rank_t, jnp.broadcast_to(pstart.astype(F32)[:, None], (E, LANES)))


def _expert_kernel(us_ref, ps_ref, x_hbm, wug_hbm, wd_hbm, y_hbm,
                   wug_f32, wd_f32, wug_bf, wd_bf, xbuf, ybuf, cnt_ref, wsem, xsem, ysem, *, n_rows):
    e = pl.program_id(0)
    n_experts = pl.num_programs(0)
    _, _, R, Ch = xbuf.shape
    C = 2 * Ch
    F = wd_hbm.shape[1]

    def w_copies(expert, slot):
        return (pltpu.make_async_copy(wug_hbm.at[expert], wug_f32.at[slot], wsem.at[slot]),
                pltpu.make_async_copy(wd_hbm.at[expert], wd_f32.at[slot], wsem.at[slot]))

    def x_copy(slot, half, row):
        return pltpu.make_async_copy(x_hbm.at[half, pl.ds(row, R)], xbuf.at[slot, half], xsem.at[slot])

    def y_copy(slot, half, row):
        return pltpu.make_async_copy(ybuf.at[slot, half], y_hbm.at[half, pl.ds(row, R)], ysem.at[slot])

    def start(copy, slot, row):
        for half in range(2):
            copy(slot, half, row).start()

    def wait(copy, slot):
        for half in range(2):
            copy(slot, half, 0).wait()

    @pl.when(e == 0)
    def _():
        for ahead in range(W_SLOTS - 1):
            for cp in w_copies(ahead, ahead):
                cp.start()
        cnt_ref[0] = 0
        for t in range(X_AHEAD):
            start(x_copy, t, t * R)
        ybuf[...] = jnp.zeros_like(ybuf)
        for slot in range(Y_SLOTS):
            start(y_copy, slot, n_rows + slot * R)

    @pl.when(e + (W_SLOTS - 1) < n_experts)
    def _():
        for cp in w_copies(e + (W_SLOTS - 1), lax.rem(e + (W_SLOTS - 1), W_SLOTS)):
            cp.start()

    wslot = lax.rem(e, W_SLOTS)
    for cp in w_copies(e, wslot):
        cp.wait()
    wug_bf[...] = wug_f32[wslot].astype(BF16)
    wd_bf[...] = wd_f32[wslot].astype(BF16)
    n_valid = us_ref[e + 1] - us_ref[e]
    pbase = ps_ref[e]
    n_tiles = lax.shift_right_logical(n_valid + (R - 1), int(math.log2(R)))

    def make_tile(nb):
        rowid = lax.broadcasted_iota(I32, (nb * R, C), 0)

        def tile(blk0):
            n_done = cnt_ref[0]
            xslots = [lax.rem(n_done + b, X_SLOTS) for b in range(nb)]
            for b in range(nb):
                ahead = n_done + X_AHEAD + b
                start(x_copy, lax.rem(ahead, X_SLOTS), pl.multiple_of(ahead * R, R))
            for b in range(nb):
                wait(x_copy, xslots[b])
            x = jnp.concatenate([jnp.concatenate([xbuf[xs, 0], xbuf[xs, 1]], axis=1) for xs in xslots], axis=0)
            x = jnp.where(rowid < n_valid - blk0 * R, x, jnp.uint32(0))
            lo, hi = _unpack_halves(x)
            gu = (jnp.dot(lo.astype(BF16), wug_bf[0:C, :], preferred_element_type=F32)
                  + jnp.dot(hi.astype(BF16), wug_bf[C:, :], preferred_element_type=F32))
            hid = (_silu(gu[:, :F]) * gu[:, F:]).astype(BF16)
            y = jnp.dot(hid, wd_bf[...], preferred_element_type=F32)
            packed = _pack_halves(y.astype(BF16).astype(F32))
            for b in range(nb):
                slot = lax.rem(n_done + b, Y_SLOTS)
                wait(y_copy, slot)
                ybuf[slot, 0] = packed[b * R:(b + 1) * R, :Ch]
                ybuf[slot, 1] = packed[b * R:(b + 1) * R, Ch:]
                start(y_copy, slot, pl.multiple_of(pbase + (blk0 + b) * R, R))
            cnt_ref[0] = n_done + nb

        return tile

    tiles = {nb: make_tile(nb) for nb in range(1, TILE_BLOCKS + 1)}
    quads = lax.shift_right_logical(n_tiles, 2)
    left = n_tiles & 3
    five = jnp.logical_and(left == 1, quads >= 1)
    quads = quads - jnp.where(five, 1, 0)

    def quad_tiles(j, carry):
        tiles[4](4 * j)
        return carry

    lax.fori_loop(0, quads, quad_tiles, 0)

    @pl.when(five)
    def _():
        tiles[5](n_tiles - 5)

    for nb in (1, 2, 3):
        @pl.when(jnp.logical_and(left == nb, jnp.logical_not(five)))
        def _(nb=nb):
            tiles[nb](n_tiles - nb)

    @pl.when(e == pl.num_programs(0) - 1)
    def _():
        for t in range(X_AHEAD):
            wait(x_copy, lax.rem(cnt_ref[0] + t, X_SLOTS))
        for slot in range(Y_SLOTS):
            wait(y_copy, slot)
        ybuf[0] = jnp.zeros((2, R, Ch), U32)
        first = lax.shift_right_logical(pbase + n_tiles * R, int(math.log2(R)))
        n_left = n_rows // R - first

        def fill(t, carry):
            start(y_copy, 0, pl.multiple_of((first + t) * R, R))
            return carry

        def drain(t, carry):
            wait(y_copy, 0)
            return carry

        lax.fori_loop(0, n_left, fill, 0)
        lax.fori_loop(0, n_left, drain, 0)


def _experts(ustart, pstart, xg, w_ug, w_d, n_rows, R):
    _, _, Ch = xg.shape
    E, D, F2 = w_ug.shape
    F = w_d.shape[1]
    grid_spec = pltpu.PrefetchScalarGridSpec(
        num_scalar_prefetch=2,
        grid=(E,),
        in_specs=[pl.BlockSpec(memory_space=pl.ANY)] * 3,
        out_specs=pl.BlockSpec(memory_space=pl.ANY),
        scratch_shapes=[pltpu.VMEM((W_SLOTS, D, F2), F32), pltpu.VMEM((W_SLOTS, F, D), F32),
                        pltpu.VMEM((D, F2), BF16), pltpu.VMEM((F, D), BF16),
                        pltpu.VMEM((X_SLOTS, 2, R, Ch), U32), pltpu.VMEM((Y_SLOTS, 2, R, Ch), U32), pltpu.SMEM((1,), I32),
                        pltpu.SemaphoreType.DMA((W_SLOTS,)), pltpu.SemaphoreType.DMA((X_SLOTS,)),
                        pltpu.SemaphoreType.DMA((Y_SLOTS,))],
    )
    return pl.pallas_call(
        functools.partial(_expert_kernel, n_rows=n_rows),
        grid_spec=grid_spec,
        out_shape=jax.ShapeDtypeStruct((2, n_rows + Y_SLOTS * R, Ch), U32),
        compiler_params=_cp("arbitrary"),
        name="experts",
    )(ustart, pstart, xg, w_ug, w_d)


def _sc_scatter_rows(src, idx, n_rows):
    H, N, C = src.shape
    K = idx.shape[0]
    per_row = N // SC_WINDOW
    mesh = plsc.VectorSubcoreMesh(core_axis_name="c", subcore_axis_name="s")

    @functools.partial(pl.kernel, out_type=jax.ShapeDtypeStruct((H, n_rows, C), src.dtype), mesh=mesh,
                       scratch_types=[])
    def scatter_kernel(x_hbm, i_hbm, o_hbm):
        for h in range(H):
            def body(x_vmem, i_vmem):
                for k in range(K):
                    pltpu.sync_copy(x_vmem, o_hbm.at[h].at[i_vmem.at[k]])

            pltpu.emit_pipeline(
                body,
                grid=(per_row,),
                in_specs=[pl.BlockSpec((SC_WINDOW, C), lambda i: (i, 0)),
                          pl.BlockSpec((K, SC_WINDOW), lambda i: (0, i))],
                out_specs=[],
                core_axis_name=("c", "s"),
                dimension_semantics=(pltpu.PARALLEL,),
            )(x_hbm.at[h], i_hbm)

    return scatter_kernel(src, idx)


def _sc_gather_rows(src, idx):
    H, _, C = src.shape
    K, N = idx.shape
    per_row = N // SC_WINDOW
    mesh = plsc.VectorSubcoreMesh(core_axis_name="c", subcore_axis_name="s")

    @functools.partial(pl.kernel, out_type=jax.ShapeDtypeStruct((H, K * N, C), src.dtype), mesh=mesh,
                       scratch_types=[])
    def gather_kernel(x_hbm, i_hbm, o_hbm):
        for h in range(H):
            def body(i_vmem, o_vmem):
                pltpu.sync_copy(x_hbm.at[h].at[i_vmem.at[0]], o_vmem)

            pltpu.emit_pipeline(
                body,
                grid=(K * per_row,),
                in_specs=[pl.BlockSpec((1, SC_WINDOW), lambda i: (i // per_row, i % per_row))],
                out_specs=[pl.BlockSpec((SC_WINDOW, C), lambda i: (i, 0))],
                core_axis_name=("c", "s"),
                dimension_semantics=(pltpu.PARALLEL,),
            )(i_hbm, o_hbm.at[h])

    return gather_kernel(src, idx)


def _combine_kernel(*refs):
    y_refs = refs[:2 * TOP_K]
    w_ref, base_ref, mod_ref, o_ref = refs[2 * TOP_K:]
    T = base_ref.shape[0]
    tr = lax.broadcasted_iota(I32, (T, T), 0)
    tc = lax.broadcasted_iota(I32, (T, T), 1)
    wcol = _nt((tr == tc).astype(F32), w_ref[...], precision=HI)
    acc = [None] * 4
    for kk in range(TOP_K):
        wk = wcol[:, kk:kk + 1]
        for half in range(2):
            lo, hi = _unpack_halves(y_refs[2 * kk + half][...])
            for q, val in ((half, lo), (2 + half, hi)):
                acc[q] = val * wk if acc[q] is None else acc[q] + val * wk
    o_ref[...] = base_ref[...] + mod_ref[5:6, :] * jnp.concatenate(acc, axis=1)


def _combine(yg, w_t, base, mod, S, T=512):
    N, D = base.shape
    C = yg.shape[2]
    tpb = S // T
    n_tiles = N // T
    row = pl.BlockSpec((T, D), lambda i: (i, 0))
    piece = lambda kk, half: pl.BlockSpec((None, T, C), lambda i: (half, kk * n_tiles + i, 0))
    return pl.pallas_call(
        _combine_kernel,
        grid=(n_tiles,),
        in_specs=[piece(kk, half) for kk in range(TOP_K) for half in range(2)] + [
            pl.BlockSpec((TOP_K, T), lambda i: (0, i)),
            row,
            pl.BlockSpec((None, 6, D), lambda i: (i // tpb, 0, 0))],
        out_specs=row,
        out_shape=jax.ShapeDtypeStruct((N, D), F32),
        compiler_params=_cp("parallel"),
        name="combine",
    )(*([yg] * (2 * TOP_K)), w_t, base, mod)


def _layer(x, c, positions, layer_idx, w_ada, b_ada, norm1_g, w_in, w_gate, b_gate,
           q_norm_g, k_norm_g, lambda_q1, lambda_k1, lambda_q2, lambda_k2, subln_g,
           rwkv_mu, w_decay0, w_decay2, a0, a2, g2, k_k, k_a, r_k, ln_x_w, ln_x_b,
           w_branch_a, w_branch_b, w_out, norm2_g, w_router, router_bias,
           w_expert_up_gate, w_expert_down, w_shared_up_gate, w_shared_down):
    B, S, D = x.shape
    N = B * S
    E = w_router.shape[1]
    da_width = w_branch_a.shape[0]
    rw_width = w_branch_b.shape[0]
    lambda_init = 0.8 - 0.6 * math.exp(-0.3 * layer_idx)

    mod = _adaln(c, w_ada, b_ada)
    x2 = x.reshape(N, D)
    w_cat = jnp.concatenate([w_in, w_gate], axis=1).astype(BF16)
    qn, kn, v, r_, k_, v_, a_, ld_, g_, gate = _mixer_in(
        x2, positions.reshape(N, 1), mod, norm1_g, w_cat, b_gate, q_norm_g, k_norm_g,
        rwkv_mu, w_decay0, w_decay2, a0, a2, g2, S, da_width, rw_width)

    lam_vecs = jnp.stack([lambda_q1, lambda_k1, lambda_q2, lambda_k2])
    score_bound = 1.01 * DA_HEAD_DIM ** 0.5 * jnp.max(jnp.abs(q_norm_g)) * jnp.max(jnp.abs(k_norm_g))
    attn = _diff_attention(qn, kn, v, score_bound, lam_vecs, subln_g, B, S, lambda_init)

    seq = lambda t: t.reshape(B, S, rw_width)
    rw = _rwkv_scan(seq(r_), seq(k_), seq(v_), seq(a_), seq(ld_), seq(g_), k_k, k_a, r_k.reshape(-1),
                    ln_x_w, ln_x_b).reshape(N, rw_width)

    base, h2p, logits_t = _post(attn, rw, gate, x2, mod, w_branch_a.astype(BF16), w_branch_b.astype(BF16),
                                w_out.astype(BF16), norm2_g, w_router.T,
                                w_shared_up_gate.astype(BF16), w_shared_down.astype(BF16), S)

    eidx_t, w_t = _route(logits_t, router_bias)
    rank_t, counts = _ranks(eidx_t, E)
    R = EXPERT_TILE
    cnt = counts[:, 0].astype(I32)
    ustart = jnp.concatenate([jnp.zeros((1,), I32), jnp.cumsum(cnt)])
    pcnt = (cnt + R - 1) // R * R
    pstart = jnp.cumsum(pcnt) - pcnt
    dest_p = _dests(eidx_t, rank_t, pstart)
    n_rows = (N * TOP_K + E * (R - 1) + R - 1) // R * R
    xg = _sc_scatter_rows(h2p, dest_p, n_rows + X_AHEAD * R)
    y = _experts(ustart, pstart, xg, w_expert_up_gate, w_expert_down, n_rows, R)
    yg = _sc_gather_rows(y, dest_p)
    out = _combine(yg, w_t, base, mod, S)
    return out.reshape(B, S, D)


def kernel(x, c, positions, w_ada, b_ada, norm1_g, w_in, w_gate, b_gate, q_norm_g, k_norm_g, lambda_q1, lambda_k1, lambda_q2, lambda_k2, subln_g, rwkv_mu, w_decay0, w_decay2, a0, a2, g2, k_k, k_a, r_k, ln_x_w, ln_x_b, w_branch_a, w_branch_b, w_out, norm2_g, w_router, router_bias, w_expert_up_gate, w_expert_down, w_shared_up_gate, w_shared_down):
    for l in range(w_ada.shape[0]):
        x = _layer(x, c, positions, l, w_ada[l], b_ada[l], norm1_g[l], w_in[l], w_gate[l], b_gate[l],
                   q_norm_g[l], k_norm_g[l], lambda_q1[l], lambda_k1[l], lambda_q2[l], lambda_k2[l],
                   subln_g[l], rwkv_mu[l], w_decay0[l], w_decay2[l], a0[l], a2[l], g2[l], k_k[l],
                   k_a[l], r_k[l], ln_x_w[l], ln_x_b[l], w_branch_a[l], w_branch_b[l], w_out[l],
                   norm2_g[l], w_router[l], router_bias[l], w_expert_up_gate[l], w_expert_down[l],
                   w_shared_up_gate[l], w_shared_down[l])
    return x
```

```python
import functools
import math

import jax
import jax.numpy as jnp
from jax import lax
from jax.experimental import pallas as pl
from jax.experimental.pallas import tpu as pltpu
from jax.experimental.pallas import tpu_sc as plsc

F32 = jnp.float32
BF16 = jnp.bfloat16
I32 = jnp.int32
U32 = jnp.uint32
HI = lax.Precision.HIGHEST

CHUNK = 64
ROPE_THETA = 10000.0
NORM_EPS = 1e-6
SUBLN_EPS = 1e-5
DA_HEAD_DIM = 64
RWKV_HEAD = 64
GN_EPS = 64e-5
TOP_K = 8
N_GROUPS = 8
TOPK_GROUPS = 4
ROUTED_SCALE = 2.5
EXPERT_TILE = 128
W_SLOTS = 3
TILE_BLOCKS = 5
X_SLOTS = 13
X_AHEAD = X_SLOTS - TILE_BLOCKS
Y_SLOTS = 8
RWKV_CHUNK = 64
LANES = 128
SC_WINDOW = 128
NEG = -1e30
MAX_PLAIN_SCORE = 40.0
VMEM_LIMIT = 56 * 1024 * 1024


def _cp(*sem):
    return pltpu.CompilerParams(dimension_semantics=sem, vmem_limit_bytes=VMEM_LIMIT)


def _bdot(a, b):
    return jnp.dot(a.astype(BF16), b.astype(BF16), preferred_element_type=F32)


def _fdot(a, b):
    return jnp.dot(a, b, precision=HI, preferred_element_type=F32)


def _nt(a, b, precision=None):
    return lax.dot_general(a, b, (((1,), (1,)), ((), ())), precision=precision,
                           preferred_element_type=F32)


def _tn(a, b, precision=None):
    return lax.dot_general(a, b, (((0,), (0,)), ((), ())), precision=precision,
                           preferred_element_type=F32)


def _pack_halves(x):
    c = x.shape[1] // 2
    lo = lax.bitcast_convert_type(x[:, :c], U32)
    hi = lax.bitcast_convert_type(x[:, c:], U32)
    return (hi & jnp.uint32(0xFFFF0000)) | (lo >> 16)


def _unpack_halves(w):
    lo = lax.bitcast_convert_type(w << 16, F32)
    hi = lax.bitcast_convert_type(w & jnp.uint32(0xFFFF0000), F32)
    return lo, hi


def _sigmoid(x):
    return 1.0 / (1.0 + jnp.exp(-x))


def _silu(x):
    return x * _sigmoid(x)


def _ada_kernel(c_ref, w_ref, b_ref, o_ref):
    o_ref[...] = _fdot(_silu(c_ref[...]), w_ref[...]) + b_ref[...]


def _adaln(c, w_ada, b_ada):
    B, D = c.shape
    rows = -(-B // 8) * 8
    cpad = jnp.zeros((rows, D), F32).at[:B].set(c)
    n_out = w_ada.shape[1]
    out = pl.pallas_call(
        _ada_kernel,
        grid=(n_out // D,),
        in_specs=[pl.BlockSpec((rows, D), lambda j: (0, 0)),
                  pl.BlockSpec((D, D), lambda j: (0, j)),
                  pl.BlockSpec((1, D), lambda j: (0, j))],
        out_specs=pl.BlockSpec((rows, D), lambda j: (0, j)),
        out_shape=jax.ShapeDtypeStruct((rows, n_out), F32),
        compiler_params=_cp("arbitrary"),
        name="adaln",
    )(cpad, w_ada, b_ada.reshape(1, n_out))
    return out[:B].reshape(B, n_out // D, D)


def _mixer_in_kernel(x_ref, xprev_ref, mod_ref, g_ref, w_ref, bg_ref, pos_ref, invf_ref, qg_ref, kg_ref,
                     mu_ref, w0_ref, w2_ref, a0_ref, a2_ref, g2_ref,
                     qn_ref, kn_ref, v_ref, r_ref, k_ref, vr_ref, a_ref, ld_ref, gr_ref, gate_ref,
                     *, da_width, rw_cols, rw_width, q_scale, tiles_per_seq):
    tm = x_ref.shape[0]

    def modulated(x):
        y = x * lax.rsqrt(jnp.mean(x * x, axis=-1, keepdims=True) + NORM_EPS) * g_ref[...]
        return (y * (1.0 + mod_ref[1:2, :]) + mod_ref[0:1, :]).astype(BF16)

    def proj(hb, c0, width, step=512):
        parts = [jnp.dot(hb, w_ref[:, c0 + o:c0 + min(o + step, width)], preferred_element_type=F32)
                 for o in range(0, width, step)]
        return parts[0] if len(parts) == 1 else jnp.concatenate(parts, axis=1)

    h = modulated(x_ref[...])

    lane = lax.broadcasted_iota(I32, (tm, LANES), 1)
    first = lane < DA_HEAD_DIM
    lo_half = (lane & (DA_HEAD_DIM - 1)) < DA_HEAD_DIM // 2
    ang = pos_ref[...].astype(F32) * invf_ref[...]
    cos = jnp.cos(ang)
    sin = jnp.sin(ang)
    sin = jnp.where(lo_half, -sin, sin)
    for c0, dst, gn_ref, mult in ((0, qn_ref, qg_ref, q_scale), (da_width, kn_ref, kg_ref, 1.0)):
        raw = proj(h, c0, da_width)
        for blk in range(da_width // LANES):
            x = raw[:, blk * LANES:(blk + 1) * LANES]
            xx = x * x
            s_first = jnp.sum(jnp.where(first, xx, 0.0), axis=-1, keepdims=True)
            s_second = jnp.sum(jnp.where(first, 0.0, xx), axis=-1, keepdims=True)
            ms = jnp.where(first, s_first, s_second) * (1.0 / DA_HEAD_DIM)
            xn = x * lax.rsqrt(ms + NORM_EPS) * gn_ref[...]
            rot = jnp.where(lo_half, pltpu.roll(xn, LANES - DA_HEAD_DIM // 2, axis=1),
                            pltpu.roll(xn, DA_HEAD_DIM // 2, axis=1))
            dst[:, blk * LANES:(blk + 1) * LANES] = ((xn * cos + rot * sin) * mult).astype(dst.dtype)
    v_ref[...] = proj(h, 2 * da_width, da_width).astype(v_ref.dtype)

    c_rw = 3 * da_width
    p = proj(h, c_rw, rw_cols)
    p_before = proj(modulated(xprev_ref[...]), c_rw, rw_cols)
    seq_start = (pl.program_id(0) % tiles_per_seq) == 0
    last_prev = jnp.where(seq_start, 0.0, p_before[7:8, :])
    rowi = lax.broadcasted_iota(I32, p.shape, 0)
    prev = jnp.where(rowi == 0, last_prev, pltpu.roll(p, 1, axis=0))
    xs = p + (prev - p) * mu_ref[...]
    width = rw_width
    r_ref[...] = xs[:, 0:width]
    k_ref[...] = xs[:, width:2 * width]
    vr_ref[...] = xs[:, 2 * width:3 * width]
    xwa = xs[:, 3 * width:3 * width + LANES]
    xg = xs[:, 3 * width + LANES:]
    z = w0_ref[...] + _bdot(jnp.tanh(xwa), w2_ref[...])
    w = -(jnp.maximum(-z, 0.0) + jnp.log(1.0 + jnp.exp(-jnp.abs(z)))) - 0.5
    ld_ref[...] = -jnp.exp(w)
    a_ref[...] = _sigmoid(a0_ref[...] + _bdot(xwa, a2_ref[...]))
    gr_ref[...] = _bdot(_sigmoid(xg), g2_ref[...])

    gate_ref[...] = _sigmoid(proj(h, c_rw + rw_cols, gate_ref.shape[1]) + bg_ref[...]).astype(gate_ref.dtype)


def _mixer_in(x2, pos2, mod, norm1_g, w_cat, b_gate, q_norm_g, k_norm_g, mu, w_decay0, w_decay2, a0, a2, g2,
              S, da_width, rw_width, tm=512):
    N, D = x2.shape
    n_gate = b_gate.shape[0]
    rw_cols = mu.shape[0]
    tpb = S // tm
    d = DA_HEAD_DIM
    inv_freq = 1.0 / (ROPE_THETA ** (jnp.arange(0, d, 2, dtype=F32) / d))
    invf = jnp.tile(inv_freq, LANES // (d // 2)).reshape(1, LANES)
    dl, al = w_decay2.shape[0], a2.shape[0]
    assert dl + al == LANES and g2.shape[0] == LANES
    w2p = jnp.zeros((LANES, rw_width), F32).at[:dl].set(w_decay2)
    a2p = jnp.zeros((LANES, rw_width), F32).at[dl:].set(a2)
    kern = functools.partial(_mixer_in_kernel, da_width=da_width, rw_cols=rw_cols, rw_width=rw_width,
                             q_scale=d ** -0.5 * math.log2(math.e),
                             tiles_per_seq=tpb)
    row = lambda w: pl.BlockSpec((tm, w), lambda i: (i, 0))
    vec = lambda n: pl.BlockSpec((1, n), lambda i: (0, 0))
    mat = pl.BlockSpec((LANES, rw_width), lambda i: (0, 0))
    f32 = lambda w: jax.ShapeDtypeStruct((N, w), F32)
    bf16 = lambda w: jax.ShapeDtypeStruct((N, w), BF16)
    return pl.pallas_call(
        kern,
        grid=(N // tm,),
        in_specs=[row(D),
                  pl.BlockSpec((8, D), lambda i: (jnp.maximum(i * (tm // 8) - 1, 0), 0)),
                  pl.BlockSpec((None, 6, D), lambda i: (i // tpb, 0, 0)),
                  vec(D),
                  pl.BlockSpec(w_cat.shape, lambda i: (0, 0)),
                  vec(n_gate),
                  pl.BlockSpec((tm, 1), lambda i: (i, 0)),
                  vec(LANES), vec(LANES), vec(LANES),
                  vec(rw_cols), vec(rw_width), mat, vec(rw_width), mat, mat],
        out_specs=[row(da_width)] * 3 + [row(rw_width)] * 6 + [row(n_gate)],
        out_shape=[bf16(da_width)] * 3 + [f32(rw_width)] * 6 + [bf16(n_gate)],
        compiler_params=_cp("parallel"),
        name="mixer_in",
    )(x2, x2, mod, norm1_g.reshape(1, D), w_cat, b_gate.reshape(1, n_gate), pos2, invf,
      jnp.tile(q_norm_g, 2).reshape(1, LANES), jnp.tile(k_norm_g, 2).reshape(1, LANES),
      mu.reshape(1, rw_cols), w_decay0.reshape(1, rw_width), w2p, a0.reshape(1, rw_width), a2p, g2)


def _attn_kernel(flag_ref, q_ref, k_ref, v_ref, vt_ref, lam_ref, sg_ref, sgc_ref, o_ref,
                 qz_ref, m_ref, l_ref, acc_ref, lpt_ref, acct_ref, *, tq, lambda_init):
    i = pl.program_id(2)
    lane = lax.broadcasted_iota(I32, (tq, LANES), 1)
    q = q_ref[...]
    zero = jnp.zeros_like(q)
    qz_ref[0:tq, :] = jnp.where(lane < DA_HEAD_DIM, q, zero)
    qz_ref[tq:, :] = jnp.where(lane >= DA_HEAD_DIM, q, zero)
    bounded = flag_ref[0] == 1
    lv = lam_ref[...]
    lam = (jnp.exp(jnp.sum(lv[0:1] * lv[1:2], keepdims=True))
           - jnp.exp(jnp.sum(lv[2:3] * lv[3:4], keepdims=True)) + lambda_init)

    def run(step):
        def body(j, carry):
            step(j, False)
            return carry
        lax.fori_loop(0, i, body, 0)
        step(i, True)

    def probs_t(keys, queries, masked, q_mask):
        st = _nt(keys, queries)
        if masked:
            row = lax.broadcasted_iota(I32, st.shape, 0)
            col = lax.broadcasted_iota(I32, st.shape, 1)
            st = jnp.where((row // CHUNK) <= ((col & q_mask) // CHUNK), st, NEG)
        pt = jnp.exp2(st)
        part = pt[0:8, :]
        for g in range(1, st.shape[0] // 8):
            part = part + pt[8 * g:8 * g + 8, :]
        return pt.astype(BF16), part

    def plain_step(j, masked):
        off = pl.multiple_of(j * tq, tq)
        if not masked:
            pt, part = probs_t(k_ref[pl.ds(off, tq), :], qz_ref[...], False, 0)
            lpt_ref[...] += part
            acct_ref[...] += jnp.dot(vt_ref[:, pl.ds(off, tq)], pt, preferred_element_type=F32)
            return
        hq = tq // 2
        pt, part = probs_t(k_ref[pl.ds(off, hq), :], qz_ref[...], True, tq - 1)
        lpt_ref[...] += part
        acct_ref[...] += jnp.dot(vt_ref[:, pl.ds(off, hq)], pt, preferred_element_type=F32)
        off2 = pl.multiple_of(off + hq, hq)
        late_q = jnp.concatenate([qz_ref[hq:tq, :], qz_ref[tq + hq:, :]], axis=0)
        pt, part = probs_t(k_ref[pl.ds(off2, hq), :], late_q, True, hq - 1)
        upd = jnp.dot(vt_ref[:, pl.ds(off2, hq)], pt, preferred_element_type=F32)
        for m in range(2):
            cols = slice(m * tq + hq, (m + 1) * tq)
            lpt_ref[:, cols] += part[:, m * hq:(m + 1) * hq]
            acct_ref[:, cols] += upd[:, m * hq:(m + 1) * hq]

    @pl.when(bounded)
    def _():
        lpt_ref[...] = jnp.zeros_like(lpt_ref)
        acct_ref[...] = jnp.zeros_like(acct_ref)

        def two_steps(p, carry):
            plain_step(2 * p, False)
            plain_step(2 * p + 1, False)
            return carry

        lax.fori_loop(0, lax.shift_right_logical(i, 1), two_steps, 0)

        @pl.when((i & 1) == 1)
        def _():
            plain_step(i - 1, False)
            plain_step(i, True)

        @pl.when((i & 1) == 0)
        def _():
            plain_step(i, True)

        lsum = jnp.sum(lpt_ref[...], axis=0, keepdims=True)
        ot = acct_ref[:, 0:tq] / lsum[:, 0:tq] - lam * (acct_ref[:, tq:] / lsum[:, tq:])
        ot = ot * lax.rsqrt(jnp.mean(ot * ot, axis=0, keepdims=True) + SUBLN_EPS) * sgc_ref[...]
        o_ref[...] = (ot * (1.0 - lambda_init)).T.astype(o_ref.dtype)

    def online_step(j, masked):
        off = pl.multiple_of(j * tq, tq)
        s = _nt(qz_ref[...], k_ref[pl.ds(off, tq), :])
        if masked:
            row = lax.broadcasted_iota(I32, s.shape, 0)
            col = lax.broadcasted_iota(I32, s.shape, 1)
            s = jnp.where((col // CHUNK) <= ((row & (tq - 1)) // CHUNK), s, NEG)
        m_old = m_ref[...]
        m_new = jnp.maximum(m_old, jnp.max(s, axis=-1, keepdims=True))
        alpha = jnp.exp2(m_old - m_new)
        pr = jnp.exp2(s - m_new)
        l_ref[...] = alpha * l_ref[...] + jnp.sum(pr, axis=-1, keepdims=True)
        acc_ref[...] = alpha * acc_ref[...] + jnp.dot(pr.astype(BF16), v_ref[pl.ds(off, tq), :],
                                                      preferred_element_type=F32)
        m_ref[...] = m_new

    @pl.when(jnp.logical_not(bounded))
    def _():
        m_ref[...] = jnp.full_like(m_ref, NEG)
        l_ref[...] = jnp.zeros_like(l_ref)
        acc_ref[...] = jnp.zeros_like(acc_ref)
        run(online_step)
        o = acc_ref[0:tq, :] / l_ref[0:tq, :] - lam * (acc_ref[tq:, :] / l_ref[tq:, :])
        o = o * lax.rsqrt(jnp.mean(o * o, axis=-1, keepdims=True) + SUBLN_EPS) * sg_ref[...]
        o_ref[...] = (o * (1.0 - lambda_init)).astype(o_ref.dtype)


def _diff_attention(qn, kn, v, score_bound, lam_vecs, subln_g, B, S, lambda_init, tq=1024):
    W = qn.shape[1]
    H = W // LANES
    tq = min(tq, S)
    q3 = qn.reshape(B, S, W)
    k3 = kn.reshape(B, S, W)
    v3 = v.reshape(B, S, W)
    vt3 = v3.transpose(0, 2, 1)
    flag = (score_bound <= MAX_PLAIN_SCORE).astype(I32).reshape(1)
    qblk = pl.BlockSpec((None, tq, LANES), lambda b, h, i, f: (b, i, h))
    kvblk = pl.BlockSpec((None, S, LANES), lambda b, h, i, f: (b, 0, h))
    grid_spec = pltpu.PrefetchScalarGridSpec(
        num_scalar_prefetch=1,
        grid=(B, H, S // tq),
        in_specs=[qblk, kvblk, kvblk,
                  pl.BlockSpec((None, LANES, S), lambda b, h, i, f: (b, h, 0)),
                  pl.BlockSpec((4, DA_HEAD_DIM), lambda b, h, i, f: (0, 0)),
                  pl.BlockSpec((1, LANES), lambda b, h, i, f: (0, 0)),
                  pl.BlockSpec((LANES, 1), lambda b, h, i, f: (0, 0))],
        out_specs=qblk,
        scratch_shapes=[pltpu.VMEM((2 * tq, LANES), BF16),
                        pltpu.VMEM((2 * tq, 1), F32),
                        pltpu.VMEM((2 * tq, 1), F32),
                        pltpu.VMEM((2 * tq, LANES), F32),
                        pltpu.VMEM((8, 2 * tq), F32),
                        pltpu.VMEM((LANES, 2 * tq), F32)],
    )
    out = pl.pallas_call(
        functools.partial(_attn_kernel, tq=tq, lambda_init=lambda_init),
        grid_spec=grid_spec,
        out_shape=jax.ShapeDtypeStruct((B, S, W), BF16),
        compiler_params=_cp("parallel", "parallel", "arbitrary"),
        name="diff_attn",
    )(flag, q3, k3, v3, vt3, lam_vecs, subln_g.reshape(1, LANES), subln_g.reshape(LANES, 1))
    return out.reshape(B * S, W)


def _stackmask(m):
    lane = lax.broadcasted_iota(I32, m.shape, 1)
    z = jnp.zeros_like(m)
    return jnp.concatenate([jnp.where(lane < RWKV_HEAD, m, z), jnp.where(lane >= RWKV_HEAD, m, z)], axis=0)


def _pair_sum(x, first):
    s1 = jnp.sum(jnp.where(first, x, 0.0), axis=-1, keepdims=True)
    s2 = jnp.sum(jnp.where(first, 0.0, x), axis=-1, keepdims=True)
    return jnp.where(first, s1, s2)


def _rwkv_scan_kernel(r_ref, k_ref, v_ref, a_ref, ld_ref, g_ref, kk_ref, ka_ref, rk_ref, lnw_ref, lnb_ref,
                      o_ref, s_ref, *, L):
    nb, tb, W = r_ref.shape
    tm = nb * tb
    n_chunks = tm // L
    per_batch = tb // L
    n_pairs = W // LANES
    hd = RWKV_HEAD
    bf = lambda t: t.astype(BF16)
    rows2d = lambda ref: ref[...].reshape(tm, W)

    @pl.when(pl.program_id(0) == 0)
    def _():
        s_ref[...] = jnp.zeros_like(s_ref)

    row = lax.broadcasted_iota(I32, (tm, tm), 0)
    col = lax.broadcasted_iota(I32, (tm, tm), 1)
    tri = jnp.where(jnp.logical_and(col <= row, (col // L) == (row // L)), 1.0, 0.0).astype(BF16)
    ld = rows2d(ld_ref)
    ld_hi = bf(ld)
    rem = ld - ld_hi.astype(F32)
    ld_mid = bf(rem)
    ld_lo = bf(rem - ld_mid.astype(F32))
    c = (jnp.dot(tri, ld_hi, preferred_element_type=F32) + jnp.dot(tri, ld_mid, preferred_element_type=F32)
         + jnp.dot(tri, ld_lo, preferred_element_type=F32))
    ec = jnp.exp(c)
    eci = jnp.exp(-c)
    ecm = jnp.exp(c - ld)
    r = rows2d(r_ref)
    k = rows2d(k_ref)
    v = rows2d(v_ref)
    a = rows2d(a_ref)
    kkr = k * kk_ref[...]
    kmod = k * (1.0 + (a - 1.0) * ka_ref[...])
    brk = r * kmod * rk_ref[...]

    lane = lax.broadcasted_iota(I32, (L, LANES), 1)
    rowl = lax.broadcasted_iota(I32, (L, LANES), 0)
    first = lane < hd
    lane_h = lane & (hd - 1)
    strict = lane_h < rowl
    incl = lane_h <= rowl
    eye = jnp.where(lane_h == rowl, 1.0, 0.0)

    chains = [(ch, p) for ch in range(n_chunks) for p in range(n_pairs)]
    rsl = lambda ch: slice(ch * L, (ch + 1) * L)
    csl = lambda p: slice(p * LANES, (p + 1) * LANES)
    fdot = lambda x, y: jnp.dot(x, y, preferred_element_type=F32)
    at, bt, kt, rt, vh, g_l = {}, {}, {}, {}, {}, {}
    for c_ in chains:
        ch, p = c_
        rs, cs = rsl(ch), csl(p)
        kkh = kkr[rs, cs]
        kkh = kkh / jnp.maximum(jnp.sqrt(_pair_sum(kkh * kkh, first)), 1e-12)
        vh[c_] = v[rs, cs]
        g_l[c_] = ec[ch * L + L - 1:ch * L + L, cs]
        at[c_] = -kkh * ecm[rs, cs]
        bt[c_] = kkh * a[rs, cs] * eci[rs, cs]
        kt[c_] = kmod[rs, cs] * eci[rs, cs]
        rt[c_] = r[rs, cs] * ec[rs, cs]
    gm = {c_: _nt(bf(jnp.concatenate([at[c_], rt[c_]], axis=0)),
                  jnp.concatenate([_stackmask(bf(bt[c_])), _stackmask(bf(kt[c_]))], axis=0)) for c_ in chains}
    a_ab = {c_: jnp.where(strict, gm[c_][:L, :LANES], 0.0) for c_ in chains}
    vsm = {c_: _stackmask(bf(vh[c_])) for c_ in chains}
    akv = {c_: fdot(bf(jnp.concatenate([jnp.where(strict, gm[c_][:L, LANES:], 0.0),
                                        jnp.where(incl, gm[c_][L:, LANES:], 0.0)], axis=0)), vsm[c_])
           for c_ in chains}
    cmat = {c_: akv[c_][:L] for c_ in chains}
    t_inv = {c_: eye + a_ab[c_] for c_ in chains}
    pw = {c_: bf(a_ab[c_]) for c_ in chains}
    for _ in range(int(math.log2(L)) - 1):
        pw = {c_: bf(fdot(pw[c_], _stackmask(pw[c_]))) for c_ in chains}
        t_inv = {c_: t_inv[c_] + fdot(pw[c_], _stackmask(bf(t_inv[c_]))) for c_ in chains}
    zz = {c_: fdot(bf(t_inv[c_]), jnp.concatenate([_stackmask(bf(at[c_])), _stackmask(bf(cmat[c_]))], axis=1))
          for c_ in chains}
    qy = {c_: fdot(bf(jnp.where(incl, gm[c_][L:, :LANES], 0.0)),
                   jnp.concatenate([_stackmask(bf(zz[c_][:, :LANES])), _stackmask(bf(zz[c_][:, LANES:]))], axis=1))
          for c_ in chains}
    y0 = {c_: qy[c_][:, LANES:] + akv[c_][L:] for c_ in chains}
    qa = {c_: bf(jnp.concatenate([rt[c_] + qy[c_][:, :LANES], zz[c_][:, :LANES]], axis=0)) for c_ in chains}
    bkg = {c_: bf(jnp.concatenate([bt[c_] * g_l[c_], kt[c_] * g_l[c_]], axis=0)) for c_ in chains}

    lane_s = lax.broadcasted_iota(I32, (hd, LANES), 1)
    heads = [(b, p) for b in range(nb) for p in range(n_pairs)]
    sp = {bp: s_ref[bp[0] * n_pairs + bp[1]] for bp in heads}
    for j in range(per_batch):
        chunk = lambda b: b * per_batch + j
        yw = {(b, p): _nt(qa[chunk(b), p], _stackmask(bf(sp[b, p]))) for b, p in heads}
        upd = {(b, p): _tn(bf(jnp.concatenate([yw[b, p][L:] + zz[chunk(b), p][:, LANES:], vh[chunk(b), p]],
                                              axis=0)), bkg[chunk(b), p]) for b, p in heads}
        for b, p in heads:
            ch, cs, ts = chunk(b), csl(p), rsl(j)
            sp[b, p] = sp[b, p] * g_l[ch, p] + jnp.where(lane_s < hd, upd[b, p][:hd], upd[b, p][hd:])
            y = yw[b, p][:L] + y0[ch, p]
            mean = _pair_sum(y, first) * (1.0 / hd)
            yc = y - mean
            var = _pair_sum(yc * yc, first) * (1.0 / hd)
            yn = yc * lax.rsqrt(var + GN_EPS) * lnw_ref[:, cs] + lnb_ref[:, cs]
            bonus = _pair_sum(brk[rsl(ch), cs], first) * vh[ch, p]
            o_ref[b, ts, cs] = ((yn + bonus) * g_ref[b, ts, cs]).astype(o_ref.dtype)
    for b, p in heads:
        s_ref[b * n_pairs + p] = sp[b, p]


def _rwkv_scan(r, k, v, a, ld, g, k_k, k_a, r_k, ln_w, ln_b, L=RWKV_CHUNK, tm=128):
    B, S, W = r.shape
    seq = pl.BlockSpec((B, tm, W), lambda c: (0, c, 0))
    vec = pl.BlockSpec((1, W), lambda c: (0, 0))
    return pl.pallas_call(
        functools.partial(_rwkv_scan_kernel, L=L),
        grid=(S // tm,),
        in_specs=[seq] * 6 + [vec] * 5,
        out_specs=seq,
        out_shape=jax.ShapeDtypeStruct((B, S, W), BF16),
        scratch_shapes=[pltpu.VMEM((B * (W // LANES), RWKV_HEAD, LANES), F32)],
        compiler_params=_cp("arbitrary"),
        name="rwkv_scan",
    )(r, k, v, a, ld, g, k_k.reshape(1, W), k_a.reshape(1, W), r_k.reshape(1, W),
      ln_w.reshape(1, W), ln_b.reshape(1, W))


def _post_kernel(attn_ref, rw_ref, gate_ref, x_ref, mod_ref, wa_ref, wb_ref, wo_ref, g2_ref, wrh_ref, wrm_ref,
                 sug_ref, sd_ref, base_ref, h2p_ref, lg_ref):
    D = x_ref.shape[1]
    ya = jnp.dot(attn_ref[...], wa_ref[...], preferred_element_type=F32)
    yb = jnp.dot(rw_ref[...], wb_ref[...], preferred_element_type=F32)
    m = gate_ref[:, 0:D] * ya + gate_ref[:, D:] * yb
    x1 = x_ref[...] + mod_ref[2:3, :] * jnp.dot(m.astype(BF16), wo_ref[...], preferred_element_type=F32)
    y = x1 * lax.rsqrt(jnp.mean(x1 * x1, axis=-1, keepdims=True) + NORM_EPS) * g2_ref[...]
    h2 = y * (1.0 + mod_ref[4:5, :]) + mod_ref[3:4, :]
    hb = h2.astype(BF16)
    hm = (h2 - hb.astype(F32)).astype(BF16)
    lg_ref[...] = _nt(wrh_ref[...], hb) + _nt(wrh_ref[...], hm) + _nt(wrm_ref[...], hb)
    packed = _pack_halves(hb.astype(F32))
    half = packed.shape[1] // 2
    h2p_ref[0] = packed[:, :half]
    h2p_ref[1] = packed[:, half:]
    F = sd_ref.shape[0]
    gu = jnp.dot(hb, sug_ref[...], preferred_element_type=F32)
    shared = jnp.dot((_silu(gu[:, :F]) * gu[:, F:]).astype(BF16), sd_ref[...], preferred_element_type=F32)
    base_ref[...] = x1 + mod_ref[5:6, :] * shared


def _post(attn, rw, gate, x2, mod, wa, wb, wo, norm2_g, w_router_t, sug, sd, S, tm=512):
    N, D = x2.shape
    E = w_router_t.shape[0]
    wr_hi = w_router_t.astype(BF16)
    wr_mid = (w_router_t - wr_hi.astype(F32)).astype(BF16)
    tpb = S // tm
    row = lambda w: pl.BlockSpec((tm, w), lambda i: (i, 0))
    full = lambda a: pl.BlockSpec(a.shape, lambda i: (0, 0))
    return pl.pallas_call(
        _post_kernel,
        grid=(N // tm,),
        in_specs=[row(attn.shape[1]), row(rw.shape[1]), row(gate.shape[1]), row(D),
                  pl.BlockSpec((None, 6, D), lambda i: (i // tpb, 0, 0)),
                  full(wa), full(wb), full(wo), pl.BlockSpec((1, D), lambda i: (0, 0)), full(wr_hi), full(wr_mid),
                  full(sug), full(sd)],
        out_specs=[row(D), pl.BlockSpec((2, tm, D // 4), lambda i: (0, i, 0)), pl.BlockSpec((E, tm), lambda i: (0, i))],
        out_shape=[jax.ShapeDtypeStruct((N, D), F32), jax.ShapeDtypeStruct((2, N, D // 4), U32),
                   jax.ShapeDtypeStruct((E, N), F32)],
        compiler_params=_cp("parallel"),
        name="post_mixer",
    )(attn, rw, gate, x2, mod, wa, wb, wo, norm2_g.reshape(1, D), wr_hi, wr_mid, sug, sd)


def _first_argmax(x, idx, sentinel):
    m = jnp.max(x, axis=0, keepdims=True)
    return m, jnp.min(jnp.where(x == m, idx, sentinel), axis=0, keepdims=True)


def _route_kernel(lg_ref, bias_ref, e_ref, w_ref):
    E, T = lg_ref.shape
    gsz = E // N_GROUPS
    scores = _sigmoid(lg_ref[...])
    biased = scores + bias_ref[...]
    ig = lax.broadcasted_iota(I32, (gsz, T), 0)
    gs = []
    for g in range(N_GROUPS):
        blk = biased[g * gsz:(g + 1) * gsz, :]
        m1, i1 = _first_argmax(blk, ig, gsz)
        m2 = jnp.max(jnp.where(ig == i1, -jnp.inf, blk), axis=0, keepdims=True)
        gs.append(m1 + m2)
    gsc = jnp.concatenate(gs, axis=0)
    i8 = lax.broadcasted_iota(I32, (N_GROUPS, T), 0)
    chosen = jnp.zeros((N_GROUPS, T), F32)
    for _ in range(TOPK_GROUPS):
        _, gi = _first_argmax(gsc, i8, N_GROUPS)
        hit = i8 == gi
        chosen = jnp.where(hit, 1.0, chosen)
        gsc = jnp.where(hit, -jnp.inf, gsc)
    masked = jnp.concatenate(
        [jnp.where(chosen[g:g + 1, :] > 0.0, biased[g * gsz:(g + 1) * gsz, :], -jnp.inf)
         for g in range(N_GROUPS)], axis=0)
    ie = lax.broadcasted_iota(I32, (E, T), 0)
    idxs, wts = [], []
    for _ in range(TOP_K):
        _, ei = _first_argmax(masked, ie, E)
        hit = ie == ei
        idxs.append(ei)
        wts.append(jnp.sum(jnp.where(hit, scores, 0.0), axis=0, keepdims=True))
        masked = jnp.where(hit, -jnp.inf, masked)
    wt = jnp.concatenate(wts, axis=0)
    e_ref[...] = jnp.concatenate(idxs, axis=0)
    w_ref[...] = wt / jnp.sum(wt, axis=0, keepdims=True) * ROUTED_SCALE


def _route(logits_t, router_bias, T=512):
    E, N = logits_t.shape
    blk = pl.BlockSpec((TOP_K, T), lambda i: (0, i))
    return pl.pallas_call(
        _route_kernel,
        grid=(N // T,),
        in_specs=[pl.BlockSpec((E, T), lambda i: (0, i)), pl.BlockSpec((E, 1), lambda i: (0, 0))],
        out_specs=[blk, blk],
        out_shape=[jax.ShapeDtypeStruct((TOP_K, N), I32), jax.ShapeDtypeStruct((TOP_K, N), F32)],
        compiler_params=_cp("parallel"),
        name="route",
    )(logits_t, router_bias.reshape(E, 1))


def _rank_kernel(e_ref, rank_ref, cnt_ref, carry_ref, *, n_experts):
    T = e_ref.shape[1]

    @pl.when(pl.program_id(0) == 0)
    def _():
        carry_ref[...] = jnp.zeros_like(carry_ref)

    ie = lax.broadcasted_iota(I32, (n_experts, T), 0)
    e = e_ref[...]
    hits = [ie == e[kk:kk + 1, :] for kk in range(TOP_K)]
    onehot = jnp.zeros((n_experts, T), F32)
    for hsel in hits:
        onehot = onehot + hsel.astype(F32)
    tr = lax.broadcasted_iota(I32, (T, T), 0)
    tc = lax.broadcasted_iota(I32, (T, T), 1)
    before = (tr < tc).astype(BF16)
    base = _bdot(onehot, before) + carry_ref[:, 0:1]
    rank_ref[...] = jnp.concatenate(
        [jnp.sum(jnp.where(hsel, base, 0.0), axis=0, keepdims=True) for hsel in hits], axis=0).astype(I32)
    carry_ref[...] = carry_ref[...] + jnp.sum(onehot, axis=1, keepdims=True)
    cnt_ref[...] = carry_ref[...]


def _ranks(eidx_t, n_experts, T=512):
    N = eidx_t.shape[1]
    T = min(T, N)
    blk = pl.BlockSpec((TOP_K, T), lambda i: (0, i))
    cnt = pl.BlockSpec((n_experts, LANES), lambda i: (0, 0))
    return pl.pallas_call(
        functools.partial(_rank_kernel, n_experts=n_experts),
        grid=(N // T,),
        in_specs=[blk],
        out_specs=[blk, cnt],
        out_shape=[jax.ShapeDtypeStruct((TOP_K, N), I32), jax.ShapeDtypeStruct((n_experts, LANES), F32)],
        scratch_shapes=[pltpu.VMEM((n_experts, LANES), F32)],
        compiler_params=_cp("arbitrary"),
        name="ranks",
    )(eidx_t)


def _dest_kernel(e_ref, rank_ref, start_ref, d_ref):
    E = start_ref.shape[0]
    T = e_ref.shape[1]
    ie = lax.broadcasted_iota(I32, (E, T), 0)
    e = e_ref[...]
    start = start_ref[:, 0:1]
    rows = [jnp.sum(jnp.where(ie == e[kk:kk + 1, :], start, 0.0), axis=0, keepdims=True) for kk in range(TOP_K)]
    d_ref[...] = jnp.concatenate(rows, axis=0).astype(I32) + rank_ref[...]


def _dests(eidx_t, rank_t, pstart, T=2048):
    N = eidx_t.shape[1]
    T = min(T, N)
    E = pstart.shape[0]
    blk = pl.BlockSpec((TOP_K, T), lambda i: (0, i))
    return pl.pallas_call(
        _dest_kernel,
        grid=(N // T,),
        in_specs=[blk, blk, pl.BlockSpec((E, LANES), lambda i: (0, 0))],
        out_specs=blk,
        out_shape=jax.ShapeDtypeStruct((TOP_K, N), I32),
        compiler_params=_cp("parallel"),
        name="dests",
    )(eidx_t, rank_t, jnp.broadcast_to(pstart.astype(F32)[:, None], (E, LANES)))


def _expert_kernel(us_ref, ps_ref, x_hbm, wug_hbm, wd_hbm, y_hbm,
                   wug_f32, wd_f32, wug_bf, wd_bf, xbuf, ybuf, cnt_ref, wsem, xsem, ysem, *, n_rows):
    e = pl.program_id(0)
    n_experts = pl.num_programs(0)
    _, _, R, Ch = xbuf.shape
    C = 2 * Ch
    F = wd_hbm.shape[1]

    def w_copies(expert, slot):
        return (pltpu.make_async_copy(wug_hbm.at[expert], wug_f32.at[slot], wsem.at[slot]),
                pltpu.make_async_copy(wd_hbm.at[expert], wd_f32.at[slot], wsem.at[slot]))

    def x_copy(slot, half, row):
        return pltpu.make_async_copy(x_hbm.at[half, pl.ds(row, R)], xbuf.at[slot, half], xsem.at[slot])

    def y_copy(slot, half, row):
        return pltpu.make_async_copy(ybuf.at[slot, half], y_hbm.at[half, pl.ds(row, R)], ysem.at[slot])

    def start(copy, slot, row):
        for half in range(2):
            copy(slot, half, row).start()

    def wait(copy, slot):
        for half in range(2):
            copy(slot, half, 0).wait()

    @pl.when(e == 0)
    def _():
        for ahead in range(W_SLOTS - 1):
            for cp in w_copies(ahead, ahead):
                cp.start()
        cnt_ref[0] = 0
        for t in range(X_AHEAD):
            start(x_copy, t, t * R)
        ybuf[...] = jnp.zeros_like(ybuf)
        for slot in range(Y_SLOTS):
            start(y_copy, slot, n_rows + slot * R)

    @pl.when(e + (W_SLOTS - 1) < n_experts)
    def _():
        for cp in w_copies(e + (W_SLOTS - 1), lax.rem(e + (W_SLOTS - 1), W_SLOTS)):
            cp.start()

    wslot = lax.rem(e, W_SLOTS)
    for cp in w_copies(e, wslot):
        cp.wait()
    wug_bf[...] = wug_f32[wslot].astype(BF16)
    wd_bf[...] = wd_f32[wslot].astype(BF16)
    n_valid = us_ref[e + 1] - us_ref[e]
    pbase = ps_ref[e]
    n_tiles = lax.shift_right_logical(n_valid + (R - 1), int(math.log2(R)))

    def make_tile(nb):
        rowid = lax.broadcasted_iota(I32, (nb * R, C), 0)

        def tile(blk0):
            n_done = cnt_ref[0]
            xslots = [lax.rem(n_done + b, X_SLOTS) for b in range(nb)]
            for b in range(nb):
                ahead = n_done + X_AHEAD + b
                start(x_copy, lax.rem(ahead, X_SLOTS), pl.multiple_of(ahead * R, R))
            for b in range(nb):
                wait(x_copy, xslots[b])
            x = jnp.concatenate([jnp.concatenate([xbuf[xs, 0], xbuf[xs, 1]], axis=1) for xs in xslots], axis=0)
            x = jnp.where(rowid < n_valid - blk0 * R, x, jnp.uint32(0))
            lo, hi = _unpack_halves(x)
            gu = (jnp.dot(lo.astype(BF16), wug_bf[0:C, :], preferred_element_type=F32)
                  + jnp.dot(hi.astype(BF16), wug_bf[C:, :], preferred_element_type=F32))
            hid = (_silu(gu[:, :F]) * gu[:, F:]).astype(BF16)
            y = jnp.dot(hid, wd_bf[...], preferred_element_type=F32)
            packed = _pack_halves(y.astype(BF16).astype(F32))
            for b in range(nb):
                slot = lax.rem(n_done + b, Y_SLOTS)
                wait(y_copy, slot)
                ybuf[slot, 0] = packed[b * R:(b + 1) * R, :Ch]
                ybuf[slot, 1] = packed[b * R:(b + 1) * R, Ch:]
                start(y_copy, slot, pl.multiple_of(pbase + (blk0 + b) * R, R))
            cnt_ref[0] = n_done + nb

        return tile

    tiles = {nb: make_tile(nb) for nb in range(1, TILE_BLOCKS + 1)}
    quads = lax.shift_right_logical(n_tiles, 2)
    left = n_tiles & 3
    five = jnp.logical_and(left == 1, quads >= 1)
    quads = quads - jnp.where(five, 1, 0)

    def quad_tiles(j, carry):
        tiles[4](4 * j)
        return carry

    lax.fori_loop(0, quads, quad_tiles, 0)

    @pl.when(five)
    def _():
        tiles[5](n_tiles - 5)

    for nb in (1, 2, 3):
        @pl.when(jnp.logical_and(left == nb, jnp.logical_not(five)))
        def _(nb=nb):
            tiles[nb](n_tiles - nb)

    @pl.when(e == pl.num_programs(0) - 1)
    def _():
        for t in range(X_AHEAD):
            wait(x_copy, lax.rem(cnt_ref[0] + t, X_SLOTS))
        for slot in range(Y_SLOTS):
            wait(y_copy, slot)
        ybuf[0] = jnp.zeros((2, R, Ch), U32)
        first = lax.shift_right_logical(pbase + n_tiles * R, int(math.log2(R)))
        n_left = n_rows // R - first

        def fill(t, carry):
            start(y_copy, 0, pl.multiple_of((first + t) * R, R))
            return carry

        def drain(t, carry):
            wait(y_copy, 0)
            return carry

        lax.fori_loop(0, n_left, fill, 0)
        lax.fori_loop(0, n_left, drain, 0)


def _experts(ustart, pstart, xg, w_ug, w_d, n_rows, R):
    _, _, Ch = xg.shape
    E, D, F2 = w_ug.shape
    F = w_d.shape[1]
    grid_spec = pltpu.PrefetchScalarGridSpec(
        num_scalar_prefetch=2,
        grid=(E,),
        in_specs=[pl.BlockSpec(memory_space=pl.ANY)] * 3,
        out_specs=pl.BlockSpec(memory_space=pl.ANY),
        scratch_shapes=[pltpu.VMEM((W_SLOTS, D, F2), F32), pltpu.VMEM((W_SLOTS, F, D), F32),
                        pltpu.VMEM((D, F2), BF16), pltpu.VMEM((F, D), BF16),
                        pltpu.VMEM((X_SLOTS, 2, R, Ch), U32), pltpu.VMEM((Y_SLOTS, 2, R, Ch), U32), pltpu.SMEM((1,), I32),
                        pltpu.SemaphoreType.DMA((W_SLOTS,)), pltpu.SemaphoreType.DMA((X_SLOTS,)),
                        pltpu.SemaphoreType.DMA((Y_SLOTS,))],
    )
    return pl.pallas_call(
        functools.partial(_expert_kernel, n_rows=n_rows),
        grid_spec=grid_spec,
        out_shape=jax.ShapeDtypeStruct((2, n_rows + Y_SLOTS * R, Ch), U32),
        compiler_params=_cp("arbitrary"),
        name="experts",
    )(ustart, pstart, xg, w_ug, w_d)


def _sc_scatter_rows(src, idx, n_rows):
    H, N, C = src.shape
    K = idx.shape[0]
    per_row = N // SC_WINDOW
    mesh = plsc.VectorSubcoreMesh(core_axis_name="c", subcore_axis_name="s")

    @functools.partial(pl.kernel, out_type=jax.ShapeDtypeStruct((H, n_rows, C), src.dtype), mesh=mesh,
                       scratch_types=[])
    def scatter_kernel(x_hbm, i_hbm, o_hbm):
        for h in range(H):
            def body(x_vmem, i_vmem):
                for k in range(K):
                    pltpu.sync_copy(x_vmem, o_hbm.at[h].at[i_vmem.at[k]])

            pltpu.emit_pipeline(
                body,
                grid=(per_row,),
                in_specs=[pl.BlockSpec((SC_WINDOW, C), lambda i: (i, 0)),
                          pl.BlockSpec((K, SC_WINDOW), lambda i: (0, i))],
                out_specs=[],
                core_axis_name=("c", "s"),
                dimension_semantics=(pltpu.PARALLEL,),
            )(x_hbm.at[h], i_hbm)

    return scatter_kernel(src, idx)


def _sc_gather_rows(src, idx):
    H, _, C = src.shape
    K, N = idx.shape
    per_row = N // SC_WINDOW
    mesh = plsc.VectorSubcoreMesh(core_axis_name="c", subcore_axis_name="s")

    @functools.partial(pl.kernel, out_type=jax.ShapeDtypeStruct((H, K * N, C), src.dtype), mesh=mesh,
                       scratch_types=[])
    def gather_kernel(x_hbm, i_hbm, o_hbm):
        for h in range(H):
            def body(i_vmem, o_vmem):
                pltpu.sync_copy(x_hbm.at[h].at[i_vmem.at[0]], o_vmem)

            pltpu.emit_pipeline(
                body,
                grid=(K * per_row,),
                in_specs=[pl.BlockSpec((1, SC_WINDOW), lambda i: (i // per_row, i % per_row))],
                out_specs=[pl.BlockSpec((SC_WINDOW, C), lambda i: (i, 0))],
                core_axis_name=("c", "s"),
                dimension_semantics=(pltpu.PARALLEL,),
            )(i_hbm, o_hbm.at[h])

    return gather_kernel(src, idx)


def _combine_kernel(*refs):
    y_refs = refs[:2 * TOP_K]
    w_ref, base_ref, mod_ref, o_ref = refs[2 * TOP_K:]
    T = base_ref.shape[0]
    tr = lax.broadcasted_iota(I32, (T, T), 0)
    tc = lax.broadcasted_iota(I32, (T, T), 1)
    wcol = _nt((tr == tc).astype(F32), w_ref[...], precision=HI)
    acc = [None] * 4
    for kk in range(TOP_K):
        wk = wcol[:, kk:kk + 1]
        for half in range(2):
            lo, hi = _unpack_halves(y_refs[2 * kk + half][...])
            for q, val in ((half, lo), (2 + half, hi)):
                acc[q] = val * wk if acc[q] is None else acc[q] + val * wk
    o_ref[...] = base_ref[...] + mod_ref[5:6, :] * jnp.concatenate(acc, axis=1)


def _combine(yg, w_t, base, mod, S, T=512):
    N, D = base.shape
    C = yg.shape[2]
    tpb = S // T
    n_tiles = N // T
    row = pl.BlockSpec((T, D), lambda i: (i, 0))
    piece = lambda kk, half: pl.BlockSpec((None, T, C), lambda i: (half, kk * n_tiles + i, 0))
    return pl.pallas_call(
        _combine_kernel,
        grid=(n_tiles,),
        in_specs=[piece(kk, half) for kk in range(TOP_K) for half in range(2)] + [
            pl.BlockSpec((TOP_K, T), lambda i: (0, i)),
            row,
            pl.BlockSpec((None, 6, D), lambda i: (i // tpb, 0, 0))],
        out_specs=row,
        out_shape=jax.ShapeDtypeStruct((N, D), F32),
        compiler_params=_cp("parallel"),
        name="combine",
    )(*([yg] * (2 * TOP_K)), w_t, base, mod)


def _layer(x, c, positions, layer_idx, w_ada, b_ada, norm1_g, w_in, w_gate, b_gate,
           q_norm_g, k_norm_g, lambda_q1, lambda_k1, lambda_q2, lambda_k2, subln_g,
           rwkv_mu, w_decay0, w_decay2, a0, a2, g2, k_k, k_a, r_k, ln_x_w, ln_x_b,
           w_branch_a, w_branch_b, w_out, norm2_g, w_router, router_bias,
           w_expert_up_gate, w_expert_down, w_shared_up_gate, w_shared_down):
    B, S, D = x.shape
    N = B * S
    E = w_router.shape[1]
    da_width = w_branch_a.shape[0]
    rw_width = w_branch_b.shape[0]
    lambda_init = 0.8 - 0.6 * math.exp(-0.3 * layer_idx)

    mod = _adaln(c, w_ada, b_ada)
    x2 = x.reshape(N, D)
    w_cat = jnp.concatenate([w_in, w_gate], axis=1).astype(BF16)
    qn, kn, v, r_, k_, v_, a_, ld_, g_, gate = _mixer_in(
        x2, positions.reshape(N, 1), mod, norm1_g, w_cat, b_gate, q_norm_g, k_norm_g,
        rwkv_mu, w_decay0, w_decay2, a0, a2, g2, S, da_width, rw_width)

    lam_vecs = jnp.stack([lambda_q1, lambda_k1, lambda_q2, lambda_k2])
    score_bound = 1.01 * DA_HEAD_DIM ** 0.5 * jnp.max(jnp.abs(q_norm_g)) * jnp.max(jnp.abs(k_norm_g))
    attn = _diff_attention(qn, kn, v, score_bound, lam_vecs, subln_g, B, S, lambda_init)

    seq = lambda t: t.reshape(B, S, rw_width)
    rw = _rwkv_scan(seq(r_), seq(k_), seq(v_), seq(a_), seq(ld_), seq(g_), k_k, k_a, r_k.reshape(-1),
                    ln_x_w, ln_x_b).reshape(N, rw_width)

    base, h2p, logits_t = _post(attn, rw, gate, x2, mod, w_branch_a.astype(BF16), w_branch_b.astype(BF16),
                                w_out.astype(BF16), norm2_g, w_router.T,
                                w_shared_up_gate.astype(BF16), w_shared_down.astype(BF16), S)

    eidx_t, w_t = _route(logits_t, router_bias)
    rank_t, counts = _ranks(eidx_t, E)
    R = EXPERT_TILE
    cnt = counts[:, 0].astype(I32)
    ustart = jnp.concatenate([jnp.zeros((1,), I32), jnp.cumsum(cnt)])
    pcnt = (cnt + R - 1) // R * R
    pstart = jnp.cumsum(pcnt) - pcnt
    dest_p = _dests(eidx_t, rank_t, pstart)
    n_rows = (N * TOP_K + E * (R - 1) + R - 1) // R * R
    xg = _sc_scatter_rows(h2p, dest_p, n_rows + X_AHEAD * R)
    y = _experts(ustart, pstart, xg, w_expert_up_gate, w_expert_down, n_rows, R)
    yg = _sc_gather_rows(y, dest_p)
    out = _combine(yg, w_t, base, mod, S)
    return out.reshape(B, S, D)


def kernel(x, c, positions, w_ada, b_ada, norm1_g, w_in, w_gate, b_gate, q_norm_g, k_norm_g, lambda_q1, lambda_k1, lambda_q2, lambda_k2, subln_g, rwkv_mu, w_decay0, w_decay2, a0, a2, g2, k_k, k_a, r_k, ln_x_w, ln_x_b, w_branch_a, w_branch_b, w_out, norm2_g, w_router, router_bias, w_expert_up_gate, w_expert_down, w_shared_up_gate, w_shared_down):
    for l in range(w_ada.shape[0]):
        x = _layer(x, c, positions, l, w_ada[l], b_ada[l], norm1_g[l], w_in[l], w_gate[l], b_gate[l],
                   q_norm_g[l], k_norm_g[l], lambda_q1[l], lambda_k1[l], lambda_q2[l], lambda_k2[l],
                   subln_g[l], rwkv_mu[l], w_decay0[l], w_decay2[l], a0[l], a2[l], g2[l], k_k[l],
                   k_a[l], r_k[l], ln_x_w[l], ln_x_b[l], w_branch_a[l], w_branch_b[l], w_out[l],
                   norm2_g[l], w_router[l], router_bias[l], w_expert_up_gate[l], w_expert_down[l],
                   w_shared_up_gate[l], w_shared_down[l])
    return x
```

```python
import functools
import math

import jax
import jax.numpy as jnp
from jax import lax
from jax.experimental import pallas as pl
from jax.experimental.pallas import tpu as pltpu
from jax.experimental.pallas import tpu_sc as plsc

F32 = jnp.float32
BF16 = jnp.bfloat16
I32 = jnp.int32
U32 = jnp.uint32
HI = lax.Precision.HIGHEST

CHUNK = 64
ROPE_THETA = 10000.0
NORM_EPS = 1e-6
SUBLN_EPS = 1e-5
DA_HEAD_DIM = 64
RWKV_HEAD = 64
GN_EPS = 64e-5
TOP_K = 8
N_GROUPS = 8
TOPK_GROUPS = 4
ROUTED_SCALE = 2.5
EXPERT_TILE = 128
W_SLOTS = 3
TILE_BLOCKS = 5
X_SLOTS = 13
X_AHEAD = X_SLOTS - TILE_BLOCKS
Y_SLOTS = 8
RWKV_CHUNK = 64
LANES = 128
SC_WINDOW = 128
NEG = -1e30
MAX_PLAIN_SCORE = 40.0
VMEM_LIMIT = 56 * 1024 * 1024


def _cp(*sem):
    return pltpu.CompilerParams(dimension_semantics=sem, vmem_limit_bytes=VMEM_LIMIT)


def _bdot(a, b):
    return jnp.dot(a.astype(BF16), b.astype(BF16), preferred_element_type=F32)


def _fdot(a, b):
    return jnp.dot(a, b, precision=HI, preferred_element_type=F32)


def _nt(a, b, precision=None):
    return lax.dot_general(a, b, (((1,), (1,)), ((), ())), precision=precision,
                           preferred_element_type=F32)


def _tn(a, b, precision=None):
    return lax.dot_general(a, b, (((0,), (0,)), ((), ())), precision=precision,
                           preferred_element_type=F32)


def _pack_halves(x):
    c = x.shape[1] // 2
    lo = lax.bitcast_convert_type(x[:, :c], U32)
    hi = lax.bitcast_convert_type(x[:, c:], U32)
    return (hi & jnp.uint32(0xFFFF0000)) | (lo >> 16)


def _unpack_halves(w):
    lo = lax.bitcast_convert_type(w << 16, F32)
    hi = lax.bitcast_convert_type(w & jnp.uint32(0xFFFF0000), F32)
    return lo, hi


def _sigmoid(x):
    return 1.0 / (1.0 + jnp.exp(-x))


def _silu(x):
    return x * _sigmoid(x)


def _ada_kernel(c_ref, w_ref, b_ref, o_ref):
    o_ref[...] = _fdot(_silu(c_ref[...]), w_ref[...]) + b_ref[...]


def _adaln(c, w_ada, b_ada):
    B, D = c.shape
    rows = -(-B // 8) * 8
    cpad = jnp.zeros((rows, D), F32).at[:B].set(c)
    n_out = w_ada.shape[1]
    out = pl.pallas_call(
        _ada_kernel,
        grid=(n_out // D,),
        in_specs=[pl.BlockSpec((rows, D), lambda j: (0, 0)),
                  pl.BlockSpec((D, D), lambda j: (0, j)),
                  pl.BlockSpec((1, D), lambda j: (0, j))],
        out_specs=pl.BlockSpec((rows, D), lambda j: (0, j)),
        out_shape=jax.ShapeDtypeStruct((rows, n_out), F32),
        compiler_params=_cp("arbitrary"),
        name="adaln",
    )(cpad, w_ada, b_ada.reshape(1, n_out))
    return out[:B].reshape(B, n_out // D, D)


def _mixer_in_kernel(x_ref, xprev_ref, mod_ref, g_ref, w_ref, bg_ref, pos_ref, invf_ref, qg_ref, kg_ref,
                     mu_ref, w0_ref, w2_ref, a0_ref, a2_ref, g2_ref,
                     qn_ref, kn_ref, v_ref, r_ref, k_ref, vr_ref, a_ref, ld_ref, gr_ref, gate_ref,
                     *, da_width, rw_cols, rw_width, q_scale, tiles_per_seq):
    tm = x_ref.shape[0]

    def modulated(x):
        y = x * lax.rsqrt(jnp.mean(x * x, axis=-1, keepdims=True) + NORM_EPS) * g_ref[...]
        return (y * (1.0 + mod_ref[1:2, :]) + mod_ref[0:1, :]).astype(BF16)

    def proj(hb, c0, width, step=512):
        parts = [jnp.dot(hb, w_ref[:, c0 + o:c0 + min(o + step, width)], preferred_element_type=F32)
                 for o in range(0, width, step)]
        return parts[0] if len(parts) == 1 else jnp.concatenate(parts, axis=1)

    h_ext = modulated(jnp.concatenate([x_ref[...], xprev_ref[...]], axis=0))
    h = h_ext[:tm]

    lane = lax.broadcasted_iota(I32, (tm, LANES), 1)
    first = lane < DA_HEAD_DIM
    lo_half = (lane & (DA_HEAD_DIM - 1)) < DA_HEAD_DIM // 2
    ang = pos_ref[...].astype(F32) * invf_ref[...]
    cos = jnp.cos(ang)
    sin = jnp.sin(ang)
    sin = jnp.where(lo_half, -sin, sin)
    for c0, dst, gn_ref, mult in ((0, qn_ref, qg_ref, q_scale), (da_width, kn_ref, kg_ref, 1.0)):
        raw = proj(h, c0, da_width)
        for blk in range(da_width // LANES):
            x = raw[:, blk * LANES:(blk + 1) * LANES]
            xx = x * x
            s_first = jnp.sum(jnp.where(first, xx, 0.0), axis=-1, keepdims=True)
            s_second = jnp.sum(jnp.where(first, 0.0, xx), axis=-1, keepdims=True)
            ms = jnp.where(first, s_first, s_second) * (1.0 / DA_HEAD_DIM)
            xn = x * lax.rsqrt(ms + NORM_EPS) * gn_ref[...]
            rot = jnp.where(lo_half, pltpu.roll(xn, LANES - DA_HEAD_DIM // 2, axis=1),
                            pltpu.roll(xn, DA_HEAD_DIM // 2, axis=1))
            dst[:, blk * LANES:(blk + 1) * LANES] = ((xn * cos + rot * sin) * mult).astype(dst.dtype)
    v_ref[...] = proj(h, 2 * da_width, da_width).astype(v_ref.dtype)

    c_rw = 3 * da_width
    seq_start = (pl.program_id(0) % tiles_per_seq) == 0
    width = rw_width

    def shifted(o, ncols):
        p_ext = jnp.dot(h_ext, w_ref[:, c_rw + o:c_rw + o + ncols], preferred_element_type=F32)
        p = p_ext[:tm]
        last_prev = jnp.where(seq_start, 0.0, p_ext[tm + 7:tm + 8, :])
        rowi = lax.broadcasted_iota(I32, p.shape, 0)
        prev = jnp.where(rowi == 0, last_prev, pltpu.roll(p, 1, axis=0))
        return p + (prev - p) * mu_ref[:, o:o + ncols]

    r_ref[...] = shifted(0, width)
    k_ref[...] = shifted(width, width)
    vr_ref[...] = shifted(2 * width, width)
    xs = shifted(3 * width, rw_cols - 3 * width)
    xwa = xs[:, 0:LANES]
    xg = xs[:, LANES:]
    z = w0_ref[...] + _bdot(jnp.tanh(xwa), w2_ref[...])
    w = -(jnp.maximum(-z, 0.0) + jnp.log(1.0 + jnp.exp(-jnp.abs(z)))) - 0.5
    ld_ref[...] = -jnp.exp(w)
    a_ref[...] = _sigmoid(a0_ref[...] + _bdot(xwa, a2_ref[...]))
    gr_ref[...] = _bdot(_sigmoid(xg), g2_ref[...])

    c_gate = c_rw + rw_cols
    for o in range(0, gate_ref.shape[1], 512):
        e = min(o + 512, gate_ref.shape[1])
        gate_ref[:, o:e] = _sigmoid(jnp.dot(h, w_ref[:, c_gate + o:c_gate + e], preferred_element_type=F32)
                                    + bg_ref[:, o:e]).astype(gate_ref.dtype)


def _mixer_in(x2, pos2, mod, norm1_g, w_cat, b_gate, q_norm_g, k_norm_g, mu, w_decay0, w_decay2, a0, a2, g2,
              S, da_width, rw_width, tm=512):
    N, D = x2.shape
    n_gate = b_gate.shape[0]
    rw_cols = mu.shape[0]
    tpb = S // tm
    d = DA_HEAD_DIM
    inv_freq = 1.0 / (ROPE_THETA ** (jnp.arange(0, d, 2, dtype=F32) / d))
    invf = jnp.tile(inv_freq, LANES // (d // 2)).reshape(1, LANES)
    dl, al = w_decay2.shape[0], a2.shape[0]
    assert dl + al == LANES and g2.shape[0] == LANES
    w2p = jnp.zeros((LANES, rw_width), F32).at[:dl].set(w_decay2)
    a2p = jnp.zeros((LANES, rw_width), F32).at[dl:].set(a2)
    kern = functools.partial(_mixer_in_kernel, da_width=da_width, rw_cols=rw_cols, rw_width=rw_width,
                             q_scale=d ** -0.5 * math.log2(math.e),
                             tiles_per_seq=tpb)
    row = lambda w: pl.BlockSpec((tm, w), lambda i: (i, 0))
    vec = lambda n: pl.BlockSpec((1, n), lambda i: (0, 0))
    mat = pl.BlockSpec((LANES, rw_width), lambda i: (0, 0))
    f32 = lambda w: jax.ShapeDtypeStruct((N, w), F32)
    bf16 = lambda w: jax.ShapeDtypeStruct((N, w), BF16)
    return pl.pallas_call(
        kern,
        grid=(N // tm,),
        in_specs=[row(D),
                  pl.BlockSpec((8, D), lambda i: (jnp.maximum(i * (tm // 8) - 1, 0), 0)),
                  pl.BlockSpec((None, 6, D), lambda i: (i // tpb, 0, 0)),
                  vec(D),
                  pl.BlockSpec(w_cat.shape, lambda i: (0, 0)),
                  vec(n_gate),
                  pl.BlockSpec((tm, 1), lambda i: (i, 0)),
                  vec(LANES), vec(LANES), vec(LANES),
                  vec(rw_cols), vec(rw_width), mat, vec(rw_width), mat, mat],
        out_specs=[row(da_width)] * 3 + [row(rw_width)] * 6 + [row(n_gate)],
        out_shape=[bf16(da_width)] * 3 + [f32(rw_width)] * 6 + [bf16(n_gate)],
        compiler_params=_cp("parallel"),
        name="mixer_in",
    )(x2, x2, mod, norm1_g.reshape(1, D), w_cat, b_gate.reshape(1, n_gate), pos2, invf,
      jnp.tile(q_norm_g, 2).reshape(1, LANES), jnp.tile(k_norm_g, 2).reshape(1, LANES),
      mu.reshape(1, rw_cols), w_decay0.reshape(1, rw_width), w2p, a0.reshape(1, rw_width), a2p, g2)


def _attn_kernel(flag_ref, q_ref, k_ref, v_ref, vt_ref, lam_ref, sg_ref, sgc_ref, o_ref,
                 qz_ref, m_ref, l_ref, acc_ref, lpt_ref, acct_ref, *, tq, lambda_init):
    i = pl.program_id(2)
    lane = lax.broadcasted_iota(I32, (tq, LANES), 1)
    q = q_ref[...]
    zero = jnp.zeros_like(q)
    qz_ref[0:tq, :] = jnp.where(lane < DA_HEAD_DIM, q, zero)
    qz_ref[tq:, :] = jnp.where(lane >= DA_HEAD_DIM, q, zero)
    bounded = flag_ref[0] == 1
    lv = lam_ref[...]
    lam = (jnp.exp(jnp.sum(lv[0:1] * lv[1:2], keepdims=True))
           - jnp.exp(jnp.sum(lv[2:3] * lv[3:4], keepdims=True)) + lambda_init)

    def run(step):
        def body(j, carry):
            step(j, False)
            return carry
        lax.fori_loop(0, i, body, 0)
        step(i, True)

    def probs_t(keys, queries, masked, q_mask):
        st = _nt(keys, queries)
        if masked:
            row = lax.broadcasted_iota(I32, st.shape, 0)
            col = lax.broadcasted_iota(I32, st.shape, 1)
            st = jnp.where((row // CHUNK) <= ((col & q_mask) // CHUNK), st, NEG)
        pt = jnp.exp2(st)
        part = pt[0:8, :]
        for g in range(1, st.shape[0] // 8):
            part = part + pt[8 * g:8 * g + 8, :]
        return pt.astype(BF16), part

    def plain_step(j, masked):
        off = pl.multiple_of(j * tq, tq)
        if not masked:
            pt, part = probs_t(k_ref[pl.ds(off, tq), :], qz_ref[...], False, 0)
            lpt_ref[...] += part
            acct_ref[...] += jnp.dot(vt_ref[:, pl.ds(off, tq)], pt, preferred_element_type=F32)
            return
        hq = tq // 2
        pt, part = probs_t(k_ref[pl.ds(off, hq), :], qz_ref[...], True, tq - 1)
        lpt_ref[...] += part
        acct_ref[...] += jnp.dot(vt_ref[:, pl.ds(off, hq)], pt, preferred_element_type=F32)
        off2 = pl.multiple_of(off + hq, hq)
        late_q = jnp.concatenate([qz_ref[hq:tq, :], qz_ref[tq + hq:, :]], axis=0)
        pt, part = probs_t(k_ref[pl.ds(off2, hq), :], late_q, True, hq - 1)
        upd = jnp.dot(vt_ref[:, pl.ds(off2, hq)], pt, preferred_element_type=F32)
        for m in range(2):
            cols = slice(m * tq + hq, (m + 1) * tq)
            lpt_ref[:, cols] += part[:, m * hq:(m + 1) * hq]
            acct_ref[:, cols] += upd[:, m * hq:(m + 1) * hq]

    @pl.when(bounded)
    def _():
        lpt_ref[...] = jnp.zeros_like(lpt_ref)
        acct_ref[...] = jnp.zeros_like(acct_ref)

        def two_steps(p, carry):
            plain_step(2 * p, False)
            plain_step(2 * p + 1, False)
            return carry

        lax.fori_loop(0, lax.shift_right_logical(i, 1), two_steps, 0)

        @pl.when((i & 1) == 1)
        def _():
            plain_step(i - 1, False)
            plain_step(i, True)

        @pl.when((i & 1) == 0)
        def _():
            plain_step(i, True)

        lsum = jnp.sum(lpt_ref[...], axis=0, keepdims=True)
        ot = acct_ref[:, 0:tq] / lsum[:, 0:tq] - lam * (acct_ref[:, tq:] / lsum[:, tq:])
        ot = ot * lax.rsqrt(jnp.mean(ot * ot, axis=0, keepdims=True) + SUBLN_EPS) * sgc_ref[...]
        o_ref[...] = (ot * (1.0 - lambda_init)).T.astype(o_ref.dtype)

    def online_step(j, masked):
        off = pl.multiple_of(j * tq, tq)
        s = _nt(qz_ref[...], k_ref[pl.ds(off, tq), :])
        if masked:
            row = lax.broadcasted_iota(I32, s.shape, 0)
            col = lax.broadcasted_iota(I32, s.shape, 1)
            s = jnp.where((col // CHUNK) <= ((row & (tq - 1)) // CHUNK), s, NEG)
        m_old = m_ref[...]
        m_new = jnp.maximum(m_old, jnp.max(s, axis=-1, keepdims=True))
        alpha = jnp.exp2(m_old - m_new)
        pr = jnp.exp2(s - m_new)
        l_ref[...] = alpha * l_ref[...] + jnp.sum(pr, axis=-1, keepdims=True)
        acc_ref[...] = alpha * acc_ref[...] + jnp.dot(pr.astype(BF16), v_ref[pl.ds(off, tq), :],
                                                      preferred_element_type=F32)
        m_ref[...] = m_new

    @pl.when(jnp.logical_not(bounded))
    def _():
        m_ref[...] = jnp.full_like(m_ref, NEG)
        l_ref[...] = jnp.zeros_like(l_ref)
        acc_ref[...] = jnp.zeros_like(acc_ref)
        run(online_step)
        o = acc_ref[0:tq, :] / l_ref[0:tq, :] - lam * (acc_ref[tq:, :] / l_ref[tq:, :])
        o = o * lax.rsqrt(jnp.mean(o * o, axis=-1, keepdims=True) + SUBLN_EPS) * sg_ref[...]
        o_ref[...] = (o * (1.0 - lambda_init)).astype(o_ref.dtype)


def _diff_attention(qn, kn, v, score_bound, lam_vecs, subln_g, B, S, lambda_init, tq=1024):
    W = qn.shape[1]
    H = W // LANES
    tq = min(tq, S)
    q3 = qn.reshape(B, S, W)
    k3 = kn.reshape(B, S, W)
    v3 = v.reshape(B, S, W)
    vt3 = v3.transpose(0, 2, 1)
    flag = (score_bound <= MAX_PLAIN_SCORE).astype(I32).reshape(1)
    qblk = pl.BlockSpec((None, tq, LANES), lambda b, h, i, f: (b, i, h))
    kvblk = pl.BlockSpec((None, S, LANES), lambda b, h, i, f: (b, 0, h))
    grid_spec = pltpu.PrefetchScalarGridSpec(
        num_scalar_prefetch=1,
        grid=(B, H, S // tq),
        in_specs=[qblk, kvblk, kvblk,
                  pl.BlockSpec((None, LANES, S), lambda b, h, i, f: (b, h, 0)),
                  pl.BlockSpec((4, DA_HEAD_DIM), lambda b, h, i, f: (0, 0)),
                  pl.BlockSpec((1, LANES), lambda b, h, i, f: (0, 0)),
                  pl.BlockSpec((LANES, 1), lambda b, h, i, f: (0, 0))],
        out_specs=qblk,
        scratch_shapes=[pltpu.VMEM((2 * tq, LANES), BF16),
                        pltpu.VMEM((2 * tq, 1), F32),
                        pltpu.VMEM((2 * tq, 1), F32),
                        pltpu.VMEM((2 * tq, LANES), F32),
                        pltpu.VMEM((8, 2 * tq), F32),
                        pltpu.VMEM((LANES, 2 * tq), F32)],
    )
    out = pl.pallas_call(
        functools.partial(_attn_kernel, tq=tq, lambda_init=lambda_init),
        grid_spec=grid_spec,
        out_shape=jax.ShapeDtypeStruct((B, S, W), BF16),
        compiler_params=_cp("parallel", "parallel", "arbitrary"),
        name="diff_attn",
    )(flag, q3, k3, v3, vt3, lam_vecs, subln_g.reshape(1, LANES), subln_g.reshape(LANES, 1))
    return out.reshape(B * S, W)


def _stackmask(m):
    lane = lax.broadcasted_iota(I32, m.shape, 1)
    z = jnp.zeros_like(m)
    return jnp.concatenate([jnp.where(lane < RWKV_HEAD, m, z), jnp.where(lane >= RWKV_HEAD, m, z)], axis=0)


def _pair_sum(x, first):
    s1 = jnp.sum(jnp.where(first, x, 0.0), axis=-1, keepdims=True)
    s2 = jnp.sum(jnp.where(first, 0.0, x), axis=-1, keepdims=True)
    return jnp.where(first, s1, s2)


def _rwkv_scan_kernel(r_ref, k_ref, v_ref, a_ref, ld_ref, g_ref, kk_ref, ka_ref, rk_ref, lnw_ref, lnb_ref,
                      o_ref, s_ref, *, L):
    nb, tb, W = r_ref.shape
    tm = nb * tb
    n_chunks = tm // L
    per_batch = tb // L
    n_pairs = W // LANES
    hd = RWKV_HEAD
    bf = lambda t: t.astype(BF16)
    rows2d = lambda ref: ref[...].reshape(tm, W)

    @pl.when(pl.program_id(0) == 0)
    def _():
        s_ref[...] = jnp.zeros_like(s_ref)

    row = lax.broadcasted_iota(I32, (tm, tm), 0)
    col = lax.broadcasted_iota(I32, (tm, tm), 1)
    tri = jnp.where(jnp.logical_and(col <= row, (col // L) == (row // L)), 1.0, 0.0).astype(BF16)
    ld = rows2d(ld_ref)
    ld_hi = bf(ld)
    rem = ld - ld_hi.astype(F32)
    ld_mid = bf(rem)
    ld_lo = bf(rem - ld_mid.astype(F32))
    c = (jnp.dot(tri, ld_hi, preferred_element_type=F32) + jnp.dot(tri, ld_mid, preferred_element_type=F32)
         + jnp.dot(tri, ld_lo, preferred_element_type=F32))
    ec = jnp.exp(c)
    eci = jnp.exp(-c)
    ecm = jnp.exp(c - ld)
    r = rows2d(r_ref)
    k = rows2d(k_ref)
    v = rows2d(v_ref)
    a = rows2d(a_ref)
    kkr = k * kk_ref[...]
    kmod = k * (1.0 + (a - 1.0) * ka_ref[...])
    brk = r * kmod * rk_ref[...]

    lane = lax.broadcasted_iota(I32, (L, LANES), 1)
    rowl = lax.broadcasted_iota(I32, (L, LANES), 0)
    first = lane < hd
    lane_h = lane & (hd - 1)
    strict = lane_h < rowl
    incl = lane_h <= rowl
    eye = jnp.where(lane_h == rowl, 1.0, 0.0)

    chains = [(ch, p) for ch in range(n_chunks) for p in range(n_pairs)]
    rsl = lambda ch: slice(ch * L, (ch + 1) * L)
    csl = lambda p: slice(p * LANES, (p + 1) * LANES)
    fdot = lambda x, y: jnp.dot(x, y, preferred_element_type=F32)
    at, bt, kt, rt, vh, g_l = {}, {}, {}, {}, {}, {}
    for c_ in chains:
        ch, p = c_
        rs, cs = rsl(ch), csl(p)
        kkh = kkr[rs, cs]
        kkh = kkh / jnp.maximum(jnp.sqrt(_pair_sum(kkh * kkh, first)), 1e-12)
        vh[c_] = v[rs, cs]
        g_l[c_] = ec[ch * L + L - 1:ch * L + L, cs]
        at[c_] = -kkh * ecm[rs, cs]
        bt[c_] = kkh * a[rs, cs] * eci[rs, cs]
        kt[c_] = kmod[rs, cs] * eci[rs, cs]
        rt[c_] = r[rs, cs] * ec[rs, cs]
    gm = {c_: _nt(bf(jnp.concatenate([at[c_], rt[c_]], axis=0)),
                  jnp.concatenate([_stackmask(bf(bt[c_])), _stackmask(bf(kt[c_]))], axis=0)) for c_ in chains}
    a_ab = {c_: jnp.where(strict, gm[c_][:L, :LANES], 0.0) for c_ in chains}
    vsm = {c_: _stackmask(bf(vh[c_])) for c_ in chains}
    akv = {c_: fdot(bf(jnp.concatenate([jnp.where(strict, gm[c_][:L, LANES:], 0.0),
                                        jnp.where(incl, gm[c_][L:, LANES:], 0.0)], axis=0)), vsm[c_])
           for c_ in chains}
    cmat = {c_: akv[c_][:L] for c_ in chains}
    t_inv = {c_: eye + a_ab[c_] for c_ in chains}
    pw = {c_: bf(a_ab[c_]) for c_ in chains}
    for _ in range(int(math.log2(L)) - 1):
        pw = {c_: bf(fdot(pw[c_], _stackmask(pw[c_]))) for c_ in chains}
        t_inv = {c_: t_inv[c_] + fdot(pw[c_], _stackmask(bf(t_inv[c_]))) for c_ in chains}
    zz = {c_: fdot(bf(t_inv[c_]), jnp.concatenate([_stackmask(bf(at[c_])), _stackmask(bf(cmat[c_]))], axis=1))
          for c_ in chains}
    qy = {c_: fdot(bf(jnp.where(incl, gm[c_][L:, :LANES], 0.0)),
                   jnp.concatenate([_stackmask(bf(zz[c_][:, :LANES])), _stackmask(bf(zz[c_][:, LANES:]))], axis=1))
          for c_ in chains}
    y0 = {c_: qy[c_][:, LANES:] + akv[c_][L:] for c_ in chains}
    qa = {c_: bf(jnp.concatenate([rt[c_] + qy[c_][:, :LANES], zz[c_][:, :LANES]], axis=0)) for c_ in chains}
    bkg = {c_: bf(jnp.concatenate([bt[c_] * g_l[c_], kt[c_] * g_l[c_]], axis=0)) for c_ in chains}

    lane_s = lax.broadcasted_iota(I32, (hd, LANES), 1)
    heads = [(b, p) for b in range(nb) for p in range(n_pairs)]
    sp = {bp: s_ref[bp[0] * n_pairs + bp[1]] for bp in heads}
    for j in range(per_batch):
        chunk = lambda b: b * per_batch + j
        yw = {(b, p): _nt(qa[chunk(b), p], _stackmask(bf(sp[b, p]))) for b, p in heads}
        upd = {(b, p): _tn(bf(jnp.concatenate([yw[b, p][L:] + zz[chunk(b), p][:, LANES:], vh[chunk(b), p]],
                                              axis=0)), bkg[chunk(b), p]) for b, p in heads}
        for b, p in heads:
            ch, cs, ts = chunk(b), csl(p), rsl(j)
            sp[b, p] = sp[b, p] * g_l[ch, p] + jnp.where(lane_s < hd, upd[b, p][:hd], upd[b, p][hd:])
            y = yw[b, p][:L] + y0[ch, p]
            mean = _pair_sum(y, first) * (1.0 / hd)
            yc = y - mean
            var = _pair_sum(yc * yc, first) * (1.0 / hd)
            yn = yc * lax.rsqrt(var + GN_EPS) * lnw_ref[:, cs] + lnb_ref[:, cs]
            bonus = _pair_sum(brk[rsl(ch), cs], first) * vh[ch, p]
            o_ref[b, ts, cs] = ((yn + bonus) * g_ref[b, ts, cs]).astype(o_ref.dtype)
    for b, p in heads:
        s_ref[b * n_pairs + p] = sp[b, p]


def _rwkv_scan(r, k, v, a, ld, g, k_k, k_a, r_k, ln_w, ln_b, L=RWKV_CHUNK, tm=128):
    B, S, W = r.shape
    seq = pl.BlockSpec((B, tm, W), lambda c: (0, c, 0))
    vec = pl.BlockSpec((1, W), lambda c: (0, 0))
    return pl.pallas_call(
        functools.partial(_rwkv_scan_kernel, L=L),
        grid=(S // tm,),
        in_specs=[seq] * 6 + [vec] * 5,
        out_specs=seq,
        out_shape=jax.ShapeDtypeStruct((B, S, W), BF16),
        scratch_shapes=[pltpu.VMEM((B * (W // LANES), RWKV_HEAD, LANES), F32)],
        compiler_params=_cp("arbitrary"),
        name="rwkv_scan",
    )(r, k, v, a, ld, g, k_k.reshape(1, W), k_a.reshape(1, W), r_k.reshape(1, W),
      ln_w.reshape(1, W), ln_b.reshape(1, W))


def _post_kernel(attn_ref, rw_ref, gate_ref, x_ref, mod_ref, wa_ref, wb_ref, wo_ref, g2_ref, wrh_ref, wrm_ref,
                 sug_ref, sd_ref, base_ref, h2p_ref, lg_ref):
    D = x_ref.shape[1]
    ya = jnp.dot(attn_ref[...], wa_ref[...], preferred_element_type=F32)
    yb = jnp.dot(rw_ref[...], wb_ref[...], preferred_element_type=F32)
    m = gate_ref[:, 0:D] * ya + gate_ref[:, D:] * yb
    x1 = x_ref[...] + mod_ref[2:3, :] * jnp.dot(m.astype(BF16), wo_ref[...], preferred_element_type=F32)
    y = x1 * lax.rsqrt(jnp.mean(x1 * x1, axis=-1, keepdims=True) + NORM_EPS) * g2_ref[...]
    h2 = y * (1.0 + mod_ref[4:5, :]) + mod_ref[3:4, :]
    hb = h2.astype(BF16)
    hm = (h2 - hb.astype(F32)).astype(BF16)
    lg_ref[...] = _nt(wrh_ref[...], hb) + _nt(wrh_ref[...], hm) + _nt(wrm_ref[...], hb)
    packed = _pack_halves(hb.astype(F32))
    half = packed.shape[1] // 2
    h2p_ref[0] = packed[:, :half]
    h2p_ref[1] = packed[:, half:]
    F = sd_ref.shape[0]
    gu = jnp.dot(hb, sug_ref[...], preferred_element_type=F32)
    shared = jnp.dot((_silu(gu[:, :F]) * gu[:, F:]).astype(BF16), sd_ref[...], preferred_element_type=F32)
    base_ref[...] = x1 + mod_ref[5:6, :] * shared


def _post(attn, rw, gate, x2, mod, wa, wb, wo, norm2_g, w_router_t, sug, sd, S, tm=512):
    N, D = x2.shape
    E = w_router_t.shape[0]
    wr_hi = w_router_t.astype(BF16)
    wr_mid = (w_router_t - wr_hi.astype(F32)).astype(BF16)
    tpb = S // tm
    row = lambda w: pl.BlockSpec((tm, w), lambda i: (i, 0))
    full = lambda a: pl.BlockSpec(a.shape, lambda i: (0, 0))
    return pl.pallas_call(
        _post_kernel,
        grid=(N // tm,),
        in_specs=[row(attn.shape[1]), row(rw.shape[1]), row(gate.shape[1]), row(D),
                  pl.BlockSpec((None, 6, D), lambda i: (i // tpb, 0, 0)),
                  full(wa), full(wb), full(wo), pl.BlockSpec((1, D), lambda i: (0, 0)), full(wr_hi), full(wr_mid),
                  full(sug), full(sd)],
        out_specs=[row(D), pl.BlockSpec((2, tm, D // 4), lambda i: (0, i, 0)), pl.BlockSpec((E, tm), lambda i: (0, i))],
        out_shape=[jax.ShapeDtypeStruct((N, D), F32), jax.ShapeDtypeStruct((2, N, D // 4), U32),
                   jax.ShapeDtypeStruct((E, N), F32)],
        compiler_params=_cp("parallel"),
        name="post_mixer",
    )(attn, rw, gate, x2, mod, wa, wb, wo, norm2_g.reshape(1, D), wr_hi, wr_mid, sug, sd)


def _first_argmax(x, idx, sentinel):
    m = jnp.max(x, axis=0, keepdims=True)
    return m, jnp.min(jnp.where(x == m, idx, sentinel), axis=0, keepdims=True)


def _route_kernel(lg_ref, bias_ref, e_ref, w_ref):
    E, T = lg_ref.shape
    gsz = E // N_GROUPS
    scores = _sigmoid(lg_ref[...])
    biased = scores + bias_ref[...]
    ig = lax.broadcasted_iota(I32, (gsz, T), 0)
    gs = []
    for g in range(N_GROUPS):
        blk = biased[g * gsz:(g + 1) * gsz, :]
        m1, i1 = _first_argmax(blk, ig, gsz)
        m2 = jnp.max(jnp.where(ig == i1, -jnp.inf, blk), axis=0, keepdims=True)
        gs.append(m1 + m2)
    gsc = jnp.concatenate(gs, axis=0)
    i8 = lax.broadcasted_iota(I32, (N_GROUPS, T), 0)
    chosen = jnp.zeros((N_GROUPS, T), F32)
    for _ in range(TOPK_GROUPS):
        _, gi = _first_argmax(gsc, i8, N_GROUPS)
        hit = i8 == gi
        chosen = jnp.where(hit, 1.0, chosen)
        gsc = jnp.where(hit, -jnp.inf, gsc)
    masked = jnp.concatenate(
        [jnp.where(chosen[g:g + 1, :] > 0.0, biased[g * gsz:(g + 1) * gsz, :], -jnp.inf)
         for g in range(N_GROUPS)], axis=0)
    ie = lax.broadcasted_iota(I32, (E, T), 0)
    idxs, wts = [], []
    for _ in range(TOP_K):
        _, ei = _first_argmax(masked, ie, E)
        hit = ie == ei
        idxs.append(ei)
        wts.append(jnp.sum(jnp.where(hit, scores, 0.0), axis=0, keepdims=True))
        masked = jnp.where(hit, -jnp.inf, masked)
    wt = jnp.concatenate(wts, axis=0)
    e_ref[...] = jnp.concatenate(idxs, axis=0)
    w_ref[...] = wt / jnp.sum(wt, axis=0, keepdims=True) * ROUTED_SCALE


def _route(logits_t, router_bias, T=512):
    E, N = logits_t.shape
    blk = pl.BlockSpec((TOP_K, T), lambda i: (0, i))
    return pl.pallas_call(
        _route_kernel,
        grid=(N // T,),
        in_specs=[pl.BlockSpec((E, T), lambda i: (0, i)), pl.BlockSpec((E, 1), lambda i: (0, 0))],
        out_specs=[blk, blk],
        out_shape=[jax.ShapeDtypeStruct((TOP_K, N), I32), jax.ShapeDtypeStruct((TOP_K, N), F32)],
        compiler_params=_cp("parallel"),
        name="route",
    )(logits_t, router_bias.reshape(E, 1))


def _rank_kernel(e_ref, rank_ref, cnt_ref, carry_ref, *, n_experts):
    T = e_ref.shape[1]

    @pl.when(pl.program_id(0) == 0)
    def _():
        carry_ref[...] = jnp.zeros_like(carry_ref)

    ie = lax.broadcasted_iota(I32, (n_experts, T), 0)
    e = e_ref[...]
    hits = [ie == e[kk:kk + 1, :] for kk in range(TOP_K)]
    onehot = jnp.zeros((n_experts, T), F32)
    for hsel in hits:
        onehot = onehot + hsel.astype(F32)
    tr = lax.broadcasted_iota(I32, (T, T), 0)
    tc = lax.broadcasted_iota(I32, (T, T), 1)
    before = (tr < tc).astype(BF16)
    base = _bdot(onehot, before) + carry_ref[:, 0:1]
    rank_ref[...] = jnp.concatenate(
        [jnp.sum(jnp.where(hsel, base, 0.0), axis=0, keepdims=True) for hsel in hits], axis=0).astype(I32)
    carry_ref[...] = carry_ref[...] + jnp.sum(onehot, axis=1, keepdims=True)
    cnt_ref[...] = carry_ref[...]


def _ranks(eidx_t, n_experts, T=512):
    N = eidx_t.shape[1]
    T = min(T, N)
    blk = pl.BlockSpec((TOP_K, T), lambda i: (0, i))
    cnt = pl.BlockSpec((n_experts, LANES), lambda i: (0, 0))
    return pl.pallas_call(
        functools.partial(_rank_kernel, n_experts=n_experts),
        grid=(N // T,),
        in_specs=[blk],
        out_specs=[blk, cnt],
        out_shape=[jax.ShapeDtypeStruct((TOP_K, N), I32), jax.ShapeDtypeStruct((n_experts, LANES), F32)],
        scratch_shapes=[pltpu.VMEM((n_experts, LANES), F32)],
        compiler_params=_cp("arbitrary"),
        name="ranks",
    )(eidx_t)


def _dest_kernel(e_ref, rank_ref, start_ref, d_ref):
    E = start_ref.shape[0]
    T = e_ref.shape[1]
    ie = lax.broadcasted_iota(I32, (E, T), 0)
    e = e_ref[...]
    start = start_ref[:, 0:1]
    rows = [jnp.sum(jnp.where(ie == e[kk:kk + 1, :], start, 0.0), axis=0, keepdims=True) for kk in range(TOP_K)]
    d_ref[...] = jnp.concatenate(rows, axis=0).astype(I32) + rank_ref[...]


def _dests(eidx_t, rank_t, pstart, T=2048):
    N = eidx_t.shape[1]
    T = min(T, N)
    E = pstart.shape[0]
    blk = pl.BlockSpec((TOP_K, T), lambda i: (0, i))
    return pl.pallas_call(
        _dest_kernel,
        grid=(N // T,),
        in_specs=[blk, blk, pl.BlockSpec((E, LANES), lambda i: (0, 0))],
        out_specs=blk,
        out_shape=jax.ShapeDtypeStruct((TOP_K, N), I32),
        compiler_params=_cp("parallel"),
        name="dests",
    )(eidx_t, rank_t, jnp.broadcast_to(pstart.astype(F32)[:, None], (E, LANES)))


def _expert_kernel(us_ref, ps_ref, x_hbm, wug_hbm, wd_hbm, y_hbm,
                   wug_f32, wd_f32, wug_bf, wd_bf, xbuf, ybuf, cnt_ref, wsem, xsem, ysem, *, n_rows):
    e = pl.program_id(0)
    n_experts = pl.num_programs(0)
    _, _, R, Ch = xbuf.shape
    C = 2 * Ch
    F = wd_hbm.shape[1]

    def w_copies(expert, slot):
        return (pltpu.make_async_copy(wug_hbm.at[expert], wug_f32.at[slot], wsem.at[slot]),
                pltpu.make_async_copy(wd_hbm.at[expert], wd_f32.at[slot], wsem.at[slot]))

    def x_copy(slot, half, row):
        return pltpu.make_async_copy(x_hbm.at[half, pl.ds(row, R)], xbuf.at[slot, half], xsem.at[slot])

    def y_copy(slot, half, row):
        return pltpu.make_async_copy(ybuf.at[slot, half], y_hbm.at[half, pl.ds(row, R)], ysem.at[slot])

    def start(copy, slot, row):
        for half in range(2):
            copy(slot, half, row).start()

    def wait(copy, slot):
        for half in range(2):
            copy(slot, half, 0).wait()

    @pl.when(e == 0)
    def _():
        for ahead in range(W_SLOTS - 1):
            for cp in w_copies(ahead, ahead):
                cp.start()
        cnt_ref[0] = 0
        for t in range(X_AHEAD):
            start(x_copy, t, t * R)
        ybuf[...] = jnp.zeros_like(ybuf)
        for slot in range(Y_SLOTS):
            start(y_copy, slot, n_rows + slot * R)

    @pl.when(e + (W_SLOTS - 1) < n_experts)
    def _():
        for cp in w_copies(e + (W_SLOTS - 1), lax.rem(e + (W_SLOTS - 1), W_SLOTS)):
            cp.start()

    wslot = lax.rem(e, W_SLOTS)
    for cp in w_copies(e, wslot):
        cp.wait()
    wug_bf[...] = wug_f32[wslot].astype(BF16)
    wd_bf[...] = wd_f32[wslot].astype(BF16)
    n_valid = us_ref[e + 1] - us_ref[e]
    pbase = ps_ref[e]
    n_tiles = lax.shift_right_logical(n_valid + (R - 1), int(math.log2(R)))

    def make_tile(nb):
        rowid = lax.broadcasted_iota(I32, (nb * R, C), 0)

        def tile(blk0):
            n_done = cnt_ref[0]
            xslots = [lax.rem(n_done + b, X_SLOTS) for b in range(nb)]
            for b in range(nb):
                ahead = n_done + X_AHEAD + b
                start(x_copy, lax.rem(ahead, X_SLOTS), pl.multiple_of(ahead * R, R))
            for b in range(nb):
                wait(x_copy, xslots[b])
            x = jnp.concatenate([jnp.concatenate([xbuf[xs, 0], xbuf[xs, 1]], axis=1) for xs in xslots], axis=0)
            x = jnp.where(rowid < n_valid - blk0 * R, x, jnp.uint32(0))
            lo, hi = _unpack_halves(x)
            gu = (jnp.dot(lo.astype(BF16), wug_bf[0:C, :], preferred_element_type=F32)
                  + jnp.dot(hi.astype(BF16), wug_bf[C:, :], preferred_element_type=F32))
            hid = (_silu(gu[:, :F]) * gu[:, F:]).astype(BF16)
            y = jnp.dot(hid, wd_bf[...], preferred_element_type=F32)
            packed = _pack_halves(y.astype(BF16).astype(F32))
            for b in range(nb):
                slot = lax.rem(n_done + b, Y_SLOTS)
                wait(y_copy, slot)
                ybuf[slot, 0] = packed[b * R:(b + 1) * R, :Ch]
                ybuf[slot, 1] = packed[b * R:(b + 1) * R, Ch:]
                start(y_copy, slot, pl.multiple_of(pbase + (blk0 + b) * R, R))
            cnt_ref[0] = n_done + nb

        return tile

    tiles = {nb: make_tile(nb) for nb in range(1, TILE_BLOCKS + 1)}
    quads = lax.shift_right_logical(n_tiles, 2)
    left = n_tiles & 3
    five = jnp.logical_and(left == 1, quads >= 1)
    quads = quads - jnp.where(five, 1, 0)

    def quad_tiles(j, carry):
        tiles[4](4 * j)
        return carry

    lax.fori_loop(0, quads, quad_tiles, 0)

    @pl.when(five)
    def _():
        tiles[5](n_tiles - 5)

    for nb in (1, 2, 3):
        @pl.when(jnp.logical_and(left == nb, jnp.logical_not(five)))
        def _(nb=nb):
            tiles[nb](n_tiles - nb)

    @pl.when(e == pl.num_programs(0) - 1)
    def _():
        for t in range(X_AHEAD):
            wait(x_copy, lax.rem(cnt_ref[0] + t, X_SLOTS))
        for slot in range(Y_SLOTS):
            wait(y_copy, slot)
        ybuf[0] = jnp.zeros((2, R, Ch), U32)
        first = lax.shift_right_logical(pbase + n_tiles * R, int(math.log2(R)))
        n_left = n_rows // R - first

        def fill(t, carry):
            start(y_copy, 0, pl.multiple_of((first + t) * R, R))
            return carry

        def drain(t, carry):
            wait(y_copy, 0)
            return carry

        lax.fori_loop(0, n_left, fill, 0)
        lax.fori_loop(0, n_left, drain, 0)


def _experts(ustart, pstart, xg, w_ug, w_d, n_rows, R):
    _, _, Ch = xg.shape
    E, D, F2 = w_ug.shape
    F = w_d.shape[1]
    grid_spec = pltpu.PrefetchScalarGridSpec(
        num_scalar_prefetch=2,
        grid=(E,),
        in_specs=[pl.BlockSpec(memory_space=pl.ANY)] * 3,
        out_specs=pl.BlockSpec(memory_space=pl.ANY),
        scratch_shapes=[pltpu.VMEM((W_SLOTS, D, F2), F32), pltpu.VMEM((W_SLOTS, F, D), F32),
                        pltpu.VMEM((D, F2), BF16), pltpu.VMEM((F, D), BF16),
                        pltpu.VMEM((X_SLOTS, 2, R, Ch), U32), pltpu.VMEM((Y_SLOTS, 2, R, Ch), U32), pltpu.SMEM((1,), I32),
                        pltpu.SemaphoreType.DMA((W_SLOTS,)), pltpu.SemaphoreType.DMA((X_SLOTS,)),
                        pltpu.SemaphoreType.DMA((Y_SLOTS,))],
    )
    return pl.pallas_call(
        functools.partial(_expert_kernel, n_rows=n_rows),
        grid_spec=grid_spec,
        out_shape=jax.ShapeDtypeStruct((2, n_rows + Y_SLOTS * R, Ch), U32),
        compiler_params=_cp("arbitrary"),
        name="experts",
    )(ustart, pstart, xg, w_ug, w_d)


def _sc_scatter_rows(src, idx, n_rows):
    H, N, C = src.shape
    K = idx.shape[0]
    per_row = N // SC_WINDOW
    mesh = plsc.VectorSubcoreMesh(core_axis_name="c", subcore_axis_name="s")

    @functools.partial(pl.kernel, out_type=jax.ShapeDtypeStruct((H, n_rows, C), src.dtype), mesh=mesh,
                       scratch_types=[])
    def scatter_kernel(x_hbm, i_hbm, o_hbm):
        for h in range(H):
            def body(x_vmem, i_vmem):
                for k in range(K):
                    pltpu.sync_copy(x_vmem, o_hbm.at[h].at[i_vmem.at[k]])

            pltpu.emit_pipeline(
                body,
                grid=(per_row,),
                in_specs=[pl.BlockSpec((SC_WINDOW, C), lambda i: (i, 0)),
                          pl.BlockSpec((K, SC_WINDOW), lambda i: (0, i))],
                out_specs=[],
                core_axis_name=("c", "s"),
                dimension_semantics=(pltpu.PARALLEL,),
            )(x_hbm.at[h], i_hbm)

    return scatter_kernel(src, idx)


def _sc_gather_rows(src, idx):
    H, _, C = src.shape
    K, N = idx.shape
    per_row = N // SC_WINDOW
    mesh = plsc.VectorSubcoreMesh(core_axis_name="c", subcore_axis_name="s")

    @functools.partial(pl.kernel, out_type=jax.ShapeDtypeStruct((H, K * N, C), src.dtype), mesh=mesh,
                       scratch_types=[])
    def gather_kernel(x_hbm, i_hbm, o_hbm):
        for h in range(H):
            def body(i_vmem, o_vmem):
                pltpu.sync_copy(x_hbm.at[h].at[i_vmem.at[0]], o_vmem)

            pltpu.emit_pipeline(
                body,
                grid=(K * per_row,),
                in_specs=[pl.BlockSpec((1, SC_WINDOW), lambda i: (i // per_row, i % per_row))],
                out_specs=[pl.BlockSpec((SC_WINDOW, C), lambda i: (i, 0))],
                core_axis_name=("c", "s"),
                dimension_semantics=(pltpu.PARALLEL,),
            )(i_hbm, o_hbm.at[h])

    return gather_kernel(src, idx)


def _combine_kernel(*refs):
    y_refs = refs[:2 * TOP_K]
    w_ref, base_ref, mod_ref, o_ref = refs[2 * TOP_K:]
    T = base_ref.shape[0]
    tr = lax.broadcasted_iota(I32, (T, T), 0)
    tc = lax.broadcasted_iota(I32, (T, T), 1)
    wcol = _nt((tr == tc).astype(F32), w_ref[...], precision=HI)
    acc = [None] * 4
    for kk in range(TOP_K):
        wk = wcol[:, kk:kk + 1]
        for half in range(2):
            lo, hi = _unpack_halves(y_refs[2 * kk + half][...])
            for q, val in ((half, lo), (2 + half, hi)):
                acc[q] = val * wk if acc[q] is None else acc[q] + val * wk
    o_ref[...] = base_ref[...] + mod_ref[5:6, :] * jnp.concatenate(acc, axis=1)


def _combine(yg, w_t, base, mod, S, T=512):
    N, D = base.shape
    C = yg.shape[2]
    tpb = S // T
    n_tiles = N // T
    row = pl.BlockSpec((T, D), lambda i: (i, 0))
    piece = lambda kk, half: pl.BlockSpec((None, T, C), lambda i: (half, kk * n_tiles + i, 0))
    return pl.pallas_call(
        _combine_kernel,
        grid=(n_tiles,),
        in_specs=[piece(kk, half) for kk in range(TOP_K) for half in range(2)] + [
            pl.BlockSpec((TOP_K, T), lambda i: (0, i)),
            row,
            pl.BlockSpec((None, 6, D), lambda i: (i // tpb, 0, 0))],
        out_specs=row,
        out_shape=jax.ShapeDtypeStruct((N, D), F32),
        compiler_params=_cp("parallel"),
        name="combine",
    )(*([yg] * (2 * TOP_K)), w_t, base, mod)


def _layer(x, c, positions, layer_idx, w_ada, b_ada, norm1_g, w_in, w_gate, b_gate,
           q_norm_g, k_norm_g, lambda_q1, lambda_k1, lambda_q2, lambda_k2, subln_g,
           rwkv_mu, w_decay0, w_decay2, a0, a2, g2, k_k, k_a, r_k, ln_x_w, ln_x_b,
           w_branch_a, w_branch_b, w_out, norm2_g, w_router, router_bias,
           w_expert_up_gate, w_expert_down, w_shared_up_gate, w_shared_down):
    B, S, D = x.shape
    N = B * S
    E = w_router.shape[1]
    da_width = w_branch_a.shape[0]
    rw_width = w_branch_b.shape[0]
    lambda_init = 0.8 - 0.6 * math.exp(-0.3 * layer_idx)

    mod = _adaln(c, w_ada, b_ada)
    x2 = x.reshape(N, D)
    w_cat = jnp.concatenate([w_in, w_gate], axis=1).astype(BF16)
    qn, kn, v, r_, k_, v_, a_, ld_, g_, gate = _mixer_in(
        x2, positions.reshape(N, 1), mod, norm1_g, w_cat, b_gate, q_norm_g, k_norm_g,
        rwkv_mu, w_decay0, w_decay2, a0, a2, g2, S, da_width, rw_width)

    lam_vecs = jnp.stack([lambda_q1, lambda_k1, lambda_q2, lambda_k2])
    score_bound = 1.01 * DA_HEAD_DIM ** 0.5 * jnp.max(jnp.abs(q_norm_g)) * jnp.max(jnp.abs(k_norm_g))
    attn = _diff_attention(qn, kn, v, score_bound, lam_vecs, subln_g, B, S, lambda_init)

    seq = lambda t: t.reshape(B, S, rw_width)
    rw = _rwkv_scan(seq(r_), seq(k_), seq(v_), seq(a_), seq(ld_), seq(g_), k_k, k_a, r_k.reshape(-1),
                    ln_x_w, ln_x_b).reshape(N, rw_width)

    base, h2p, logits_t = _post(attn, rw, gate, x2, mod, w_branch_a.astype(BF16), w_branch_b.astype(BF16),
                                w_out.astype(BF16), norm2_g, w_router.T,
                                w_shared_up_gate.astype(BF16), w_shared_down.astype(BF16), S)

    eidx_t, w_t = _route(logits_t, router_bias)
    rank_t, counts = _ranks(eidx_t, E)
    R = EXPERT_TILE
    cnt = counts[:, 0].astype(I32)
    ustart = jnp.concatenate([jnp.zeros((1,), I32), jnp.cumsum(cnt)])
    pcnt = (cnt + R - 1) // R * R
    pstart = jnp.cumsum(pcnt) - pcnt
    dest_p = _dests(eidx_t, rank_t, pstart)
    n_rows = (N * TOP_K + E * (R - 1) + R - 1) // R * R
    xg = _sc_scatter_rows(h2p, dest_p, n_rows + X_AHEAD * R)
    y = _experts(ustart, pstart, xg, w_expert_up_gate, w_expert_down, n_rows, R)
    yg = _sc_gather_rows(y, dest_p)
    out = _combine(yg, w_t, base, mod, S)
    return out.reshape(B, S, D)


def kernel(x, c, positions, w_ada, b_ada, norm1_g, w_in, w_gate, b_gate, q_norm_g, k_norm_g, lambda_q1, lambda_k1, lambda_q2, lambda_k2, subln_g, rwkv_mu, w_decay0, w_decay2, a0, a2, g2, k_k, k_a, r_k, ln_x_w, ln_x_b, w_branch_a, w_branch_b, w_out, norm2_g, w_router, router_bias, w_expert_up_gate, w_expert_down, w_shared_up_gate, w_shared_down):
    for l in range(w_ada.shape[0]):
        x = _layer(x, c, positions, l, w_ada[l], b_ada[l], norm1_g[l], w_in[l], w_gate[l], b_gate[l],
                   q_norm_g[l], k_norm_g[l], lambda_q1[l], lambda_k1[l], lambda_q2[l], lambda_k2[l],
                   subln_g[l], rwkv_mu[l], w_decay0[l], w_decay2[l], a0[l], a2[l], g2[l], k_k[l],
                   k_a[l], r_k[l], ln_x_w[l], ln_x_b[l], w_branch_a[l], w_branch_b[l], w_out[l],
                   norm2_g[l], w_router[l], router_bias[l], w_expert_up_gate[l], w_expert_down[l],
                   w_shared_up_gate[l], w_shared_down[l])
    return x
```

```python
import functools
import math

import jax
import jax.numpy as jnp
from jax import lax
from jax.experimental import pallas as pl
from jax.experimental.pallas import tpu as pltpu
from jax.experimental.pallas import tpu_sc as plsc

F32 = jnp.float32
BF16 = jnp.bfloat16
I32 = jnp.int32
U32 = jnp.uint32
HI = lax.Precision.HIGHEST

CHUNK = 64
ROPE_THETA = 10000.0
NORM_EPS = 1e-6
SUBLN_EPS = 1e-5
DA_HEAD_DIM = 64
RWKV_HEAD = 64
GN_EPS = 64e-5
TOP_K = 8
N_GROUPS = 8
TOPK_GROUPS = 4
ROUTED_SCALE = 2.5
EXPERT_TILE = 128
W_SLOTS = 3
TILE_BLOCKS = 5
X_SLOTS = 13
X_AHEAD = X_SLOTS - TILE_BLOCKS
Y_SLOTS = 8
RWKV_CHUNK = 64
LANES = 128
SC_WINDOW = 128
NEG = -1e30
MAX_PLAIN_SCORE = 40.0
VMEM_LIMIT = 56 * 1024 * 1024


def _cp(*sem):
    return pltpu.CompilerParams(dimension_semantics=sem, vmem_limit_bytes=VMEM_LIMIT)


def _bdot(a, b):
    return jnp.dot(a.astype(BF16), b.astype(BF16), preferred_element_type=F32)


def _fdot(a, b):
    return jnp.dot(a, b, precision=HI, preferred_element_type=F32)


def _nt(a, b, precision=None):
    return lax.dot_general(a, b, (((1,), (1,)), ((), ())), precision=precision,
                           preferred_element_type=F32)


def _tn(a, b, precision=None):
    return lax.dot_general(a, b, (((0,), (0,)), ((), ())), precision=precision,
                           preferred_element_type=F32)


def _pack_halves(x):
    c = x.shape[1] // 2
    lo = lax.bitcast_convert_type(x[:, :c], U32)
    hi = lax.bitcast_convert_type(x[:, c:], U32)
    return (hi & jnp.uint32(0xFFFF0000)) | (lo >> 16)


def _unpack_halves(w):
    lo = lax.bitcast_convert_type(w << 16, F32)
    hi = lax.bitcast_convert_type(w & jnp.uint32(0xFFFF0000), F32)
    return lo, hi


def _sigmoid(x):
    return 1.0 / (1.0 + jnp.exp(-x))


def _silu(x):
    return x * _sigmoid(x)


def _ada_kernel(c_ref, w_ref, b_ref, o_ref):
    o_ref[...] = _fdot(_silu(c_ref[...]), w_ref[...]) + b_ref[...]


def _adaln(c, w_ada, b_ada):
    B, D = c.shape
    rows = -(-B // 8) * 8
    cpad = jnp.zeros((rows, D), F32).at[:B].set(c)
    n_out = w_ada.shape[1]
    out = pl.pallas_call(
        _ada_kernel,
        grid=(n_out // D,),
        in_specs=[pl.BlockSpec((rows, D), lambda j: (0, 0)),
                  pl.BlockSpec((D, D), lambda j: (0, j)),
                  pl.BlockSpec((1, D), lambda j: (0, j))],
        out_specs=pl.BlockSpec((rows, D), lambda j: (0, j)),
        out_shape=jax.ShapeDtypeStruct((rows, n_out), F32),
        compiler_params=_cp("arbitrary"),
        name="adaln",
    )(cpad, w_ada, b_ada.reshape(1, n_out))
    return out[:B].reshape(B, n_out // D, D)


def _mixer_in_kernel(x_ref, xprev_ref, mod_ref, g_ref, w_ref, bg_ref, pos_ref, invf_ref, qg_ref, kg_ref,
                     mu_ref, w0_ref, w2_ref, a0_ref, a2_ref, g2_ref,
                     qn_ref, kn_ref, v_ref, r_ref, k_ref, vr_ref, a_ref, ld_ref, gr_ref, gate_ref,
                     *, da_width, rw_cols, rw_width, q_scale, tiles_per_seq):
    tm = x_ref.shape[0]

    def modulated(x):
        y = x * lax.rsqrt(jnp.mean(x * x, axis=-1, keepdims=True) + NORM_EPS) * g_ref[...]
        return (y * (1.0 + mod_ref[1:2, :]) + mod_ref[0:1, :]).astype(BF16)

    def proj(hb, c0, width, step=512):
        parts = [jnp.dot(hb, w_ref[:, c0 + o:c0 + min(o + step, width)], preferred_element_type=F32)
                 for o in range(0, width, step)]
        return parts[0] if len(parts) == 1 else jnp.concatenate(parts, axis=1)

    h_ext = modulated(jnp.concatenate([x_ref[...], xprev_ref[...]], axis=0))
    h = h_ext[:tm]

    lane = lax.broadcasted_iota(I32, (tm, LANES), 1)
    first = lane < DA_HEAD_DIM
    lo_half = (lane & (DA_HEAD_DIM - 1)) < DA_HEAD_DIM // 2
    ang = pos_ref[...].astype(F32) * invf_ref[...]
    cos = jnp.cos(ang)
    sin = jnp.sin(ang)
    sin = jnp.where(lo_half, -sin, sin)
    for c0, dst, gn_ref, mult in ((0, qn_ref, qg_ref, q_scale), (da_width, kn_ref, kg_ref, 1.0)):
        raw = proj(h, c0, da_width)
        for blk in range(da_width // LANES):
            x = raw[:, blk * LANES:(blk + 1) * LANES]
            xx = x * x
            s_first = jnp.sum(jnp.where(first, xx, 0.0), axis=-1, keepdims=True)
            s_second = jnp.sum(jnp.where(first, 0.0, xx), axis=-1, keepdims=True)
            ms = jnp.where(first, s_first, s_second) * (1.0 / DA_HEAD_DIM)
            xn = x * lax.rsqrt(ms + NORM_EPS) * gn_ref[...]
            rot = jnp.where(lo_half, pltpu.roll(xn, LANES - DA_HEAD_DIM // 2, axis=1),
                            pltpu.roll(xn, DA_HEAD_DIM // 2, axis=1))
            dst[:, blk * LANES:(blk + 1) * LANES] = ((xn * cos + rot * sin) * mult).astype(dst.dtype)
    v_ref[...] = proj(h, 2 * da_width, da_width).astype(v_ref.dtype)

    c_rw = 3 * da_width
    seq_start = (pl.program_id(0) % tiles_per_seq) == 0
    width = rw_width

    def shifted(o, ncols):
        p_ext = jnp.dot(h_ext, w_ref[:, c_rw + o:c_rw + o + ncols], preferred_element_type=F32)
        p = p_ext[:tm]
        last_prev = jnp.where(seq_start, 0.0, p_ext[tm + 7:tm + 8, :])
        rowi = lax.broadcasted_iota(I32, p.shape, 0)
        prev = jnp.where(rowi == 0, last_prev, pltpu.roll(p, 1, axis=0))
        return p + (prev - p) * mu_ref[:, o:o + ncols]

    r_ref[...] = shifted(0, width)
    k_ref[...] = shifted(width, width)
    vr_ref[...] = shifted(2 * width, width)
    xs = shifted(3 * width, rw_cols - 3 * width)
    xwa = xs[:, 0:LANES]
    xg = xs[:, LANES:]
    z = w0_ref[...] + _bdot(jnp.tanh(xwa), w2_ref[...])
    w = -(jnp.maximum(-z, 0.0) + jnp.log(1.0 + jnp.exp(-jnp.abs(z)))) - 0.5
    ld_ref[...] = -jnp.exp(w)
    a_ref[...] = _sigmoid(a0_ref[...] + _bdot(xwa, a2_ref[...]))
    gr_ref[...] = _bdot(_sigmoid(xg), g2_ref[...])

    c_gate = c_rw + rw_cols
    for o in range(0, gate_ref.shape[1], 512):
        e = min(o + 512, gate_ref.shape[1])
        gate_ref[:, o:e] = _sigmoid(jnp.dot(h, w_ref[:, c_gate + o:c_gate + e], preferred_element_type=F32)
                                    + bg_ref[:, o:e]).astype(gate_ref.dtype)


def _mixer_in(x2, pos2, mod, norm1_g, w_cat, b_gate, q_norm_g, k_norm_g, mu, w_decay0, w_decay2, a0, a2, g2,
              S, da_width, rw_width, tm=512):
    N, D = x2.shape
    n_gate = b_gate.shape[0]
    rw_cols = mu.shape[0]
    tpb = S // tm
    d = DA_HEAD_DIM
    inv_freq = 1.0 / (ROPE_THETA ** (jnp.arange(0, d, 2, dtype=F32) / d))
    invf = jnp.tile(inv_freq, LANES // (d // 2)).reshape(1, LANES)
    dl, al = w_decay2.shape[0], a2.shape[0]
    assert dl + al == LANES and g2.shape[0] == LANES
    w2p = jnp.zeros((LANES, rw_width), F32).at[:dl].set(w_decay2)
    a2p = jnp.zeros((LANES, rw_width), F32).at[dl:].set(a2)
    kern = functools.partial(_mixer_in_kernel, da_width=da_width, rw_cols=rw_cols, rw_width=rw_width,
                             q_scale=d ** -0.5 * math.log2(math.e),
                             tiles_per_seq=tpb)
    row = lambda w: pl.BlockSpec((tm, w), lambda i: (i, 0))
    vec = lambda n: pl.BlockSpec((1, n), lambda i: (0, 0))
    mat = pl.BlockSpec((LANES, rw_width), lambda i: (0, 0))
    f32 = lambda w: jax.ShapeDtypeStruct((N, w), F32)
    bf16 = lambda w: jax.ShapeDtypeStruct((N, w), BF16)
    return pl.pallas_call(
        kern,
        grid=(N // tm,),
        in_specs=[row(D),
                  pl.BlockSpec((8, D), lambda i: (jnp.maximum(i * (tm // 8) - 1, 0), 0)),
                  pl.BlockSpec((None, 6, D), lambda i: (i // tpb, 0, 0)),
                  vec(D),
                  pl.BlockSpec(w_cat.shape, lambda i: (0, 0)),
                  vec(n_gate),
                  pl.BlockSpec((tm, 1), lambda i: (i, 0)),
                  vec(LANES), vec(LANES), vec(LANES),
                  vec(rw_cols), vec(rw_width), mat, vec(rw_width), mat, mat],
        out_specs=[row(da_width)] * 3 + [row(rw_width)] * 6 + [row(n_gate)],
        out_shape=[bf16(da_width)] * 3 + [f32(rw_width)] * 6 + [bf16(n_gate)],
        compiler_params=_cp("parallel"),
        name="mixer_in",
    )(x2, x2, mod, norm1_g.reshape(1, D), w_cat, b_gate.reshape(1, n_gate), pos2, invf,
      jnp.tile(q_norm_g, 2).reshape(1, LANES), jnp.tile(k_norm_g, 2).reshape(1, LANES),
      mu.reshape(1, rw_cols), w_decay0.reshape(1, rw_width), w2p, a0.reshape(1, rw_width), a2p, g2)


def _attn_kernel(flag_ref, q_ref, k_ref, v_ref, vt_ref, lam_ref, sg_ref, sgc_ref, o_ref,
                 qz_ref, m_ref, l_ref, acc_ref, lpt_ref, acct_ref, *, tq, lambda_init):
    i = pl.program_id(2)
    lane = lax.broadcasted_iota(I32, (tq, LANES), 1)
    q = q_ref[...]
    zero = jnp.zeros_like(q)
    qz_ref[0:tq, :] = jnp.where(lane < DA_HEAD_DIM, q, zero)
    qz_ref[tq:, :] = jnp.where(lane >= DA_HEAD_DIM, q, zero)
    bounded = flag_ref[0] == 1
    lv = lam_ref[...]
    lam = (jnp.exp(jnp.sum(lv[0:1] * lv[1:2], keepdims=True))
           - jnp.exp(jnp.sum(lv[2:3] * lv[3:4], keepdims=True)) + lambda_init)

    def run(step):
        def body(j, carry):
            step(j, False)
            return carry
        lax.fori_loop(0, i, body, 0)
        step(i, True)

    def probs_t(keys, queries, masked, q_mask):
        st = _nt(keys, queries)
        if masked:
            row = lax.broadcasted_iota(I32, st.shape, 0)
            col = lax.broadcasted_iota(I32, st.shape, 1)
            st = jnp.where((row // CHUNK) <= ((col & q_mask) // CHUNK), st, NEG)
        pt = jnp.exp2(st)
        part = pt[0:8, :]
        for g in range(1, st.shape[0] // 8):
            part = part + pt[8 * g:8 * g + 8, :]
        return pt.astype(BF16), part

    def plain_step(j, masked):
        off = pl.multiple_of(j * tq, tq)
        if not masked:
            pt, part = probs_t(k_ref[pl.ds(off, tq), :], qz_ref[...], False, 0)
            lpt_ref[...] += part
            acct_ref[...] += jnp.dot(vt_ref[:, pl.ds(off, tq)], pt, preferred_element_type=F32)
            return
        hq = tq // 2
        pt, part = probs_t(k_ref[pl.ds(off, hq), :], qz_ref[...], True, tq - 1)
        lpt_ref[...] += part
        acct_ref[...] += jnp.dot(vt_ref[:, pl.ds(off, hq)], pt, preferred_element_type=F32)
        off2 = pl.multiple_of(off + hq, hq)
        late_q = jnp.concatenate([qz_ref[hq:tq, :], qz_ref[tq + hq:, :]], axis=0)
        pt, part = probs_t(k_ref[pl.ds(off2, hq), :], late_q, True, hq - 1)
        upd = jnp.dot(vt_ref[:, pl.ds(off2, hq)], pt, preferred_element_type=F32)
        for m in range(2):
            cols = slice(m * tq + hq, (m + 1) * tq)
            lpt_ref[:, cols] += part[:, m * hq:(m + 1) * hq]
            acct_ref[:, cols] += upd[:, m * hq:(m + 1) * hq]

    @pl.when(bounded)
    def _():
        lpt_ref[...] = jnp.zeros_like(lpt_ref)
        acct_ref[...] = jnp.zeros_like(acct_ref)

        def two_steps(p, carry):
            plain_step(2 * p, False)
            plain_step(2 * p + 1, False)
            return carry

        lax.fori_loop(0, lax.shift_right_logical(i, 1), two_steps, 0)

        @pl.when((i & 1) == 1)
        def _():
            plain_step(i - 1, False)
            plain_step(i, True)

        @pl.when((i & 1) == 0)
        def _():
            plain_step(i, True)

        lsum = jnp.sum(lpt_ref[...], axis=0, keepdims=True)
        ot = acct_ref[:, 0:tq] / lsum[:, 0:tq] - lam * (acct_ref[:, tq:] / lsum[:, tq:])
        ot = ot * lax.rsqrt(jnp.mean(ot * ot, axis=0, keepdims=True) + SUBLN_EPS) * sgc_ref[...]
        o_ref[...] = (ot * (1.0 - lambda_init)).T.astype(o_ref.dtype)

    def online_step(j, masked):
        off = pl.multiple_of(j * tq, tq)
        s = _nt(qz_ref[...], k_ref[pl.ds(off, tq), :])
        if masked:
            row = lax.broadcasted_iota(I32, s.shape, 0)
            col = lax.broadcasted_iota(I32, s.shape, 1)
            s = jnp.where((col // CHUNK) <= ((row & (tq - 1)) // CHUNK), s, NEG)
        m_old = m_ref[...]
        m_new = jnp.maximum(m_old, jnp.max(s, axis=-1, keepdims=True))
        alpha = jnp.exp2(m_old - m_new)
        pr = jnp.exp2(s - m_new)
        l_ref[...] = alpha * l_ref[...] + jnp.sum(pr, axis=-1, keepdims=True)
        acc_ref[...] = alpha * acc_ref[...] + jnp.dot(pr.astype(BF16), v_ref[pl.ds(off, tq), :],
                                                      preferred_element_type=F32)
        m_ref[...] = m_new

    @pl.when(jnp.logical_not(bounded))
    def _():
        m_ref[...] = jnp.full_like(m_ref, NEG)
        l_ref[...] = jnp.zeros_like(l_ref)
        acc_ref[...] = jnp.zeros_like(acc_ref)
        run(online_step)
        o = acc_ref[0:tq, :] / l_ref[0:tq, :] - lam * (acc_ref[tq:, :] / l_ref[tq:, :])
        o = o * lax.rsqrt(jnp.mean(o * o, axis=-1, keepdims=True) + SUBLN_EPS) * sg_ref[...]
        o_ref[...] = (o * (1.0 - lambda_init)).astype(o_ref.dtype)


def _diff_attention(qn, kn, v, score_bound, lam_vecs, subln_g, B, S, lambda_init, tq=1024):
    W = qn.shape[1]
    H = W // LANES
    tq = min(tq, S)
    q3 = qn.reshape(B, S, W)
    k3 = kn.reshape(B, S, W)
    v3 = v.reshape(B, S, W)
    vt3 = v3.transpose(0, 2, 1)
    flag = (score_bound <= MAX_PLAIN_SCORE).astype(I32).reshape(1)
    qblk = pl.BlockSpec((None, tq, LANES), lambda b, h, i, f: (b, i, h))
    kvblk = pl.BlockSpec((None, S, LANES), lambda b, h, i, f: (b, 0, h))
    grid_spec = pltpu.PrefetchScalarGridSpec(
        num_scalar_prefetch=1,
        grid=(B, H, S // tq),
        in_specs=[qblk, kvblk, kvblk,
                  pl.BlockSpec((None, LANES, S), lambda b, h, i, f: (b, h, 0)),
                  pl.BlockSpec((4, DA_HEAD_DIM), lambda b, h, i, f: (0, 0)),
                  pl.BlockSpec((1, LANES), lambda b, h, i, f: (0, 0)),
                  pl.BlockSpec((LANES, 1), lambda b, h, i, f: (0, 0))],
        out_specs=qblk,
        scratch_shapes=[pltpu.VMEM((2 * tq, LANES), BF16),
                        pltpu.VMEM((2 * tq, 1), F32),
                        pltpu.VMEM((2 * tq, 1), F32),
                        pltpu.VMEM((2 * tq, LANES), F32),
                        pltpu.VMEM((8, 2 * tq), F32),
                        pltpu.VMEM((LANES, 2 * tq), F32)],
    )
    out = pl.pallas_call(
        functools.partial(_attn_kernel, tq=tq, lambda_init=lambda_init),
        grid_spec=grid_spec,
        out_shape=jax.ShapeDtypeStruct((B, S, W), BF16),
        compiler_params=_cp("parallel", "parallel", "arbitrary"),
        name="diff_attn",
    )(flag, q3, k3, v3, vt3, lam_vecs, subln_g.reshape(1, LANES), subln_g.reshape(LANES, 1))
    return out.reshape(B * S, W)


def _stackmask(m):
    lane = lax.broadcasted_iota(I32, m.shape, 1)
    z = jnp.zeros_like(m)
    return jnp.concatenate([jnp.where(lane < RWKV_HEAD, m, z), jnp.where(lane >= RWKV_HEAD, m, z)], axis=0)


def _pair_sum(x, first):
    s1 = jnp.sum(jnp.where(first, x, 0.0), axis=-1, keepdims=True)
    s2 = jnp.sum(jnp.where(first, 0.0, x), axis=-1, keepdims=True)
    return jnp.where(first, s1, s2)


def _rwkv_scan_kernel(r_ref, k_ref, v_ref, a_ref, ld_ref, g_ref, kk_ref, ka_ref, rk_ref, lnw_ref, lnb_ref,
                      o_ref, s_ref, *, L):
    nb, tb, W = r_ref.shape
    tm = nb * tb
    n_chunks = tm // L
    per_batch = tb // L
    n_pairs = W // LANES
    hd = RWKV_HEAD
    bf = lambda t: t.astype(BF16)
    rows2d = lambda ref: ref[...].reshape(tm, W)

    @pl.when(pl.program_id(0) == 0)
    def _():
        s_ref[...] = jnp.zeros_like(s_ref)

    row = lax.broadcasted_iota(I32, (tm, tm), 0)
    col = lax.broadcasted_iota(I32, (tm, tm), 1)
    tri = jnp.where(jnp.logical_and(col <= row, (col // L) == (row // L)), 1.0, 0.0).astype(BF16)
    ld = rows2d(ld_ref)
    ld_hi = bf(ld)
    rem = ld - ld_hi.astype(F32)
    ld_mid = bf(rem)
    ld_lo = bf(rem - ld_mid.astype(F32))
    c = (jnp.dot(tri, ld_hi, preferred_element_type=F32) + jnp.dot(tri, ld_mid, preferred_element_type=F32)
         + jnp.dot(tri, ld_lo, preferred_element_type=F32))
    ec = jnp.exp(c)
    eci = jnp.exp(-c)
    ecm = jnp.exp(c - ld)
    r = rows2d(r_ref)
    k = rows2d(k_ref)
    v = rows2d(v_ref)
    a = rows2d(a_ref)
    kkr = k * kk_ref[...]
    kmod = k * (1.0 + (a - 1.0) * ka_ref[...])
    brk = r * kmod * rk_ref[...]

    lane = lax.broadcasted_iota(I32, (L, LANES), 1)
    rowl = lax.broadcasted_iota(I32, (L, LANES), 0)
    first = lane < hd
    lane_h = lane & (hd - 1)
    strict = lane_h < rowl
    incl = lane_h <= rowl
    eye = jnp.where(lane_h == rowl, 1.0, 0.0)

    chains = [(ch, p) for ch in range(n_chunks) for p in range(n_pairs)]
    rsl = lambda ch: slice(ch * L, (ch + 1) * L)
    csl = lambda p: slice(p * LANES, (p + 1) * LANES)
    fdot = lambda x, y: jnp.dot(x, y, preferred_element_type=F32)
    at, bt, kt, rt, vh, g_l = {}, {}, {}, {}, {}, {}
    for c_ in chains:
        ch, p = c_
        rs, cs = rsl(ch), csl(p)
        kkh = kkr[rs, cs]
        kkh = kkh / jnp.maximum(jnp.sqrt(_pair_sum(kkh * kkh, first)), 1e-12)
        vh[c_] = v[rs, cs]
        g_l[c_] = ec[ch * L + L - 1:ch * L + L, cs]
        at[c_] = -kkh * ecm[rs, cs]
        bt[c_] = kkh * a[rs, cs] * eci[rs, cs]
        kt[c_] = kmod[rs, cs] * eci[rs, cs]
        rt[c_] = r[rs, cs] * ec[rs, cs]
    gm = {c_: _nt(bf(jnp.concatenate([at[c_], rt[c_]], axis=0)),
                  jnp.concatenate([_stackmask(bf(bt[c_])), _stackmask(bf(kt[c_]))], axis=0)) for c_ in chains}
    a_ab = {c_: jnp.where(strict, gm[c_][:L, :LANES], 0.0) for c_ in chains}
    vsm = {c_: _stackmask(bf(vh[c_])) for c_ in chains}
    akv = {c_: fdot(bf(jnp.concatenate([jnp.where(strict, gm[c_][:L, LANES:], 0.0),
                                        jnp.where(incl, gm[c_][L:, LANES:], 0.0)], axis=0)), vsm[c_])
           for c_ in chains}
    cmat = {c_: akv[c_][:L] for c_ in chains}
    t_inv = {c_: eye + a_ab[c_] for c_ in chains}
    pw = {c_: bf(a_ab[c_]) for c_ in chains}
    for _ in range(int(math.log2(L)) - 1):
        pw = {c_: bf(fdot(pw[c_], _stackmask(pw[c_]))) for c_ in chains}
        t_inv = {c_: t_inv[c_] + fdot(pw[c_], _stackmask(bf(t_inv[c_]))) for c_ in chains}
    zz = {c_: fdot(bf(t_inv[c_]), jnp.concatenate([_stackmask(bf(at[c_])), _stackmask(bf(cmat[c_]))], axis=1))
          for c_ in chains}
    qy = {c_: fdot(bf(jnp.where(incl, gm[c_][L:, :LANES], 0.0)),
                   jnp.concatenate([_stackmask(bf(zz[c_][:, :LANES])), _stackmask(bf(zz[c_][:, LANES:]))], axis=1))
          for c_ in chains}
    y0 = {c_: qy[c_][:, LANES:] + akv[c_][L:] for c_ in chains}
    qa = {c_: bf(jnp.concatenate([rt[c_] + qy[c_][:, :LANES], zz[c_][:, :LANES]], axis=0)) for c_ in chains}
    bkg = {c_: bf(jnp.concatenate([bt[c_] * g_l[c_], kt[c_] * g_l[c_]], axis=0)) for c_ in chains}

    lane_s = lax.broadcasted_iota(I32, (hd, LANES), 1)
    heads = [(b, p) for b in range(nb) for p in range(n_pairs)]
    sp = {bp: s_ref[bp[0] * n_pairs + bp[1]] for bp in heads}
    for j in range(per_batch):
        chunk = lambda b: b * per_batch + j
        yw = {(b, p): _nt(qa[chunk(b), p], _stackmask(bf(sp[b, p]))) for b, p in heads}
        upd = {(b, p): _tn(bf(jnp.concatenate([yw[b, p][L:] + zz[chunk(b), p][:, LANES:], vh[chunk(b), p]],
                                              axis=0)), bkg[chunk(b), p]) for b, p in heads}
        for b, p in heads:
            ch, cs, ts = chunk(b), csl(p), rsl(j)
            sp[b, p] = sp[b, p] * g_l[ch, p] + jnp.where(lane_s < hd, upd[b, p][:hd], upd[b, p][hd:])
            y = yw[b, p][:L] + y0[ch, p]
            mean = _pair_sum(y, first) * (1.0 / hd)
            yc = y - mean
            var = _pair_sum(yc * yc, first) * (1.0 / hd)
            yn = yc * lax.rsqrt(var + GN_EPS) * lnw_ref[:, cs] + lnb_ref[:, cs]
            bonus = _pair_sum(brk[rsl(ch), cs], first) * vh[ch, p]
            o_ref[b, ts, cs] = ((yn + bonus) * g_ref[b, ts, cs]).astype(o_ref.dtype)
    for b, p in heads:
        s_ref[b * n_pairs + p] = sp[b, p]


def _rwkv_scan(r, k, v, a, ld, g, k_k, k_a, r_k, ln_w, ln_b, L=RWKV_CHUNK, tm=128):
    B, S, W = r.shape
    seq = pl.BlockSpec((B, tm, W), lambda c: (0, c, 0))
    vec = pl.BlockSpec((1, W), lambda c: (0, 0))
    return pl.pallas_call(
        functools.partial(_rwkv_scan_kernel, L=L),
        grid=(S // tm,),
        in_specs=[seq] * 6 + [vec] * 5,
        out_specs=seq,
        out_shape=jax.ShapeDtypeStruct((B, S, W), BF16),
        scratch_shapes=[pltpu.VMEM((B * (W // LANES), RWKV_HEAD, LANES), F32)],
        compiler_params=_cp("arbitrary"),
        name="rwkv_scan",
    )(r, k, v, a, ld, g, k_k.reshape(1, W), k_a.reshape(1, W), r_k.reshape(1, W),
      ln_w.reshape(1, W), ln_b.reshape(1, W))


def _post_kernel(attn_ref, rw_ref, gate_ref, x_ref, mod_ref, wa_ref, wb_ref, wo_ref, g2_ref, wrh_ref, wrm_ref,
                 sug_ref, sd_ref, base_ref, h2p_ref, lg_ref):
    D = x_ref.shape[1]
    ya = jnp.dot(attn_ref[...], wa_ref[...], preferred_element_type=F32)
    yb = jnp.dot(rw_ref[...], wb_ref[...], preferred_element_type=F32)
    m = gate_ref[:, 0:D] * ya + gate_ref[:, D:] * yb
    x1 = x_ref[...] + mod_ref[2:3, :] * jnp.dot(m.astype(BF16), wo_ref[...], preferred_element_type=F32)
    y = x1 * lax.rsqrt(jnp.mean(x1 * x1, axis=-1, keepdims=True) + NORM_EPS) * g2_ref[...]
    h2 = y * (1.0 + mod_ref[4:5, :]) + mod_ref[3:4, :]
    hb = h2.astype(BF16)
    hm = (h2 - hb.astype(F32)).astype(BF16)
    lg_ref[...] = _nt(wrh_ref[...], hb) + _nt(wrh_ref[...], hm) + _nt(wrm_ref[...], hb)
    packed = _pack_halves(hb.astype(F32))
    half = packed.shape[1] // 2
    h2p_ref[0] = packed[:, :half]
    h2p_ref[1] = packed[:, half:]
    F = sd_ref.shape[0]
    gu = jnp.dot(hb, sug_ref[...], preferred_element_type=F32)
    shared = jnp.dot((_silu(gu[:, :F]) * gu[:, F:]).astype(BF16), sd_ref[...], preferred_element_type=F32)
    base_ref[...] = x1 + mod_ref[5:6, :] * shared


def _post(attn, rw, gate, x2, mod, wa, wb, wo, norm2_g, w_router_t, sug, sd, S, tm=512):
    N, D = x2.shape
    E = w_router_t.shape[0]
    wr_hi = w_router_t.astype(BF16)
    wr_mid = (w_router_t - wr_hi.astype(F32)).astype(BF16)
    tpb = S // tm
    row = lambda w: pl.BlockSpec((tm, w), lambda i: (i, 0))
    full = lambda a: pl.BlockSpec(a.shape, lambda i: (0, 0))
    return pl.pallas_call(
        _post_kernel,
        grid=(N // tm,),
        in_specs=[row(attn.shape[1]), row(rw.shape[1]), row(gate.shape[1]), row(D),
                  pl.BlockSpec((None, 6, D), lambda i: (i // tpb, 0, 0)),
                  full(wa), full(wb), full(wo), pl.BlockSpec((1, D), lambda i: (0, 0)), full(wr_hi), full(wr_mid),
                  full(sug), full(sd)],
        out_specs=[row(D), pl.BlockSpec((2, tm, D // 4), lambda i: (0, i, 0)), pl.BlockSpec((E, tm), lambda i: (0, i))],
        out_shape=[jax.ShapeDtypeStruct((N, D), F32), jax.ShapeDtypeStruct((2, N, D // 4), U32),
                   jax.ShapeDtypeStruct((E, N), F32)],
        compiler_params=_cp("parallel"),
        name="post_mixer",
    )(attn, rw, gate, x2, mod, wa, wb, wo, norm2_g.reshape(1, D), wr_hi, wr_mid, sug, sd)


def _first_argmax(x, idx, sentinel):
    m = jnp.max(x, axis=0, keepdims=True)
    return m, jnp.min(jnp.where(x == m, idx, sentinel), axis=0, keepdims=True)


def _route_kernel(lg_ref, bias_ref, e_ref, w_ref):
    E, T = lg_ref.shape
    gsz = E // N_GROUPS
    scores = _sigmoid(lg_ref[...])
    biased = scores + bias_ref[...]
    ig = lax.broadcasted_iota(I32, (gsz, T), 0).astype(F32)
    gs = []
    for g in range(N_GROUPS):
        blk = biased[g * gsz:(g + 1) * gsz, :]
        m1, i1 = _first_argmax(blk, ig, float(gsz))
        m2 = jnp.max(jnp.where(ig == i1, -jnp.inf, blk), axis=0, keepdims=True)
        gs.append(m1 + m2)
    gsc = jnp.concatenate(gs, axis=0)
    i8 = lax.broadcasted_iota(I32, (N_GROUPS, T), 0).astype(F32)
    chosen = jnp.zeros((N_GROUPS, T), F32)
    for _ in range(TOPK_GROUPS):
        _, gi = _first_argmax(gsc, i8, float(N_GROUPS))
        hit = i8 == gi
        chosen = jnp.where(hit, 1.0, chosen)
        gsc = jnp.where(hit, -jnp.inf, gsc)
    masked = jnp.concatenate(
        [jnp.where(chosen[g:g + 1, :] > 0.0, biased[g * gsz:(g + 1) * gsz, :], -jnp.inf)
         for g in range(N_GROUPS)], axis=0)
    ie = lax.broadcasted_iota(I32, (E, T), 0).astype(F32)
    idxs, wts = [], []
    for _ in range(TOP_K):
        _, ei = _first_argmax(masked, ie, float(E))
        hit = ie == ei
        idxs.append(ei)
        wts.append(jnp.sum(jnp.where(hit, scores, 0.0), axis=0, keepdims=True))
        masked = jnp.where(hit, -jnp.inf, masked)
    wt = jnp.concatenate(wts, axis=0)
    e_ref[...] = jnp.concatenate(idxs, axis=0).astype(I32)
    w_ref[...] = wt / jnp.sum(wt, axis=0, keepdims=True) * ROUTED_SCALE


def _route(logits_t, router_bias, T=512):
    E, N = logits_t.shape
    blk = pl.BlockSpec((TOP_K, T), lambda i: (0, i))
    return pl.pallas_call(
        _route_kernel,
        grid=(N // T,),
        in_specs=[pl.BlockSpec((E, T), lambda i: (0, i)), pl.BlockSpec((E, 1), lambda i: (0, 0))],
        out_specs=[blk, blk],
        out_shape=[jax.ShapeDtypeStruct((TOP_K, N), I32), jax.ShapeDtypeStruct((TOP_K, N), F32)],
        compiler_params=_cp("parallel"),
        name="route",
    )(logits_t, router_bias.reshape(E, 1))


def _rank_kernel(e_ref, rank_ref, cnt_ref, carry_ref, *, n_experts):
    T = e_ref.shape[1]

    @pl.when(pl.program_id(0) == 0)
    def _():
        carry_ref[...] = jnp.zeros_like(carry_ref)

    ie = lax.broadcasted_iota(I32, (n_experts, T), 0)
    e = e_ref[...]
    hits = [ie == e[kk:kk + 1, :] for kk in range(TOP_K)]
    onehot = jnp.zeros((n_experts, T), F32)
    for hsel in hits:
        onehot = onehot + hsel.astype(F32)
    tr = lax.broadcasted_iota(I32, (T, T), 0)
    tc = lax.broadcasted_iota(I32, (T, T), 1)
    before = (tr < tc).astype(BF16)
    base = _bdot(onehot, before) + carry_ref[:, 0:1]
    rank_ref[...] = jnp.concatenate(
        [jnp.sum(jnp.where(hsel, base, 0.0), axis=0, keepdims=True) for hsel in hits], axis=0).astype(I32)
    carry_ref[...] = carry_ref[...] + jnp.sum(onehot, axis=1, keepdims=True)
    cnt_ref[...] = carry_ref[...]


def _ranks(eidx_t, n_experts, T=512):
    N = eidx_t.shape[1]
    T = min(T, N)
    blk = pl.BlockSpec((TOP_K, T), lambda i: (0, i))
    cnt = pl.BlockSpec((n_experts, LANES), lambda i: (0, 0))
    return pl.pallas_call(
        functools.partial(_rank_kernel, n_experts=n_experts),
        grid=(N // T,),
        in_specs=[blk],
        out_specs=[blk, cnt],
        out_shape=[jax.ShapeDtypeStruct((TOP_K, N), I32), jax.ShapeDtypeStruct((n_experts, LANES), F32)],
        scratch_shapes=[pltpu.VMEM((n_experts, LANES), F32)],
        compiler_params=_cp("arbitrary"),
        name="ranks",
    )(eidx_t)


def _dest_kernel(e_ref, rank_ref, start_ref, d_ref):
    E = start_ref.shape[0]
    T = e_ref.shape[1]
    ie = lax.broadcasted_iota(I32, (E, T), 0)
    e = e_ref[...]
    start = start_ref[:, 0:1]
    rows = [jnp.sum(jnp.where(ie == e[kk:kk + 1, :], start, 0.0), axis=0, keepdims=True) for kk in range(TOP_K)]
    d_ref[...] = jnp.concatenate(rows, axis=0).astype(I32) + rank_ref[...]


def _dests(eidx_t, rank_t, pstart, T=2048):
    N = eidx_t.shape[1]
    T = min(T, N)
    E = pstart.shape[0]
    blk = pl.BlockSpec((TOP_K, T), lambda i: (0, i))
    return pl.pallas_call(
        _dest_kernel,
        grid=(N // T,),
        in_specs=[blk, blk, pl.BlockSpec((E, LANES), lambda i: (0, 0))],
        out_specs=blk,
        out_shape=jax.ShapeDtypeStruct((TOP_K, N), I32),
        compiler_params=_cp("parallel"),
        name="dests",
    )(eidx_t, rank_t, jnp.broadcast_to(pstart.astype(F32)[:, None], (E, LANES)))


def _expert_kernel(us_ref, ps_ref, x_hbm, wug_hbm, wd_hbm, y_hbm,
                   wug_f32, wd_f32, wug_bf, wd_bf, xbuf, ybuf, cnt_ref, wsem, xsem, ysem, *, n_rows):
    e = pl.program_id(0)
    n_experts = pl.num_programs(0)
    _, _, R, Ch = xbuf.shape
    C = 2 * Ch
    F = wd_hbm.shape[1]

    def w_copies(expert, slot):
        return (pltpu.make_async_copy(wug_hbm.at[expert], wug_f32.at[slot], wsem.at[slot]),
                pltpu.make_async_copy(wd_hbm.at[expert], wd_f32.at[slot], wsem.at[slot]))

    def x_copy(slot, half, row):
        return pltpu.make_async_copy(x_hbm.at[half, pl.ds(row, R)], xbuf.at[slot, half], xsem.at[slot])

    def y_copy(slot, half, row):
        return pltpu.make_async_copy(ybuf.at[slot, half], y_hbm.at[half, pl.ds(row, R)], ysem.at[slot])

    def start(copy, slot, row):
        for half in range(2):
            copy(slot, half, row).start()

    def wait(copy, slot):
        for half in range(2):
            copy(slot, half, 0).wait()

    @pl.when(e == 0)
    def _():
        for ahead in range(W_SLOTS - 1):
            for cp in w_copies(ahead, ahead):
                cp.start()
        cnt_ref[0] = 0
        for t in range(X_AHEAD):
            start(x_copy, t, t * R)
        ybuf[...] = jnp.zeros_like(ybuf)
        for slot in range(Y_SLOTS):
            start(y_copy, slot, n_rows + slot * R)

    @pl.when(e + (W_SLOTS - 1) < n_experts)
    def _():
        for cp in w_copies(e + (W_SLOTS - 1), lax.rem(e + (W_SLOTS - 1), W_SLOTS)):
            cp.start()

    wslot = lax.rem(e, W_SLOTS)
    for cp in w_copies(e, wslot):
        cp.wait()
    wug_bf[...] = wug_f32[wslot].astype(BF16)
    wd_bf[...] = wd_f32[wslot].astype(BF16)
    n_valid = us_ref[e + 1] - us_ref[e]
    pbase = ps_ref[e]
    n_tiles = lax.shift_right_logical(n_valid + (R - 1), int(math.log2(R)))

    def make_tile(nb):
        rowid = lax.broadcasted_iota(I32, (nb * R, C), 0)

        def tile(blk0):
            n_done = cnt_ref[0]
            xslots = [lax.rem(n_done + b, X_SLOTS) for b in range(nb)]
            for b in range(nb):
                ahead = n_done + X_AHEAD + b
                start(x_copy, lax.rem(ahead, X_SLOTS), pl.multiple_of(ahead * R, R))
            for b in range(nb):
                wait(x_copy, xslots[b])
            x = jnp.concatenate([jnp.concatenate([xbuf[xs, 0], xbuf[xs, 1]], axis=1) for xs in xslots], axis=0)
            x = jnp.where(rowid < n_valid - blk0 * R, x, jnp.uint32(0))
            lo, hi = _unpack_halves(x)
            gu = (jnp.dot(lo.astype(BF16), wug_bf[0:C, :], preferred_element_type=F32)
                  + jnp.dot(hi.astype(BF16), wug_bf[C:, :], preferred_element_type=F32))
            hid = (_silu(gu[:, :F]) * gu[:, F:]).astype(BF16)
            y = jnp.dot(hid, wd_bf[...], preferred_element_type=F32)
            packed = _pack_halves(y.astype(BF16).astype(F32))
            for b in range(nb):
                slot = lax.rem(n_done + b, Y_SLOTS)
                wait(y_copy, slot)
                ybuf[slot, 0] = packed[b * R:(b + 1) * R, :Ch]
                ybuf[slot, 1] = packed[b * R:(b + 1) * R, Ch:]
                start(y_copy, slot, pl.multiple_of(pbase + (blk0 + b) * R, R))
            cnt_ref[0] = n_done + nb

        return tile

    tiles = {nb: make_tile(nb) for nb in range(1, TILE_BLOCKS + 1)}
    quads = lax.shift_right_logical(n_tiles, 2)
    left = n_tiles & 3
    five = jnp.logical_and(left == 1, quads >= 1)
    quads = quads - jnp.where(five, 1, 0)

    def quad_tiles(j, carry):
        tiles[4](4 * j)
        return carry

    lax.fori_loop(0, quads, quad_tiles, 0)

    @pl.when(five)
    def _():
        tiles[5](n_tiles - 5)

    for nb in (1, 2, 3):
        @pl.when(jnp.logical_and(left == nb, jnp.logical_not(five)))
        def _(nb=nb):
            tiles[nb](n_tiles - nb)

    @pl.when(e == pl.num_programs(0) - 1)
    def _():
        for t in range(X_AHEAD):
            wait(x_copy, lax.rem(cnt_ref[0] + t, X_SLOTS))
        for slot in range(Y_SLOTS):
            wait(y_copy, slot)
        ybuf[0] = jnp.zeros((2, R, Ch), U32)
        first = lax.shift_right_logical(pbase + n_tiles * R, int(math.log2(R)))
        n_left = n_rows // R - first

        def fill(t, carry):
            start(y_copy, 0, pl.multiple_of((first + t) * R, R))
            return carry

        def drain(t, carry):
            wait(y_copy, 0)
            return carry

        lax.fori_loop(0, n_left, fill, 0)
        lax.fori_loop(0, n_left, drain, 0)


def _experts(ustart, pstart, xg, w_ug, w_d, n_rows, R):
    _, _, Ch = xg.shape
    E, D, F2 = w_ug.shape
    F = w_d.shape[1]
    grid_spec = pltpu.PrefetchScalarGridSpec(
        num_scalar_prefetch=2,
        grid=(E,),
        in_specs=[pl.BlockSpec(memory_space=pl.ANY)] * 3,
        out_specs=pl.BlockSpec(memory_space=pl.ANY),
        scratch_shapes=[pltpu.VMEM((W_SLOTS, D, F2), F32), pltpu.VMEM((W_SLOTS, F, D), F32),
                        pltpu.VMEM((D, F2), BF16), pltpu.VMEM((F, D), BF16),
                        pltpu.VMEM((X_SLOTS, 2, R, Ch), U32), pltpu.VMEM((Y_SLOTS, 2, R, Ch), U32), pltpu.SMEM((1,), I32),
                        pltpu.SemaphoreType.DMA((W_SLOTS,)), pltpu.SemaphoreType.DMA((X_SLOTS,)),
                        pltpu.SemaphoreType.DMA((Y_SLOTS,))],
    )
    return pl.pallas_call(
        functools.partial(_expert_kernel, n_rows=n_rows),
        grid_spec=grid_spec,
        out_shape=jax.ShapeDtypeStruct((2, n_rows + Y_SLOTS * R, Ch), U32),
        compiler_params=_cp("arbitrary"),
        name="experts",
    )(ustart, pstart, xg, w_ug, w_d)


def _sc_scatter_rows(src, idx, n_rows):
    H, N, C = src.shape
    K = idx.shape[0]
    per_row = N // SC_WINDOW
    mesh = plsc.VectorSubcoreMesh(core_axis_name="c", subcore_axis_name="s")

    @functools.partial(pl.kernel, out_type=jax.ShapeDtypeStruct((H, n_rows, C), src.dtype), mesh=mesh,
                       scratch_types=[])
    def scatter_kernel(x_hbm, i_hbm, o_hbm):
        for h in range(H):
            def body(x_vmem, i_vmem):
                for k in range(K):
                    pltpu.sync_copy(x_vmem, o_hbm.at[h].at[i_vmem.at[k]])

            pltpu.emit_pipeline(
                body,
                grid=(per_row,),
                in_specs=[pl.BlockSpec((SC_WINDOW, C), lambda i: (i, 0)),
                          pl.BlockSpec((K, SC_WINDOW), lambda i: (0, i))],
                out_specs=[],
                core_axis_name=("c", "s"),
                dimension_semantics=(pltpu.PARALLEL,),
            )(x_hbm.at[h], i_hbm)

    return scatter_kernel(src, idx)


def _sc_gather_rows(src, idx):
    H, _, C = src.shape
    K, N = idx.shape
    per_row = N // SC_WINDOW
    mesh = plsc.VectorSubcoreMesh(core_axis_name="c", subcore_axis_name="s")

    @functools.partial(pl.kernel, out_type=jax.ShapeDtypeStruct((H, K * N, C), src.dtype), mesh=mesh,
                       scratch_types=[])
    def gather_kernel(x_hbm, i_hbm, o_hbm):
        for h in range(H):
            def body(i_vmem, o_vmem):
                pltpu.sync_copy(x_hbm.at[h].at[i_vmem.at[0]], o_vmem)

            pltpu.emit_pipeline(
                body,
                grid=(K * per_row,),
                in_specs=[pl.BlockSpec((1, SC_WINDOW), lambda i: (i // per_row, i % per_row))],
                out_specs=[pl.BlockSpec((SC_WINDOW, C), lambda i: (i, 0))],
                core_axis_name=("c", "s"),
                dimension_semantics=(pltpu.PARALLEL,),
            )(i_hbm, o_hbm.at[h])

    return gather_kernel(src, idx)


def _combine_kernel(*refs):
    y_refs = refs[:2 * TOP_K]
    w_ref, base_ref, mod_ref, o_ref = refs[2 * TOP_K:]
    T = base_ref.shape[0]
    tr = lax.broadcasted_iota(I32, (T, T), 0)
    tc = lax.broadcasted_iota(I32, (T, T), 1)
    wcol = _nt((tr == tc).astype(F32), w_ref[...], precision=HI)
    acc = [None] * 4
    for kk in range(TOP_K):
        wk = wcol[:, kk:kk + 1]
        for half in range(2):
            lo, hi = _unpack_halves(y_refs[2 * kk + half][...])
            for q, val in ((half, lo), (2 + half, hi)):
                acc[q] = val * wk if acc[q] is None else acc[q] + val * wk
    o_ref[...] = base_ref[...] + mod_ref[5:6, :] * jnp.concatenate(acc, axis=1)


def _combine(yg, w_t, base, mod, S, T=512):
    N, D = base.shape
    C = yg.shape[2]
    tpb = S // T
    n_tiles = N // T
    row = pl.BlockSpec((T, D), lambda i: (i, 0))
    piece = lambda kk, half: pl.BlockSpec((None, T, C), lambda i: (half, kk * n_tiles + i, 0))
    return pl.pallas_call(
        _combine_kernel,
        grid=(n_tiles,),
        in_specs=[piece(kk, half) for kk in range(TOP_K) for half in range(2)] + [
            pl.BlockSpec((TOP_K, T), lambda i: (0, i)),
            row,
            pl.BlockSpec((None, 6, D), lambda i: (i // tpb, 0, 0))],
        out_specs=row,
        out_shape=jax.ShapeDtypeStruct((N, D), F32),
        compiler_params=_cp("parallel"),
        name="combine",
    )(*([yg] * (2 * TOP_K)), w_t, base, mod)


def _layer(x, c, positions, layer_idx, w_ada, b_ada, norm1_g, w_in, w_gate, b_gate,
           q_norm_g, k_norm_g, lambda_q1, lambda_k1, lambda_q2, lambda_k2, subln_g,
           rwkv_mu, w_decay0, w_decay2, a0, a2, g2, k_k, k_a, r_k, ln_x_w, ln_x_b,
           w_branch_a, w_branch_b, w_out, norm2_g, w_router, router_bias,
           w_expert_up_gate, w_expert_down, w_shared_up_gate, w_shared_down):
    B, S, D = x.shape
    N = B * S
    E = w_router.shape[1]
    da_width = w_branch_a.shape[0]
    rw_width = w_branch_b.shape[0]
    lambda_init = 0.8 - 0.6 * math.exp(-0.3 * layer_idx)

    mod = _adaln(c, w_ada, b_ada)
    x2 = x.reshape(N, D)
    w_cat = jnp.concatenate([w_in, w_gate], axis=1).astype(BF16)
    qn, kn, v, r_, k_, v_, a_, ld_, g_, gate = _mixer_in(
        x2, positions.reshape(N, 1), mod, norm1_g, w_cat, b_gate, q_norm_g, k_norm_g,
        rwkv_mu, w_decay0, w_decay2, a0, a2, g2, S, da_width, rw_width)

    lam_vecs = jnp.stack([lambda_q1, lambda_k1, lambda_q2, lambda_k2])
    score_bound = 1.01 * DA_HEAD_DIM ** 0.5 * jnp.max(jnp.abs(q_norm_g)) * jnp.max(jnp.abs(k_norm_g))
    attn = _diff_attention(qn, kn, v, score_bound, lam_vecs, subln_g, B, S, lambda_init)

    seq = lambda t: t.reshape(B, S, rw_width)
    rw = _rwkv_scan(seq(r_), seq(k_), seq(v_), seq(a_), seq(ld_), seq(g_), k_k, k_a, r_k.reshape(-1),
                    ln_x_w, ln_x_b).reshape(N, rw_width)

    base, h2p, logits_t = _post(attn, rw, gate, x2, mod, w_branch_a.astype(BF16), w_branch_b.astype(BF16),
                                w_out.astype(BF16), norm2_g, w_router.T,
                                w_shared_up_gate.astype(BF16), w_shared_down.astype(BF16), S)

    eidx_t, w_t = _route(logits_t, router_bias)
    rank_t, counts = _ranks(eidx_t, E)
    R = EXPERT_TILE
    cnt = counts[:, 0].astype(I32)
    ustart = jnp.concatenate([jnp.zeros((1,), I32), jnp.cumsum(cnt)])
    pcnt = (cnt + R - 1) // R * R
    pstart = jnp.cumsum(pcnt) - pcnt
    dest_p = _dests(eidx_t, rank_t, pstart)
    n_rows = (N * TOP_K + E * (R - 1) + R - 1) // R * R
    xg = _sc_scatter_rows(h2p, dest_p, n_rows + X_AHEAD * R)
    y = _experts(ustart, pstart, xg, w_expert_up_gate, w_expert_down, n_rows, R)
    yg = _sc_gather_rows(y, dest_p)
    out = _combine(yg, w_t, base, mod, S)
    return out.reshape(B, S, D)


def kernel(x, c, positions, w_ada, b_ada, norm1_g, w_in, w_gate, b_gate, q_norm_g, k_norm_g, lambda_q1, lambda_k1, lambda_q2, lambda_k2, subln_g, rwkv_mu, w_decay0, w_decay2, a0, a2, g2, k_k, k_a, r_k, ln_x_w, ln_x_b, w_branch_a, w_branch_b, w_out, norm2_g, w_router, router_bias, w_expert_up_gate, w_expert_down, w_shared_up_gate, w_shared_down):
    for l in range(w_ada.shape[0]):
        x = _layer(x, c, positions, l, w_ada[l], b_ada[l], norm1_g[l], w_in[l], w_gate[l], b_gate[l],
                   q_norm_g[l], k_norm_g[l], lambda_q1[l], lambda_k1[l], lambda_q2[l], lambda_k2[l],
                   subln_g[l], rwkv_mu[l], w_decay0[l], w_decay2[l], a0[l], a2[l], g2[l], k_k[l],
                   k_a[l], r_k[l], ln_x_w[l], ln_x_b[l], w_branch_a[l], w_branch_b[l], w_out[l],
                   norm2_g[l], w_router[l], router_bias[l], w_expert_up_gate[l], w_expert_down[l],
                   w_shared_up_gate[l], w_shared_down[l])
    return x
```
